```python
import math
import jax
import jax.numpy as jnp
from jax import lax
import numpy as np

D_MODEL = 1024
BATCH = 8
SEQ = 2048
DEPTH = 4

GRID_W = 64
CTX_LEN = 256

A_HEADS = D_MODEL // 256
A_DH = 64
A_DV = 2 * A_DH
B_HEADS = D_MODEL // 128
B_DH = 64
NA_ROWS = 8
NA_COLS = 16
C_HEADS = 4
C_DK = D_MODEL // 2 // C_HEADS
C_DV = D_MODEL // C_HEADS
C_RANK = 16
C_TAU = 16.0
C_CHUNK = 64
N_EXPERTS = 16
N_GROUPS = 4
TOP_K = 2
D_EXPERT = D_MODEL // 2

ROPE_BASE = 10000.0
EPS = 1e-6
Q_BLOCK = 128

A_QK = A_HEADS * 2 * A_DH
A_V = A_HEADS * A_DV
B_W = B_HEADS * B_DH
ATT_IN = 2 * A_QK + A_V + 3 * B_W
ATT_OUT = A_V + B_W
GLA_QK = C_HEADS * C_DK
GLA_V = C_HEADS * C_DV
GLA_IN = 2 * GLA_QK + 2 * GLA_V + 2 * C_RANK
N_ATT = (DEPTH + 1) // 2
N_GLA = DEPTH // 2

kernel_name = 'hybrid_diffusion_trunk'


def rmsnorm(x, g):
    xf = x.astype(jnp.float32)
    y = xf * lax.rsqrt(jnp.mean(xf * xf, axis=-1, keepdims=True) + EPS)
    return (y * g.astype(jnp.float32)).astype(x.dtype)


def modulate(h, shift, scale):
    return h * (1.0 + scale) + shift


def heads(t, n_heads, d):
    b, n, _ = t.shape
    return t.reshape(b, n, n_heads, d).transpose(0, 2, 1, 3)


def rope_1d(x, pos):
    n = x.shape[-1]
    inv = ROPE_BASE ** (-jnp.arange(0, n, 2, dtype=jnp.float32) / n)
    ang = pos.astype(jnp.float32)[:, None] * inv[None, :]
    cos = jnp.cos(ang).astype(x.dtype)
    sin = jnp.sin(ang).astype(x.dtype)
    x1, x2 = jnp.split(x, 2, axis=-1)
    return jnp.concatenate([x1 * cos - x2 * sin, x1 * sin + x2 * cos], axis=-1)


def rope_2d(x, rows, cols):
    xr, xc = jnp.split(x, 2, axis=-1)
    return jnp.concatenate([rope_1d(xr, rows), rope_1d(xc, cols)], axis=-1)


def diff_core(q, k, v, lam):
    s = jnp.einsum('bhcqd,bhckd->bhcqk', q, k).astype(jnp.float32) * (q.shape[-1] ** -0.5)
    p = jax.nn.softmax(s, axis=-1)
    a = p[:, :, 0] - lam * p[:, :, 1]
    return jnp.einsum('bhqk,bhkv->bhqv', a.astype(v.dtype), v)


def softmax_attn(q, k, v):
    s = jnp.einsum('bhqd,bhkd->bhqk', q, k).astype(jnp.float32) * (q.shape[-1] ** -0.5)
    p = jax.nn.softmax(s, axis=-1)
    return jnp.einsum('bhqk,bhkd->bhqd', p.astype(v.dtype), v)


def neighbourhood_attn(q, k, v, kc, vc, bias_table):
    b, h, l, dh = q.shape
    rows = l // GRID_W
    wh = min(NA_ROWS, rows)
    qg = q.reshape(b, h, rows, GRID_W, dh)
    kg = k.reshape(b, h, rows, GRID_W, dh)
    vg = v.reshape(b, h, rows, GRID_W, dh)
    col = jnp.arange(GRID_W)
    col_start = jnp.clip(col - NA_COLS // 2, 0, GRID_W - NA_COLS)
    col_idx = col_start[:, None] + jnp.arange(NA_COLS)[None, :]
    dc = col_idx - col[:, None]
    scale = dh ** -0.5
    n_loc = wh * NA_COLS

    def row_block(r):
        r0 = jnp.clip(r - wh // 2, 0, rows - wh)
        q_r = lax.dynamic_index_in_dim(qg, r, axis=2, keepdims=False)
        k_r = lax.dynamic_slice_in_dim(kg, r0, wh, axis=2)
        v_r = lax.dynamic_slice_in_dim(vg, r0, wh, axis=2)
        k_w = k_r[:, :, :, col_idx]
        v_w = v_r[:, :, :, col_idx]
        dr = r0 + jnp.arange(wh) - r
        bias = bias_table[:, (dr + NA_ROWS - 1)[None, :, None], (dc + NA_COLS - 1)[:, None, :]]
        s_loc = jnp.einsum('bhqd,bhrqcd->bhqrc', q_r, k_w).astype(jnp.float32) * scale + bias[None].astype(jnp.float32)
        s_ctx = jnp.einsum('bhqd,bhkd->bhqk', q_r, kc).astype(jnp.float32) * scale
        s = jnp.concatenate([s_loc.reshape(b, h, GRID_W, n_loc), s_ctx], axis=-1)
        p = jax.nn.softmax(s, axis=-1).astype(v.dtype)
        p_loc = p[..., :n_loc].reshape(b, h, GRID_W, wh, NA_COLS)
        return (jnp.einsum('bhqrc,bhrqcd->bhqd', p_loc, v_w)
                + jnp.einsum('bhqk,bhkd->bhqd', p[..., n_loc:], vc))

    o = lax.map(row_block, jnp.arange(rows))
    return o.transpose(1, 2, 0, 3, 4).reshape(b, h, l, dh)


def attn_layer(hx, hc, w_in, w_out, lam_p, lam_init, subln_g, na_bias, need_ctx):
    b, l, _ = hx.shape
    pos = jnp.arange(l)
    rows_pos, cols_pos = pos // GRID_W, pos % GRID_W
    cuts = [A_QK, 2 * A_QK, 2 * A_QK + A_V, 2 * A_QK + A_V + B_W, 2 * A_QK + A_V + 2 * B_W]

    def project(h):
        n = h.shape[1]
        qa, ka, va, qb, kb, vb = jnp.split(h @ w_in, cuts, axis=-1)
        qa = qa.reshape(b, n, A_HEADS, 2, A_DH).transpose(0, 2, 3, 1, 4)
        ka = ka.reshape(b, n, A_HEADS, 2, A_DH).transpose(0, 2, 3, 1, 4)
        return (qa, ka, heads(va, A_HEADS, A_DV), heads(qb, B_HEADS, B_DH),
                heads(kb, B_HEADS, B_DH), heads(vb, B_HEADS, B_DH))

    qa_x, ka_x, va_x, qb_x, kb_x, vb_x = project(hx)
    qa_c, ka_c, va_c, qb_c, kb_c, vb_c = project(hc)
    lp = lam_p.astype(jnp.float32)
    lam = jnp.exp(jnp.sum(lp[0] * lp[1])) - jnp.exp(jnp.sum(lp[2] * lp[3])) + lam_init

    qa_x = rope_2d(qa_x, rows_pos, cols_pos)
    ka_x = rope_2d(ka_x, rows_pos, cols_pos)
    k_all = jnp.concatenate([ka_x, ka_c], axis=3)
    v_all = jnp.concatenate([va_x, va_c], axis=2)
    nb = l // Q_BLOCK
    q_blocks = qa_x.reshape(b, A_HEADS, 2, nb, Q_BLOCK, A_DH).transpose(3, 0, 1, 2, 4, 5)
    oa_x = lax.map(lambda qq: diff_core(qq, k_all, v_all, lam), q_blocks)
    oa_x = oa_x.transpose(1, 2, 0, 3, 4).reshape(b, A_HEADS, l, A_DV)
    ob_x = neighbourhood_attn(qb_x, kb_x, vb_x, kb_c, vb_c, na_bias)

    def merge(oa, ob):
        n = oa.shape[2]
        oa = rmsnorm(oa, subln_g) * (1.0 - lam_init)
        oa = oa.transpose(0, 2, 1, 3).reshape(b, n, A_V)
        ob = ob.transpose(0, 2, 1, 3).reshape(b, n, B_W)
        return jnp.concatenate([oa, ob], axis=-1) @ w_out

    out_x = merge(oa_x, ob_x)
    if not need_ctx:
        return out_x, None
    out_c = merge(diff_core(qa_c, ka_c, va_c, lam), softmax_attn(qb_c, kb_c, vb_c))
    return out_x, out_c


def gla_chunked(q, k, v, g, s0):
    b, h, l, dk = q.shape
    dv = v.shape[-1]
    nc = l // C_CHUNK
    f32 = jnp.float32
    qc = q.astype(f32).reshape(b, h, nc, C_CHUNK, dk)
    kc = k.astype(f32).reshape(b, h, nc, C_CHUNK, dk)
    vc = v.astype(f32).reshape(b, h, nc, C_CHUNK, dv)
    gc = jnp.cumsum(g.astype(f32).reshape(b, h, nc, C_CHUNK, dk), axis=3)
    g_end = gc[:, :, :, -1:]
    g_mid = gc[:, :, :, C_CHUNK // 2 - 1:C_CHUNK // 2]
    k_to_end = kc * jnp.exp(g_end - gc)
    chunk_decay = jnp.exp(g_end[:, :, :, 0])

    def step(s, inp):
        k_e, v_c, dec = inp
        return s * dec[..., None] + jnp.einsum('bhcd,bhcv->bhdv', k_e, v_c), s

    xs = (jnp.moveaxis(k_to_end, 2, 0), jnp.moveaxis(vc, 2, 0), jnp.moveaxis(chunk_decay, 2, 0))
    s_final, s_start = lax.scan(step, s0.astype(f32), xs)
    o_inter = jnp.einsum('bhncd,nbhdv->bhncv', qc * jnp.exp(gc), s_start)
    a = jnp.einsum('bhncd,bhnsd->bhncs', qc * jnp.exp(gc - g_mid), kc * jnp.exp(g_mid - gc))
    upto_t = jnp.tril(jnp.ones((C_CHUNK, C_CHUNK), dtype=bool))
    a = jnp.where(upto_t, a, 0.0)
    o = o_inter + jnp.einsum('bhncs,bhnsv->bhncv', a, vc)
    return o.reshape(b, h, l, dv).astype(v.dtype), s_final


def gla_final_state(k, v, g):
    gc = jnp.cumsum(g.astype(jnp.float32), axis=2)
    k_to_end = k.astype(jnp.float32) * jnp.exp(gc[:, :, -1:] - gc)
    return jnp.einsum('bhld,bhlv->bhdv', k_to_end, v.astype(jnp.float32))


def gla_layer(hx, hc, w_in, w_gate, b_gate, norm_g, w_out, need_ctx):
    b = hx.shape[0]
    cuts = [GLA_QK, 2 * GLA_QK, 2 * GLA_QK + GLA_V, 2 * GLA_QK + 2 * GLA_V, 2 * GLA_QK + 2 * GLA_V + C_RANK]

    def project(h):
        q, k, v, gate, lf, lb = jnp.split(h @ w_in, cuts, axis=-1)

        def log_decay(lr, d):
            z = (lr @ w_gate[d] + b_gate[d]).astype(jnp.float32)
            return heads(jax.nn.log_sigmoid(z) / C_TAU, C_HEADS, C_DK)

        return (heads(q, C_HEADS, C_DK) * (C_DK ** -0.5), heads(k, C_HEADS, C_DK),
                heads(v, C_HEADS, C_DV), gate, log_decay(lf, 0), log_decay(lb, 1))

    qx, kx, vx, gate_x, gfx, gbx = project(hx)
    qc, kc, vc, gate_c, gfc, gbc = project(hc)
    flip = lambda t: jnp.flip(t, axis=2)
    zero = jnp.zeros((b, C_HEADS, C_DK, C_DV), jnp.float32)
    if need_ctx:
        oc_f, sc_f = gla_chunked(qc, kc, vc, gfc, zero)
        oc_b, sc_b = gla_chunked(flip(qc), flip(kc), flip(vc), flip(gbc), zero)
    else:
        sc_f = gla_final_state(kc, vc, gfc)
        sc_b = gla_final_state(flip(kc), flip(vc), flip(gbc))
    ox_f, _ = gla_chunked(qx, kx, vx, gfx, sc_f)
    ox_b, _ = gla_chunked(flip(qx), flip(kx), flip(vx), flip(gbx), sc_b)

    def readout(o, gate):
        n = o.shape[2]
        o = rmsnorm(o, norm_g).transpose(0, 2, 1, 3).reshape(b, n, GLA_V)
        return (o * jax.nn.silu(gate)) @ w_out

    out_x = readout(ox_f + flip(ox_b), gate_x)
    if not need_ctx:
        return out_x, None
    return out_x, readout(oc_f + flip(oc_b), gate_c)


def moe(h, router_w, router_bias, w1, w3, w2):
    t = h.shape[0]
    per_group = N_EXPERTS // N_GROUPS
    scores = jax.nn.sigmoid((h @ router_w).astype(jnp.float32))
    sel = scores + router_bias.astype(jnp.float32)
    grp_score = lax.top_k(sel.reshape(t, N_GROUPS, per_group), 2)[0].sum(-1)
    grp = jnp.argmax(grp_score, axis=-1)
    in_grp = (jnp.arange(N_EXPERTS) // per_group)[None, :] == grp[:, None]
    _, idx = lax.top_k(jnp.where(in_grp, sel, -jnp.inf), TOP_K)
    w_sel = jnp.take_along_axis(scores, idx, axis=-1)
    w_sel = w_sel / jnp.sum(w_sel, axis=-1, keepdims=True)
    gates = jnp.sum(jax.nn.one_hot(idx, N_EXPERTS, dtype=jnp.float32) * w_sel[..., None], axis=1)
    y = jnp.zeros_like(h)
    for e in range(N_EXPERTS):
        he = jax.nn.silu(h @ w1[e]) * (h @ w3[e])
        y = y + gates[:, e:e + 1].astype(h.dtype) * (he @ w2[e])
    return y


def setup_inputs(seed: int = 0) -> dict:
    key = jax.random.key(seed)
    ks = jax.random.split(key, 24)
    nrm = lambda k, shape, s: jax.random.normal(k, shape, jnp.float32) * s
    d = D_MODEL
    return {
        'x': nrm(ks[0], (BATCH, SEQ, d), 1.0),
        'c': nrm(ks[1], (BATCH, d), 1.0),
        'ctx': nrm(ks[2], (BATCH, CTX_LEN, d), 1.0),
        'c_ctx': nrm(ks[3], (d,), 1.0),
        'w_mod': nrm(ks[4], (DEPTH, d, 6 * d), 0.5 * d ** -0.5),
        'b_mod': nrm(ks[5], (DEPTH, 6 * d), 0.02),
        'norm_g': 1.0 + nrm(ks[6], (DEPTH, 2, d), 0.02),
        'final_g': 1.0 + nrm(ks[7], (d,), 0.02),
        'att_w_in': nrm(ks[8], (N_ATT, d, ATT_IN), d ** -0.5),
        'att_w_out': nrm(ks[9], (N_ATT, ATT_OUT, d), ATT_OUT ** -0.5),
        'att_lambda': nrm(ks[10], (N_ATT, 4, A_DH), 0.1),
        'att_subln_g': 1.0 + nrm(ks[11], (N_ATT, A_DV), 0.02),
        'na_bias': nrm(ks[12], (N_ATT, B_HEADS, 2 * NA_ROWS - 1, 2 * NA_COLS - 1), 0.1),
        'gla_w_in': nrm(ks[13], (N_GLA, d, GLA_IN), d ** -0.5),
        'gla_w_gate': nrm(ks[14], (N_GLA, 2, C_RANK, GLA_QK), C_RANK ** -0.5),
        'gla_b_gate': nrm(ks[15], (N_GLA, 2, GLA_QK), 0.1),
        'gla_norm_g': 1.0 + nrm(ks[16], (N_GLA, C_DV), 0.02),
        'gla_w_out': nrm(ks[17], (N_GLA, GLA_V, d), GLA_V ** -0.5),
        'router_w': nrm(ks[18], (d, N_EXPERTS), d ** -0.5),
        'router_bias': nrm(ks[19], (N_EXPERTS,), 0.01),
        'moe_w1': nrm(ks[20], (DEPTH, N_EXPERTS, d, D_EXPERT), d ** -0.5),
        'moe_w3': nrm(ks[21], (DEPTH, N_EXPERTS, d, D_EXPERT), d ** -0.5),
        'moe_w2': nrm(ks[22], (DEPTH, N_EXPERTS, D_EXPERT, d), D_EXPERT ** -0.5),
    }


def reference(x, c, ctx, c_ctx, w_mod, b_mod, norm_g, final_g, att_w_in, att_w_out, att_lambda,
              att_subln_g, na_bias, gla_w_in, gla_w_gate, gla_b_gate, gla_norm_g, gla_w_out,
              router_w, router_bias, moe_w1, moe_w3, moe_w2):
    b, l, d = x.shape
    n_ctx = ctx.shape[1]
    for i in range(DEPTH):
        last = i == DEPTH - 1
        j = i // 2
        mx = jax.nn.silu(c) @ w_mod[i] + b_mod[i]
        mc = jax.nn.silu(c_ctx) @ w_mod[i] + b_mod[i]
        sh1x, sc1x, g1x, sh2x, sc2x, g2x = jnp.split(mx[:, None, :], 6, axis=-1)
        sh1c, sc1c, g1c, sh2c, sc2c, g2c = jnp.split(mc[None, None, :], 6, axis=-1)
        hx = modulate(rmsnorm(x, norm_g[i, 0]), sh1x, sc1x)
        hc = modulate(rmsnorm(ctx, norm_g[i, 0]), sh1c, sc1c)
        if i % 2 == 0:
            lam_init = 0.8 - 0.6 * math.exp(-0.3 * i)
            ox, oc = attn_layer(hx, hc, att_w_in[j], att_w_out[j], att_lambda[j], lam_init,
                                att_subln_g[j], na_bias[j], not last)
        else:
            ox, oc = gla_layer(hx, hc, gla_w_in[j], gla_w_gate[j], gla_b_gate[j], gla_norm_g[j],
                               gla_w_out[j], not last)
        x = x + g1x * ox
        hx2 = modulate(rmsnorm(x, norm_g[i, 1]), sh2x, sc2x)
        if last:
            x = x + g2x * moe(hx2.reshape(-1, d), router_w, router_bias, moe_w1[i], moe_w3[i],
                              moe_w2[i]).reshape(b, l, d)
        else:
            ctx = ctx + g1c * oc
            hc2 = modulate(rmsnorm(ctx, norm_g[i, 1]), sh2c, sc2c)
            y = moe(jnp.concatenate([hc2, hx2], axis=1).reshape(-1, d), router_w, router_bias,
                    moe_w1[i], moe_w3[i], moe_w2[i]).reshape(b, n_ctx + l, d)
            ctx = ctx + g2c * y[:, :n_ctx]
            x = x + g2x * y[:, n_ctx:]
    return rmsnorm(x, final_g)
```

```python
import functools
import math

import jax
import jax.numpy as jnp
import numpy as np
from jax import lax
from jax.experimental import pallas as pl
from jax.experimental.pallas import tpu as pltpu

F32 = jnp.float32
BF16 = jnp.bfloat16

D_MODEL = 1024
DEPTH = 4
GRID_W = 64
N_CTX = 256
A_HEADS = 4
A_DH = 64
A_DV = 128
B_HEADS = 8
B_DH = 64
NA_ROWS = 8
NA_COLS = 16
C_HEADS = 4
C_DK = 128
C_DV = 256
C_RANK = 16
C_TAU = 16.0
C_CHUNK = 64
N_EXPERTS = 16
N_GROUPS = 4
PER_GROUP = 4
D_EXPERT = 512
ROPE_BASE = 10000.0
EPS = 1e-6

LANES = 128
ROW_TILE = 256
MOE_TILE = 256
N_CLASSES = N_GROUPS * 6
NA_QROWS = 4
NA_KROWS = NA_QROWS + NA_ROWS - 1
VMEM_LIMIT = 52 * 1024 * 1024
GATHER_WINDOW = 256

PAIR_LO = (0, 0, 0, 1, 1, 2)
PAIR_HI = (1, 2, 3, 2, 3, 3)


def _cparams(sem):
    return pltpu.CompilerParams(dimension_semantics=sem, vmem_limit_bytes=VMEM_LIMIT)


def _sigmoid(x):
    return 1.0 / (1.0 + jnp.exp(-x))


def _silu(x):
    return x * _sigmoid(x)


def _rms(x, g):
    return x * lax.rsqrt(jnp.mean(x * x, axis=-1, keepdims=True) + EPS) * g


def _dot(a, b):
    return jnp.dot(a, b, preferred_element_type=F32)


def _dot_nt(a, b):
    return lax.dot_general(a, b, (((1,), (1,)), ((), ())), preferred_element_type=F32)


def _dot_tn(a, b):
    return lax.dot_general(a, b, (((0,), (0,)), ((), ())), preferred_element_type=F32)


def _mod_kernel(c_ref, w_ref, b_ref, o_ref):
    a = _silu(c_ref[...]).astype(BF16)
    o_ref[0] = _dot(a, w_ref[0].astype(BF16)) + b_ref[0]


def _mod_vectors(cc, w_mod, b_mod):
    depth, d, n6 = w_mod.shape
    rows = cc.shape[0]
    tn = 1536
    return pl.pallas_call(
        _mod_kernel,
        out_shape=jax.ShapeDtypeStruct((depth, rows, n6), F32),
        grid=(depth, n6 // tn),
        in_specs=[
            pl.BlockSpec((rows, d), lambda i, j: (0, 0)),
            pl.BlockSpec((1, d, tn), lambda i, j: (i, 0, j)),
            pl.BlockSpec((1, 1, tn), lambda i, j: (i, 0, j)),
        ],
        out_specs=pl.BlockSpec((1, rows, tn), lambda i, j: (i, 0, j)),
        compiler_params=_cparams(("arbitrary", "arbitrary")),
        name="mod_vectors",
    )(cc, w_mod, b_mod.reshape(depth, 1, n6))


def _proj_kernel(*refs, has_y, n_rope, n_bf16, col_chunk):
    it = iter(refs)
    s_ref = next(it)
    y_ref = next(it) if has_y else None
    modp_ref = next(it) if has_y else None
    mod_ref = next(it)
    ng_ref = next(it)
    w_ref = next(it)
    cos_ref = next(it) if n_rope else None
    sin_ref = next(it) if n_rope else None
    snew_ref = next(it) if has_y else None
    p_ref = next(it)
    lr_ref = next(it) if w_ref.shape[1] > n_bf16 else None

    x = s_ref[0]
    if has_y:
        x = x + modp_ref[0, 0, 5:6, :] * y_ref[0]
        snew_ref[0] = x
    h = _rms(x, ng_ref[...])
    h = h * (1.0 + mod_ref[0, 0, 1:2, :]) + mod_ref[0, 0, 0:1, :]
    hb = h.astype(BF16)
    n_out = w_ref.shape[1]
    for c0 in range(0, n_out, col_chunk):
        c1 = min(c0 + col_chunk, n_out)
        acc = _dot(hb, w_ref[:, c0:c1])
        for b0 in range(c0, c1, LANES):
            t = acc[:, b0 - c0:b0 - c0 + LANES]
            if b0 < n_rope:
                t = t * cos_ref[...] + pltpu.roll(t, LANES // 2, 1) * sin_ref[...]
            if b0 < n_bf16:
                p_ref[0, :, b0:b0 + LANES] = t.astype(BF16)
            else:
                lr_ref[0, :, b0 - n_bf16:b0 - n_bf16 + LANES] = t


def _project(s, y, modp, mod, ng, w, cos, sin, *, n_rope, n_bf16):
    b, n, d = s.shape
    n_out = w.shape[1]
    tm = ROW_TILE
    has_y = y is not None
    row_spec = pl.BlockSpec((1, tm, d), lambda i, j: (i, j, 0))
    mod_spec = pl.BlockSpec((1, 1, 6, d), lambda i, j: (i, jnp.minimum(j, 1), 0, 0))
    in_specs = [row_spec]
    args = [s]
    if has_y:
        in_specs += [row_spec, mod_spec]
        args += [y, modp]
    in_specs += [mod_spec, pl.BlockSpec((1, d), lambda i, j: (0, 0)),
                 pl.BlockSpec((d, n_out), lambda i, j: (0, 0))]
    args += [mod, ng.reshape(1, d), w]
    if n_rope:
        tab_spec = pl.BlockSpec((tm, LANES), lambda i, j: (j, 0))
        in_specs += [tab_spec, tab_spec]
        args += [cos, sin]
    out_shape, out_specs = [], []
    if has_y:
        out_shape.append(jax.ShapeDtypeStruct((b, n, d), F32))
        out_specs.append(row_spec)
    out_shape.append(jax.ShapeDtypeStruct((b, n, n_bf16), BF16))
    out_specs.append(pl.BlockSpec((1, tm, n_bf16), lambda i, j: (i, j, 0)))
    if n_out > n_bf16:
        out_shape.append(jax.ShapeDtypeStruct((b, n, n_out - n_bf16), F32))
        out_specs.append(pl.BlockSpec((1, tm, n_out - n_bf16), lambda i, j: (i, j, 0)))
    return pl.pallas_call(
        functools.partial(_proj_kernel, has_y=has_y, n_rope=n_rope, n_bf16=n_bf16, col_chunk=512),
        out_shape=out_shape,
        grid=(b, n // tm),
        in_specs=in_specs,
        out_specs=out_specs,
        compiler_params=_cparams(("arbitrary", "arbitrary")),
        name="norm_mod_project",
    )(*args)


def _softmax_rows(s):
    e = jnp.exp(s - jnp.max(s, axis=-1, keepdims=True))
    return e * (1.0 / jnp.sum(e, axis=-1, keepdims=True))


def _diff_attn_kernel(q_ref, k_ref, v_ref, lam_ref, g_ref, o_ref, *, lam_init):
    lp = lam_ref[...]
    lam = (jnp.exp(jnp.sum(lp[0:1] * lp[1:2], axis=-1, keepdims=True))
           - jnp.exp(jnp.sum(lp[2:3] * lp[3:4], axis=-1, keepdims=True)) + lam_init)
    q = q_ref[0]
    lane = lax.broadcasted_iota(jnp.int32, (1, LANES), 1)
    first_map = (lane // 32) % 2 == 0
    zero = jnp.zeros_like(q)
    q0 = jnp.where(first_map, q, zero)
    q1 = jnp.where(first_map, zero, q)

    def attend(n_keys):
        k = k_ref[0, 0:n_keys, :]
        v = v_ref[0, 0:n_keys, :]
        a = _softmax_rows(_dot_nt(q0, k)) - lam * _softmax_rows(_dot_nt(q1, k))
        o = _dot(a.astype(BF16), v)
        o_ref[0] = (_rms(o, g_ref[...]) * (1.0 - lam_init)).astype(BF16)

    qi = pl.program_id(2)

    @pl.when(qi == 0)
    def _():
        attend(N_CTX)

    @pl.when(qi > 0)
    def _():
        attend(k_ref.shape[1])


def _diff_attention(p, lam_p, subln_g, lam_init):
    b, n, _ = p.shape
    tq = ROW_TILE
    return pl.pallas_call(
        functools.partial(_diff_attn_kernel, lam_init=lam_init),
        out_shape=jax.ShapeDtypeStruct((b, n, A_HEADS * A_DV), BF16),
        grid=(b, A_HEADS, n // tq),
        in_specs=[
            pl.BlockSpec((1, tq, LANES), lambda i, h, j: (i, j, h)),
            pl.BlockSpec((1, n, LANES), lambda i, h, j: (i, 0, A_HEADS + h)),
            pl.BlockSpec((1, n, LANES), lambda i, h, j: (i, 0, 2 * A_HEADS + h)),
            pl.BlockSpec((4, A_DH), lambda i, h, j: (0, 0)),
            pl.BlockSpec((1, A_DV), lambda i, h, j: (0, 0)),
        ],
        out_specs=pl.BlockSpec((1, tq, LANES), lambda i, h, j: (i, j, h)),
        compiler_params=_cparams(("arbitrary", "arbitrary", "arbitrary")),
        name="diff_attention",
    )(p, p, p, lam_p, subln_g.reshape(1, A_DV))


def _na_kernel(q_ref, k_ref, v_ref, bm_ref, o_ref):
    blk = pl.program_id(2)
    q = q_ref[0]
    lane = lax.broadcasted_iota(jnp.int32, (1, LANES), 1)
    first_head = lane < B_DH
    zero = jnp.zeros_like(q)
    qh = (jnp.where(first_head, q, zero), jnp.where(first_head, zero, q))
    kc = k_ref[0, 0:N_CTX, :]
    vc = v_ref[0, 0:N_CTX, :]

    @pl.when(blk == 0)
    def _():
        outs = [_dot(_softmax_rows(_dot_nt(qh[h], kc)).astype(BF16), vc) for h in range(2)]
        o_ref[0] = jnp.where(first_head, outs[0], outs[1]).astype(BF16)

    @pl.when(blk > 0)
    def _():
        rows = (k_ref.shape[1] - N_CTX) // GRID_W
        k_row0 = jnp.clip((blk - 1) * NA_QROWS - NA_ROWS // 2, 0, rows - NA_KROWS)
        start = pl.multiple_of(N_CTX + k_row0 * GRID_W, GRID_W)
        kw = k_ref[0, pl.ds(start, NA_KROWS * GRID_W), :]
        vw = v_ref[0, pl.ds(start, NA_KROWS * GRID_W), :]
        outs = []
        for h in range(2):
            s_loc = _dot_nt(qh[h], kw) + bm_ref[h, 0]
            s_ctx = _dot_nt(qh[h], kc)
            m = jnp.maximum(jnp.max(s_loc, axis=-1, keepdims=True), jnp.max(s_ctx, axis=-1, keepdims=True))
            e_loc = jnp.exp(s_loc - m)
            e_ctx = jnp.exp(s_ctx - m)
            den = jnp.sum(e_loc, axis=-1, keepdims=True) + jnp.sum(e_ctx, axis=-1, keepdims=True)
            outs.append((_dot(e_loc.astype(BF16), vw) + _dot(e_ctx.astype(BF16), vc)) * (1.0 / den))
        o_ref[0] = jnp.where(first_head, outs[0], outs[1]).astype(BF16)


def _na_bias_table(na_bias, rows):
    col = np.arange(GRID_W)
    col_start = np.clip(col - NA_COLS // 2, 0, GRID_W - NA_COLS)
    dc = col[None, :] - col[:, None]
    col_ok = (col[None, :] >= col_start[:, None]) & (col[None, :] < col_start[:, None] + NA_COLS)
    dc_idx = np.where(col_ok, dc + NA_COLS - 1, 0)
    pats = []
    for q_row0 in (0, NA_QROWS, 2 * NA_QROWS, rows - NA_QROWS):
        k_row0 = int(np.clip(q_row0 - NA_ROWS // 2, 0, rows - NA_KROWS))
        r = q_row0 + np.arange(NA_QROWS)
        kr = k_row0 + np.arange(NA_KROWS)
        r0 = np.clip(r - NA_ROWS // 2, 0, rows - NA_ROWS)
        row_ok = (kr[None, :] >= r0[:, None]) & (kr[None, :] < r0[:, None] + NA_ROWS)
        dr_idx = np.where(row_ok, kr[None, :] - r[:, None] + NA_ROWS - 1, 0)
        ok = row_ok[:, None, :, None] & col_ok[None, :, None, :]
        i_r = np.broadcast_to(dr_idx[:, None, :, None], ok.shape)
        i_c = np.broadcast_to(dc_idx[None, :, None, :], ok.shape)
        t = na_bias[:, i_r, i_c]
        t = jnp.where(ok[None], t, -jnp.inf)
        pats.append(t.reshape(B_HEADS, NA_QROWS * GRID_W, NA_KROWS * GRID_W))
    return jnp.stack(pats, axis=1)


def _neighbourhood_attention(p, bm):
    b, n, _ = p.shape
    tq = NA_QROWS * GRID_W
    col0 = 3 * A_HEADS
    pairs = B_HEADS // 2

    def pattern(j):
        return jnp.clip(j - 1, 0, 2) + (j == n // tq - 1).astype(jnp.int32)

    return pl.pallas_call(
        _na_kernel,
        out_shape=jax.ShapeDtypeStruct((b, n, B_HEADS * B_DH), BF16),
        grid=(b, pairs, n // tq),
        in_specs=[
            pl.BlockSpec((1, tq, LANES), lambda i, h, j: (i, j, col0 + h)),
            pl.BlockSpec((1, n, LANES), lambda i, h, j: (i, 0, col0 + pairs + h)),
            pl.BlockSpec((1, n, LANES), lambda i, h, j: (i, 0, col0 + 2 * pairs + h)),
            pl.BlockSpec((2, 1, tq, NA_KROWS * GRID_W), lambda i, h, j: (h, pattern(j), 0, 0)),
        ],
        out_specs=pl.BlockSpec((1, tq, LANES), lambda i, h, j: (i, j, h)),
        compiler_params=_cparams(("arbitrary", "arbitrary", "arbitrary")),
        name="neighbourhood_attention",
    )(p, p, p, bm)


def _gla_kernel(q_ref, k_ref, v_ref, lr_ref, wg_ref, bg_ref, o_ref, g_ref, st_ref):
    n = q_ref.shape[1]
    n_chunks = n // C_CHUNK
    ctx_chunks = N_CTX // C_CHUNK
    ri = lax.broadcasted_iota(jnp.int32, (C_CHUNK, C_CHUNK), 0)
    ci = lax.broadcasted_iota(jnp.int32, (C_CHUNK, C_CHUNK), 1)
    lr = lr_ref[0].astype(BF16)

    def run(direction):
        z = _dot(lr, wg_ref[direction].astype(BF16)) + bg_ref[direction:direction + 1, :]
        g_ref[...] = (jnp.minimum(z, 0.0) - jnp.log(1.0 + jnp.exp(-jnp.abs(z)))) * (1.0 / C_TAU)
        st_ref[...] = jnp.zeros_like(st_ref)
        keep = (ci <= ri) if direction == 0 else (ci >= ri)
        tri = keep.astype(F32)
        end_row = C_CHUNK - 1 if direction == 0 else 0
        mid_row = C_CHUNK // 2 - 1 if direction == 0 else C_CHUNK // 2

        def chunk(c):
            rows = pl.ds(pl.multiple_of(c * C_CHUNK, C_CHUNK), C_CHUNK)
            gc = jnp.dot(tri, g_ref[rows, :], preferred_element_type=F32, precision=lax.Precision.HIGHEST)
            g_end = gc[end_row:end_row + 1, :]
            g_mid = gc[mid_row:mid_row + 1, :]
            q = q_ref[0, rows, :].astype(F32) * (C_DK ** -0.5)
            k = k_ref[0, rows, :].astype(F32)
            v = v_ref[0, rows, :]
            q_in = (q * jnp.exp(gc)).astype(BF16)
            q_mid = (q * jnp.exp(gc - g_mid)).astype(BF16)
            k_mid = (k * jnp.exp(g_mid - gc)).astype(BF16)
            k_end = (k * jnp.exp(g_end - gc)).astype(BF16)
            a = jnp.where(keep, _dot_nt(q_mid, k_mid), 0.0)
            st = st_ref[...]
            o = _dot_nt(q_in, st.astype(BF16)) + _dot(a.astype(BF16), v)
            if direction == 0:
                o_ref[0, rows, :] = o
            else:
                o_ref[0, rows, :] = o_ref[0, rows, :] + o
            st_ref[...] = st * jnp.exp(g_end) + _dot_tn(v, k_end)

        if direction == 0:
            def body(c, carry):
                chunk(c)
                return carry
            lax.fori_loop(0, n_chunks, body, 0)
        else:
            def body_ctx(i, carry):
                chunk(ctx_chunks - 1 - i)
                return carry
            lax.fori_loop(0, ctx_chunks, body_ctx, 0)

            def body_x(i, carry):
                chunk(n_chunks - 1 - i)
                return carry
            lax.fori_loop(0, n_chunks - ctx_chunks, body_x, 0)

    run(0)
    run(1)


def _gla(p, lr, wg, bg):
    b, n, _ = p.shape
    return pl.pallas_call(
        _gla_kernel,
        out_shape=jax.ShapeDtypeStruct((b, n, C_HEADS * C_DV), F32),
        grid=(b, C_HEADS),
        in_specs=[
            pl.BlockSpec((1, n, C_DK), lambda i, h: (i, 0, h)),
            pl.BlockSpec((1, n, C_DK), lambda i, h: (i, 0, C_HEADS + h)),
            pl.BlockSpec((1, n, C_DV), lambda i, h: (i, 0, C_HEADS + h)),
            pl.BlockSpec((1, n, LANES), lambda i, h: (i, 0, 0)),
            pl.BlockSpec((2, LANES, C_DK), lambda i, h: (0, 0, h)),
            pl.BlockSpec((2, C_DK), lambda i, h: (0, h)),
        ],
        out_specs=pl.BlockSpec((1, n, C_DV), lambda i, h: (i, 0, h)),
        scratch_shapes=[pltpu.VMEM((n, C_DK), F32), pltpu.VMEM((C_DV, C_DK), F32)],
        compiler_params=_cparams(("arbitrary", "arbitrary")),
        name="gla",
    )(p, p, p, lr, wg, bg)


def _out_kernel(*refs, gla):
    if gla:
        o_ref, gate_ref, gn_ref, w_ref, s_ref, mod_ref, ng_ref, rw1_ref, rw2_ref, snew_ref, h2_ref, lg_ref = refs
        o = o_ref[0]
        gate = gate_ref[0].astype(F32)
        parts = []
        for hd in range(C_HEADS):
            oh = _rms(o[:, hd * C_DV:(hd + 1) * C_DV], gn_ref[...])
            parts.append((oh * _silu(gate[:, hd * C_DV:(hd + 1) * C_DV])).astype(BF16))
        acc = _dot(jnp.concatenate(parts, axis=-1), w_ref[...])
    else:
        oa_ref, ob_ref, w_ref, s_ref, mod_ref, ng_ref, rw1_ref, rw2_ref, snew_ref, h2_ref, lg_ref = refs
        half = oa_ref.shape[2]
        acc = _dot(oa_ref[0], w_ref[0:half, :]) + _dot(ob_ref[0], w_ref[half:, :])
    x = s_ref[0] + mod_ref[0, 0, 2:3, :] * acc
    snew_ref[0] = x
    h2 = _rms(x, ng_ref[...]) * (1.0 + mod_ref[0, 0, 4:5, :]) + mod_ref[0, 0, 3:4, :]
    h2_ref[0] = h2
    hi = h2.astype(BF16)
    lo = (h2 - hi.astype(F32)).astype(BF16)
    t = _dot(hi, rw1_ref[...])
    lg_ref[0] = t + pltpu.roll(t, LANES - N_EXPERTS, 1) + _dot(lo, rw2_ref[...])


def _out_project(mix, w_out, s, mod, ng, rw1, rw2, *, gla, gn=None):
    b, n, d = s.shape
    tm = ROW_TILE
    row_spec = pl.BlockSpec((1, tm, d), lambda i, j: (i, j, 0))
    const2 = lambda i, j: (0, 0)
    if gla:
        o, p = mix
        in_specs = [row_spec, pl.BlockSpec((1, tm, d), lambda i, j: (i, j, 2)),
                    pl.BlockSpec((1, C_DV), const2)]
        args = [o, p, gn.reshape(1, C_DV)]
    else:
        oa, ob = mix
        half_spec = pl.BlockSpec((1, tm, oa.shape[2]), lambda i, j: (i, j, 0))
        in_specs = [half_spec, half_spec]
        args = [oa, ob]
    in_specs += [pl.BlockSpec((d, d), const2), row_spec,
                 pl.BlockSpec((1, 1, 6, d), lambda i, j: (i, jnp.minimum(j, 1), 0, 0)),
                 pl.BlockSpec((1, d), const2), pl.BlockSpec((d, LANES), const2), pl.BlockSpec((d, LANES), const2)]
    args += [w_out, s, mod, ng.reshape(1, d), rw1, rw2]
    return pl.pallas_call(
        functools.partial(_out_kernel, gla=gla),
        out_shape=[jax.ShapeDtypeStruct((b, n, d), F32), jax.ShapeDtypeStruct((b, n, d), F32),
                   jax.ShapeDtypeStruct((b, n, LANES), F32)],
        grid=(b, n // tm),
        in_specs=in_specs,
        out_specs=[row_spec, row_spec, pl.BlockSpec((1, tm, LANES), lambda i, j: (i, j, 0))],
        compiler_params=_cparams(("arbitrary", "arbitrary")),
        name="out_project",
    )(*args)


def _route_kernel(bias_ref, lg_ref, cls_ref, wlo_ref, whi_ref):
    score = [_sigmoid(lg_ref[e]) for e in range(N_EXPERTS)]
    sel = [score[e] + bias_ref[e] for e in range(N_EXPERTS)]
    grp_score = []
    for g in range(N_GROUPS):
        v = sel[g * PER_GROUP:(g + 1) * PER_GROUP]
        best = v[0] + v[1]
        for a in range(PER_GROUP):
            for c in range(a + 1, PER_GROUP):
                if (a, c) != (0, 1):
                    best = jnp.maximum(best, v[a] + v[c])
        grp_score.append(best)
    grp = jnp.zeros(grp_score[0].shape, jnp.int32)
    best = grp_score[0]
    for g in range(1, N_GROUPS):
        upd = grp_score[g] > best
        best = jnp.where(upd, grp_score[g], best)
        grp = jnp.where(upd, g, grp)

    def pick(vals, j):
        out = vals[j]
        for g in range(1, N_GROUPS):
            out = jnp.where(grp == g, vals[g * PER_GROUP + j], out)
        return out

    v = [pick(sel, j) for j in range(PER_GROUP)]
    sc = [pick(score, j) for j in range(PER_GROUP)]
    one = jnp.ones(grp.shape, jnp.int32)
    zero = jnp.zeros(grp.shape, jnp.int32)
    chosen = []
    for j in range(PER_GROUP):
        rank = zero
        for m in range(PER_GROUP):
            if m == j:
                continue
            ahead = (v[m] >= v[j]) if m < j else (v[m] > v[j])
            rank = rank + jnp.where(ahead, one, zero)
        chosen.append(rank < 2)
    code = zero
    for j in range(PER_GROUP):
        code = code + jnp.where(chosen[j], one * (1 << j), zero)
    pair = zero
    for idx in range(6):
        pair = jnp.where(code == (1 << PAIR_LO[idx]) + (1 << PAIR_HI[idx]), idx, pair)
    s_lo = jnp.where(chosen[0], sc[0], jnp.where(chosen[1], sc[1], sc[2]))
    s_hi = jnp.where(chosen[3], sc[3], jnp.where(chosen[2], sc[2], sc[1]))
    den = s_lo + s_hi
    cls_ref[...] = grp * 6 + pair
    wlo_ref[...] = s_lo / den
    whi_ref[...] = s_hi / den


def _route(logits_t, router_bias):
    _, r, _ = logits_t.shape
    full = pl.BlockSpec((r, LANES), lambda i: (0, 0))
    return pl.pallas_call(
        _route_kernel,
        out_shape=[jax.ShapeDtypeStruct((r, LANES), jnp.int32), jax.ShapeDtypeStruct((r, LANES), F32),
                   jax.ShapeDtypeStruct((r, LANES), F32)],
        grid=(1,),
        in_specs=[pl.BlockSpec(memory_space=pltpu.SMEM),
                  pl.BlockSpec((N_EXPERTS, r, LANES), lambda i: (0, 0, 0))],
        out_specs=[full, full, full],
        compiler_params=_cparams(("arbitrary",)),
        name="route",
    )(router_bias, logits_t)


def _gather_kernel(idx_ref, n_ref, src_ref, dst_ref, sem):
    n_windows = n_ref[0] // GATHER_WINDOW

    def copy(src_row, dst_row):
        return pltpu.make_async_copy(src_ref.at[pl.ds(src_row, 1)], dst_ref.at[pl.ds(dst_row, 1)], sem)

    def issue(w):
        def body(r, carry):
            p = w * GATHER_WINDOW + r
            copy(idx_ref[p], p).start()
            return carry
        lax.fori_loop(0, GATHER_WINDOW, body, 0)

    def drain():
        def body(r, carry):
            copy(0, 0).wait()
            return carry
        lax.fori_loop(0, GATHER_WINDOW, body, 0)

    @pl.when(n_windows > 0)
    def _():
        issue(0)

        def body(w, carry):
            issue(w)
            drain()
            return carry
        lax.fori_loop(1, n_windows, body, 0)
        drain()


def _gather_rows(src, idx, n_rows, n_out):
    d = src.shape[1]
    return pl.pallas_call(
        _gather_kernel,
        out_shape=jax.ShapeDtypeStruct((n_out, d), src.dtype),
        grid_spec=pltpu.PrefetchScalarGridSpec(
            num_scalar_prefetch=2,
            grid=(1,),
            in_specs=[pl.BlockSpec(memory_space=pl.ANY)],
            out_specs=pl.BlockSpec(memory_space=pl.ANY),
            scratch_shapes=[pltpu.SemaphoreType.DMA],
        ),
        compiler_params=pltpu.CompilerParams(dimension_semantics=("arbitrary",), has_side_effects=True),
        name="gather_rows",
    )(idx, n_rows, src)


def _moe_kernel(lo_ref, hi_ref, nt_ref, x_ref, wr_ref, w1l_ref, w1h_ref, w3l_ref, w3h_ref, w2l_ref, w2h_ref, o_ref):
    j = pl.program_id(0)

    @pl.when(j < nt_ref[0])
    def _():
        x = x_ref[...].astype(BF16)
        wr = wr_ref[...]
        he_lo = _silu(_dot(x, w1l_ref[0])) * _dot(x, w3l_ref[0]) * wr[:, 0:1]
        he_hi = _silu(_dot(x, w1h_ref[0])) * _dot(x, w3h_ref[0]) * wr[:, 1:2]
        o_ref[...] = _dot(he_lo.astype(BF16), w2l_ref[0]) + _dot(he_hi.astype(BF16), w2h_ref[0])

    @pl.when(j >= nt_ref[0])
    def _():
        o_ref[...] = jnp.zeros_like(o_ref)


def _moe_experts(xs, wrow, tile_lo, tile_hi, n_tiles, w1, w3, w2):
    tp, d = xs.shape
    tm = MOE_TILE
    de = w1.shape[2]

    def row_map(j, lo, hi, nt):
        return (jnp.minimum(j, nt[0] - 1), 0)

    def lo_map(j, lo, hi, nt):
        return (lo[jnp.minimum(j, nt[0] - 1)], 0, 0)

    def hi_map(j, lo, hi, nt):
        return (hi[jnp.minimum(j, nt[0] - 1)], 0, 0)

    up = (1, d, de)
    down = (1, de, d)
    return pl.pallas_call(
        _moe_kernel,
        out_shape=jax.ShapeDtypeStruct((tp, d), F32),
        grid_spec=pltpu.PrefetchScalarGridSpec(
            num_scalar_prefetch=3,
            grid=(tp // tm,),
            in_specs=[pl.BlockSpec((tm, d), row_map), pl.BlockSpec((tm, 2), row_map),
                      pl.BlockSpec(up, lo_map), pl.BlockSpec(up, hi_map),
                      pl.BlockSpec(up, lo_map), pl.BlockSpec(up, hi_map),
                      pl.BlockSpec(down, lo_map), pl.BlockSpec(down, hi_map)],
            out_specs=pl.BlockSpec((tm, d), lambda j, lo, hi, nt: (j, 0)),
        ),
        compiler_params=_cparams(("arbitrary",)),
        name="moe_experts",
    )(tile_lo, tile_hi, n_tiles, xs, wrow, w1, w1, w3, w3, w2, w2)


def _moe(h2, logits, router_bias, w1, w3, w2):
    t, d = h2.shape
    tm = MOE_TILE
    n_tiles_max = t // tm + N_CLASSES
    tp = n_tiles_max * tm
    lg_t = logits[:, :N_EXPERTS].T.reshape(N_EXPERTS, t // LANES, LANES)
    cls, wlo, whi = _route(lg_t, router_bias)
    cls, wlo, whi = cls.reshape(t), wlo.reshape(t), whi.reshape(t)
    onehot = (cls[:, None] == jnp.arange(N_CLASSES, dtype=jnp.int32)[None, :]).astype(jnp.int32)
    csum = jnp.cumsum(onehot, axis=0)
    rank = jnp.take_along_axis(csum, cls[:, None], axis=1)[:, 0] - 1
    tiles_per = (csum[-1] + tm - 1) // tm
    tile_end = jnp.cumsum(tiles_per)
    dest = ((tile_end - tiles_per) * tm)[cls] + rank
    n_tiles = tile_end[-1:]
    tile_cls = jnp.searchsorted(tile_end, jnp.arange(n_tiles_max, dtype=jnp.int32), side="right").astype(jnp.int32)
    tile_cls = jnp.minimum(tile_cls, N_CLASSES - 1)
    pair = tile_cls % 6
    base = (tile_cls // 6) * PER_GROUP
    tile_lo = base + jnp.asarray(PAIR_LO, jnp.int32)[pair]
    tile_hi = base + jnp.asarray(PAIR_HI, jnp.int32)[pair]
    src = jnp.zeros((tp,), jnp.int32).at[dest].set(jnp.arange(t, dtype=jnp.int32))
    wrow = jnp.zeros((tp, 2), F32).at[dest].set(jnp.stack([wlo, whi], axis=1))
    xs = _gather_rows(h2, src, jnp.full((1,), tp, jnp.int32), tp)
    ys = _moe_experts(xs, wrow, tile_lo, tile_hi, n_tiles, w1, w3, w2)
    return _gather_rows(ys, dest, jnp.full((1,), t, jnp.int32), t)


def _final_kernel(s_ref, y_ref, mod_ref, g_ref, o_ref):
    x = s_ref[0] + mod_ref[0, 0, 5:6, :] * y_ref[0]
    o_ref[0] = _rms(x, g_ref[...])


def _final(s, y, mod, final_g):
    b, n, d = s.shape
    tm = ROW_TILE
    skip = N_CTX // tm
    row_spec = pl.BlockSpec((1, tm, d), lambda i, j: (i, j + skip, 0))
    return pl.pallas_call(
        _final_kernel,
        out_shape=jax.ShapeDtypeStruct((b, n - N_CTX, d), F32),
        grid=(b, (n - N_CTX) // tm),
        in_specs=[row_spec, row_spec, pl.BlockSpec((1, 1, 6, d), lambda i, j: (i, 1, 0, 0)),
                  pl.BlockSpec((1, d), lambda i, j: (0, 0))],
        out_specs=pl.BlockSpec((1, tm, d), lambda i, j: (i, j, 0)),
        compiler_params=_cparams(("arbitrary", "arbitrary")),
        name="final_norm",
    )(s, y, mod, final_g.reshape(1, d))


def _rope_tables(seq):
    pos = jnp.arange(seq)
    row_pos, col_pos = pos // GRID_W, pos % GRID_W
    n = A_DH // 2
    inv = ROPE_BASE ** (-jnp.arange(0, n, 2, dtype=F32) / n)
    ang_r = row_pos.astype(F32)[:, None] * inv[None, :]
    ang_c = col_pos.astype(F32)[:, None] * inv[None, :]
    ang = jnp.concatenate([ang_r, ang_c], axis=-1)
    cos = jnp.tile(jnp.cos(ang), (1, 4))
    sin = jnp.tile(jnp.sin(ang), (1, 4))
    sin = jnp.concatenate([-sin[:, :LANES // 2], sin[:, LANES // 2:]], axis=-1)
    cos = jnp.concatenate([jnp.ones((N_CTX, LANES), F32), cos], axis=0)
    sin = jnp.concatenate([jnp.zeros((N_CTX, LANES), F32), sin], axis=0)
    return cos, sin


def _qk_perm():
    perm = np.zeros((A_HEADS, LANES), np.int64)
    for p in range(LANES):
        half, c, r = p // 64, (p % 64) // 32, p % 32
        part, f = r // 16, r % 16
        perm[:, p] = np.arange(A_HEADS) * LANES + c * A_DH + part * 32 + half * 16 + f
    return perm.reshape(-1)


def _att_weights(w_in):
    a_qk = A_HEADS * 2 * A_DH
    a_v = A_HEADS * A_DV
    b_w = B_HEADS * B_DH
    perm = _qk_perm()
    qa = w_in[:, :a_qk][:, perm] * (A_DH ** -0.5)
    ka = w_in[:, a_qk:2 * a_qk][:, perm]
    va = w_in[:, 2 * a_qk:2 * a_qk + a_v]
    o = 2 * a_qk + a_v
    qb = w_in[:, o:o + b_w] * (B_DH ** -0.5)
    rest = w_in[:, o + b_w:]
    return jnp.concatenate([qa, ka, va, qb, rest], axis=1).astype(BF16)


def _gla_weights(w_in):
    d = w_in.shape[0]
    n_main = 2 * C_HEADS * C_DK + 2 * C_HEADS * C_DV
    pad = jnp.zeros((d, LANES - 2 * C_RANK), w_in.dtype)
    return jnp.concatenate([w_in, pad], axis=1).astype(BF16), n_main


def kernel(x, c, ctx, c_ctx, w_mod, b_mod, norm_g, final_g, att_w_in, att_w_out, att_lambda, att_subln_g, na_bias,
           gla_w_in, gla_w_gate, gla_b_gate, gla_norm_g, gla_w_out, router_w, router_bias, moe_w1, moe_w3, moe_w2):
    b, seq, d = x.shape
    n = N_CTX + seq
    s = jnp.concatenate([ctx, x], axis=1)

    rows = b + 1
    rows_pad = -(-rows // 8) * 8
    cc = jnp.concatenate([c, c_ctx[None, :], jnp.zeros((rows_pad - rows, d), F32)], axis=0)
    mod_all = _mod_vectors(cc, w_mod, b_mod)
    mod_x = mod_all[:, :b].reshape(DEPTH, b, 1, 6, d)
    mod_c = jnp.broadcast_to(mod_all[:, b].reshape(DEPTH, 1, 1, 6, d), (DEPTH, b, 1, 6, d))
    mods = jnp.concatenate([mod_c, mod_x], axis=2)

    cos, sin = _rope_tables(seq)
    rw_hi = router_w.astype(BF16)
    rw_lo = (router_w - rw_hi.astype(F32)).astype(BF16)
    zpad = jnp.zeros((d, LANES - 2 * N_EXPERTS), BF16)
    rw1 = jnp.concatenate([rw_hi, rw_lo, zpad], axis=1)
    rw2 = jnp.concatenate([rw_hi, jnp.zeros((d, N_EXPERTS), BF16), zpad], axis=1)

    y = None
    for i in range(DEPTH):
        j = i // 2
        modp = mods[i - 1] if i else None
        if i % 2 == 0:
            lam_init = 0.8 - 0.6 * math.exp(-0.3 * i)
            w = _att_weights(att_w_in[j])
            outs = _project(s, y, modp, mods[i], norm_g[i, 0], w, cos, sin,
                            n_rope=2 * A_HEADS * LANES, n_bf16=w.shape[1])
            if i:
                s = outs[0]
            p = outs[-1]
            oa = _diff_attention(p, att_lambda[j], att_subln_g[j], lam_init)
            ob = _neighbourhood_attention(p, _na_bias_table(na_bias[j], seq // GRID_W))
            s, h2, logits = _out_project((oa, ob), att_w_out[j].astype(BF16), s, mods[i], norm_g[i, 1],
                                         rw1, rw2, gla=False)
        else:
            w, n_main = _gla_weights(gla_w_in[j])
            outs = _project(s, y, modp, mods[i], norm_g[i, 0], w, None, None, n_rope=0, n_bf16=n_main)
            s, p, lr = outs
            wg = jnp.zeros((2, LANES, C_HEADS * C_DK), F32)
            wg = wg.at[0, :C_RANK].set(gla_w_gate[j, 0]).at[1, C_RANK:2 * C_RANK].set(gla_w_gate[j, 1])
            o = _gla(p, lr, wg, gla_b_gate[j])
            s, h2, logits = _out_project((o, p), gla_w_out[j].astype(BF16), s, mods[i], norm_g[i, 1],
                                         rw1, rw2, gla=True, gn=gla_norm_g[j])
        y = _moe(h2.reshape(b * n, d), logits.reshape(b * n, LANES), router_bias,
                 moe_w1[i].astype(BF16), moe_w3[i].astype(BF16), moe_w2[i].astype(BF16)).reshape(b, n, d)
    return _final(s, y, mods[DEPTH - 1], final_g)
```

```python
import functools
import math

import jax
import jax.numpy as jnp
import numpy as np
from jax import lax
from jax.experimental import pallas as pl
from jax.experimental.pallas import tpu as pltpu

F32 = jnp.float32
BF16 = jnp.bfloat16

D_MODEL = 1024
DEPTH = 4
GRID_W = 64
N_CTX = 256
A_HEADS = 4
A_DH = 64
A_DV = 128
B_HEADS = 8
B_DH = 64
NA_ROWS = 8
NA_COLS = 16
C_HEADS = 4
C_DK = 128
C_DV = 256
C_RANK = 16
C_TAU = 16.0
C_CHUNK = 64
N_EXPERTS = 16
N_GROUPS = 4
PER_GROUP = 4
D_EXPERT = 512
ROPE_BASE = 10000.0
EPS = 1e-6

LANES = 128
ROW_TILE = 256
MOE_TILE = 256
N_CLASSES = N_GROUPS * 6
NA_QROWS = 4
NA_KROWS = NA_QROWS + NA_ROWS - 1
VMEM_LIMIT = 52 * 1024 * 1024
ROW_SEGS = D_MODEL // LANES

PAIR_LO = (0, 0, 0, 1, 1, 2)
PAIR_HI = (1, 2, 3, 2, 3, 3)


def _cparams(sem):
    return pltpu.CompilerParams(dimension_semantics=sem, vmem_limit_bytes=VMEM_LIMIT)


def _sigmoid(x):
    return 1.0 / (1.0 + jnp.exp(-x))


def _silu(x):
    return x * _sigmoid(x)


def _rms(x, g):
    return x * lax.rsqrt(jnp.mean(x * x, axis=-1, keepdims=True) + EPS) * g


def _dot(a, b):
    return jnp.dot(a, b, preferred_element_type=F32)


def _dot_nt(a, b):
    return lax.dot_general(a, b, (((1,), (1,)), ((), ())), preferred_element_type=F32)


def _dot_tn(a, b):
    return lax.dot_general(a, b, (((0,), (0,)), ((), ())), preferred_element_type=F32)


def _mod_kernel(c_ref, w_ref, b_ref, o_ref):
    a = _silu(c_ref[...]).astype(BF16)
    o_ref[0] = _dot(a, w_ref[0].astype(BF16)) + b_ref[0]


def _mod_vectors(cc, w_mod, b_mod):
    depth, d, n6 = w_mod.shape
    rows = cc.shape[0]
    tn = 1536
    return pl.pallas_call(
        _mod_kernel,
        out_shape=jax.ShapeDtypeStruct((depth, rows, n6), F32),
        grid=(depth, n6 // tn),
        in_specs=[
            pl.BlockSpec((rows, d), lambda i, j: (0, 0)),
            pl.BlockSpec((1, d, tn), lambda i, j: (i, 0, j)),
            pl.BlockSpec((1, 1, tn), lambda i, j: (i, 0, j)),
        ],
        out_specs=pl.BlockSpec((1, rows, tn), lambda i, j: (i, 0, j)),
        compiler_params=_cparams(("arbitrary", "arbitrary")),
        name="mod_vectors",
    )(cc, w_mod, b_mod.reshape(depth, 1, n6))


def _token_rows(ref):
    return jnp.concatenate([ref[0, :, sg, :] for sg in range(ROW_SEGS)], axis=-1)


def _proj_kernel(*refs, has_y, n_rope, n_bf16, col_chunk):
    it = iter(refs)
    s_ref = next(it)
    y_ref = next(it) if has_y else None
    modp_ref = next(it) if has_y else None
    mod_ref = next(it)
    ng_ref = next(it)
    w_ref = next(it)
    cos_ref = next(it) if n_rope else None
    sin_ref = next(it) if n_rope else None
    snew_ref = next(it) if has_y else None
    p_ref = next(it)
    lr_ref = next(it) if w_ref.shape[1] > n_bf16 else None

    x = s_ref[0]
    if has_y:
        x = x + modp_ref[0, 0, 5:6, :] * _token_rows(y_ref)
        snew_ref[0] = x
    h = _rms(x, ng_ref[...])
    h = h * (1.0 + mod_ref[0, 0, 1:2, :]) + mod_ref[0, 0, 0:1, :]
    hb = h.astype(BF16)
    n_out = w_ref.shape[1]
    for c0 in range(0, n_out, col_chunk):
        c1 = min(c0 + col_chunk, n_out)
        acc = _dot(hb, w_ref[:, c0:c1])
        for b0 in range(c0, c1, LANES):
            t = acc[:, b0 - c0:b0 - c0 + LANES]
            if b0 < n_rope:
                t = t * cos_ref[...] + pltpu.roll(t, LANES // 2, 1) * sin_ref[...]
            if b0 < n_bf16:
                p_ref[0, :, b0:b0 + LANES] = t.astype(BF16)
            else:
                lr_ref[0, :, b0 - n_bf16:b0 - n_bf16 + LANES] = t


def _project(s, y, modp, mod, ng, w, cos, sin, *, n_rope, n_bf16):
    b, n, d = s.shape
    n_out = w.shape[1]
    tm = ROW_TILE
    has_y = y is not None
    row_spec = pl.BlockSpec((1, tm, d), lambda i, j: (i, j, 0))
    mod_spec = pl.BlockSpec((1, 1, 6, d), lambda i, j: (i, jnp.minimum(j, 1), 0, 0))
    in_specs = [row_spec]
    args = [s]
    if has_y:
        in_specs += [pl.BlockSpec((1, tm, ROW_SEGS, LANES), lambda i, j: (i, j, 0, 0)), mod_spec]
        args += [y, modp]
    in_specs += [mod_spec, pl.BlockSpec((1, d), lambda i, j: (0, 0)),
                 pl.BlockSpec((d, n_out), lambda i, j: (0, 0))]
    args += [mod, ng.reshape(1, d), w]
    if n_rope:
        tab_spec = pl.BlockSpec((tm, LANES), lambda i, j: (j, 0))
        in_specs += [tab_spec, tab_spec]
        args += [cos, sin]
    out_shape, out_specs = [], []
    if has_y:
        out_shape.append(jax.ShapeDtypeStruct((b, n, d), F32))
        out_specs.append(row_spec)
    out_shape.append(jax.ShapeDtypeStruct((b, n, n_bf16), BF16))
    out_specs.append(pl.BlockSpec((1, tm, n_bf16), lambda i, j: (i, j, 0)))
    if n_out > n_bf16:
        out_shape.append(jax.ShapeDtypeStruct((b, n, n_out - n_bf16), F32))
        out_specs.append(pl.BlockSpec((1, tm, n_out - n_bf16), lambda i, j: (i, j, 0)))
    return pl.pallas_call(
        functools.partial(_proj_kernel, has_y=has_y, n_rope=n_rope, n_bf16=n_bf16, col_chunk=512),
        out_shape=out_shape,
        grid=(b, n // tm),
        in_specs=in_specs,
        out_specs=out_specs,
        compiler_params=_cparams(("arbitrary", "arbitrary")),
        name="norm_mod_project",
    )(*args)


def _softmax_rows(s):
    e = jnp.exp(s - jnp.max(s, axis=-1, keepdims=True))
    return e * (1.0 / jnp.sum(e, axis=-1, keepdims=True))


def _diff_attn_kernel(q_ref, k_ref, v_ref, lam_ref, g_ref, o_ref, *, lam_init):
    lp = lam_ref[...]
    lam = (jnp.exp(jnp.sum(lp[0:1] * lp[1:2], axis=-1, keepdims=True))
           - jnp.exp(jnp.sum(lp[2:3] * lp[3:4], axis=-1, keepdims=True)) + lam_init)
    q = q_ref[0]
    lane = lax.broadcasted_iota(jnp.int32, (1, LANES), 1)
    first_map = (lane // 32) % 2 == 0
    zero = jnp.zeros_like(q)
    q0 = jnp.where(first_map, q, zero)
    q1 = jnp.where(first_map, zero, q)

    def attend(n_keys):
        k = k_ref[0, 0:n_keys, :]
        v = v_ref[0, 0:n_keys, :]
        a = _softmax_rows(_dot_nt(q0, k)) - lam * _softmax_rows(_dot_nt(q1, k))
        o = _dot(a.astype(BF16), v)
        o_ref[0] = (_rms(o, g_ref[...]) * (1.0 - lam_init)).astype(BF16)

    qi = pl.program_id(2)

    @pl.when(qi == 0)
    def _():
        attend(N_CTX)

    @pl.when(qi > 0)
    def _():
        attend(k_ref.shape[1])


def _diff_attention(p, lam_p, subln_g, lam_init):
    b, n, _ = p.shape
    tq = ROW_TILE
    return pl.pallas_call(
        functools.partial(_diff_attn_kernel, lam_init=lam_init),
        out_shape=jax.ShapeDtypeStruct((b, n, A_HEADS * A_DV), BF16),
        grid=(b, A_HEADS, n // tq),
        in_specs=[
            pl.BlockSpec((1, tq, LANES), lambda i, h, j: (i, j, h)),
            pl.BlockSpec((1, n, LANES), lambda i, h, j: (i, 0, A_HEADS + h)),
            pl.BlockSpec((1, n, LANES), lambda i, h, j: (i, 0, 2 * A_HEADS + h)),
            pl.BlockSpec((4, A_DH), lambda i, h, j: (0, 0)),
            pl.BlockSpec((1, A_DV), lambda i, h, j: (0, 0)),
        ],
        out_specs=pl.BlockSpec((1, tq, LANES), lambda i, h, j: (i, j, h)),
        compiler_params=_cparams(("arbitrary", "arbitrary", "arbitrary")),
        name="diff_attention",
    )(p, p, p, lam_p, subln_g.reshape(1, A_DV))


def _na_kernel(q_ref, k_ref, v_ref, bm_ref, o_ref):
    blk = pl.program_id(2)
    q = q_ref[0]
    lane = lax.broadcasted_iota(jnp.int32, (1, LANES), 1)
    first_head = lane < B_DH
    zero = jnp.zeros_like(q)
    qh = (jnp.where(first_head, q, zero), jnp.where(first_head, zero, q))
    kc = k_ref[0, 0:N_CTX, :]
    vc = v_ref[0, 0:N_CTX, :]

    @pl.when(blk == 0)
    def _():
        outs = [_dot(_softmax_rows(_dot_nt(qh[h], kc)).astype(BF16), vc) for h in range(2)]
        o_ref[0] = jnp.where(first_head, outs[0], outs[1]).astype(BF16)

    @pl.when(blk > 0)
    def _():
        rows = (k_ref.shape[1] - N_CTX) // GRID_W
        k_row0 = jnp.clip((blk - 1) * NA_QROWS - NA_ROWS // 2, 0, rows - NA_KROWS)
        start = pl.multiple_of(N_CTX + k_row0 * GRID_W, GRID_W)
        kw = k_ref[0, pl.ds(start, NA_KROWS * GRID_W), :]
        vw = v_ref[0, pl.ds(start, NA_KROWS * GRID_W), :]
        outs = []
        for h in range(2):
            s_loc = _dot_nt(qh[h], kw) + bm_ref[h, 0]
            s_ctx = _dot_nt(qh[h], kc)
            m = jnp.maximum(jnp.max(s_loc, axis=-1, keepdims=True), jnp.max(s_ctx, axis=-1, keepdims=True))
            e_loc = jnp.exp(s_loc - m)
            e_ctx = jnp.exp(s_ctx - m)
            den = jnp.sum(e_loc, axis=-1, keepdims=True) + jnp.sum(e_ctx, axis=-1, keepdims=True)
            outs.append((_dot(e_loc.astype(BF16), vw) + _dot(e_ctx.astype(BF16), vc)) * (1.0 / den))
        o_ref[0] = jnp.where(first_head, outs[0], outs[1]).astype(BF16)


def _na_bias_table(na_bias, rows):
    h = na_bias.shape[0]
    n_dr, n_dc = 2 * NA_ROWS - 1, 2 * NA_COLS - 1
    width = 2 * GRID_W
    left = GRID_W - NA_COLS
    u = jnp.pad(na_bias, ((0, 0), (0, 0), (left, width - left - n_dc)))
    skew = jnp.tile(u, (1, 1, GRID_W))[:, :, :GRID_W * (width - 1)].reshape(h, n_dr, GRID_W, width - 1)
    toeplitz = skew[:, :, :, GRID_W - 1:]
    margin = NA_KROWS - NA_ROWS
    by_col = jnp.pad(toeplitz.transpose(0, 2, 1, 3), ((0, 0), (0, 0), (margin, margin), (0, 0)))
    col = np.arange(GRID_W)
    col_start = np.clip(col - NA_COLS // 2, 0, GRID_W - NA_COLS)
    col_ok = (col[None, :] >= col_start[:, None]) & (col[None, :] < col_start[:, None] + NA_COLS)
    blocks = []
    for q_row0 in (0, NA_QROWS, 2 * NA_QROWS, rows - NA_QROWS):
        k_row0 = int(np.clip(q_row0 - NA_ROWS // 2, 0, rows - NA_KROWS))
        kr = k_row0 + np.arange(NA_KROWS)
        for rq in range(NA_QROWS):
            r = q_row0 + rq
            r0 = int(np.clip(r - NA_ROWS // 2, 0, rows - NA_ROWS))
            row_ok = (kr >= r0) & (kr < r0 + NA_ROWS)
            ok = col_ok[:, None, :] & row_ok[None, :, None]
            first = k_row0 - r + NA_ROWS - 1 + margin
            blocks.append(jnp.where(ok[None], by_col[:, :, first:first + NA_KROWS, :], -jnp.inf))
    table = jnp.stack(blocks, axis=1)
    return table.reshape(h, 4, NA_QROWS * GRID_W, NA_KROWS * GRID_W)


def _neighbourhood_attention(p, bm):
    b, n, _ = p.shape
    tq = NA_QROWS * GRID_W
    col0 = 3 * A_HEADS
    pairs = B_HEADS // 2

    def pattern(j):
        return jnp.clip(j - 1, 0, 2) + (j == n // tq - 1).astype(jnp.int32)

    return pl.pallas_call(
        _na_kernel,
        out_shape=jax.ShapeDtypeStruct((b, n, B_HEADS * B_DH), BF16),
        grid=(b, pairs, n // tq),
        in_specs=[
            pl.BlockSpec((1, tq, LANES), lambda i, h, j: (i, j, col0 + h)),
            pl.BlockSpec((1, n, LANES), lambda i, h, j: (i, 0, col0 + pairs + h)),
            pl.BlockSpec((1, n, LANES), lambda i, h, j: (i, 0, col0 + 2 * pairs + h)),
            pl.BlockSpec((2, 1, tq, NA_KROWS * GRID_W), lambda i, h, j: (h, pattern(j), 0, 0)),
        ],
        out_specs=pl.BlockSpec((1, tq, LANES), lambda i, h, j: (i, j, h)),
        compiler_params=_cparams(("arbitrary", "arbitrary", "arbitrary")),
        name="neighbourhood_attention",
    )(p, p, p, bm)


def _gla_kernel(q_ref, k_ref, v_ref, lr_ref, wg_ref, bg_ref, o_ref, g_ref, st_ref):
    n = q_ref.shape[1]
    n_chunks = n // C_CHUNK
    ctx_chunks = N_CTX // C_CHUNK
    ri = lax.broadcasted_iota(jnp.int32, (C_CHUNK, C_CHUNK), 0)
    ci = lax.broadcasted_iota(jnp.int32, (C_CHUNK, C_CHUNK), 1)
    lr = lr_ref[0].astype(BF16)

    def run(direction):
        z = _dot(lr, wg_ref[direction].astype(BF16)) + bg_ref[direction:direction + 1, :]
        g_ref[...] = (jnp.minimum(z, 0.0) - jnp.log(1.0 + jnp.exp(-jnp.abs(z)))) * (1.0 / C_TAU)
        st_ref[...] = jnp.zeros_like(st_ref)
        keep = (ci <= ri) if direction == 0 else (ci >= ri)
        tri = keep.astype(F32)
        end_row = C_CHUNK - 1 if direction == 0 else 0
        mid_row = C_CHUNK // 2 - 1 if direction == 0 else C_CHUNK // 2

        def chunk(c):
            rows = pl.ds(pl.multiple_of(c * C_CHUNK, C_CHUNK), C_CHUNK)
            gc = jnp.dot(tri, g_ref[rows, :], preferred_element_type=F32, precision=lax.Precision.HIGHEST)
            g_end = gc[end_row:end_row + 1, :]
            g_mid = gc[mid_row:mid_row + 1, :]
            q = q_ref[0, rows, :].astype(F32) * (C_DK ** -0.5)
            k = k_ref[0, rows, :].astype(F32)
            v = v_ref[0, rows, :]
            q_in = (q * jnp.exp(gc)).astype(BF16)
            q_mid = (q * jnp.exp(gc - g_mid)).astype(BF16)
            k_mid = (k * jnp.exp(g_mid - gc)).astype(BF16)
            k_end = (k * jnp.exp(g_end - gc)).astype(BF16)
            a = jnp.where(keep, _dot_nt(q_mid, k_mid), 0.0)
            st = st_ref[...]
            o = _dot_nt(q_in, st.astype(BF16)) + _dot(a.astype(BF16), v)
            if direction == 0:
                o_ref[0, rows, :] = o
            else:
                o_ref[0, rows, :] = o_ref[0, rows, :] + o
            st_ref[...] = st * jnp.exp(g_end) + _dot_tn(v, k_end)

        if direction == 0:
            def body(c, carry):
                chunk(c)
                return carry
            lax.fori_loop(0, n_chunks, body, 0)
        else:
            def body_ctx(i, carry):
                chunk(ctx_chunks - 1 - i)
                return carry
            lax.fori_loop(0, ctx_chunks, body_ctx, 0)

            def body_x(i, carry):
                chunk(n_chunks - 1 - i)
                return carry
            lax.fori_loop(0, n_chunks - ctx_chunks, body_x, 0)

    run(0)
    run(1)


def _gla(p, lr, wg, bg):
    b, n, _ = p.shape
    return pl.pallas_call(
        _gla_kernel,
        out_shape=jax.ShapeDtypeStruct((b, n, C_HEADS * C_DV), F32),
        grid=(b, C_HEADS),
        in_specs=[
            pl.BlockSpec((1, n, C_DK), lambda i, h: (i, 0, h)),
            pl.BlockSpec((1, n, C_DK), lambda i, h: (i, 0, C_HEADS + h)),
            pl.BlockSpec((1, n, C_DV), lambda i, h: (i, 0, C_HEADS + h)),
            pl.BlockSpec((1, n, LANES), lambda i, h: (i, 0, 0)),
            pl.BlockSpec((2, LANES, C_DK), lambda i, h: (0, 0, h)),
            pl.BlockSpec((2, C_DK), lambda i, h: (0, h)),
        ],
        out_specs=pl.BlockSpec((1, n, C_DV), lambda i, h: (i, 0, h)),
        scratch_shapes=[pltpu.VMEM((n, C_DK), F32), pltpu.VMEM((C_DV, C_DK), F32)],
        compiler_params=_cparams(("arbitrary", "arbitrary")),
        name="gla",
    )(p, p, p, lr, wg, bg)


def _out_kernel(*refs, gla):
    if gla:
        o_ref, gate_ref, gn_ref, w_ref, s_ref, mod_ref, ng_ref, rw1_ref, rw2_ref, snew_ref, h2_ref, lg_ref = refs
        o = o_ref[0]
        gate = gate_ref[0].astype(F32)
        parts = []
        for hd in range(C_HEADS):
            oh = _rms(o[:, hd * C_DV:(hd + 1) * C_DV], gn_ref[...])
            parts.append((oh * _silu(gate[:, hd * C_DV:(hd + 1) * C_DV])).astype(BF16))
        acc = _dot(jnp.concatenate(parts, axis=-1), w_ref[...])
    else:
        oa_ref, ob_ref, w_ref, s_ref, mod_ref, ng_ref, rw1_ref, rw2_ref, snew_ref, h2_ref, lg_ref = refs
        half = oa_ref.shape[2]
        acc = _dot(oa_ref[0], w_ref[0:half, :]) + _dot(ob_ref[0], w_ref[half:, :])
    x = s_ref[0] + mod_ref[0, 0, 2:3, :] * acc
    snew_ref[0] = x
    h2 = _rms(x, ng_ref[...]) * (1.0 + mod_ref[0, 0, 4:5, :]) + mod_ref[0, 0, 3:4, :]
    for sg in range(ROW_SEGS):
        h2_ref[0, :, sg, :] = h2[:, sg * LANES:(sg + 1) * LANES]
    hi = h2.astype(BF16)
    lo = (h2 - hi.astype(F32)).astype(BF16)
    t = _dot(hi, rw1_ref[...])
    lg_ref[0] = t + pltpu.roll(t, LANES - N_EXPERTS, 1) + _dot(lo, rw2_ref[...])


def _out_project(mix, w_out, s, mod, ng, rw1, rw2, *, gla, gn=None):
    b, n, d = s.shape
    tm = ROW_TILE
    row_spec = pl.BlockSpec((1, tm, d), lambda i, j: (i, j, 0))
    const2 = lambda i, j: (0, 0)
    if gla:
        o, p = mix
        in_specs = [row_spec, pl.BlockSpec((1, tm, d), lambda i, j: (i, j, 2)),
                    pl.BlockSpec((1, C_DV), const2)]
        args = [o, p, gn.reshape(1, C_DV)]
    else:
        oa, ob = mix
        half_spec = pl.BlockSpec((1, tm, oa.shape[2]), lambda i, j: (i, j, 0))
        in_specs = [half_spec, half_spec]
        args = [oa, ob]
    in_specs += [pl.BlockSpec((d, d), const2), row_spec,
                 pl.BlockSpec((1, 1, 6, d), lambda i, j: (i, jnp.minimum(j, 1), 0, 0)),
                 pl.BlockSpec((1, d), const2), pl.BlockSpec((d, LANES), const2), pl.BlockSpec((d, LANES), const2)]
    args += [w_out, s, mod, ng.reshape(1, d), rw1, rw2]
    return pl.pallas_call(
        functools.partial(_out_kernel, gla=gla),
        out_shape=[jax.ShapeDtypeStruct((b, n, d), F32), jax.ShapeDtypeStruct((b, n, ROW_SEGS, LANES), F32),
                   jax.ShapeDtypeStruct((b, n, LANES), F32)],
        grid=(b, n // tm),
        in_specs=in_specs,
        out_specs=[row_spec, pl.BlockSpec((1, tm, ROW_SEGS, LANES), lambda i, j: (i, j, 0, 0)),
                   pl.BlockSpec((1, tm, LANES), lambda i, j: (i, j, 0))],
        compiler_params=_cparams(("arbitrary", "arbitrary")),
        name="out_project",
    )(*args)


def _route_kernel(bias_ref, lg_ref, cls_ref, wlo_ref, whi_ref):
    score = [_sigmoid(lg_ref[e]) for e in range(N_EXPERTS)]
    sel = [score[e] + bias_ref[e] for e in range(N_EXPERTS)]
    grp_score = []
    for g in range(N_GROUPS):
        v = sel[g * PER_GROUP:(g + 1) * PER_GROUP]
        best = v[0] + v[1]
        for a in range(PER_GROUP):
            for c in range(a + 1, PER_GROUP):
                if (a, c) != (0, 1):
                    best = jnp.maximum(best, v[a] + v[c])
        grp_score.append(best)
    grp = jnp.zeros(grp_score[0].shape, jnp.int32)
    best = grp_score[0]
    for g in range(1, N_GROUPS):
        upd = grp_score[g] > best
        best = jnp.where(upd, grp_score[g], best)
        grp = jnp.where(upd, g, grp)

    def pick(vals, j):
        out = vals[j]
        for g in range(1, N_GROUPS):
            out = jnp.where(grp == g, vals[g * PER_GROUP + j], out)
        return out

    v = [pick(sel, j) for j in range(PER_GROUP)]
    sc = [pick(score, j) for j in range(PER_GROUP)]
    one = jnp.ones(grp.shape, jnp.int32)
    zero = jnp.zeros(grp.shape, jnp.int32)
    chosen = []
    for j in range(PER_GROUP):
        rank = zero
        for m in range(PER_GROUP):
            if m == j:
                continue
            ahead = (v[m] >= v[j]) if m < j else (v[m] > v[j])
            rank = rank + jnp.where(ahead, one, zero)
        chosen.append(rank < 2)
    code = zero
    for j in range(PER_GROUP):
        code = code + jnp.where(chosen[j], one * (1 << j), zero)
    pair = zero
    for idx in range(6):
        pair = jnp.where(code == (1 << PAIR_LO[idx]) + (1 << PAIR_HI[idx]), idx, pair)
    s_lo = jnp.where(chosen[0], sc[0], jnp.where(chosen[1], sc[1], sc[2]))
    s_hi = jnp.where(chosen[3], sc[3], jnp.where(chosen[2], sc[2], sc[1]))
    den = s_lo + s_hi
    cls_ref[...] = grp * 6 + pair
    wlo_ref[...] = s_lo / den
    whi_ref[...] = s_hi / den


def _route(logits_t, router_bias):
    _, r, _ = logits_t.shape
    full = pl.BlockSpec((r, LANES), lambda i: (0, 0))
    return pl.pallas_call(
        _route_kernel,
        out_shape=[jax.ShapeDtypeStruct((r, LANES), jnp.int32), jax.ShapeDtypeStruct((r, LANES), F32),
                   jax.ShapeDtypeStruct((r, LANES), F32)],
        grid=(1,),
        in_specs=[pl.BlockSpec(memory_space=pltpu.SMEM),
                  pl.BlockSpec((N_EXPERTS, r, LANES), lambda i: (0, 0, 0))],
        out_specs=[full, full, full],
        compiler_params=_cparams(("arbitrary",)),
        name="route",
    )(router_bias, logits_t)


def _moe_kernel(tok_ref, lo_ref, hi_ref, nv_ref, nt_ref, h2_ref, wr_ref, w1l_ref, w1h_ref, w3l_ref, w3h_ref,
                w2l_ref, w2h_ref, y_ref, xbuf, ybuf, sem_in, sem_out):
    tm = xbuf.shape[1]
    j = pl.program_id(0)
    n_tiles = nt_ref[0]
    slot = j % 2

    def gather(tile, to_slot):
        def body(r, carry):
            pltpu.make_async_copy(h2_ref.at[tok_ref[tile * tm + r]], xbuf.at[to_slot, r], sem_in.at[to_slot]).start()
            return carry
        lax.fori_loop(0, tm, body, 0, unroll=8)

    def wait_scatter(tile, from_slot):
        n = nv_ref[tile]

        @pl.when(n == tm)
        def _():
            pltpu.make_async_copy(ybuf.at[from_slot], y_ref.at[pl.ds(0, tm)], sem_out.at[from_slot]).wait()

        @pl.when(n < tm)
        def _():
            def body(r, carry):
                pltpu.make_async_copy(ybuf.at[from_slot, 0], y_ref.at[0], sem_out.at[from_slot]).wait()
                return carry
            lax.fori_loop(0, n, body, 0)

    @pl.when(j == 0)
    def _():
        gather(0, 0)

    @pl.when(j + 1 < n_tiles)
    def _():
        gather(j + 1, 1 - slot)

    @pl.when(j < n_tiles)
    def _():
        pltpu.make_async_copy(h2_ref.at[pl.ds(0, tm)], xbuf.at[slot], sem_in.at[slot]).wait()
        xb = xbuf.at[slot]
        x = jnp.concatenate([xb[:, sg, :] for sg in range(ROW_SEGS)], axis=-1).astype(BF16)
        wr = wr_ref[...]
        he_lo = _silu(_dot(x, w1l_ref[0])) * _dot(x, w3l_ref[0]) * wr[:, 0:1]
        he_hi = _silu(_dot(x, w1h_ref[0])) * _dot(x, w3h_ref[0]) * wr[:, 1:2]
        y = _dot(he_lo.astype(BF16), w2l_ref[0]) + _dot(he_hi.astype(BF16), w2h_ref[0])

        @pl.when(j >= 2)
        def _():
            wait_scatter(j - 2, slot)

        yb = ybuf.at[slot]
        for sg in range(ROW_SEGS):
            yb[:, sg, :] = y[:, sg * LANES:(sg + 1) * LANES]

        def body(r, carry):
            pltpu.make_async_copy(ybuf.at[slot, r], y_ref.at[tok_ref[j * tm + r]], sem_out.at[slot]).start()
            return carry
        lax.fori_loop(0, nv_ref[j], body, 0)

    @pl.when(j == n_tiles - 1)
    def _():
        @pl.when(j >= 1)
        def _():
            wait_scatter(j - 1, 1 - slot)
        wait_scatter(j, slot)


def _moe_experts(h2, tok, wrow, tile_lo, tile_hi, n_valid, n_tiles, w1, w3, w2):
    t = h2.shape[0]
    tp = tok.shape[0]
    tm = MOE_TILE
    d, de = w1.shape[1], w1.shape[2]

    def row_map(j, tok, lo, hi, nv, nt):
        return (jnp.minimum(j, nt[0] - 1), 0)

    def lo_map(j, tok, lo, hi, nv, nt):
        return (lo[jnp.minimum(j, nt[0] - 1)], 0, 0)

    def hi_map(j, tok, lo, hi, nv, nt):
        return (hi[jnp.minimum(j, nt[0] - 1)], 0, 0)

    up = (1, d, de)
    down = (1, de, d)
    return pl.pallas_call(
        _moe_kernel,
        out_shape=jax.ShapeDtypeStruct((t, ROW_SEGS, LANES), F32),
        grid_spec=pltpu.PrefetchScalarGridSpec(
            num_scalar_prefetch=5,
            grid=(tp // tm,),
            in_specs=[pl.BlockSpec(memory_space=pl.ANY), pl.BlockSpec((tm, 2), row_map),
                      pl.BlockSpec(up, lo_map), pl.BlockSpec(up, hi_map),
                      pl.BlockSpec(up, lo_map), pl.BlockSpec(up, hi_map),
                      pl.BlockSpec(down, lo_map), pl.BlockSpec(down, hi_map)],
            out_specs=pl.BlockSpec(memory_space=pl.ANY),
            scratch_shapes=[pltpu.VMEM((2, tm, ROW_SEGS, LANES), F32), pltpu.VMEM((2, tm, ROW_SEGS, LANES), F32),
                            pltpu.SemaphoreType.DMA((2,)), pltpu.SemaphoreType.DMA((2,))],
        ),
        compiler_params=pltpu.CompilerParams(dimension_semantics=("arbitrary",), vmem_limit_bytes=VMEM_LIMIT,
                                             has_side_effects=True),
        name="moe_experts",
    )(tok, tile_lo, tile_hi, n_valid, n_tiles, h2, wrow, w1, w1, w3, w3, w2, w2)


def _moe(h2, logits, router_bias, w1, w3, w2):
    t = h2.shape[0]
    tm = MOE_TILE
    n_tiles_max = t // tm + N_CLASSES
    tp = n_tiles_max * tm
    lg_t = logits[:, :N_EXPERTS].T.reshape(N_EXPERTS, t // LANES, LANES)
    cls, wlo, whi = _route(lg_t, router_bias)
    cls, wlo, whi = cls.reshape(t), wlo.reshape(t), whi.reshape(t)
    classes = jnp.arange(N_CLASSES, dtype=jnp.int32)
    onehot = (cls[:, None] == classes[None, :]).astype(jnp.int32)
    csum = jnp.cumsum(onehot, axis=0)
    rank = jnp.sum(csum * onehot, axis=1) - 1
    counts = csum[-1]
    tiles_per = (counts + tm - 1) // tm
    tile_end = jnp.cumsum(tiles_per)
    tile_start = tile_end - tiles_per
    dest = jnp.sum(onehot * (tile_start * tm)[None, :], axis=1) + rank
    tiles = jnp.arange(n_tiles_max, dtype=jnp.int32)
    tile_cls = jnp.minimum(jnp.sum((tile_end[None, :] <= tiles[:, None]).astype(jnp.int32), axis=1), N_CLASSES - 1)
    tile_onehot = (tile_cls[:, None] == classes[None, :]).astype(jnp.int32)
    local = tiles - jnp.sum(tile_onehot * tile_start[None, :], axis=1)
    n_valid = jnp.clip(jnp.sum(tile_onehot * counts[None, :], axis=1) - local * tm, 0, tm)
    pair = tile_cls % 6
    base = (tile_cls // 6) * PER_GROUP
    pair_onehot = (pair[:, None] == jnp.arange(6, dtype=jnp.int32)[None, :]).astype(jnp.int32)
    tile_lo = base + jnp.sum(pair_onehot * jnp.asarray(PAIR_LO, jnp.int32)[None, :], axis=1)
    tile_hi = base + jnp.sum(pair_onehot * jnp.asarray(PAIR_HI, jnp.int32)[None, :], axis=1)
    per_token = jnp.stack([jnp.arange(t, dtype=jnp.int32), lax.bitcast_convert_type(wlo, jnp.int32),
                           lax.bitcast_convert_type(whi, jnp.int32)], axis=1)
    per_row = jnp.zeros((tp, 3), jnp.int32).at[dest].set(per_token)
    tok = per_row[:, 0]
    wrow = lax.bitcast_convert_type(per_row[:, 1:3], F32)
    return _moe_experts(h2, tok, wrow, tile_lo, tile_hi, n_valid, tile_end[-1:], w1, w3, w2)


def _final_kernel(s_ref, y_ref, mod_ref, g_ref, o_ref):
    x = s_ref[0] + mod_ref[0, 0, 5:6, :] * _token_rows(y_ref)
    o_ref[0] = _rms(x, g_ref[...])


def _final(s, y, mod, final_g):
    b, n, d = s.shape
    tm = ROW_TILE
    skip = N_CTX // tm
    row_spec = pl.BlockSpec((1, tm, d), lambda i, j: (i, j + skip, 0))
    return pl.pallas_call(
        _final_kernel,
        out_shape=jax.ShapeDtypeStruct((b, n - N_CTX, d), F32),
        grid=(b, (n - N_CTX) // tm),
        in_specs=[row_spec, pl.BlockSpec((1, tm, ROW_SEGS, LANES), lambda i, j: (i, j + skip, 0, 0)),
                  pl.BlockSpec((1, 1, 6, d), lambda i, j: (i, 1, 0, 0)),
                  pl.BlockSpec((1, d), lambda i, j: (0, 0))],
        out_specs=pl.BlockSpec((1, tm, d), lambda i, j: (i, j, 0)),
        compiler_params=_cparams(("arbitrary", "arbitrary")),
        name="final_norm",
    )(s, y, mod, final_g.reshape(1, d))


def _rope_tables(seq):
    pos = jnp.arange(seq)
    row_pos, col_pos = pos // GRID_W, pos % GRID_W
    n = A_DH // 2
    inv = ROPE_BASE ** (-jnp.arange(0, n, 2, dtype=F32) / n)
    ang_r = row_pos.astype(F32)[:, None] * inv[None, :]
    ang_c = col_pos.astype(F32)[:, None] * inv[None, :]
    ang = jnp.concatenate([ang_r, ang_c], axis=-1)
    cos = jnp.tile(jnp.cos(ang), (1, 4))
    sin = jnp.tile(jnp.sin(ang), (1, 4))
    sin = jnp.concatenate([-sin[:, :LANES // 2], sin[:, LANES // 2:]], axis=-1)
    cos = jnp.concatenate([jnp.ones((N_CTX, LANES), F32), cos], axis=0)
    sin = jnp.concatenate([jnp.zeros((N_CTX, LANES), F32), sin], axis=0)
    return cos, sin


def _interleave_maps(w):
    d = w.shape[0]
    w = w.reshape(d, A_HEADS, 2, 2, 2, A_DH // 4)
    return w.transpose(0, 1, 4, 2, 3, 5).reshape(d, A_HEADS * LANES)


def _att_weights(w_in):
    a_qk = A_HEADS * 2 * A_DH
    a_v = A_HEADS * A_DV
    b_w = B_HEADS * B_DH
    qa = _interleave_maps(w_in[:, :a_qk]) * (A_DH ** -0.5)
    ka = _interleave_maps(w_in[:, a_qk:2 * a_qk])
    va = w_in[:, 2 * a_qk:2 * a_qk + a_v]
    o = 2 * a_qk + a_v
    qb = w_in[:, o:o + b_w] * (B_DH ** -0.5)
    rest = w_in[:, o + b_w:]
    return jnp.concatenate([qa, ka, va, qb, rest], axis=1).astype(BF16)


def _gla_weights(w_in):
    d = w_in.shape[0]
    n_main = 2 * C_HEADS * C_DK + 2 * C_HEADS * C_DV
    pad = jnp.zeros((d, LANES - 2 * C_RANK), w_in.dtype)
    return jnp.concatenate([w_in, pad], axis=1).astype(BF16), n_main


def kernel(x, c, ctx, c_ctx, w_mod, b_mod, norm_g, final_g, att_w_in, att_w_out, att_lambda, att_subln_g, na_bias,
           gla_w_in, gla_w_gate, gla_b_gate, gla_norm_g, gla_w_out, router_w, router_bias, moe_w1, moe_w3, moe_w2):
    b, seq, d = x.shape
    n = N_CTX + seq
    s = jnp.concatenate([ctx, x], axis=1)

    rows = b + 1
    rows_pad = -(-rows // 8) * 8
    cc = jnp.concatenate([c, c_ctx[None, :], jnp.zeros((rows_pad - rows, d), F32)], axis=0)
    mod_all = _mod_vectors(cc, w_mod, b_mod)
    mod_x = mod_all[:, :b].reshape(DEPTH, b, 1, 6, d)
    mod_c = jnp.broadcast_to(mod_all[:, b].reshape(DEPTH, 1, 1, 6, d), (DEPTH, b, 1, 6, d))
    mods = jnp.concatenate([mod_c, mod_x], axis=2)

    cos, sin = _rope_tables(seq)
    rw_hi = router_w.astype(BF16)
    rw_lo = (router_w - rw_hi.astype(F32)).astype(BF16)
    zpad = jnp.zeros((d, LANES - 2 * N_EXPERTS), BF16)
    rw1 = jnp.concatenate([rw_hi, rw_lo, zpad], axis=1)
    rw2 = jnp.concatenate([rw_hi, jnp.zeros((d, N_EXPERTS), BF16), zpad], axis=1)

    y = None
    for i in range(DEPTH):
        j = i // 2
        modp = mods[i - 1] if i else None
        if i % 2 == 0:
            lam_init = 0.8 - 0.6 * math.exp(-0.3 * i)
            w = _att_weights(att_w_in[j])
            outs = _project(s, y, modp, mods[i], norm_g[i, 0], w, cos, sin,
                            n_rope=2 * A_HEADS * LANES, n_bf16=w.shape[1])
            if i:
                s = outs[0]
            p = outs[-1]
            oa = _diff_attention(p, att_lambda[j], att_subln_g[j], lam_init)
            ob = _neighbourhood_attention(p, _na_bias_table(na_bias[j], seq // GRID_W))
            s, h2, logits = _out_project((oa, ob), att_w_out[j].astype(BF16), s, mods[i], norm_g[i, 1],
                                         rw1, rw2, gla=False)
        else:
            w, n_main = _gla_weights(gla_w_in[j])
            outs = _project(s, y, modp, mods[i], norm_g[i, 0], w, None, None, n_rope=0, n_bf16=n_main)
            s, p, lr = outs
            wg = jnp.zeros((2, LANES, C_HEADS * C_DK), F32)
            wg = wg.at[0, :C_RANK].set(gla_w_gate[j, 0]).at[1, C_RANK:2 * C_RANK].set(gla_w_gate[j, 1])
            o = _gla(p, lr, wg, gla_b_gate[j])
            s, h2, logits = _out_project((o, p), gla_w_out[j].astype(BF16), s, mods[i], norm_g[i, 1],
                                         rw1, rw2, gla=True, gn=gla_norm_g[j])
        y = _moe(h2.reshape(b * n, ROW_SEGS, LANES), logits.reshape(b * n, LANES), router_bias,
                 moe_w1[i].astype(BF16), moe_w3[i].astype(BF16), moe_w2[i].astype(BF16))
        y = y.reshape(b, n, ROW_SEGS, LANES)
    return _final(s, y, mods[DEPTH - 1], final_g)
```

```python
import functools
import math

import jax
import jax.numpy as jnp
import numpy as np
from jax import lax
from jax.experimental import pallas as pl
from jax.experimental.pallas import tpu as pltpu

F32 = jnp.float32
BF16 = jnp.bfloat16

D_MODEL = 1024
DEPTH = 4
GRID_W = 64
N_CTX = 256
A_HEADS = 4
A_DH = 64
A_DV = 128
B_HEADS = 8
B_DH = 64
NA_ROWS = 8
NA_COLS = 16
C_HEADS = 4
C_DK = 128
C_DV = 256
C_RANK = 16
C_TAU = 16.0
C_CHUNK = 64
GLA_BLOCK = 256
N_EXPERTS = 16
N_GROUPS = 4
PER_GROUP = 4
D_EXPERT = 512
ROPE_BASE = 10000.0
EPS = 1e-6

LANES = 128
ROW_TILE = 256
MOE_TILE = 256
N_CLASSES = N_GROUPS * 6
NA_QROWS = 8
NA_KROWS = NA_QROWS + NA_ROWS - 1
VMEM_LIMIT = 52 * 1024 * 1024
ROW_SEGS = D_MODEL // LANES

PAIR_LO = (0, 0, 0, 1, 1, 2)
PAIR_HI = (1, 2, 3, 2, 3, 3)


def _cparams(sem):
    return pltpu.CompilerParams(dimension_semantics=sem, vmem_limit_bytes=VMEM_LIMIT)


def _sigmoid(x):
    return 1.0 / (1.0 + jnp.exp(-x))


def _silu(x):
    return x * _sigmoid(x)


def _rms(x, g):
    return x * lax.rsqrt(jnp.mean(x * x, axis=-1, keepdims=True) + EPS) * g


def _dot(a, b):
    return jnp.dot(a, b, preferred_element_type=F32)


def _dot_nt(a, b):
    return lax.dot_general(a, b, (((1,), (1,)), ((), ())), preferred_element_type=F32)


def _dot_tn(a, b):
    return lax.dot_general(a, b, (((0,), (0,)), ((), ())), preferred_element_type=F32)


def _mod_kernel(c_ref, w_ref, b_ref, o_ref):
    a = _silu(c_ref[...]).astype(BF16)
    o_ref[0] = _dot(a, w_ref[0].astype(BF16)) + b_ref[0]


def _mod_vectors(cc, w_mod, b_mod):
    depth, d, n6 = w_mod.shape
    rows = cc.shape[0]
    tn = 1536
    return pl.pallas_call(
        _mod_kernel,
        out_shape=jax.ShapeDtypeStruct((depth, rows, n6), F32),
        grid=(depth, n6 // tn),
        in_specs=[
            pl.BlockSpec((rows, d), lambda i, j: (0, 0)),
            pl.BlockSpec((1, d, tn), lambda i, j: (i, 0, j)),
            pl.BlockSpec((1, 1, tn), lambda i, j: (i, 0, j)),
        ],
        out_specs=pl.BlockSpec((1, rows, tn), lambda i, j: (i, 0, j)),
        compiler_params=_cparams(("arbitrary", "arbitrary")),
        name="mod_vectors",
    )(cc, w_mod, b_mod.reshape(depth, 1, n6))


def _token_rows(ref):
    return jnp.concatenate([ref[0, :, sg, :] for sg in range(ROW_SEGS)], axis=-1)


def _proj_kernel(*refs, has_y, n_rope, n_bf16, col_chunk):
    it = iter(refs)
    s_ref = next(it)
    y_ref = next(it) if has_y else None
    modp_ref = next(it) if has_y else None
    mod_ref = next(it)
    ng_ref = next(it)
    w_ref = next(it)
    cos_ref = next(it) if n_rope else None
    sin_ref = next(it) if n_rope else None
    snew_ref = next(it) if has_y else None
    p_ref = next(it)
    lr_ref = next(it) if w_ref.shape[1] > n_bf16 else None

    x = s_ref[0]
    if has_y:
        x = x + modp_ref[0, 0, 5:6, :] * _token_rows(y_ref)
        snew_ref[0] = x
    h = _rms(x, ng_ref[...])
    h = h * (1.0 + mod_ref[0, 0, 1:2, :]) + mod_ref[0, 0, 0:1, :]
    hb = h.astype(BF16)
    n_out = w_ref.shape[1]
    for c0 in range(0, n_out, col_chunk):
        c1 = min(c0 + col_chunk, n_out)
        acc = _dot(hb, w_ref[:, c0:c1])
        for b0 in range(c0, c1, LANES):
            t = acc[:, b0 - c0:b0 - c0 + LANES]
            if b0 < n_rope:
                t = t * cos_ref[...] + pltpu.roll(t, LANES // 2, 1) * sin_ref[...]
            if b0 < n_bf16:
                p_ref[0, :, b0:b0 + LANES] = t.astype(BF16)
            else:
                lr_ref[0, :, b0 - n_bf16:b0 - n_bf16 + LANES] = t


def _project(s, y, modp, mod, ng, w, cos, sin, *, n_rope, n_bf16):
    b, n, d = s.shape
    n_out = w.shape[1]
    tm = ROW_TILE
    has_y = y is not None
    row_spec = pl.BlockSpec((1, tm, d), lambda i, j: (i, j, 0))
    mod_spec = pl.BlockSpec((1, 1, 6, d), lambda i, j: (i, jnp.minimum(j, 1), 0, 0))
    in_specs = [row_spec]
    args = [s]
    if has_y:
        in_specs += [pl.BlockSpec((1, tm, ROW_SEGS, LANES), lambda i, j: (i, j, 0, 0)), mod_spec]
        args += [y, modp]
    in_specs += [mod_spec, pl.BlockSpec((1, d), lambda i, j: (0, 0)),
                 pl.BlockSpec((d, n_out), lambda i, j: (0, 0))]
    args += [mod, ng.reshape(1, d), w]
    if n_rope:
        tab_spec = pl.BlockSpec((tm, LANES), lambda i, j: (j, 0))
        in_specs += [tab_spec, tab_spec]
        args += [cos, sin]
    out_shape, out_specs = [], []
    if has_y:
        out_shape.append(jax.ShapeDtypeStruct((b, n, d), F32))
        out_specs.append(row_spec)
    out_shape.append(jax.ShapeDtypeStruct((b, n, n_bf16), BF16))
    out_specs.append(pl.BlockSpec((1, tm, n_bf16), lambda i, j: (i, j, 0)))
    if n_out > n_bf16:
        out_shape.append(jax.ShapeDtypeStruct((b, n, n_out - n_bf16), F32))
        out_specs.append(pl.BlockSpec((1, tm, n_out - n_bf16), lambda i, j: (i, j, 0)))
    return pl.pallas_call(
        functools.partial(_proj_kernel, has_y=has_y, n_rope=n_rope, n_bf16=n_bf16, col_chunk=512),
        out_shape=out_shape,
        grid=(b, n // tm),
        in_specs=in_specs,
        out_specs=out_specs,
        compiler_params=_cparams(("arbitrary", "arbitrary")),
        name="norm_mod_project",
    )(*args)


def _softmax_rows(s):
    e = jnp.exp(s - jnp.max(s, axis=-1, keepdims=True))
    return e * (1.0 / jnp.sum(e, axis=-1, keepdims=True))


def _diff_attn_kernel(q_ref, k_ref, v_ref, lam_ref, g_ref, o_ref, *, lam_init):
    lp = lam_ref[...]
    lam = (jnp.exp(jnp.sum(lp[0:1] * lp[1:2], axis=-1, keepdims=True))
           - jnp.exp(jnp.sum(lp[2:3] * lp[3:4], axis=-1, keepdims=True)) + lam_init)
    q = q_ref[0]
    lane = lax.broadcasted_iota(jnp.int32, (1, LANES), 1)
    first_map = (lane // 32) % 2 == 0
    zero = jnp.zeros_like(q)
    q0 = jnp.where(first_map, q, zero)
    q1 = jnp.where(first_map, zero, q)

    def attend(n_keys):
        k = k_ref[0, 0:n_keys, :]
        v = v_ref[0, 0:n_keys, :]
        a = _softmax_rows(_dot_nt(q0, k)) - lam * _softmax_rows(_dot_nt(q1, k))
        o = _dot(a.astype(BF16), v)
        o_ref[0] = (_rms(o, g_ref[...]) * (1.0 - lam_init)).astype(BF16)

    qi = pl.program_id(2)

    @pl.when(qi == 0)
    def _():
        attend(N_CTX)

    @pl.when(qi > 0)
    def _():
        attend(k_ref.shape[1])


def _diff_attention(p, lam_p, subln_g, lam_init):
    b, n, _ = p.shape
    tq = ROW_TILE
    return pl.pallas_call(
        functools.partial(_diff_attn_kernel, lam_init=lam_init),
        out_shape=jax.ShapeDtypeStruct((b, n, A_HEADS * A_DV), BF16),
        grid=(b, A_HEADS, n // tq),
        in_specs=[
            pl.BlockSpec((1, tq, LANES), lambda i, h, j: (i, j, h)),
            pl.BlockSpec((1, n, LANES), lambda i, h, j: (i, 0, A_HEADS + h)),
            pl.BlockSpec((1, n, LANES), lambda i, h, j: (i, 0, 2 * A_HEADS + h)),
            pl.BlockSpec((4, A_DH), lambda i, h, j: (0, 0)),
            pl.BlockSpec((1, A_DV), lambda i, h, j: (0, 0)),
        ],
        out_specs=pl.BlockSpec((1, tq, LANES), lambda i, h, j: (i, j, h)),
        compiler_params=_cparams(("arbitrary", "arbitrary", "arbitrary")),
        name="diff_attention",
    )(p, p, p, lam_p, subln_g.reshape(1, A_DV))


def _split_heads(q):
    lane = lax.broadcasted_iota(jnp.int32, (1, LANES), 1)
    first_head = lane < B_DH
    zero = jnp.zeros_like(q)
    return first_head, (jnp.where(first_head, q, zero), jnp.where(first_head, zero, q))


def _na_ctx_kernel(q_ref, k_ref, v_ref, o_ref):
    first_head, qh = _split_heads(q_ref[0])
    kc = k_ref[0]
    vc = v_ref[0]
    outs = [_dot(_softmax_rows(_dot_nt(qh[h], kc)).astype(BF16), vc) for h in range(2)]
    o_ref[0] = jnp.where(first_head, outs[0], outs[1]).astype(BF16)


def _na_kernel(qa_ref, qb_ref, k_ref, v_ref, bm_ref, o_ref):
    blk = pl.program_id(2)
    first_head, qh = _split_heads(jnp.concatenate([qa_ref[0], qb_ref[0]], axis=0))
    kc = k_ref[0, 0:N_CTX, :]
    vc = v_ref[0, 0:N_CTX, :]
    rows = (k_ref.shape[1] - N_CTX) // GRID_W
    k_row0 = jnp.clip(blk * NA_QROWS - NA_ROWS // 2, 0, rows - NA_KROWS)
    start = pl.multiple_of(N_CTX + k_row0 * GRID_W, GRID_W)
    kw = k_ref[0, pl.ds(start, NA_KROWS * GRID_W), :]
    vw = v_ref[0, pl.ds(start, NA_KROWS * GRID_W), :]
    outs = []
    for h in range(2):
        s_loc = _dot_nt(qh[h], kw) + bm_ref[h, 0]
        s_ctx = _dot_nt(qh[h], kc)
        m = jnp.maximum(jnp.max(s_loc, axis=-1, keepdims=True), jnp.max(s_ctx, axis=-1, keepdims=True))
        e_loc = jnp.exp(s_loc - m)
        e_ctx = jnp.exp(s_ctx - m)
        den = jnp.sum(e_loc, axis=-1, keepdims=True) + jnp.sum(e_ctx, axis=-1, keepdims=True)
        outs.append((_dot(e_loc.astype(BF16), vw) + _dot(e_ctx.astype(BF16), vc)) * (1.0 / den))
    o_ref[0] = jnp.where(first_head, outs[0], outs[1]).astype(BF16)


def _na_bias_table(na_bias, rows):
    h = na_bias.shape[0]
    n_dr, n_dc = 2 * NA_ROWS - 1, 2 * NA_COLS - 1
    width = 2 * GRID_W
    left = GRID_W - NA_COLS
    u = jnp.pad(na_bias, ((0, 0), (0, 0), (left, width - left - n_dc)))
    skew = jnp.tile(u, (1, 1, GRID_W))[:, :, :GRID_W * (width - 1)].reshape(h, n_dr, GRID_W, width - 1)
    toeplitz = skew[:, :, :, GRID_W - 1:]
    margin = NA_KROWS - NA_ROWS
    by_col = jnp.pad(toeplitz.transpose(0, 2, 1, 3), ((0, 0), (0, 0), (margin, margin), (0, 0)))
    col = np.arange(GRID_W)
    col_start = np.clip(col - NA_COLS // 2, 0, GRID_W - NA_COLS)
    col_ok = (col[None, :] >= col_start[:, None]) & (col[None, :] < col_start[:, None] + NA_COLS)
    blocks = []
    for q_row0 in (0, NA_QROWS, rows - NA_QROWS):
        k_row0 = int(np.clip(q_row0 - NA_ROWS // 2, 0, rows - NA_KROWS))
        kr = k_row0 + np.arange(NA_KROWS)
        for rq in range(NA_QROWS):
            r = q_row0 + rq
            r0 = int(np.clip(r - NA_ROWS // 2, 0, rows - NA_ROWS))
            row_ok = (kr >= r0) & (kr < r0 + NA_ROWS)
            ok = col_ok[:, None, :] & row_ok[None, :, None]
            first = k_row0 - r + NA_ROWS - 1 + margin
            blocks.append(jnp.where(ok[None], by_col[:, :, first:first + NA_KROWS, :], -jnp.inf))
    table = jnp.stack(blocks, axis=1)
    return table.reshape(h, 3, NA_QROWS * GRID_W, NA_KROWS * GRID_W)


def _neighbourhood_attention(p, bm):
    b, n, _ = p.shape
    tq = NA_QROWS * GRID_W
    half = tq // 2
    n_blocks = (n - N_CTX) // tq
    ctx_blocks = N_CTX // half
    col0 = 3 * A_HEADS
    pairs = B_HEADS // 2
    width = B_HEADS * B_DH

    def pattern(j):
        return jnp.minimum(j, 1) + (j == n_blocks - 1).astype(jnp.int32)

    ob_x = pl.pallas_call(
        _na_kernel,
        out_shape=jax.ShapeDtypeStruct((b, n - N_CTX, width), BF16),
        grid=(b, pairs, n_blocks),
        in_specs=[
            pl.BlockSpec((1, half, LANES), lambda i, h, j: (i, ctx_blocks + 2 * j, col0 + h)),
            pl.BlockSpec((1, half, LANES), lambda i, h, j: (i, ctx_blocks + 2 * j + 1, col0 + h)),
            pl.BlockSpec((1, n, LANES), lambda i, h, j: (i, 0, col0 + pairs + h)),
            pl.BlockSpec((1, n, LANES), lambda i, h, j: (i, 0, col0 + 2 * pairs + h)),
            pl.BlockSpec((2, 1, tq, NA_KROWS * GRID_W), lambda i, h, j: (h, pattern(j), 0, 0)),
        ],
        out_specs=pl.BlockSpec((1, tq, LANES), lambda i, h, j: (i, j, h)),
        compiler_params=_cparams(("arbitrary", "arbitrary", "arbitrary")),
        name="neighbourhood_attention",
    )(p, p, p, p, bm)
    ob_c = pl.pallas_call(
        _na_ctx_kernel,
        out_shape=jax.ShapeDtypeStruct((b, N_CTX, width), BF16),
        grid=(b, pairs),
        in_specs=[
            pl.BlockSpec((1, N_CTX, LANES), lambda i, h: (i, 0, col0 + h)),
            pl.BlockSpec((1, N_CTX, LANES), lambda i, h: (i, 0, col0 + pairs + h)),
            pl.BlockSpec((1, N_CTX, LANES), lambda i, h: (i, 0, col0 + 2 * pairs + h)),
        ],
        out_specs=pl.BlockSpec((1, N_CTX, LANES), lambda i, h: (i, 0, h)),
        compiler_params=_cparams(("arbitrary", "arbitrary")),
        name="context_attention",
    )(p, p, p)
    return ob_x, ob_c


def _gla_kernel(q_ref, k_ref, v_ref, lr_ref, wg_ref, bg_ref, o_ref, ob_ref, g_ref):
    n = q_ref.shape[1]
    blk = GLA_BLOCK
    per_blk = blk // C_CHUNK
    n_blocks = n // blk
    ri = lax.broadcasted_iota(jnp.int32, (blk, blk), 0)
    ci = lax.broadcasted_iota(jnp.int32, (blk, blk), 1)
    same_chunk = (ri // C_CHUNK) == (ci // C_CHUNK)
    keeps = (same_chunk & (ci <= ri), same_chunk & (ci >= ri))
    tris = tuple(jnp.where(kp, 1.0, 0.0).astype(BF16) for kp in keeps)
    w_gate = jnp.concatenate([wg_ref[0], wg_ref[1]], axis=1).astype(BF16)
    b_gate = jnp.concatenate([bg_ref[0:1, :], bg_ref[1:2, :]], axis=1)
    for i in range(n_blocks):
        z = _dot(lr_ref[0, i * blk:(i + 1) * blk, :].astype(BF16), w_gate) + b_gate
        g_ref[i * blk:(i + 1) * blk, :] = (jnp.minimum(z, 0.0) - jnp.log(1.0 + jnp.exp(-jnp.abs(z)))) * (1.0 / C_TAU)

    def block(sb, direction, st):
        keep = keeps[direction]
        end_row = C_CHUNK - 1 if direction == 0 else 0
        mid_row = C_CHUNK // 2 - 1 if direction == 0 else C_CHUNK // 2
        r0 = sb * blk
        g = g_ref[r0:r0 + blk, direction * C_DK:(direction + 1) * C_DK]
        g_hi = g.astype(BF16)
        g_lo = (g - g_hi.astype(F32)).astype(BF16)
        gc2 = _dot(tris[direction], jnp.concatenate([g_hi, g_lo], axis=1))
        gc = (gc2[:, :C_DK] + gc2[:, C_DK:]).reshape(per_blk, C_CHUNK, C_DK)
        g_end = gc[:, end_row:end_row + 1, :]
        g_mid = gc[:, mid_row:mid_row + 1, :]
        q = (q_ref[0, r0:r0 + blk, :].astype(F32) * (C_DK ** -0.5)).reshape(per_blk, C_CHUNK, C_DK)
        k = k_ref[0, r0:r0 + blk, :].astype(F32).reshape(per_blk, C_CHUNK, C_DK)
        v = v_ref[0, r0:r0 + blk, :]
        q_in = (q * jnp.exp(gc)).astype(BF16).reshape(blk, C_DK)
        q_mid = (q * jnp.exp(gc - g_mid)).astype(BF16).reshape(blk, C_DK)
        k_mid = (k * jnp.exp(g_mid - gc)).astype(BF16).reshape(blk, C_DK)
        k_end = (k * jnp.exp(g_end - gc)).astype(BF16).reshape(blk, C_DK)
        a = jnp.where(keep, _dot_nt(q_mid, k_mid), 0.0)
        o_intra = _dot(a.astype(BF16), v)
        ends = jnp.concatenate([g_end.reshape(per_blk, C_DK), jnp.zeros((8 - per_blk, C_DK), F32)], axis=0)
        decay = jnp.transpose(jnp.exp(ends))
        o_inter = [None] * per_blk
        for c in (range(per_blk) if direction == 0 else reversed(range(per_blk))):
            rows = slice(c * C_CHUNK, (c + 1) * C_CHUNK)
            o_inter[c] = _dot(q_in[rows], st.astype(BF16))
            st = st * decay[:, c:c + 1] + _dot_tn(k_end[rows], v[rows])
        return o_intra + jnp.concatenate(o_inter, axis=0), st

    st_f = jnp.zeros((C_DK, C_DV), F32)
    st_b = jnp.zeros((C_DK, C_DV), F32)
    ctx_blocks = N_CTX // blk
    order_b = list(reversed(range(ctx_blocks))) + list(reversed(range(ctx_blocks, n_blocks)))
    for i in range(n_blocks):
        o, st_f = block(i, 0, st_f)
        o_ref[0, i * blk:(i + 1) * blk, :] = o
        sb = order_b[i]
        o, st_b = block(sb, 1, st_b)
        ob_ref[sb * blk:(sb + 1) * blk, :] = o
    o_ref[0] = o_ref[0] + ob_ref[...]


def _gla(p, lr, wg, bg):
    b, n, _ = p.shape
    return pl.pallas_call(
        _gla_kernel,
        out_shape=jax.ShapeDtypeStruct((b, n, C_HEADS * C_DV), F32),
        grid=(b, C_HEADS),
        in_specs=[
            pl.BlockSpec((1, n, C_DK), lambda i, h: (i, 0, h)),
            pl.BlockSpec((1, n, C_DK), lambda i, h: (i, 0, C_HEADS + h)),
            pl.BlockSpec((1, n, C_DV), lambda i, h: (i, 0, C_HEADS + h)),
            pl.BlockSpec((1, n, LANES), lambda i, h: (i, 0, 0)),
            pl.BlockSpec((2, LANES, C_DK), lambda i, h: (0, 0, h)),
            pl.BlockSpec((2, C_DK), lambda i, h: (0, h)),
        ],
        out_specs=pl.BlockSpec((1, n, C_DV), lambda i, h: (i, 0, h)),
        scratch_shapes=[pltpu.VMEM((n, C_DV), F32), pltpu.VMEM((n, 2 * C_DK), F32)],
        compiler_params=_cparams(("arbitrary", "arbitrary")),
        name="gla",
    )(p, p, p, lr, wg, bg)


def _out_kernel(*refs, gla):
    if gla:
        o_ref, gate_ref, gn_ref, w_ref, s_ref, mod_ref, ng_ref, rw1_ref, rw2_ref, snew_ref, h2_ref, lg_ref = refs
        o = o_ref[0]
        gate = gate_ref[0].astype(F32)
        parts = []
        for hd in range(C_HEADS):
            oh = _rms(o[:, hd * C_DV:(hd + 1) * C_DV], gn_ref[...])
            parts.append((oh * _silu(gate[:, hd * C_DV:(hd + 1) * C_DV])).astype(BF16))
        acc = _dot(jnp.concatenate(parts, axis=-1), w_ref[...])
    else:
        oa_ref, obx_ref, obc_ref, w_ref, s_ref, mod_ref, ng_ref, rw1_ref, rw2_ref, snew_ref, h2_ref, lg_ref = refs
        half = oa_ref.shape[2]
        ob = jnp.where(pl.program_id(1) == 0, obc_ref[0], obx_ref[0])
        acc = _dot(oa_ref[0], w_ref[0:half, :]) + _dot(ob, w_ref[half:, :])
    x = s_ref[0] + mod_ref[0, 0, 2:3, :] * acc
    snew_ref[0] = x
    h2 = _rms(x, ng_ref[...]) * (1.0 + mod_ref[0, 0, 4:5, :]) + mod_ref[0, 0, 3:4, :]
    for sg in range(ROW_SEGS):
        h2_ref[0, :, sg, :] = h2[:, sg * LANES:(sg + 1) * LANES]
    hi = h2.astype(BF16)
    lo = (h2 - hi.astype(F32)).astype(BF16)
    t = _dot(hi, rw1_ref[...])
    lg_ref[0] = t + pltpu.roll(t, LANES - N_EXPERTS, 1) + _dot(lo, rw2_ref[...])


def _out_project(mix, w_out, s, mod, ng, rw1, rw2, *, gla, gn=None):
    b, n, d = s.shape
    tm = ROW_TILE
    row_spec = pl.BlockSpec((1, tm, d), lambda i, j: (i, j, 0))
    const2 = lambda i, j: (0, 0)
    if gla:
        o, p = mix
        in_specs = [row_spec, pl.BlockSpec((1, tm, d), lambda i, j: (i, j, 2)),
                    pl.BlockSpec((1, C_DV), const2)]
        args = [o, p, gn.reshape(1, C_DV)]
    else:
        oa, ob_x, ob_c = mix
        half = oa.shape[2]
        in_specs = [pl.BlockSpec((1, tm, half), lambda i, j: (i, j, 0)),
                    pl.BlockSpec((1, tm, half), lambda i, j: (i, jnp.maximum(j - 1, 0), 0)),
                    pl.BlockSpec((1, tm, half), lambda i, j: (i, 0, 0))]
        args = [oa, ob_x, ob_c]
    in_specs += [pl.BlockSpec((d, d), const2), row_spec,
                 pl.BlockSpec((1, 1, 6, d), lambda i, j: (i, jnp.minimum(j, 1), 0, 0)),
                 pl.BlockSpec((1, d), const2), pl.BlockSpec((d, LANES), const2), pl.BlockSpec((d, LANES), const2)]
    args += [w_out, s, mod, ng.reshape(1, d), rw1, rw2]
    return pl.pallas_call(
        functools.partial(_out_kernel, gla=gla),
        out_shape=[jax.ShapeDtypeStruct((b, n, d), F32), jax.ShapeDtypeStruct((b, n, ROW_SEGS, LANES), F32),
                   jax.ShapeDtypeStruct((b, n, LANES), F32)],
        grid=(b, n // tm),
        in_specs=in_specs,
        out_specs=[row_spec, pl.BlockSpec((1, tm, ROW_SEGS, LANES), lambda i, j: (i, j, 0, 0)),
                   pl.BlockSpec((1, tm, LANES), lambda i, j: (i, j, 0))],
        compiler_params=_cparams(("arbitrary", "arbitrary")),
        name="out_project",
    )(*args)


def _route_kernel(bias_ref, lg_ref, cls_ref, wlo_ref, whi_ref):
    score = [_sigmoid(lg_ref[e]) for e in range(N_EXPERTS)]
    sel = [score[e] + bias_ref[e] for e in range(N_EXPERTS)]
    grp_score = []
    for g in range(N_GROUPS):
        v = sel[g * PER_GROUP:(g + 1) * PER_GROUP]
        best = v[0] + v[1]
        for a in range(PER_GROUP):
            for c in range(a + 1, PER_GROUP):
                if (a, c) != (0, 1):
                    best = jnp.maximum(best, v[a] + v[c])
        grp_score.append(best)
    grp = jnp.zeros(grp_score[0].shape, jnp.int32)
    best = grp_score[0]
    for g in range(1, N_GROUPS):
        upd = grp_score[g] > best
        best = jnp.where(upd, grp_score[g], best)
        grp = jnp.where(upd, g, grp)

    def pick(vals, j):
        out = vals[j]
        for g in range(1, N_GROUPS):
            out = jnp.where(grp == g, vals[g * PER_GROUP + j], out)
        return out

    v = [pick(sel, j) for j in range(PER_GROUP)]
    sc = [pick(score, j) for j in range(PER_GROUP)]
    one = jnp.ones(grp.shape, jnp.int32)
    zero = jnp.zeros(grp.shape, jnp.int32)
    chosen = []
    for j in range(PER_GROUP):
        rank = zero
        for m in range(PER_GROUP):
            if m == j:
                continue
            ahead = (v[m] >= v[j]) if m < j else (v[m] > v[j])
            rank = rank + jnp.where(ahead, one, zero)
        chosen.append(rank < 2)
    code = zero
    for j in range(PER_GROUP):
        code = code + jnp.where(chosen[j], one * (1 << j), zero)
    pair = zero
    for idx in range(6):
        pair = jnp.where(code == (1 << PAIR_LO[idx]) + (1 << PAIR_HI[idx]), idx, pair)
    s_lo = jnp.where(chosen[0], sc[0], jnp.where(chosen[1], sc[1], sc[2]))
    s_hi = jnp.where(chosen[3], sc[3], jnp.where(chosen[2], sc[2], sc[1]))
    den = s_lo + s_hi
    cls_ref[...] = grp * 6 + pair
    wlo_ref[...] = s_lo / den
    whi_ref[...] = s_hi / den


def _route(logits_t, router_bias):
    _, r, _ = logits_t.shape
    full = pl.BlockSpec((r, LANES), lambda i: (0, 0))
    return pl.pallas_call(
        _route_kernel,
        out_shape=[jax.ShapeDtypeStruct((r, LANES), jnp.int32), jax.ShapeDtypeStruct((r, LANES), F32),
                   jax.ShapeDtypeStruct((r, LANES), F32)],
        grid=(1,),
        in_specs=[pl.BlockSpec(memory_space=pltpu.SMEM),
                  pl.BlockSpec((N_EXPERTS, r, LANES), lambda i: (0, 0, 0))],
        out_specs=[full, full, full],
        compiler_params=_cparams(("arbitrary",)),
        name="route",
    )(router_bias, logits_t)


def _moe_kernel(tok_ref, lo_ref, hi_ref, nv_ref, nt_ref, h2_ref, wr_ref, w1l_ref, w1h_ref, w3l_ref, w3h_ref,
                w2l_ref, w2h_ref, y_ref, xbuf, ybuf, sem_in, sem_out):
    tm = xbuf.shape[1]
    j = pl.program_id(0)
    n_tiles = nt_ref[0]
    slot = j % 2

    def gather(tile, to_slot):
        def body(r, carry):
            pltpu.make_async_copy(h2_ref.at[tok_ref[tile * tm + r]], xbuf.at[to_slot, r], sem_in.at[to_slot]).start()
            return carry
        lax.fori_loop(0, tm, body, 0, unroll=8)

    def wait_scatter(tile, from_slot):
        n = nv_ref[tile]

        @pl.when(n == tm)
        def _():
            pltpu.make_async_copy(ybuf.at[from_slot], y_ref.at[pl.ds(0, tm)], sem_out.at[from_slot]).wait()

        @pl.when(n < tm)
        def _():
            def body(r, carry):
                pltpu.make_async_copy(ybuf.at[from_slot, 0], y_ref.at[0], sem_out.at[from_slot]).wait()
                return carry
            lax.fori_loop(0, n, body, 0)

    @pl.when(j == 0)
    def _():
        gather(0, 0)

    @pl.when(j + 1 < n_tiles)
    def _():
        gather(j + 1, 1 - slot)

    @pl.when(j < n_tiles)
    def _():
        pltpu.make_async_copy(h2_ref.at[pl.ds(0, tm)], xbuf.at[slot], sem_in.at[slot]).wait()
        xb = xbuf.at[slot]
        x = jnp.concatenate([xb[:, sg, :] for sg in range(ROW_SEGS)], axis=-1).astype(BF16)
        wr = wr_ref[...]
        he_lo = _silu(_dot(x, w1l_ref[0])) * _dot(x, w3l_ref[0]) * wr[:, 0:1]
        he_hi = _silu(_dot(x, w1h_ref[0])) * _dot(x, w3h_ref[0]) * wr[:, 1:2]
        y = _dot(he_lo.astype(BF16), w2l_ref[0]) + _dot(he_hi.astype(BF16), w2h_ref[0])

        @pl.when(j >= 2)
        def _():
            wait_scatter(j - 2, slot)

        yb = ybuf.at[slot]
        for sg in range(ROW_SEGS):
            yb[:, sg, :] = y[:, sg * LANES:(sg + 1) * LANES]

        def body(r, carry):
            pltpu.make_async_copy(ybuf.at[slot, r], y_ref.at[tok_ref[j * tm + r]], sem_out.at[slot]).start()
            return carry
        lax.fori_loop(0, nv_ref[j], body, 0)

    @pl.when(j == n_tiles - 1)
    def _():
        @pl.when(j >= 1)
        def _():
            wait_scatter(j - 1, 1 - slot)
        wait_scatter(j, slot)


def _moe_experts(h2, tok, wrow, tile_lo, tile_hi, n_valid, n_tiles, w1, w3, w2):
    t = h2.shape[0]
    tp = tok.shape[0]
    tm = MOE_TILE
    d, de = w1.shape[1], w1.shape[2]

    def row_map(j, tok, lo, hi, nv, nt):
        return (jnp.minimum(j, nt[0] - 1), 0)

    def lo_map(j, tok, lo, hi, nv, nt):
        return (lo[jnp.minimum(j, nt[0] - 1)], 0, 0)

    def hi_map(j, tok, lo, hi, nv, nt):
        return (hi[jnp.minimum(j, nt[0] - 1)], 0, 0)

    up = (1, d, de)
    down = (1, de, d)
    return pl.pallas_call(
        _moe_kernel,
        out_shape=jax.ShapeDtypeStruct((t, ROW_SEGS, LANES), F32),
        grid_spec=pltpu.PrefetchScalarGridSpec(
            num_scalar_prefetch=5,
            grid=(tp // tm,),
            in_specs=[pl.BlockSpec(memory_space=pl.ANY), pl.BlockSpec((tm, 2), row_map),
                      pl.BlockSpec(up, lo_map), pl.BlockSpec(up, hi_map),
                      pl.BlockSpec(up, lo_map), pl.BlockSpec(up, hi_map),
                      pl.BlockSpec(down, lo_map), pl.BlockSpec(down, hi_map)],
            out_specs=pl.BlockSpec(memory_space=pl.ANY),
            scratch_shapes=[pltpu.VMEM((2, tm, ROW_SEGS, LANES), F32), pltpu.VMEM((2, tm, ROW_SEGS, LANES), F32),
                            pltpu.SemaphoreType.DMA((2,)), pltpu.SemaphoreType.DMA((2,))],
        ),
        compiler_params=pltpu.CompilerParams(dimension_semantics=("arbitrary",), vmem_limit_bytes=VMEM_LIMIT,
                                             has_side_effects=True),
        name="moe_experts",
    )(tok, tile_lo, tile_hi, n_valid, n_tiles, h2, wrow, w1, w1, w3, w3, w2, w2)


def _moe(h2, logits, router_bias, w1, w3, w2):
    t = h2.shape[0]
    tm = MOE_TILE
    n_tiles_max = t // tm + N_CLASSES
    tp = n_tiles_max * tm
    lg_t = logits[:, :N_EXPERTS].T.reshape(N_EXPERTS, t // LANES, LANES)
    cls, wlo, whi = _route(lg_t, router_bias)
    cls, wlo, whi = cls.reshape(t), wlo.reshape(t), whi.reshape(t)
    classes = jnp.arange(N_CLASSES, dtype=jnp.int32)
    onehot = (cls[:, None] == classes[None, :]).astype(jnp.int32)
    csum = jnp.cumsum(onehot, axis=0)
    rank = jnp.sum(csum * onehot, axis=1) - 1
    counts = csum[-1]
    tiles_per = (counts + tm - 1) // tm
    tile_end = jnp.cumsum(tiles_per)
    tile_start = tile_end - tiles_per
    dest = jnp.sum(onehot * (tile_start * tm)[None, :], axis=1) + rank
    tiles = jnp.arange(n_tiles_max, dtype=jnp.int32)
    tile_cls = jnp.minimum(jnp.sum((tile_end[None, :] <= tiles[:, None]).astype(jnp.int32), axis=1), N_CLASSES - 1)
    tile_onehot = (tile_cls[:, None] == classes[None, :]).astype(jnp.int32)
    local = tiles - jnp.sum(tile_onehot * tile_start[None, :], axis=1)
    n_valid = jnp.clip(jnp.sum(tile_onehot * counts[None, :], axis=1) - local * tm, 0, tm)
    pair = tile_cls % 6
    base = (tile_cls // 6) * PER_GROUP
    pair_onehot = (pair[:, None] == jnp.arange(6, dtype=jnp.int32)[None, :]).astype(jnp.int32)
    tile_lo = base + jnp.sum(pair_onehot * jnp.asarray(PAIR_LO, jnp.int32)[None, :], axis=1)
    tile_hi = base + jnp.sum(pair_onehot * jnp.asarray(PAIR_HI, jnp.int32)[None, :], axis=1)
    per_token = jnp.stack([jnp.arange(t, dtype=jnp.int32), lax.bitcast_convert_type(wlo, jnp.int32),
                           lax.bitcast_convert_type(whi, jnp.int32)], axis=1)
    per_row = jnp.zeros((tp, 3), jnp.int32).at[dest].set(per_token)
    tok = per_row[:, 0]
    wrow = lax.bitcast_convert_type(per_row[:, 1:3], F32)
    return _moe_experts(h2, tok, wrow, tile_lo, tile_hi, n_valid, tile_end[-1:], w1, w3, w2)


def _final_kernel(s_ref, y_ref, mod_ref, g_ref, o_ref):
    x = s_ref[0] + mod_ref[0, 0, 5:6, :] * _token_rows(y_ref)
    o_ref[0] = _rms(x, g_ref[...])


def _final(s, y, mod, final_g):
    b, n, d = s.shape
    tm = ROW_TILE
    skip = N_CTX // tm
    row_spec = pl.BlockSpec((1, tm, d), lambda i, j: (i, j + skip, 0))
    return pl.pallas_call(
        _final_kernel,
        out_shape=jax.ShapeDtypeStruct((b, n - N_CTX, d), F32),
        grid=(b, (n - N_CTX) // tm),
        in_specs=[row_spec, pl.BlockSpec((1, tm, ROW_SEGS, LANES), lambda i, j: (i, j + skip, 0, 0)),
                  pl.BlockSpec((1, 1, 6, d), lambda i, j: (i, 1, 0, 0)),
                  pl.BlockSpec((1, d), lambda i, j: (0, 0))],
        out_specs=pl.BlockSpec((1, tm, d), lambda i, j: (i, j, 0)),
        compiler_params=_cparams(("arbitrary", "arbitrary")),
        name="final_norm",
    )(s, y, mod, final_g.reshape(1, d))


def _rope_tables(seq):
    pos = jnp.arange(seq)
    row_pos, col_pos = pos // GRID_W, pos % GRID_W
    n = A_DH // 2
    inv = ROPE_BASE ** (-jnp.arange(0, n, 2, dtype=F32) / n)
    ang_r = row_pos.astype(F32)[:, None] * inv[None, :]
    ang_c = col_pos.astype(F32)[:, None] * inv[None, :]
    ang = jnp.concatenate([ang_r, ang_c], axis=-1)
    cos = jnp.tile(jnp.cos(ang), (1, 4))
    sin = jnp.tile(jnp.sin(ang), (1, 4))
    sin = jnp.concatenate([-sin[:, :LANES // 2], sin[:, LANES // 2:]], axis=-1)
    cos = jnp.concatenate([jnp.ones((N_CTX, LANES), F32), cos], axis=0)
    sin = jnp.concatenate([jnp.zeros((N_CTX, LANES), F32), sin], axis=0)
    return cos, sin


def _interleave_maps(w):
    d = w.shape[0]
    w = w.reshape(d, A_HEADS, 2, 2, 2, A_DH // 4)
    return w.transpose(0, 1, 4, 2, 3, 5).reshape(d, A_HEADS * LANES)


def _att_weights(w_in):
    a_qk = A_HEADS * 2 * A_DH
    a_v = A_HEADS * A_DV
    b_w = B_HEADS * B_DH
    qa = _interleave_maps(w_in[:, :a_qk]) * (A_DH ** -0.5)
    ka = _interleave_maps(w_in[:, a_qk:2 * a_qk])
    va = w_in[:, 2 * a_qk:2 * a_qk + a_v]
    o = 2 * a_qk + a_v
    qb = w_in[:, o:o + b_w] * (B_DH ** -0.5)
    rest = w_in[:, o + b_w:]
    return jnp.concatenate([qa, ka, va, qb, rest], axis=1).astype(BF16)


def _gla_weights(w_in):
    d = w_in.shape[0]
    n_main = 2 * C_HEADS * C_DK + 2 * C_HEADS * C_DV
    pad = jnp.zeros((d, LANES - 2 * C_RANK), w_in.dtype)
    return jnp.concatenate([w_in, pad], axis=1).astype(BF16), n_main


def kernel(x, c, ctx, c_ctx, w_mod, b_mod, norm_g, final_g, att_w_in, att_w_out, att_lambda, att_subln_g, na_bias,
           gla_w_in, gla_w_gate, gla_b_gate, gla_norm_g, gla_w_out, router_w, router_bias, moe_w1, moe_w3, moe_w2):
    b, seq, d = x.shape
    n = N_CTX + seq
    s = jnp.concatenate([ctx, x], axis=1)

    rows = b + 1
    rows_pad = -(-rows // 8) * 8
    cc = jnp.concatenate([c, c_ctx[None, :], jnp.zeros((rows_pad - rows, d), F32)], axis=0)
    mod_all = _mod_vectors(cc, w_mod, b_mod)
    mod_x = mod_all[:, :b].reshape(DEPTH, b, 1, 6, d)
    mod_c = jnp.broadcast_to(mod_all[:, b].reshape(DEPTH, 1, 1, 6, d), (DEPTH, b, 1, 6, d))
    mods = jnp.concatenate([mod_c, mod_x], axis=2)

    cos, sin = _rope_tables(seq)
    rw_hi = router_w.astype(BF16)
    rw_lo = (router_w - rw_hi.astype(F32)).astype(BF16)
    zpad = jnp.zeros((d, LANES - 2 * N_EXPERTS), BF16)
    rw1 = jnp.concatenate([rw_hi, rw_lo, zpad], axis=1)
    rw2 = jnp.concatenate([rw_hi, jnp.zeros((d, N_EXPERTS), BF16), zpad], axis=1)

    y = None
    for i in range(DEPTH):
        j = i // 2
        modp = mods[i - 1] if i else None
        if i % 2 == 0:
            lam_init = 0.8 - 0.6 * math.exp(-0.3 * i)
            w = _att_weights(att_w_in[j])
            outs = _project(s, y, modp, mods[i], norm_g[i, 0], w, cos, sin,
                            n_rope=2 * A_HEADS * LANES, n_bf16=w.shape[1])
            if i:
                s = outs[0]
            p = outs[-1]
            oa = _diff_attention(p, att_lambda[j], att_subln_g[j], lam_init)
            ob_x, ob_c = _neighbourhood_attention(p, _na_bias_table(na_bias[j], seq // GRID_W))
            s, h2, logits = _out_project((oa, ob_x, ob_c), att_w_out[j].astype(BF16), s, mods[i], norm_g[i, 1],
                                         rw1, rw2, gla=False)
        else:
            w, n_main = _gla_weights(gla_w_in[j])
            outs = _project(s, y, modp, mods[i], norm_g[i, 0], w, None, None, n_rope=0, n_bf16=n_main)
            s, p, lr = outs
            wg = jnp.zeros((2, LANES, C_HEADS * C_DK), F32)
            wg = wg.at[0, :C_RANK].set(gla_w_gate[j, 0]).at[1, C_RANK:2 * C_RANK].set(gla_w_gate[j, 1])
            o = _gla(p, lr, wg, gla_b_gate[j])
            s, h2, logits = _out_project((o, p), gla_w_out[j].astype(BF16), s, mods[i], norm_g[i, 1],
                                         rw1, rw2, gla=True, gn=gla_norm_g[j])
        y = _moe(h2.reshape(b * n, ROW_SEGS, LANES), logits.reshape(b * n, LANES), router_bias,
                 moe_w1[i].astype(BF16), moe_w3[i].astype(BF16), moe_w2[i].astype(BF16))
        y = y.reshape(b, n, ROW_SEGS, LANES)
    return _final(s, y, mods[DEPTH - 1], final_g)
```

```python
import functools
import math

import jax
import jax.numpy as jnp
import numpy as np
from jax import lax
from jax.experimental import pallas as pl
from jax.experimental.pallas import tpu as pltpu
from jax.experimental.pallas import tpu_sc as plsc

F32 = jnp.float32
BF16 = jnp.bfloat16

D_MODEL = 1024
DEPTH = 4
GRID_W = 64
N_CTX = 256
A_HEADS = 4
A_DH = 64
A_DV = 128
B_HEADS = 8
B_DH = 64
NA_ROWS = 8
NA_COLS = 16
C_HEADS = 4
C_DK = 128
C_DV = 256
C_RANK = 16
C_TAU = 16.0
C_CHUNK = 64
GLA_BLOCK = 256
N_EXPERTS = 16
N_GROUPS = 4
PER_GROUP = 4
D_EXPERT = 512
ROPE_BASE = 10000.0
EPS = 1e-6

LANES = 128
ROW_TILE = 256
MOE_TILE = 256
SC_CORES = 2
SC_SUBCORES = 16
SC_WORKERS = SC_CORES * SC_SUBCORES
SC_GATHER_ROWS = 64
N_CLASSES = N_GROUPS * 6
NA_QROWS = 8
NA_KROWS = NA_QROWS + NA_ROWS - 1
VMEM_LIMIT = 52 * 1024 * 1024
ROW_SEGS = D_MODEL // LANES

PAIR_LO = (0, 0, 0, 1, 1, 2)
PAIR_HI = (1, 2, 3, 2, 3, 3)


def _cparams(sem):
    return pltpu.CompilerParams(dimension_semantics=sem, vmem_limit_bytes=VMEM_LIMIT)


def _sigmoid(x):
    return 1.0 / (1.0 + jnp.exp(-x))


def _silu(x):
    return x * _sigmoid(x)


def _rms(x, g):
    return x * lax.rsqrt(jnp.mean(x * x, axis=-1, keepdims=True) + EPS) * g


def _dot(a, b):
    return jnp.dot(a, b, preferred_element_type=F32)


def _dot_nt(a, b):
    return lax.dot_general(a, b, (((1,), (1,)), ((), ())), preferred_element_type=F32)


def _dot_tn(a, b):
    return lax.dot_general(a, b, (((0,), (0,)), ((), ())), preferred_element_type=F32)


def _mod_kernel(c_ref, w_ref, b_ref, o_ref):
    a = _silu(c_ref[...]).astype(BF16)
    o_ref[0] = _dot(a, w_ref[0].astype(BF16)) + b_ref[0]


def _mod_vectors(cc, w_mod, b_mod):
    depth, d, n6 = w_mod.shape
    rows = cc.shape[0]
    tn = 1536
    return pl.pallas_call(
        _mod_kernel,
        out_shape=jax.ShapeDtypeStruct((depth, rows, n6), F32),
        grid=(depth, n6 // tn),
        in_specs=[
            pl.BlockSpec((rows, d), lambda i, j: (0, 0)),
            pl.BlockSpec((1, d, tn), lambda i, j: (i, 0, j)),
            pl.BlockSpec((1, 1, tn), lambda i, j: (i, 0, j)),
        ],
        out_specs=pl.BlockSpec((1, rows, tn), lambda i, j: (i, 0, j)),
        compiler_params=_cparams(("arbitrary", "arbitrary")),
        name="mod_vectors",
    )(cc, w_mod, b_mod.reshape(depth, 1, n6))


def _token_rows(ref):
    return jnp.concatenate([ref[0, :, sg, :] for sg in range(ROW_SEGS)], axis=-1)


def _proj_kernel(*refs, has_y, n_rope, n_bf16, col_chunk):
    it = iter(refs)
    s_ref = next(it)
    y_ref = next(it) if has_y else None
    modp_ref = next(it) if has_y else None
    mod_ref = next(it)
    ng_ref = next(it)
    w_ref = next(it)
    cos_ref = next(it) if n_rope else None
    sin_ref = next(it) if n_rope else None
    snew_ref = next(it) if has_y else None
    p_ref = next(it)
    lr_ref = next(it) if w_ref.shape[1] > n_bf16 else None

    x = s_ref[0]
    if has_y:
        x = x + modp_ref[0, 0, 5:6, :] * _token_rows(y_ref)
        snew_ref[0] = x
    h = _rms(x, ng_ref[...])
    h = h * (1.0 + mod_ref[0, 0, 1:2, :]) + mod_ref[0, 0, 0:1, :]
    hb = h.astype(BF16)
    n_out = w_ref.shape[1]
    for c0 in range(0, n_out, col_chunk):
        c1 = min(c0 + col_chunk, n_out)
        acc = _dot(hb, w_ref[:, c0:c1])
        for b0 in range(c0, c1, LANES):
            t = acc[:, b0 - c0:b0 - c0 + LANES]
            if b0 < n_rope:
                t = t * cos_ref[...] + pltpu.roll(t, LANES // 2, 1) * sin_ref[...]
            if b0 < n_bf16:
                p_ref[0, :, b0:b0 + LANES] = t.astype(BF16)
            else:
                lr_ref[0, :, b0 - n_bf16:b0 - n_bf16 + LANES] = t


def _project(s, y, modp, mod, ng, w, cos, sin, *, n_rope, n_bf16):
    b, n, d = s.shape
    n_out = w.shape[1]
    tm = ROW_TILE
    has_y = y is not None
    row_spec = pl.BlockSpec((1, tm, d), lambda i, j: (i, j, 0))
    mod_spec = pl.BlockSpec((1, 1, 6, d), lambda i, j: (i, jnp.minimum(j, 1), 0, 0))
    in_specs = [row_spec]
    args = [s]
    if has_y:
        in_specs += [pl.BlockSpec((1, tm, ROW_SEGS, LANES), lambda i, j: (i, j, 0, 0)), mod_spec]
        args += [y, modp]
    in_specs += [mod_spec, pl.BlockSpec((1, d), lambda i, j: (0, 0)),
                 pl.BlockSpec((d, n_out), lambda i, j: (0, 0))]
    args += [mod, ng.reshape(1, d), w]
    if n_rope:
        tab_spec = pl.BlockSpec((tm, LANES), lambda i, j: (j, 0))
        in_specs += [tab_spec, tab_spec]
        args += [cos, sin]
    out_shape, out_specs = [], []
    if has_y:
        out_shape.append(jax.ShapeDtypeStruct((b, n, d), F32))
        out_specs.append(row_spec)
    out_shape.append(jax.ShapeDtypeStruct((b, n, n_bf16), BF16))
    out_specs.append(pl.BlockSpec((1, tm, n_bf16), lambda i, j: (i, j, 0)))
    if n_out > n_bf16:
        out_shape.append(jax.ShapeDtypeStruct((b, n, n_out - n_bf16), F32))
        out_specs.append(pl.BlockSpec((1, tm, n_out - n_bf16), lambda i, j: (i, j, 0)))
    return pl.pallas_call(
        functools.partial(_proj_kernel, has_y=has_y, n_rope=n_rope, n_bf16=n_bf16, col_chunk=512),
        out_shape=out_shape,
        grid=(b, n // tm),
        in_specs=in_specs,
        out_specs=out_specs,
        compiler_params=_cparams(("arbitrary", "arbitrary")),
        name="norm_mod_project",
    )(*args)


def _softmax_rows(s):
    e = jnp.exp(s - jnp.max(s, axis=-1, keepdims=True))
    return e * (1.0 / jnp.sum(e, axis=-1, keepdims=True))


def _diff_attn_kernel(q_ref, k_ref, v_ref, lam_ref, g_ref, o_ref, *, lam_init):
    lp = lam_ref[...]
    lam = (jnp.exp(jnp.sum(lp[0:1] * lp[1:2], axis=-1, keepdims=True))
           - jnp.exp(jnp.sum(lp[2:3] * lp[3:4], axis=-1, keepdims=True)) + lam_init)
    q = q_ref[0]
    lane = lax.broadcasted_iota(jnp.int32, (1, LANES), 1)
    first_map = (lane // 32) % 2 == 0
    zero = jnp.zeros_like(q)
    q0 = jnp.where(first_map, q, zero)
    q1 = jnp.where(first_map, zero, q)

    def attend(n_keys):
        k = k_ref[0, 0:n_keys, :]
        v = v_ref[0, 0:n_keys, :]
        a = _softmax_rows(_dot_nt(q0, k)) - lam * _softmax_rows(_dot_nt(q1, k))
        o = _dot(a.astype(BF16), v)
        o_ref[0] = (_rms(o, g_ref[...]) * (1.0 - lam_init)).astype(BF16)

    qi = pl.program_id(2)

    @pl.when(qi == 0)
    def _():
        attend(N_CTX)

    @pl.when(qi > 0)
    def _():
        attend(k_ref.shape[1])


def _diff_attention(p, lam_p, subln_g, lam_init):
    b, n, _ = p.shape
    tq = ROW_TILE
    return pl.pallas_call(
        functools.partial(_diff_attn_kernel, lam_init=lam_init),
        out_shape=jax.ShapeDtypeStruct((b, n, A_HEADS * A_DV), BF16),
        grid=(b, A_HEADS, n // tq),
        in_specs=[
            pl.BlockSpec((1, tq, LANES), lambda i, h, j: (i, j, h)),
            pl.BlockSpec((1, n, LANES), lambda i, h, j: (i, 0, A_HEADS + h)),
            pl.BlockSpec((1, n, LANES), lambda i, h, j: (i, 0, 2 * A_HEADS + h)),
            pl.BlockSpec((4, A_DH), lambda i, h, j: (0, 0)),
            pl.BlockSpec((1, A_DV), lambda i, h, j: (0, 0)),
        ],
        out_specs=pl.BlockSpec((1, tq, LANES), lambda i, h, j: (i, j, h)),
        compiler_params=_cparams(("arbitrary", "arbitrary", "arbitrary")),
        name="diff_attention",
    )(p, p, p, lam_p, subln_g.reshape(1, A_DV))


def _split_heads(q):
    lane = lax.broadcasted_iota(jnp.int32, (1, LANES), 1)
    first_head = lane < B_DH
    zero = jnp.zeros_like(q)
    return first_head, (jnp.where(first_head, q, zero), jnp.where(first_head, zero, q))


def _na_ctx_kernel(q_ref, k_ref, v_ref, o_ref):
    first_head, qh = _split_heads(q_ref[0])
    kc = k_ref[0]
    vc = v_ref[0]
    outs = [_dot(_softmax_rows(_dot_nt(qh[h], kc)).astype(BF16), vc) for h in range(2)]
    o_ref[0] = jnp.where(first_head, outs[0], outs[1]).astype(BF16)


def _na_kernel(qa_ref, qb_ref, k_ref, v_ref, bm_ref, o_ref):
    blk = pl.program_id(2)
    first_head, qh = _split_heads(jnp.concatenate([qa_ref[0], qb_ref[0]], axis=0))
    kc = k_ref[0, 0:N_CTX, :]
    vc = v_ref[0, 0:N_CTX, :]
    rows = (k_ref.shape[1] - N_CTX) // GRID_W
    k_row0 = jnp.clip(blk * NA_QROWS - NA_ROWS // 2, 0, rows - NA_KROWS)
    start = pl.multiple_of(N_CTX + k_row0 * GRID_W, GRID_W)
    kw = k_ref[0, pl.ds(start, NA_KROWS * GRID_W), :]
    vw = v_ref[0, pl.ds(start, NA_KROWS * GRID_W), :]
    outs = []
    for h in range(2):
        s_loc = _dot_nt(qh[h], kw) + bm_ref[h, 0]
        s_ctx = _dot_nt(qh[h], kc)
        m = jnp.maximum(jnp.max(s_loc, axis=-1, keepdims=True), jnp.max(s_ctx, axis=-1, keepdims=True))
        e_loc = jnp.exp(s_loc - m)
        e_ctx = jnp.exp(s_ctx - m)
        den = jnp.sum(e_loc, axis=-1, keepdims=True) + jnp.sum(e_ctx, axis=-1, keepdims=True)
        outs.append((_dot(e_loc.astype(BF16), vw) + _dot(e_ctx.astype(BF16), vc)) * (1.0 / den))
    o_ref[0] = jnp.where(first_head, outs[0], outs[1]).astype(BF16)


def _na_bias_table(na_bias, rows):
    h = na_bias.shape[0]
    n_dr, n_dc = 2 * NA_ROWS - 1, 2 * NA_COLS - 1
    width = 2 * GRID_W
    left = GRID_W - NA_COLS
    u = jnp.pad(na_bias, ((0, 0), (0, 0), (left, width - left - n_dc)))
    skew = jnp.tile(u, (1, 1, GRID_W))[:, :, :GRID_W * (width - 1)].reshape(h, n_dr, GRID_W, width - 1)
    toeplitz = skew[:, :, :, GRID_W - 1:]
    margin = NA_KROWS - NA_ROWS
    by_col = jnp.pad(toeplitz.transpose(0, 2, 1, 3), ((0, 0), (0, 0), (margin, margin), (0, 0)))
    by_col = by_col.reshape(h, GRID_W, (n_dr + 2 * margin) * GRID_W)
    col = np.arange(GRID_W)
    col_start = np.clip(col - NA_COLS // 2, 0, GRID_W - NA_COLS)
    col_ok = (col[None, :] >= col_start[:, None]) & (col[None, :] < col_start[:, None] + NA_COLS)
    blocks = []
    for q_row0 in (0, NA_QROWS, rows - NA_QROWS):
        k_row0 = int(np.clip(q_row0 - NA_ROWS // 2, 0, rows - NA_KROWS))
        kr = k_row0 + np.arange(NA_KROWS)
        for rq in range(NA_QROWS):
            r = q_row0 + rq
            r0 = int(np.clip(r - NA_ROWS // 2, 0, rows - NA_ROWS))
            row_ok = (kr >= r0) & (kr < r0 + NA_ROWS)
            ok = (col_ok[:, None, :] & row_ok[None, :, None]).reshape(GRID_W, NA_KROWS * GRID_W)
            first = k_row0 - r + NA_ROWS - 1 + margin
            window = by_col[:, :, first * GRID_W:(first + NA_KROWS) * GRID_W]
            blocks.append(jnp.where(ok[None], window, -jnp.inf))
    table = jnp.stack(blocks, axis=1)
    return table.reshape(h, 3, NA_QROWS * GRID_W, NA_KROWS * GRID_W)


def _neighbourhood_attention(p, bm):
    b, n, _ = p.shape
    tq = NA_QROWS * GRID_W
    half = tq // 2
    n_blocks = (n - N_CTX) // tq
    ctx_blocks = N_CTX // half
    col0 = 3 * A_HEADS
    pairs = B_HEADS // 2
    width = B_HEADS * B_DH

    def pattern(j):
        return jnp.minimum(j, 1) + (j == n_blocks - 1).astype(jnp.int32)

    ob_x = pl.pallas_call(
        _na_kernel,
        out_shape=jax.ShapeDtypeStruct((b, n - N_CTX, width), BF16),
        grid=(b, pairs, n_blocks),
        in_specs=[
            pl.BlockSpec((1, half, LANES), lambda i, h, j: (i, ctx_blocks + 2 * j, col0 + h)),
            pl.BlockSpec((1, half, LANES), lambda i, h, j: (i, ctx_blocks + 2 * j + 1, col0 + h)),
            pl.BlockSpec((1, n, LANES), lambda i, h, j: (i, 0, col0 + pairs + h)),
            pl.BlockSpec((1, n, LANES), lambda i, h, j: (i, 0, col0 + 2 * pairs + h)),
            pl.BlockSpec((2, 1, tq, NA_KROWS * GRID_W), lambda i, h, j: (h, pattern(j), 0, 0)),
        ],
        out_specs=pl.BlockSpec((1, tq, LANES), lambda i, h, j: (i, j, h)),
        compiler_params=_cparams(("arbitrary", "arbitrary", "arbitrary")),
        name="neighbourhood_attention",
    )(p, p, p, p, bm)
    ob_c = pl.pallas_call(
        _na_ctx_kernel,
        out_shape=jax.ShapeDtypeStruct((b, N_CTX, width), BF16),
        grid=(b, pairs),
        in_specs=[
            pl.BlockSpec((1, N_CTX, LANES), lambda i, h: (i, 0, col0 + h)),
            pl.BlockSpec((1, N_CTX, LANES), lambda i, h: (i, 0, col0 + pairs + h)),
            pl.BlockSpec((1, N_CTX, LANES), lambda i, h: (i, 0, col0 + 2 * pairs + h)),
        ],
        out_specs=pl.BlockSpec((1, N_CTX, LANES), lambda i, h: (i, 0, h)),
        compiler_params=_cparams(("arbitrary", "arbitrary")),
        name="context_attention",
    )(p, p, p)
    return ob_x, ob_c


def _gla_kernel(q_ref, k_ref, v_ref, lr_ref, wg_ref, bg_ref, o_ref, ob_ref, g_ref):
    n = q_ref.shape[1]
    blk = GLA_BLOCK
    per_blk = blk // C_CHUNK
    n_blocks = n // blk
    ri = lax.broadcasted_iota(jnp.int32, (blk, blk), 0)
    ci = lax.broadcasted_iota(jnp.int32, (blk, blk), 1)
    same_chunk = (ri // C_CHUNK) == (ci // C_CHUNK)
    keeps = (same_chunk & (ci <= ri), same_chunk & (ci >= ri))
    tris = tuple(jnp.where(kp, 1.0, 0.0).astype(BF16) for kp in keeps)
    w_gate = jnp.concatenate([wg_ref[0], wg_ref[1]], axis=1).astype(BF16)
    b_gate = jnp.concatenate([bg_ref[0:1, :], bg_ref[1:2, :]], axis=1)
    for i in range(n_blocks):
        z = _dot(lr_ref[0, i * blk:(i + 1) * blk, :].astype(BF16), w_gate) + b_gate
        g_ref[i * blk:(i + 1) * blk, :] = (jnp.minimum(z, 0.0) - jnp.log(1.0 + jnp.exp(-jnp.abs(z)))) * (1.0 / C_TAU)

    def block(sb, direction, st):
        keep = keeps[direction]
        end_row = C_CHUNK - 1 if direction == 0 else 0
        mid_row = C_CHUNK // 2 - 1 if direction == 0 else C_CHUNK // 2
        r0 = sb * blk
        g = g_ref[r0:r0 + blk, direction * C_DK:(direction + 1) * C_DK]
        g_hi = g.astype(BF16)
        g_lo = (g - g_hi.astype(F32)).astype(BF16)
        gc2 = _dot(tris[direction], jnp.concatenate([g_hi, g_lo], axis=1))
        gc = (gc2[:, :C_DK] + gc2[:, C_DK:]).reshape(per_blk, C_CHUNK, C_DK)
        g_end = gc[:, end_row:end_row + 1, :]
        g_mid = gc[:, mid_row:mid_row + 1, :]
        q = (q_ref[0, r0:r0 + blk, :].astype(F32) * (C_DK ** -0.5)).reshape(per_blk, C_CHUNK, C_DK)
        k = k_ref[0, r0:r0 + blk, :].astype(F32).reshape(per_blk, C_CHUNK, C_DK)
        v = v_ref[0, r0:r0 + blk, :]
        q_in = (q * jnp.exp(gc)).astype(BF16).reshape(blk, C_DK)
        q_mid = (q * jnp.exp(gc - g_mid)).astype(BF16).reshape(blk, C_DK)
        k_mid = (k * jnp.exp(g_mid - gc)).astype(BF16).reshape(blk, C_DK)
        k_end = (k * jnp.exp(g_end - gc)).astype(BF16).reshape(blk, C_DK)
        a = jnp.where(keep, _dot_nt(q_mid, k_mid), 0.0)
        o_intra = _dot(a.astype(BF16), v)
        ends = jnp.concatenate([g_end.reshape(per_blk, C_DK), jnp.zeros((8 - per_blk, C_DK), F32)], axis=0)
        decay = jnp.transpose(jnp.exp(ends))
        o_inter = [None] * per_blk
        for c in (range(per_blk) if direction == 0 else reversed(range(per_blk))):
            rows = slice(c * C_CHUNK, (c + 1) * C_CHUNK)
            o_inter[c] = _dot(q_in[rows], st.astype(BF16))
            st = st * decay[:, c:c + 1] + _dot_tn(k_end[rows], v[rows])
        return o_intra + jnp.concatenate(o_inter, axis=0), st

    st_f = jnp.zeros((C_DK, C_DV), F32)
    st_b = jnp.zeros((C_DK, C_DV), F32)
    ctx_blocks = N_CTX // blk
    order_b = list(reversed(range(ctx_blocks))) + list(reversed(range(ctx_blocks, n_blocks)))
    for i in range(n_blocks):
        o, st_f = block(i, 0, st_f)
        o_ref[0, i * blk:(i + 1) * blk, :] = o
        sb = order_b[i]
        o, st_b = block(sb, 1, st_b)
        ob_ref[sb * blk:(sb + 1) * blk, :] = o
    o_ref[0] = o_ref[0] + ob_ref[...]


def _gla(p, lr, wg, bg):
    b, n, _ = p.shape
    return pl.pallas_call(
        _gla_kernel,
        out_shape=jax.ShapeDtypeStruct((b, n, C_HEADS * C_DV), F32),
        grid=(b, C_HEADS),
        in_specs=[
            pl.BlockSpec((1, n, C_DK), lambda i, h: (i, 0, h)),
            pl.BlockSpec((1, n, C_DK), lambda i, h: (i, 0, C_HEADS + h)),
            pl.BlockSpec((1, n, C_DV), lambda i, h: (i, 0, C_HEADS + h)),
            pl.BlockSpec((1, n, LANES), lambda i, h: (i, 0, 0)),
            pl.BlockSpec((2, LANES, C_DK), lambda i, h: (0, 0, h)),
            pl.BlockSpec((2, C_DK), lambda i, h: (0, h)),
        ],
        out_specs=pl.BlockSpec((1, n, C_DV), lambda i, h: (i, 0, h)),
        scratch_shapes=[pltpu.VMEM((n, C_DV), F32), pltpu.VMEM((n, 2 * C_DK), F32)],
        compiler_params=_cparams(("arbitrary", "arbitrary")),
        name="gla",
    )(p, p, p, lr, wg, bg)


def _out_kernel(*refs, gla):
    if gla:
        o_ref, gate_ref, gn_ref, w_ref, s_ref, mod_ref, ng_ref, rw1_ref, rw2_ref, snew_ref, h2_ref, lg_ref = refs
        o = o_ref[0]
        gate = gate_ref[0].astype(F32)
        parts = []
        for hd in range(C_HEADS):
            oh = _rms(o[:, hd * C_DV:(hd + 1) * C_DV], gn_ref[...])
            parts.append((oh * _silu(gate[:, hd * C_DV:(hd + 1) * C_DV])).astype(BF16))
        acc = _dot(jnp.concatenate(parts, axis=-1), w_ref[...])
    else:
        oa_ref, obx_ref, obc_ref, w_ref, s_ref, mod_ref, ng_ref, rw1_ref, rw2_ref, snew_ref, h2_ref, lg_ref = refs
        half = oa_ref.shape[2]
        ob = jnp.where(pl.program_id(1) == 0, obc_ref[0], obx_ref[0])
        acc = _dot(oa_ref[0], w_ref[0:half, :]) + _dot(ob, w_ref[half:, :])
    x = s_ref[0] + mod_ref[0, 0, 2:3, :] * acc
    snew_ref[0] = x
    h2 = _rms(x, ng_ref[...]) * (1.0 + mod_ref[0, 0, 4:5, :]) + mod_ref[0, 0, 3:4, :]
    for sg in range(ROW_SEGS):
        h2_ref[0, :, sg, :] = h2[:, sg * LANES:(sg + 1) * LANES]
    hi = h2.astype(BF16)
    lo = (h2 - hi.astype(F32)).astype(BF16)
    t = _dot(hi, rw1_ref[...])
    lg_ref[0] = t + pltpu.roll(t, LANES - N_EXPERTS, 1) + _dot(lo, rw2_ref[...])


def _out_project(mix, w_out, s, mod, ng, rw1, rw2, *, gla, gn=None):
    b, n, d = s.shape
    tm = ROW_TILE
    row_spec = pl.BlockSpec((1, tm, d), lambda i, j: (i, j, 0))
    const2 = lambda i, j: (0, 0)
    if gla:
        o, p = mix
        in_specs = [row_spec, pl.BlockSpec((1, tm, d), lambda i, j: (i, j, 2)),
                    pl.BlockSpec((1, C_DV), const2)]
        args = [o, p, gn.reshape(1, C_DV)]
    else:
        oa, ob_x, ob_c = mix
        half = oa.shape[2]
        in_specs = [pl.BlockSpec((1, tm, half), lambda i, j: (i, j, 0)),
                    pl.BlockSpec((1, tm, half), lambda i, j: (i, jnp.maximum(j - 1, 0), 0)),
                    pl.BlockSpec((1, tm, half), lambda i, j: (i, 0, 0))]
        args = [oa, ob_x, ob_c]
    in_specs += [pl.BlockSpec((d, d), const2), row_spec,
                 pl.BlockSpec((1, 1, 6, d), lambda i, j: (i, jnp.minimum(j, 1), 0, 0)),
                 pl.BlockSpec((1, d), const2), pl.BlockSpec((d, LANES), const2), pl.BlockSpec((d, LANES), const2)]
    args += [w_out, s, mod, ng.reshape(1, d), rw1, rw2]
    return pl.pallas_call(
        functools.partial(_out_kernel, gla=gla),
        out_shape=[jax.ShapeDtypeStruct((b, n, d), F32), jax.ShapeDtypeStruct((b, n, ROW_SEGS, LANES), F32),
                   jax.ShapeDtypeStruct((b, n, LANES), F32)],
        grid=(b, n // tm),
        in_specs=in_specs,
        out_specs=[row_spec, pl.BlockSpec((1, tm, ROW_SEGS, LANES), lambda i, j: (i, j, 0, 0)),
                   pl.BlockSpec((1, tm, LANES), lambda i, j: (i, j, 0))],
        compiler_params=_cparams(("arbitrary", "arbitrary")),
        name="out_project",
    )(*args)


def _route_kernel(bias_ref, lg_ref, cls_ref, wlo_ref, whi_ref):
    score = [_sigmoid(lg_ref[e]) for e in range(N_EXPERTS)]
    sel = [score[e] + bias_ref[e] for e in range(N_EXPERTS)]
    grp_score = []
    for g in range(N_GROUPS):
        v = sel[g * PER_GROUP:(g + 1) * PER_GROUP]
        best = v[0] + v[1]
        for a in range(PER_GROUP):
            for c in range(a + 1, PER_GROUP):
                if (a, c) != (0, 1):
                    best = jnp.maximum(best, v[a] + v[c])
        grp_score.append(best)
    grp = jnp.zeros(grp_score[0].shape, jnp.int32)
    best = grp_score[0]
    for g in range(1, N_GROUPS):
        upd = grp_score[g] > best
        best = jnp.where(upd, grp_score[g], best)
        grp = jnp.where(upd, g, grp)

    def pick(vals, j):
        out = vals[j]
        for g in range(1, N_GROUPS):
            out = jnp.where(grp == g, vals[g * PER_GROUP + j], out)
        return out

    v = [pick(sel, j) for j in range(PER_GROUP)]
    sc = [pick(score, j) for j in range(PER_GROUP)]
    one = jnp.ones(grp.shape, jnp.int32)
    zero = jnp.zeros(grp.shape, jnp.int32)
    chosen = []
    for j in range(PER_GROUP):
        rank = zero
        for m in range(PER_GROUP):
            if m == j:
                continue
            ahead = (v[m] >= v[j]) if m < j else (v[m] > v[j])
            rank = rank + jnp.where(ahead, one, zero)
        chosen.append(rank < 2)
    code = zero
    for j in range(PER_GROUP):
        code = code + jnp.where(chosen[j], one * (1 << j), zero)
    pair = zero
    for idx in range(6):
        pair = jnp.where(code == (1 << PAIR_LO[idx]) + (1 << PAIR_HI[idx]), idx, pair)
    s_lo = jnp.where(chosen[0], sc[0], jnp.where(chosen[1], sc[1], sc[2]))
    s_hi = jnp.where(chosen[3], sc[3], jnp.where(chosen[2], sc[2], sc[1]))
    den = s_lo + s_hi
    cls_ref[...] = grp * 6 + pair
    wlo_ref[...] = s_lo / den
    whi_ref[...] = s_hi / den


def _route(logits_t, router_bias):
    _, r, _ = logits_t.shape
    full = pl.BlockSpec((r, LANES), lambda i: (0, 0))
    return pl.pallas_call(
        _route_kernel,
        out_shape=[jax.ShapeDtypeStruct((r, LANES), jnp.int32), jax.ShapeDtypeStruct((r, LANES), F32),
                   jax.ShapeDtypeStruct((r, LANES), F32)],
        grid=(1,),
        in_specs=[pl.BlockSpec(memory_space=pltpu.SMEM),
                  pl.BlockSpec((N_EXPERTS, r, LANES), lambda i: (0, 0, 0))],
        out_specs=[full, full, full],
        compiler_params=_cparams(("arbitrary",)),
        name="route",
    )(router_bias, logits_t)


def _gather_rows(table, idx):
    p = idx.shape[0]
    per_worker = p // SC_WORKERS
    n_chunks = per_worker // SC_GATHER_ROWS
    mesh = plsc.VectorSubcoreMesh(core_axis_name="c", subcore_axis_name="s", num_cores=SC_CORES,
                                  num_subcores=SC_SUBCORES)

    @functools.partial(
        pl.kernel, mesh=mesh,
        out_type=jax.ShapeDtypeStruct((p,) + table.shape[1:], table.dtype),
        scratch_types=[pltpu.VMEM((per_worker,), jnp.int32),
                       pltpu.VMEM((SC_GATHER_ROWS,) + table.shape[1:], table.dtype),
                       pltpu.SemaphoreType.DMA],
        compiler_params=pltpu.CompilerParams(use_tc_tiling_on_sc=True),
        name="gather_rows",
    )
    def gather(table_hbm, idx_hbm, out_hbm, idx_v, rows_v, sem):
        worker = lax.axis_index("s") * SC_CORES + lax.axis_index("c")
        base = worker * per_worker
        pltpu.sync_copy(idx_hbm.at[pl.ds(base, per_worker)], idx_v)

        @pl.loop(0, n_chunks)
        def _(i):
            off = pl.multiple_of(i * SC_GATHER_ROWS, SC_GATHER_ROWS)
            pltpu.async_copy(table_hbm.at[idx_v.at[pl.ds(off, SC_GATHER_ROWS)]], rows_v, sem).wait()
            pltpu.sync_copy(rows_v, out_hbm.at[pl.ds(base + off, SC_GATHER_ROWS)])

    return gather(table, idx)


def _moe_kernel(lo_ref, hi_ref, nt_ref, x_ref, wr_ref, w1l_ref, w1h_ref, w3l_ref, w3h_ref, w2l_ref, w2h_ref, y_ref,
                w1l_b, w1h_b, w3l_b, w3h_b, w2l_b, w2h_b):
    j = pl.program_id(0)
    prev = jnp.maximum(j - 1, 0)

    @pl.when(j < nt_ref[0])
    def _():
        @pl.when((j == 0) | (lo_ref[j] != lo_ref[prev]))
        def _():
            w1l_b[...] = w1l_ref[0, 0].astype(BF16)
            w3l_b[...] = w3l_ref[0, 0].astype(BF16)
            w2l_b[...] = w2l_ref[0, 0].astype(BF16)

        @pl.when((j == 0) | (hi_ref[j] != hi_ref[prev]))
        def _():
            w1h_b[...] = w1h_ref[0, 0].astype(BF16)
            w3h_b[...] = w3h_ref[0, 0].astype(BF16)
            w2h_b[...] = w2h_ref[0, 0].astype(BF16)

        x = jnp.concatenate([x_ref[:, sg, :] for sg in range(ROW_SEGS)], axis=-1).astype(BF16)
        wr = wr_ref[...]
        he_lo = (_silu(_dot(x, w1l_b[...])) * _dot(x, w3l_b[...]) * wr[:, 0:1]).astype(BF16)
        he_hi = (_silu(_dot(x, w1h_b[...])) * _dot(x, w3h_b[...]) * wr[:, 1:2]).astype(BF16)
        y = _dot(he_lo, w2l_b[...]) + _dot(he_hi, w2h_b[...])
        for sg in range(ROW_SEGS):
            y_ref[:, sg, :] = y[:, sg * LANES:(sg + 1) * LANES]

    @pl.when(j >= nt_ref[0])
    def _():
        y_ref[...] = jnp.zeros_like(y_ref)


def _moe_experts(xs, wrow, tile_lo, tile_hi, n_tiles, layer, w1, w3, w2):
    tp = xs.shape[0]
    tm = MOE_TILE
    d, de = w1.shape[2], w1.shape[3]

    def row_map(j, lo, hi, nt):
        return (jnp.minimum(j, nt[0] - 1), 0, 0)

    def gate_map(j, lo, hi, nt):
        return (jnp.minimum(j, nt[0] - 1), 0)

    def lo_map(j, lo, hi, nt):
        return (layer, lo[jnp.minimum(j, nt[0] - 1)], 0, 0)

    def hi_map(j, lo, hi, nt):
        return (layer, hi[jnp.minimum(j, nt[0] - 1)], 0, 0)

    up = (1, 1, d, de)
    down = (1, 1, de, d)
    return pl.pallas_call(
        _moe_kernel,
        out_shape=jax.ShapeDtypeStruct((tp, ROW_SEGS, LANES), F32),
        grid_spec=pltpu.PrefetchScalarGridSpec(
            num_scalar_prefetch=3,
            grid=(tp // tm,),
            in_specs=[pl.BlockSpec((tm, ROW_SEGS, LANES), row_map), pl.BlockSpec((tm, 2), gate_map),
                      pl.BlockSpec(up, lo_map), pl.BlockSpec(up, hi_map),
                      pl.BlockSpec(up, lo_map), pl.BlockSpec(up, hi_map),
                      pl.BlockSpec(down, lo_map), pl.BlockSpec(down, hi_map)],
            out_specs=pl.BlockSpec((tm, ROW_SEGS, LANES), lambda j, lo, hi, nt: (j, 0, 0)),
            scratch_shapes=[pltpu.VMEM((d, de), BF16), pltpu.VMEM((d, de), BF16),
                            pltpu.VMEM((d, de), BF16), pltpu.VMEM((d, de), BF16),
                            pltpu.VMEM((de, d), BF16), pltpu.VMEM((de, d), BF16)],
        ),
        compiler_params=_cparams(("arbitrary",)),
        name="moe_experts",
    )(tile_lo, tile_hi, n_tiles, xs, wrow, w1, w1, w3, w3, w2, w2)


def _moe(h2, logits, router_bias, layer, w1, w3, w2):
    t = h2.shape[0]
    tm = MOE_TILE
    n_tiles_max = t // tm + N_CLASSES
    tp = n_tiles_max * tm
    lg_t = logits[:, :N_EXPERTS].T.reshape(N_EXPERTS, t // LANES, LANES)
    cls, wlo, whi = _route(lg_t, router_bias)
    cls, wlo, whi = cls.reshape(t), wlo.reshape(t), whi.reshape(t)
    classes = jnp.arange(N_CLASSES, dtype=jnp.int32)
    onehot = (cls[:, None] == classes[None, :]).astype(jnp.int32)
    csum = jnp.cumsum(onehot, axis=0)
    rank = jnp.sum(csum * onehot, axis=1) - 1
    counts = csum[-1]
    tiles_per = (counts + tm - 1) // tm
    tile_end = jnp.cumsum(tiles_per)
    tile_start = tile_end - tiles_per
    dest = jnp.sum(onehot * (tile_start * tm)[None, :], axis=1) + rank
    tiles = jnp.arange(n_tiles_max, dtype=jnp.int32)
    tile_cls = jnp.minimum(jnp.sum((tile_end[None, :] <= tiles[:, None]).astype(jnp.int32), axis=1), N_CLASSES - 1)
    pair = tile_cls % 6
    base = (tile_cls // 6) * PER_GROUP
    pair_onehot = (pair[:, None] == jnp.arange(6, dtype=jnp.int32)[None, :]).astype(jnp.int32)
    tile_lo = base + jnp.sum(pair_onehot * jnp.asarray(PAIR_LO, jnp.int32)[None, :], axis=1)
    tile_hi = base + jnp.sum(pair_onehot * jnp.asarray(PAIR_HI, jnp.int32)[None, :], axis=1)
    per_token = jnp.stack([jnp.arange(t, dtype=jnp.int32), lax.bitcast_convert_type(wlo, jnp.int32),
                           lax.bitcast_convert_type(whi, jnp.int32)], axis=1)
    per_row = jnp.zeros((tp, 3), jnp.int32).at[dest].set(per_token, unique_indices=True)
    tok = per_row[:, 0]
    wrow = lax.bitcast_convert_type(per_row[:, 1:3], F32)
    xs = _gather_rows(h2, tok)
    ys = _moe_experts(xs, wrow, tile_lo, tile_hi, tile_end[-1:], layer, w1, w3, w2)
    return _gather_rows(ys, dest)


def _final_kernel(s_ref, y_ref, mod_ref, g_ref, o_ref):
    x = s_ref[0] + mod_ref[0, 0, 5:6, :] * _token_rows(y_ref)
    o_ref[0] = _rms(x, g_ref[...])


def _final(s, y, mod, final_g):
    b, n, d = s.shape
    tm = ROW_TILE
    skip = N_CTX // tm
    row_spec = pl.BlockSpec((1, tm, d), lambda i, j: (i, j + skip, 0))
    return pl.pallas_call(
        _final_kernel,
        out_shape=jax.ShapeDtypeStruct((b, n - N_CTX, d), F32),
        grid=(b, (n - N_CTX) // tm),
        in_specs=[row_spec, pl.BlockSpec((1, tm, ROW_SEGS, LANES), lambda i, j: (i, j + skip, 0, 0)),
                  pl.BlockSpec((1, 1, 6, d), lambda i, j: (i, 1, 0, 0)),
                  pl.BlockSpec((1, d), lambda i, j: (0, 0))],
        out_specs=pl.BlockSpec((1, tm, d), lambda i, j: (i, j, 0)),
        compiler_params=_cparams(("arbitrary", "arbitrary")),
        name="final_norm",
    )(s, y, mod, final_g.reshape(1, d))


def _rope_tables(seq):
    pos = jnp.arange(seq)
    row_pos, col_pos = pos // GRID_W, pos % GRID_W
    n = A_DH // 2
    inv = ROPE_BASE ** (-jnp.arange(0, n, 2, dtype=F32) / n)
    ang_r = row_pos.astype(F32)[:, None] * inv[None, :]
    ang_c = col_pos.astype(F32)[:, None] * inv[None, :]
    ang = jnp.concatenate([ang_r, ang_c], axis=-1)
    cos = jnp.tile(jnp.cos(ang), (1, 4))
    sin = jnp.tile(jnp.sin(ang), (1, 4))
    sin = jnp.concatenate([-sin[:, :LANES // 2], sin[:, LANES // 2:]], axis=-1)
    cos = jnp.concatenate([jnp.ones((N_CTX, LANES), F32), cos], axis=0)
    sin = jnp.concatenate([jnp.zeros((N_CTX, LANES), F32), sin], axis=0)
    return cos, sin


def _interleave_maps(w):
    d = w.shape[0]
    w = w.reshape(d, A_HEADS, 2, 2, 2, A_DH // 4)
    return w.transpose(0, 1, 4, 2, 3, 5).reshape(d, A_HEADS * LANES)


def _att_weights(w_in):
    a_qk = A_HEADS * 2 * A_DH
    a_v = A_HEADS * A_DV
    b_w = B_HEADS * B_DH
    qa = _interleave_maps(w_in[:, :a_qk]) * (A_DH ** -0.5)
    ka = _interleave_maps(w_in[:, a_qk:2 * a_qk])
    va = w_in[:, 2 * a_qk:2 * a_qk + a_v]
    o = 2 * a_qk + a_v
    qb = w_in[:, o:o + b_w] * (B_DH ** -0.5)
    rest = w_in[:, o + b_w:]
    return jnp.concatenate([qa, ka, va, qb, rest], axis=1).astype(BF16)


def _gla_weights(w_in):
    d = w_in.shape[0]
    n_main = 2 * C_HEADS * C_DK + 2 * C_HEADS * C_DV
    pad = jnp.zeros((d, LANES - 2 * C_RANK), w_in.dtype)
    return jnp.concatenate([w_in, pad], axis=1).astype(BF16), n_main


def kernel(x, c, ctx, c_ctx, w_mod, b_mod, norm_g, final_g, att_w_in, att_w_out, att_lambda, att_subln_g, na_bias,
           gla_w_in, gla_w_gate, gla_b_gate, gla_norm_g, gla_w_out, router_w, router_bias, moe_w1, moe_w3, moe_w2):
    b, seq, d = x.shape
    n = N_CTX + seq
    s = jnp.concatenate([ctx, x], axis=1)

    rows = b + 1
    rows_pad = -(-rows // 8) * 8
    cc = jnp.concatenate([c, c_ctx[None, :], jnp.zeros((rows_pad - rows, d), F32)], axis=0)
    mod_all = _mod_vectors(cc, w_mod, b_mod)
    mod_x = mod_all[:, :b].reshape(DEPTH, b, 1, 6, d)
    mod_c = jnp.broadcast_to(mod_all[:, b].reshape(DEPTH, 1, 1, 6, d), (DEPTH, b, 1, 6, d))
    mods = jnp.concatenate([mod_c, mod_x], axis=2)

    cos, sin = _rope_tables(seq)
    rw_hi = router_w.astype(BF16)
    rw_lo = (router_w - rw_hi.astype(F32)).astype(BF16)
    zpad = jnp.zeros((d, LANES - 2 * N_EXPERTS), BF16)
    rw1 = jnp.concatenate([rw_hi, rw_lo, zpad], axis=1)
    rw2 = jnp.concatenate([rw_hi, jnp.zeros((d, N_EXPERTS), BF16), zpad], axis=1)

    y = None
    for i in range(DEPTH):
        j = i // 2
        modp = mods[i - 1] if i else None
        if i % 2 == 0:
            lam_init = 0.8 - 0.6 * math.exp(-0.3 * i)
            w = _att_weights(att_w_in[j])
            outs = _project(s, y, modp, mods[i], norm_g[i, 0], w, cos, sin,
                            n_rope=2 * A_HEADS * LANES, n_bf16=w.shape[1])
            if i:
                s = outs[0]
            p = outs[-1]
            oa = _diff_attention(p, att_lambda[j], att_subln_g[j], lam_init)
            ob_x, ob_c = _neighbourhood_attention(p, _na_bias_table(na_bias[j], seq // GRID_W))
            s, h2, logits = _out_project((oa, ob_x, ob_c), att_w_out[j].astype(BF16), s, mods[i], norm_g[i, 1],
                                         rw1, rw2, gla=False)
        else:
            w, n_main = _gla_weights(gla_w_in[j])
            outs = _project(s, y, modp, mods[i], norm_g[i, 0], w, None, None, n_rope=0, n_bf16=n_main)
            s, p, lr = outs
            wg = jnp.zeros((2, LANES, C_HEADS * C_DK), F32)
            wg = wg.at[0, :C_RANK].set(gla_w_gate[j, 0]).at[1, C_RANK:2 * C_RANK].set(gla_w_gate[j, 1])
            o = _gla(p, lr, wg, gla_b_gate[j])
            s, h2, logits = _out_project((o, p), gla_w_out[j].astype(BF16), s, mods[i], norm_g[i, 1],
                                         rw1, rw2, gla=True, gn=gla_norm_g[j])
        y = _moe(h2.reshape(b * n, ROW_SEGS, LANES), logits.reshape(b * n, LANES), router_bias,
                 i, moe_w1, moe_w3, moe_w2)
        y = y.reshape(b, n, ROW_SEGS, LANES)
    return _final(s, y, mods[DEPTH - 1], final_g)
```

```python
import functools
import math

import jax
import jax.numpy as jnp
import numpy as np
from jax import lax
from jax.experimental import pallas as pl
from jax.experimental.pallas import tpu as pltpu
from jax.experimental.pallas import tpu_sc as plsc

F32 = jnp.float32
BF16 = jnp.bfloat16

D_MODEL = 1024
DEPTH = 4
GRID_W = 64
N_CTX = 256
A_HEADS = 4
A_DH = 64
A_DV = 128
B_HEADS = 8
B_DH = 64
NA_ROWS = 8
NA_COLS = 16
C_HEADS = 4
C_DK = 128
C_DV = 256
C_RANK = 16
C_TAU = 16.0
C_CHUNK = 64
GLA_BLOCK = 256
N_EXPERTS = 16
N_GROUPS = 4
PER_GROUP = 4
D_EXPERT = 512
ROPE_BASE = 10000.0
EPS = 1e-6

LANES = 128
ROW_TILE = 256
MOE_TILE = 256
SC_CORES = 2
SC_SUBCORES = 16
SC_WORKERS = SC_CORES * SC_SUBCORES
SC_GATHER_ROWS = 64
N_CLASSES = N_GROUPS * 6
NA_QROWS = 8
NA_KROWS = NA_QROWS + NA_ROWS - 1
VMEM_LIMIT = 52 * 1024 * 1024
ROW_SEGS = D_MODEL // LANES

PAIR_LO = (0, 0, 0, 1, 1, 2)
PAIR_HI = (1, 2, 3, 2, 3, 3)


def _cparams(sem):
    return pltpu.CompilerParams(dimension_semantics=sem, vmem_limit_bytes=VMEM_LIMIT)


def _sigmoid(x):
    return 1.0 / (1.0 + jnp.exp(-x))


def _silu(x):
    return x * _sigmoid(x)


def _rms(x, g):
    return x * lax.rsqrt(jnp.mean(x * x, axis=-1, keepdims=True) + EPS) * g


def _dot(a, b):
    return jnp.dot(a, b, preferred_element_type=F32)


def _dot_nt(a, b):
    return lax.dot_general(a, b, (((1,), (1,)), ((), ())), preferred_element_type=F32)


def _dot_tn(a, b):
    return lax.dot_general(a, b, (((0,), (0,)), ((), ())), preferred_element_type=F32)


def _mod_kernel(c_ref, w_ref, b_ref, o_ref):
    a = _silu(c_ref[...]).astype(BF16)
    o_ref[0] = _dot(a, w_ref[0].astype(BF16)) + b_ref[0]


def _mod_vectors(cc, w_mod, b_mod):
    depth, d, n6 = w_mod.shape
    rows = cc.shape[0]
    tn = 1536
    return pl.pallas_call(
        _mod_kernel,
        out_shape=jax.ShapeDtypeStruct((depth, rows, n6), F32),
        grid=(depth, n6 // tn),
        in_specs=[
            pl.BlockSpec((rows, d), lambda i, j: (0, 0)),
            pl.BlockSpec((1, d, tn), lambda i, j: (i, 0, j)),
            pl.BlockSpec((1, 1, tn), lambda i, j: (i, 0, j)),
        ],
        out_specs=pl.BlockSpec((1, rows, tn), lambda i, j: (i, 0, j)),
        compiler_params=_cparams(("arbitrary", "arbitrary")),
        name="mod_vectors",
    )(cc, w_mod, b_mod.reshape(depth, 1, n6))


def _token_rows(ref):
    return jnp.concatenate([ref[0, :, sg, :] for sg in range(ROW_SEGS)], axis=-1)


def _proj_kernel(*refs, has_y, n_rope, n_bf16, col_chunk):
    it = iter(refs)
    s_ref = next(it)
    y_ref = next(it) if has_y else None
    modp_ref = next(it) if has_y else None
    mod_ref = next(it)
    ng_ref = next(it)
    w_ref = next(it)
    cos_ref = next(it) if n_rope else None
    sin_ref = next(it) if n_rope else None
    snew_ref = next(it) if has_y else None
    p_ref = next(it)
    lr_ref = next(it) if w_ref.shape[1] > n_bf16 else None

    x = s_ref[0]
    if has_y:
        x = x + modp_ref[0, 0, 5:6, :] * _token_rows(y_ref)
        snew_ref[0] = x
    h = _rms(x, ng_ref[...])
    h = h * (1.0 + mod_ref[0, 0, 1:2, :]) + mod_ref[0, 0, 0:1, :]
    hb = h.astype(BF16)
    n_out = w_ref.shape[1]
    for c0 in range(0, n_out, col_chunk):
        c1 = min(c0 + col_chunk, n_out)
        acc = _dot(hb, w_ref[:, c0:c1])
        for b0 in range(c0, c1, LANES):
            t = acc[:, b0 - c0:b0 - c0 + LANES]
            if b0 < n_rope:
                t = t * cos_ref[...] + pltpu.roll(t, LANES // 2, 1) * sin_ref[...]
            if b0 < n_bf16:
                p_ref[0, :, b0:b0 + LANES] = t.astype(BF16)
            else:
                lr_ref[0, :, b0 - n_bf16:b0 - n_bf16 + LANES] = t


def _project(s, y, modp, mod, ng, w, cos, sin, *, n_rope, n_bf16):
    b, n, d = s.shape
    n_out = w.shape[1]
    tm = ROW_TILE
    has_y = y is not None
    row_spec = pl.BlockSpec((1, tm, d), lambda i, j: (i, j, 0))
    mod_spec = pl.BlockSpec((1, 1, 6, d), lambda i, j: (i, jnp.minimum(j, 1), 0, 0))
    in_specs = [row_spec]
    args = [s]
    if has_y:
        in_specs += [pl.BlockSpec((1, tm, ROW_SEGS, LANES), lambda i, j: (i, j, 0, 0)), mod_spec]
        args += [y, modp]
    in_specs += [mod_spec, pl.BlockSpec((1, d), lambda i, j: (0, 0)),
                 pl.BlockSpec((d, n_out), lambda i, j: (0, 0))]
    args += [mod, ng.reshape(1, d), w]
    if n_rope:
        tab_spec = pl.BlockSpec((tm, LANES), lambda i, j: (j, 0))
        in_specs += [tab_spec, tab_spec]
        args += [cos, sin]
    out_shape, out_specs = [], []
    if has_y:
        out_shape.append(jax.ShapeDtypeStruct((b, n, d), F32))
        out_specs.append(row_spec)
    out_shape.append(jax.ShapeDtypeStruct((b, n, n_bf16), BF16))
    out_specs.append(pl.BlockSpec((1, tm, n_bf16), lambda i, j: (i, j, 0)))
    if n_out > n_bf16:
        out_shape.append(jax.ShapeDtypeStruct((b, n, n_out - n_bf16), F32))
        out_specs.append(pl.BlockSpec((1, tm, n_out - n_bf16), lambda i, j: (i, j, 0)))
    return pl.pallas_call(
        functools.partial(_proj_kernel, has_y=has_y, n_rope=n_rope, n_bf16=n_bf16, col_chunk=512),
        out_shape=out_shape,
        grid=(b, n // tm),
        in_specs=in_specs,
        out_specs=out_specs,
        compiler_params=_cparams(("arbitrary", "arbitrary")),
        name="norm_mod_project",
    )(*args)


def _softmax_rows(s):
    e = jnp.exp(s - jnp.max(s, axis=-1, keepdims=True))
    return e * (1.0 / jnp.sum(e, axis=-1, keepdims=True))


def _diff_attn_kernel(q_ref, k_ref, v_ref, lam_ref, g_ref, o_ref, *, lam_init):
    lp = lam_ref[...]
    lam = (jnp.exp(jnp.sum(lp[0:1] * lp[1:2], axis=-1, keepdims=True))
           - jnp.exp(jnp.sum(lp[2:3] * lp[3:4], axis=-1, keepdims=True)) + lam_init)
    q = q_ref[0]
    lane = lax.broadcasted_iota(jnp.int32, (1, LANES), 1)
    first_map = (lane // 32) % 2 == 0
    zero = jnp.zeros_like(q)
    q0 = jnp.where(first_map, q, zero)
    q1 = jnp.where(first_map, zero, q)

    def attend(n_keys):
        k = k_ref[0, 0:n_keys, :]
        v = v_ref[0, 0:n_keys, :]
        a = _softmax_rows(_dot_nt(q0, k)) - lam * _softmax_rows(_dot_nt(q1, k))
        o = _dot(a.astype(BF16), v)
        o_ref[0] = (_rms(o, g_ref[...]) * (1.0 - lam_init)).astype(BF16)

    qi = pl.program_id(2)

    @pl.when(qi == 0)
    def _():
        attend(N_CTX)

    @pl.when(qi > 0)
    def _():
        attend(k_ref.shape[1])


def _diff_attention(p, lam_p, subln_g, lam_init):
    b, n, _ = p.shape
    tq = ROW_TILE
    return pl.pallas_call(
        functools.partial(_diff_attn_kernel, lam_init=lam_init),
        out_shape=jax.ShapeDtypeStruct((b, n, A_HEADS * A_DV), BF16),
        grid=(b, A_HEADS, n // tq),
        in_specs=[
            pl.BlockSpec((1, tq, LANES), lambda i, h, j: (i, j, h)),
            pl.BlockSpec((1, n, LANES), lambda i, h, j: (i, 0, A_HEADS + h)),
            pl.BlockSpec((1, n, LANES), lambda i, h, j: (i, 0, 2 * A_HEADS + h)),
            pl.BlockSpec((4, A_DH), lambda i, h, j: (0, 0)),
            pl.BlockSpec((1, A_DV), lambda i, h, j: (0, 0)),
        ],
        out_specs=pl.BlockSpec((1, tq, LANES), lambda i, h, j: (i, j, h)),
        compiler_params=_cparams(("arbitrary", "arbitrary", "arbitrary")),
        name="diff_attention",
    )(p, p, p, lam_p, subln_g.reshape(1, A_DV))


def _split_heads(q):
    lane = lax.broadcasted_iota(jnp.int32, (1, LANES), 1)
    first_head = lane < B_DH
    zero = jnp.zeros_like(q)
    return first_head, (jnp.where(first_head, q, zero), jnp.where(first_head, zero, q))


def _na_ctx_kernel(q_ref, k_ref, v_ref, o_ref):
    first_head, qh = _split_heads(q_ref[0])
    kc = k_ref[0]
    vc = v_ref[0]
    outs = [_dot(_softmax_rows(_dot_nt(qh[h], kc)).astype(BF16), vc) for h in range(2)]
    o_ref[0] = jnp.where(first_head, outs[0], outs[1]).astype(BF16)


def _na_kernel(qa_ref, qb_ref, k_ref, v_ref, bm_ref, o_ref):
    blk = pl.program_id(2)
    first_head, qh = _split_heads(jnp.concatenate([qa_ref[0], qb_ref[0]], axis=0))
    kc = k_ref[0, 0:N_CTX, :]
    vc = v_ref[0, 0:N_CTX, :]
    rows = (k_ref.shape[1] - N_CTX) // GRID_W
    k_row0 = jnp.clip(blk * NA_QROWS - NA_ROWS // 2, 0, rows - NA_KROWS)
    start = pl.multiple_of(N_CTX + k_row0 * GRID_W, GRID_W)
    kw = k_ref[0, pl.ds(start, NA_KROWS * GRID_W), :]
    vw = v_ref[0, pl.ds(start, NA_KROWS * GRID_W), :]
    outs = []
    for h in range(2):
        s_loc = _dot_nt(qh[h], kw) + bm_ref[h, 0]
        s_ctx = _dot_nt(qh[h], kc)
        m = jnp.maximum(jnp.max(s_loc, axis=-1, keepdims=True), jnp.max(s_ctx, axis=-1, keepdims=True))
        e_loc = jnp.exp(s_loc - m)
        e_ctx = jnp.exp(s_ctx - m)
        den = jnp.sum(e_loc, axis=-1, keepdims=True) + jnp.sum(e_ctx, axis=-1, keepdims=True)
        outs.append((_dot(e_loc.astype(BF16), vw) + _dot(e_ctx.astype(BF16), vc)) * (1.0 / den))
    o_ref[0] = jnp.where(first_head, outs[0], outs[1]).astype(BF16)


def _na_bias_table(na_bias, rows):
    h = na_bias.shape[0]
    n_dr, n_dc = 2 * NA_ROWS - 1, 2 * NA_COLS - 1
    width = 2 * GRID_W
    left = GRID_W - NA_COLS
    u = jnp.pad(na_bias, ((0, 0), (0, 0), (left, width - left - n_dc)))
    skew = jnp.tile(u, (1, 1, GRID_W))[:, :, :GRID_W * (width - 1)].reshape(h, n_dr, GRID_W, width - 1)
    toeplitz = skew[:, :, :, GRID_W - 1:]
    margin = NA_KROWS - NA_ROWS
    by_col = jnp.pad(toeplitz.transpose(0, 2, 1, 3), ((0, 0), (0, 0), (margin, margin), (0, 0)))
    by_col = by_col.reshape(h, GRID_W, (n_dr + 2 * margin) * GRID_W)
    col = np.arange(GRID_W)
    col_start = np.clip(col - NA_COLS // 2, 0, GRID_W - NA_COLS)
    col_ok = (col[None, :] >= col_start[:, None]) & (col[None, :] < col_start[:, None] + NA_COLS)
    blocks = []
    for q_row0 in (0, NA_QROWS, rows - NA_QROWS):
        k_row0 = int(np.clip(q_row0 - NA_ROWS // 2, 0, rows - NA_KROWS))
        kr = k_row0 + np.arange(NA_KROWS)
        for rq in range(NA_QROWS):
            r = q_row0 + rq
            r0 = int(np.clip(r - NA_ROWS // 2, 0, rows - NA_ROWS))
            row_ok = (kr >= r0) & (kr < r0 + NA_ROWS)
            ok = (col_ok[:, None, :] & row_ok[None, :, None]).reshape(GRID_W, NA_KROWS * GRID_W)
            first = k_row0 - r + NA_ROWS - 1 + margin
            window = by_col[:, :, first * GRID_W:(first + NA_KROWS) * GRID_W]
            blocks.append(jnp.where(ok[None], window, -jnp.inf))
    table = jnp.stack(blocks, axis=1)
    return table.reshape(h, 3, NA_QROWS * GRID_W, NA_KROWS * GRID_W)


def _neighbourhood_attention(p, bm):
    b, n, _ = p.shape
    tq = NA_QROWS * GRID_W
    half = tq // 2
    n_blocks = (n - N_CTX) // tq
    ctx_blocks = N_CTX // half
    col0 = 3 * A_HEADS
    pairs = B_HEADS // 2
    width = B_HEADS * B_DH

    def pattern(j):
        return jnp.minimum(j, 1) + (j == n_blocks - 1).astype(jnp.int32)

    ob_x = pl.pallas_call(
        _na_kernel,
        out_shape=jax.ShapeDtypeStruct((b, n - N_CTX, width), BF16),
        grid=(b, pairs, n_blocks),
        in_specs=[
            pl.BlockSpec((1, half, LANES), lambda i, h, j: (i, ctx_blocks + 2 * j, col0 + h)),
            pl.BlockSpec((1, half, LANES), lambda i, h, j: (i, ctx_blocks + 2 * j + 1, col0 + h)),
            pl.BlockSpec((1, n, LANES), lambda i, h, j: (i, 0, col0 + pairs + h)),
            pl.BlockSpec((1, n, LANES), lambda i, h, j: (i, 0, col0 + 2 * pairs + h)),
            pl.BlockSpec((2, 1, tq, NA_KROWS * GRID_W), lambda i, h, j: (h, pattern(j), 0, 0)),
        ],
        out_specs=pl.BlockSpec((1, tq, LANES), lambda i, h, j: (i, j, h)),
        compiler_params=_cparams(("arbitrary", "arbitrary", "arbitrary")),
        name="neighbourhood_attention",
    )(p, p, p, p, bm)
    ob_c = pl.pallas_call(
        _na_ctx_kernel,
        out_shape=jax.ShapeDtypeStruct((b, N_CTX, width), BF16),
        grid=(b, pairs),
        in_specs=[
            pl.BlockSpec((1, N_CTX, LANES), lambda i, h: (i, 0, col0 + h)),
            pl.BlockSpec((1, N_CTX, LANES), lambda i, h: (i, 0, col0 + pairs + h)),
            pl.BlockSpec((1, N_CTX, LANES), lambda i, h: (i, 0, col0 + 2 * pairs + h)),
        ],
        out_specs=pl.BlockSpec((1, N_CTX, LANES), lambda i, h: (i, 0, h)),
        compiler_params=_cparams(("arbitrary", "arbitrary")),
        name="context_attention",
    )(p, p, p)
    return ob_x, ob_c


def _gla_kernel(q_ref, k_ref, v_ref, lr_ref, wg_ref, bg_ref, o_ref, ob_ref, g_ref):
    n = q_ref.shape[1]
    blk = GLA_BLOCK
    per_blk = blk // C_CHUNK
    n_blocks = n // blk
    ri = lax.broadcasted_iota(jnp.int32, (blk, blk), 0)
    ci = lax.broadcasted_iota(jnp.int32, (blk, blk), 1)
    same_chunk = (ri // C_CHUNK) == (ci // C_CHUNK)
    keeps = (same_chunk & (ci <= ri), same_chunk & (ci >= ri))
    tris = tuple(jnp.where(kp, 1.0, 0.0).astype(BF16) for kp in keeps)
    w_gate = jnp.concatenate([wg_ref[0], wg_ref[1]], axis=1).astype(BF16)
    b_gate = jnp.concatenate([bg_ref[0:1, :], bg_ref[1:2, :]], axis=1)
    for i in range(n_blocks):
        z = _dot(lr_ref[0, i * blk:(i + 1) * blk, :].astype(BF16), w_gate) + b_gate
        g_ref[i * blk:(i + 1) * blk, :] = (jnp.minimum(z, 0.0) - jnp.log(1.0 + jnp.exp(-jnp.abs(z)))) * (1.0 / C_TAU)

    def block(sb, direction, st):
        keep = keeps[direction]
        end_row = C_CHUNK - 1 if direction == 0 else 0
        mid_row = C_CHUNK // 2 - 1 if direction == 0 else C_CHUNK // 2
        r0 = sb * blk
        g = g_ref[r0:r0 + blk, direction * C_DK:(direction + 1) * C_DK]
        g_hi = g.astype(BF16)
        g_lo = (g - g_hi.astype(F32)).astype(BF16)
        gc2 = _dot(tris[direction], jnp.concatenate([g_hi, g_lo], axis=1))
        gc = (gc2[:, :C_DK] + gc2[:, C_DK:]).reshape(per_blk, C_CHUNK, C_DK)
        g_end = gc[:, end_row:end_row + 1, :]
        g_mid = gc[:, mid_row:mid_row + 1, :]
        q = (q_ref[0, r0:r0 + blk, :].astype(F32) * (C_DK ** -0.5)).reshape(per_blk, C_CHUNK, C_DK)
        k = k_ref[0, r0:r0 + blk, :].astype(F32).reshape(per_blk, C_CHUNK, C_DK)
        v = v_ref[0, r0:r0 + blk, :]
        q_in = (q * jnp.exp(gc)).astype(BF16).reshape(blk, C_DK)
        q_mid = (q * jnp.exp(gc - g_mid)).astype(BF16).reshape(blk, C_DK)
        k_mid = (k * jnp.exp(g_mid - gc)).astype(BF16).reshape(blk, C_DK)
        k_end = (k * jnp.exp(g_end - gc)).astype(BF16).reshape(blk, C_DK)
        a = jnp.where(keep, _dot_nt(q_mid, k_mid), 0.0)
        o_intra = _dot(a.astype(BF16), v)
        ends = jnp.concatenate([g_end.reshape(per_blk, C_DK), jnp.zeros((8 - per_blk, C_DK), F32)], axis=0)
        decay = jnp.transpose(jnp.exp(ends))
        o_inter = [None] * per_blk
        for c in (range(per_blk) if direction == 0 else reversed(range(per_blk))):
            rows = slice(c * C_CHUNK, (c + 1) * C_CHUNK)
            o_inter[c] = _dot(q_in[rows], st.astype(BF16))
            st = st * decay[:, c:c + 1] + _dot_tn(k_end[rows], v[rows])
        return o_intra + jnp.concatenate(o_inter, axis=0), st

    st_f = jnp.zeros((C_DK, C_DV), F32)
    st_b = jnp.zeros((C_DK, C_DV), F32)
    ctx_blocks = N_CTX // blk
    order_b = list(reversed(range(ctx_blocks))) + list(reversed(range(ctx_blocks, n_blocks)))
    for i in range(n_blocks):
        o, st_f = block(i, 0, st_f)
        o_ref[0, i * blk:(i + 1) * blk, :] = o
        sb = order_b[i]
        o, st_b = block(sb, 1, st_b)
        ob_ref[sb * blk:(sb + 1) * blk, :] = o
    o_ref[0] = o_ref[0] + ob_ref[...]


def _gla(p, lr, wg, bg):
    b, n, _ = p.shape
    return pl.pallas_call(
        _gla_kernel,
        out_shape=jax.ShapeDtypeStruct((b, n, C_HEADS * C_DV), F32),
        grid=(b, C_HEADS),
        in_specs=[
            pl.BlockSpec((1, n, C_DK), lambda i, h: (i, 0, h)),
            pl.BlockSpec((1, n, C_DK), lambda i, h: (i, 0, C_HEADS + h)),
            pl.BlockSpec((1, n, C_DV), lambda i, h: (i, 0, C_HEADS + h)),
            pl.BlockSpec((1, n, LANES), lambda i, h: (i, 0, 0)),
            pl.BlockSpec((2, LANES, C_DK), lambda i, h: (0, 0, h)),
            pl.BlockSpec((2, C_DK), lambda i, h: (0, h)),
        ],
        out_specs=pl.BlockSpec((1, n, C_DV), lambda i, h: (i, 0, h)),
        scratch_shapes=[pltpu.VMEM((n, C_DV), F32), pltpu.VMEM((n, 2 * C_DK), F32)],
        compiler_params=_cparams(("arbitrary", "arbitrary")),
        name="gla",
    )(p, p, p, lr, wg, bg)


def _out_kernel(*refs, gla):
    if gla:
        o_ref, gate_ref, gn_ref, w_ref, s_ref, mod_ref, ng_ref, rw1_ref, rw2_ref, snew_ref, h2_ref, lg_ref = refs
        o = o_ref[0]
        gate = gate_ref[0].astype(F32)
        parts = []
        for hd in range(C_HEADS):
            oh = _rms(o[:, hd * C_DV:(hd + 1) * C_DV], gn_ref[...])
            parts.append((oh * _silu(gate[:, hd * C_DV:(hd + 1) * C_DV])).astype(BF16))
        acc = _dot(jnp.concatenate(parts, axis=-1), w_ref[...])
    else:
        oa_ref, obx_ref, obc_ref, w_ref, s_ref, mod_ref, ng_ref, rw1_ref, rw2_ref, snew_ref, h2_ref, lg_ref = refs
        half = oa_ref.shape[2]
        ob = jnp.where(pl.program_id(1) == 0, obc_ref[0], obx_ref[0])
        acc = _dot(oa_ref[0], w_ref[0:half, :]) + _dot(ob, w_ref[half:, :])
    x = s_ref[0] + mod_ref[0, 0, 2:3, :] * acc
    snew_ref[0] = x
    h2 = _rms(x, ng_ref[...]) * (1.0 + mod_ref[0, 0, 4:5, :]) + mod_ref[0, 0, 3:4, :]
    for sg in range(ROW_SEGS):
        h2_ref[0, :, sg, :] = h2[:, sg * LANES:(sg + 1) * LANES]
    hi = h2.astype(BF16)
    lo = (h2 - hi.astype(F32)).astype(BF16)
    t = _dot(hi, rw1_ref[...])
    lg_ref[0] = t + pltpu.roll(t, LANES - N_EXPERTS, 1) + _dot(lo, rw2_ref[...])


def _out_project(mix, w_out, s, mod, ng, rw1, rw2, *, gla, gn=None):
    b, n, d = s.shape
    tm = ROW_TILE
    row_spec = pl.BlockSpec((1, tm, d), lambda i, j: (i, j, 0))
    const2 = lambda i, j: (0, 0)
    if gla:
        o, p = mix
        in_specs = [row_spec, pl.BlockSpec((1, tm, d), lambda i, j: (i, j, 2)),
                    pl.BlockSpec((1, C_DV), const2)]
        args = [o, p, gn.reshape(1, C_DV)]
    else:
        oa, ob_x, ob_c = mix
        half = oa.shape[2]
        in_specs = [pl.BlockSpec((1, tm, half), lambda i, j: (i, j, 0)),
                    pl.BlockSpec((1, tm, half), lambda i, j: (i, jnp.maximum(j - 1, 0), 0)),
                    pl.BlockSpec((1, tm, half), lambda i, j: (i, 0, 0))]
        args = [oa, ob_x, ob_c]
    in_specs += [pl.BlockSpec((d, d), const2), row_spec,
                 pl.BlockSpec((1, 1, 6, d), lambda i, j: (i, jnp.minimum(j, 1), 0, 0)),
                 pl.BlockSpec((1, d), const2), pl.BlockSpec((d, LANES), const2), pl.BlockSpec((d, LANES), const2)]
    args += [w_out, s, mod, ng.reshape(1, d), rw1, rw2]
    return pl.pallas_call(
        functools.partial(_out_kernel, gla=gla),
        out_shape=[jax.ShapeDtypeStruct((b, n, d), F32), jax.ShapeDtypeStruct((b, n, ROW_SEGS, LANES), F32),
                   jax.ShapeDtypeStruct((b, n, LANES), F32)],
        grid=(b, n // tm),
        in_specs=in_specs,
        out_specs=[row_spec, pl.BlockSpec((1, tm, ROW_SEGS, LANES), lambda i, j: (i, j, 0, 0)),
                   pl.BlockSpec((1, tm, LANES), lambda i, j: (i, j, 0))],
        compiler_params=_cparams(("arbitrary", "arbitrary")),
        name="out_project",
    )(*args)


def _route_kernel(bias_ref, lg_ref, cls_ref, wlo_ref, whi_ref):
    score = [_sigmoid(lg_ref[e]) for e in range(N_EXPERTS)]
    sel = [score[e] + bias_ref[e] for e in range(N_EXPERTS)]
    grp_score = []
    for g in range(N_GROUPS):
        v = sel[g * PER_GROUP:(g + 1) * PER_GROUP]
        best = v[0] + v[1]
        for a in range(PER_GROUP):
            for c in range(a + 1, PER_GROUP):
                if (a, c) != (0, 1):
                    best = jnp.maximum(best, v[a] + v[c])
        grp_score.append(best)
    grp = jnp.zeros(grp_score[0].shape, jnp.int32)
    best = grp_score[0]
    for g in range(1, N_GROUPS):
        upd = grp_score[g] > best
        best = jnp.where(upd, grp_score[g], best)
        grp = jnp.where(upd, g, grp)

    def pick(vals, j):
        out = vals[j]
        for g in range(1, N_GROUPS):
            out = jnp.where(grp == g, vals[g * PER_GROUP + j], out)
        return out

    v = [pick(sel, j) for j in range(PER_GROUP)]
    sc = [pick(score, j) for j in range(PER_GROUP)]
    one = jnp.ones(grp.shape, jnp.int32)
    zero = jnp.zeros(grp.shape, jnp.int32)
    chosen = []
    for j in range(PER_GROUP):
        rank = zero
        for m in range(PER_GROUP):
            if m == j:
                continue
            ahead = (v[m] >= v[j]) if m < j else (v[m] > v[j])
            rank = rank + jnp.where(ahead, one, zero)
        chosen.append(rank < 2)
    code = zero
    for j in range(PER_GROUP):
        code = code + jnp.where(chosen[j], one * (1 << j), zero)
    pair = zero
    for idx in range(6):
        pair = jnp.where(code == (1 << PAIR_LO[idx]) + (1 << PAIR_HI[idx]), idx, pair)
    s_lo = jnp.where(chosen[0], sc[0], jnp.where(chosen[1], sc[1], sc[2]))
    s_hi = jnp.where(chosen[3], sc[3], jnp.where(chosen[2], sc[2], sc[1]))
    den = s_lo + s_hi
    cls_ref[...] = grp * 6 + pair
    wlo_ref[...] = s_lo / den
    whi_ref[...] = s_hi / den


def _route(logits_t, router_bias):
    _, r, _ = logits_t.shape
    full = pl.BlockSpec((r, LANES), lambda i: (0, 0))
    return pl.pallas_call(
        _route_kernel,
        out_shape=[jax.ShapeDtypeStruct((r, LANES), jnp.int32), jax.ShapeDtypeStruct((r, LANES), F32),
                   jax.ShapeDtypeStruct((r, LANES), F32)],
        grid=(1,),
        in_specs=[pl.BlockSpec(memory_space=pltpu.SMEM),
                  pl.BlockSpec((N_EXPERTS, r, LANES), lambda i: (0, 0, 0))],
        out_specs=[full, full, full],
        compiler_params=_cparams(("arbitrary",)),
        name="route",
    )(router_bias, logits_t)


def _gather_rows(table, idx):
    p = idx.shape[0]
    per_worker = p // SC_WORKERS
    n_chunks = per_worker // SC_GATHER_ROWS
    mesh = plsc.VectorSubcoreMesh(core_axis_name="c", subcore_axis_name="s", num_cores=SC_CORES,
                                  num_subcores=SC_SUBCORES)

    @functools.partial(
        pl.kernel, mesh=mesh,
        out_type=jax.ShapeDtypeStruct((p,) + table.shape[1:], table.dtype),
        scratch_types=[pltpu.VMEM((per_worker,), jnp.int32),
                       pltpu.VMEM((SC_GATHER_ROWS,) + table.shape[1:], table.dtype),
                       pltpu.SemaphoreType.DMA],
        compiler_params=pltpu.CompilerParams(use_tc_tiling_on_sc=True),
        name="gather_rows",
    )
    def gather(table_hbm, idx_hbm, out_hbm, idx_v, rows_v, sem):
        worker = lax.axis_index("s") * SC_CORES + lax.axis_index("c")
        base = worker * per_worker
        pltpu.sync_copy(idx_hbm.at[pl.ds(base, per_worker)], idx_v)

        @pl.loop(0, n_chunks)
        def _(i):
            off = pl.multiple_of(i * SC_GATHER_ROWS, SC_GATHER_ROWS)
            pltpu.async_copy(table_hbm.at[idx_v.at[pl.ds(off, SC_GATHER_ROWS)]], rows_v, sem).wait()
            pltpu.sync_copy(rows_v, out_hbm.at[pl.ds(base + off, SC_GATHER_ROWS)])

    return gather(table, idx)


def _moe_kernel(lo_ref, hi_ref, nt_ref, x_ref, wr_ref, w1l_ref, w1h_ref, w3l_ref, w3h_ref, w2l_ref, w2h_ref, y_ref):
    j = pl.program_id(0)

    @pl.when(j < nt_ref[0])
    def _():
        x = jnp.concatenate([x_ref[:, sg, :] for sg in range(ROW_SEGS)], axis=-1).astype(BF16)
        wr = wr_ref[...]
        he_lo = (_silu(_dot(x, w1l_ref[0, 0])) * _dot(x, w3l_ref[0, 0]) * wr[:, 0:1]).astype(BF16)
        he_hi = (_silu(_dot(x, w1h_ref[0, 0])) * _dot(x, w3h_ref[0, 0]) * wr[:, 1:2]).astype(BF16)
        y = _dot(he_lo, w2l_ref[0, 0]) + _dot(he_hi, w2h_ref[0, 0])
        for sg in range(ROW_SEGS):
            y_ref[:, sg, :] = y[:, sg * LANES:(sg + 1) * LANES]

    @pl.when(j >= nt_ref[0])
    def _():
        y_ref[...] = jnp.zeros_like(y_ref)


def _moe_experts(xs, wrow, tile_lo, tile_hi, n_tiles, layer, w1, w3, w2):
    tp = xs.shape[0]
    tm = MOE_TILE
    d, de = w1.shape[2], w1.shape[3]

    def row_map(j, lo, hi, nt):
        return (jnp.minimum(j, nt[0] - 1), 0, 0)

    def gate_map(j, lo, hi, nt):
        return (jnp.minimum(j, nt[0] - 1), 0)

    def lo_map(j, lo, hi, nt):
        return (layer, lo[jnp.minimum(j, nt[0] - 1)], 0, 0)

    def hi_map(j, lo, hi, nt):
        return (layer, hi[jnp.minimum(j, nt[0] - 1)], 0, 0)

    up = (1, 1, d, de)
    down = (1, 1, de, d)
    return pl.pallas_call(
        _moe_kernel,
        out_shape=jax.ShapeDtypeStruct((tp, ROW_SEGS, LANES), F32),
        grid_spec=pltpu.PrefetchScalarGridSpec(
            num_scalar_prefetch=3,
            grid=(tp // tm,),
            in_specs=[pl.BlockSpec((tm, ROW_SEGS, LANES), row_map), pl.BlockSpec((tm, 2), gate_map),
                      pl.BlockSpec(up, lo_map), pl.BlockSpec(up, hi_map),
                      pl.BlockSpec(up, lo_map), pl.BlockSpec(up, hi_map),
                      pl.BlockSpec(down, lo_map), pl.BlockSpec(down, hi_map)],
            out_specs=pl.BlockSpec((tm, ROW_SEGS, LANES), lambda j, lo, hi, nt: (j, 0, 0)),
        ),
        compiler_params=_cparams(("arbitrary",)),
        name="moe_experts",
    )(tile_lo, tile_hi, n_tiles, xs, wrow, w1, w1, w3, w3, w2, w2)


def _moe(h2, logits, router_bias, layer, w1, w3, w2):
    t = h2.shape[0]
    tm = MOE_TILE
    n_tiles_max = t // tm + N_CLASSES
    tp = n_tiles_max * tm
    lg_t = logits[:, :N_EXPERTS].T.reshape(N_EXPERTS, t // LANES, LANES)
    cls, wlo, whi = _route(lg_t, router_bias)
    cls, wlo, whi = cls.reshape(t), wlo.reshape(t), whi.reshape(t)
    classes = jnp.arange(N_CLASSES, dtype=jnp.int32)
    onehot = (cls[:, None] == classes[None, :]).astype(jnp.int32)
    csum = jnp.cumsum(onehot, axis=0)
    rank = jnp.sum(csum * onehot, axis=1) - 1
    counts = csum[-1]
    tiles_per = (counts + tm - 1) // tm
    tile_end = jnp.cumsum(tiles_per)
    tile_start = tile_end - tiles_per
    dest = jnp.sum(onehot * (tile_start * tm)[None, :], axis=1) + rank
    tiles = jnp.arange(n_tiles_max, dtype=jnp.int32)
    tile_cls = jnp.minimum(jnp.sum((tile_end[None, :] <= tiles[:, None]).astype(jnp.int32), axis=1), N_CLASSES - 1)
    pair = tile_cls % 6
    base = (tile_cls // 6) * PER_GROUP
    pair_onehot = (pair[:, None] == jnp.arange(6, dtype=jnp.int32)[None, :]).astype(jnp.int32)
    tile_lo = base + jnp.sum(pair_onehot * jnp.asarray(PAIR_LO, jnp.int32)[None, :], axis=1)
    tile_hi = base + jnp.sum(pair_onehot * jnp.asarray(PAIR_HI, jnp.int32)[None, :], axis=1)
    per_token = jnp.stack([jnp.arange(t, dtype=jnp.int32), lax.bitcast_convert_type(wlo, jnp.int32),
                           lax.bitcast_convert_type(whi, jnp.int32)], axis=1)
    padding = jnp.stack([jnp.arange(tp, dtype=jnp.int32) % t, jnp.zeros((tp,), jnp.int32),
                         jnp.zeros((tp,), jnp.int32)], axis=1)
    per_row = padding.at[dest].set(per_token, unique_indices=True)
    tok = per_row[:, 0]
    wrow = lax.bitcast_convert_type(per_row[:, 1:3], F32)
    xs = _gather_rows(h2, tok)
    ys = _moe_experts(xs, wrow, tile_lo, tile_hi, tile_end[-1:], layer, w1, w3, w2)
    return _gather_rows(ys, dest)


def _final_kernel(s_ref, y_ref, mod_ref, g_ref, o_ref):
    x = s_ref[0] + mod_ref[0, 0, 5:6, :] * _token_rows(y_ref)
    o_ref[0] = _rms(x, g_ref[...])


def _final(s, y, mod, final_g):
    b, n, d = s.shape
    tm = ROW_TILE
    skip = N_CTX // tm
    row_spec = pl.BlockSpec((1, tm, d), lambda i, j: (i, j + skip, 0))
    return pl.pallas_call(
        _final_kernel,
        out_shape=jax.ShapeDtypeStruct((b, n - N_CTX, d), F32),
        grid=(b, (n - N_CTX) // tm),
        in_specs=[row_spec, pl.BlockSpec((1, tm, ROW_SEGS, LANES), lambda i, j: (i, j + skip, 0, 0)),
                  pl.BlockSpec((1, 1, 6, d), lambda i, j: (i, 1, 0, 0)),
                  pl.BlockSpec((1, d), lambda i, j: (0, 0))],
        out_specs=pl.BlockSpec((1, tm, d), lambda i, j: (i, j, 0)),
        compiler_params=_cparams(("arbitrary", "arbitrary")),
        name="final_norm",
    )(s, y, mod, final_g.reshape(1, d))


def _rope_tables(seq):
    pos = jnp.arange(seq)
    row_pos, col_pos = pos // GRID_W, pos % GRID_W
    n = A_DH // 2
    inv = ROPE_BASE ** (-jnp.arange(0, n, 2, dtype=F32) / n)
    ang_r = row_pos.astype(F32)[:, None] * inv[None, :]
    ang_c = col_pos.astype(F32)[:, None] * inv[None, :]
    ang = jnp.concatenate([ang_r, ang_c], axis=-1)
    cos = jnp.tile(jnp.cos(ang), (1, 4))
    sin = jnp.tile(jnp.sin(ang), (1, 4))
    sin = jnp.concatenate([-sin[:, :LANES // 2], sin[:, LANES // 2:]], axis=-1)
    cos = jnp.concatenate([jnp.ones((N_CTX, LANES), F32), cos], axis=0)
    sin = jnp.concatenate([jnp.zeros((N_CTX, LANES), F32), sin], axis=0)
    return cos, sin


def _interleave_maps(w):
    d = w.shape[0]
    w = w.reshape(d, A_HEADS, 2, 2, 2, A_DH // 4)
    return w.transpose(0, 1, 4, 2, 3, 5).reshape(d, A_HEADS * LANES)


def _att_weights(w_in):
    a_qk = A_HEADS * 2 * A_DH
    a_v = A_HEADS * A_DV
    b_w = B_HEADS * B_DH
    qa = _interleave_maps(w_in[:, :a_qk]) * (A_DH ** -0.5)
    ka = _interleave_maps(w_in[:, a_qk:2 * a_qk])
    va = w_in[:, 2 * a_qk:2 * a_qk + a_v]
    o = 2 * a_qk + a_v
    qb = w_in[:, o:o + b_w] * (B_DH ** -0.5)
    rest = w_in[:, o + b_w:]
    return jnp.concatenate([qa, ka, va, qb, rest], axis=1).astype(BF16)


def _gla_weights(w_in):
    d = w_in.shape[0]
    n_main = 2 * C_HEADS * C_DK + 2 * C_HEADS * C_DV
    pad = jnp.zeros((d, LANES - 2 * C_RANK), w_in.dtype)
    return jnp.concatenate([w_in, pad], axis=1).astype(BF16), n_main


def kernel(x, c, ctx, c_ctx, w_mod, b_mod, norm_g, final_g, att_w_in, att_w_out, att_lambda, att_subln_g, na_bias,
           gla_w_in, gla_w_gate, gla_b_gate, gla_norm_g, gla_w_out, router_w, router_bias, moe_w1, moe_w3, moe_w2):
    b, seq, d = x.shape
    n = N_CTX + seq
    s = jnp.concatenate([ctx, x], axis=1)

    rows = b + 1
    rows_pad = -(-rows // 8) * 8
    cc = jnp.concatenate([c, c_ctx[None, :], jnp.zeros((rows_pad - rows, d), F32)], axis=0)
    mod_all = _mod_vectors(cc, w_mod, b_mod)
    mod_x = mod_all[:, :b].reshape(DEPTH, b, 1, 6, d)
    mod_c = jnp.broadcast_to(mod_all[:, b].reshape(DEPTH, 1, 1, 6, d), (DEPTH, b, 1, 6, d))
    mods = jnp.concatenate([mod_c, mod_x], axis=2)

    cos, sin = _rope_tables(seq)
    rw_hi = router_w.astype(BF16)
    rw_lo = (router_w - rw_hi.astype(F32)).astype(BF16)
    zpad = jnp.zeros((d, LANES - 2 * N_EXPERTS), BF16)
    rw1 = jnp.concatenate([rw_hi, rw_lo, zpad], axis=1)
    rw2 = jnp.concatenate([rw_hi, jnp.zeros((d, N_EXPERTS), BF16), zpad], axis=1)

    w1_b, w3_b, w2_b = moe_w1.astype(BF16), moe_w3.astype(BF16), moe_w2.astype(BF16)
    y = None
    for i in range(DEPTH):
        j = i // 2
        modp = mods[i - 1] if i else None
        if i % 2 == 0:
            lam_init = 0.8 - 0.6 * math.exp(-0.3 * i)
            w = _att_weights(att_w_in[j])
            outs = _project(s, y, modp, mods[i], norm_g[i, 0], w, cos, sin,
                            n_rope=2 * A_HEADS * LANES, n_bf16=w.shape[1])
            if i:
                s = outs[0]
            p = outs[-1]
            oa = _diff_attention(p, att_lambda[j], att_subln_g[j], lam_init)
            ob_x, ob_c = _neighbourhood_attention(p, _na_bias_table(na_bias[j], seq // GRID_W))
            s, h2, logits = _out_project((oa, ob_x, ob_c), att_w_out[j].astype(BF16), s, mods[i], norm_g[i, 1],
                                         rw1, rw2, gla=False)
        else:
            w, n_main = _gla_weights(gla_w_in[j])
            outs = _project(s, y, modp, mods[i], norm_g[i, 0], w, None, None, n_rope=0, n_bf16=n_main)
            s, p, lr = outs
            wg = jnp.zeros((2, LANES, C_HEADS * C_DK), F32)
            wg = wg.at[0, :C_RANK].set(gla_w_gate[j, 0]).at[1, C_RANK:2 * C_RANK].set(gla_w_gate[j, 1])
            o = _gla(p, lr, wg, gla_b_gate[j])
            s, h2, logits = _out_project((o, p), gla_w_out[j].astype(BF16), s, mods[i], norm_g[i, 1],
                                         rw1, rw2, gla=True, gn=gla_norm_g[j])
        y = _moe(h2.reshape(b * n, ROW_SEGS, LANES), logits.reshape(b * n, LANES), router_bias,
                 i, w1_b, w3_b, w2_b)
        y = y.reshape(b, n, ROW_SEGS, LANES)
    return _final(s, y, mods[DEPTH - 1], final_g)
```

```python
import functools
import math

import jax
import jax.numpy as jnp
import numpy as np
from jax import lax
from jax.experimental import pallas as pl
from jax.experimental.pallas import tpu as pltpu
from jax.experimental.pallas import tpu_sc as plsc

F32 = jnp.float32
BF16 = jnp.bfloat16

D_MODEL = 1024
DEPTH = 4
GRID_W = 64
N_CTX = 256
A_HEADS = 4
A_DH = 64
A_DV = 128
B_HEADS = 8
B_DH = 64
NA_ROWS = 8
NA_COLS = 16
C_HEADS = 4
C_DK = 128
C_DV = 256
C_RANK = 16
C_TAU = 16.0
C_CHUNK = 64
GLA_BLOCK = 256
N_EXPERTS = 16
N_GROUPS = 4
PER_GROUP = 4
D_EXPERT = 512
ROPE_BASE = 10000.0
EPS = 1e-6

LANES = 128
ROW_TILE = 256
MOE_TILE = 256
SC_CORES = 2
SC_SUBCORES = 16
SC_WORKERS = SC_CORES * SC_SUBCORES
SC_GATHER_ROWS = 64
N_CLASSES = N_GROUPS * 6
NA_QROWS = 8
NA_KROWS = NA_QROWS + NA_ROWS - 1
VMEM_LIMIT = 52 * 1024 * 1024
ROW_SEGS = D_MODEL // LANES

PAIR_LO = (0, 0, 0, 1, 1, 2)
PAIR_HI = (1, 2, 3, 2, 3, 3)


def _cparams(sem):
    return pltpu.CompilerParams(dimension_semantics=sem, vmem_limit_bytes=VMEM_LIMIT)


def _sigmoid(x):
    return 1.0 / (1.0 + jnp.exp(-x))


def _silu(x):
    return x * _sigmoid(x)


def _rms(x, g):
    return x * lax.rsqrt(jnp.mean(x * x, axis=-1, keepdims=True) + EPS) * g


def _dot(a, b):
    return jnp.dot(a, b, preferred_element_type=F32)


def _dot_nt(a, b):
    return lax.dot_general(a, b, (((1,), (1,)), ((), ())), preferred_element_type=F32)


def _dot_tn(a, b):
    return lax.dot_general(a, b, (((0,), (0,)), ((), ())), preferred_element_type=F32)


def _mod_kernel(c_ref, w_ref, b_ref, o_ref):
    a = _silu(c_ref[...]).astype(BF16)
    o_ref[0] = _dot(a, w_ref[0].astype(BF16)) + b_ref[0]


def _mod_vectors(cc, w_mod, b_mod):
    depth, d, n6 = w_mod.shape
    rows = cc.shape[0]
    tn = 1536
    return pl.pallas_call(
        _mod_kernel,
        out_shape=jax.ShapeDtypeStruct((depth, rows, n6), F32),
        grid=(depth, n6 // tn),
        in_specs=[
            pl.BlockSpec((rows, d), lambda i, j: (0, 0)),
            pl.BlockSpec((1, d, tn), lambda i, j: (i, 0, j)),
            pl.BlockSpec((1, 1, tn), lambda i, j: (i, 0, j)),
        ],
        out_specs=pl.BlockSpec((1, rows, tn), lambda i, j: (i, 0, j)),
        compiler_params=_cparams(("arbitrary", "arbitrary")),
        name="mod_vectors",
    )(cc, w_mod, b_mod.reshape(depth, 1, n6))


def _token_rows(ref):
    rows = ref.shape[0] // ROW_SEGS
    return jnp.concatenate([ref[pl.ds(sg, rows, stride=ROW_SEGS), :] for sg in range(ROW_SEGS)], axis=-1)


def _store_token_rows(ref, val):
    rows = val.shape[0]
    for sg in range(ROW_SEGS):
        ref[pl.ds(sg, rows, stride=ROW_SEGS), :] = val[:, sg * LANES:(sg + 1) * LANES]


def _proj_kernel(*refs, has_y, n_rope, n_bf16, col_chunk):
    it = iter(refs)
    s_ref = next(it)
    y_ref = next(it) if has_y else None
    modp_ref = next(it) if has_y else None
    mod_ref = next(it)
    ng_ref = next(it)
    w_ref = next(it)
    cos_ref = next(it) if n_rope else None
    sin_ref = next(it) if n_rope else None
    snew_ref = next(it) if has_y else None
    p_ref = next(it)
    lr_ref = next(it) if w_ref.shape[1] > n_bf16 else None

    x = s_ref[0]
    if has_y:
        x = x + modp_ref[0, 0, 5:6, :] * _token_rows(y_ref)
        snew_ref[0] = x
    h = _rms(x, ng_ref[...])
    h = h * (1.0 + mod_ref[0, 0, 1:2, :]) + mod_ref[0, 0, 0:1, :]
    hb = h.astype(BF16)
    n_out = w_ref.shape[1]
    for c0 in range(0, n_out, col_chunk):
        c1 = min(c0 + col_chunk, n_out)
        acc = _dot(hb, w_ref[:, c0:c1])
        for b0 in range(c0, c1, LANES):
            t = acc[:, b0 - c0:b0 - c0 + LANES]
            if b0 < n_rope:
                t = t * cos_ref[...] + pltpu.roll(t, LANES // 2, 1) * sin_ref[...]
            if b0 < n_bf16:
                p_ref[0, :, b0:b0 + LANES] = t.astype(BF16)
            else:
                lr_ref[0, :, b0 - n_bf16:b0 - n_bf16 + LANES] = t


def _project(s, y, modp, mod, ng, w, cos, sin, *, n_rope, n_bf16):
    b, n, d = s.shape
    n_out = w.shape[1]
    tm = ROW_TILE
    has_y = y is not None
    row_spec = pl.BlockSpec((1, tm, d), lambda i, j: (i, j, 0))
    mod_spec = pl.BlockSpec((1, 1, 6, d), lambda i, j: (i, jnp.minimum(j, 1), 0, 0))
    in_specs = [row_spec]
    args = [s]
    if has_y:
        in_specs += [pl.BlockSpec((tm * ROW_SEGS, LANES), lambda i, j: (i * (n // tm) + j, 0)), mod_spec]
        args += [y, modp]
    in_specs += [mod_spec, pl.BlockSpec((1, d), lambda i, j: (0, 0)),
                 pl.BlockSpec((d, n_out), lambda i, j: (0, 0))]
    args += [mod, ng.reshape(1, d), w]
    if n_rope:
        tab_spec = pl.BlockSpec((tm, LANES), lambda i, j: (j, 0))
        in_specs += [tab_spec, tab_spec]
        args += [cos, sin]
    out_shape, out_specs = [], []
    if has_y:
        out_shape.append(jax.ShapeDtypeStruct((b, n, d), F32))
        out_specs.append(row_spec)
    out_shape.append(jax.ShapeDtypeStruct((b, n, n_bf16), BF16))
    out_specs.append(pl.BlockSpec((1, tm, n_bf16), lambda i, j: (i, j, 0)))
    if n_out > n_bf16:
        out_shape.append(jax.ShapeDtypeStruct((b, n, n_out - n_bf16), F32))
        out_specs.append(pl.BlockSpec((1, tm, n_out - n_bf16), lambda i, j: (i, j, 0)))
    return pl.pallas_call(
        functools.partial(_proj_kernel, has_y=has_y, n_rope=n_rope, n_bf16=n_bf16, col_chunk=512),
        out_shape=out_shape,
        grid=(b, n // tm),
        in_specs=in_specs,
        out_specs=out_specs,
        compiler_params=_cparams(("arbitrary", "arbitrary")),
        name="norm_mod_project",
    )(*args)


def _softmax_rows(s):
    e = jnp.exp(s - jnp.max(s, axis=-1, keepdims=True))
    return e * (1.0 / jnp.sum(e, axis=-1, keepdims=True))


def _diff_attn_kernel(q_ref, k_ref, v_ref, lam_ref, g_ref, o_ref, *, lam_init):
    lp = lam_ref[...]
    lam = (jnp.exp(jnp.sum(lp[0:1] * lp[1:2], axis=-1, keepdims=True))
           - jnp.exp(jnp.sum(lp[2:3] * lp[3:4], axis=-1, keepdims=True)) + lam_init)
    q = q_ref[0]
    lane = lax.broadcasted_iota(jnp.int32, (1, LANES), 1)
    first_map = (lane // 32) % 2 == 0
    zero = jnp.zeros_like(q)
    q0 = jnp.where(first_map, q, zero)
    q1 = jnp.where(first_map, zero, q)

    def attend(n_keys):
        k = k_ref[0, 0:n_keys, :]
        v = v_ref[0, 0:n_keys, :]
        a = _softmax_rows(_dot_nt(q0, k)) - lam * _softmax_rows(_dot_nt(q1, k))
        o = _dot(a.astype(BF16), v)
        o_ref[0] = (_rms(o, g_ref[...]) * (1.0 - lam_init)).astype(BF16)

    qi = pl.program_id(2)

    @pl.when(qi == 0)
    def _():
        attend(N_CTX)

    @pl.when(qi > 0)
    def _():
        attend(k_ref.shape[1])


def _diff_attention(p, lam_p, subln_g, lam_init):
    b, n, _ = p.shape
    tq = ROW_TILE
    return pl.pallas_call(
        functools.partial(_diff_attn_kernel, lam_init=lam_init),
        out_shape=jax.ShapeDtypeStruct((b, n, A_HEADS * A_DV), BF16),
        grid=(b, A_HEADS, n // tq),
        in_specs=[
            pl.BlockSpec((1, tq, LANES), lambda i, h, j: (i, j, h)),
            pl.BlockSpec((1, n, LANES), lambda i, h, j: (i, 0, A_HEADS + h)),
            pl.BlockSpec((1, n, LANES), lambda i, h, j: (i, 0, 2 * A_HEADS + h)),
            pl.BlockSpec((4, A_DH), lambda i, h, j: (0, 0)),
            pl.BlockSpec((1, A_DV), lambda i, h, j: (0, 0)),
        ],
        out_specs=pl.BlockSpec((1, tq, LANES), lambda i, h, j: (i, j, h)),
        compiler_params=_cparams(("arbitrary", "arbitrary", "arbitrary")),
        name="diff_attention",
    )(p, p, p, lam_p, subln_g.reshape(1, A_DV))


def _split_heads(q):
    lane = lax.broadcasted_iota(jnp.int32, (1, LANES), 1)
    first_head = lane < B_DH
    zero = jnp.zeros_like(q)
    return first_head, (jnp.where(first_head, q, zero), jnp.where(first_head, zero, q))


def _na_ctx_kernel(q_ref, k_ref, v_ref, o_ref):
    first_head, qh = _split_heads(q_ref[0])
    kc = k_ref[0]
    vc = v_ref[0]
    outs = [_dot(_softmax_rows(_dot_nt(qh[h], kc)).astype(BF16), vc) for h in range(2)]
    o_ref[0] = jnp.where(first_head, outs[0], outs[1]).astype(BF16)


def _na_kernel(qa_ref, qb_ref, k_ref, v_ref, bm_ref, o_ref):
    blk = pl.program_id(2)
    first_head, qh = _split_heads(jnp.concatenate([qa_ref[0], qb_ref[0]], axis=0))
    kc = k_ref[0, 0:N_CTX, :]
    vc = v_ref[0, 0:N_CTX, :]
    rows = (k_ref.shape[1] - N_CTX) // GRID_W
    k_row0 = jnp.clip(blk * NA_QROWS - NA_ROWS // 2, 0, rows - NA_KROWS)
    start = pl.multiple_of(N_CTX + k_row0 * GRID_W, GRID_W)
    kw = k_ref[0, pl.ds(start, NA_KROWS * GRID_W), :]
    vw = v_ref[0, pl.ds(start, NA_KROWS * GRID_W), :]
    outs = []
    for h in range(2):
        s_loc = _dot_nt(qh[h], kw) + bm_ref[h, 0]
        s_ctx = _dot_nt(qh[h], kc)
        m = jnp.maximum(jnp.max(s_loc, axis=-1, keepdims=True), jnp.max(s_ctx, axis=-1, keepdims=True))
        e_loc = jnp.exp(s_loc - m)
        e_ctx = jnp.exp(s_ctx - m)
        den = jnp.sum(e_loc, axis=-1, keepdims=True) + jnp.sum(e_ctx, axis=-1, keepdims=True)
        outs.append((_dot(e_loc.astype(BF16), vw) + _dot(e_ctx.astype(BF16), vc)) * (1.0 / den))
    o_ref[0] = jnp.where(first_head, outs[0], outs[1]).astype(BF16)


def _na_bias_table(na_bias, rows):
    h = na_bias.shape[0]
    n_dr, n_dc = 2 * NA_ROWS - 1, 2 * NA_COLS - 1
    width = 2 * GRID_W
    left = GRID_W - NA_COLS
    u = jnp.pad(na_bias, ((0, 0), (0, 0), (left, width - left - n_dc)))
    skew = jnp.tile(u, (1, 1, GRID_W))[:, :, :GRID_W * (width - 1)].reshape(h, n_dr, GRID_W, width - 1)
    toeplitz = skew[:, :, :, GRID_W - 1:]
    margin = NA_KROWS - NA_ROWS
    by_col = jnp.pad(toeplitz.transpose(0, 2, 1, 3), ((0, 0), (0, 0), (margin, margin), (0, 0)))
    by_col = by_col.reshape(h, GRID_W, (n_dr + 2 * margin) * GRID_W)
    col = np.arange(GRID_W)
    col_start = np.clip(col - NA_COLS // 2, 0, GRID_W - NA_COLS)
    col_ok = (col[None, :] >= col_start[:, None]) & (col[None, :] < col_start[:, None] + NA_COLS)
    blocks = []
    for q_row0 in (0, NA_QROWS, rows - NA_QROWS):
        k_row0 = int(np.clip(q_row0 - NA_ROWS // 2, 0, rows - NA_KROWS))
        kr = k_row0 + np.arange(NA_KROWS)
        for rq in range(NA_QROWS):
            r = q_row0 + rq
            r0 = int(np.clip(r - NA_ROWS // 2, 0, rows - NA_ROWS))
            row_ok = (kr >= r0) & (kr < r0 + NA_ROWS)
            ok = (col_ok[:, None, :] & row_ok[None, :, None]).reshape(GRID_W, NA_KROWS * GRID_W)
            first = k_row0 - r + NA_ROWS - 1 + margin
            window = by_col[:, :, first * GRID_W:(first + NA_KROWS) * GRID_W]
            blocks.append(jnp.where(ok[None], window, -jnp.inf))
    table = jnp.stack(blocks, axis=1)
    return table.reshape(h, 3, NA_QROWS * GRID_W, NA_KROWS * GRID_W)


def _neighbourhood_attention(p, bm):
    b, n, _ = p.shape
    tq = NA_QROWS * GRID_W
    half = tq // 2
    n_blocks = (n - N_CTX) // tq
    ctx_blocks = N_CTX // half
    col0 = 3 * A_HEADS
    pairs = B_HEADS // 2
    width = B_HEADS * B_DH

    def pattern(j):
        return jnp.minimum(j, 1) + (j == n_blocks - 1).astype(jnp.int32)

    ob_x = pl.pallas_call(
        _na_kernel,
        out_shape=jax.ShapeDtypeStruct((b, n - N_CTX, width), BF16),
        grid=(b, pairs, n_blocks),
        in_specs=[
            pl.BlockSpec((1, half, LANES), lambda i, h, j: (i, ctx_blocks + 2 * j, col0 + h)),
            pl.BlockSpec((1, half, LANES), lambda i, h, j: (i, ctx_blocks + 2 * j + 1, col0 + h)),
            pl.BlockSpec((1, n, LANES), lambda i, h, j: (i, 0, col0 + pairs + h)),
            pl.BlockSpec((1, n, LANES), lambda i, h, j: (i, 0, col0 + 2 * pairs + h)),
            pl.BlockSpec((2, 1, tq, NA_KROWS * GRID_W), lambda i, h, j: (h, pattern(j), 0, 0)),
        ],
        out_specs=pl.BlockSpec((1, tq, LANES), lambda i, h, j: (i, j, h)),
        compiler_params=_cparams(("arbitrary", "arbitrary", "arbitrary")),
        name="neighbourhood_attention",
    )(p, p, p, p, bm)
    ob_c = pl.pallas_call(
        _na_ctx_kernel,
        out_shape=jax.ShapeDtypeStruct((b, N_CTX, width), BF16),
        grid=(b, pairs),
        in_specs=[
            pl.BlockSpec((1, N_CTX, LANES), lambda i, h: (i, 0, col0 + h)),
            pl.BlockSpec((1, N_CTX, LANES), lambda i, h: (i, 0, col0 + pairs + h)),
            pl.BlockSpec((1, N_CTX, LANES), lambda i, h: (i, 0, col0 + 2 * pairs + h)),
        ],
        out_specs=pl.BlockSpec((1, N_CTX, LANES), lambda i, h: (i, 0, h)),
        compiler_params=_cparams(("arbitrary", "arbitrary")),
        name="context_attention",
    )(p, p, p)
    return ob_x, ob_c


def _gla_kernel(q_ref, k_ref, v_ref, lr_ref, wg_ref, bg_ref, o_ref, ob_ref, g_ref):
    n = q_ref.shape[1]
    blk = GLA_BLOCK
    per_blk = blk // C_CHUNK
    n_blocks = n // blk
    ri = lax.broadcasted_iota(jnp.int32, (blk, blk), 0)
    ci = lax.broadcasted_iota(jnp.int32, (blk, blk), 1)
    same_chunk = (ri // C_CHUNK) == (ci // C_CHUNK)
    keeps = (same_chunk & (ci <= ri), same_chunk & (ci >= ri))
    tris = tuple(jnp.where(kp, 1.0, 0.0).astype(BF16) for kp in keeps)
    w_gate = jnp.concatenate([wg_ref[0], wg_ref[1]], axis=1).astype(BF16)
    b_gate = jnp.concatenate([bg_ref[0:1, :], bg_ref[1:2, :]], axis=1)
    for i in range(n_blocks):
        z = _dot(lr_ref[0, i * blk:(i + 1) * blk, :].astype(BF16), w_gate) + b_gate
        g_ref[i * blk:(i + 1) * blk, :] = (jnp.minimum(z, 0.0) - jnp.log(1.0 + jnp.exp(-jnp.abs(z)))) * (1.0 / C_TAU)

    def block(sb, direction, st):
        keep = keeps[direction]
        end_row = C_CHUNK - 1 if direction == 0 else 0
        mid_row = C_CHUNK // 2 - 1 if direction == 0 else C_CHUNK // 2
        r0 = sb * blk
        g = g_ref[r0:r0 + blk, direction * C_DK:(direction + 1) * C_DK]
        g_hi = g.astype(BF16)
        g_lo = (g - g_hi.astype(F32)).astype(BF16)
        gc2 = _dot(tris[direction], jnp.concatenate([g_hi, g_lo], axis=1))
        gc = (gc2[:, :C_DK] + gc2[:, C_DK:]).reshape(per_blk, C_CHUNK, C_DK)
        g_end = gc[:, end_row:end_row + 1, :]
        g_mid = gc[:, mid_row:mid_row + 1, :]
        q = (q_ref[0, r0:r0 + blk, :].astype(F32) * (C_DK ** -0.5)).reshape(per_blk, C_CHUNK, C_DK)
        k = k_ref[0, r0:r0 + blk, :].astype(F32).reshape(per_blk, C_CHUNK, C_DK)
        v = v_ref[0, r0:r0 + blk, :]
        q_in = (q * jnp.exp(gc)).astype(BF16).reshape(blk, C_DK)
        q_mid = (q * jnp.exp(gc - g_mid)).astype(BF16).reshape(blk, C_DK)
        k_mid = (k * jnp.exp(g_mid - gc)).astype(BF16).reshape(blk, C_DK)
        k_end = (k * jnp.exp(g_end - gc)).astype(BF16).reshape(blk, C_DK)
        a = jnp.where(keep, _dot_nt(q_mid, k_mid), 0.0)
        o_intra = _dot(a.astype(BF16), v)
        ends = jnp.concatenate([g_end.reshape(per_blk, C_DK), jnp.zeros((8 - per_blk, C_DK), F32)], axis=0)
        decay = jnp.transpose(jnp.exp(ends))
        o_inter = [None] * per_blk
        for c in (range(per_blk) if direction == 0 else reversed(range(per_blk))):
            rows = slice(c * C_CHUNK, (c + 1) * C_CHUNK)
            o_inter[c] = _dot(q_in[rows], st.astype(BF16))
            st = st * decay[:, c:c + 1] + _dot_tn(k_end[rows], v[rows])
        return o_intra + jnp.concatenate(o_inter, axis=0), st

    st_f = jnp.zeros((C_DK, C_DV), F32)
    st_b = jnp.zeros((C_DK, C_DV), F32)
    ctx_blocks = N_CTX // blk
    order_b = list(reversed(range(ctx_blocks))) + list(reversed(range(ctx_blocks, n_blocks)))
    for i in range(n_blocks):
        o, st_f = block(i, 0, st_f)
        o_ref[0, i * blk:(i + 1) * blk, :] = o
        sb = order_b[i]
        o, st_b = block(sb, 1, st_b)
        ob_ref[sb * blk:(sb + 1) * blk, :] = o
    o_ref[0] = o_ref[0] + ob_ref[...]


def _gla(p, lr, wg, bg):
    b, n, _ = p.shape
    return pl.pallas_call(
        _gla_kernel,
        out_shape=jax.ShapeDtypeStruct((b, n, C_HEADS * C_DV), F32),
        grid=(b, C_HEADS),
        in_specs=[
            pl.BlockSpec((1, n, C_DK), lambda i, h: (i, 0, h)),
            pl.BlockSpec((1, n, C_DK), lambda i, h: (i, 0, C_HEADS + h)),
            pl.BlockSpec((1, n, C_DV), lambda i, h: (i, 0, C_HEADS + h)),
            pl.BlockSpec((1, n, LANES), lambda i, h: (i, 0, 0)),
            pl.BlockSpec((2, LANES, C_DK), lambda i, h: (0, 0, h)),
            pl.BlockSpec((2, C_DK), lambda i, h: (0, h)),
        ],
        out_specs=pl.BlockSpec((1, n, C_DV), lambda i, h: (i, 0, h)),
        scratch_shapes=[pltpu.VMEM((n, C_DV), F32), pltpu.VMEM((n, 2 * C_DK), F32)],
        compiler_params=_cparams(("arbitrary", "arbitrary")),
        name="gla",
    )(p, p, p, lr, wg, bg)


def _out_kernel(*refs, gla):
    if gla:
        o_ref, gate_ref, gn_ref, w_ref, s_ref, mod_ref, ng_ref, rw1_ref, rw2_ref, snew_ref, h2_ref, lg_ref = refs
        o = o_ref[0]
        gate = gate_ref[0].astype(F32)
        parts = []
        for hd in range(C_HEADS):
            oh = _rms(o[:, hd * C_DV:(hd + 1) * C_DV], gn_ref[...])
            parts.append((oh * _silu(gate[:, hd * C_DV:(hd + 1) * C_DV])).astype(BF16))
        acc = _dot(jnp.concatenate(parts, axis=-1), w_ref[...])
    else:
        oa_ref, obx_ref, obc_ref, w_ref, s_ref, mod_ref, ng_ref, rw1_ref, rw2_ref, snew_ref, h2_ref, lg_ref = refs
        half = oa_ref.shape[2]
        ob = jnp.where(pl.program_id(1) == 0, obc_ref[0], obx_ref[0])
        acc = _dot(oa_ref[0], w_ref[0:half, :]) + _dot(ob, w_ref[half:, :])
    x = s_ref[0] + mod_ref[0, 0, 2:3, :] * acc
    snew_ref[0] = x
    h2 = _rms(x, ng_ref[...]) * (1.0 + mod_ref[0, 0, 4:5, :]) + mod_ref[0, 0, 3:4, :]
    _store_token_rows(h2_ref, h2)
    hi = h2.astype(BF16)
    lo = (h2 - hi.astype(F32)).astype(BF16)
    t = _dot(hi, rw1_ref[...])
    lg_ref[0] = t + pltpu.roll(t, LANES - N_EXPERTS, 1) + _dot(lo, rw2_ref[...])


def _out_project(mix, w_out, s, mod, ng, rw1, rw2, *, gla, gn=None):
    b, n, d = s.shape
    tm = ROW_TILE
    row_spec = pl.BlockSpec((1, tm, d), lambda i, j: (i, j, 0))
    const2 = lambda i, j: (0, 0)
    if gla:
        o, p = mix
        in_specs = [row_spec, pl.BlockSpec((1, tm, d), lambda i, j: (i, j, 2)),
                    pl.BlockSpec((1, C_DV), const2)]
        args = [o, p, gn.reshape(1, C_DV)]
    else:
        oa, ob_x, ob_c = mix
        half = oa.shape[2]
        in_specs = [pl.BlockSpec((1, tm, half), lambda i, j: (i, j, 0)),
                    pl.BlockSpec((1, tm, half), lambda i, j: (i, jnp.maximum(j - 1, 0), 0)),
                    pl.BlockSpec((1, tm, half), lambda i, j: (i, 0, 0))]
        args = [oa, ob_x, ob_c]
    in_specs += [pl.BlockSpec((d, d), const2), row_spec,
                 pl.BlockSpec((1, 1, 6, d), lambda i, j: (i, jnp.minimum(j, 1), 0, 0)),
                 pl.BlockSpec((1, d), const2), pl.BlockSpec((d, LANES), const2), pl.BlockSpec((d, LANES), const2)]
    args += [w_out, s, mod, ng.reshape(1, d), rw1, rw2]
    return pl.pallas_call(
        functools.partial(_out_kernel, gla=gla),
        out_shape=[jax.ShapeDtypeStruct((b, n, d), F32), jax.ShapeDtypeStruct((b * n * ROW_SEGS, LANES), F32),
                   jax.ShapeDtypeStruct((b, n, LANES), F32)],
        grid=(b, n // tm),
        in_specs=in_specs,
        out_specs=[row_spec, pl.BlockSpec((tm * ROW_SEGS, LANES), lambda i, j: (i * (n // tm) + j, 0)),
                   pl.BlockSpec((1, tm, LANES), lambda i, j: (i, j, 0))],
        compiler_params=_cparams(("arbitrary", "arbitrary")),
        name="out_project",
    )(*args)


def _route_kernel(bias_ref, lg_ref, cls_ref, wlo_ref, whi_ref):
    score = [_sigmoid(lg_ref[e]) for e in range(N_EXPERTS)]
    sel = [score[e] + bias_ref[e] for e in range(N_EXPERTS)]
    grp_score = []
    for g in range(N_GROUPS):
        v = sel[g * PER_GROUP:(g + 1) * PER_GROUP]
        best = v[0] + v[1]
        for a in range(PER_GROUP):
            for c in range(a + 1, PER_GROUP):
                if (a, c) != (0, 1):
                    best = jnp.maximum(best, v[a] + v[c])
        grp_score.append(best)
    grp = jnp.zeros(grp_score[0].shape, jnp.int32)
    best = grp_score[0]
    for g in range(1, N_GROUPS):
        upd = grp_score[g] > best
        best = jnp.where(upd, grp_score[g], best)
        grp = jnp.where(upd, g, grp)

    def pick(vals, j):
        out = vals[j]
        for g in range(1, N_GROUPS):
            out = jnp.where(grp == g, vals[g * PER_GROUP + j], out)
        return out

    v = [pick(sel, j) for j in range(PER_GROUP)]
    sc = [pick(score, j) for j in range(PER_GROUP)]
    one = jnp.ones(grp.shape, jnp.int32)
    zero = jnp.zeros(grp.shape, jnp.int32)
    chosen = []
    for j in range(PER_GROUP):
        rank = zero
        for m in range(PER_GROUP):
            if m == j:
                continue
            ahead = (v[m] >= v[j]) if m < j else (v[m] > v[j])
            rank = rank + jnp.where(ahead, one, zero)
        chosen.append(rank < 2)
    code = zero
    for j in range(PER_GROUP):
        code = code + jnp.where(chosen[j], one * (1 << j), zero)
    pair = zero
    for idx in range(6):
        pair = jnp.where(code == (1 << PAIR_LO[idx]) + (1 << PAIR_HI[idx]), idx, pair)
    s_lo = jnp.where(chosen[0], sc[0], jnp.where(chosen[1], sc[1], sc[2]))
    s_hi = jnp.where(chosen[3], sc[3], jnp.where(chosen[2], sc[2], sc[1]))
    den = s_lo + s_hi
    cls_ref[...] = grp * 6 + pair
    wlo_ref[...] = s_lo / den
    whi_ref[...] = s_hi / den


def _route(logits_t, router_bias):
    _, r, _ = logits_t.shape
    full = pl.BlockSpec((r, LANES), lambda i: (0, 0))
    return pl.pallas_call(
        _route_kernel,
        out_shape=[jax.ShapeDtypeStruct((r, LANES), jnp.int32), jax.ShapeDtypeStruct((r, LANES), F32),
                   jax.ShapeDtypeStruct((r, LANES), F32)],
        grid=(1,),
        in_specs=[pl.BlockSpec(memory_space=pltpu.SMEM),
                  pl.BlockSpec((N_EXPERTS, r, LANES), lambda i: (0, 0, 0))],
        out_specs=[full, full, full],
        compiler_params=_cparams(("arbitrary",)),
        name="route",
    )(router_bias, logits_t)


def _gather_rows(table, idx):
    p = idx.shape[0]
    per_worker = p // SC_WORKERS
    n_chunks = per_worker // SC_GATHER_ROWS
    mesh = plsc.VectorSubcoreMesh(core_axis_name="c", subcore_axis_name="s", num_cores=SC_CORES,
                                  num_subcores=SC_SUBCORES)

    @functools.partial(
        pl.kernel, mesh=mesh,
        out_type=jax.ShapeDtypeStruct((p,) + table.shape[1:], table.dtype),
        scratch_types=[pltpu.VMEM((per_worker,), jnp.int32),
                       pltpu.VMEM((SC_GATHER_ROWS,) + table.shape[1:], table.dtype),
                       pltpu.SemaphoreType.DMA],
        compiler_params=pltpu.CompilerParams(use_tc_tiling_on_sc=True),
        name="gather_rows",
    )
    def gather(table_hbm, idx_hbm, out_hbm, idx_v, rows_v, sem):
        worker = lax.axis_index("s") * SC_CORES + lax.axis_index("c")
        base = worker * per_worker
        pltpu.sync_copy(idx_hbm.at[pl.ds(base, per_worker)], idx_v)

        @pl.loop(0, n_chunks)
        def _(i):
            off = pl.multiple_of(i * SC_GATHER_ROWS, SC_GATHER_ROWS)
            pltpu.async_copy(table_hbm.at[idx_v.at[pl.ds(off, SC_GATHER_ROWS)]], rows_v, sem).wait()
            pltpu.sync_copy(rows_v, out_hbm.at[pl.ds(base + off, SC_GATHER_ROWS)])

    return gather(table, idx)


def _moe_kernel(lo_ref, hi_ref, nt_ref, x_ref, wr_ref, w1l_ref, w1h_ref, w3l_ref, w3h_ref, w2l_ref, w2h_ref, y_ref):
    j = pl.program_id(0)

    @pl.when(j < nt_ref[0])
    def _():
        x = _token_rows(x_ref).astype(BF16)
        wr = wr_ref[...]
        he_lo = (_silu(_dot(x, w1l_ref[0, 0])) * _dot(x, w3l_ref[0, 0]) * wr[:, 0:1]).astype(BF16)
        he_hi = (_silu(_dot(x, w1h_ref[0, 0])) * _dot(x, w3h_ref[0, 0]) * wr[:, 1:2]).astype(BF16)
        y = _dot(he_lo, w2l_ref[0, 0]) + _dot(he_hi, w2h_ref[0, 0])
        _store_token_rows(y_ref, y)

    @pl.when(j >= nt_ref[0])
    def _():
        y_ref[...] = jnp.zeros_like(y_ref)


def _moe_experts(xs, wrow, tile_lo, tile_hi, n_tiles, layer, w1, w3, w2):
    tp = xs.shape[0] // ROW_SEGS
    tm = MOE_TILE
    d, de = w1.shape[2], w1.shape[3]

    def gate_map(j, lo, hi, nt):
        return (jnp.minimum(j, nt[0] - 1), 0)

    def lo_map(j, lo, hi, nt):
        return (layer, lo[jnp.minimum(j, nt[0] - 1)], 0, 0)

    def hi_map(j, lo, hi, nt):
        return (layer, hi[jnp.minimum(j, nt[0] - 1)], 0, 0)

    up = (1, 1, d, de)
    down = (1, 1, de, d)
    return pl.pallas_call(
        _moe_kernel,
        out_shape=jax.ShapeDtypeStruct((tp * ROW_SEGS, LANES), F32),
        grid_spec=pltpu.PrefetchScalarGridSpec(
            num_scalar_prefetch=3,
            grid=(tp // tm,),
            in_specs=[pl.BlockSpec((tm * ROW_SEGS, LANES), gate_map), pl.BlockSpec((tm, 2), gate_map),
                      pl.BlockSpec(up, lo_map), pl.BlockSpec(up, hi_map),
                      pl.BlockSpec(up, lo_map), pl.BlockSpec(up, hi_map),
                      pl.BlockSpec(down, lo_map), pl.BlockSpec(down, hi_map)],
            out_specs=pl.BlockSpec((tm * ROW_SEGS, LANES), lambda j, lo, hi, nt: (j, 0)),
        ),
        compiler_params=_cparams(("arbitrary",)),
        name="moe_experts",
    )(tile_lo, tile_hi, n_tiles, xs, wrow, w1, w1, w3, w3, w2, w2)


def _moe(h2, logits, router_bias, layer, w1, w3, w2):
    t = h2.shape[0]
    tm = MOE_TILE
    n_tiles_max = t // tm + N_CLASSES
    tp = n_tiles_max * tm
    lg_t = logits[:, :N_EXPERTS].T.reshape(N_EXPERTS, t // LANES, LANES)
    cls, wlo, whi = _route(lg_t, router_bias)
    cls, wlo, whi = cls.reshape(t), wlo.reshape(t), whi.reshape(t)
    classes = jnp.arange(N_CLASSES, dtype=jnp.int32)
    onehot = (cls[:, None] == classes[None, :]).astype(jnp.int32)
    csum = jnp.cumsum(onehot, axis=0)
    rank = jnp.sum(csum * onehot, axis=1) - 1
    counts = csum[-1]
    tiles_per = (counts + tm - 1) // tm
    tile_end = jnp.cumsum(tiles_per)
    tile_start = tile_end - tiles_per
    dest = jnp.sum(onehot * (tile_start * tm)[None, :], axis=1) + rank
    tiles = jnp.arange(n_tiles_max, dtype=jnp.int32)
    tile_cls = jnp.minimum(jnp.sum((tile_end[None, :] <= tiles[:, None]).astype(jnp.int32), axis=1), N_CLASSES - 1)
    pair = tile_cls % 6
    base = (tile_cls // 6) * PER_GROUP
    pair_onehot = (pair[:, None] == jnp.arange(6, dtype=jnp.int32)[None, :]).astype(jnp.int32)
    tile_lo = base + jnp.sum(pair_onehot * jnp.asarray(PAIR_LO, jnp.int32)[None, :], axis=1)
    tile_hi = base + jnp.sum(pair_onehot * jnp.asarray(PAIR_HI, jnp.int32)[None, :], axis=1)
    per_token = jnp.stack([jnp.arange(t, dtype=jnp.int32), lax.bitcast_convert_type(wlo, jnp.int32),
                           lax.bitcast_convert_type(whi, jnp.int32)], axis=1)
    padding = jnp.stack([jnp.arange(tp, dtype=jnp.int32) % t, jnp.zeros((tp,), jnp.int32),
                         jnp.zeros((tp,), jnp.int32)], axis=1)
    per_row = padding.at[dest].set(per_token, unique_indices=True)
    tok = per_row[:, 0]
    wrow = lax.bitcast_convert_type(per_row[:, 1:3], F32)
    xs = _gather_rows(h2, tok).reshape(tp * ROW_SEGS, LANES)
    ys = _moe_experts(xs, wrow, tile_lo, tile_hi, tile_end[-1:], layer, w1, w3, w2)
    return _gather_rows(ys.reshape(tp, ROW_SEGS, LANES), dest)


def _final_kernel(s_ref, y_ref, mod_ref, g_ref, o_ref):
    x = s_ref[0] + mod_ref[0, 0, 5:6, :] * _token_rows(y_ref)
    o_ref[0] = _rms(x, g_ref[...])


def _final(s, y, mod, final_g):
    b, n, d = s.shape
    tm = ROW_TILE
    skip = N_CTX // tm
    row_spec = pl.BlockSpec((1, tm, d), lambda i, j: (i, j + skip, 0))
    return pl.pallas_call(
        _final_kernel,
        out_shape=jax.ShapeDtypeStruct((b, n - N_CTX, d), F32),
        grid=(b, (n - N_CTX) // tm),
        in_specs=[row_spec, pl.BlockSpec((tm * ROW_SEGS, LANES), lambda i, j: (i * (n // tm) + j + skip, 0)),
                  pl.BlockSpec((1, 1, 6, d), lambda i, j: (i, 1, 0, 0)),
                  pl.BlockSpec((1, d), lambda i, j: (0, 0))],
        out_specs=pl.BlockSpec((1, tm, d), lambda i, j: (i, j, 0)),
        compiler_params=_cparams(("arbitrary", "arbitrary")),
        name="final_norm",
    )(s, y, mod, final_g.reshape(1, d))


def _rope_tables(seq):
    pos = jnp.arange(seq)
    row_pos, col_pos = pos // GRID_W, pos % GRID_W
    n = A_DH // 2
    inv = ROPE_BASE ** (-jnp.arange(0, n, 2, dtype=F32) / n)
    ang_r = row_pos.astype(F32)[:, None] * inv[None, :]
    ang_c = col_pos.astype(F32)[:, None] * inv[None, :]
    ang = jnp.concatenate([ang_r, ang_c], axis=-1)
    cos = jnp.tile(jnp.cos(ang), (1, 4))
    sin = jnp.tile(jnp.sin(ang), (1, 4))
    sin = jnp.concatenate([-sin[:, :LANES // 2], sin[:, LANES // 2:]], axis=-1)
    cos = jnp.concatenate([jnp.ones((N_CTX, LANES), F32), cos], axis=0)
    sin = jnp.concatenate([jnp.zeros((N_CTX, LANES), F32), sin], axis=0)
    return cos, sin


def _interleave_maps(w):
    d = w.shape[0]
    w = w.reshape(d, A_HEADS, 2, 2, 2, A_DH // 4)
    return w.transpose(0, 1, 4, 2, 3, 5).reshape(d, A_HEADS * LANES)


def _att_weights(w_in):
    a_qk = A_HEADS * 2 * A_DH
    a_v = A_HEADS * A_DV
    b_w = B_HEADS * B_DH
    qa = _interleave_maps(w_in[:, :a_qk]) * (A_DH ** -0.5)
    ka = _interleave_maps(w_in[:, a_qk:2 * a_qk])
    va = w_in[:, 2 * a_qk:2 * a_qk + a_v]
    o = 2 * a_qk + a_v
    qb = w_in[:, o:o + b_w] * (B_DH ** -0.5)
    rest = w_in[:, o + b_w:]
    return jnp.concatenate([qa, ka, va, qb, rest], axis=1).astype(BF16)


def _gla_weights(w_in):
    d = w_in.shape[0]
    n_main = 2 * C_HEADS * C_DK + 2 * C_HEADS * C_DV
    pad = jnp.zeros((d, LANES - 2 * C_RANK), w_in.dtype)
    return jnp.concatenate([w_in, pad], axis=1).astype(BF16), n_main


def kernel(x, c, ctx, c_ctx, w_mod, b_mod, norm_g, final_g, att_w_in, att_w_out, att_lambda, att_subln_g, na_bias,
           gla_w_in, gla_w_gate, gla_b_gate, gla_norm_g, gla_w_out, router_w, router_bias, moe_w1, moe_w3, moe_w2):
    b, seq, d = x.shape
    n = N_CTX + seq
    s = jnp.concatenate([ctx, x], axis=1)

    rows = b + 1
    rows_pad = -(-rows // 8) * 8
    cc = jnp.concatenate([c, c_ctx[None, :], jnp.zeros((rows_pad - rows, d), F32)], axis=0)
    mod_all = _mod_vectors(cc, w_mod, b_mod)
    mod_x = mod_all[:, :b].reshape(DEPTH, b, 1, 6, d)
    mod_c = jnp.broadcast_to(mod_all[:, b].reshape(DEPTH, 1, 1, 6, d), (DEPTH, b, 1, 6, d))
    mods = jnp.concatenate([mod_c, mod_x], axis=2)

    cos, sin = _rope_tables(seq)
    rw_hi = router_w.astype(BF16)
    rw_lo = (router_w - rw_hi.astype(F32)).astype(BF16)
    zpad = jnp.zeros((d, LANES - 2 * N_EXPERTS), BF16)
    rw1 = jnp.concatenate([rw_hi, rw_lo, zpad], axis=1)
    rw2 = jnp.concatenate([rw_hi, jnp.zeros((d, N_EXPERTS), BF16), zpad], axis=1)

    w1_b, w3_b, w2_b = moe_w1.astype(BF16), moe_w3.astype(BF16), moe_w2.astype(BF16)
    y = None
    for i in range(DEPTH):
        j = i // 2
        modp = mods[i - 1] if i else None
        if i % 2 == 0:
            lam_init = 0.8 - 0.6 * math.exp(-0.3 * i)
            w = _att_weights(att_w_in[j])
            outs = _project(s, y, modp, mods[i], norm_g[i, 0], w, cos, sin,
                            n_rope=2 * A_HEADS * LANES, n_bf16=w.shape[1])
            if i:
                s = outs[0]
            p = outs[-1]
            oa = _diff_attention(p, att_lambda[j], att_subln_g[j], lam_init)
            ob_x, ob_c = _neighbourhood_attention(p, _na_bias_table(na_bias[j], seq // GRID_W))
            s, h2, logits = _out_project((oa, ob_x, ob_c), att_w_out[j].astype(BF16), s, mods[i], norm_g[i, 1],
                                         rw1, rw2, gla=False)
        else:
            w, n_main = _gla_weights(gla_w_in[j])
            outs = _project(s, y, modp, mods[i], norm_g[i, 0], w, None, None, n_rope=0, n_bf16=n_main)
            s, p, lr = outs
            wg = jnp.zeros((2, LANES, C_HEADS * C_DK), F32)
            wg = wg.at[0, :C_RANK].set(gla_w_gate[j, 0]).at[1, C_RANK:2 * C_RANK].set(gla_w_gate[j, 1])
            o = _gla(p, lr, wg, gla_b_gate[j])
            s, h2, logits = _out_project((o, p), gla_w_out[j].astype(BF16), s, mods[i], norm_g[i, 1],
                                         rw1, rw2, gla=True, gn=gla_norm_g[j])
        y = _moe(h2.reshape(b * n, ROW_SEGS, LANES), logits.reshape(b * n, LANES), router_bias,
                 i, w1_b, w3_b, w2_b)
        y = y.reshape(b * n * ROW_SEGS, LANES)
    return _final(s, y, mods[DEPTH - 1], final_g)
```

```python
import functools
import math

import jax
import jax.numpy as jnp
import numpy as np
from jax import lax
from jax.experimental import pallas as pl
from jax.experimental.pallas import tpu as pltpu
from jax.experimental.pallas import tpu_sc as plsc

F32 = jnp.float32
BF16 = jnp.bfloat16

D_MODEL = 1024
DEPTH = 4
GRID_W = 64
N_CTX = 256
A_HEADS = 4
A_DH = 64
A_DV = 128
B_HEADS = 8
B_DH = 64
NA_ROWS = 8
NA_COLS = 16
C_HEADS = 4
C_DK = 128
C_DV = 256
C_RANK = 16
C_TAU = 16.0
C_CHUNK = 64
GLA_BLOCK = 256
N_EXPERTS = 16
N_GROUPS = 4
PER_GROUP = 4
D_EXPERT = 512
ROPE_BASE = 10000.0
EPS = 1e-6
LOG2_E = math.log2(math.e)
ATT_KEY_BLOCK = 768

LANES = 128
ROW_TILE = 256
MOE_TILE = 256
SC_CORES = 2
SC_SUBCORES = 16
SC_WORKERS = SC_CORES * SC_SUBCORES
SC_GATHER_ROWS = 64
N_CLASSES = N_GROUPS * 6
NA_QROWS = 8
NA_SUB = 2
NA_KROWS = NA_SUB + NA_ROWS - 1
VMEM_LIMIT = 52 * 1024 * 1024
ROW_SEGS = D_MODEL // LANES

PAIR_LO = (0, 0, 0, 1, 1, 2)
PAIR_HI = (1, 2, 3, 2, 3, 3)


def _cparams(sem):
    return pltpu.CompilerParams(dimension_semantics=sem, vmem_limit_bytes=VMEM_LIMIT)


def _sigmoid(x):
    return 1.0 / (1.0 + jnp.exp(-x))


def _silu(x):
    return x * _sigmoid(x)


def _rms(x, g):
    return x * lax.rsqrt(jnp.mean(x * x, axis=-1, keepdims=True) + EPS) * g


def _dot(a, b):
    return jnp.dot(a, b, preferred_element_type=F32)


def _dot_nt(a, b):
    return lax.dot_general(a, b, (((1,), (1,)), ((), ())), preferred_element_type=F32)


def _dot_tn(a, b):
    return lax.dot_general(a, b, (((0,), (0,)), ((), ())), preferred_element_type=F32)


def _mod_kernel(c_ref, w_ref, b_ref, o_ref):
    a = _silu(c_ref[...]).astype(BF16)
    o_ref[0] = _dot(a, w_ref[0].astype(BF16)) + b_ref[0]


def _mod_vectors(cc, w_mod, b_mod):
    depth, d, n6 = w_mod.shape
    rows = cc.shape[0]
    tn = 1536
    return pl.pallas_call(
        _mod_kernel,
        out_shape=jax.ShapeDtypeStruct((depth, rows, n6), F32),
        grid=(depth, n6 // tn),
        in_specs=[
            pl.BlockSpec((rows, d), lambda i, j: (0, 0)),
            pl.BlockSpec((1, d, tn), lambda i, j: (i, 0, j)),
            pl.BlockSpec((1, 1, tn), lambda i, j: (i, 0, j)),
        ],
        out_specs=pl.BlockSpec((1, rows, tn), lambda i, j: (i, 0, j)),
        compiler_params=_cparams(("arbitrary", "arbitrary")),
        name="mod_vectors",
    )(cc, w_mod, b_mod.reshape(depth, 1, n6))


def _token_rows(ref):
    rows = ref.shape[0] // ROW_SEGS
    return jnp.concatenate([ref[pl.ds(sg, rows, stride=ROW_SEGS), :] for sg in range(ROW_SEGS)], axis=-1)


def _store_token_rows(ref, val):
    rows = val.shape[0]
    for sg in range(ROW_SEGS):
        ref[pl.ds(sg, rows, stride=ROW_SEGS), :] = val[:, sg * LANES:(sg + 1) * LANES]


def _proj_kernel(*refs, has_y, n_rope, n_bf16, col_chunk):
    it = iter(refs)
    s_ref = next(it)
    y_ref = next(it) if has_y else None
    modp_ref = next(it) if has_y else None
    mod_ref = next(it)
    ng_ref = next(it)
    w_ref = next(it)
    cos_ref = next(it) if n_rope else None
    sin_ref = next(it) if n_rope else None
    snew_ref = next(it) if has_y else None
    p_ref = next(it)
    lr_ref = next(it) if w_ref.shape[1] > n_bf16 else None

    x = s_ref[0]
    if has_y:
        x = x + modp_ref[0, 0, 5:6, :] * _token_rows(y_ref)
        snew_ref[0] = x
    h = _rms(x, ng_ref[...])
    h = h * (1.0 + mod_ref[0, 0, 1:2, :]) + mod_ref[0, 0, 0:1, :]
    hb = h.astype(BF16)
    n_out = w_ref.shape[1]
    for c0 in range(0, n_out, col_chunk):
        c1 = min(c0 + col_chunk, n_out)
        acc = _dot(hb, w_ref[:, c0:c1])
        for b0 in range(c0, c1, LANES):
            t = acc[:, b0 - c0:b0 - c0 + LANES]
            if b0 < n_rope:
                t = t * cos_ref[...] + pltpu.roll(t, LANES // 2, 1) * sin_ref[...]
            if b0 < n_bf16:
                p_ref[0, :, b0:b0 + LANES] = t.astype(BF16)
            else:
                lr_ref[0, :, b0 - n_bf16:b0 - n_bf16 + LANES] = t


def _project(s, y, modp, mod, ng, w, cos, sin, *, n_rope, n_bf16):
    b, n, d = s.shape
    n_out = w.shape[1]
    tm = ROW_TILE
    has_y = y is not None
    row_spec = pl.BlockSpec((1, tm, d), lambda i, j: (i, j, 0))
    mod_spec = pl.BlockSpec((1, 1, 6, d), lambda i, j: (i, jnp.minimum(j, 1), 0, 0))
    in_specs = [row_spec]
    args = [s]
    if has_y:
        in_specs += [pl.BlockSpec((tm * ROW_SEGS, LANES), lambda i, j: (i * (n // tm) + j, 0)), mod_spec]
        args += [y, modp]
    in_specs += [mod_spec, pl.BlockSpec((1, d), lambda i, j: (0, 0)),
                 pl.BlockSpec((d, n_out), lambda i, j: (0, 0))]
    args += [mod, ng.reshape(1, d), w]
    if n_rope:
        tab_spec = pl.BlockSpec((tm, LANES), lambda i, j: (j, 0))
        in_specs += [tab_spec, tab_spec]
        args += [cos, sin]
    out_shape, out_specs = [], []
    if has_y:
        out_shape.append(jax.ShapeDtypeStruct((b, n, d), F32))
        out_specs.append(row_spec)
    out_shape.append(jax.ShapeDtypeStruct((b, n, n_bf16), BF16))
    out_specs.append(pl.BlockSpec((1, tm, n_bf16), lambda i, j: (i, j, 0)))
    if n_out > n_bf16:
        out_shape.append(jax.ShapeDtypeStruct((b, n, n_out - n_bf16), F32))
        out_specs.append(pl.BlockSpec((1, tm, n_out - n_bf16), lambda i, j: (i, j, 0)))
    return pl.pallas_call(
        functools.partial(_proj_kernel, has_y=has_y, n_rope=n_rope, n_bf16=n_bf16, col_chunk=512),
        out_shape=out_shape,
        grid=(b, n // tm),
        in_specs=in_specs,
        out_specs=out_specs,
        compiler_params=_cparams(("arbitrary", "arbitrary")),
        name="norm_mod_project",
    )(*args)


def _softmax_rows(s):
    e = jnp.exp(s - jnp.max(s, axis=-1, keepdims=True))
    return e * (1.0 / jnp.sum(e, axis=-1, keepdims=True))


def _diff_attn_kernel(q_ref, k_ref, v_ref, lam_ref, g_ref, o_ref, *, lam_init):
    lp = lam_ref[...]
    lam = (jnp.exp(jnp.sum(lp[0:1] * lp[1:2], axis=-1, keepdims=True))
           - jnp.exp(jnp.sum(lp[2:3] * lp[3:4], axis=-1, keepdims=True)) + lam_init)
    q = q_ref[0]
    lane = lax.broadcasted_iota(jnp.int32, (1, LANES), 1)
    first_map = (lane // 32) % 2 == 0
    zero = jnp.zeros_like(q)
    q0 = jnp.where(first_map, q, zero)
    q1 = jnp.where(first_map, zero, q)

    def attend(n_keys):
        blk = min(ATT_KEY_BLOCK, n_keys)
        ones = jnp.ones((blk, A_DV), BF16)
        state = [None, None]
        for kb in range(n_keys // blk):
            k = k_ref[0, kb * blk:(kb + 1) * blk, :]
            v1 = jnp.concatenate([v_ref[0, kb * blk:(kb + 1) * blk, :], ones], axis=1)
            for i, qm in enumerate((q0, q1)):
                s = _dot_nt(qm, k)
                m_new = jnp.max(s, axis=-1, keepdims=True)
                if kb:
                    m_old, acc_old = state[i]
                    m_new = jnp.maximum(m_old, m_new)
                acc = _dot(jnp.exp2(s - m_new).astype(BF16), v1)
                if kb:
                    acc = acc_old * jnp.exp2(m_old - m_new) + acc
                state[i] = (m_new, acc)
        acc0, acc1 = state[0][1], state[1][1]
        o = acc0[:, :A_DV] * (1.0 / acc0[:, A_DV:]) - acc1[:, :A_DV] * (lam / acc1[:, A_DV:])
        o_ref[0] = (_rms(o, g_ref[...]) * (1.0 - lam_init)).astype(BF16)

    qi = pl.program_id(2)

    @pl.when(qi == 0)
    def _():
        attend(N_CTX)

    @pl.when(qi > 0)
    def _():
        attend(k_ref.shape[1])


def _diff_attention(p, lam_p, subln_g, lam_init):
    b, n, _ = p.shape
    tq = ROW_TILE
    return pl.pallas_call(
        functools.partial(_diff_attn_kernel, lam_init=lam_init),
        out_shape=jax.ShapeDtypeStruct((b, n, A_HEADS * A_DV), BF16),
        grid=(b, A_HEADS, n // tq),
        in_specs=[
            pl.BlockSpec((1, tq, LANES), lambda i, h, j: (i, j, h)),
            pl.BlockSpec((1, n, LANES), lambda i, h, j: (i, 0, A_HEADS + h)),
            pl.BlockSpec((1, n, LANES), lambda i, h, j: (i, 0, 2 * A_HEADS + h)),
            pl.BlockSpec((4, A_DH), lambda i, h, j: (0, 0)),
            pl.BlockSpec((1, A_DV), lambda i, h, j: (0, 0)),
        ],
        out_specs=pl.BlockSpec((1, tq, LANES), lambda i, h, j: (i, j, h)),
        compiler_params=_cparams(("arbitrary", "arbitrary", "arbitrary")),
        name="diff_attention",
    )(p, p, p, lam_p, subln_g.reshape(1, A_DV))


def _split_heads(q):
    lane = lax.broadcasted_iota(jnp.int32, (1, LANES), 1)
    first_head = lane < B_DH
    zero = jnp.zeros_like(q)
    return first_head, (jnp.where(first_head, q, zero), jnp.where(first_head, zero, q))


def _na_ctx_kernel(q_ref, k_ref, v_ref, o_ref):
    first_head, qh = _split_heads(q_ref[0])
    kc = k_ref[0]
    vc = v_ref[0]
    outs = [_dot(_softmax_rows(_dot_nt(qh[h], kc)).astype(BF16), vc) for h in range(2)]
    o_ref[0] = jnp.where(first_head, outs[0], outs[1]).astype(BF16)


def _na_kernel(qa_ref, qb_ref, k_ref, v_ref, bm_ref, o_ref):
    blk = pl.program_id(2)
    rows = (k_ref.shape[1] - N_CTX) // GRID_W
    sub_q = NA_SUB * GRID_W
    sub_k = NA_KROWS * GRID_W
    kc = k_ref[0, 0:N_CTX, :]
    vc = jnp.concatenate([v_ref[0, 0:N_CTX, :], jnp.ones((N_CTX, LANES), BF16)], axis=1)
    ones = jnp.ones((sub_k, LANES), BF16)
    for sub in range(NA_QROWS // NA_SUB):
        q_ref = qa_ref if sub * sub_q < qa_ref.shape[1] else qb_ref
        q0 = (sub * sub_q) % qa_ref.shape[1]
        first_head, qh = _split_heads(q_ref[0, q0:q0 + sub_q, :])
        k_row0 = jnp.clip(blk * NA_QROWS + sub * NA_SUB - NA_ROWS // 2, 0, rows - NA_KROWS)
        start = pl.multiple_of(N_CTX + k_row0 * GRID_W, GRID_W)
        kw = k_ref[0, pl.ds(start, sub_k), :]
        vw = jnp.concatenate([v_ref[0, pl.ds(start, sub_k), :], ones], axis=1)
        outs = []
        for h in range(2):
            s_loc = _dot_nt(qh[h], kw) + bm_ref[h, 0, sub]
            s_ctx = _dot_nt(qh[h], kc)
            m = jnp.maximum(jnp.max(s_loc, axis=-1, keepdims=True), jnp.max(s_ctx, axis=-1, keepdims=True))
            acc = _dot(jnp.exp(s_loc - m).astype(BF16), vw) + _dot(jnp.exp(s_ctx - m).astype(BF16), vc)
            outs.append(acc[:, :LANES] * (1.0 / acc[:, LANES:]))
        o_ref[0, sub * sub_q:(sub + 1) * sub_q, :] = jnp.where(first_head, outs[0], outs[1]).astype(BF16)


def _na_bias_table(na_bias, rows):
    h = na_bias.shape[0]
    n_dr, n_dc = 2 * NA_ROWS - 1, 2 * NA_COLS - 1
    width = 2 * GRID_W
    left = GRID_W - NA_COLS
    u = jnp.pad(na_bias, ((0, 0), (0, 0), (left, width - left - n_dc)))
    skew = jnp.tile(u, (1, 1, GRID_W))[:, :, :GRID_W * (width - 1)].reshape(h, n_dr, GRID_W, width - 1)
    toeplitz = skew[:, :, :, GRID_W - 1:]
    margin = NA_KROWS - NA_ROWS
    by_col = jnp.pad(toeplitz.transpose(0, 2, 1, 3), ((0, 0), (0, 0), (margin, margin), (0, 0)))
    by_col = by_col.reshape(h, GRID_W, (n_dr + 2 * margin) * GRID_W)
    col = np.arange(GRID_W)
    col_start = np.clip(col - NA_COLS // 2, 0, GRID_W - NA_COLS)
    col_ok = (col[None, :] >= col_start[:, None]) & (col[None, :] < col_start[:, None] + NA_COLS)
    n_sub = NA_QROWS // NA_SUB
    blocks = []
    for q_row0 in (0, NA_QROWS, rows - NA_QROWS):
        for sub in range(n_sub):
            k_row0 = int(np.clip(q_row0 + sub * NA_SUB - NA_ROWS // 2, 0, rows - NA_KROWS))
            kr = k_row0 + np.arange(NA_KROWS)
            for rq in range(NA_SUB):
                r = q_row0 + sub * NA_SUB + rq
                r0 = int(np.clip(r - NA_ROWS // 2, 0, rows - NA_ROWS))
                row_ok = (kr >= r0) & (kr < r0 + NA_ROWS)
                ok = (col_ok[:, None, :] & row_ok[None, :, None]).reshape(GRID_W, NA_KROWS * GRID_W)
                first = k_row0 - r + NA_ROWS - 1 + margin
                window = by_col[:, :, first * GRID_W:(first + NA_KROWS) * GRID_W]
                blocks.append(jnp.where(ok[None], window, -jnp.inf))
    table = jnp.stack(blocks, axis=1)
    return table.reshape(h, 3, n_sub, NA_SUB * GRID_W, NA_KROWS * GRID_W)


def _neighbourhood_attention(p, bm):
    b, n, _ = p.shape
    tq = NA_QROWS * GRID_W
    half = tq // 2
    n_blocks = (n - N_CTX) // tq
    ctx_blocks = N_CTX // half
    col0 = 3 * A_HEADS
    pairs = B_HEADS // 2
    width = B_HEADS * B_DH

    def pattern(j):
        return jnp.minimum(j, 1) + (j == n_blocks - 1).astype(jnp.int32)

    ob_x = pl.pallas_call(
        _na_kernel,
        out_shape=jax.ShapeDtypeStruct((b, n - N_CTX, width), BF16),
        grid=(b, pairs, n_blocks),
        in_specs=[
            pl.BlockSpec((1, half, LANES), lambda i, h, j: (i, ctx_blocks + 2 * j, col0 + h)),
            pl.BlockSpec((1, half, LANES), lambda i, h, j: (i, ctx_blocks + 2 * j + 1, col0 + h)),
            pl.BlockSpec((1, n, LANES), lambda i, h, j: (i, 0, col0 + pairs + h)),
            pl.BlockSpec((1, n, LANES), lambda i, h, j: (i, 0, col0 + 2 * pairs + h)),
            pl.BlockSpec((2, 1, NA_QROWS // NA_SUB, NA_SUB * GRID_W, NA_KROWS * GRID_W),
                         lambda i, h, j: (h, pattern(j), 0, 0, 0)),
        ],
        out_specs=pl.BlockSpec((1, tq, LANES), lambda i, h, j: (i, j, h)),
        compiler_params=_cparams(("arbitrary", "arbitrary", "arbitrary")),
        name="neighbourhood_attention",
    )(p, p, p, p, bm)
    ob_c = pl.pallas_call(
        _na_ctx_kernel,
        out_shape=jax.ShapeDtypeStruct((b, N_CTX, width), BF16),
        grid=(b, pairs),
        in_specs=[
            pl.BlockSpec((1, N_CTX, LANES), lambda i, h: (i, 0, col0 + h)),
            pl.BlockSpec((1, N_CTX, LANES), lambda i, h: (i, 0, col0 + pairs + h)),
            pl.BlockSpec((1, N_CTX, LANES), lambda i, h: (i, 0, col0 + 2 * pairs + h)),
        ],
        out_specs=pl.BlockSpec((1, N_CTX, LANES), lambda i, h: (i, 0, h)),
        compiler_params=_cparams(("arbitrary", "arbitrary")),
        name="context_attention",
    )(p, p, p)
    return ob_x, ob_c


def _gla_kernel(q_ref, k_ref, v_ref, lr_ref, wg_ref, bg_ref, o_ref, ob_ref, g_ref):
    n = q_ref.shape[1]
    blk = GLA_BLOCK
    per_blk = blk // C_CHUNK
    n_blocks = n // blk
    ri = lax.broadcasted_iota(jnp.int32, (blk, blk), 0)
    ci = lax.broadcasted_iota(jnp.int32, (blk, blk), 1)
    same_chunk = (ri // C_CHUNK) == (ci // C_CHUNK)
    keeps = (same_chunk & (ci <= ri), same_chunk & (ci >= ri))
    tris = tuple(jnp.where(kp, 1.0, 0.0).astype(BF16) for kp in keeps)
    w_gate = jnp.concatenate([wg_ref[0], wg_ref[1]], axis=1).astype(BF16)
    b_gate = jnp.concatenate([bg_ref[0:1, :], bg_ref[1:2, :]], axis=1)
    for i in range(n_blocks):
        z = _dot(lr_ref[0, i * blk:(i + 1) * blk, :].astype(BF16), w_gate) + b_gate
        g_ref[i * blk:(i + 1) * blk, :] = (jnp.minimum(z, 0.0) - jnp.log(1.0 + jnp.exp(-jnp.abs(z)))) * (1.0 / C_TAU)

    def block(sb, direction, st):
        keep = keeps[direction]
        end_row = C_CHUNK - 1 if direction == 0 else 0
        mid_row = C_CHUNK // 2 - 1 if direction == 0 else C_CHUNK // 2
        r0 = sb * blk
        g = g_ref[r0:r0 + blk, direction * C_DK:(direction + 1) * C_DK]
        g_hi = g.astype(BF16)
        g_lo = (g - g_hi.astype(F32)).astype(BF16)
        gc2 = _dot(tris[direction], jnp.concatenate([g_hi, g_lo], axis=1))
        gc = (gc2[:, :C_DK] + gc2[:, C_DK:]).reshape(per_blk, C_CHUNK, C_DK)
        g_end = gc[:, end_row:end_row + 1, :]
        g_mid = gc[:, mid_row:mid_row + 1, :]
        q = (q_ref[0, r0:r0 + blk, :].astype(F32) * (C_DK ** -0.5)).reshape(per_blk, C_CHUNK, C_DK)
        k = k_ref[0, r0:r0 + blk, :].astype(F32).reshape(per_blk, C_CHUNK, C_DK)
        v = v_ref[0, r0:r0 + blk, :]
        q_in = (q * jnp.exp(gc)).astype(BF16).reshape(blk, C_DK)
        q_mid = (q * jnp.exp(gc - g_mid)).astype(BF16).reshape(blk, C_DK)
        k_mid = (k * jnp.exp(g_mid - gc)).astype(BF16).reshape(blk, C_DK)
        k_end = (k * jnp.exp(g_end - gc)).astype(BF16).reshape(blk, C_DK)
        a = jnp.where(keep, _dot_nt(q_mid, k_mid), 0.0)
        o_intra = _dot(a.astype(BF16), v)
        ends = jnp.concatenate([g_end.reshape(per_blk, C_DK), jnp.zeros((8 - per_blk, C_DK), F32)], axis=0)
        decay = jnp.transpose(jnp.exp(ends))
        o_inter = [None] * per_blk
        for c in (range(per_blk) if direction == 0 else reversed(range(per_blk))):
            rows = slice(c * C_CHUNK, (c + 1) * C_CHUNK)
            o_inter[c] = _dot(q_in[rows], st.astype(BF16))
            st = st * decay[:, c:c + 1] + _dot_tn(k_end[rows], v[rows])
        return o_intra + jnp.concatenate(o_inter, axis=0), st

    st_f = jnp.zeros((C_DK, C_DV), F32)
    st_b = jnp.zeros((C_DK, C_DV), F32)
    ctx_blocks = N_CTX // blk
    order_b = list(reversed(range(ctx_blocks))) + list(reversed(range(ctx_blocks, n_blocks)))
    for i in range(n_blocks):
        o, st_f = block(i, 0, st_f)
        o_ref[0, i * blk:(i + 1) * blk, :] = o
        sb = order_b[i]
        o, st_b = block(sb, 1, st_b)
        ob_ref[sb * blk:(sb + 1) * blk, :] = o
    o_ref[0] = o_ref[0] + ob_ref[...]


def _gla(p, lr, wg, bg):
    b, n, _ = p.shape
    return pl.pallas_call(
        _gla_kernel,
        out_shape=jax.ShapeDtypeStruct((b, n, C_HEADS * C_DV), F32),
        grid=(b, C_HEADS),
        in_specs=[
            pl.BlockSpec((1, n, C_DK), lambda i, h: (i, 0, h)),
            pl.BlockSpec((1, n, C_DK), lambda i, h: (i, 0, C_HEADS + h)),
            pl.BlockSpec((1, n, C_DV), lambda i, h: (i, 0, C_HEADS + h)),
            pl.BlockSpec((1, n, LANES), lambda i, h: (i, 0, 0)),
            pl.BlockSpec((2, LANES, C_DK), lambda i, h: (0, 0, h)),
            pl.BlockSpec((2, C_DK), lambda i, h: (0, h)),
        ],
        out_specs=pl.BlockSpec((1, n, C_DV), lambda i, h: (i, 0, h)),
        scratch_shapes=[pltpu.VMEM((n, C_DV), F32), pltpu.VMEM((n, 2 * C_DK), F32)],
        compiler_params=_cparams(("arbitrary", "arbitrary")),
        name="gla",
    )(p, p, p, lr, wg, bg)


def _out_kernel(*refs, gla):
    if gla:
        o_ref, gate_ref, gn_ref, w_ref, s_ref, mod_ref, ng_ref, rw1_ref, rw2_ref, snew_ref, h2_ref, lg_ref = refs
        o = o_ref[0]
        gate = gate_ref[0].astype(F32)
        parts = []
        for hd in range(C_HEADS):
            oh = _rms(o[:, hd * C_DV:(hd + 1) * C_DV], gn_ref[...])
            parts.append((oh * _silu(gate[:, hd * C_DV:(hd + 1) * C_DV])).astype(BF16))
        acc = _dot(jnp.concatenate(parts, axis=-1), w_ref[...])
    else:
        oa_ref, obx_ref, obc_ref, w_ref, s_ref, mod_ref, ng_ref, rw1_ref, rw2_ref, snew_ref, h2_ref, lg_ref = refs
        half = oa_ref.shape[2]
        ob = jnp.where(pl.program_id(1) == 0, obc_ref[0], obx_ref[0])
        acc = _dot(oa_ref[0], w_ref[0:half, :]) + _dot(ob, w_ref[half:, :])
    x = s_ref[0] + mod_ref[0, 0, 2:3, :] * acc
    snew_ref[0] = x
    h2 = _rms(x, ng_ref[...]) * (1.0 + mod_ref[0, 0, 4:5, :]) + mod_ref[0, 0, 3:4, :]
    _store_token_rows(h2_ref, h2)
    hi = h2.astype(BF16)
    lo = (h2 - hi.astype(F32)).astype(BF16)
    t = _dot(hi, rw1_ref[...])
    lg_ref[0] = t + pltpu.roll(t, LANES - N_EXPERTS, 1) + _dot(lo, rw2_ref[...])


def _out_project(mix, w_out, s, mod, ng, rw1, rw2, *, gla, gn=None):
    b, n, d = s.shape
    tm = ROW_TILE
    row_spec = pl.BlockSpec((1, tm, d), lambda i, j: (i, j, 0))
    const2 = lambda i, j: (0, 0)
    if gla:
        o, p = mix
        in_specs = [row_spec, pl.BlockSpec((1, tm, d), lambda i, j: (i, j, 2)),
                    pl.BlockSpec((1, C_DV), const2)]
        args = [o, p, gn.reshape(1, C_DV)]
    else:
        oa, ob_x, ob_c = mix
        half = oa.shape[2]
        in_specs = [pl.BlockSpec((1, tm, half), lambda i, j: (i, j, 0)),
                    pl.BlockSpec((1, tm, half), lambda i, j: (i, jnp.maximum(j - 1, 0), 0)),
                    pl.BlockSpec((1, tm, half), lambda i, j: (i, 0, 0))]
        args = [oa, ob_x, ob_c]
    in_specs += [pl.BlockSpec((d, d), const2), row_spec,
                 pl.BlockSpec((1, 1, 6, d), lambda i, j: (i, jnp.minimum(j, 1), 0, 0)),
                 pl.BlockSpec((1, d), const2), pl.BlockSpec((d, LANES), const2), pl.BlockSpec((d, LANES), const2)]
    args += [w_out, s, mod, ng.reshape(1, d), rw1, rw2]
    return pl.pallas_call(
        functools.partial(_out_kernel, gla=gla),
        out_shape=[jax.ShapeDtypeStruct((b, n, d), F32), jax.ShapeDtypeStruct((b * n * ROW_SEGS, LANES), F32),
                   jax.ShapeDtypeStruct((b, n, LANES), F32)],
        grid=(b, n // tm),
        in_specs=in_specs,
        out_specs=[row_spec, pl.BlockSpec((tm * ROW_SEGS, LANES), lambda i, j: (i * (n // tm) + j, 0)),
                   pl.BlockSpec((1, tm, LANES), lambda i, j: (i, j, 0))],
        compiler_params=_cparams(("arbitrary", "arbitrary")),
        name="out_project",
    )(*args)


def _route_kernel(bias_ref, lg_ref, cls_ref, wlo_ref, whi_ref):
    score = [_sigmoid(lg_ref[e]) for e in range(N_EXPERTS)]
    sel = [score[e] + bias_ref[e] for e in range(N_EXPERTS)]
    grp_score = []
    for g in range(N_GROUPS):
        v = sel[g * PER_GROUP:(g + 1) * PER_GROUP]
        best = v[0] + v[1]
        for a in range(PER_GROUP):
            for c in range(a + 1, PER_GROUP):
                if (a, c) != (0, 1):
                    best = jnp.maximum(best, v[a] + v[c])
        grp_score.append(best)
    grp = jnp.zeros(grp_score[0].shape, jnp.int32)
    best = grp_score[0]
    for g in range(1, N_GROUPS):
        upd = grp_score[g] > best
        best = jnp.where(upd, grp_score[g], best)
        grp = jnp.where(upd, g, grp)

    def pick(vals, j):
        out = vals[j]
        for g in range(1, N_GROUPS):
            out = jnp.where(grp == g, vals[g * PER_GROUP + j], out)
        return out

    v = [pick(sel, j) for j in range(PER_GROUP)]
    sc = [pick(score, j) for j in range(PER_GROUP)]
    one = jnp.ones(grp.shape, jnp.int32)
    zero = jnp.zeros(grp.shape, jnp.int32)
    chosen = []
    for j in range(PER_GROUP):
        rank = zero
        for m in range(PER_GROUP):
            if m == j:
                continue
            ahead = (v[m] >= v[j]) if m < j else (v[m] > v[j])
            rank = rank + jnp.where(ahead, one, zero)
        chosen.append(rank < 2)
    code = zero
    for j in range(PER_GROUP):
        code = code + jnp.where(chosen[j], one * (1 << j), zero)
    pair = zero
    for idx in range(6):
        pair = jnp.where(code == (1 << PAIR_LO[idx]) + (1 << PAIR_HI[idx]), idx, pair)
    s_lo = jnp.where(chosen[0], sc[0], jnp.where(chosen[1], sc[1], sc[2]))
    s_hi = jnp.where(chosen[3], sc[3], jnp.where(chosen[2], sc[2], sc[1]))
    den = s_lo + s_hi
    cls_ref[...] = grp * 6 + pair
    wlo_ref[...] = s_lo / den
    whi_ref[...] = s_hi / den


def _route(logits_t, router_bias):
    _, r, _ = logits_t.shape
    full = pl.BlockSpec((r, LANES), lambda i: (0, 0))
    return pl.pallas_call(
        _route_kernel,
        out_shape=[jax.ShapeDtypeStruct((r, LANES), jnp.int32), jax.ShapeDtypeStruct((r, LANES), F32),
                   jax.ShapeDtypeStruct((r, LANES), F32)],
        grid=(1,),
        in_specs=[pl.BlockSpec(memory_space=pltpu.SMEM),
                  pl.BlockSpec((N_EXPERTS, r, LANES), lambda i: (0, 0, 0))],
        out_specs=[full, full, full],
        compiler_params=_cparams(("arbitrary",)),
        name="route",
    )(router_bias, logits_t)


def _gather_rows(table, idx):
    p = idx.shape[0]
    per_worker = p // SC_WORKERS
    n_chunks = per_worker // SC_GATHER_ROWS
    mesh = plsc.VectorSubcoreMesh(core_axis_name="c", subcore_axis_name="s", num_cores=SC_CORES,
                                  num_subcores=SC_SUBCORES)

    @functools.partial(
        pl.kernel, mesh=mesh,
        out_type=jax.ShapeDtypeStruct((p,) + table.shape[1:], table.dtype),
        scratch_types=[pltpu.VMEM((per_worker,), jnp.int32),
                       pltpu.VMEM((SC_GATHER_ROWS,) + table.shape[1:], table.dtype),
                       pltpu.SemaphoreType.DMA],
        compiler_params=pltpu.CompilerParams(use_tc_tiling_on_sc=True),
        name="gather_rows",
    )
    def gather(table_hbm, idx_hbm, out_hbm, idx_v, rows_v, sem):
        worker = lax.axis_index("s") * SC_CORES + lax.axis_index("c")
        base = worker * per_worker
        pltpu.sync_copy(idx_hbm.at[pl.ds(base, per_worker)], idx_v)

        @pl.loop(0, n_chunks)
        def _(i):
            off = pl.multiple_of(i * SC_GATHER_ROWS, SC_GATHER_ROWS)
            pltpu.async_copy(table_hbm.at[idx_v.at[pl.ds(off, SC_GATHER_ROWS)]], rows_v, sem).wait()
            pltpu.sync_copy(rows_v, out_hbm.at[pl.ds(base + off, SC_GATHER_ROWS)])

    return gather(table, idx)


def _moe_kernel(lo_ref, hi_ref, nt_ref, x_ref, wr_ref, w1l_ref, w1h_ref, w3l_ref, w3h_ref, w2l_ref, w2h_ref, y_ref):
    j = pl.program_id(0)

    @pl.when(j < nt_ref[0])
    def _():
        x = _token_rows(x_ref).astype(BF16)
        wr = wr_ref[...]
        he_lo = (_silu(_dot(x, w1l_ref[0, 0])) * _dot(x, w3l_ref[0, 0]) * wr[:, 0:1]).astype(BF16)
        he_hi = (_silu(_dot(x, w1h_ref[0, 0])) * _dot(x, w3h_ref[0, 0]) * wr[:, 1:2]).astype(BF16)
        y = _dot(he_lo, w2l_ref[0, 0]) + _dot(he_hi, w2h_ref[0, 0])
        _store_token_rows(y_ref, y)

    @pl.when(j >= nt_ref[0])
    def _():
        y_ref[...] = jnp.zeros_like(y_ref)


def _moe_experts(xs, wrow, tile_lo, tile_hi, n_tiles, layer, w1, w3, w2):
    tp = xs.shape[0] // ROW_SEGS
    tm = MOE_TILE
    d, de = w1.shape[2], w1.shape[3]

    def gate_map(j, lo, hi, nt):
        return (jnp.minimum(j, nt[0] - 1), 0)

    def lo_map(j, lo, hi, nt):
        return (layer, lo[jnp.minimum(j, nt[0] - 1)], 0, 0)

    def hi_map(j, lo, hi, nt):
        return (layer, hi[jnp.minimum(j, nt[0] - 1)], 0, 0)

    up = (1, 1, d, de)
    down = (1, 1, de, d)
    return pl.pallas_call(
        _moe_kernel,
        out_shape=jax.ShapeDtypeStruct((tp * ROW_SEGS, LANES), F32),
        grid_spec=pltpu.PrefetchScalarGridSpec(
            num_scalar_prefetch=3,
            grid=(tp // tm,),
            in_specs=[pl.BlockSpec((tm * ROW_SEGS, LANES), gate_map), pl.BlockSpec((tm, 2), gate_map),
                      pl.BlockSpec(up, lo_map), pl.BlockSpec(up, hi_map),
                      pl.BlockSpec(up, lo_map), pl.BlockSpec(up, hi_map),
                      pl.BlockSpec(down, lo_map), pl.BlockSpec(down, hi_map)],
            out_specs=pl.BlockSpec((tm * ROW_SEGS, LANES), lambda j, lo, hi, nt: (j, 0)),
        ),
        compiler_params=_cparams(("arbitrary",)),
        name="moe_experts",
    )(tile_lo, tile_hi, n_tiles, xs, wrow, w1, w1, w3, w3, w2, w2)


def _moe(h2, logits, router_bias, layer, w1, w3, w2):
    t = h2.shape[0]
    tm = MOE_TILE
    n_tiles_max = t // tm + N_CLASSES
    tp = n_tiles_max * tm
    lg_t = logits[:, :N_EXPERTS].T.reshape(N_EXPERTS, t // LANES, LANES)
    cls, wlo, whi = _route(lg_t, router_bias)
    cls, wlo, whi = cls.reshape(t), wlo.reshape(t), whi.reshape(t)
    classes = jnp.arange(N_CLASSES, dtype=jnp.int32)
    onehot = (cls[:, None] == classes[None, :]).astype(jnp.int32)
    csum = jnp.cumsum(onehot, axis=0)
    rank = jnp.sum(csum * onehot, axis=1) - 1
    counts = csum[-1]
    tiles_per = (counts + tm - 1) // tm
    tile_end = jnp.cumsum(tiles_per)
    tile_start = tile_end - tiles_per
    dest = jnp.sum(onehot * (tile_start * tm)[None, :], axis=1) + rank
    tiles = jnp.arange(n_tiles_max, dtype=jnp.int32)
    tile_cls = jnp.minimum(jnp.sum((tile_end[None, :] <= tiles[:, None]).astype(jnp.int32), axis=1), N_CLASSES - 1)
    pair = tile_cls % 6
    base = (tile_cls // 6) * PER_GROUP
    pair_onehot = (pair[:, None] == jnp.arange(6, dtype=jnp.int32)[None, :]).astype(jnp.int32)
    tile_lo = base + jnp.sum(pair_onehot * jnp.asarray(PAIR_LO, jnp.int32)[None, :], axis=1)
    tile_hi = base + jnp.sum(pair_onehot * jnp.asarray(PAIR_HI, jnp.int32)[None, :], axis=1)
    per_token = jnp.stack([jnp.arange(t, dtype=jnp.int32), lax.bitcast_convert_type(wlo, jnp.int32),
                           lax.bitcast_convert_type(whi, jnp.int32)], axis=1)
    padding = jnp.stack([jnp.arange(tp, dtype=jnp.int32) % t, jnp.zeros((tp,), jnp.int32),
                         jnp.zeros((tp,), jnp.int32)], axis=1)
    per_row = padding.at[dest].set(per_token, unique_indices=True)
    tok = per_row[:, 0]
    wrow = lax.bitcast_convert_type(per_row[:, 1:3], F32)
    xs = _gather_rows(h2, tok).reshape(tp * ROW_SEGS, LANES)
    ys = _moe_experts(xs, wrow, tile_lo, tile_hi, tile_end[-1:], layer, w1, w3, w2)
    return _gather_rows(ys.reshape(tp, ROW_SEGS, LANES), dest)


def _final_kernel(s_ref, y_ref, mod_ref, g_ref, o_ref):
    x = s_ref[0] + mod_ref[0, 0, 5:6, :] * _token_rows(y_ref)
    o_ref[0] = _rms(x, g_ref[...])


def _final(s, y, mod, final_g):
    b, n, d = s.shape
    tm = ROW_TILE
    skip = N_CTX // tm
    row_spec = pl.BlockSpec((1, tm, d), lambda i, j: (i, j + skip, 0))
    return pl.pallas_call(
        _final_kernel,
        out_shape=jax.ShapeDtypeStruct((b, n - N_CTX, d), F32),
        grid=(b, (n - N_CTX) // tm),
        in_specs=[row_spec, pl.BlockSpec((tm * ROW_SEGS, LANES), lambda i, j: (i * (n // tm) + j + skip, 0)),
                  pl.BlockSpec((1, 1, 6, d), lambda i, j: (i, 1, 0, 0)),
                  pl.BlockSpec((1, d), lambda i, j: (0, 0))],
        out_specs=pl.BlockSpec((1, tm, d), lambda i, j: (i, j, 0)),
        compiler_params=_cparams(("arbitrary", "arbitrary")),
        name="final_norm",
    )(s, y, mod, final_g.reshape(1, d))


def _rope_tables(seq):
    pos = jnp.arange(seq)
    row_pos, col_pos = pos // GRID_W, pos % GRID_W
    n = A_DH // 2
    inv = ROPE_BASE ** (-jnp.arange(0, n, 2, dtype=F32) / n)
    ang_r = row_pos.astype(F32)[:, None] * inv[None, :]
    ang_c = col_pos.astype(F32)[:, None] * inv[None, :]
    ang = jnp.concatenate([ang_r, ang_c], axis=-1)
    cos = jnp.tile(jnp.cos(ang), (1, 4))
    sin = jnp.tile(jnp.sin(ang), (1, 4))
    sin = jnp.concatenate([-sin[:, :LANES // 2], sin[:, LANES // 2:]], axis=-1)
    cos = jnp.concatenate([jnp.ones((N_CTX, LANES), F32), cos], axis=0)
    sin = jnp.concatenate([jnp.zeros((N_CTX, LANES), F32), sin], axis=0)
    return cos, sin


def _interleave_maps(w):
    d = w.shape[0]
    w = w.reshape(d, A_HEADS, 2, 2, 2, A_DH // 4)
    return w.transpose(0, 1, 4, 2, 3, 5).reshape(d, A_HEADS * LANES)


def _att_weights(w_in):
    a_qk = A_HEADS * 2 * A_DH
    a_v = A_HEADS * A_DV
    b_w = B_HEADS * B_DH
    qa = _interleave_maps(w_in[:, :a_qk]) * (A_DH ** -0.5 * LOG2_E)
    ka = _interleave_maps(w_in[:, a_qk:2 * a_qk])
    va = w_in[:, 2 * a_qk:2 * a_qk + a_v]
    o = 2 * a_qk + a_v
    qb = w_in[:, o:o + b_w] * (B_DH ** -0.5)
    rest = w_in[:, o + b_w:]
    return jnp.concatenate([qa, ka, va, qb, rest], axis=1).astype(BF16)


def _gla_weights(w_in):
    d = w_in.shape[0]
    n_main = 2 * C_HEADS * C_DK + 2 * C_HEADS * C_DV
    pad = jnp.zeros((d, LANES - 2 * C_RANK), w_in.dtype)
    return jnp.concatenate([w_in, pad], axis=1).astype(BF16), n_main


def kernel(x, c, ctx, c_ctx, w_mod, b_mod, norm_g, final_g, att_w_in, att_w_out, att_lambda, att_subln_g, na_bias,
           gla_w_in, gla_w_gate, gla_b_gate, gla_norm_g, gla_w_out, router_w, router_bias, moe_w1, moe_w3, moe_w2):
    b, seq, d = x.shape
    n = N_CTX + seq
    s = jnp.concatenate([ctx, x], axis=1)

    rows = b + 1
    rows_pad = -(-rows // 8) * 8
    cc = jnp.concatenate([c, c_ctx[None, :], jnp.zeros((rows_pad - rows, d), F32)], axis=0)
    mod_all = _mod_vectors(cc, w_mod, b_mod)
    mod_x = mod_all[:, :b].reshape(DEPTH, b, 1, 6, d)
    mod_c = jnp.broadcast_to(mod_all[:, b].reshape(DEPTH, 1, 1, 6, d), (DEPTH, b, 1, 6, d))
    mods = jnp.concatenate([mod_c, mod_x], axis=2)

    cos, sin = _rope_tables(seq)
    rw_hi = router_w.astype(BF16)
    rw_lo = (router_w - rw_hi.astype(F32)).astype(BF16)
    zpad = jnp.zeros((d, LANES - 2 * N_EXPERTS), BF16)
    rw1 = jnp.concatenate([rw_hi, rw_lo, zpad], axis=1)
    rw2 = jnp.concatenate([rw_hi, jnp.zeros((d, N_EXPERTS), BF16), zpad], axis=1)

    w1_b, w3_b, w2_b = moe_w1.astype(BF16), moe_w3.astype(BF16), moe_w2.astype(BF16)
    y = None
    for i in range(DEPTH):
        j = i // 2
        modp = mods[i - 1] if i else None
        if i % 2 == 0:
            lam_init = 0.8 - 0.6 * math.exp(-0.3 * i)
            w = _att_weights(att_w_in[j])
            outs = _project(s, y, modp, mods[i], norm_g[i, 0], w, cos, sin,
                            n_rope=2 * A_HEADS * LANES, n_bf16=w.shape[1])
            if i:
                s = outs[0]
            p = outs[-1]
            oa = _diff_attention(p, att_lambda[j], att_subln_g[j], lam_init)
            ob_x, ob_c = _neighbourhood_attention(p, _na_bias_table(na_bias[j], seq // GRID_W))
            s, h2, logits = _out_project((oa, ob_x, ob_c), att_w_out[j].astype(BF16), s, mods[i], norm_g[i, 1],
                                         rw1, rw2, gla=False)
        else:
            w, n_main = _gla_weights(gla_w_in[j])
            outs = _project(s, y, modp, mods[i], norm_g[i, 0], w, None, None, n_rope=0, n_bf16=n_main)
            s, p, lr = outs
            wg = jnp.zeros((2, LANES, C_HEADS * C_DK), F32)
            wg = wg.at[0, :C_RANK].set(gla_w_gate[j, 0]).at[1, C_RANK:2 * C_RANK].set(gla_w_gate[j, 1])
            o = _gla(p, lr, wg, gla_b_gate[j])
            s, h2, logits = _out_project((o, p), gla_w_out[j].astype(BF16), s, mods[i], norm_g[i, 1],
                                         rw1, rw2, gla=True, gn=gla_norm_g[j])
        y = _moe(h2.reshape(b * n, ROW_SEGS, LANES), logits.reshape(b * n, LANES), router_bias,
                 i, w1_b, w3_b, w2_b)
        y = y.reshape(b * n * ROW_SEGS, LANES)
    return _final(s, y, mods[DEPTH - 1], final_g)
```

```python
import functools
import math

import jax
import jax.numpy as jnp
import numpy as np
from jax import lax
from jax.experimental import pallas as pl
from jax.experimental.pallas import tpu as pltpu
from jax.experimental.pallas import tpu_sc as plsc

F32 = jnp.float32
BF16 = jnp.bfloat16

D_MODEL = 1024
DEPTH = 4
GRID_W = 64
N_CTX = 256
A_HEADS = 4
A_DH = 64
A_DV = 128
B_HEADS = 8
B_DH = 64
NA_ROWS = 8
NA_COLS = 16
C_HEADS = 4
C_DK = 128
C_DV = 256
C_RANK = 16
C_TAU = 16.0
C_CHUNK = 64
GLA_BLOCK = 256
N_EXPERTS = 16
N_GROUPS = 4
PER_GROUP = 4
D_EXPERT = 512
ROPE_BASE = 10000.0
EPS = 1e-6
LOG2_E = math.log2(math.e)
ATT_KEY_BLOCK = 768

LANES = 128
ROW_TILE = 256
MOE_TILE = 256
SC_CORES = 2
SC_SUBCORES = 16
SC_WORKERS = SC_CORES * SC_SUBCORES
SC_GATHER_ROWS = 64
N_CLASSES = N_GROUPS * 6
NA_QROWS = 8
NA_SUB = 2
NA_KROWS = NA_SUB + NA_ROWS - 1
VMEM_LIMIT = 52 * 1024 * 1024
ROW_SEGS = D_MODEL // LANES // 2

PAIR_LO = (0, 0, 0, 1, 1, 2)
PAIR_HI = (1, 2, 3, 2, 3, 3)


def _cparams(sem):
    return pltpu.CompilerParams(dimension_semantics=sem, vmem_limit_bytes=VMEM_LIMIT)


def _sigmoid(x):
    return 1.0 / (1.0 + jnp.exp(-x))


def _silu(x):
    return x * _sigmoid(x)


def _rms(x, g):
    return x * lax.rsqrt(jnp.mean(x * x, axis=-1, keepdims=True) + EPS) * g


def _dot(a, b):
    return jnp.dot(a, b, preferred_element_type=F32)


def _dot_nt(a, b):
    return lax.dot_general(a, b, (((1,), (1,)), ((), ())), preferred_element_type=F32)


def _dot_tn(a, b):
    return lax.dot_general(a, b, (((0,), (0,)), ((), ())), preferred_element_type=F32)


def _mod_kernel(c_ref, w_ref, b_ref, o_ref):
    a = _silu(c_ref[...]).astype(BF16)
    o_ref[0] = _dot(a, w_ref[0].astype(BF16)) + b_ref[0]


def _mod_vectors(cc, w_mod, b_mod):
    depth, d, n6 = w_mod.shape
    rows = cc.shape[0]
    tn = 1536
    return pl.pallas_call(
        _mod_kernel,
        out_shape=jax.ShapeDtypeStruct((depth, rows, n6), F32),
        grid=(depth, n6 // tn),
        in_specs=[
            pl.BlockSpec((rows, d), lambda i, j: (0, 0)),
            pl.BlockSpec((1, d, tn), lambda i, j: (i, 0, j)),
            pl.BlockSpec((1, 1, tn), lambda i, j: (i, 0, j)),
        ],
        out_specs=pl.BlockSpec((1, rows, tn), lambda i, j: (i, 0, j)),
        compiler_params=_cparams(("arbitrary", "arbitrary")),
        name="mod_vectors",
    )(cc, w_mod, b_mod.reshape(depth, 1, n6))


def _token_rows(ref):
    rows = ref.shape[0] // ROW_SEGS
    packed = jnp.concatenate([ref[pl.ds(sg, rows, stride=ROW_SEGS), :] for sg in range(ROW_SEGS)], axis=-1)
    low = lax.bitcast_convert_type(lax.shift_left(packed, jnp.int32(16)), F32)
    high = lax.bitcast_convert_type(packed & jnp.int32(-65536), F32)
    return jnp.concatenate([low, high], axis=-1)


def _store_token_rows(ref, val):
    rows, d = val.shape
    rounded = val.astype(BF16).astype(F32)
    bits = lax.bitcast_convert_type(rounded, jnp.int32)
    packed = lax.shift_right_logical(bits[:, :d // 2], jnp.int32(16)) | bits[:, d // 2:]
    for sg in range(ROW_SEGS):
        ref[pl.ds(sg, rows, stride=ROW_SEGS), :] = packed[:, sg * LANES:(sg + 1) * LANES]


def _proj_kernel(*refs, has_y, n_rope, n_bf16, col_chunk):
    it = iter(refs)
    s_ref = next(it)
    y_ref = next(it) if has_y else None
    modp_ref = next(it) if has_y else None
    mod_ref = next(it)
    ng_ref = next(it)
    w_ref = next(it)
    cos_ref = next(it) if n_rope else None
    sin_ref = next(it) if n_rope else None
    snew_ref = next(it) if has_y else None
    p_ref = next(it)
    lr_ref = next(it) if w_ref.shape[1] > n_bf16 else None

    x = s_ref[0]
    if has_y:
        x = x + modp_ref[0, 0, 5:6, :] * _token_rows(y_ref)
        snew_ref[0] = x
    h = _rms(x, ng_ref[...])
    h = h * (1.0 + mod_ref[0, 0, 1:2, :]) + mod_ref[0, 0, 0:1, :]
    hb = h.astype(BF16)
    n_out = w_ref.shape[1]
    for c0 in range(0, n_out, col_chunk):
        c1 = min(c0 + col_chunk, n_out)
        acc = _dot(hb, w_ref[:, c0:c1])
        for b0 in range(c0, c1, LANES):
            t = acc[:, b0 - c0:b0 - c0 + LANES]
            if b0 < n_rope:
                t = t * cos_ref[...] + pltpu.roll(t, LANES // 2, 1) * sin_ref[...]
            if b0 < n_bf16:
                p_ref[0, :, b0:b0 + LANES] = t.astype(BF16)
            else:
                lr_ref[0, :, b0 - n_bf16:b0 - n_bf16 + LANES] = t


def _project(s, y, modp, mod, ng, w, cos, sin, *, n_rope, n_bf16):
    b, n, d = s.shape
    n_out = w.shape[1]
    tm = ROW_TILE
    has_y = y is not None
    row_spec = pl.BlockSpec((1, tm, d), lambda i, j: (i, j, 0))
    mod_spec = pl.BlockSpec((1, 1, 6, d), lambda i, j: (i, jnp.minimum(j, 1), 0, 0))
    in_specs = [row_spec]
    args = [s]
    if has_y:
        in_specs += [pl.BlockSpec((tm * ROW_SEGS, LANES), lambda i, j: (i * (n // tm) + j, 0)), mod_spec]
        args += [y, modp]
    in_specs += [mod_spec, pl.BlockSpec((1, d), lambda i, j: (0, 0)),
                 pl.BlockSpec((d, n_out), lambda i, j: (0, 0))]
    args += [mod, ng.reshape(1, d), w]
    if n_rope:
        tab_spec = pl.BlockSpec((tm, LANES), lambda i, j: (j, 0))
        in_specs += [tab_spec, tab_spec]
        args += [cos, sin]
    out_shape, out_specs = [], []
    if has_y:
        out_shape.append(jax.ShapeDtypeStruct((b, n, d), F32))
        out_specs.append(row_spec)
    out_shape.append(jax.ShapeDtypeStruct((b, n, n_bf16), BF16))
    out_specs.append(pl.BlockSpec((1, tm, n_bf16), lambda i, j: (i, j, 0)))
    if n_out > n_bf16:
        out_shape.append(jax.ShapeDtypeStruct((b, n, n_out - n_bf16), F32))
        out_specs.append(pl.BlockSpec((1, tm, n_out - n_bf16), lambda i, j: (i, j, 0)))
    return pl.pallas_call(
        functools.partial(_proj_kernel, has_y=has_y, n_rope=n_rope, n_bf16=n_bf16, col_chunk=512),
        out_shape=out_shape,
        grid=(b, n // tm),
        in_specs=in_specs,
        out_specs=out_specs,
        compiler_params=_cparams(("arbitrary", "arbitrary")),
        name="norm_mod_project",
    )(*args)


def _softmax_rows(s):
    e = jnp.exp(s - jnp.max(s, axis=-1, keepdims=True))
    return e * (1.0 / jnp.sum(e, axis=-1, keepdims=True))


def _diff_attn_kernel(q_ref, k_ref, v_ref, lam_ref, g_ref, o_ref, *, lam_init):
    lp = lam_ref[...]
    lam = (jnp.exp(jnp.sum(lp[0:1] * lp[1:2], axis=-1, keepdims=True))
           - jnp.exp(jnp.sum(lp[2:3] * lp[3:4], axis=-1, keepdims=True)) + lam_init)
    q = q_ref[0]
    lane = lax.broadcasted_iota(jnp.int32, (1, LANES), 1)
    first_map = (lane // 32) % 2 == 0
    zero = jnp.zeros_like(q)
    q0 = jnp.where(first_map, q, zero)
    q1 = jnp.where(first_map, zero, q)

    def attend(n_keys):
        blk = min(ATT_KEY_BLOCK, n_keys)
        ones = jnp.ones((blk, A_DV), BF16)
        state = [None, None]
        for kb in range(n_keys // blk):
            k = k_ref[0, kb * blk:(kb + 1) * blk, :]
            v1 = jnp.concatenate([v_ref[0, kb * blk:(kb + 1) * blk, :], ones], axis=1)
            for i, qm in enumerate((q0, q1)):
                s = _dot_nt(qm, k)
                m_new = jnp.max(s, axis=-1, keepdims=True)
                if kb:
                    m_old, acc_old = state[i]
                    m_new = jnp.maximum(m_old, m_new)
                acc = _dot(jnp.exp2(s - m_new).astype(BF16), v1)
                if kb:
                    acc = acc_old * jnp.exp2(m_old - m_new) + acc
                state[i] = (m_new, acc)
        acc0, acc1 = state[0][1], state[1][1]
        o = acc0[:, :A_DV] * (1.0 / acc0[:, A_DV:]) - acc1[:, :A_DV] * (lam / acc1[:, A_DV:])
        o_ref[0] = (_rms(o, g_ref[...]) * (1.0 - lam_init)).astype(BF16)

    qi = pl.program_id(2)

    @pl.when(qi == 0)
    def _():
        attend(N_CTX)

    @pl.when(qi > 0)
    def _():
        attend(k_ref.shape[1])


def _diff_attention(p, lam_p, subln_g, lam_init):
    b, n, _ = p.shape
    tq = ROW_TILE
    return pl.pallas_call(
        functools.partial(_diff_attn_kernel, lam_init=lam_init),
        out_shape=jax.ShapeDtypeStruct((b, n, A_HEADS * A_DV), BF16),
        grid=(b, A_HEADS, n // tq),
        in_specs=[
            pl.BlockSpec((1, tq, LANES), lambda i, h, j: (i, j, h)),
            pl.BlockSpec((1, n, LANES), lambda i, h, j: (i, 0, A_HEADS + h)),
            pl.BlockSpec((1, n, LANES), lambda i, h, j: (i, 0, 2 * A_HEADS + h)),
            pl.BlockSpec((4, A_DH), lambda i, h, j: (0, 0)),
            pl.BlockSpec((1, A_DV), lambda i, h, j: (0, 0)),
        ],
        out_specs=pl.BlockSpec((1, tq, LANES), lambda i, h, j: (i, j, h)),
        compiler_params=_cparams(("arbitrary", "arbitrary", "arbitrary")),
        name="diff_attention",
    )(p, p, p, lam_p, subln_g.reshape(1, A_DV))


def _split_heads(q):
    lane = lax.broadcasted_iota(jnp.int32, (1, LANES), 1)
    first_head = lane < B_DH
    zero = jnp.zeros_like(q)
    return first_head, (jnp.where(first_head, q, zero), jnp.where(first_head, zero, q))


def _na_ctx_kernel(q_ref, k_ref, v_ref, o_ref):
    first_head, qh = _split_heads(q_ref[0])
    kc = k_ref[0]
    vc = v_ref[0]
    outs = [_dot(_softmax_rows(_dot_nt(qh[h], kc)).astype(BF16), vc) for h in range(2)]
    o_ref[0] = jnp.where(first_head, outs[0], outs[1]).astype(BF16)


def _na_kernel(qa_ref, qb_ref, k_ref, v_ref, bm_ref, o_ref):
    blk = pl.program_id(2)
    rows = (k_ref.shape[1] - N_CTX) // GRID_W
    sub_q = NA_SUB * GRID_W
    sub_k = NA_KROWS * GRID_W
    kc = k_ref[0, 0:N_CTX, :]
    vc = jnp.concatenate([v_ref[0, 0:N_CTX, :], jnp.ones((N_CTX, LANES), BF16)], axis=1)
    ones = jnp.ones((sub_k, LANES), BF16)
    for sub in range(NA_QROWS // NA_SUB):
        q_ref = qa_ref if sub * sub_q < qa_ref.shape[1] else qb_ref
        q0 = (sub * sub_q) % qa_ref.shape[1]
        first_head, qh = _split_heads(q_ref[0, q0:q0 + sub_q, :])
        k_row0 = jnp.clip(blk * NA_QROWS + sub * NA_SUB - NA_ROWS // 2, 0, rows - NA_KROWS)
        start = pl.multiple_of(N_CTX + k_row0 * GRID_W, GRID_W)
        kw = k_ref[0, pl.ds(start, sub_k), :]
        vw = jnp.concatenate([v_ref[0, pl.ds(start, sub_k), :], ones], axis=1)
        outs = []
        for h in range(2):
            s_loc = _dot_nt(qh[h], kw) + bm_ref[h, 0, sub]
            s_ctx = _dot_nt(qh[h], kc)
            m = jnp.maximum(jnp.max(s_loc, axis=-1, keepdims=True), jnp.max(s_ctx, axis=-1, keepdims=True))
            acc = _dot(jnp.exp(s_loc - m).astype(BF16), vw) + _dot(jnp.exp(s_ctx - m).astype(BF16), vc)
            outs.append(acc[:, :LANES] * (1.0 / acc[:, LANES:]))
        o_ref[0, sub * sub_q:(sub + 1) * sub_q, :] = jnp.where(first_head, outs[0], outs[1]).astype(BF16)


def _na_bias_table(na_bias, rows):
    h = na_bias.shape[0]
    n_dr, n_dc = 2 * NA_ROWS - 1, 2 * NA_COLS - 1
    width = 2 * GRID_W
    left = GRID_W - NA_COLS
    u = jnp.pad(na_bias, ((0, 0), (0, 0), (left, width - left - n_dc)))
    skew = jnp.tile(u, (1, 1, GRID_W))[:, :, :GRID_W * (width - 1)].reshape(h, n_dr, GRID_W, width - 1)
    toeplitz = skew[:, :, :, GRID_W - 1:]
    margin = NA_KROWS - NA_ROWS
    by_col = jnp.pad(toeplitz.transpose(0, 2, 1, 3), ((0, 0), (0, 0), (margin, margin), (0, 0)))
    by_col = by_col.reshape(h, GRID_W, (n_dr + 2 * margin) * GRID_W)
    col = np.arange(GRID_W)
    col_start = np.clip(col - NA_COLS // 2, 0, GRID_W - NA_COLS)
    col_ok = (col[None, :] >= col_start[:, None]) & (col[None, :] < col_start[:, None] + NA_COLS)
    n_sub = NA_QROWS // NA_SUB
    blocks = []
    for q_row0 in (0, NA_QROWS, rows - NA_QROWS):
        for sub in range(n_sub):
            k_row0 = int(np.clip(q_row0 + sub * NA_SUB - NA_ROWS // 2, 0, rows - NA_KROWS))
            kr = k_row0 + np.arange(NA_KROWS)
            for rq in range(NA_SUB):
                r = q_row0 + sub * NA_SUB + rq
                r0 = int(np.clip(r - NA_ROWS // 2, 0, rows - NA_ROWS))
                row_ok = (kr >= r0) & (kr < r0 + NA_ROWS)
                ok = (col_ok[:, None, :] & row_ok[None, :, None]).reshape(GRID_W, NA_KROWS * GRID_W)
                first = k_row0 - r + NA_ROWS - 1 + margin
                window = by_col[:, :, first * GRID_W:(first + NA_KROWS) * GRID_W]
                blocks.append(jnp.where(ok[None], window, -jnp.inf))
    table = jnp.stack(blocks, axis=1)
    return table.reshape(h, 3, n_sub, NA_SUB * GRID_W, NA_KROWS * GRID_W)


def _neighbourhood_attention(p, bm):
    b, n, _ = p.shape
    tq = NA_QROWS * GRID_W
    half = tq // 2
    n_blocks = (n - N_CTX) // tq
    ctx_blocks = N_CTX // half
    col0 = 3 * A_HEADS
    pairs = B_HEADS // 2
    width = B_HEADS * B_DH

    def pattern(j):
        return jnp.minimum(j, 1) + (j == n_blocks - 1).astype(jnp.int32)

    ob_x = pl.pallas_call(
        _na_kernel,
        out_shape=jax.ShapeDtypeStruct((b, n - N_CTX, width), BF16),
        grid=(b, pairs, n_blocks),
        in_specs=[
            pl.BlockSpec((1, half, LANES), lambda i, h, j: (i, ctx_blocks + 2 * j, col0 + h)),
            pl.BlockSpec((1, half, LANES), lambda i, h, j: (i, ctx_blocks + 2 * j + 1, col0 + h)),
            pl.BlockSpec((1, n, LANES), lambda i, h, j: (i, 0, col0 + pairs + h)),
            pl.BlockSpec((1, n, LANES), lambda i, h, j: (i, 0, col0 + 2 * pairs + h)),
            pl.BlockSpec((2, 1, NA_QROWS // NA_SUB, NA_SUB * GRID_W, NA_KROWS * GRID_W),
                         lambda i, h, j: (h, pattern(j), 0, 0, 0)),
        ],
        out_specs=pl.BlockSpec((1, tq, LANES), lambda i, h, j: (i, j, h)),
        compiler_params=_cparams(("arbitrary", "arbitrary", "arbitrary")),
        name="neighbourhood_attention",
    )(p, p, p, p, bm)
    ob_c = pl.pallas_call(
        _na_ctx_kernel,
        out_shape=jax.ShapeDtypeStruct((b, N_CTX, width), BF16),
        grid=(b, pairs),
        in_specs=[
            pl.BlockSpec((1, N_CTX, LANES), lambda i, h: (i, 0, col0 + h)),
            pl.BlockSpec((1, N_CTX, LANES), lambda i, h: (i, 0, col0 + pairs + h)),
            pl.BlockSpec((1, N_CTX, LANES), lambda i, h: (i, 0, col0 + 2 * pairs + h)),
        ],
        out_specs=pl.BlockSpec((1, N_CTX, LANES), lambda i, h: (i, 0, h)),
        compiler_params=_cparams(("arbitrary", "arbitrary")),
        name="context_attention",
    )(p, p, p)
    return ob_x, ob_c


def _gla_kernel(q_ref, k_ref, v_ref, lr_ref, wg_ref, bg_ref, o_ref, ob_ref, g_ref):
    n = q_ref.shape[1]
    blk = GLA_BLOCK
    per_blk = blk // C_CHUNK
    n_blocks = n // blk
    ri = lax.broadcasted_iota(jnp.int32, (blk, blk), 0)
    ci = lax.broadcasted_iota(jnp.int32, (blk, blk), 1)
    same_chunk = (ri // C_CHUNK) == (ci // C_CHUNK)
    keeps = (same_chunk & (ci <= ri), same_chunk & (ci >= ri))
    tris = tuple(jnp.where(kp, 1.0, 0.0).astype(BF16) for kp in keeps)
    w_gate = jnp.concatenate([wg_ref[0], wg_ref[1]], axis=1).astype(BF16)
    b_gate = jnp.concatenate([bg_ref[0:1, :], bg_ref[1:2, :]], axis=1)
    for i in range(n_blocks):
        z = _dot(lr_ref[0, i * blk:(i + 1) * blk, :].astype(BF16), w_gate) + b_gate
        g_ref[i * blk:(i + 1) * blk, :] = (jnp.minimum(z, 0.0) - jnp.log(1.0 + jnp.exp(-jnp.abs(z)))) * (1.0 / C_TAU)

    def block(sb, direction, st):
        keep = keeps[direction]
        end_row = C_CHUNK - 1 if direction == 0 else 0
        mid_row = C_CHUNK // 2 - 1 if direction == 0 else C_CHUNK // 2
        r0 = sb * blk
        g = g_ref[r0:r0 + blk, direction * C_DK:(direction + 1) * C_DK]
        g_hi = g.astype(BF16)
        g_lo = (g - g_hi.astype(F32)).astype(BF16)
        gc2 = _dot(tris[direction], jnp.concatenate([g_hi, g_lo], axis=1))
        gc = (gc2[:, :C_DK] + gc2[:, C_DK:]).reshape(per_blk, C_CHUNK, C_DK)
        g_end = gc[:, end_row:end_row + 1, :]
        g_mid = gc[:, mid_row:mid_row + 1, :]
        q = (q_ref[0, r0:r0 + blk, :].astype(F32) * (C_DK ** -0.5)).reshape(per_blk, C_CHUNK, C_DK)
        k = k_ref[0, r0:r0 + blk, :].astype(F32).reshape(per_blk, C_CHUNK, C_DK)
        v = v_ref[0, r0:r0 + blk, :]
        q_in = (q * jnp.exp(gc)).astype(BF16).reshape(blk, C_DK)
        q_mid = (q * jnp.exp(gc - g_mid)).astype(BF16).reshape(blk, C_DK)
        k_mid = (k * jnp.exp(g_mid - gc)).astype(BF16).reshape(blk, C_DK)
        k_end = (k * jnp.exp(g_end - gc)).astype(BF16).reshape(blk, C_DK)
        a = jnp.where(keep, _dot_nt(q_mid, k_mid), 0.0)
        o_intra = _dot(a.astype(BF16), v)
        ends = jnp.concatenate([g_end.reshape(per_blk, C_DK), jnp.zeros((8 - per_blk, C_DK), F32)], axis=0)
        decay = jnp.transpose(jnp.exp(ends))
        o_inter = [None] * per_blk
        for c in (range(per_blk) if direction == 0 else reversed(range(per_blk))):
            rows = slice(c * C_CHUNK, (c + 1) * C_CHUNK)
            o_inter[c] = _dot(q_in[rows], st.astype(BF16))
            st = st * decay[:, c:c + 1] + _dot_tn(k_end[rows], v[rows])
        return o_intra + jnp.concatenate(o_inter, axis=0), st

    st_f = jnp.zeros((C_DK, C_DV), F32)
    st_b = jnp.zeros((C_DK, C_DV), F32)
    ctx_blocks = N_CTX // blk
    order_b = list(reversed(range(ctx_blocks))) + list(reversed(range(ctx_blocks, n_blocks)))
    for i in range(n_blocks):
        o, st_f = block(i, 0, st_f)
        o_ref[0, i * blk:(i + 1) * blk, :] = o
        sb = order_b[i]
        o, st_b = block(sb, 1, st_b)
        ob_ref[sb * blk:(sb + 1) * blk, :] = o
    o_ref[0] = o_ref[0] + ob_ref[...]


def _gla(p, lr, wg, bg):
    b, n, _ = p.shape
    return pl.pallas_call(
        _gla_kernel,
        out_shape=jax.ShapeDtypeStruct((b, n, C_HEADS * C_DV), F32),
        grid=(b, C_HEADS),
        in_specs=[
            pl.BlockSpec((1, n, C_DK), lambda i, h: (i, 0, h)),
            pl.BlockSpec((1, n, C_DK), lambda i, h: (i, 0, C_HEADS + h)),
            pl.BlockSpec((1, n, C_DV), lambda i, h: (i, 0, C_HEADS + h)),
            pl.BlockSpec((1, n, LANES), lambda i, h: (i, 0, 0)),
            pl.BlockSpec((2, LANES, C_DK), lambda i, h: (0, 0, h)),
            pl.BlockSpec((2, C_DK), lambda i, h: (0, h)),
        ],
        out_specs=pl.BlockSpec((1, n, C_DV), lambda i, h: (i, 0, h)),
        scratch_shapes=[pltpu.VMEM((n, C_DV), F32), pltpu.VMEM((n, 2 * C_DK), F32)],
        compiler_params=_cparams(("arbitrary", "arbitrary")),
        name="gla",
    )(p, p, p, lr, wg, bg)


def _out_kernel(*refs, gla):
    if gla:
        o_ref, gate_ref, gn_ref, w_ref, s_ref, mod_ref, ng_ref, rw1_ref, rw2_ref, snew_ref, h2_ref, lg_ref = refs
        o = o_ref[0]
        gate = gate_ref[0].astype(F32)
        parts = []
        for hd in range(C_HEADS):
            oh = _rms(o[:, hd * C_DV:(hd + 1) * C_DV], gn_ref[...])
            parts.append((oh * _silu(gate[:, hd * C_DV:(hd + 1) * C_DV])).astype(BF16))
        acc = _dot(jnp.concatenate(parts, axis=-1), w_ref[...])
    else:
        oa_ref, obx_ref, obc_ref, w_ref, s_ref, mod_ref, ng_ref, rw1_ref, rw2_ref, snew_ref, h2_ref, lg_ref = refs
        half = oa_ref.shape[2]
        ob = jnp.where(pl.program_id(1) == 0, obc_ref[0], obx_ref[0])
        acc = _dot(oa_ref[0], w_ref[0:half, :]) + _dot(ob, w_ref[half:, :])
    x = s_ref[0] + mod_ref[0, 0, 2:3, :] * acc
    snew_ref[0] = x
    h2 = _rms(x, ng_ref[...]) * (1.0 + mod_ref[0, 0, 4:5, :]) + mod_ref[0, 0, 3:4, :]
    _store_token_rows(h2_ref, h2)
    hi = h2.astype(BF16)
    lo = (h2 - hi.astype(F32)).astype(BF16)
    t = _dot(hi, rw1_ref[...])
    lg_ref[0] = t + pltpu.roll(t, LANES - N_EXPERTS, 1) + _dot(lo, rw2_ref[...])


def _out_project(mix, w_out, s, mod, ng, rw1, rw2, *, gla, gn=None):
    b, n, d = s.shape
    tm = ROW_TILE
    row_spec = pl.BlockSpec((1, tm, d), lambda i, j: (i, j, 0))
    const2 = lambda i, j: (0, 0)
    if gla:
        o, p = mix
        in_specs = [row_spec, pl.BlockSpec((1, tm, d), lambda i, j: (i, j, 2)),
                    pl.BlockSpec((1, C_DV), const2)]
        args = [o, p, gn.reshape(1, C_DV)]
    else:
        oa, ob_x, ob_c = mix
        half = oa.shape[2]
        in_specs = [pl.BlockSpec((1, tm, half), lambda i, j: (i, j, 0)),
                    pl.BlockSpec((1, tm, half), lambda i, j: (i, jnp.maximum(j - 1, 0), 0)),
                    pl.BlockSpec((1, tm, half), lambda i, j: (i, 0, 0))]
        args = [oa, ob_x, ob_c]
    in_specs += [pl.BlockSpec((d, d), const2), row_spec,
                 pl.BlockSpec((1, 1, 6, d), lambda i, j: (i, jnp.minimum(j, 1), 0, 0)),
                 pl.BlockSpec((1, d), const2), pl.BlockSpec((d, LANES), const2), pl.BlockSpec((d, LANES), const2)]
    args += [w_out, s, mod, ng.reshape(1, d), rw1, rw2]
    return pl.pallas_call(
        functools.partial(_out_kernel, gla=gla),
        out_shape=[jax.ShapeDtypeStruct((b, n, d), F32), jax.ShapeDtypeStruct((b * n * ROW_SEGS, LANES), jnp.int32),
                   jax.ShapeDtypeStruct((b, n, LANES), F32)],
        grid=(b, n // tm),
        in_specs=in_specs,
        out_specs=[row_spec, pl.BlockSpec((tm * ROW_SEGS, LANES), lambda i, j: (i * (n // tm) + j, 0)),
                   pl.BlockSpec((1, tm, LANES), lambda i, j: (i, j, 0))],
        compiler_params=_cparams(("arbitrary", "arbitrary")),
        name="out_project",
    )(*args)


def _route_kernel(bias_ref, lg_ref, cls_ref, wlo_ref, whi_ref):
    score = [_sigmoid(lg_ref[e]) for e in range(N_EXPERTS)]
    sel = [score[e] + bias_ref[e] for e in range(N_EXPERTS)]
    grp_score = []
    for g in range(N_GROUPS):
        v = sel[g * PER_GROUP:(g + 1) * PER_GROUP]
        best = v[0] + v[1]
        for a in range(PER_GROUP):
            for c in range(a + 1, PER_GROUP):
                if (a, c) != (0, 1):
                    best = jnp.maximum(best, v[a] + v[c])
        grp_score.append(best)
    grp = jnp.zeros(grp_score[0].shape, jnp.int32)
    best = grp_score[0]
    for g in range(1, N_GROUPS):
        upd = grp_score[g] > best
        best = jnp.where(upd, grp_score[g], best)
        grp = jnp.where(upd, g, grp)

    def pick(vals, j):
        out = vals[j]
        for g in range(1, N_GROUPS):
            out = jnp.where(grp == g, vals[g * PER_GROUP + j], out)
        return out

    v = [pick(sel, j) for j in range(PER_GROUP)]
    sc = [pick(score, j) for j in range(PER_GROUP)]
    one = jnp.ones(grp.shape, jnp.int32)
    zero = jnp.zeros(grp.shape, jnp.int32)
    chosen = []
    for j in range(PER_GROUP):
        rank = zero
        for m in range(PER_GROUP):
            if m == j:
                continue
            ahead = (v[m] >= v[j]) if m < j else (v[m] > v[j])
            rank = rank + jnp.where(ahead, one, zero)
        chosen.append(rank < 2)
    code = zero
    for j in range(PER_GROUP):
        code = code + jnp.where(chosen[j], one * (1 << j), zero)
    pair = zero
    for idx in range(6):
        pair = jnp.where(code == (1 << PAIR_LO[idx]) + (1 << PAIR_HI[idx]), idx, pair)
    s_lo = jnp.where(chosen[0], sc[0], jnp.where(chosen[1], sc[1], sc[2]))
    s_hi = jnp.where(chosen[3], sc[3], jnp.where(chosen[2], sc[2], sc[1]))
    den = s_lo + s_hi
    cls_ref[...] = grp * 6 + pair
    wlo_ref[...] = s_lo / den
    whi_ref[...] = s_hi / den


def _route(logits_t, router_bias):
    _, r, _ = logits_t.shape
    full = pl.BlockSpec((r, LANES), lambda i: (0, 0))
    return pl.pallas_call(
        _route_kernel,
        out_shape=[jax.ShapeDtypeStruct((r, LANES), jnp.int32), jax.ShapeDtypeStruct((r, LANES), F32),
                   jax.ShapeDtypeStruct((r, LANES), F32)],
        grid=(1,),
        in_specs=[pl.BlockSpec(memory_space=pltpu.SMEM),
                  pl.BlockSpec((N_EXPERTS, r, LANES), lambda i: (0, 0, 0))],
        out_specs=[full, full, full],
        compiler_params=_cparams(("arbitrary",)),
        name="route",
    )(router_bias, logits_t)


def _gather_rows(table, idx):
    p = idx.shape[0]
    per_worker = p // SC_WORKERS
    n_chunks = per_worker // SC_GATHER_ROWS
    mesh = plsc.VectorSubcoreMesh(core_axis_name="c", subcore_axis_name="s", num_cores=SC_CORES,
                                  num_subcores=SC_SUBCORES)

    @functools.partial(
        pl.kernel, mesh=mesh,
        out_type=jax.ShapeDtypeStruct((p,) + table.shape[1:], table.dtype),
        scratch_types=[pltpu.VMEM((per_worker,), jnp.int32),
                       pltpu.VMEM((SC_GATHER_ROWS,) + table.shape[1:], table.dtype),
                       pltpu.SemaphoreType.DMA],
        compiler_params=pltpu.CompilerParams(use_tc_tiling_on_sc=True),
        name="gather_rows",
    )
    def gather(table_hbm, idx_hbm, out_hbm, idx_v, rows_v, sem):
        worker = lax.axis_index("s") * SC_CORES + lax.axis_index("c")
        base = worker * per_worker
        pltpu.sync_copy(idx_hbm.at[pl.ds(base, per_worker)], idx_v)

        @pl.loop(0, n_chunks)
        def _(i):
            off = pl.multiple_of(i * SC_GATHER_ROWS, SC_GATHER_ROWS)
            pltpu.async_copy(table_hbm.at[idx_v.at[pl.ds(off, SC_GATHER_ROWS)]], rows_v, sem).wait()
            pltpu.sync_copy(rows_v, out_hbm.at[pl.ds(base + off, SC_GATHER_ROWS)])

    return gather(table, idx)


def _moe_kernel(lo_ref, hi_ref, nt_ref, x_ref, wr_ref, w1l_ref, w1h_ref, w3l_ref, w3h_ref, w2l_ref, w2h_ref, y_ref):
    j = pl.program_id(0)

    @pl.when(j < nt_ref[0])
    def _():
        x = _token_rows(x_ref).astype(BF16)
        wr = wr_ref[...]
        he_lo = (_silu(_dot(x, w1l_ref[0, 0])) * _dot(x, w3l_ref[0, 0]) * wr[:, 0:1]).astype(BF16)
        he_hi = (_silu(_dot(x, w1h_ref[0, 0])) * _dot(x, w3h_ref[0, 0]) * wr[:, 1:2]).astype(BF16)
        y = _dot(he_lo, w2l_ref[0, 0]) + _dot(he_hi, w2h_ref[0, 0])
        _store_token_rows(y_ref, y)

    @pl.when(j >= nt_ref[0])
    def _():
        y_ref[...] = jnp.zeros_like(y_ref)


def _moe_experts(xs, wrow, tile_lo, tile_hi, n_tiles, layer, w1, w3, w2):
    tp = xs.shape[0] // ROW_SEGS
    tm = MOE_TILE
    d, de = w1.shape[2], w1.shape[3]

    def gate_map(j, lo, hi, nt):
        return (jnp.minimum(j, nt[0] - 1), 0)

    def lo_map(j, lo, hi, nt):
        return (layer, lo[jnp.minimum(j, nt[0] - 1)], 0, 0)

    def hi_map(j, lo, hi, nt):
        return (layer, hi[jnp.minimum(j, nt[0] - 1)], 0, 0)

    up = (1, 1, d, de)
    down = (1, 1, de, d)
    return pl.pallas_call(
        _moe_kernel,
        out_shape=jax.ShapeDtypeStruct((tp * ROW_SEGS, LANES), jnp.int32),
        grid_spec=pltpu.PrefetchScalarGridSpec(
            num_scalar_prefetch=3,
            grid=(tp // tm,),
            in_specs=[pl.BlockSpec((tm * ROW_SEGS, LANES), gate_map), pl.BlockSpec((tm, 2), gate_map),
                      pl.BlockSpec(up, lo_map), pl.BlockSpec(up, hi_map),
                      pl.BlockSpec(up, lo_map), pl.BlockSpec(up, hi_map),
                      pl.BlockSpec(down, lo_map), pl.BlockSpec(down, hi_map)],
            out_specs=pl.BlockSpec((tm * ROW_SEGS, LANES), lambda j, lo, hi, nt: (j, 0)),
        ),
        compiler_params=_cparams(("arbitrary",)),
        name="moe_experts",
    )(tile_lo, tile_hi, n_tiles, xs, wrow, w1, w1, w3, w3, w2, w2)


def _moe(h2, logits, router_bias, layer, w1, w3, w2):
    t = h2.shape[0]
    tm = MOE_TILE
    n_tiles_max = t // tm + N_CLASSES
    tp = n_tiles_max * tm
    lg_t = logits[:, :N_EXPERTS].T.reshape(N_EXPERTS, t // LANES, LANES)
    cls, wlo, whi = _route(lg_t, router_bias)
    cls, wlo, whi = cls.reshape(t), wlo.reshape(t), whi.reshape(t)
    classes = jnp.arange(N_CLASSES, dtype=jnp.int32)
    onehot = (cls[:, None] == classes[None, :]).astype(jnp.int32)
    csum = jnp.cumsum(onehot, axis=0)
    rank = jnp.sum(csum * onehot, axis=1) - 1
    counts = csum[-1]
    tiles_per = (counts + tm - 1) // tm
    tile_end = jnp.cumsum(tiles_per)
    tile_start = tile_end - tiles_per
    dest = jnp.sum(onehot * (tile_start * tm)[None, :], axis=1) + rank
    tiles = jnp.arange(n_tiles_max, dtype=jnp.int32)
    tile_cls = jnp.minimum(jnp.sum((tile_end[None, :] <= tiles[:, None]).astype(jnp.int32), axis=1), N_CLASSES - 1)
    pair = tile_cls % 6
    base = (tile_cls // 6) * PER_GROUP
    pair_onehot = (pair[:, None] == jnp.arange(6, dtype=jnp.int32)[None, :]).astype(jnp.int32)
    tile_lo = base + jnp.sum(pair_onehot * jnp.asarray(PAIR_LO, jnp.int32)[None, :], axis=1)
    tile_hi = base + jnp.sum(pair_onehot * jnp.asarray(PAIR_HI, jnp.int32)[None, :], axis=1)
    per_token = jnp.stack([jnp.arange(t, dtype=jnp.int32), lax.bitcast_convert_type(wlo, jnp.int32),
                           lax.bitcast_convert_type(whi, jnp.int32)], axis=1)
    padding = jnp.stack([jnp.arange(tp, dtype=jnp.int32) % t, jnp.zeros((tp,), jnp.int32),
                         jnp.zeros((tp,), jnp.int32)], axis=1)
    per_row = padding.at[dest].set(per_token, unique_indices=True)
    tok = per_row[:, 0]
    wrow = lax.bitcast_convert_type(per_row[:, 1:3], F32)
    xs = _gather_rows(h2, tok).reshape(tp * ROW_SEGS, LANES)
    ys = _moe_experts(xs, wrow, tile_lo, tile_hi, tile_end[-1:], layer, w1, w3, w2)
    return _gather_rows(ys.reshape(tp, ROW_SEGS, LANES), dest)


def _final_kernel(s_ref, y_ref, mod_ref, g_ref, o_ref):
    x = s_ref[0] + mod_ref[0, 0, 5:6, :] * _token_rows(y_ref)
    o_ref[0] = _rms(x, g_ref[...])


def _final(s, y, mod, final_g):
    b, n, d = s.shape
    tm = ROW_TILE
    skip = N_CTX // tm
    row_spec = pl.BlockSpec((1, tm, d), lambda i, j: (i, j + skip, 0))
    return pl.pallas_call(
        _final_kernel,
        out_shape=jax.ShapeDtypeStruct((b, n - N_CTX, d), F32),
        grid=(b, (n - N_CTX) // tm),
        in_specs=[row_spec, pl.BlockSpec((tm * ROW_SEGS, LANES), lambda i, j: (i * (n // tm) + j + skip, 0)),
                  pl.BlockSpec((1, 1, 6, d), lambda i, j: (i, 1, 0, 0)),
                  pl.BlockSpec((1, d), lambda i, j: (0, 0))],
        out_specs=pl.BlockSpec((1, tm, d), lambda i, j: (i, j, 0)),
        compiler_params=_cparams(("arbitrary", "arbitrary")),
        name="final_norm",
    )(s, y, mod, final_g.reshape(1, d))


def _rope_tables(seq):
    pos = jnp.arange(seq)
    row_pos, col_pos = pos // GRID_W, pos % GRID_W
    n = A_DH // 2
    inv = ROPE_BASE ** (-jnp.arange(0, n, 2, dtype=F32) / n)
    ang_r = row_pos.astype(F32)[:, None] * inv[None, :]
    ang_c = col_pos.astype(F32)[:, None] * inv[None, :]
    ang = jnp.concatenate([ang_r, ang_c], axis=-1)
    cos = jnp.tile(jnp.cos(ang), (1, 4))
    sin = jnp.tile(jnp.sin(ang), (1, 4))
    sin = jnp.concatenate([-sin[:, :LANES // 2], sin[:, LANES // 2:]], axis=-1)
    cos = jnp.concatenate([jnp.ones((N_CTX, LANES), F32), cos], axis=0)
    sin = jnp.concatenate([jnp.zeros((N_CTX, LANES), F32), sin], axis=0)
    return cos, sin


def _interleave_maps(w):
    d = w.shape[0]
    w = w.reshape(d, A_HEADS, 2, 2, 2, A_DH // 4)
    return w.transpose(0, 1, 4, 2, 3, 5).reshape(d, A_HEADS * LANES)


def _att_weights(w_in):
    a_qk = A_HEADS * 2 * A_DH
    a_v = A_HEADS * A_DV
    b_w = B_HEADS * B_DH
    qa = _interleave_maps(w_in[:, :a_qk]) * (A_DH ** -0.5 * LOG2_E)
    ka = _interleave_maps(w_in[:, a_qk:2 * a_qk])
    va = w_in[:, 2 * a_qk:2 * a_qk + a_v]
    o = 2 * a_qk + a_v
    qb = w_in[:, o:o + b_w] * (B_DH ** -0.5)
    rest = w_in[:, o + b_w:]
    return jnp.concatenate([qa, ka, va, qb, rest], axis=1).astype(BF16)


def _gla_weights(w_in):
    d = w_in.shape[0]
    n_main = 2 * C_HEADS * C_DK + 2 * C_HEADS * C_DV
    pad = jnp.zeros((d, LANES - 2 * C_RANK), w_in.dtype)
    return jnp.concatenate([w_in, pad], axis=1).astype(BF16), n_main


def kernel(x, c, ctx, c_ctx, w_mod, b_mod, norm_g, final_g, att_w_in, att_w_out, att_lambda, att_subln_g, na_bias,
           gla_w_in, gla_w_gate, gla_b_gate, gla_norm_g, gla_w_out, router_w, router_bias, moe_w1, moe_w3, moe_w2):
    b, seq, d = x.shape
    n = N_CTX + seq
    s = jnp.concatenate([ctx, x], axis=1)

    rows = b + 1
    rows_pad = -(-rows // 8) * 8
    cc = jnp.concatenate([c, c_ctx[None, :], jnp.zeros((rows_pad - rows, d), F32)], axis=0)
    mod_all = _mod_vectors(cc, w_mod, b_mod)
    mod_x = mod_all[:, :b].reshape(DEPTH, b, 1, 6, d)
    mod_c = jnp.broadcast_to(mod_all[:, b].reshape(DEPTH, 1, 1, 6, d), (DEPTH, b, 1, 6, d))
    mods = jnp.concatenate([mod_c, mod_x], axis=2)

    cos, sin = _rope_tables(seq)
    rw_hi = router_w.astype(BF16)
    rw_lo = (router_w - rw_hi.astype(F32)).astype(BF16)
    zpad = jnp.zeros((d, LANES - 2 * N_EXPERTS), BF16)
    rw1 = jnp.concatenate([rw_hi, rw_lo, zpad], axis=1)
    rw2 = jnp.concatenate([rw_hi, jnp.zeros((d, N_EXPERTS), BF16), zpad], axis=1)

    w1_b, w3_b, w2_b = moe_w1.astype(BF16), moe_w3.astype(BF16), moe_w2.astype(BF16)
    y = None
    for i in range(DEPTH):
        j = i // 2
        modp = mods[i - 1] if i else None
        if i % 2 == 0:
            lam_init = 0.8 - 0.6 * math.exp(-0.3 * i)
            w = _att_weights(att_w_in[j])
            outs = _project(s, y, modp, mods[i], norm_g[i, 0], w, cos, sin,
                            n_rope=2 * A_HEADS * LANES, n_bf16=w.shape[1])
            if i:
                s = outs[0]
            p = outs[-1]
            oa = _diff_attention(p, att_lambda[j], att_subln_g[j], lam_init)
            ob_x, ob_c = _neighbourhood_attention(p, _na_bias_table(na_bias[j], seq // GRID_W))
            s, h2, logits = _out_project((oa, ob_x, ob_c), att_w_out[j].astype(BF16), s, mods[i], norm_g[i, 1],
                                         rw1, rw2, gla=False)
        else:
            w, n_main = _gla_weights(gla_w_in[j])
            outs = _project(s, y, modp, mods[i], norm_g[i, 0], w, None, None, n_rope=0, n_bf16=n_main)
            s, p, lr = outs
            wg = jnp.zeros((2, LANES, C_HEADS * C_DK), F32)
            wg = wg.at[0, :C_RANK].set(gla_w_gate[j, 0]).at[1, C_RANK:2 * C_RANK].set(gla_w_gate[j, 1])
            o = _gla(p, lr, wg, gla_b_gate[j])
            s, h2, logits = _out_project((o, p), gla_w_out[j].astype(BF16), s, mods[i], norm_g[i, 1],
                                         rw1, rw2, gla=True, gn=gla_norm_g[j])
        y = _moe(h2.reshape(b * n, ROW_SEGS, LANES), logits.reshape(b * n, LANES), router_bias,
                 i, w1_b, w3_b, w2_b)
        y = y.reshape(b * n * ROW_SEGS, LANES)
    return _final(s, y, mods[DEPTH - 1], final_g)
```

```python
import functools
import math

import jax
import jax.numpy as jnp
import numpy as np
from jax import lax
from jax.experimental import pallas as pl
from jax.experimental.pallas import tpu as pltpu
from jax.experimental.pallas import tpu_sc as plsc

F32 = jnp.float32
BF16 = jnp.bfloat16

D_MODEL = 1024
DEPTH = 4
GRID_W = 64
N_CTX = 256
A_HEADS = 4
A_DH = 64
A_DV = 128
B_HEADS = 8
B_DH = 64
NA_ROWS = 8
NA_COLS = 16
C_HEADS = 4
C_DK = 128
C_DV = 256
C_RANK = 16
C_TAU = 16.0
C_CHUNK = 64
GLA_BLOCK = 256
N_EXPERTS = 16
N_GROUPS = 4
PER_GROUP = 4
D_EXPERT = 512
ROPE_BASE = 10000.0
EPS = 1e-6
LOG2_E = math.log2(math.e)
ATT_KEY_BLOCK = 768

LANES = 128
ROW_TILE = 256
MOE_TILE = 256
SC_CORES = 2
SC_SUBCORES = 16
SC_WORKERS = SC_CORES * SC_SUBCORES
SC_GATHER_ROWS = 64
N_CLASSES = N_GROUPS * 6
NA_QROWS = 8
NA_SUB = 2
NA_KROWS = NA_SUB + NA_ROWS - 1
VMEM_LIMIT = 52 * 1024 * 1024
ROW_SEGS = D_MODEL // LANES // 2

PAIR_LO = (0, 0, 0, 1, 1, 2)
PAIR_HI = (1, 2, 3, 2, 3, 3)


def _cparams(sem):
    return pltpu.CompilerParams(dimension_semantics=sem, vmem_limit_bytes=VMEM_LIMIT)


def _sigmoid(x):
    return 1.0 / (1.0 + jnp.exp(-x))


def _silu(x):
    return x * _sigmoid(x)


def _rms(x, g):
    return x * lax.rsqrt(jnp.mean(x * x, axis=-1, keepdims=True) + EPS) * g


def _dot(a, b):
    return jnp.dot(a, b, preferred_element_type=F32)


def _dot_nt(a, b):
    return lax.dot_general(a, b, (((1,), (1,)), ((), ())), preferred_element_type=F32)


def _dot_tn(a, b):
    return lax.dot_general(a, b, (((0,), (0,)), ((), ())), preferred_element_type=F32)


def _mod_kernel(c_ref, w_ref, b_ref, o_ref):
    a = _silu(c_ref[...]).astype(BF16)
    o_ref[0] = _dot(a, w_ref[0].astype(BF16)) + b_ref[0]


def _mod_vectors(cc, w_mod, b_mod):
    depth, d, n6 = w_mod.shape
    rows = cc.shape[0]
    tn = 1536
    return pl.pallas_call(
        _mod_kernel,
        out_shape=jax.ShapeDtypeStruct((depth, rows, n6), F32),
        grid=(depth, n6 // tn),
        in_specs=[
            pl.BlockSpec((rows, d), lambda i, j: (0, 0)),
            pl.BlockSpec((1, d, tn), lambda i, j: (i, 0, j)),
            pl.BlockSpec((1, 1, tn), lambda i, j: (i, 0, j)),
        ],
        out_specs=pl.BlockSpec((1, rows, tn), lambda i, j: (i, 0, j)),
        compiler_params=_cparams(("arbitrary", "arbitrary")),
        name="mod_vectors",
    )(cc, w_mod, b_mod.reshape(depth, 1, n6))


def _token_rows(ref):
    rows = ref.shape[0] // ROW_SEGS
    packed = jnp.concatenate([ref[pl.ds(sg, rows, stride=ROW_SEGS), :] for sg in range(ROW_SEGS)], axis=-1)
    low = lax.bitcast_convert_type(lax.shift_left(packed, jnp.int32(16)), F32)
    high = lax.bitcast_convert_type(packed & jnp.int32(-65536), F32)
    return jnp.concatenate([low, high], axis=-1)


def _store_token_rows(ref, val):
    rows, d = val.shape
    rounded = val.astype(BF16).astype(F32)
    bits = lax.bitcast_convert_type(rounded, jnp.int32)
    packed = lax.shift_right_logical(bits[:, :d // 2], jnp.int32(16)) | bits[:, d // 2:]
    for sg in range(ROW_SEGS):
        ref[pl.ds(sg, rows, stride=ROW_SEGS), :] = packed[:, sg * LANES:(sg + 1) * LANES]


def _proj_kernel(*refs, has_y, n_rope, n_bf16, col_chunk):
    it = iter(refs)
    s_ref = next(it)
    y_ref = next(it) if has_y else None
    modp_ref = next(it) if has_y else None
    mod_ref = next(it)
    ng_ref = next(it)
    w_ref = next(it)
    cos_ref = next(it) if n_rope else None
    sin_ref = next(it) if n_rope else None
    snew_ref = next(it) if has_y else None
    p_ref = next(it)
    lr_ref = next(it) if w_ref.shape[1] > n_bf16 else None

    x = s_ref[0]
    if has_y:
        x = x + modp_ref[0, 0, 5:6, :] * _token_rows(y_ref)
        snew_ref[0] = x
    h = _rms(x, ng_ref[...])
    h = h * (1.0 + mod_ref[0, 0, 1:2, :]) + mod_ref[0, 0, 0:1, :]
    hb = h.astype(BF16)
    n_out = w_ref.shape[1]
    for c0 in range(0, n_out, col_chunk):
        c1 = min(c0 + col_chunk, n_out)
        acc = _dot(hb, w_ref[:, c0:c1])
        for b0 in range(c0, c1, LANES):
            t = acc[:, b0 - c0:b0 - c0 + LANES]
            if b0 < n_rope:
                t = t * cos_ref[...] + pltpu.roll(t, LANES // 2, 1) * sin_ref[...]
            if b0 < n_bf16:
                p_ref[0, :, b0:b0 + LANES] = t.astype(BF16)
            else:
                lr_ref[0, :, b0 - n_bf16:b0 - n_bf16 + LANES] = t


def _project(s, y, modp, mod, ng, w, cos, sin, *, n_rope, n_bf16):
    b, n, d = s.shape
    n_out = w.shape[1]
    tm = ROW_TILE
    has_y = y is not None
    row_spec = pl.BlockSpec((1, tm, d), lambda i, j: (i, j, 0))
    mod_spec = pl.BlockSpec((1, 1, 6, d), lambda i, j: (i, jnp.minimum(j, 1), 0, 0))
    in_specs = [row_spec]
    args = [s]
    if has_y:
        in_specs += [pl.BlockSpec((tm * ROW_SEGS, LANES), lambda i, j: (i * (n // tm) + j, 0)), mod_spec]
        args += [y, modp]
    in_specs += [mod_spec, pl.BlockSpec((1, d), lambda i, j: (0, 0)),
                 pl.BlockSpec((d, n_out), lambda i, j: (0, 0))]
    args += [mod, ng.reshape(1, d), w]
    if n_rope:
        tab_spec = pl.BlockSpec((tm, LANES), lambda i, j: (j, 0))
        in_specs += [tab_spec, tab_spec]
        args += [cos, sin]
    out_shape, out_specs = [], []
    if has_y:
        out_shape.append(jax.ShapeDtypeStruct((b, n, d), F32))
        out_specs.append(row_spec)
    out_shape.append(jax.ShapeDtypeStruct((b, n, n_bf16), BF16))
    out_specs.append(pl.BlockSpec((1, tm, n_bf16), lambda i, j: (i, j, 0)))
    if n_out > n_bf16:
        out_shape.append(jax.ShapeDtypeStruct((b, n, n_out - n_bf16), F32))
        out_specs.append(pl.BlockSpec((1, tm, n_out - n_bf16), lambda i, j: (i, j, 0)))
    return pl.pallas_call(
        functools.partial(_proj_kernel, has_y=has_y, n_rope=n_rope, n_bf16=n_bf16, col_chunk=512),
        out_shape=out_shape,
        grid=(b, n // tm),
        in_specs=in_specs,
        out_specs=out_specs,
        compiler_params=_cparams(("arbitrary", "arbitrary")),
        name="norm_mod_project",
    )(*args)


def _softmax_rows(s):
    e = jnp.exp(s - jnp.max(s, axis=-1, keepdims=True))
    return e * (1.0 / jnp.sum(e, axis=-1, keepdims=True))


def _diff_attn_kernel(q_ref, k_ref, v_ref, lam_ref, g_ref, o_ref, *, lam_init):
    lp = lam_ref[...]
    lam = (jnp.exp(jnp.sum(lp[0:1] * lp[1:2], axis=-1, keepdims=True))
           - jnp.exp(jnp.sum(lp[2:3] * lp[3:4], axis=-1, keepdims=True)) + lam_init)
    q = q_ref[0]
    lane = lax.broadcasted_iota(jnp.int32, (1, LANES), 1)
    first_map = (lane // 32) % 2 == 0
    zero = jnp.zeros_like(q)
    q0 = jnp.where(first_map, q, zero)
    q1 = jnp.where(first_map, zero, q)

    def attend(n_keys):
        blk = min(ATT_KEY_BLOCK, n_keys)
        ones = jnp.ones((blk, A_DV), BF16)
        state = [None, None]
        for kb in range(n_keys // blk):
            k = k_ref[0, kb * blk:(kb + 1) * blk, :]
            v1 = jnp.concatenate([v_ref[0, kb * blk:(kb + 1) * blk, :], ones], axis=1)
            for i, qm in enumerate((q0, q1)):
                s = _dot_nt(qm, k)
                m_new = jnp.max(s, axis=-1, keepdims=True)
                if kb:
                    m_old, acc_old = state[i]
                    m_new = jnp.maximum(m_old, m_new)
                acc = _dot(jnp.exp2(s - m_new).astype(BF16), v1)
                if kb:
                    acc = acc_old * jnp.exp2(m_old - m_new) + acc
                state[i] = (m_new, acc)
        acc0, acc1 = state[0][1], state[1][1]
        o = acc0[:, :A_DV] * (1.0 / acc0[:, A_DV:]) - acc1[:, :A_DV] * (lam / acc1[:, A_DV:])
        o_ref[0] = (_rms(o, g_ref[...]) * (1.0 - lam_init)).astype(BF16)

    qi = pl.program_id(2)

    @pl.when(qi == 0)
    def _():
        attend(N_CTX)

    @pl.when(qi > 0)
    def _():
        attend(k_ref.shape[1])


def _diff_attention(p, lam_p, subln_g, lam_init):
    b, n, _ = p.shape
    tq = ROW_TILE
    return pl.pallas_call(
        functools.partial(_diff_attn_kernel, lam_init=lam_init),
        out_shape=jax.ShapeDtypeStruct((b, n, A_HEADS * A_DV), BF16),
        grid=(b, A_HEADS, n // tq),
        in_specs=[
            pl.BlockSpec((1, tq, LANES), lambda i, h, j: (i, j, h)),
            pl.BlockSpec((1, n, LANES), lambda i, h, j: (i, 0, A_HEADS + h)),
            pl.BlockSpec((1, n, LANES), lambda i, h, j: (i, 0, 2 * A_HEADS + h)),
            pl.BlockSpec((4, A_DH), lambda i, h, j: (0, 0)),
            pl.BlockSpec((1, A_DV), lambda i, h, j: (0, 0)),
        ],
        out_specs=pl.BlockSpec((1, tq, LANES), lambda i, h, j: (i, j, h)),
        compiler_params=_cparams(("arbitrary", "arbitrary", "arbitrary")),
        name="diff_attention",
    )(p, p, p, lam_p, subln_g.reshape(1, A_DV))


def _split_heads(q):
    lane = lax.broadcasted_iota(jnp.int32, (1, LANES), 1)
    first_head = lane < B_DH
    zero = jnp.zeros_like(q)
    return first_head, (jnp.where(first_head, q, zero), jnp.where(first_head, zero, q))


def _na_ctx_kernel(q_ref, k_ref, v_ref, o_ref):
    first_head, qh = _split_heads(q_ref[0])
    kc = k_ref[0]
    vc = v_ref[0]
    outs = [_dot(_softmax_rows(_dot_nt(qh[h], kc)).astype(BF16), vc) for h in range(2)]
    o_ref[0] = jnp.where(first_head, outs[0], outs[1]).astype(BF16)


def _na_kernel(qa_ref, qb_ref, k_ref, v_ref, bm_ref, o_ref):
    blk = pl.program_id(2)
    rows = (k_ref.shape[1] - N_CTX) // GRID_W
    sub_q = NA_SUB * GRID_W
    sub_k = NA_KROWS * GRID_W
    kc = k_ref[0, 0:N_CTX, :]
    vc = jnp.concatenate([v_ref[0, 0:N_CTX, :], jnp.ones((N_CTX, LANES), BF16)], axis=1)
    ones = jnp.ones((sub_k, LANES), BF16)
    for sub in range(NA_QROWS // NA_SUB):
        q_ref = qa_ref if sub * sub_q < qa_ref.shape[1] else qb_ref
        q0 = (sub * sub_q) % qa_ref.shape[1]
        first_head, qh = _split_heads(q_ref[0, q0:q0 + sub_q, :])
        k_row0 = jnp.clip(blk * NA_QROWS + sub * NA_SUB - NA_ROWS // 2, 0, rows - NA_KROWS)
        start = pl.multiple_of(N_CTX + k_row0 * GRID_W, GRID_W)
        kw = k_ref[0, pl.ds(start, sub_k), :]
        vw = jnp.concatenate([v_ref[0, pl.ds(start, sub_k), :], ones], axis=1)
        outs = []
        for h in range(2):
            s_loc = _dot_nt(qh[h], kw) + bm_ref[h, 0, sub]
            s_ctx = _dot_nt(qh[h], kc)
            m = jnp.maximum(jnp.max(s_loc, axis=-1, keepdims=True), jnp.max(s_ctx, axis=-1, keepdims=True))
            acc = _dot(jnp.exp(s_loc - m).astype(BF16), vw) + _dot(jnp.exp(s_ctx - m).astype(BF16), vc)
            outs.append(acc[:, :LANES] * (1.0 / acc[:, LANES:]))
        o_ref[0, sub * sub_q:(sub + 1) * sub_q, :] = jnp.where(first_head, outs[0], outs[1]).astype(BF16)


def _na_bias_table(na_bias, rows):
    h = na_bias.shape[0]
    n_dr, n_dc = 2 * NA_ROWS - 1, 2 * NA_COLS - 1
    width = 2 * GRID_W
    left = GRID_W - NA_COLS
    u = jnp.pad(na_bias, ((0, 0), (0, 0), (left, width - left - n_dc)))
    skew = jnp.tile(u, (1, 1, GRID_W))[:, :, :GRID_W * (width - 1)].reshape(h, n_dr, GRID_W, width - 1)
    toeplitz = skew[:, :, :, GRID_W - 1:]
    margin = NA_KROWS - NA_ROWS
    by_col = jnp.pad(toeplitz.transpose(0, 2, 1, 3), ((0, 0), (0, 0), (margin, margin), (0, 0)))
    by_col = by_col.reshape(h, GRID_W, (n_dr + 2 * margin) * GRID_W)
    col = np.arange(GRID_W)
    col_start = np.clip(col - NA_COLS // 2, 0, GRID_W - NA_COLS)
    col_ok = (col[None, :] >= col_start[:, None]) & (col[None, :] < col_start[:, None] + NA_COLS)
    n_sub = NA_QROWS // NA_SUB
    blocks = []
    for q_row0 in (0, NA_QROWS, rows - NA_QROWS):
        for sub in range(n_sub):
            k_row0 = int(np.clip(q_row0 + sub * NA_SUB - NA_ROWS // 2, 0, rows - NA_KROWS))
            kr = k_row0 + np.arange(NA_KROWS)
            for rq in range(NA_SUB):
                r = q_row0 + sub * NA_SUB + rq
                r0 = int(np.clip(r - NA_ROWS // 2, 0, rows - NA_ROWS))
                row_ok = (kr >= r0) & (kr < r0 + NA_ROWS)
                ok = (col_ok[:, None, :] & row_ok[None, :, None]).reshape(GRID_W, NA_KROWS * GRID_W)
                first = k_row0 - r + NA_ROWS - 1 + margin
                window = by_col[:, :, first * GRID_W:(first + NA_KROWS) * GRID_W]
                blocks.append(jnp.where(ok[None], window, -jnp.inf))
    table = jnp.stack(blocks, axis=1)
    return table.reshape(h, 3, n_sub, NA_SUB * GRID_W, NA_KROWS * GRID_W)


def _neighbourhood_attention(p, bm):
    b, n, _ = p.shape
    tq = NA_QROWS * GRID_W
    half = tq // 2
    n_blocks = (n - N_CTX) // tq
    ctx_blocks = N_CTX // half
    col0 = 3 * A_HEADS
    pairs = B_HEADS // 2
    width = B_HEADS * B_DH

    def pattern(j):
        return jnp.minimum(j, 1) + (j == n_blocks - 1).astype(jnp.int32)

    ob_x = pl.pallas_call(
        _na_kernel,
        out_shape=jax.ShapeDtypeStruct((b, n - N_CTX, width), BF16),
        grid=(b, pairs, n_blocks),
        in_specs=[
            pl.BlockSpec((1, half, LANES), lambda i, h, j: (i, ctx_blocks + 2 * j, col0 + h)),
            pl.BlockSpec((1, half, LANES), lambda i, h, j: (i, ctx_blocks + 2 * j + 1, col0 + h)),
            pl.BlockSpec((1, n, LANES), lambda i, h, j: (i, 0, col0 + pairs + h)),
            pl.BlockSpec((1, n, LANES), lambda i, h, j: (i, 0, col0 + 2 * pairs + h)),
            pl.BlockSpec((2, 1, NA_QROWS // NA_SUB, NA_SUB * GRID_W, NA_KROWS * GRID_W),
                         lambda i, h, j: (h, pattern(j), 0, 0, 0)),
        ],
        out_specs=pl.BlockSpec((1, tq, LANES), lambda i, h, j: (i, j, h)),
        compiler_params=_cparams(("arbitrary", "arbitrary", "arbitrary")),
        name="neighbourhood_attention",
    )(p, p, p, p, bm)
    ob_c = pl.pallas_call(
        _na_ctx_kernel,
        out_shape=jax.ShapeDtypeStruct((b, N_CTX, width), BF16),
        grid=(b, pairs),
        in_specs=[
            pl.BlockSpec((1, N_CTX, LANES), lambda i, h: (i, 0, col0 + h)),
            pl.BlockSpec((1, N_CTX, LANES), lambda i, h: (i, 0, col0 + pairs + h)),
            pl.BlockSpec((1, N_CTX, LANES), lambda i, h: (i, 0, col0 + 2 * pairs + h)),
        ],
        out_specs=pl.BlockSpec((1, N_CTX, LANES), lambda i, h: (i, 0, h)),
        compiler_params=_cparams(("arbitrary", "arbitrary")),
        name="context_attention",
    )(p, p, p)
    return ob_x, ob_c


def _gla_kernel(q_ref, k_ref, v_ref, lr_ref, wg_ref, bg_ref, o_ref, ob_ref, g_ref, sf_ref, sb_ref):
    n = q_ref.shape[1]
    blk = GLA_BLOCK
    per_blk = blk // C_CHUNK
    n_blocks = n // blk
    ri = lax.broadcasted_iota(jnp.int32, (blk, blk), 0)
    ci = lax.broadcasted_iota(jnp.int32, (blk, blk), 1)
    same_chunk = (ri // C_CHUNK) == (ci // C_CHUNK)
    keeps = (same_chunk & (ci <= ri), same_chunk & (ci >= ri))
    tris = tuple(jnp.where(kp, 1.0, 0.0).astype(BF16) for kp in keeps)
    w_gate = jnp.concatenate([wg_ref[0], wg_ref[1]], axis=1).astype(BF16)
    b_gate = jnp.concatenate([bg_ref[0:1, :], bg_ref[1:2, :]], axis=1)
    for i in range(n_blocks):
        z = _dot(lr_ref[0, i * blk:(i + 1) * blk, :].astype(BF16), w_gate) + b_gate
        g_ref[i * blk:(i + 1) * blk, :] = (jnp.minimum(z, 0.0) - jnp.log(1.0 + jnp.exp(-jnp.abs(z)))) * (1.0 / C_TAU)

    def block(sb, direction, st_ref):
        keep = keeps[direction]
        end_row = C_CHUNK - 1 if direction == 0 else 0
        mid_row = C_CHUNK // 2 - 1 if direction == 0 else C_CHUNK // 2
        r0 = sb * blk
        g = g_ref[r0:r0 + blk, direction * C_DK:(direction + 1) * C_DK]
        g_hi = g.astype(BF16)
        g_lo = (g - g_hi.astype(F32)).astype(BF16)
        gc2 = _dot(tris[direction], jnp.concatenate([g_hi, g_lo], axis=1))
        gc = (gc2[:, :C_DK] + gc2[:, C_DK:]).reshape(per_blk, C_CHUNK, C_DK)
        g_end = gc[:, end_row:end_row + 1, :]
        g_mid = gc[:, mid_row:mid_row + 1, :]
        q = (q_ref[0, r0:r0 + blk, :].astype(F32) * (C_DK ** -0.5)).reshape(per_blk, C_CHUNK, C_DK)
        k = k_ref[0, r0:r0 + blk, :].astype(F32).reshape(per_blk, C_CHUNK, C_DK)
        v = v_ref[0, r0:r0 + blk, :]
        q_in = (q * jnp.exp(gc)).astype(BF16).reshape(blk, C_DK)
        q_mid = (q * jnp.exp(gc - g_mid)).astype(BF16).reshape(blk, C_DK)
        k_mid = (k * jnp.exp(g_mid - gc)).astype(BF16).reshape(blk, C_DK)
        k_end = (k * jnp.exp(g_end - gc)).astype(BF16).reshape(blk, C_DK)
        a = jnp.where(keep, _dot_nt(q_mid, k_mid), 0.0)
        o_intra = _dot(a.astype(BF16), v)
        ends = jnp.concatenate([g_end.reshape(per_blk, C_DK), jnp.zeros((8 - per_blk, C_DK), F32)], axis=0)
        decay = jnp.transpose(jnp.exp(ends))
        o_inter = [None] * per_blk
        for c in (range(per_blk) if direction == 0 else reversed(range(per_blk))):
            rows = slice(c * C_CHUNK, (c + 1) * C_CHUNK)
            st = st_ref[...]
            o_inter[c] = _dot(q_in[rows], st.astype(BF16))
            st_ref[...] = st * decay[:, c:c + 1] + _dot_tn(k_end[rows], v[rows])
        return o_intra + jnp.concatenate(o_inter, axis=0)

    sf_ref[...] = jnp.zeros_like(sf_ref)
    sb_ref[...] = jnp.zeros_like(sb_ref)
    ctx_blocks = N_CTX // blk
    order_b = list(reversed(range(ctx_blocks))) + list(reversed(range(ctx_blocks, n_blocks)))
    for i in range(n_blocks):
        o_ref[0, i * blk:(i + 1) * blk, :] = block(i, 0, sf_ref)
        sb = order_b[i]
        ob_ref[sb * blk:(sb + 1) * blk, :] = block(sb, 1, sb_ref)
    o_ref[0] = o_ref[0] + ob_ref[...]


def _gla(p, lr, wg, bg):
    b, n, _ = p.shape
    return pl.pallas_call(
        _gla_kernel,
        out_shape=jax.ShapeDtypeStruct((b, n, C_HEADS * C_DV), F32),
        grid=(b, C_HEADS),
        in_specs=[
            pl.BlockSpec((1, n, C_DK), lambda i, h: (i, 0, h)),
            pl.BlockSpec((1, n, C_DK), lambda i, h: (i, 0, C_HEADS + h)),
            pl.BlockSpec((1, n, C_DV), lambda i, h: (i, 0, C_HEADS + h)),
            pl.BlockSpec((1, n, LANES), lambda i, h: (i, 0, 0)),
            pl.BlockSpec((2, LANES, C_DK), lambda i, h: (0, 0, h)),
            pl.BlockSpec((2, C_DK), lambda i, h: (0, h)),
        ],
        out_specs=pl.BlockSpec((1, n, C_DV), lambda i, h: (i, 0, h)),
        scratch_shapes=[pltpu.VMEM((n, C_DV), F32), pltpu.VMEM((n, 2 * C_DK), F32),
                        pltpu.VMEM((C_DK, C_DV), F32), pltpu.VMEM((C_DK, C_DV), F32)],
        compiler_params=_cparams(("arbitrary", "arbitrary")),
        name="gla",
    )(p, p, p, lr, wg, bg)


def _out_kernel(*refs, gla):
    if gla:
        o_ref, gate_ref, gn_ref, w_ref, s_ref, mod_ref, ng_ref, rw_ref, snew_ref, h2_ref, lg_ref = refs
        o = o_ref[0]
        gate = gate_ref[0].astype(F32)
        parts = []
        for hd in range(C_HEADS):
            oh = _rms(o[:, hd * C_DV:(hd + 1) * C_DV], gn_ref[...])
            parts.append((oh * _silu(gate[:, hd * C_DV:(hd + 1) * C_DV])).astype(BF16))
        acc = _dot(jnp.concatenate(parts, axis=-1), w_ref[...])
    else:
        oa_ref, obx_ref, obc_ref, w_ref, s_ref, mod_ref, ng_ref, rw_ref, snew_ref, h2_ref, lg_ref = refs
        half = oa_ref.shape[2]
        ob = jnp.where(pl.program_id(1) == 0, obc_ref[0], obx_ref[0])
        acc = _dot(oa_ref[0], w_ref[0:half, :]) + _dot(ob, w_ref[half:, :])
    x = s_ref[0] + mod_ref[0, 0, 2:3, :] * acc
    snew_ref[0] = x
    h2 = _rms(x, ng_ref[...]) * (1.0 + mod_ref[0, 0, 4:5, :]) + mod_ref[0, 0, 3:4, :]
    _store_token_rows(h2_ref, h2)
    t = _dot(h2.astype(BF16), rw_ref[...])
    lg_ref[0] = t + pltpu.roll(t, LANES - N_EXPERTS, 1)


def _out_project(mix, w_out, s, mod, ng, rw, *, gla, gn=None):
    b, n, d = s.shape
    tm = ROW_TILE
    row_spec = pl.BlockSpec((1, tm, d), lambda i, j: (i, j, 0))
    const2 = lambda i, j: (0, 0)
    if gla:
        o, p = mix
        in_specs = [row_spec, pl.BlockSpec((1, tm, d), lambda i, j: (i, j, 2)),
                    pl.BlockSpec((1, C_DV), const2)]
        args = [o, p, gn.reshape(1, C_DV)]
    else:
        oa, ob_x, ob_c = mix
        half = oa.shape[2]
        in_specs = [pl.BlockSpec((1, tm, half), lambda i, j: (i, j, 0)),
                    pl.BlockSpec((1, tm, half), lambda i, j: (i, jnp.maximum(j - 1, 0), 0)),
                    pl.BlockSpec((1, tm, half), lambda i, j: (i, 0, 0))]
        args = [oa, ob_x, ob_c]
    in_specs += [pl.BlockSpec((d, d), const2), row_spec,
                 pl.BlockSpec((1, 1, 6, d), lambda i, j: (i, jnp.minimum(j, 1), 0, 0)),
                 pl.BlockSpec((1, d), const2), pl.BlockSpec((d, LANES), const2)]
    args += [w_out, s, mod, ng.reshape(1, d), rw]
    return pl.pallas_call(
        functools.partial(_out_kernel, gla=gla),
        out_shape=[jax.ShapeDtypeStruct((b, n, d), F32), jax.ShapeDtypeStruct((b * n * ROW_SEGS, LANES), jnp.int32),
                   jax.ShapeDtypeStruct((b, n, LANES), F32)],
        grid=(b, n // tm),
        in_specs=in_specs,
        out_specs=[row_spec, pl.BlockSpec((tm * ROW_SEGS, LANES), lambda i, j: (i * (n // tm) + j, 0)),
                   pl.BlockSpec((1, tm, LANES), lambda i, j: (i, j, 0))],
        compiler_params=_cparams(("arbitrary", "arbitrary")),
        name="out_project",
    )(*args)


def _route_kernel(bias_ref, lg_ref, cls_ref, wlo_ref, whi_ref):
    score = [_sigmoid(lg_ref[e]) for e in range(N_EXPERTS)]
    sel = [score[e] + bias_ref[e] for e in range(N_EXPERTS)]
    grp_score = []
    for g in range(N_GROUPS):
        v = sel[g * PER_GROUP:(g + 1) * PER_GROUP]
        best = v[0] + v[1]
        for a in range(PER_GROUP):
            for c in range(a + 1, PER_GROUP):
                if (a, c) != (0, 1):
                    best = jnp.maximum(best, v[a] + v[c])
        grp_score.append(best)
    grp = jnp.zeros(grp_score[0].shape, jnp.int32)
    best = grp_score[0]
    for g in range(1, N_GROUPS):
        upd = grp_score[g] > best
        best = jnp.where(upd, grp_score[g], best)
        grp = jnp.where(upd, g, grp)

    def pick(vals, j):
        out = vals[j]
        for g in range(1, N_GROUPS):
            out = jnp.where(grp == g, vals[g * PER_GROUP + j], out)
        return out

    v = [pick(sel, j) for j in range(PER_GROUP)]
    sc = [pick(score, j) for j in range(PER_GROUP)]
    one = jnp.ones(grp.shape, jnp.int32)
    zero = jnp.zeros(grp.shape, jnp.int32)
    chosen = []
    for j in range(PER_GROUP):
        rank = zero
        for m in range(PER_GROUP):
            if m == j:
                continue
            ahead = (v[m] >= v[j]) if m < j else (v[m] > v[j])
            rank = rank + jnp.where(ahead, one, zero)
        chosen.append(rank < 2)
    code = zero
    for j in range(PER_GROUP):
        code = code + jnp.where(chosen[j], one * (1 << j), zero)
    pair = zero
    for idx in range(6):
        pair = jnp.where(code == (1 << PAIR_LO[idx]) + (1 << PAIR_HI[idx]), idx, pair)
    s_lo = jnp.where(chosen[0], sc[0], jnp.where(chosen[1], sc[1], sc[2]))
    s_hi = jnp.where(chosen[3], sc[3], jnp.where(chosen[2], sc[2], sc[1]))
    den = s_lo + s_hi
    cls_ref[...] = grp * 6 + pair
    wlo_ref[...] = s_lo / den
    whi_ref[...] = s_hi / den


def _route(logits_t, router_bias):
    _, r, _ = logits_t.shape
    full = pl.BlockSpec((r, LANES), lambda i: (0, 0))
    return pl.pallas_call(
        _route_kernel,
        out_shape=[jax.ShapeDtypeStruct((r, LANES), jnp.int32), jax.ShapeDtypeStruct((r, LANES), F32),
                   jax.ShapeDtypeStruct((r, LANES), F32)],
        grid=(1,),
        in_specs=[pl.BlockSpec(memory_space=pltpu.SMEM),
                  pl.BlockSpec((N_EXPERTS, r, LANES), lambda i: (0, 0, 0))],
        out_specs=[full, full, full],
        compiler_params=_cparams(("arbitrary",)),
        name="route",
    )(router_bias, logits_t)


def _gather_rows(table, idx):
    p = idx.shape[0]
    per_worker = p // SC_WORKERS
    n_chunks = per_worker // SC_GATHER_ROWS
    mesh = plsc.VectorSubcoreMesh(core_axis_name="c", subcore_axis_name="s", num_cores=SC_CORES,
                                  num_subcores=SC_SUBCORES)

    @functools.partial(
        pl.kernel, mesh=mesh,
        out_type=jax.ShapeDtypeStruct((p,) + table.shape[1:], table.dtype),
        scratch_types=[pltpu.VMEM((per_worker,), jnp.int32),
                       pltpu.VMEM((SC_GATHER_ROWS,) + table.shape[1:], table.dtype),
                       pltpu.SemaphoreType.DMA],
        compiler_params=pltpu.CompilerParams(use_tc_tiling_on_sc=True),
        name="gather_rows",
    )
    def gather(table_hbm, idx_hbm, out_hbm, idx_v, rows_v, sem):
        worker = lax.axis_index("s") * SC_CORES + lax.axis_index("c")
        base = worker * per_worker
        pltpu.sync_copy(idx_hbm.at[pl.ds(base, per_worker)], idx_v)

        @pl.loop(0, n_chunks)
        def _(i):
            off = pl.multiple_of(i * SC_GATHER_ROWS, SC_GATHER_ROWS)
            pltpu.async_copy(table_hbm.at[idx_v.at[pl.ds(off, SC_GATHER_ROWS)]], rows_v, sem).wait()
            pltpu.sync_copy(rows_v, out_hbm.at[pl.ds(base + off, SC_GATHER_ROWS)])

    return gather(table, idx)


def _moe_kernel(lo_ref, hi_ref, nt_ref, x_ref, wr_ref, w1l_ref, w1h_ref, w3l_ref, w3h_ref, w2l_ref, w2h_ref, y_ref):
    j = pl.program_id(0)

    @pl.when(j < nt_ref[0])
    def _():
        x = _token_rows(x_ref).astype(BF16)
        wr = wr_ref[...]
        he_lo = (_silu(_dot(x, w1l_ref[0, 0])) * _dot(x, w3l_ref[0, 0]) * wr[:, 0:1]).astype(BF16)
        he_hi = (_silu(_dot(x, w1h_ref[0, 0])) * _dot(x, w3h_ref[0, 0]) * wr[:, 1:2]).astype(BF16)
        y = _dot(he_lo, w2l_ref[0, 0]) + _dot(he_hi, w2h_ref[0, 0])
        _store_token_rows(y_ref, y)

    @pl.when(j >= nt_ref[0])
    def _():
        y_ref[...] = jnp.zeros_like(y_ref)


def _moe_experts(xs, wrow, tile_lo, tile_hi, n_tiles, layer, w1, w3, w2):
    tp = xs.shape[0] // ROW_SEGS
    tm = MOE_TILE
    d, de = w1.shape[2], w1.shape[3]

    def gate_map(j, lo, hi, nt):
        return (jnp.minimum(j, nt[0] - 1), 0)

    def lo_map(j, lo, hi, nt):
        return (layer, lo[jnp.minimum(j, nt[0] - 1)], 0, 0)

    def hi_map(j, lo, hi, nt):
        return (layer, hi[jnp.minimum(j, nt[0] - 1)], 0, 0)

    up = (1, 1, d, de)
    down = (1, 1, de, d)
    return pl.pallas_call(
        _moe_kernel,
        out_shape=jax.ShapeDtypeStruct((tp * ROW_SEGS, LANES), jnp.int32),
        grid_spec=pltpu.PrefetchScalarGridSpec(
            num_scalar_prefetch=3,
            grid=(tp // tm,),
            in_specs=[pl.BlockSpec((tm * ROW_SEGS, LANES), gate_map), pl.BlockSpec((tm, 2), gate_map),
                      pl.BlockSpec(up, lo_map), pl.BlockSpec(up, hi_map),
                      pl.BlockSpec(up, lo_map), pl.BlockSpec(up, hi_map),
                      pl.BlockSpec(down, lo_map), pl.BlockSpec(down, hi_map)],
            out_specs=pl.BlockSpec((tm * ROW_SEGS, LANES), lambda j, lo, hi, nt: (j, 0)),
        ),
        compiler_params=_cparams(("arbitrary",)),
        name="moe_experts",
    )(tile_lo, tile_hi, n_tiles, xs, wrow, w1, w1, w3, w3, w2, w2)


def _moe(h2, logits, router_bias, layer, w1, w3, w2):
    t = h2.shape[0]
    tm = MOE_TILE
    n_tiles_max = t // tm + N_CLASSES
    tp = n_tiles_max * tm
    lg_t = logits[:, :N_EXPERTS].T.reshape(N_EXPERTS, t // LANES, LANES)
    cls, wlo, whi = _route(lg_t, router_bias)
    cls, wlo, whi = cls.reshape(t), wlo.reshape(t), whi.reshape(t)
    classes = jnp.arange(N_CLASSES, dtype=jnp.int32)
    onehot = (cls[:, None] == classes[None, :]).astype(jnp.int32)
    csum = jnp.cumsum(onehot, axis=0)
    rank = jnp.sum(csum * onehot, axis=1) - 1
    counts = csum[-1]
    tiles_per = (counts + tm - 1) // tm
    tile_end = jnp.cumsum(tiles_per)
    tile_start = tile_end - tiles_per
    dest = jnp.sum(onehot * (tile_start * tm)[None, :], axis=1) + rank
    tiles = jnp.arange(n_tiles_max, dtype=jnp.int32)
    tile_cls = jnp.minimum(jnp.sum((tile_end[None, :] <= tiles[:, None]).astype(jnp.int32), axis=1), N_CLASSES - 1)
    pair = tile_cls % 6
    base = (tile_cls // 6) * PER_GROUP
    pair_onehot = (pair[:, None] == jnp.arange(6, dtype=jnp.int32)[None, :]).astype(jnp.int32)
    tile_lo = base + jnp.sum(pair_onehot * jnp.asarray(PAIR_LO, jnp.int32)[None, :], axis=1)
    tile_hi = base + jnp.sum(pair_onehot * jnp.asarray(PAIR_HI, jnp.int32)[None, :], axis=1)
    per_token = jnp.stack([jnp.arange(t, dtype=jnp.int32), lax.bitcast_convert_type(wlo, jnp.int32),
                           lax.bitcast_convert_type(whi, jnp.int32)], axis=1)
    padding = jnp.stack([jnp.arange(tp, dtype=jnp.int32) % t, jnp.zeros((tp,), jnp.int32),
                         jnp.zeros((tp,), jnp.int32)], axis=1)
    per_row = padding.at[dest].set(per_token, unique_indices=True)
    tok = per_row[:, 0]
    wrow = lax.bitcast_convert_type(per_row[:, 1:3], F32)
    xs = _gather_rows(h2, tok).reshape(tp * ROW_SEGS, LANES)
    ys = _moe_experts(xs, wrow, tile_lo, tile_hi, tile_end[-1:], layer, w1, w3, w2)
    return _gather_rows(ys.reshape(tp, ROW_SEGS, LANES), dest)


def _final_kernel(s_ref, y_ref, mod_ref, g_ref, o_ref):
    x = s_ref[0] + mod_ref[0, 0, 5:6, :] * _token_rows(y_ref)
    o_ref[0] = _rms(x, g_ref[...])


def _final(s, y, mod, final_g):
    b, n, d = s.shape
    tm = ROW_TILE
    skip = N_CTX // tm
    row_spec = pl.BlockSpec((1, tm, d), lambda i, j: (i, j + skip, 0))
    return pl.pallas_call(
        _final_kernel,
        out_shape=jax.ShapeDtypeStruct((b, n - N_CTX, d), F32),
        grid=(b, (n - N_CTX) // tm),
        in_specs=[row_spec, pl.BlockSpec((tm * ROW_SEGS, LANES), lambda i, j: (i * (n // tm) + j + skip, 0)),
                  pl.BlockSpec((1, 1, 6, d), lambda i, j: (i, 1, 0, 0)),
                  pl.BlockSpec((1, d), lambda i, j: (0, 0))],
        out_specs=pl.BlockSpec((1, tm, d), lambda i, j: (i, j, 0)),
        compiler_params=_cparams(("arbitrary", "arbitrary")),
        name="final_norm",
    )(s, y, mod, final_g.reshape(1, d))


def _rope_tables(seq):
    pos = jnp.arange(seq)
    row_pos, col_pos = pos // GRID_W, pos % GRID_W
    n = A_DH // 2
    inv = ROPE_BASE ** (-jnp.arange(0, n, 2, dtype=F32) / n)
    ang_r = row_pos.astype(F32)[:, None] * inv[None, :]
    ang_c = col_pos.astype(F32)[:, None] * inv[None, :]
    ang = jnp.concatenate([ang_r, ang_c], axis=-1)
    cos = jnp.tile(jnp.cos(ang), (1, 4))
    sin = jnp.tile(jnp.sin(ang), (1, 4))
    sin = jnp.concatenate([-sin[:, :LANES // 2], sin[:, LANES // 2:]], axis=-1)
    cos = jnp.concatenate([jnp.ones((N_CTX, LANES), F32), cos], axis=0)
    sin = jnp.concatenate([jnp.zeros((N_CTX, LANES), F32), sin], axis=0)
    return cos, sin


def _interleave_maps(w):
    d = w.shape[0]
    w = w.reshape(d, A_HEADS, 2, 2, 2, A_DH // 4)
    return w.transpose(0, 1, 4, 2, 3, 5).reshape(d, A_HEADS * LANES)


def _att_weights(w_in):
    a_qk = A_HEADS * 2 * A_DH
    a_v = A_HEADS * A_DV
    b_w = B_HEADS * B_DH
    qa = _interleave_maps(w_in[:, :a_qk]) * (A_DH ** -0.5 * LOG2_E)
    ka = _interleave_maps(w_in[:, a_qk:2 * a_qk])
    va = w_in[:, 2 * a_qk:2 * a_qk + a_v]
    o = 2 * a_qk + a_v
    qb = w_in[:, o:o + b_w] * (B_DH ** -0.5)
    rest = w_in[:, o + b_w:]
    return jnp.concatenate([qa, ka, va, qb, rest], axis=1).astype(BF16)


def _gla_weights(w_in):
    d = w_in.shape[0]
    n_main = 2 * C_HEADS * C_DK + 2 * C_HEADS * C_DV
    pad = jnp.zeros((d, LANES - 2 * C_RANK), w_in.dtype)
    return jnp.concatenate([w_in, pad], axis=1).astype(BF16), n_main


def kernel(x, c, ctx, c_ctx, w_mod, b_mod, norm_g, final_g, att_w_in, att_w_out, att_lambda, att_subln_g, na_bias,
           gla_w_in, gla_w_gate, gla_b_gate, gla_norm_g, gla_w_out, router_w, router_bias, moe_w1, moe_w3, moe_w2):
    b, seq, d = x.shape
    n = N_CTX + seq
    s = jnp.concatenate([ctx, x], axis=1)

    rows = b + 1
    rows_pad = -(-rows // 8) * 8
    cc = jnp.concatenate([c, c_ctx[None, :], jnp.zeros((rows_pad - rows, d), F32)], axis=0)
    mod_all = _mod_vectors(cc, w_mod, b_mod)
    mod_x = mod_all[:, :b].reshape(DEPTH, b, 1, 6, d)
    mod_c = jnp.broadcast_to(mod_all[:, b].reshape(DEPTH, 1, 1, 6, d), (DEPTH, b, 1, 6, d))
    mods = jnp.concatenate([mod_c, mod_x], axis=2)

    cos, sin = _rope_tables(seq)
    rw_hi = router_w.astype(BF16)
    rw_lo = (router_w - rw_hi.astype(F32)).astype(BF16)
    zpad = jnp.zeros((d, LANES - 2 * N_EXPERTS), BF16)
    rw = jnp.concatenate([rw_hi, rw_lo, zpad], axis=1)

    w1_b, w3_b, w2_b = moe_w1.astype(BF16), moe_w3.astype(BF16), moe_w2.astype(BF16)
    y = None
    for i in range(DEPTH):
        j = i // 2
        modp = mods[i - 1] if i else None
        if i % 2 == 0:
            lam_init = 0.8 - 0.6 * math.exp(-0.3 * i)
            w = _att_weights(att_w_in[j])
            outs = _project(s, y, modp, mods[i], norm_g[i, 0], w, cos, sin,
                            n_rope=2 * A_HEADS * LANES, n_bf16=w.shape[1])
            if i:
                s = outs[0]
            p = outs[-1]
            oa = _diff_attention(p, att_lambda[j], att_subln_g[j], lam_init)
            ob_x, ob_c = _neighbourhood_attention(p, _na_bias_table(na_bias[j], seq // GRID_W))
            s, h2, logits = _out_project((oa, ob_x, ob_c), att_w_out[j].astype(BF16), s, mods[i], norm_g[i, 1],
                                         rw, gla=False)
        else:
            w, n_main = _gla_weights(gla_w_in[j])
            outs = _project(s, y, modp, mods[i], norm_g[i, 0], w, None, None, n_rope=0, n_bf16=n_main)
            s, p, lr = outs
            wg = jnp.zeros((2, LANES, C_HEADS * C_DK), F32)
            wg = wg.at[0, :C_RANK].set(gla_w_gate[j, 0]).at[1, C_RANK:2 * C_RANK].set(gla_w_gate[j, 1])
            o = _gla(p, lr, wg, gla_b_gate[j])
            s, h2, logits = _out_project((o, p), gla_w_out[j].astype(BF16), s, mods[i], norm_g[i, 1],
                                         rw, gla=True, gn=gla_norm_g[j])
        y = _moe(h2.reshape(b * n, ROW_SEGS, LANES), logits.reshape(b * n, LANES), router_bias,
                 i, w1_b, w3_b, w2_b)
        y = y.reshape(b * n * ROW_SEGS, LANES)
    return _final(s, y, mods[DEPTH - 1], final_g)
```

```python
import functools
import math

import jax
import jax.numpy as jnp
import numpy as np
from jax import lax
from jax.experimental import pallas as pl
from jax.experimental.pallas import tpu as pltpu
from jax.experimental.pallas import tpu_sc as plsc

F32 = jnp.float32
BF16 = jnp.bfloat16

D_MODEL = 1024
DEPTH = 4
GRID_W = 64
N_CTX = 256
A_HEADS = 4
A_DH = 64
A_DV = 128
B_HEADS = 8
B_DH = 64
NA_ROWS = 8
NA_COLS = 16
C_HEADS = 4
C_DK = 128
C_DV = 256
C_RANK = 16
C_TAU = 16.0
C_CHUNK = 64
GLA_BLOCK = 256
N_EXPERTS = 16
N_GROUPS = 4
PER_GROUP = 4
D_EXPERT = 512
ROPE_BASE = 10000.0
EPS = 1e-6
LOG2_E = math.log2(math.e)
ATT_KEY_BLOCK = 768

LANES = 128
ROW_TILE = 256
MOE_TILE = 256
SC_CORES = 2
SC_SUBCORES = 16
SC_WORKERS = SC_CORES * SC_SUBCORES
SC_GATHER_ROWS = 64
N_CLASSES = N_GROUPS * 6
NA_QROWS = 8
NA_SUB = 2
NA_KROWS = NA_SUB + NA_ROWS - 1
VMEM_LIMIT = 52 * 1024 * 1024
ROW_SEGS = D_MODEL // LANES // 2

PAIR_LO = (0, 0, 0, 1, 1, 2)
PAIR_HI = (1, 2, 3, 2, 3, 3)


def _cparams(sem):
    return pltpu.CompilerParams(dimension_semantics=sem, vmem_limit_bytes=VMEM_LIMIT)


def _sigmoid(x):
    return 1.0 / (1.0 + jnp.exp(-x))


def _silu(x):
    return x * _sigmoid(x)


def _rms(x, g):
    return x * lax.rsqrt(jnp.mean(x * x, axis=-1, keepdims=True) + EPS) * g


def _dot(a, b):
    return jnp.dot(a, b, preferred_element_type=F32)


def _dot_nt(a, b):
    return lax.dot_general(a, b, (((1,), (1,)), ((), ())), preferred_element_type=F32)


def _dot_tn(a, b):
    return lax.dot_general(a, b, (((0,), (0,)), ((), ())), preferred_element_type=F32)


def _mod_kernel(c_ref, w_ref, b_ref, o_ref):
    a = _silu(c_ref[...]).astype(BF16)
    o_ref[0] = _dot(a, w_ref[0].astype(BF16)) + b_ref[0]


def _mod_vectors(cc, w_mod, b_mod):
    depth, d, n6 = w_mod.shape
    rows = cc.shape[0]
    tn = 1536
    return pl.pallas_call(
        _mod_kernel,
        out_shape=jax.ShapeDtypeStruct((depth, rows, n6), F32),
        grid=(depth, n6 // tn),
        in_specs=[
            pl.BlockSpec((rows, d), lambda i, j: (0, 0)),
            pl.BlockSpec((1, d, tn), lambda i, j: (i, 0, j)),
            pl.BlockSpec((1, 1, tn), lambda i, j: (i, 0, j)),
        ],
        out_specs=pl.BlockSpec((1, rows, tn), lambda i, j: (i, 0, j)),
        compiler_params=_cparams(("arbitrary", "arbitrary")),
        name="mod_vectors",
    )(cc, w_mod, b_mod.reshape(depth, 1, n6))


def _token_rows(ref):
    rows = ref.shape[0] // ROW_SEGS
    packed = jnp.concatenate([ref[pl.ds(sg, rows, stride=ROW_SEGS), :] for sg in range(ROW_SEGS)], axis=-1)
    low = lax.bitcast_convert_type(lax.shift_left(packed, jnp.int32(16)), F32)
    high = lax.bitcast_convert_type(packed & jnp.int32(-65536), F32)
    return jnp.concatenate([low, high], axis=-1)


def _store_token_rows(ref, val):
    rows, d = val.shape
    rounded = val.astype(BF16).astype(F32)
    bits = lax.bitcast_convert_type(rounded, jnp.int32)
    packed = lax.shift_right_logical(bits[:, :d // 2], jnp.int32(16)) | bits[:, d // 2:]
    for sg in range(ROW_SEGS):
        ref[pl.ds(sg, rows, stride=ROW_SEGS), :] = packed[:, sg * LANES:(sg + 1) * LANES]


def _proj_kernel(*refs, has_y, n_rope, n_bf16, col_chunk):
    it = iter(refs)
    s_ref = next(it)
    y_ref = next(it) if has_y else None
    modp_ref = next(it) if has_y else None
    mod_ref = next(it)
    ng_ref = next(it)
    w_ref = next(it)
    cos_ref = next(it) if n_rope else None
    sin_ref = next(it) if n_rope else None
    snew_ref = next(it) if has_y else None
    p_ref = next(it)
    lr_ref = next(it) if w_ref.shape[1] > n_bf16 else None

    x = s_ref[0]
    if has_y:
        x = x + modp_ref[0, 0, 5:6, :] * _token_rows(y_ref)
        snew_ref[0] = x
    h = _rms(x, ng_ref[...])
    h = h * (1.0 + mod_ref[0, 0, 1:2, :]) + mod_ref[0, 0, 0:1, :]
    hb = h.astype(BF16)
    n_out = w_ref.shape[1]
    for c0 in range(0, n_out, col_chunk):
        c1 = min(c0 + col_chunk, n_out)
        acc = _dot(hb, w_ref[:, c0:c1])
        for b0 in range(c0, c1, LANES):
            t = acc[:, b0 - c0:b0 - c0 + LANES]
            if b0 < n_rope:
                t = t * cos_ref[...] + pltpu.roll(t, LANES // 2, 1) * sin_ref[...]
            if b0 < n_bf16:
                p_ref[0, :, b0:b0 + LANES] = t.astype(BF16)
            else:
                lr_ref[0, :, b0 - n_bf16:b0 - n_bf16 + LANES] = t


def _project(s, y, modp, mod, ng, w, cos, sin, *, n_rope, n_bf16):
    b, n, d = s.shape
    n_out = w.shape[1]
    tm = ROW_TILE
    has_y = y is not None
    row_spec = pl.BlockSpec((1, tm, d), lambda i, j: (i, j, 0))
    mod_spec = pl.BlockSpec((1, 1, 6, d), lambda i, j: (i, jnp.minimum(j, 1), 0, 0))
    in_specs = [row_spec]
    args = [s]
    if has_y:
        in_specs += [pl.BlockSpec((tm * ROW_SEGS, LANES), lambda i, j: (i * (n // tm) + j, 0)), mod_spec]
        args += [y, modp]
    in_specs += [mod_spec, pl.BlockSpec((1, d), lambda i, j: (0, 0)),
                 pl.BlockSpec((d, n_out), lambda i, j: (0, 0))]
    args += [mod, ng.reshape(1, d), w]
    if n_rope:
        tab_spec = pl.BlockSpec((tm, LANES), lambda i, j: (j, 0))
        in_specs += [tab_spec, tab_spec]
        args += [cos, sin]
    out_shape, out_specs = [], []
    if has_y:
        out_shape.append(jax.ShapeDtypeStruct((b, n, d), F32))
        out_specs.append(row_spec)
    out_shape.append(jax.ShapeDtypeStruct((b, n, n_bf16), BF16))
    out_specs.append(pl.BlockSpec((1, tm, n_bf16), lambda i, j: (i, j, 0)))
    if n_out > n_bf16:
        out_shape.append(jax.ShapeDtypeStruct((b, n, n_out - n_bf16), F32))
        out_specs.append(pl.BlockSpec((1, tm, n_out - n_bf16), lambda i, j: (i, j, 0)))
    return pl.pallas_call(
        functools.partial(_proj_kernel, has_y=has_y, n_rope=n_rope, n_bf16=n_bf16, col_chunk=512),
        out_shape=out_shape,
        grid=(b, n // tm),
        in_specs=in_specs,
        out_specs=out_specs,
        compiler_params=_cparams(("arbitrary", "arbitrary")),
        name="norm_mod_project",
    )(*args)


def _softmax_rows(s):
    e = jnp.exp(s - jnp.max(s, axis=-1, keepdims=True))
    return e * (1.0 / jnp.sum(e, axis=-1, keepdims=True))


def _diff_attn_kernel(q_ref, k_ref, v_ref, lam_ref, g_ref, o_ref, *, lam_init):
    lp = lam_ref[...]
    lam = (jnp.exp(jnp.sum(lp[0:1] * lp[1:2], axis=-1, keepdims=True))
           - jnp.exp(jnp.sum(lp[2:3] * lp[3:4], axis=-1, keepdims=True)) + lam_init)
    q = q_ref[0]
    lane = lax.broadcasted_iota(jnp.int32, (1, LANES), 1)
    first_map = (lane // 32) % 2 == 0
    zero = jnp.zeros_like(q)
    q0 = jnp.where(first_map, q, zero)
    q1 = jnp.where(first_map, zero, q)

    def attend(n_keys):
        blk = min(ATT_KEY_BLOCK, n_keys)
        ones = jnp.ones((blk, A_DV), BF16)
        state = [None, None]
        for kb in range(n_keys // blk):
            k = k_ref[0, kb * blk:(kb + 1) * blk, :]
            v1 = jnp.concatenate([v_ref[0, kb * blk:(kb + 1) * blk, :], ones], axis=1)
            for i, qm in enumerate((q0, q1)):
                s = _dot_nt(qm, k)
                m_new = jnp.max(s, axis=-1, keepdims=True)
                if kb:
                    m_old, acc_old = state[i]
                    m_new = jnp.maximum(m_old, m_new)
                acc = _dot(jnp.exp2(s - m_new).astype(BF16), v1)
                if kb:
                    acc = acc_old * jnp.exp2(m_old - m_new) + acc
                state[i] = (m_new, acc)
        acc0, acc1 = state[0][1], state[1][1]
        o = acc0[:, :A_DV] * (1.0 / acc0[:, A_DV:]) - acc1[:, :A_DV] * (lam / acc1[:, A_DV:])
        o_ref[0] = (_rms(o, g_ref[...]) * (1.0 - lam_init)).astype(BF16)

    qi = pl.program_id(2)

    @pl.when(qi == 0)
    def _():
        attend(N_CTX)

    @pl.when(qi > 0)
    def _():
        attend(k_ref.shape[1])


def _diff_attention(p, lam_p, subln_g, lam_init):
    b, n, _ = p.shape
    tq = ROW_TILE
    return pl.pallas_call(
        functools.partial(_diff_attn_kernel, lam_init=lam_init),
        out_shape=jax.ShapeDtypeStruct((b, n, A_HEADS * A_DV), BF16),
        grid=(b, A_HEADS, n // tq),
        in_specs=[
            pl.BlockSpec((1, tq, LANES), lambda i, h, j: (i, j, h)),
            pl.BlockSpec((1, n, LANES), lambda i, h, j: (i, 0, A_HEADS + h)),
            pl.BlockSpec((1, n, LANES), lambda i, h, j: (i, 0, 2 * A_HEADS + h)),
            pl.BlockSpec((4, A_DH), lambda i, h, j: (0, 0)),
            pl.BlockSpec((1, A_DV), lambda i, h, j: (0, 0)),
        ],
        out_specs=pl.BlockSpec((1, tq, LANES), lambda i, h, j: (i, j, h)),
        compiler_params=_cparams(("arbitrary", "arbitrary", "arbitrary")),
        name="diff_attention",
    )(p, p, p, lam_p, subln_g.reshape(1, A_DV))


def _split_heads(q):
    lane = lax.broadcasted_iota(jnp.int32, (1, LANES), 1)
    first_head = lane < B_DH
    zero = jnp.zeros_like(q)
    return first_head, (jnp.where(first_head, q, zero), jnp.where(first_head, zero, q))


def _na_ctx_kernel(q_ref, k_ref, v_ref, o_ref):
    first_head, qh = _split_heads(q_ref[0])
    kc = k_ref[0]
    vc = v_ref[0]
    outs = [_dot(_softmax_rows(_dot_nt(qh[h], kc)).astype(BF16), vc) for h in range(2)]
    o_ref[0] = jnp.where(first_head, outs[0], outs[1]).astype(BF16)


def _na_kernel(qa_ref, qb_ref, k_ref, v_ref, bm_ref, o_ref):
    blk = pl.program_id(2)
    rows = (k_ref.shape[1] - N_CTX) // GRID_W
    sub_q = NA_SUB * GRID_W
    sub_k = NA_KROWS * GRID_W
    kc = k_ref[0, 0:N_CTX, :]
    vc = jnp.concatenate([v_ref[0, 0:N_CTX, :], jnp.ones((N_CTX, LANES), BF16)], axis=1)
    ones = jnp.ones((sub_k, LANES), BF16)
    for sub in range(NA_QROWS // NA_SUB):
        q_ref = qa_ref if sub * sub_q < qa_ref.shape[1] else qb_ref
        q0 = (sub * sub_q) % qa_ref.shape[1]
        first_head, qh = _split_heads(q_ref[0, q0:q0 + sub_q, :])
        k_row0 = jnp.clip(blk * NA_QROWS + sub * NA_SUB - NA_ROWS // 2, 0, rows - NA_KROWS)
        start = pl.multiple_of(N_CTX + k_row0 * GRID_W, GRID_W)
        kw = k_ref[0, pl.ds(start, sub_k), :]
        vw = jnp.concatenate([v_ref[0, pl.ds(start, sub_k), :], ones], axis=1)
        outs = []
        for h in range(2):
            s_loc = _dot_nt(qh[h], kw) + bm_ref[h, 0, sub]
            s_ctx = _dot_nt(qh[h], kc)
            m = jnp.maximum(jnp.max(s_loc, axis=-1, keepdims=True), jnp.max(s_ctx, axis=-1, keepdims=True))
            acc = _dot(jnp.exp(s_loc - m).astype(BF16), vw) + _dot(jnp.exp(s_ctx - m).astype(BF16), vc)
            outs.append(acc[:, :LANES] * (1.0 / acc[:, LANES:]))
        o_ref[0, sub * sub_q:(sub + 1) * sub_q, :] = jnp.where(first_head, outs[0], outs[1]).astype(BF16)


def _na_bias_table(na_bias, rows):
    h = na_bias.shape[0]
    n_dr, n_dc = 2 * NA_ROWS - 1, 2 * NA_COLS - 1
    width = 2 * GRID_W
    left = GRID_W - NA_COLS
    u = jnp.pad(na_bias, ((0, 0), (0, 0), (left, width - left - n_dc)))
    skew = jnp.tile(u, (1, 1, GRID_W))[:, :, :GRID_W * (width - 1)].reshape(h, n_dr, GRID_W, width - 1)
    toeplitz = skew[:, :, :, GRID_W - 1:]
    margin = NA_KROWS - NA_ROWS
    by_col = jnp.pad(toeplitz.transpose(0, 2, 1, 3), ((0, 0), (0, 0), (margin, margin), (0, 0)))
    by_col = by_col.reshape(h, GRID_W, (n_dr + 2 * margin) * GRID_W)
    col = np.arange(GRID_W)
    col_start = np.clip(col - NA_COLS // 2, 0, GRID_W - NA_COLS)
    col_ok = (col[None, :] >= col_start[:, None]) & (col[None, :] < col_start[:, None] + NA_COLS)
    n_sub = NA_QROWS // NA_SUB
    blocks = []
    for q_row0 in (0, NA_QROWS, rows - NA_QROWS):
        for sub in range(n_sub):
            k_row0 = int(np.clip(q_row0 + sub * NA_SUB - NA_ROWS // 2, 0, rows - NA_KROWS))
            kr = k_row0 + np.arange(NA_KROWS)
            for rq in range(NA_SUB):
                r = q_row0 + sub * NA_SUB + rq
                r0 = int(np.clip(r - NA_ROWS // 2, 0, rows - NA_ROWS))
                row_ok = (kr >= r0) & (kr < r0 + NA_ROWS)
                ok = (col_ok[:, None, :] & row_ok[None, :, None]).reshape(GRID_W, NA_KROWS * GRID_W)
                first = k_row0 - r + NA_ROWS - 1 + margin
                window = by_col[:, :, first * GRID_W:(first + NA_KROWS) * GRID_W]
                blocks.append(jnp.where(ok[None], window, -jnp.inf))
    table = jnp.stack(blocks, axis=1)
    return table.reshape(h, 3, n_sub, NA_SUB * GRID_W, NA_KROWS * GRID_W)


def _neighbourhood_attention(p, bm):
    b, n, _ = p.shape
    tq = NA_QROWS * GRID_W
    half = tq // 2
    n_blocks = (n - N_CTX) // tq
    ctx_blocks = N_CTX // half
    col0 = 3 * A_HEADS
    pairs = B_HEADS // 2
    width = B_HEADS * B_DH

    def pattern(j):
        return jnp.minimum(j, 1) + (j == n_blocks - 1).astype(jnp.int32)

    ob_x = pl.pallas_call(
        _na_kernel,
        out_shape=jax.ShapeDtypeStruct((b, n - N_CTX, width), BF16),
        grid=(b, pairs, n_blocks),
        in_specs=[
            pl.BlockSpec((1, half, LANES), lambda i, h, j: (i, ctx_blocks + 2 * j, col0 + h)),
            pl.BlockSpec((1, half, LANES), lambda i, h, j: (i, ctx_blocks + 2 * j + 1, col0 + h)),
            pl.BlockSpec((1, n, LANES), lambda i, h, j: (i, 0, col0 + pairs + h)),
            pl.BlockSpec((1, n, LANES), lambda i, h, j: (i, 0, col0 + 2 * pairs + h)),
            pl.BlockSpec((2, 1, NA_QROWS // NA_SUB, NA_SUB * GRID_W, NA_KROWS * GRID_W),
                         lambda i, h, j: (h, pattern(j), 0, 0, 0)),
        ],
        out_specs=pl.BlockSpec((1, tq, LANES), lambda i, h, j: (i, j, h)),
        compiler_params=_cparams(("arbitrary", "arbitrary", "arbitrary")),
        name="neighbourhood_attention",
    )(p, p, p, p, bm)
    ob_c = pl.pallas_call(
        _na_ctx_kernel,
        out_shape=jax.ShapeDtypeStruct((b, N_CTX, width), BF16),
        grid=(b, pairs),
        in_specs=[
            pl.BlockSpec((1, N_CTX, LANES), lambda i, h: (i, 0, col0 + h)),
            pl.BlockSpec((1, N_CTX, LANES), lambda i, h: (i, 0, col0 + pairs + h)),
            pl.BlockSpec((1, N_CTX, LANES), lambda i, h: (i, 0, col0 + 2 * pairs + h)),
        ],
        out_specs=pl.BlockSpec((1, N_CTX, LANES), lambda i, h: (i, 0, h)),
        compiler_params=_cparams(("arbitrary", "arbitrary")),
        name="context_attention",
    )(p, p, p)
    return ob_x, ob_c


def _gla_kernel(q_ref, k_ref, v_ref, lr_ref, wg_ref, bg_ref, o_ref, ob_ref, g_ref, sf_ref, sb_ref):
    n = q_ref.shape[1]
    blk = GLA_BLOCK
    per_blk = blk // C_CHUNK
    n_blocks = n // blk
    ri = lax.broadcasted_iota(jnp.int32, (blk, blk), 0)
    ci = lax.broadcasted_iota(jnp.int32, (blk, blk), 1)
    same_chunk = (ri // C_CHUNK) == (ci // C_CHUNK)
    keeps = (same_chunk & (ci <= ri), same_chunk & (ci >= ri))
    tris = tuple(jnp.where(kp, 1.0, 0.0).astype(BF16) for kp in keeps)
    w_gate = jnp.concatenate([wg_ref[0], wg_ref[1]], axis=1).astype(BF16)
    b_gate = jnp.concatenate([bg_ref[0:1, :], bg_ref[1:2, :]], axis=1)
    for i in range(n_blocks):
        z = _dot(lr_ref[0, i * blk:(i + 1) * blk, :].astype(BF16), w_gate) + b_gate
        g_ref[i * blk:(i + 1) * blk, :] = (jnp.minimum(z, 0.0) - jnp.log(1.0 + jnp.exp(-jnp.abs(z)))) * (1.0 / C_TAU)

    def block(sb, direction, st_ref):
        keep = keeps[direction]
        end_row = C_CHUNK - 1 if direction == 0 else 0
        mid_row = C_CHUNK // 2 - 1 if direction == 0 else C_CHUNK // 2
        r0 = sb * blk
        g = g_ref[r0:r0 + blk, direction * C_DK:(direction + 1) * C_DK]
        g_hi = g.astype(BF16)
        g_lo = (g - g_hi.astype(F32)).astype(BF16)
        gc2 = _dot(tris[direction], jnp.concatenate([g_hi, g_lo], axis=1))
        gc = (gc2[:, :C_DK] + gc2[:, C_DK:]).reshape(per_blk, C_CHUNK, C_DK)
        g_end = gc[:, end_row:end_row + 1, :]
        g_mid = gc[:, mid_row:mid_row + 1, :]
        q = (q_ref[0, r0:r0 + blk, :].astype(F32) * (C_DK ** -0.5)).reshape(per_blk, C_CHUNK, C_DK)
        k = k_ref[0, r0:r0 + blk, :].astype(F32).reshape(per_blk, C_CHUNK, C_DK)
        v = v_ref[0, r0:r0 + blk, :]
        q_in = (q * jnp.exp(gc)).astype(BF16).reshape(blk, C_DK)
        q_mid = (q * jnp.exp(gc - g_mid)).astype(BF16).reshape(blk, C_DK)
        k_mid = (k * jnp.exp(g_mid - gc)).astype(BF16).reshape(blk, C_DK)
        k_end = (k * jnp.exp(g_end - gc)).astype(BF16).reshape(blk, C_DK)
        a = jnp.where(keep, _dot_nt(q_mid, k_mid), 0.0)
        o_intra = _dot(a.astype(BF16), v)
        ends = jnp.concatenate([g_end.reshape(per_blk, C_DK), jnp.zeros((8 - per_blk, C_DK), F32)], axis=0)
        decay = jnp.transpose(jnp.exp(ends))
        o_inter = [None] * per_blk
        for c in (range(per_blk) if direction == 0 else reversed(range(per_blk))):
            rows = slice(c * C_CHUNK, (c + 1) * C_CHUNK)
            st = st_ref[...]
            o_inter[c] = _dot(q_in[rows], st.astype(BF16))
            st_ref[...] = st * decay[:, c:c + 1] + _dot_tn(k_end[rows], v[rows])
        return o_intra + jnp.concatenate(o_inter, axis=0)

    sf_ref[...] = jnp.zeros_like(sf_ref)
    sb_ref[...] = jnp.zeros_like(sb_ref)
    ctx_blocks = N_CTX // blk
    order_b = list(reversed(range(ctx_blocks))) + list(reversed(range(ctx_blocks, n_blocks)))
    for i in range(n_blocks):
        o_ref[0, i * blk:(i + 1) * blk, :] = block(i, 0, sf_ref)
        sb = order_b[i]
        ob_ref[sb * blk:(sb + 1) * blk, :] = block(sb, 1, sb_ref)
    o_ref[0] = o_ref[0] + ob_ref[...]


def _gla(p, lr, wg, bg):
    b, n, _ = p.shape
    return pl.pallas_call(
        _gla_kernel,
        out_shape=jax.ShapeDtypeStruct((b, n, C_HEADS * C_DV), F32),
        grid=(b, C_HEADS),
        in_specs=[
            pl.BlockSpec((1, n, C_DK), lambda i, h: (i, 0, h)),
            pl.BlockSpec((1, n, C_DK), lambda i, h: (i, 0, C_HEADS + h)),
            pl.BlockSpec((1, n, C_DV), lambda i, h: (i, 0, C_HEADS + h)),
            pl.BlockSpec((1, n, LANES), lambda i, h: (i, 0, 0)),
            pl.BlockSpec((2, LANES, C_DK), lambda i, h: (0, 0, h)),
            pl.BlockSpec((2, C_DK), lambda i, h: (0, h)),
        ],
        out_specs=pl.BlockSpec((1, n, C_DV), lambda i, h: (i, 0, h)),
        scratch_shapes=[pltpu.VMEM((n, C_DV), F32), pltpu.VMEM((n, 2 * C_DK), F32),
                        pltpu.VMEM((C_DK, C_DV), F32), pltpu.VMEM((C_DK, C_DV), F32)],
        compiler_params=_cparams(("arbitrary", "arbitrary")),
        name="gla",
    )(p, p, p, lr, wg, bg)


def _out_kernel(*refs, gla):
    if gla:
        o_ref, gate_ref, gn_ref, w_ref, s_ref, mod_ref, ng_ref, rw_ref, snew_ref, h2_ref, lg_ref = refs
        o = o_ref[0]
        gate = gate_ref[0].astype(F32)
        parts = []
        for hd in range(C_HEADS):
            oh = _rms(o[:, hd * C_DV:(hd + 1) * C_DV], gn_ref[...])
            parts.append((oh * _silu(gate[:, hd * C_DV:(hd + 1) * C_DV])).astype(BF16))
        acc = _dot(jnp.concatenate(parts, axis=-1), w_ref[...])
    else:
        oa_ref, obx_ref, obc_ref, w_ref, s_ref, mod_ref, ng_ref, rw_ref, snew_ref, h2_ref, lg_ref = refs
        half = oa_ref.shape[2]
        ob = jnp.where(pl.program_id(1) == 0, obc_ref[0], obx_ref[0])
        acc = _dot(oa_ref[0], w_ref[0:half, :]) + _dot(ob, w_ref[half:, :])
    x = s_ref[0] + mod_ref[0, 0, 2:3, :] * acc
    snew_ref[0] = x
    h2 = _rms(x, ng_ref[...]) * (1.0 + mod_ref[0, 0, 4:5, :]) + mod_ref[0, 0, 3:4, :]
    _store_token_rows(h2_ref, h2)
    t = _dot(h2.astype(BF16), rw_ref[...])
    lg_ref[0] = t + pltpu.roll(t, LANES - N_EXPERTS, 1)


def _out_project(mix, w_out, s, mod, ng, rw, *, gla, gn=None):
    b, n, d = s.shape
    tm = ROW_TILE
    row_spec = pl.BlockSpec((1, tm, d), lambda i, j: (i, j, 0))
    const2 = lambda i, j: (0, 0)
    if gla:
        o, p = mix
        in_specs = [row_spec, pl.BlockSpec((1, tm, d), lambda i, j: (i, j, 2)),
                    pl.BlockSpec((1, C_DV), const2)]
        args = [o, p, gn.reshape(1, C_DV)]
    else:
        oa, ob_x, ob_c = mix
        half = oa.shape[2]
        in_specs = [pl.BlockSpec((1, tm, half), lambda i, j: (i, j, 0)),
                    pl.BlockSpec((1, tm, half), lambda i, j: (i, jnp.maximum(j - 1, 0), 0)),
                    pl.BlockSpec((1, tm, half), lambda i, j: (i, 0, 0))]
        args = [oa, ob_x, ob_c]
    in_specs += [pl.BlockSpec((d, d), const2), row_spec,
                 pl.BlockSpec((1, 1, 6, d), lambda i, j: (i, jnp.minimum(j, 1), 0, 0)),
                 pl.BlockSpec((1, d), const2), pl.BlockSpec((d, LANES), const2)]
    args += [w_out, s, mod, ng.reshape(1, d), rw]
    return pl.pallas_call(
        functools.partial(_out_kernel, gla=gla),
        out_shape=[jax.ShapeDtypeStruct((b, n, d), F32), jax.ShapeDtypeStruct((b * n * ROW_SEGS, LANES), jnp.int32),
                   jax.ShapeDtypeStruct((b, n, LANES), F32)],
        grid=(b, n // tm),
        in_specs=in_specs,
        out_specs=[row_spec, pl.BlockSpec((tm * ROW_SEGS, LANES), lambda i, j: (i * (n // tm) + j, 0)),
                   pl.BlockSpec((1, tm, LANES), lambda i, j: (i, j, 0))],
        compiler_params=_cparams(("arbitrary", "arbitrary")),
        name="out_project",
    )(*args)


def _route_kernel(bias_ref, lg_ref, cls_ref, wlo_ref, whi_ref):
    score = [_sigmoid(lg_ref[e]) for e in range(N_EXPERTS)]
    sel = [score[e] + bias_ref[e] for e in range(N_EXPERTS)]
    grp_score = []
    for g in range(N_GROUPS):
        v = sel[g * PER_GROUP:(g + 1) * PER_GROUP]
        best = v[0] + v[1]
        for a in range(PER_GROUP):
            for c in range(a + 1, PER_GROUP):
                if (a, c) != (0, 1):
                    best = jnp.maximum(best, v[a] + v[c])
        grp_score.append(best)
    grp = jnp.zeros(grp_score[0].shape, jnp.int32)
    best = grp_score[0]
    for g in range(1, N_GROUPS):
        upd = grp_score[g] > best
        best = jnp.where(upd, grp_score[g], best)
        grp = jnp.where(upd, g, grp)

    def pick(vals, j):
        out = vals[j]
        for g in range(1, N_GROUPS):
            out = jnp.where(grp == g, vals[g * PER_GROUP + j], out)
        return out

    v = [pick(sel, j) for j in range(PER_GROUP)]
    sc = [pick(score, j) for j in range(PER_GROUP)]
    one = jnp.ones(grp.shape, jnp.int32)
    zero = jnp.zeros(grp.shape, jnp.int32)
    chosen = []
    for j in range(PER_GROUP):
        rank = zero
        for m in range(PER_GROUP):
            if m == j:
                continue
            ahead = (v[m] >= v[j]) if m < j else (v[m] > v[j])
            rank = rank + jnp.where(ahead, one, zero)
        chosen.append(rank < 2)
    code = zero
    for j in range(PER_GROUP):
        code = code + jnp.where(chosen[j], one * (1 << j), zero)
    pair = zero
    for idx in range(6):
        pair = jnp.where(code == (1 << PAIR_LO[idx]) + (1 << PAIR_HI[idx]), idx, pair)
    s_lo = jnp.where(chosen[0], sc[0], jnp.where(chosen[1], sc[1], sc[2]))
    s_hi = jnp.where(chosen[3], sc[3], jnp.where(chosen[2], sc[2], sc[1]))
    den = s_lo + s_hi
    cls_ref[...] = grp * 6 + pair
    wlo_ref[...] = s_lo / den
    whi_ref[...] = s_hi / den


def _route(logits_t, router_bias):
    _, r, _ = logits_t.shape
    full = pl.BlockSpec((r, LANES), lambda i: (0, 0))
    return pl.pallas_call(
        _route_kernel,
        out_shape=[jax.ShapeDtypeStruct((r, LANES), jnp.int32), jax.ShapeDtypeStruct((r, LANES), F32),
                   jax.ShapeDtypeStruct((r, LANES), F32)],
        grid=(1,),
        in_specs=[pl.BlockSpec(memory_space=pltpu.SMEM),
                  pl.BlockSpec((N_EXPERTS, r, LANES), lambda i: (0, 0, 0))],
        out_specs=[full, full, full],
        compiler_params=_cparams(("arbitrary",)),
        name="route",
    )(router_bias, logits_t)


def _gather_rows(table, idx):
    p = idx.shape[0]
    per_worker = p // SC_WORKERS
    n_chunks = per_worker // SC_GATHER_ROWS
    mesh = plsc.VectorSubcoreMesh(core_axis_name="c", subcore_axis_name="s", num_cores=SC_CORES,
                                  num_subcores=SC_SUBCORES)

    @functools.partial(
        pl.kernel, mesh=mesh,
        out_type=jax.ShapeDtypeStruct((p,) + table.shape[1:], table.dtype),
        scratch_types=[pltpu.VMEM((per_worker,), jnp.int32),
                       pltpu.VMEM((SC_GATHER_ROWS,) + table.shape[1:], table.dtype),
                       pltpu.SemaphoreType.DMA],
        compiler_params=pltpu.CompilerParams(use_tc_tiling_on_sc=True),
        name="gather_rows",
    )
    def gather(table_hbm, idx_hbm, out_hbm, idx_v, rows_v, sem):
        worker = lax.axis_index("s") * SC_CORES + lax.axis_index("c")
        base = worker * per_worker
        pltpu.sync_copy(idx_hbm.at[pl.ds(base, per_worker)], idx_v)

        @pl.loop(0, n_chunks)
        def _(i):
            off = pl.multiple_of(i * SC_GATHER_ROWS, SC_GATHER_ROWS)
            pltpu.async_copy(table_hbm.at[idx_v.at[pl.ds(off, SC_GATHER_ROWS)]], rows_v, sem).wait()
            pltpu.sync_copy(rows_v, out_hbm.at[pl.ds(base + off, SC_GATHER_ROWS)])

    return gather(table, idx)


def _moe_kernel(lo_ref, hi_ref, nt_ref, x_ref, wr_ref, w1l_ref, w1h_ref, w3l_ref, w3h_ref, w2l_ref, w2h_ref, y_ref):
    j = pl.program_id(0)

    @pl.when(j < nt_ref[0])
    def _():
        x = _token_rows(x_ref).astype(BF16)
        wr = wr_ref[...]
        he_lo = (_silu(_dot(x, w1l_ref[0, 0])) * _dot(x, w3l_ref[0, 0]) * wr[:, 0:1]).astype(BF16)
        he_hi = (_silu(_dot(x, w1h_ref[0, 0])) * _dot(x, w3h_ref[0, 0]) * wr[:, 1:2]).astype(BF16)
        y = _dot(he_lo, w2l_ref[0, 0]) + _dot(he_hi, w2h_ref[0, 0])
        _store_token_rows(y_ref, y)

    @pl.when(j >= nt_ref[0])
    def _():
        y_ref[...] = jnp.zeros_like(y_ref)


def _moe_experts(xs, wrow, tile_lo, tile_hi, n_tiles, layer, w1, w3, w2):
    tp = xs.shape[0] // ROW_SEGS
    tm = MOE_TILE
    d, de = w1.shape[2], w1.shape[3]

    def gate_map(j, lo, hi, nt):
        return (jnp.minimum(j, nt[0] - 1), 0)

    def lo_map(j, lo, hi, nt):
        return (layer, lo[jnp.minimum(j, nt[0] - 1)], 0, 0)

    def hi_map(j, lo, hi, nt):
        return (layer, hi[jnp.minimum(j, nt[0] - 1)], 0, 0)

    up = (1, 1, d, de)
    down = (1, 1, de, d)
    return pl.pallas_call(
        _moe_kernel,
        out_shape=jax.ShapeDtypeStruct((tp * ROW_SEGS, LANES), jnp.int32),
        grid_spec=pltpu.PrefetchScalarGridSpec(
            num_scalar_prefetch=3,
            grid=(tp // tm,),
            in_specs=[pl.BlockSpec((tm * ROW_SEGS, LANES), gate_map), pl.BlockSpec((tm, 2), gate_map),
                      pl.BlockSpec(up, lo_map), pl.BlockSpec(up, hi_map),
                      pl.BlockSpec(up, lo_map), pl.BlockSpec(up, hi_map),
                      pl.BlockSpec(down, lo_map), pl.BlockSpec(down, hi_map)],
            out_specs=pl.BlockSpec((tm * ROW_SEGS, LANES), lambda j, lo, hi, nt: (j, 0)),
        ),
        compiler_params=_cparams(("arbitrary",)),
        name="moe_experts",
    )(tile_lo, tile_hi, n_tiles, xs, wrow, w1, w1, w3, w3, w2, w2)


def _moe(h2, logits, router_bias, layer, w1, w3, w2):
    t = h2.shape[0]
    tm = MOE_TILE
    n_tiles_max = t // tm + N_CLASSES
    tp = n_tiles_max * tm
    lg_t = logits[:, :N_EXPERTS].T.reshape(N_EXPERTS, t // LANES, LANES)
    cls, wlo, whi = _route(lg_t, router_bias)
    cls, wlo, whi = cls.reshape(t), wlo.reshape(t), whi.reshape(t)
    classes = jnp.arange(N_CLASSES, dtype=jnp.int32)
    onehot = (cls[:, None] == classes[None, :]).astype(jnp.int32)
    csum = jnp.cumsum(onehot, axis=0)
    rank = jnp.sum(csum * onehot, axis=1) - 1
    counts = csum[-1]
    tiles_per = (counts + tm - 1) // tm
    tile_end = jnp.cumsum(tiles_per)
    tile_start = tile_end - tiles_per
    dest = jnp.sum(onehot * (tile_start * tm)[None, :], axis=1) + rank
    tiles = jnp.arange(n_tiles_max, dtype=jnp.int32)
    tile_cls = jnp.minimum(jnp.sum((tile_end[None, :] <= tiles[:, None]).astype(jnp.int32), axis=1), N_CLASSES - 1)
    pair = tile_cls % 6
    base = (tile_cls // 6) * PER_GROUP
    pair_onehot = (pair[:, None] == jnp.arange(6, dtype=jnp.int32)[None, :]).astype(jnp.int32)
    tile_lo = base + jnp.sum(pair_onehot * jnp.asarray(PAIR_LO, jnp.int32)[None, :], axis=1)
    tile_hi = base + jnp.sum(pair_onehot * jnp.asarray(PAIR_HI, jnp.int32)[None, :], axis=1)
    per_token = jnp.stack([jnp.arange(t, dtype=jnp.int32), lax.bitcast_convert_type(wlo, jnp.int32),
                           lax.bitcast_convert_type(whi, jnp.int32)], axis=1)
    padding = jnp.stack([jnp.arange(tp, dtype=jnp.int32) % t, jnp.zeros((tp,), jnp.int32),
                         jnp.zeros((tp,), jnp.int32)], axis=1)
    per_row = padding.at[dest].set(per_token, unique_indices=True)
    tok = per_row[:, 0]
    wrow = lax.bitcast_convert_type(per_row[:, 1:3], F32)
    xs = _gather_rows(h2, tok).reshape(tp * ROW_SEGS, LANES)
    ys = _moe_experts(xs, wrow, tile_lo, tile_hi, tile_end[-1:], layer, w1, w3, w2)
    return _gather_rows(ys.reshape(tp, ROW_SEGS, LANES), dest)


def _final_kernel(s_ref, y_ref, mod_ref, g_ref, o_ref):
    x = s_ref[0] + mod_ref[0, 0, 5:6, :] * _token_rows(y_ref)
    o_ref[0] = _rms(x, g_ref[...])


def _final(s, y, mod, final_g):
    b, n, d = s.shape
    tm = ROW_TILE
    skip = N_CTX // tm
    row_spec = pl.BlockSpec((1, tm, d), lambda i, j: (i, j + skip, 0))
    return pl.pallas_call(
        _final_kernel,
        out_shape=jax.ShapeDtypeStruct((b, n - N_CTX, d), F32),
        grid=(b, (n - N_CTX) // tm),
        in_specs=[row_spec, pl.BlockSpec((tm * ROW_SEGS, LANES), lambda i, j: (i * (n // tm) + j + skip, 0)),
                  pl.BlockSpec((1, 1, 6, d), lambda i, j: (i, 1, 0, 0)),
                  pl.BlockSpec((1, d), lambda i, j: (0, 0))],
        out_specs=pl.BlockSpec((1, tm, d), lambda i, j: (i, j, 0)),
        compiler_params=_cparams(("arbitrary", "arbitrary")),
        name="final_norm",
    )(s, y, mod, final_g.reshape(1, d))


def _rope_tables(seq):
    pos = jnp.arange(seq)
    row_pos, col_pos = pos // GRID_W, pos % GRID_W
    n = A_DH // 2
    inv = ROPE_BASE ** (-jnp.arange(0, n, 2, dtype=F32) / n)
    ang_r = row_pos.astype(F32)[:, None] * inv[None, :]
    ang_c = col_pos.astype(F32)[:, None] * inv[None, :]
    ang = jnp.concatenate([ang_r, ang_c], axis=-1)
    cos = jnp.tile(jnp.cos(ang), (1, 4))
    sin = jnp.tile(jnp.sin(ang), (1, 4))
    sin = jnp.concatenate([-sin[:, :LANES // 2], sin[:, LANES // 2:]], axis=-1)
    cos = jnp.concatenate([jnp.ones((N_CTX, LANES), F32), cos], axis=0)
    sin = jnp.concatenate([jnp.zeros((N_CTX, LANES), F32), sin], axis=0)
    return cos, sin


def _interleave_maps(w):
    d = w.shape[0]
    w = w.reshape(d, A_HEADS, 2, 2, 2, A_DH // 4)
    return w.transpose(0, 1, 4, 2, 3, 5).reshape(d, A_HEADS * LANES)


def _att_weights(w_in):
    a_qk = A_HEADS * 2 * A_DH
    a_v = A_HEADS * A_DV
    b_w = B_HEADS * B_DH
    qa = _interleave_maps(w_in[:, :a_qk]) * (A_DH ** -0.5 * LOG2_E)
    ka = _interleave_maps(w_in[:, a_qk:2 * a_qk])
    va = w_in[:, 2 * a_qk:2 * a_qk + a_v]
    o = 2 * a_qk + a_v
    qb = w_in[:, o:o + b_w] * (B_DH ** -0.5)
    rest = w_in[:, o + b_w:]
    return jnp.concatenate([qa, ka, va, qb, rest], axis=1).astype(BF16)


def _gla_weights(w_in):
    d = w_in.shape[0]
    n_main = 2 * C_HEADS * C_DK + 2 * C_HEADS * C_DV
    pad = jnp.zeros((d, LANES - 2 * C_RANK), w_in.dtype)
    return jnp.concatenate([w_in, pad], axis=1).astype(BF16), n_main


def kernel(x, c, ctx, c_ctx, w_mod, b_mod, norm_g, final_g, att_w_in, att_w_out, att_lambda, att_subln_g, na_bias,
           gla_w_in, gla_w_gate, gla_b_gate, gla_norm_g, gla_w_out, router_w, router_bias, moe_w1, moe_w3, moe_w2):
    b, seq, d = x.shape
    n = N_CTX + seq
    s = jnp.concatenate([ctx, x], axis=1)

    rows = b + 1
    rows_pad = -(-rows // 8) * 8
    cc = jnp.concatenate([c, c_ctx[None, :], jnp.zeros((rows_pad - rows, d), F32)], axis=0)
    mod_all = _mod_vectors(cc, w_mod, b_mod)
    mod_x = mod_all[:, :b].reshape(DEPTH, b, 1, 6, d)
    mod_c = jnp.broadcast_to(mod_all[:, b].reshape(DEPTH, 1, 1, 6, d), (DEPTH, b, 1, 6, d))
    mods = jnp.concatenate([mod_c, mod_x], axis=2)

    cos, sin = _rope_tables(seq)
    rw_hi = router_w.astype(BF16)
    rw_lo = (router_w - rw_hi.astype(F32)).astype(BF16)
    zpad = jnp.zeros((d, LANES - 2 * N_EXPERTS), BF16)
    rw = jnp.concatenate([rw_hi, rw_lo, zpad], axis=1)

    y = None
    expert_w = (moe_w1, moe_w3, moe_w2)
    for i in range(DEPTH):
        w1_b, w3_b, w2_b = (w[i:i + 1].astype(BF16) for w in expert_w)
        j = i // 2
        modp = mods[i - 1] if i else None
        if i % 2 == 0:
            lam_init = 0.8 - 0.6 * math.exp(-0.3 * i)
            w = _att_weights(att_w_in[j])
            outs = _project(s, y, modp, mods[i], norm_g[i, 0], w, cos, sin,
                            n_rope=2 * A_HEADS * LANES, n_bf16=w.shape[1])
            if i:
                s = outs[0]
            p = outs[-1]
            oa = _diff_attention(p, att_lambda[j], att_subln_g[j], lam_init)
            ob_x, ob_c = _neighbourhood_attention(p, _na_bias_table(na_bias[j], seq // GRID_W))
            s, h2, logits = _out_project((oa, ob_x, ob_c), att_w_out[j].astype(BF16), s, mods[i], norm_g[i, 1],
                                         rw, gla=False)
        else:
            w, n_main = _gla_weights(gla_w_in[j])
            outs = _project(s, y, modp, mods[i], norm_g[i, 0], w, None, None, n_rope=0, n_bf16=n_main)
            s, p, lr = outs
            wg = jnp.zeros((2, LANES, C_HEADS * C_DK), F32)
            wg = wg.at[0, :C_RANK].set(gla_w_gate[j, 0]).at[1, C_RANK:2 * C_RANK].set(gla_w_gate[j, 1])
            o = _gla(p, lr, wg, gla_b_gate[j])
            s, h2, logits = _out_project((o, p), gla_w_out[j].astype(BF16), s, mods[i], norm_g[i, 1],
                                         rw, gla=True, gn=gla_norm_g[j])
        y = _moe(h2.reshape(b * n, ROW_SEGS, LANES), logits.reshape(b * n, LANES), router_bias,
                 0, w1_b, w3_b, w2_b)
        y = y.reshape(b * n * ROW_SEGS, LANES)
        expert_w = lax.optimization_barrier((moe_w1, moe_w3, moe_w2, logits))[:3]
    return _final(s, y, mods[DEPTH - 1], final_g)
```

```python
import functools
import math

import jax
import jax.numpy as jnp
import numpy as np
from jax import lax
from jax.experimental import pallas as pl
from jax.experimental.pallas import tpu as pltpu
from jax.experimental.pallas import tpu_sc as plsc

F32 = jnp.float32
BF16 = jnp.bfloat16

D_MODEL = 1024
DEPTH = 4
GRID_W = 64
N_CTX = 256
A_HEADS = 4
A_DH = 64
A_DV = 128
B_HEADS = 8
B_DH = 64
NA_ROWS = 8
NA_COLS = 16
C_HEADS = 4
C_DK = 128
C_DV = 256
C_RANK = 16
C_TAU = 16.0
C_CHUNK = 64
GLA_BLOCK = 256
N_EXPERTS = 16
N_GROUPS = 4
PER_GROUP = 4
D_EXPERT = 512
ROPE_BASE = 10000.0
EPS = 1e-6
LOG2_E = math.log2(math.e)
ATT_KEY_BLOCK = 768
ATT_Q_HALVES = 2

LANES = 128
ROW_TILE = 256
MOE_TILE = 256
SC_CORES = 2
SC_SUBCORES = 16
SC_WORKERS = SC_CORES * SC_SUBCORES
SC_GATHER_ROWS = 64
N_CLASSES = N_GROUPS * 6
NA_QROWS = 8
NA_SUB = 2
NA_KROWS = NA_SUB + NA_ROWS - 1
VMEM_LIMIT = 52 * 1024 * 1024
ROW_SEGS = D_MODEL // LANES // 2

PAIR_LO = (0, 0, 0, 1, 1, 2)
PAIR_HI = (1, 2, 3, 2, 3, 3)


def _cparams(sem):
    return pltpu.CompilerParams(dimension_semantics=sem, vmem_limit_bytes=VMEM_LIMIT)


def _sigmoid(x):
    return 1.0 / (1.0 + jnp.exp(-x))


def _silu(x):
    return x * _sigmoid(x)


def _rms(x, g):
    return x * lax.rsqrt(jnp.mean(x * x, axis=-1, keepdims=True) + EPS) * g


def _dot(a, b):
    return jnp.dot(a, b, preferred_element_type=F32)


def _dot_nt(a, b):
    return lax.dot_general(a, b, (((1,), (1,)), ((), ())), preferred_element_type=F32)


def _dot_tn(a, b):
    return lax.dot_general(a, b, (((0,), (0,)), ((), ())), preferred_element_type=F32)


def _mod_kernel(c_ref, w_ref, b_ref, o_ref):
    a = _silu(c_ref[...]).astype(BF16)
    o_ref[0] = _dot(a, w_ref[0].astype(BF16)) + b_ref[0]


def _mod_vectors(cc, w_mod, b_mod):
    depth, d, n6 = w_mod.shape
    rows = cc.shape[0]
    tn = 1536
    return pl.pallas_call(
        _mod_kernel,
        out_shape=jax.ShapeDtypeStruct((depth, rows, n6), F32),
        grid=(depth, n6 // tn),
        in_specs=[
            pl.BlockSpec((rows, d), lambda i, j: (0, 0)),
            pl.BlockSpec((1, d, tn), lambda i, j: (i, 0, j)),
            pl.BlockSpec((1, 1, tn), lambda i, j: (i, 0, j)),
        ],
        out_specs=pl.BlockSpec((1, rows, tn), lambda i, j: (i, 0, j)),
        compiler_params=_cparams(("arbitrary", "arbitrary")),
        name="mod_vectors",
    )(cc, w_mod, b_mod.reshape(depth, 1, n6))


def _token_rows(ref):
    rows = ref.shape[0] // ROW_SEGS
    packed = jnp.concatenate([ref[pl.ds(sg, rows, stride=ROW_SEGS), :] for sg in range(ROW_SEGS)], axis=-1)
    low = lax.bitcast_convert_type(lax.shift_left(packed, jnp.int32(16)), F32)
    high = lax.bitcast_convert_type(packed & jnp.int32(-65536), F32)
    return jnp.concatenate([low, high], axis=-1)


def _store_token_rows(ref, val):
    rows, d = val.shape
    rounded = val.astype(BF16).astype(F32)
    bits = lax.bitcast_convert_type(rounded, jnp.int32)
    packed = lax.shift_right_logical(bits[:, :d // 2], jnp.int32(16)) | bits[:, d // 2:]
    for sg in range(ROW_SEGS):
        ref[pl.ds(sg, rows, stride=ROW_SEGS), :] = packed[:, sg * LANES:(sg + 1) * LANES]


def _proj_kernel(*refs, has_y, n_rope, n_bf16, col_chunk):
    it = iter(refs)
    s_ref = next(it)
    y_ref = next(it) if has_y else None
    modp_ref = next(it) if has_y else None
    mod_ref = next(it)
    ng_ref = next(it)
    w_ref = next(it)
    cos_ref = next(it) if n_rope else None
    sin_ref = next(it) if n_rope else None
    snew_ref = next(it) if has_y else None
    p_ref = next(it)
    lr_ref = next(it) if w_ref.shape[1] > n_bf16 else None

    x = s_ref[0]
    if has_y:
        x = x + modp_ref[0, 0, 5:6, :] * _token_rows(y_ref)
        snew_ref[0] = x
    h = _rms(x, ng_ref[...])
    h = h * (1.0 + mod_ref[0, 0, 1:2, :]) + mod_ref[0, 0, 0:1, :]
    hb = h.astype(BF16)
    n_out = w_ref.shape[1]
    for c0 in range(0, n_out, col_chunk):
        c1 = min(c0 + col_chunk, n_out)
        acc = _dot(hb, w_ref[:, c0:c1])
        for b0 in range(c0, c1, LANES):
            t = acc[:, b0 - c0:b0 - c0 + LANES]
            if b0 < n_rope:
                t = t * cos_ref[...] + pltpu.roll(t, LANES // 2, 1) * sin_ref[...]
            if b0 < n_bf16:
                p_ref[0, :, b0:b0 + LANES] = t.astype(BF16)
            else:
                lr_ref[0, :, b0 - n_bf16:b0 - n_bf16 + LANES] = t


def _project(s, y, modp, mod, ng, w, cos, sin, *, n_rope, n_bf16):
    b, n, d = s.shape
    n_out = w.shape[1]
    tm = ROW_TILE
    has_y = y is not None
    row_spec = pl.BlockSpec((1, tm, d), lambda i, j: (i, j, 0))
    mod_spec = pl.BlockSpec((1, 1, 6, d), lambda i, j: (i, jnp.minimum(j, 1), 0, 0))
    in_specs = [row_spec]
    args = [s]
    if has_y:
        in_specs += [pl.BlockSpec((tm * ROW_SEGS, LANES), lambda i, j: (i * (n // tm) + j, 0)), mod_spec]
        args += [y, modp]
    in_specs += [mod_spec, pl.BlockSpec((1, d), lambda i, j: (0, 0)),
                 pl.BlockSpec((d, n_out), lambda i, j: (0, 0))]
    args += [mod, ng.reshape(1, d), w]
    if n_rope:
        tab_spec = pl.BlockSpec((tm, LANES), lambda i, j: (j, 0))
        in_specs += [tab_spec, tab_spec]
        args += [cos, sin]
    out_shape, out_specs = [], []
    if has_y:
        out_shape.append(jax.ShapeDtypeStruct((b, n, d), F32))
        out_specs.append(row_spec)
    out_shape.append(jax.ShapeDtypeStruct((b, n, n_bf16), BF16))
    out_specs.append(pl.BlockSpec((1, tm, n_bf16), lambda i, j: (i, j, 0)))
    if n_out > n_bf16:
        out_shape.append(jax.ShapeDtypeStruct((b, n, n_out - n_bf16), F32))
        out_specs.append(pl.BlockSpec((1, tm, n_out - n_bf16), lambda i, j: (i, j, 0)))
    return pl.pallas_call(
        functools.partial(_proj_kernel, has_y=has_y, n_rope=n_rope, n_bf16=n_bf16, col_chunk=512),
        out_shape=out_shape,
        grid=(b, n // tm),
        in_specs=in_specs,
        out_specs=out_specs,
        compiler_params=_cparams(("arbitrary", "arbitrary")),
        name="norm_mod_project",
    )(*args)


def _softmax_rows(s):
    e = jnp.exp(s - jnp.max(s, axis=-1, keepdims=True))
    return e * (1.0 / jnp.sum(e, axis=-1, keepdims=True))


def _diff_attn_kernel(*refs, lam_init, n_q):
    q_refs = refs[:n_q]
    k_ref, v_ref, lam_ref, g_ref, o_ref = refs[n_q:]
    lp = lam_ref[...]
    lam = (jnp.exp(jnp.sum(lp[0:1] * lp[1:2], axis=-1, keepdims=True))
           - jnp.exp(jnp.sum(lp[2:3] * lp[3:4], axis=-1, keepdims=True)) + lam_init)
    q = jnp.concatenate([r[0] for r in q_refs], axis=0)
    lane = lax.broadcasted_iota(jnp.int32, (1, LANES), 1)
    first_map = (lane // 32) % 2 == 0
    zero = jnp.zeros_like(q)
    q0 = jnp.where(first_map, q, zero)
    q1 = jnp.where(first_map, zero, q)
    n_keys = k_ref.shape[1]
    blk = min(ATT_KEY_BLOCK, n_keys)
    ones = jnp.ones((blk, A_DV), BF16)
    state = [None, None]
    for kb in range(n_keys // blk):
        k = k_ref[0, kb * blk:(kb + 1) * blk, :]
        v1 = jnp.concatenate([v_ref[0, kb * blk:(kb + 1) * blk, :], ones], axis=1)
        for i, qm in enumerate((q0, q1)):
            s = _dot_nt(qm, k)
            m_new = jnp.max(s, axis=-1, keepdims=True)
            if kb:
                m_old, acc_old = state[i]
                m_new = jnp.maximum(m_old, m_new)
            acc = _dot(jnp.exp2(s - m_new).astype(BF16), v1)
            if kb:
                acc = acc_old * jnp.exp2(m_old - m_new) + acc
            state[i] = (m_new, acc)
    acc0, acc1 = state[0][1], state[1][1]
    o = acc0[:, :A_DV] * (1.0 / acc0[:, A_DV:]) - acc1[:, :A_DV] * (lam / acc1[:, A_DV:])
    o_ref[0] = (_rms(o, g_ref[...]) * (1.0 - lam_init)).astype(BF16)


def _diff_attention(p, lam_p, subln_g, lam_init):
    b, n, _ = p.shape
    half = ROW_TILE
    tq = ATT_Q_HALVES * half
    ctx_blocks = N_CTX // half
    width = A_HEADS * A_DV
    small = [pl.BlockSpec((4, A_DH), lambda i, h, *_: (0, 0)), pl.BlockSpec((1, A_DV), lambda i, h, *_: (0, 0))]
    q_specs = [pl.BlockSpec((1, half, LANES), functools.partial(lambda i, h, j, part: (i, ctx_blocks + ATT_Q_HALVES * j + part, h),
                                                                 part=part)) for part in range(ATT_Q_HALVES)]
    oa_x = pl.pallas_call(
        functools.partial(_diff_attn_kernel, lam_init=lam_init, n_q=ATT_Q_HALVES),
        out_shape=jax.ShapeDtypeStruct((b, n - N_CTX, width), BF16),
        grid=(b, A_HEADS, (n - N_CTX) // tq),
        in_specs=q_specs + [
            pl.BlockSpec((1, n, LANES), lambda i, h, j: (i, 0, A_HEADS + h)),
            pl.BlockSpec((1, n, LANES), lambda i, h, j: (i, 0, 2 * A_HEADS + h)),
        ] + small,
        out_specs=pl.BlockSpec((1, tq, LANES), lambda i, h, j: (i, j, h)),
        compiler_params=_cparams(("arbitrary", "arbitrary", "arbitrary")),
        name="diff_attention",
    )(*([p] * ATT_Q_HALVES), p, p, lam_p, subln_g.reshape(1, A_DV))
    oa_c = pl.pallas_call(
        functools.partial(_diff_attn_kernel, lam_init=lam_init, n_q=1),
        out_shape=jax.ShapeDtypeStruct((b, N_CTX, width), BF16),
        grid=(b, A_HEADS),
        in_specs=[
            pl.BlockSpec((1, N_CTX, LANES), lambda i, h: (i, 0, h)),
            pl.BlockSpec((1, N_CTX, LANES), lambda i, h: (i, 0, A_HEADS + h)),
            pl.BlockSpec((1, N_CTX, LANES), lambda i, h: (i, 0, 2 * A_HEADS + h)),
        ] + small,
        out_specs=pl.BlockSpec((1, N_CTX, LANES), lambda i, h: (i, 0, h)),
        compiler_params=_cparams(("arbitrary", "arbitrary")),
        name="context_diff_attention",
    )(p, p, p, lam_p, subln_g.reshape(1, A_DV))
    return oa_x, oa_c


def _split_heads(q):
    lane = lax.broadcasted_iota(jnp.int32, (1, LANES), 1)
    first_head = lane < B_DH
    zero = jnp.zeros_like(q)
    return first_head, (jnp.where(first_head, q, zero), jnp.where(first_head, zero, q))


def _na_ctx_kernel(q_ref, k_ref, v_ref, o_ref):
    first_head, qh = _split_heads(q_ref[0])
    kc = k_ref[0]
    vc = v_ref[0]
    outs = [_dot(_softmax_rows(_dot_nt(qh[h], kc)).astype(BF16), vc) for h in range(2)]
    o_ref[0] = jnp.where(first_head, outs[0], outs[1]).astype(BF16)


def _na_kernel(qa_ref, qb_ref, k_ref, v_ref, bm_ref, o_ref):
    blk = pl.program_id(2)
    rows = (k_ref.shape[1] - N_CTX) // GRID_W
    sub_q = NA_SUB * GRID_W
    sub_k = NA_KROWS * GRID_W
    kc = k_ref[0, 0:N_CTX, :]
    vc = jnp.concatenate([v_ref[0, 0:N_CTX, :], jnp.ones((N_CTX, LANES), BF16)], axis=1)
    ones = jnp.ones((sub_k, LANES), BF16)
    for sub in range(NA_QROWS // NA_SUB):
        q_ref = qa_ref if sub * sub_q < qa_ref.shape[1] else qb_ref
        q0 = (sub * sub_q) % qa_ref.shape[1]
        first_head, qh = _split_heads(q_ref[0, q0:q0 + sub_q, :])
        k_row0 = jnp.clip(blk * NA_QROWS + sub * NA_SUB - NA_ROWS // 2, 0, rows - NA_KROWS)
        start = pl.multiple_of(N_CTX + k_row0 * GRID_W, GRID_W)
        kw = k_ref[0, pl.ds(start, sub_k), :]
        vw = jnp.concatenate([v_ref[0, pl.ds(start, sub_k), :], ones], axis=1)
        outs = []
        for h in range(2):
            s_loc = _dot_nt(qh[h], kw) + bm_ref[h, 0, sub]
            s_ctx = _dot_nt(qh[h], kc)
            m = jnp.maximum(jnp.max(s_loc, axis=-1, keepdims=True), jnp.max(s_ctx, axis=-1, keepdims=True))
            acc = _dot(jnp.exp(s_loc - m).astype(BF16), vw) + _dot(jnp.exp(s_ctx - m).astype(BF16), vc)
            outs.append(acc[:, :LANES] * (1.0 / acc[:, LANES:]))
        o_ref[0, sub * sub_q:(sub + 1) * sub_q, :] = jnp.where(first_head, outs[0], outs[1]).astype(BF16)


def _na_bias_table(na_bias, rows):
    h = na_bias.shape[0]
    n_dr, n_dc = 2 * NA_ROWS - 1, 2 * NA_COLS - 1
    width = 2 * GRID_W
    left = GRID_W - NA_COLS
    u = jnp.pad(na_bias, ((0, 0), (0, 0), (left, width - left - n_dc)))
    skew = jnp.tile(u, (1, 1, GRID_W))[:, :, :GRID_W * (width - 1)].reshape(h, n_dr, GRID_W, width - 1)
    toeplitz = skew[:, :, :, GRID_W - 1:]
    margin = NA_KROWS - NA_ROWS
    by_col = jnp.pad(toeplitz.transpose(0, 2, 1, 3), ((0, 0), (0, 0), (margin, margin), (0, 0)))
    by_col = by_col.reshape(h, GRID_W, (n_dr + 2 * margin) * GRID_W)
    col = np.arange(GRID_W)
    col_start = np.clip(col - NA_COLS // 2, 0, GRID_W - NA_COLS)
    col_ok = (col[None, :] >= col_start[:, None]) & (col[None, :] < col_start[:, None] + NA_COLS)
    n_sub = NA_QROWS // NA_SUB
    blocks = []
    for q_row0 in (0, NA_QROWS, rows - NA_QROWS):
        for sub in range(n_sub):
            k_row0 = int(np.clip(q_row0 + sub * NA_SUB - NA_ROWS // 2, 0, rows - NA_KROWS))
            kr = k_row0 + np.arange(NA_KROWS)
            for rq in range(NA_SUB):
                r = q_row0 + sub * NA_SUB + rq
                r0 = int(np.clip(r - NA_ROWS // 2, 0, rows - NA_ROWS))
                row_ok = (kr >= r0) & (kr < r0 + NA_ROWS)
                ok = (col_ok[:, None, :] & row_ok[None, :, None]).reshape(GRID_W, NA_KROWS * GRID_W)
                first = k_row0 - r + NA_ROWS - 1 + margin
                window = by_col[:, :, first * GRID_W:(first + NA_KROWS) * GRID_W]
                blocks.append(jnp.where(ok[None], window, -jnp.inf))
    table = jnp.stack(blocks, axis=1)
    return table.reshape(h, 3, n_sub, NA_SUB * GRID_W, NA_KROWS * GRID_W)


def _neighbourhood_attention(p, bm):
    b, n, _ = p.shape
    tq = NA_QROWS * GRID_W
    half = tq // 2
    n_blocks = (n - N_CTX) // tq
    ctx_blocks = N_CTX // half
    col0 = 3 * A_HEADS
    pairs = B_HEADS // 2
    width = B_HEADS * B_DH

    def pattern(j):
        return jnp.minimum(j, 1) + (j == n_blocks - 1).astype(jnp.int32)

    ob_x = pl.pallas_call(
        _na_kernel,
        out_shape=jax.ShapeDtypeStruct((b, n - N_CTX, width), BF16),
        grid=(b, pairs, n_blocks),
        in_specs=[
            pl.BlockSpec((1, half, LANES), lambda i, h, j: (i, ctx_blocks + 2 * j, col0 + h)),
            pl.BlockSpec((1, half, LANES), lambda i, h, j: (i, ctx_blocks + 2 * j + 1, col0 + h)),
            pl.BlockSpec((1, n, LANES), lambda i, h, j: (i, 0, col0 + pairs + h)),
            pl.BlockSpec((1, n, LANES), lambda i, h, j: (i, 0, col0 + 2 * pairs + h)),
            pl.BlockSpec((2, 1, NA_QROWS // NA_SUB, NA_SUB * GRID_W, NA_KROWS * GRID_W),
                         lambda i, h, j: (h, pattern(j), 0, 0, 0)),
        ],
        out_specs=pl.BlockSpec((1, tq, LANES), lambda i, h, j: (i, j, h)),
        compiler_params=_cparams(("arbitrary", "arbitrary", "arbitrary")),
        name="neighbourhood_attention",
    )(p, p, p, p, bm)
    ob_c = pl.pallas_call(
        _na_ctx_kernel,
        out_shape=jax.ShapeDtypeStruct((b, N_CTX, width), BF16),
        grid=(b, pairs),
        in_specs=[
            pl.BlockSpec((1, N_CTX, LANES), lambda i, h: (i, 0, col0 + h)),
            pl.BlockSpec((1, N_CTX, LANES), lambda i, h: (i, 0, col0 + pairs + h)),
            pl.BlockSpec((1, N_CTX, LANES), lambda i, h: (i, 0, col0 + 2 * pairs + h)),
        ],
        out_specs=pl.BlockSpec((1, N_CTX, LANES), lambda i, h: (i, 0, h)),
        compiler_params=_cparams(("arbitrary", "arbitrary")),
        name="context_attention",
    )(p, p, p)
    return ob_x, ob_c


def _gla_kernel(q_ref, k_ref, v_ref, lr_ref, wg_ref, bg_ref, o_ref, ob_ref, g_ref, sf_ref, sb_ref):
    n = q_ref.shape[1]
    blk = GLA_BLOCK
    per_blk = blk // C_CHUNK
    n_blocks = n // blk
    ri = lax.broadcasted_iota(jnp.int32, (blk, blk), 0)
    ci = lax.broadcasted_iota(jnp.int32, (blk, blk), 1)
    same_chunk = (ri // C_CHUNK) == (ci // C_CHUNK)
    keeps = (same_chunk & (ci <= ri), same_chunk & (ci >= ri))
    tris = tuple(jnp.where(kp, 1.0, 0.0).astype(BF16) for kp in keeps)
    w_gate = jnp.concatenate([wg_ref[0], wg_ref[1]], axis=1).astype(BF16)
    b_gate = jnp.concatenate([bg_ref[0:1, :], bg_ref[1:2, :]], axis=1)
    for i in range(n_blocks):
        z = _dot(lr_ref[0, i * blk:(i + 1) * blk, :].astype(BF16), w_gate) + b_gate
        g_ref[i * blk:(i + 1) * blk, :] = (jnp.minimum(z, 0.0) - jnp.log(1.0 + jnp.exp(-jnp.abs(z)))) * (1.0 / C_TAU)

    def block(sb, direction, st_ref):
        keep = keeps[direction]
        end_row = C_CHUNK - 1 if direction == 0 else 0
        mid_row = C_CHUNK // 2 - 1 if direction == 0 else C_CHUNK // 2
        r0 = sb * blk
        g = g_ref[r0:r0 + blk, direction * C_DK:(direction + 1) * C_DK]
        g_hi = g.astype(BF16)
        g_lo = (g - g_hi.astype(F32)).astype(BF16)
        gc2 = _dot(tris[direction], jnp.concatenate([g_hi, g_lo], axis=1))
        gc = (gc2[:, :C_DK] + gc2[:, C_DK:]).reshape(per_blk, C_CHUNK, C_DK)
        g_end = gc[:, end_row:end_row + 1, :]
        g_mid = gc[:, mid_row:mid_row + 1, :]
        q = (q_ref[0, r0:r0 + blk, :].astype(F32) * (C_DK ** -0.5)).reshape(per_blk, C_CHUNK, C_DK)
        k = k_ref[0, r0:r0 + blk, :].astype(F32).reshape(per_blk, C_CHUNK, C_DK)
        v = v_ref[0, r0:r0 + blk, :]
        q_in = (q * jnp.exp(gc)).astype(BF16).reshape(blk, C_DK)
        q_mid = (q * jnp.exp(gc - g_mid)).astype(BF16).reshape(blk, C_DK)
        k_mid = (k * jnp.exp(g_mid - gc)).astype(BF16).reshape(blk, C_DK)
        k_end = (k * jnp.exp(g_end - gc)).astype(BF16).reshape(blk, C_DK)
        a = jnp.where(keep, _dot_nt(q_mid, k_mid), 0.0)
        o_intra = _dot(a.astype(BF16), v)
        ends = jnp.concatenate([g_end.reshape(per_blk, C_DK), jnp.zeros((8 - per_blk, C_DK), F32)], axis=0)
        decay = jnp.transpose(jnp.exp(ends))
        o_inter = [None] * per_blk
        for c in (range(per_blk) if direction == 0 else reversed(range(per_blk))):
            rows = slice(c * C_CHUNK, (c + 1) * C_CHUNK)
            st = st_ref[...]
            o_inter[c] = _dot(q_in[rows], st.astype(BF16))
            st_ref[...] = st * decay[:, c:c + 1] + _dot_tn(k_end[rows], v[rows])
        return o_intra + jnp.concatenate(o_inter, axis=0)

    sf_ref[...] = jnp.zeros_like(sf_ref)
    sb_ref[...] = jnp.zeros_like(sb_ref)
    ctx_blocks = N_CTX // blk
    order_b = list(reversed(range(ctx_blocks))) + list(reversed(range(ctx_blocks, n_blocks)))
    for i in range(n_blocks):
        o_ref[0, i * blk:(i + 1) * blk, :] = block(i, 0, sf_ref)
        sb = order_b[i]
        ob_ref[sb * blk:(sb + 1) * blk, :] = block(sb, 1, sb_ref)
    o_ref[0] = o_ref[0] + ob_ref[...]


def _gla(p, lr, wg, bg):
    b, n, _ = p.shape
    return pl.pallas_call(
        _gla_kernel,
        out_shape=jax.ShapeDtypeStruct((b, n, C_HEADS * C_DV), F32),
        grid=(b, C_HEADS),
        in_specs=[
            pl.BlockSpec((1, n, C_DK), lambda i, h: (i, 0, h)),
            pl.BlockSpec((1, n, C_DK), lambda i, h: (i, 0, C_HEADS + h)),
            pl.BlockSpec((1, n, C_DV), lambda i, h: (i, 0, C_HEADS + h)),
            pl.BlockSpec((1, n, LANES), lambda i, h: (i, 0, 0)),
            pl.BlockSpec((2, LANES, C_DK), lambda i, h: (0, 0, h)),
            pl.BlockSpec((2, C_DK), lambda i, h: (0, h)),
        ],
        out_specs=pl.BlockSpec((1, n, C_DV), lambda i, h: (i, 0, h)),
        scratch_shapes=[pltpu.VMEM((n, C_DV), F32), pltpu.VMEM((n, 2 * C_DK), F32),
                        pltpu.VMEM((C_DK, C_DV), F32), pltpu.VMEM((C_DK, C_DV), F32)],
        compiler_params=_cparams(("arbitrary", "arbitrary")),
        name="gla",
    )(p, p, p, lr, wg, bg)


def _out_kernel(*refs, gla):
    if gla:
        o_ref, gate_ref, gn_ref, w_ref, s_ref, mod_ref, ng_ref, rw_ref, snew_ref, h2_ref, lg_ref = refs
        o = o_ref[0]
        gate = gate_ref[0].astype(F32)
        parts = []
        for hd in range(C_HEADS):
            oh = _rms(o[:, hd * C_DV:(hd + 1) * C_DV], gn_ref[...])
            parts.append((oh * _silu(gate[:, hd * C_DV:(hd + 1) * C_DV])).astype(BF16))
        acc = _dot(jnp.concatenate(parts, axis=-1), w_ref[...])
    else:
        oax_ref, oac_ref, obx_ref, obc_ref, w_ref, s_ref, mod_ref, ng_ref, rw_ref, snew_ref, h2_ref, lg_ref = refs
        half = oax_ref.shape[2]
        is_ctx = pl.program_id(1) == 0
        oa = jnp.where(is_ctx, oac_ref[0], oax_ref[0])
        ob = jnp.where(is_ctx, obc_ref[0], obx_ref[0])
        acc = _dot(oa, w_ref[0:half, :]) + _dot(ob, w_ref[half:, :])
    x = s_ref[0] + mod_ref[0, 0, 2:3, :] * acc
    snew_ref[0] = x
    h2 = _rms(x, ng_ref[...]) * (1.0 + mod_ref[0, 0, 4:5, :]) + mod_ref[0, 0, 3:4, :]
    _store_token_rows(h2_ref, h2)
    t = _dot(h2.astype(BF16), rw_ref[...])
    lg_ref[0] = t + pltpu.roll(t, LANES - N_EXPERTS, 1)


def _out_project(mix, w_out, s, mod, ng, rw, *, gla, gn=None):
    b, n, d = s.shape
    tm = ROW_TILE
    row_spec = pl.BlockSpec((1, tm, d), lambda i, j: (i, j, 0))
    const2 = lambda i, j: (0, 0)
    if gla:
        o, p = mix
        in_specs = [row_spec, pl.BlockSpec((1, tm, d), lambda i, j: (i, j, 2)),
                    pl.BlockSpec((1, C_DV), const2)]
        args = [o, p, gn.reshape(1, C_DV)]
    else:
        oa_x, oa_c, ob_x, ob_c = mix
        half = oa_x.shape[2]
        latent = pl.BlockSpec((1, tm, half), lambda i, j: (i, jnp.maximum(j - 1, 0), 0))
        context = pl.BlockSpec((1, tm, half), lambda i, j: (i, 0, 0))
        in_specs = [latent, context, latent, context]
        args = [oa_x, oa_c, ob_x, ob_c]
    in_specs += [pl.BlockSpec((d, d), const2), row_spec,
                 pl.BlockSpec((1, 1, 6, d), lambda i, j: (i, jnp.minimum(j, 1), 0, 0)),
                 pl.BlockSpec((1, d), const2), pl.BlockSpec((d, LANES), const2)]
    args += [w_out, s, mod, ng.reshape(1, d), rw]
    return pl.pallas_call(
        functools.partial(_out_kernel, gla=gla),
        out_shape=[jax.ShapeDtypeStruct((b, n, d), F32), jax.ShapeDtypeStruct((b * n * ROW_SEGS, LANES), jnp.int32),
                   jax.ShapeDtypeStruct((b, n, LANES), F32)],
        grid=(b, n // tm),
        in_specs=in_specs,
        out_specs=[row_spec, pl.BlockSpec((tm * ROW_SEGS, LANES), lambda i, j: (i * (n // tm) + j, 0)),
                   pl.BlockSpec((1, tm, LANES), lambda i, j: (i, j, 0))],
        compiler_params=_cparams(("arbitrary", "arbitrary")),
        name="out_project",
    )(*args)


def _route_kernel(bias_ref, lg_ref, cls_ref, wlo_ref, whi_ref):
    score = [_sigmoid(lg_ref[e]) for e in range(N_EXPERTS)]
    sel = [score[e] + bias_ref[e] for e in range(N_EXPERTS)]
    grp_score = []
    for g in range(N_GROUPS):
        v = sel[g * PER_GROUP:(g + 1) * PER_GROUP]
        best = v[0] + v[1]
        for a in range(PER_GROUP):
            for c in range(a + 1, PER_GROUP):
                if (a, c) != (0, 1):
                    best = jnp.maximum(best, v[a] + v[c])
        grp_score.append(best)
    grp = jnp.zeros(grp_score[0].shape, jnp.int32)
    best = grp_score[0]
    for g in range(1, N_GROUPS):
        upd = grp_score[g] > best
        best = jnp.where(upd, grp_score[g], best)
        grp = jnp.where(upd, g, grp)

    def pick(vals, j):
        out = vals[j]
        for g in range(1, N_GROUPS):
            out = jnp.where(grp == g, vals[g * PER_GROUP + j], out)
        return out

    v = [pick(sel, j) for j in range(PER_GROUP)]
    sc = [pick(score, j) for j in range(PER_GROUP)]
    one = jnp.ones(grp.shape, jnp.int32)
    zero = jnp.zeros(grp.shape, jnp.int32)
    chosen = []
    for j in range(PER_GROUP):
        rank = zero
        for m in range(PER_GROUP):
            if m == j:
                continue
            ahead = (v[m] >= v[j]) if m < j else (v[m] > v[j])
            rank = rank + jnp.where(ahead, one, zero)
        chosen.append(rank < 2)
    code = zero
    for j in range(PER_GROUP):
        code = code + jnp.where(chosen[j], one * (1 << j), zero)
    pair = zero
    for idx in range(6):
        pair = jnp.where(code == (1 << PAIR_LO[idx]) + (1 << PAIR_HI[idx]), idx, pair)
    s_lo = jnp.where(chosen[0], sc[0], jnp.where(chosen[1], sc[1], sc[2]))
    s_hi = jnp.where(chosen[3], sc[3], jnp.where(chosen[2], sc[2], sc[1]))
    den = s_lo + s_hi
    cls_ref[...] = grp * 6 + pair
    wlo_ref[...] = s_lo / den
    whi_ref[...] = s_hi / den


def _route(logits_t, router_bias):
    _, r, _ = logits_t.shape
    full = pl.BlockSpec((r, LANES), lambda i: (0, 0))
    return pl.pallas_call(
        _route_kernel,
        out_shape=[jax.ShapeDtypeStruct((r, LANES), jnp.int32), jax.ShapeDtypeStruct((r, LANES), F32),
                   jax.ShapeDtypeStruct((r, LANES), F32)],
        grid=(1,),
        in_specs=[pl.BlockSpec(memory_space=pltpu.SMEM),
                  pl.BlockSpec((N_EXPERTS, r, LANES), lambda i: (0, 0, 0))],
        out_specs=[full, full, full],
        compiler_params=_cparams(("arbitrary",)),
        name="route",
    )(router_bias, logits_t)


def _gather_rows(table, idx):
    p = idx.shape[0]
    per_worker = p // SC_WORKERS
    n_chunks = per_worker // SC_GATHER_ROWS
    mesh = plsc.VectorSubcoreMesh(core_axis_name="c", subcore_axis_name="s", num_cores=SC_CORES,
                                  num_subcores=SC_SUBCORES)

    @functools.partial(
        pl.kernel, mesh=mesh,
        out_type=jax.ShapeDtypeStruct((p,) + table.shape[1:], table.dtype),
        scratch_types=[pltpu.VMEM((per_worker,), jnp.int32),
                       pltpu.VMEM((SC_GATHER_ROWS,) + table.shape[1:], table.dtype),
                       pltpu.SemaphoreType.DMA],
        compiler_params=pltpu.CompilerParams(use_tc_tiling_on_sc=True),
        name="gather_rows",
    )
    def gather(table_hbm, idx_hbm, out_hbm, idx_v, rows_v, sem):
        worker = lax.axis_index("s") * SC_CORES + lax.axis_index("c")
        base = worker * per_worker
        pltpu.sync_copy(idx_hbm.at[pl.ds(base, per_worker)], idx_v)

        @pl.loop(0, n_chunks)
        def _(i):
            off = pl.multiple_of(i * SC_GATHER_ROWS, SC_GATHER_ROWS)
            pltpu.async_copy(table_hbm.at[idx_v.at[pl.ds(off, SC_GATHER_ROWS)]], rows_v, sem).wait()
            pltpu.sync_copy(rows_v, out_hbm.at[pl.ds(base + off, SC_GATHER_ROWS)])

    return gather(table, idx)


def _moe_kernel(lo_ref, hi_ref, nt_ref, x_ref, wr_ref, w1l_ref, w1h_ref, w3l_ref, w3h_ref, w2l_ref, w2h_ref, y_ref):
    j = pl.program_id(0)

    @pl.when(j < nt_ref[0])
    def _():
        x = _token_rows(x_ref).astype(BF16)
        wr = wr_ref[...]
        he_lo = (_silu(_dot(x, w1l_ref[0, 0])) * _dot(x, w3l_ref[0, 0]) * wr[:, 0:1]).astype(BF16)
        he_hi = (_silu(_dot(x, w1h_ref[0, 0])) * _dot(x, w3h_ref[0, 0]) * wr[:, 1:2]).astype(BF16)
        y = _dot(he_lo, w2l_ref[0, 0]) + _dot(he_hi, w2h_ref[0, 0])
        _store_token_rows(y_ref, y)

    @pl.when(j >= nt_ref[0])
    def _():
        y_ref[...] = jnp.zeros_like(y_ref)


def _moe_experts(xs, wrow, tile_lo, tile_hi, n_tiles, layer, w1, w3, w2):
    tp = xs.shape[0] // ROW_SEGS
    tm = MOE_TILE
    d, de = w1.shape[2], w1.shape[3]

    def gate_map(j, lo, hi, nt):
        return (jnp.minimum(j, nt[0] - 1), 0)

    def lo_map(j, lo, hi, nt):
        return (layer, lo[jnp.minimum(j, nt[0] - 1)], 0, 0)

    def hi_map(j, lo, hi, nt):
        return (layer, hi[jnp.minimum(j, nt[0] - 1)], 0, 0)

    up = (1, 1, d, de)
    down = (1, 1, de, d)
    return pl.pallas_call(
        _moe_kernel,
        out_shape=jax.ShapeDtypeStruct((tp * ROW_SEGS, LANES), jnp.int32),
        grid_spec=pltpu.PrefetchScalarGridSpec(
            num_scalar_prefetch=3,
            grid=(tp // tm,),
            in_specs=[pl.BlockSpec((tm * ROW_SEGS, LANES), gate_map), pl.BlockSpec((tm, 2), gate_map),
                      pl.BlockSpec(up, lo_map), pl.BlockSpec(up, hi_map),
                      pl.BlockSpec(up, lo_map), pl.BlockSpec(up, hi_map),
                      pl.BlockSpec(down, lo_map), pl.BlockSpec(down, hi_map)],
            out_specs=pl.BlockSpec((tm * ROW_SEGS, LANES), lambda j, lo, hi, nt: (j, 0)),
        ),
        compiler_params=_cparams(("arbitrary",)),
        name="moe_experts",
    )(tile_lo, tile_hi, n_tiles, xs, wrow, w1, w1, w3, w3, w2, w2)


def _moe(h2, logits, router_bias, layer, w1, w3, w2):
    t = h2.shape[0]
    tm = MOE_TILE
    n_tiles_max = t // tm + N_CLASSES
    tp = n_tiles_max * tm
    lg_t = logits[:, :N_EXPERTS].T.reshape(N_EXPERTS, t // LANES, LANES)
    cls, wlo, whi = _route(lg_t, router_bias)
    cls, wlo, whi = cls.reshape(t), wlo.reshape(t), whi.reshape(t)
    classes = jnp.arange(N_CLASSES, dtype=jnp.int32)
    onehot = (cls[:, None] == classes[None, :]).astype(jnp.int32)
    csum = jnp.cumsum(onehot, axis=0)
    rank = jnp.sum(csum * onehot, axis=1) - 1
    counts = csum[-1]
    tiles_per = (counts + tm - 1) // tm
    tile_end = jnp.cumsum(tiles_per)
    tile_start = tile_end - tiles_per
    dest = jnp.sum(onehot * (tile_start * tm)[None, :], axis=1) + rank
    tiles = jnp.arange(n_tiles_max, dtype=jnp.int32)
    tile_cls = jnp.minimum(jnp.sum((tile_end[None, :] <= tiles[:, None]).astype(jnp.int32), axis=1), N_CLASSES - 1)
    pair = tile_cls % 6
    base = (tile_cls // 6) * PER_GROUP
    pair_onehot = (pair[:, None] == jnp.arange(6, dtype=jnp.int32)[None, :]).astype(jnp.int32)
    tile_lo = base + jnp.sum(pair_onehot * jnp.asarray(PAIR_LO, jnp.int32)[None, :], axis=1)
    tile_hi = base + jnp.sum(pair_onehot * jnp.asarray(PAIR_HI, jnp.int32)[None, :], axis=1)
    per_token = jnp.stack([jnp.arange(t, dtype=jnp.int32), lax.bitcast_convert_type(wlo, jnp.int32),
                           lax.bitcast_convert_type(whi, jnp.int32)], axis=1)
    padding = jnp.stack([jnp.arange(tp, dtype=jnp.int32) % t, jnp.zeros((tp,), jnp.int32),
                         jnp.zeros((tp,), jnp.int32)], axis=1)
    per_row = padding.at[dest].set(per_token, unique_indices=True)
    tok = per_row[:, 0]
    wrow = lax.bitcast_convert_type(per_row[:, 1:3], F32)
    xs = _gather_rows(h2, tok).reshape(tp * ROW_SEGS, LANES)
    ys = _moe_experts(xs, wrow, tile_lo, tile_hi, tile_end[-1:], layer, w1, w3, w2)
    return _gather_rows(ys.reshape(tp, ROW_SEGS, LANES), dest)


def _final_kernel(s_ref, y_ref, mod_ref, g_ref, o_ref):
    x = s_ref[0] + mod_ref[0, 0, 5:6, :] * _token_rows(y_ref)
    o_ref[0] = _rms(x, g_ref[...])


def _final(s, y, mod, final_g):
    b, n, d = s.shape
    tm = ROW_TILE
    skip = N_CTX // tm
    row_spec = pl.BlockSpec((1, tm, d), lambda i, j: (i, j + skip, 0))
    return pl.pallas_call(
        _final_kernel,
        out_shape=jax.ShapeDtypeStruct((b, n - N_CTX, d), F32),
        grid=(b, (n - N_CTX) // tm),
        in_specs=[row_spec, pl.BlockSpec((tm * ROW_SEGS, LANES), lambda i, j: (i * (n // tm) + j + skip, 0)),
                  pl.BlockSpec((1, 1, 6, d), lambda i, j: (i, 1, 0, 0)),
                  pl.BlockSpec((1, d), lambda i, j: (0, 0))],
        out_specs=pl.BlockSpec((1, tm, d), lambda i, j: (i, j, 0)),
        compiler_params=_cparams(("arbitrary", "arbitrary")),
        name="final_norm",
    )(s, y, mod, final_g.reshape(1, d))


def _rope_tables(seq):
    pos = jnp.arange(seq)
    row_pos, col_pos = pos // GRID_W, pos % GRID_W
    n = A_DH // 2
    inv = ROPE_BASE ** (-jnp.arange(0, n, 2, dtype=F32) / n)
    ang_r = row_pos.astype(F32)[:, None] * inv[None, :]
    ang_c = col_pos.astype(F32)[:, None] * inv[None, :]
    ang = jnp.concatenate([ang_r, ang_c], axis=-1)
    cos = jnp.tile(jnp.cos(ang), (1, 4))
    sin = jnp.tile(jnp.sin(ang), (1, 4))
    sin = jnp.concatenate([-sin[:, :LANES // 2], sin[:, LANES // 2:]], axis=-1)
    cos = jnp.concatenate([jnp.ones((N_CTX, LANES), F32), cos], axis=0)
    sin = jnp.concatenate([jnp.zeros((N_CTX, LANES), F32), sin], axis=0)
    return cos, sin


def _interleave_maps(w):
    d = w.shape[0]
    w = w.reshape(d, A_HEADS, 2, 2, 2, A_DH // 4)
    return w.transpose(0, 1, 4, 2, 3, 5).reshape(d, A_HEADS * LANES)


def _att_weights(w_in):
    a_qk = A_HEADS * 2 * A_DH
    a_v = A_HEADS * A_DV
    b_w = B_HEADS * B_DH
    qa = _interleave_maps(w_in[:, :a_qk]) * (A_DH ** -0.5 * LOG2_E)
    ka = _interleave_maps(w_in[:, a_qk:2 * a_qk])
    va = w_in[:, 2 * a_qk:2 * a_qk + a_v]
    o = 2 * a_qk + a_v
    qb = w_in[:, o:o + b_w] * (B_DH ** -0.5)
    rest = w_in[:, o + b_w:]
    return jnp.concatenate([qa, ka, va, qb, rest], axis=1).astype(BF16)


def _gla_weights(w_in):
    d = w_in.shape[0]
    n_main = 2 * C_HEADS * C_DK + 2 * C_HEADS * C_DV
    pad = jnp.zeros((d, LANES - 2 * C_RANK), w_in.dtype)
    return jnp.concatenate([w_in, pad], axis=1).astype(BF16), n_main


def kernel(x, c, ctx, c_ctx, w_mod, b_mod, norm_g, final_g, att_w_in, att_w_out, att_lambda, att_subln_g, na_bias,
           gla_w_in, gla_w_gate, gla_b_gate, gla_norm_g, gla_w_out, router_w, router_bias, moe_w1, moe_w3, moe_w2):
    b, seq, d = x.shape
    n = N_CTX + seq
    s = jnp.concatenate([ctx, x], axis=1)

    rows = b + 1
    rows_pad = -(-rows // 8) * 8
    cc = jnp.concatenate([c, c_ctx[None, :], jnp.zeros((rows_pad - rows, d), F32)], axis=0)
    mod_all = _mod_vectors(cc, w_mod, b_mod)
    mod_x = mod_all[:, :b].reshape(DEPTH, b, 1, 6, d)
    mod_c = jnp.broadcast_to(mod_all[:, b].reshape(DEPTH, 1, 1, 6, d), (DEPTH, b, 1, 6, d))
    mods = jnp.concatenate([mod_c, mod_x], axis=2)

    cos, sin = _rope_tables(seq)
    rw_hi = router_w.astype(BF16)
    rw_lo = (router_w - rw_hi.astype(F32)).astype(BF16)
    zpad = jnp.zeros((d, LANES - 2 * N_EXPERTS), BF16)
    rw = jnp.concatenate([rw_hi, rw_lo, zpad], axis=1)

    w1_b, w3_b, w2_b = moe_w1.astype(BF16), moe_w3.astype(BF16), moe_w2.astype(BF16)
    y = None
    for i in range(DEPTH):
        j = i // 2
        modp = mods[i - 1] if i else None
        if i % 2 == 0:
            lam_init = 0.8 - 0.6 * math.exp(-0.3 * i)
            w = _att_weights(att_w_in[j])
            outs = _project(s, y, modp, mods[i], norm_g[i, 0], w, cos, sin,
                            n_rope=2 * A_HEADS * LANES, n_bf16=w.shape[1])
            if i:
                s = outs[0]
            p = outs[-1]
            oa_x, oa_c = _diff_attention(p, att_lambda[j], att_subln_g[j], lam_init)
            ob_x, ob_c = _neighbourhood_attention(p, _na_bias_table(na_bias[j], seq // GRID_W))
            s, h2, logits = _out_project((oa_x, oa_c, ob_x, ob_c), att_w_out[j].astype(BF16), s, mods[i], norm_g[i, 1],
                                         rw, gla=False)
        else:
            w, n_main = _gla_weights(gla_w_in[j])
            outs = _project(s, y, modp, mods[i], norm_g[i, 0], w, None, None, n_rope=0, n_bf16=n_main)
            s, p, lr = outs
            wg = jnp.zeros((2, LANES, C_HEADS * C_DK), F32)
            wg = wg.at[0, :C_RANK].set(gla_w_gate[j, 0]).at[1, C_RANK:2 * C_RANK].set(gla_w_gate[j, 1])
            o = _gla(p, lr, wg, gla_b_gate[j])
            s, h2, logits = _out_project((o, p), gla_w_out[j].astype(BF16), s, mods[i], norm_g[i, 1],
                                         rw, gla=True, gn=gla_norm_g[j])
        y = _moe(h2.reshape(b * n, ROW_SEGS, LANES), logits.reshape(b * n, LANES), router_bias,
                 i, w1_b, w3_b, w2_b)
        y = y.reshape(b * n * ROW_SEGS, LANES)
    return _final(s, y, mods[DEPTH - 1], final_g)
```

```python
import functools
import math

import jax
import jax.numpy as jnp
import numpy as np
from jax import lax
from jax.experimental import pallas as pl
from jax.experimental.pallas import tpu as pltpu
from jax.experimental.pallas import tpu_sc as plsc

F32 = jnp.float32
BF16 = jnp.bfloat16

D_MODEL = 1024
DEPTH = 4
GRID_W = 64
N_CTX = 256
A_HEADS = 4
A_DH = 64
A_DV = 128
B_HEADS = 8
B_DH = 64
NA_ROWS = 8
NA_COLS = 16
C_HEADS = 4
C_DK = 128
C_DV = 256
C_RANK = 16
C_TAU = 16.0
C_CHUNK = 64
GLA_BLOCK = 256
N_EXPERTS = 16
N_GROUPS = 4
PER_GROUP = 4
D_EXPERT = 512
ROPE_BASE = 10000.0
EPS = 1e-6
LOG2_E = math.log2(math.e)
ATT_KEY_BLOCK = 768
ATT_Q_HALVES = 4

LANES = 128
ROW_TILE = 256
MOE_TILE = 256
SC_CORES = 2
SC_SUBCORES = 16
SC_WORKERS = SC_CORES * SC_SUBCORES
SC_GATHER_ROWS = 64
N_CLASSES = N_GROUPS * 6
NA_QROWS = 16
NA_SUB = 2
NA_KROWS = NA_SUB + NA_ROWS - 1
VMEM_LIMIT = 52 * 1024 * 1024
ROW_SEGS = D_MODEL // LANES // 2

PAIR_LO = (0, 0, 0, 1, 1, 2)
PAIR_HI = (1, 2, 3, 2, 3, 3)


def _cparams(sem):
    return pltpu.CompilerParams(dimension_semantics=sem, vmem_limit_bytes=VMEM_LIMIT)


def _sigmoid(x):
    return 1.0 / (1.0 + jnp.exp(-x))


def _silu(x):
    return x * _sigmoid(x)


def _rms(x, g):
    return x * lax.rsqrt(jnp.mean(x * x, axis=-1, keepdims=True) + EPS) * g


def _dot(a, b):
    return jnp.dot(a, b, preferred_element_type=F32)


def _dot_nt(a, b):
    return lax.dot_general(a, b, (((1,), (1,)), ((), ())), preferred_element_type=F32)


def _dot_tn(a, b):
    return lax.dot_general(a, b, (((0,), (0,)), ((), ())), preferred_element_type=F32)


def _mod_kernel(c_ref, w_ref, b_ref, o_ref):
    a = _silu(c_ref[...]).astype(BF16)
    o_ref[0] = _dot(a, w_ref[0].astype(BF16)) + b_ref[0]


def _mod_vectors(cc, w_mod, b_mod):
    depth, d, n6 = w_mod.shape
    rows = cc.shape[0]
    tn = 1536
    return pl.pallas_call(
        _mod_kernel,
        out_shape=jax.ShapeDtypeStruct((depth, rows, n6), F32),
        grid=(depth, n6 // tn),
        in_specs=[
            pl.BlockSpec((rows, d), lambda i, j: (0, 0)),
            pl.BlockSpec((1, d, tn), lambda i, j: (i, 0, j)),
            pl.BlockSpec((1, 1, tn), lambda i, j: (i, 0, j)),
        ],
        out_specs=pl.BlockSpec((1, rows, tn), lambda i, j: (i, 0, j)),
        compiler_params=_cparams(("arbitrary", "arbitrary")),
        name="mod_vectors",
    )(cc, w_mod, b_mod.reshape(depth, 1, n6))


def _token_rows(ref):
    rows = ref.shape[0] // ROW_SEGS
    packed = jnp.concatenate([ref[pl.ds(sg, rows, stride=ROW_SEGS), :] for sg in range(ROW_SEGS)], axis=-1)
    low = lax.bitcast_convert_type(lax.shift_left(packed, jnp.int32(16)), F32)
    high = lax.bitcast_convert_type(packed & jnp.int32(-65536), F32)
    return jnp.concatenate([low, high], axis=-1)


def _store_token_rows(ref, val):
    rows, d = val.shape
    rounded = val.astype(BF16).astype(F32)
    bits = lax.bitcast_convert_type(rounded, jnp.int32)
    packed = lax.shift_right_logical(bits[:, :d // 2], jnp.int32(16)) | bits[:, d // 2:]
    for sg in range(ROW_SEGS):
        ref[pl.ds(sg, rows, stride=ROW_SEGS), :] = packed[:, sg * LANES:(sg + 1) * LANES]


def _proj_kernel(*refs, has_y, n_rope, n_bf16, col_chunk):
    it = iter(refs)
    s_ref = next(it)
    y_ref = next(it) if has_y else None
    modp_ref = next(it) if has_y else None
    mod_ref = next(it)
    ng_ref = next(it)
    w_ref = next(it)
    cos_ref = next(it) if n_rope else None
    sin_ref = next(it) if n_rope else None
    snew_ref = next(it) if has_y else None
    p_ref = next(it)
    lr_ref = next(it) if w_ref.shape[1] > n_bf16 else None

    x = s_ref[0]
    if has_y:
        x = x + modp_ref[0, 0, 5:6, :] * _token_rows(y_ref)
        snew_ref[0] = x
    h = _rms(x, ng_ref[...])
    h = h * (1.0 + mod_ref[0, 0, 1:2, :]) + mod_ref[0, 0, 0:1, :]
    hb = h.astype(BF16)
    n_out = w_ref.shape[1]
    for c0 in range(0, n_out, col_chunk):
        c1 = min(c0 + col_chunk, n_out)
        acc = _dot(hb, w_ref[:, c0:c1])
        for b0 in range(c0, c1, LANES):
            t = acc[:, b0 - c0:b0 - c0 + LANES]
            if b0 < n_rope:
                t = t * cos_ref[...] + pltpu.roll(t, LANES // 2, 1) * sin_ref[...]
            if b0 < n_bf16:
                p_ref[0, :, b0:b0 + LANES] = t.astype(BF16)
            else:
                lr_ref[0, :, b0 - n_bf16:b0 - n_bf16 + LANES] = t


def _project(s, y, modp, mod, ng, w, cos, sin, *, n_rope, n_bf16):
    b, n, d = s.shape
    n_out = w.shape[1]
    tm = ROW_TILE
    has_y = y is not None
    row_spec = pl.BlockSpec((1, tm, d), lambda i, j: (i, j, 0))
    mod_spec = pl.BlockSpec((1, 1, 6, d), lambda i, j: (i, jnp.minimum(j, 1), 0, 0))
    in_specs = [row_spec]
    args = [s]
    if has_y:
        in_specs += [pl.BlockSpec((tm * ROW_SEGS, LANES), lambda i, j: (i * (n // tm) + j, 0)), mod_spec]
        args += [y, modp]
    in_specs += [mod_spec, pl.BlockSpec((1, d), lambda i, j: (0, 0)),
                 pl.BlockSpec((d, n_out), lambda i, j: (0, 0))]
    args += [mod, ng.reshape(1, d), w]
    if n_rope:
        tab_spec = pl.BlockSpec((tm, LANES), lambda i, j: (j, 0))
        in_specs += [tab_spec, tab_spec]
        args += [cos, sin]
    out_shape, out_specs = [], []
    if has_y:
        out_shape.append(jax.ShapeDtypeStruct((b, n, d), F32))
        out_specs.append(row_spec)
    out_shape.append(jax.ShapeDtypeStruct((b, n, n_bf16), BF16))
    out_specs.append(pl.BlockSpec((1, tm, n_bf16), lambda i, j: (i, j, 0)))
    if n_out > n_bf16:
        out_shape.append(jax.ShapeDtypeStruct((b, n, n_out - n_bf16), F32))
        out_specs.append(pl.BlockSpec((1, tm, n_out - n_bf16), lambda i, j: (i, j, 0)))
    return pl.pallas_call(
        functools.partial(_proj_kernel, has_y=has_y, n_rope=n_rope, n_bf16=n_bf16, col_chunk=512),
        out_shape=out_shape,
        grid=(b, n // tm),
        in_specs=in_specs,
        out_specs=out_specs,
        compiler_params=_cparams(("arbitrary", "arbitrary")),
        name="norm_mod_project",
    )(*args)


def _softmax_rows(s):
    e = jnp.exp(s - jnp.max(s, axis=-1, keepdims=True))
    return e * (1.0 / jnp.sum(e, axis=-1, keepdims=True))


def _diff_attn_kernel(*refs, lam_init, n_q):
    q_refs = refs[:n_q]
    k_ref, v_ref, lam_ref, g_ref, o_ref = refs[n_q:]
    lp = lam_ref[...]
    lam = (jnp.exp(jnp.sum(lp[0:1] * lp[1:2], axis=-1, keepdims=True))
           - jnp.exp(jnp.sum(lp[2:3] * lp[3:4], axis=-1, keepdims=True)) + lam_init)
    q = jnp.concatenate([r[0] for r in q_refs], axis=0)
    lane = lax.broadcasted_iota(jnp.int32, (1, LANES), 1)
    first_map = (lane // 32) % 2 == 0
    zero = jnp.zeros_like(q)
    q0 = jnp.where(first_map, q, zero)
    q1 = jnp.where(first_map, zero, q)
    n_keys = k_ref.shape[1]
    blk = min(ATT_KEY_BLOCK, n_keys)
    ones = jnp.ones((blk, A_DV), BF16)
    state = [None, None]
    for kb in range(n_keys // blk):
        k = k_ref[0, kb * blk:(kb + 1) * blk, :]
        v1 = jnp.concatenate([v_ref[0, kb * blk:(kb + 1) * blk, :], ones], axis=1)
        for i, qm in enumerate((q0, q1)):
            s = _dot_nt(qm, k)
            m_new = jnp.max(s, axis=-1, keepdims=True)
            if kb:
                m_old, acc_old = state[i]
                m_new = jnp.maximum(m_old, m_new)
            acc = _dot(jnp.exp2(s - m_new).astype(BF16), v1)
            if kb:
                acc = acc_old * jnp.exp2(m_old - m_new) + acc
            state[i] = (m_new, acc)
    acc0, acc1 = state[0][1], state[1][1]
    o = acc0[:, :A_DV] * (1.0 / acc0[:, A_DV:]) - acc1[:, :A_DV] * (lam / acc1[:, A_DV:])
    o_ref[0] = (_rms(o, g_ref[...]) * (1.0 - lam_init)).astype(BF16)


def _diff_attention(p, lam_p, subln_g, lam_init):
    b, n, _ = p.shape
    half = ROW_TILE
    tq = ATT_Q_HALVES * half
    ctx_blocks = N_CTX // half
    width = A_HEADS * A_DV
    small = [pl.BlockSpec((4, A_DH), lambda i, h, *_: (0, 0)), pl.BlockSpec((1, A_DV), lambda i, h, *_: (0, 0))]
    q_specs = [pl.BlockSpec((1, half, LANES), functools.partial(lambda i, h, j, part: (i, ctx_blocks + ATT_Q_HALVES * j + part, h),
                                                                 part=part)) for part in range(ATT_Q_HALVES)]
    oa_x = pl.pallas_call(
        functools.partial(_diff_attn_kernel, lam_init=lam_init, n_q=ATT_Q_HALVES),
        out_shape=jax.ShapeDtypeStruct((b, n - N_CTX, width), BF16),
        grid=(b, A_HEADS, (n - N_CTX) // tq),
        in_specs=q_specs + [
            pl.BlockSpec((1, n, LANES), lambda i, h, j: (i, 0, A_HEADS + h)),
            pl.BlockSpec((1, n, LANES), lambda i, h, j: (i, 0, 2 * A_HEADS + h)),
        ] + small,
        out_specs=pl.BlockSpec((1, tq, LANES), lambda i, h, j: (i, j, h)),
        compiler_params=_cparams(("arbitrary", "arbitrary", "arbitrary")),
        name="diff_attention",
    )(*([p] * ATT_Q_HALVES), p, p, lam_p, subln_g.reshape(1, A_DV))
    oa_c = pl.pallas_call(
        functools.partial(_diff_attn_kernel, lam_init=lam_init, n_q=1),
        out_shape=jax.ShapeDtypeStruct((b, N_CTX, width), BF16),
        grid=(b, A_HEADS),
        in_specs=[
            pl.BlockSpec((1, N_CTX, LANES), lambda i, h: (i, 0, h)),
            pl.BlockSpec((1, N_CTX, LANES), lambda i, h: (i, 0, A_HEADS + h)),
            pl.BlockSpec((1, N_CTX, LANES), lambda i, h: (i, 0, 2 * A_HEADS + h)),
        ] + small,
        out_specs=pl.BlockSpec((1, N_CTX, LANES), lambda i, h: (i, 0, h)),
        compiler_params=_cparams(("arbitrary", "arbitrary")),
        name="context_diff_attention",
    )(p, p, p, lam_p, subln_g.reshape(1, A_DV))
    return oa_x, oa_c


def _split_heads(q):
    lane = lax.broadcasted_iota(jnp.int32, (1, LANES), 1)
    first_head = lane < B_DH
    zero = jnp.zeros_like(q)
    return first_head, (jnp.where(first_head, q, zero), jnp.where(first_head, zero, q))


def _na_ctx_kernel(q_ref, k_ref, v_ref, o_ref):
    first_head, qh = _split_heads(q_ref[0])
    kc = k_ref[0]
    vc = v_ref[0]
    outs = [_dot(_softmax_rows(_dot_nt(qh[h], kc)).astype(BF16), vc) for h in range(2)]
    o_ref[0] = jnp.where(first_head, outs[0], outs[1]).astype(BF16)


def _na_kernel(*refs):
    q_refs = refs[:-4]
    k_ref, v_ref, bm_ref, o_ref = refs[-4:]
    blk = pl.program_id(2)
    rows = (k_ref.shape[1] - N_CTX) // GRID_W
    sub_q = NA_SUB * GRID_W
    sub_k = NA_KROWS * GRID_W
    kc = k_ref[0, 0:N_CTX, :]
    vc = jnp.concatenate([v_ref[0, 0:N_CTX, :], jnp.ones((N_CTX, LANES), BF16)], axis=1)
    ones = jnp.ones((sub_k, LANES), BF16)
    for sub in range(NA_QROWS // NA_SUB):
        q_ref = q_refs[sub * sub_q // ROW_TILE]
        q0 = (sub * sub_q) % ROW_TILE
        first_head, qh = _split_heads(q_ref[0, q0:q0 + sub_q, :])
        k_row0 = jnp.clip(blk * NA_QROWS + sub * NA_SUB - NA_ROWS // 2, 0, rows - NA_KROWS)
        start = pl.multiple_of(N_CTX + k_row0 * GRID_W, GRID_W)
        kw = k_ref[0, pl.ds(start, sub_k), :]
        vw = jnp.concatenate([v_ref[0, pl.ds(start, sub_k), :], ones], axis=1)
        outs = []
        for h in range(2):
            s_loc = _dot_nt(qh[h], kw) + bm_ref[h, 0, sub]
            s_ctx = _dot_nt(qh[h], kc)
            m = jnp.maximum(jnp.max(s_loc, axis=-1, keepdims=True), jnp.max(s_ctx, axis=-1, keepdims=True))
            acc = _dot(jnp.exp(s_loc - m).astype(BF16), vw) + _dot(jnp.exp(s_ctx - m).astype(BF16), vc)
            outs.append(acc[:, :LANES] * (1.0 / acc[:, LANES:]))
        o_ref[0, sub * sub_q:(sub + 1) * sub_q, :] = jnp.where(first_head, outs[0], outs[1]).astype(BF16)


def _na_bias_table(na_bias, rows):
    h = na_bias.shape[0]
    n_dr, n_dc = 2 * NA_ROWS - 1, 2 * NA_COLS - 1
    width = 2 * GRID_W
    left = GRID_W - NA_COLS
    u = jnp.pad(na_bias, ((0, 0), (0, 0), (left, width - left - n_dc)))
    skew = jnp.tile(u, (1, 1, GRID_W))[:, :, :GRID_W * (width - 1)].reshape(h, n_dr, GRID_W, width - 1)
    toeplitz = skew[:, :, :, GRID_W - 1:]
    margin = NA_KROWS - NA_ROWS
    by_col = jnp.pad(toeplitz.transpose(0, 2, 1, 3), ((0, 0), (0, 0), (margin, margin), (0, 0)))
    by_col = by_col.reshape(h, GRID_W, (n_dr + 2 * margin) * GRID_W)
    col = np.arange(GRID_W)
    col_start = np.clip(col - NA_COLS // 2, 0, GRID_W - NA_COLS)
    col_ok = (col[None, :] >= col_start[:, None]) & (col[None, :] < col_start[:, None] + NA_COLS)
    n_sub = NA_QROWS // NA_SUB
    blocks = []
    for q_row0 in (0, NA_QROWS, rows - NA_QROWS):
        for sub in range(n_sub):
            k_row0 = int(np.clip(q_row0 + sub * NA_SUB - NA_ROWS // 2, 0, rows - NA_KROWS))
            kr = k_row0 + np.arange(NA_KROWS)
            for rq in range(NA_SUB):
                r = q_row0 + sub * NA_SUB + rq
                r0 = int(np.clip(r - NA_ROWS // 2, 0, rows - NA_ROWS))
                row_ok = (kr >= r0) & (kr < r0 + NA_ROWS)
                ok = (col_ok[:, None, :] & row_ok[None, :, None]).reshape(GRID_W, NA_KROWS * GRID_W)
                first = k_row0 - r + NA_ROWS - 1 + margin
                window = by_col[:, :, first * GRID_W:(first + NA_KROWS) * GRID_W]
                blocks.append(jnp.where(ok[None], window, -jnp.inf))
    table = jnp.stack(blocks, axis=1)
    return table.reshape(h, 3, n_sub, NA_SUB * GRID_W, NA_KROWS * GRID_W)


def _neighbourhood_attention(p, bm):
    b, n, _ = p.shape
    tq = NA_QROWS * GRID_W
    parts = tq // ROW_TILE
    n_blocks = (n - N_CTX) // tq
    ctx_blocks = N_CTX // ROW_TILE
    col0 = 3 * A_HEADS
    pairs = B_HEADS // 2
    width = B_HEADS * B_DH

    def pattern(j):
        return jnp.minimum(j, 1) + (j == n_blocks - 1).astype(jnp.int32)

    ob_x = pl.pallas_call(
        _na_kernel,
        out_shape=jax.ShapeDtypeStruct((b, n - N_CTX, width), BF16),
        grid=(b, pairs, n_blocks),
        in_specs=[pl.BlockSpec((1, ROW_TILE, LANES),
                               functools.partial(lambda i, h, j, part: (i, ctx_blocks + parts * j + part, col0 + h),
                                                 part=part)) for part in range(parts)] + [
            pl.BlockSpec((1, n, LANES), lambda i, h, j: (i, 0, col0 + pairs + h)),
            pl.BlockSpec((1, n, LANES), lambda i, h, j: (i, 0, col0 + 2 * pairs + h)),
            pl.BlockSpec((2, 1, NA_QROWS // NA_SUB, NA_SUB * GRID_W, NA_KROWS * GRID_W),
                         lambda i, h, j: (h, pattern(j), 0, 0, 0)),
        ],
        out_specs=pl.BlockSpec((1, tq, LANES), lambda i, h, j: (i, j, h)),
        compiler_params=_cparams(("arbitrary", "arbitrary", "arbitrary")),
        name="neighbourhood_attention",
    )(*([p] * parts), p, p, bm)
    ob_c = pl.pallas_call(
        _na_ctx_kernel,
        out_shape=jax.ShapeDtypeStruct((b, N_CTX, width), BF16),
        grid=(b, pairs),
        in_specs=[
            pl.BlockSpec((1, N_CTX, LANES), lambda i, h: (i, 0, col0 + h)),
            pl.BlockSpec((1, N_CTX, LANES), lambda i, h: (i, 0, col0 + pairs + h)),
            pl.BlockSpec((1, N_CTX, LANES), lambda i, h: (i, 0, col0 + 2 * pairs + h)),
        ],
        out_specs=pl.BlockSpec((1, N_CTX, LANES), lambda i, h: (i, 0, h)),
        compiler_params=_cparams(("arbitrary", "arbitrary")),
        name="context_attention",
    )(p, p, p)
    return ob_x, ob_c


def _gla_kernel(q_ref, k_ref, v_ref, lr_ref, wg_ref, bg_ref, o_ref, ob_ref, g_ref, sf_ref, sb_ref):
    n = q_ref.shape[1]
    blk = GLA_BLOCK
    per_blk = blk // C_CHUNK
    n_blocks = n // blk
    ri = lax.broadcasted_iota(jnp.int32, (blk, blk), 0)
    ci = lax.broadcasted_iota(jnp.int32, (blk, blk), 1)
    same_chunk = (ri // C_CHUNK) == (ci // C_CHUNK)
    keeps = (same_chunk & (ci <= ri), same_chunk & (ci >= ri))
    tris = tuple(jnp.where(kp, 1.0, 0.0).astype(BF16) for kp in keeps)
    w_gate = jnp.concatenate([wg_ref[0], wg_ref[1]], axis=1).astype(BF16)
    b_gate = jnp.concatenate([bg_ref[0:1, :], bg_ref[1:2, :]], axis=1)
    for i in range(n_blocks):
        z = _dot(lr_ref[0, i * blk:(i + 1) * blk, :].astype(BF16), w_gate) + b_gate
        g_ref[i * blk:(i + 1) * blk, :] = (jnp.minimum(z, 0.0) - jnp.log(1.0 + jnp.exp(-jnp.abs(z)))) * (1.0 / C_TAU)

    def block(sb, direction, st_ref):
        keep = keeps[direction]
        end_row = C_CHUNK - 1 if direction == 0 else 0
        mid_row = C_CHUNK // 2 - 1 if direction == 0 else C_CHUNK // 2
        r0 = sb * blk
        g = g_ref[r0:r0 + blk, direction * C_DK:(direction + 1) * C_DK]
        g_hi = g.astype(BF16)
        g_lo = (g - g_hi.astype(F32)).astype(BF16)
        gc2 = _dot(tris[direction], jnp.concatenate([g_hi, g_lo], axis=1))
        gc = (gc2[:, :C_DK] + gc2[:, C_DK:]).reshape(per_blk, C_CHUNK, C_DK)
        g_end = gc[:, end_row:end_row + 1, :]
        g_mid = gc[:, mid_row:mid_row + 1, :]
        q = (q_ref[0, r0:r0 + blk, :].astype(F32) * (C_DK ** -0.5)).reshape(per_blk, C_CHUNK, C_DK)
        k = k_ref[0, r0:r0 + blk, :].astype(F32).reshape(per_blk, C_CHUNK, C_DK)
        v = v_ref[0, r0:r0 + blk, :]
        q_in = (q * jnp.exp(gc)).astype(BF16).reshape(blk, C_DK)
        q_mid = (q * jnp.exp(gc - g_mid)).astype(BF16).reshape(blk, C_DK)
        k_mid = (k * jnp.exp(g_mid - gc)).astype(BF16).reshape(blk, C_DK)
        k_end = (k * jnp.exp(g_end - gc)).astype(BF16).reshape(blk, C_DK)
        a = jnp.where(keep, _dot_nt(q_mid, k_mid), 0.0)
        o_intra = _dot(a.astype(BF16), v)
        ends = jnp.concatenate([g_end.reshape(per_blk, C_DK), jnp.zeros((8 - per_blk, C_DK), F32)], axis=0)
        decay = jnp.transpose(jnp.exp(ends))
        o_inter = [None] * per_blk
        for c in (range(per_blk) if direction == 0 else reversed(range(per_blk))):
            rows = slice(c * C_CHUNK, (c + 1) * C_CHUNK)
            st = st_ref[...]
            o_inter[c] = _dot(q_in[rows], st.astype(BF16))
            st_ref[...] = st * decay[:, c:c + 1] + _dot_tn(k_end[rows], v[rows])
        return o_intra + jnp.concatenate(o_inter, axis=0)

    sf_ref[...] = jnp.zeros_like(sf_ref)
    sb_ref[...] = jnp.zeros_like(sb_ref)
    ctx_blocks = N_CTX // blk
    order_b = list(reversed(range(ctx_blocks))) + list(reversed(range(ctx_blocks, n_blocks)))
    for i in range(n_blocks):
        o_ref[0, i * blk:(i + 1) * blk, :] = block(i, 0, sf_ref)
        sb = order_b[i]
        ob_ref[sb * blk:(sb + 1) * blk, :] = block(sb, 1, sb_ref)
    o_ref[0] = o_ref[0] + ob_ref[...]


def _gla(p, lr, wg, bg):
    b, n, _ = p.shape
    return pl.pallas_call(
        _gla_kernel,
        out_shape=jax.ShapeDtypeStruct((b, n, C_HEADS * C_DV), F32),
        grid=(b, C_HEADS),
        in_specs=[
            pl.BlockSpec((1, n, C_DK), lambda i, h: (i, 0, h)),
            pl.BlockSpec((1, n, C_DK), lambda i, h: (i, 0, C_HEADS + h)),
            pl.BlockSpec((1, n, C_DV), lambda i, h: (i, 0, C_HEADS + h)),
            pl.BlockSpec((1, n, LANES), lambda i, h: (i, 0, 0)),
            pl.BlockSpec((2, LANES, C_DK), lambda i, h: (0, 0, h)),
            pl.BlockSpec((2, C_DK), lambda i, h: (0, h)),
        ],
        out_specs=pl.BlockSpec((1, n, C_DV), lambda i, h: (i, 0, h)),
        scratch_shapes=[pltpu.VMEM((n, C_DV), F32), pltpu.VMEM((n, 2 * C_DK), F32),
                        pltpu.VMEM((C_DK, C_DV), F32), pltpu.VMEM((C_DK, C_DV), F32)],
        compiler_params=_cparams(("arbitrary", "arbitrary")),
        name="gla",
    )(p, p, p, lr, wg, bg)


def _out_kernel(*refs, gla):
    if gla:
        o_ref, gate_ref, gn_ref, w_ref, s_ref, mod_ref, ng_ref, rw_ref, snew_ref, h2_ref, lg_ref = refs
        o = o_ref[0]
        gate = gate_ref[0].astype(F32)
        parts = []
        for hd in range(C_HEADS):
            oh = _rms(o[:, hd * C_DV:(hd + 1) * C_DV], gn_ref[...])
            parts.append((oh * _silu(gate[:, hd * C_DV:(hd + 1) * C_DV])).astype(BF16))
        acc = _dot(jnp.concatenate(parts, axis=-1), w_ref[...])
    else:
        oax_ref, oac_ref, obx_ref, obc_ref, w_ref, s_ref, mod_ref, ng_ref, rw_ref, snew_ref, h2_ref, lg_ref = refs
        half = oax_ref.shape[2]
        is_ctx = pl.program_id(1) == 0
        oa = jnp.where(is_ctx, oac_ref[0], oax_ref[0])
        ob = jnp.where(is_ctx, obc_ref[0], obx_ref[0])
        acc = _dot(oa, w_ref[0:half, :]) + _dot(ob, w_ref[half:, :])
    x = s_ref[0] + mod_ref[0, 0, 2:3, :] * acc
    snew_ref[0] = x
    h2 = _rms(x, ng_ref[...]) * (1.0 + mod_ref[0, 0, 4:5, :]) + mod_ref[0, 0, 3:4, :]
    _store_token_rows(h2_ref, h2)
    t = _dot(h2.astype(BF16), rw_ref[...])
    lg_ref[0] = t + pltpu.roll(t, LANES - N_EXPERTS, 1)


def _out_project(mix, w_out, s, mod, ng, rw, *, gla, gn=None):
    b, n, d = s.shape
    tm = ROW_TILE
    row_spec = pl.BlockSpec((1, tm, d), lambda i, j: (i, j, 0))
    const2 = lambda i, j: (0, 0)
    if gla:
        o, p = mix
        in_specs = [row_spec, pl.BlockSpec((1, tm, d), lambda i, j: (i, j, 2)),
                    pl.BlockSpec((1, C_DV), const2)]
        args = [o, p, gn.reshape(1, C_DV)]
    else:
        oa_x, oa_c, ob_x, ob_c = mix
        half = oa_x.shape[2]
        latent = pl.BlockSpec((1, tm, half), lambda i, j: (i, jnp.maximum(j - 1, 0), 0))
        context = pl.BlockSpec((1, tm, half), lambda i, j: (i, 0, 0))
        in_specs = [latent, context, latent, context]
        args = [oa_x, oa_c, ob_x, ob_c]
    in_specs += [pl.BlockSpec((d, d), const2), row_spec,
                 pl.BlockSpec((1, 1, 6, d), lambda i, j: (i, jnp.minimum(j, 1), 0, 0)),
                 pl.BlockSpec((1, d), const2), pl.BlockSpec((d, LANES), const2)]
    args += [w_out, s, mod, ng.reshape(1, d), rw]
    return pl.pallas_call(
        functools.partial(_out_kernel, gla=gla),
        out_shape=[jax.ShapeDtypeStruct((b, n, d), F32), jax.ShapeDtypeStruct((b * n * ROW_SEGS, LANES), jnp.int32),
                   jax.ShapeDtypeStruct((b, n, LANES), F32)],
        grid=(b, n // tm),
        in_specs=in_specs,
        out_specs=[row_spec, pl.BlockSpec((tm * ROW_SEGS, LANES), lambda i, j: (i * (n // tm) + j, 0)),
                   pl.BlockSpec((1, tm, LANES), lambda i, j: (i, j, 0))],
        compiler_params=_cparams(("arbitrary", "arbitrary")),
        name="out_project",
    )(*args)


def _route_kernel(bias_ref, lg_ref, cls_ref, wlo_ref, whi_ref):
    score = [_sigmoid(lg_ref[e]) for e in range(N_EXPERTS)]
    sel = [score[e] + bias_ref[e] for e in range(N_EXPERTS)]
    grp_score = []
    for g in range(N_GROUPS):
        v = sel[g * PER_GROUP:(g + 1) * PER_GROUP]
        best = v[0] + v[1]
        for a in range(PER_GROUP):
            for c in range(a + 1, PER_GROUP):
                if (a, c) != (0, 1):
                    best = jnp.maximum(best, v[a] + v[c])
        grp_score.append(best)
    grp = jnp.zeros(grp_score[0].shape, jnp.int32)
    best = grp_score[0]
    for g in range(1, N_GROUPS):
        upd = grp_score[g] > best
        best = jnp.where(upd, grp_score[g], best)
        grp = jnp.where(upd, g, grp)

    def pick(vals, j):
        out = vals[j]
        for g in range(1, N_GROUPS):
            out = jnp.where(grp == g, vals[g * PER_GROUP + j], out)
        return out

    v = [pick(sel, j) for j in range(PER_GROUP)]
    sc = [pick(score, j) for j in range(PER_GROUP)]
    one = jnp.ones(grp.shape, jnp.int32)
    zero = jnp.zeros(grp.shape, jnp.int32)
    chosen = []
    for j in range(PER_GROUP):
        rank = zero
        for m in range(PER_GROUP):
            if m == j:
                continue
            ahead = (v[m] >= v[j]) if m < j else (v[m] > v[j])
            rank = rank + jnp.where(ahead, one, zero)
        chosen.append(rank < 2)
    code = zero
    for j in range(PER_GROUP):
        code = code + jnp.where(chosen[j], one * (1 << j), zero)
    pair = zero
    for idx in range(6):
        pair = jnp.where(code == (1 << PAIR_LO[idx]) + (1 << PAIR_HI[idx]), idx, pair)
    s_lo = jnp.where(chosen[0], sc[0], jnp.where(chosen[1], sc[1], sc[2]))
    s_hi = jnp.where(chosen[3], sc[3], jnp.where(chosen[2], sc[2], sc[1]))
    den = s_lo + s_hi
    cls_ref[...] = grp * 6 + pair
    wlo_ref[...] = s_lo / den
    whi_ref[...] = s_hi / den


def _route(logits_t, router_bias):
    _, r, _ = logits_t.shape
    full = pl.BlockSpec((r, LANES), lambda i: (0, 0))
    return pl.pallas_call(
        _route_kernel,
        out_shape=[jax.ShapeDtypeStruct((r, LANES), jnp.int32), jax.ShapeDtypeStruct((r, LANES), F32),
                   jax.ShapeDtypeStruct((r, LANES), F32)],
        grid=(1,),
        in_specs=[pl.BlockSpec(memory_space=pltpu.SMEM),
                  pl.BlockSpec((N_EXPERTS, r, LANES), lambda i: (0, 0, 0))],
        out_specs=[full, full, full],
        compiler_params=_cparams(("arbitrary",)),
        name="route",
    )(router_bias, logits_t)


def _gather_rows(table, idx):
    p = idx.shape[0]
    per_worker = p // SC_WORKERS
    n_chunks = per_worker // SC_GATHER_ROWS
    mesh = plsc.VectorSubcoreMesh(core_axis_name="c", subcore_axis_name="s", num_cores=SC_CORES,
                                  num_subcores=SC_SUBCORES)

    @functools.partial(
        pl.kernel, mesh=mesh,
        out_type=jax.ShapeDtypeStruct((p,) + table.shape[1:], table.dtype),
        scratch_types=[pltpu.VMEM((per_worker,), jnp.int32),
                       pltpu.VMEM((SC_GATHER_ROWS,) + table.shape[1:], table.dtype),
                       pltpu.SemaphoreType.DMA],
        compiler_params=pltpu.CompilerParams(use_tc_tiling_on_sc=True),
        name="gather_rows",
    )
    def gather(table_hbm, idx_hbm, out_hbm, idx_v, rows_v, sem):
        worker = lax.axis_index("s") * SC_CORES + lax.axis_index("c")
        base = worker * per_worker
        pltpu.sync_copy(idx_hbm.at[pl.ds(base, per_worker)], idx_v)

        @pl.loop(0, n_chunks)
        def _(i):
            off = pl.multiple_of(i * SC_GATHER_ROWS, SC_GATHER_ROWS)
            pltpu.async_copy(table_hbm.at[idx_v.at[pl.ds(off, SC_GATHER_ROWS)]], rows_v, sem).wait()
            pltpu.sync_copy(rows_v, out_hbm.at[pl.ds(base + off, SC_GATHER_ROWS)])

    return gather(table, idx)


def _moe_kernel(lo_ref, hi_ref, nt_ref, x_ref, wr_ref, w1l_ref, w1h_ref, w3l_ref, w3h_ref, w2l_ref, w2h_ref, y_ref):
    j = pl.program_id(0)

    @pl.when(j < nt_ref[0])
    def _():
        x = _token_rows(x_ref).astype(BF16)
        wr = wr_ref[...]
        he_lo = (_silu(_dot(x, w1l_ref[0, 0])) * _dot(x, w3l_ref[0, 0]) * wr[:, 0:1]).astype(BF16)
        he_hi = (_silu(_dot(x, w1h_ref[0, 0])) * _dot(x, w3h_ref[0, 0]) * wr[:, 1:2]).astype(BF16)
        y = _dot(he_lo, w2l_ref[0, 0]) + _dot(he_hi, w2h_ref[0, 0])
        _store_token_rows(y_ref, y)

    @pl.when(j >= nt_ref[0])
    def _():
        y_ref[...] = jnp.zeros_like(y_ref)


def _moe_experts(xs, wrow, tile_lo, tile_hi, n_tiles, layer, w1, w3, w2):
    tp = xs.shape[0] // ROW_SEGS
    tm = MOE_TILE
    d, de = w1.shape[2], w1.shape[3]

    def gate_map(j, lo, hi, nt):
        return (jnp.minimum(j, nt[0] - 1), 0)

    def lo_map(j, lo, hi, nt):
        return (layer, lo[jnp.minimum(j, nt[0] - 1)], 0, 0)

    def hi_map(j, lo, hi, nt):
        return (layer, hi[jnp.minimum(j, nt[0] - 1)], 0, 0)

    up = (1, 1, d, de)
    down = (1, 1, de, d)
    return pl.pallas_call(
        _moe_kernel,
        out_shape=jax.ShapeDtypeStruct((tp * ROW_SEGS, LANES), jnp.int32),
        grid_spec=pltpu.PrefetchScalarGridSpec(
            num_scalar_prefetch=3,
            grid=(tp // tm,),
            in_specs=[pl.BlockSpec((tm * ROW_SEGS, LANES), gate_map), pl.BlockSpec((tm, 2), gate_map),
                      pl.BlockSpec(up, lo_map), pl.BlockSpec(up, hi_map),
                      pl.BlockSpec(up, lo_map), pl.BlockSpec(up, hi_map),
                      pl.BlockSpec(down, lo_map), pl.BlockSpec(down, hi_map)],
            out_specs=pl.BlockSpec((tm * ROW_SEGS, LANES), lambda j, lo, hi, nt: (j, 0)),
        ),
        compiler_params=_cparams(("arbitrary",)),
        name="moe_experts",
    )(tile_lo, tile_hi, n_tiles, xs, wrow, w1, w1, w3, w3, w2, w2)


def _moe(h2, logits, router_bias, layer, w1, w3, w2):
    t = h2.shape[0]
    tm = MOE_TILE
    n_tiles_max = t // tm + N_CLASSES
    tp = n_tiles_max * tm
    lg_t = logits[:, :N_EXPERTS].T.reshape(N_EXPERTS, t // LANES, LANES)
    cls, wlo, whi = _route(lg_t, router_bias)
    cls, wlo, whi = cls.reshape(t), wlo.reshape(t), whi.reshape(t)
    classes = jnp.arange(N_CLASSES, dtype=jnp.int32)
    onehot = (cls[:, None] == classes[None, :]).astype(jnp.int32)
    csum = jnp.cumsum(onehot, axis=0)
    rank = jnp.sum(csum * onehot, axis=1) - 1
    counts = csum[-1]
    tiles_per = (counts + tm - 1) // tm
    tile_end = jnp.cumsum(tiles_per)
    tile_start = tile_end - tiles_per
    dest = jnp.sum(onehot * (tile_start * tm)[None, :], axis=1) + rank
    tiles = jnp.arange(n_tiles_max, dtype=jnp.int32)
    tile_cls = jnp.minimum(jnp.sum((tile_end[None, :] <= tiles[:, None]).astype(jnp.int32), axis=1), N_CLASSES - 1)
    pair = tile_cls % 6
    base = (tile_cls // 6) * PER_GROUP
    pair_onehot = (pair[:, None] == jnp.arange(6, dtype=jnp.int32)[None, :]).astype(jnp.int32)
    tile_lo = base + jnp.sum(pair_onehot * jnp.asarray(PAIR_LO, jnp.int32)[None, :], axis=1)
    tile_hi = base + jnp.sum(pair_onehot * jnp.asarray(PAIR_HI, jnp.int32)[None, :], axis=1)
    per_token = jnp.stack([jnp.arange(t, dtype=jnp.int32), lax.bitcast_convert_type(wlo, jnp.int32),
                           lax.bitcast_convert_type(whi, jnp.int32)], axis=1)
    padding = jnp.stack([jnp.arange(tp, dtype=jnp.int32) % t, jnp.zeros((tp,), jnp.int32),
                         jnp.zeros((tp,), jnp.int32)], axis=1)
    per_row = padding.at[dest].set(per_token, unique_indices=True)
    tok = per_row[:, 0]
    wrow = lax.bitcast_convert_type(per_row[:, 1:3], F32)
    xs = _gather_rows(h2, tok).reshape(tp * ROW_SEGS, LANES)
    ys = _moe_experts(xs, wrow, tile_lo, tile_hi, tile_end[-1:], layer, w1, w3, w2)
    return _gather_rows(ys.reshape(tp, ROW_SEGS, LANES), dest)


def _final_kernel(s_ref, y_ref, mod_ref, g_ref, o_ref):
    x = s_ref[0] + mod_ref[0, 0, 5:6, :] * _token_rows(y_ref)
    o_ref[0] = _rms(x, g_ref[...])


def _final(s, y, mod, final_g):
    b, n, d = s.shape
    tm = ROW_TILE
    skip = N_CTX // tm
    row_spec = pl.BlockSpec((1, tm, d), lambda i, j: (i, j + skip, 0))
    return pl.pallas_call(
        _final_kernel,
        out_shape=jax.ShapeDtypeStruct((b, n - N_CTX, d), F32),
        grid=(b, (n - N_CTX) // tm),
        in_specs=[row_spec, pl.BlockSpec((tm * ROW_SEGS, LANES), lambda i, j: (i * (n // tm) + j + skip, 0)),
                  pl.BlockSpec((1, 1, 6, d), lambda i, j: (i, 1, 0, 0)),
                  pl.BlockSpec((1, d), lambda i, j: (0, 0))],
        out_specs=pl.BlockSpec((1, tm, d), lambda i, j: (i, j, 0)),
        compiler_params=_cparams(("arbitrary", "arbitrary")),
        name="final_norm",
    )(s, y, mod, final_g.reshape(1, d))


def _rope_tables(seq):
    pos = jnp.arange(seq)
    row_pos, col_pos = pos // GRID_W, pos % GRID_W
    n = A_DH // 2
    inv = ROPE_BASE ** (-jnp.arange(0, n, 2, dtype=F32) / n)
    ang_r = row_pos.astype(F32)[:, None] * inv[None, :]
    ang_c = col_pos.astype(F32)[:, None] * inv[None, :]
    ang = jnp.concatenate([ang_r, ang_c], axis=-1)
    cos = jnp.tile(jnp.cos(ang), (1, 4))
    sin = jnp.tile(jnp.sin(ang), (1, 4))
    sin = jnp.concatenate([-sin[:, :LANES // 2], sin[:, LANES // 2:]], axis=-1)
    cos = jnp.concatenate([jnp.ones((N_CTX, LANES), F32), cos], axis=0)
    sin = jnp.concatenate([jnp.zeros((N_CTX, LANES), F32), sin], axis=0)
    return cos, sin


def _interleave_maps(w):
    d = w.shape[0]
    w = w.reshape(d, A_HEADS, 2, 2, 2, A_DH // 4)
    return w.transpose(0, 1, 4, 2, 3, 5).reshape(d, A_HEADS * LANES)


def _att_weights(w_in):
    a_qk = A_HEADS * 2 * A_DH
    a_v = A_HEADS * A_DV
    b_w = B_HEADS * B_DH
    qa = _interleave_maps(w_in[:, :a_qk]) * (A_DH ** -0.5 * LOG2_E)
    ka = _interleave_maps(w_in[:, a_qk:2 * a_qk])
    va = w_in[:, 2 * a_qk:2 * a_qk + a_v]
    o = 2 * a_qk + a_v
    qb = w_in[:, o:o + b_w] * (B_DH ** -0.5)
    rest = w_in[:, o + b_w:]
    return jnp.concatenate([qa, ka, va, qb, rest], axis=1).astype(BF16)


def _gla_weights(w_in):
    d = w_in.shape[0]
    n_main = 2 * C_HEADS * C_DK + 2 * C_HEADS * C_DV
    pad = jnp.zeros((d, LANES - 2 * C_RANK), w_in.dtype)
    return jnp.concatenate([w_in, pad], axis=1).astype(BF16), n_main


def kernel(x, c, ctx, c_ctx, w_mod, b_mod, norm_g, final_g, att_w_in, att_w_out, att_lambda, att_subln_g, na_bias,
           gla_w_in, gla_w_gate, gla_b_gate, gla_norm_g, gla_w_out, router_w, router_bias, moe_w1, moe_w3, moe_w2):
    b, seq, d = x.shape
    n = N_CTX + seq
    s = jnp.concatenate([ctx, x], axis=1)

    rows = b + 1
    rows_pad = -(-rows // 8) * 8
    cc = jnp.concatenate([c, c_ctx[None, :], jnp.zeros((rows_pad - rows, d), F32)], axis=0)
    mod_all = _mod_vectors(cc, w_mod, b_mod)
    mod_x = mod_all[:, :b].reshape(DEPTH, b, 1, 6, d)
    mod_c = jnp.broadcast_to(mod_all[:, b].reshape(DEPTH, 1, 1, 6, d), (DEPTH, b, 1, 6, d))
    mods = jnp.concatenate([mod_c, mod_x], axis=2)

    cos, sin = _rope_tables(seq)
    rw_hi = router_w.astype(BF16)
    rw_lo = (router_w - rw_hi.astype(F32)).astype(BF16)
    zpad = jnp.zeros((d, LANES - 2 * N_EXPERTS), BF16)
    rw = jnp.concatenate([rw_hi, rw_lo, zpad], axis=1)

    w1_b, w3_b, w2_b = moe_w1.astype(BF16), moe_w3.astype(BF16), moe_w2.astype(BF16)
    y = None
    for i in range(DEPTH):
        j = i // 2
        modp = mods[i - 1] if i else None
        if i % 2 == 0:
            lam_init = 0.8 - 0.6 * math.exp(-0.3 * i)
            w = _att_weights(att_w_in[j])
            outs = _project(s, y, modp, mods[i], norm_g[i, 0], w, cos, sin,
                            n_rope=2 * A_HEADS * LANES, n_bf16=w.shape[1])
            if i:
                s = outs[0]
            p = outs[-1]
            oa_x, oa_c = _diff_attention(p, att_lambda[j], att_subln_g[j], lam_init)
            ob_x, ob_c = _neighbourhood_attention(p, _na_bias_table(na_bias[j], seq // GRID_W))
            s, h2, logits = _out_project((oa_x, oa_c, ob_x, ob_c), att_w_out[j].astype(BF16), s, mods[i], norm_g[i, 1],
                                         rw, gla=False)
        else:
            w, n_main = _gla_weights(gla_w_in[j])
            outs = _project(s, y, modp, mods[i], norm_g[i, 0], w, None, None, n_rope=0, n_bf16=n_main)
            s, p, lr = outs
            wg = jnp.zeros((2, LANES, C_HEADS * C_DK), F32)
            wg = wg.at[0, :C_RANK].set(gla_w_gate[j, 0]).at[1, C_RANK:2 * C_RANK].set(gla_w_gate[j, 1])
            o = _gla(p, lr, wg, gla_b_gate[j])
            s, h2, logits = _out_project((o, p), gla_w_out[j].astype(BF16), s, mods[i], norm_g[i, 1],
                                         rw, gla=True, gn=gla_norm_g[j])
        y = _moe(h2.reshape(b * n, ROW_SEGS, LANES), logits.reshape(b * n, LANES), router_bias,
                 i, w1_b, w3_b, w2_b)
        y = y.reshape(b * n * ROW_SEGS, LANES)
    return _final(s, y, mods[DEPTH - 1], final_g)
```

```python
import functools
import math

import jax
import jax.numpy as jnp
import numpy as np
from jax import lax
from jax.experimental import pallas as pl
from jax.experimental.pallas import tpu as pltpu
from jax.experimental.pallas import tpu_sc as plsc

F32 = jnp.float32
BF16 = jnp.bfloat16

D_MODEL = 1024
DEPTH = 4
GRID_W = 64
N_CTX = 256
A_HEADS = 4
A_DH = 64
A_DV = 128
B_HEADS = 8
B_DH = 64
NA_ROWS = 8
NA_COLS = 16
C_HEADS = 4
C_DK = 128
C_DV = 256
C_RANK = 16
C_TAU = 16.0
C_CHUNK = 64
GLA_BLOCK = 256
N_EXPERTS = 16
N_GROUPS = 4
PER_GROUP = 4
ROPE_BASE = 10000.0
EPS = 1e-6
LOG2_E = math.log2(math.e)
ATT_KEY_BLOCK = 768
ATT_Q_HALVES = 4

LANES = 128
ROW_TILE = 256
MOE_TILE = 256
SC_CORES = 2
SC_SUBCORES = 16
SC_WORKERS = SC_CORES * SC_SUBCORES
SC_GATHER_ROWS = 64
N_CLASSES = N_GROUPS * 6
NA_QROWS = 16
NA_SUB = 2
NA_KROWS = NA_SUB + NA_ROWS - 1
VMEM_LIMIT = 52 * 1024 * 1024
ROW_SEGS = D_MODEL // LANES // 2

PAIR_LO = (0, 0, 0, 1, 1, 2)
PAIR_HI = (1, 2, 3, 2, 3, 3)


def _cparams(sem):
    return pltpu.CompilerParams(dimension_semantics=sem, vmem_limit_bytes=VMEM_LIMIT)


def _sigmoid(x):
    return 1.0 / (1.0 + jnp.exp(-x))


def _silu(x):
    return x * _sigmoid(x)


def _rms(x, g):
    return x * lax.rsqrt(jnp.mean(x * x, axis=-1, keepdims=True) + EPS) * g


def _dot(a, b):
    return jnp.dot(a, b, preferred_element_type=F32)


def _dot_nt(a, b):
    return lax.dot_general(a, b, (((1,), (1,)), ((), ())), preferred_element_type=F32)


def _dot_tn(a, b):
    return lax.dot_general(a, b, (((0,), (0,)), ((), ())), preferred_element_type=F32)


def _mod_kernel(c_ref, w_ref, b_ref, o_ref):
    a = _silu(c_ref[...]).astype(BF16)
    o_ref[0] = _dot(a, w_ref[0].astype(BF16)) + b_ref[0]


def _mod_vectors(cc, w_mod, b_mod):
    depth, d, n6 = w_mod.shape
    rows = cc.shape[0]
    tn = 1536
    return pl.pallas_call(
        _mod_kernel,
        out_shape=jax.ShapeDtypeStruct((depth, rows, n6), F32),
        grid=(depth, n6 // tn),
        in_specs=[
            pl.BlockSpec((rows, d), lambda i, j: (0, 0)),
            pl.BlockSpec((1, d, tn), lambda i, j: (i, 0, j)),
            pl.BlockSpec((1, 1, tn), lambda i, j: (i, 0, j)),
        ],
        out_specs=pl.BlockSpec((1, rows, tn), lambda i, j: (i, 0, j)),
        compiler_params=_cparams(("arbitrary", "arbitrary")),
        name="mod_vectors",
    )(cc, w_mod, b_mod.reshape(depth, 1, n6))


def _token_rows(ref):
    rows = ref.shape[0] // ROW_SEGS
    packed = jnp.concatenate([ref[pl.ds(sg, rows, stride=ROW_SEGS), :] for sg in range(ROW_SEGS)], axis=-1)
    low = lax.bitcast_convert_type(lax.shift_left(packed, jnp.int32(16)), F32)
    high = lax.bitcast_convert_type(packed & jnp.int32(-65536), F32)
    return jnp.concatenate([low, high], axis=-1)


def _store_token_rows(ref, val):
    rows, d = val.shape
    rounded = val.astype(BF16).astype(F32)
    bits = lax.bitcast_convert_type(rounded, jnp.int32)
    packed = lax.shift_right_logical(bits[:, :d // 2], jnp.int32(16)) | bits[:, d // 2:]
    for sg in range(ROW_SEGS):
        ref[pl.ds(sg, rows, stride=ROW_SEGS), :] = packed[:, sg * LANES:(sg + 1) * LANES]


def _proj_kernel(*refs, has_y, n_rope, n_bf16, col_chunk):
    it = iter(refs)
    s_ref = next(it)
    y_ref = next(it) if has_y else None
    modp_ref = next(it) if has_y else None
    mod_ref = next(it)
    ng_ref = next(it)
    w_ref = next(it)
    cos_ref = next(it) if n_rope else None
    sin_ref = next(it) if n_rope else None
    snew_ref = next(it) if has_y else None
    p_ref = next(it)
    lr_ref = next(it) if w_ref.shape[1] > n_bf16 else None

    x = s_ref[0]
    if has_y:
        x = x + modp_ref[0, 0, 5:6, :] * _token_rows(y_ref)
        snew_ref[0] = x
    h = _rms(x, ng_ref[...])
    h = h * (1.0 + mod_ref[0, 0, 1:2, :]) + mod_ref[0, 0, 0:1, :]
    hb = h.astype(BF16)
    n_out = w_ref.shape[1]
    for c0 in range(0, n_out, col_chunk):
        c1 = min(c0 + col_chunk, n_out)
        acc = _dot(hb, w_ref[:, c0:c1])
        for b0 in range(c0, c1, LANES):
            t = acc[:, b0 - c0:b0 - c0 + LANES]
            if b0 < n_rope:
                t = t * cos_ref[...] + pltpu.roll(t, LANES // 2, 1) * sin_ref[...]
            if b0 < n_bf16:
                p_ref[0, :, b0:b0 + LANES] = t.astype(BF16)
            else:
                lr_ref[0, :, b0 - n_bf16:b0 - n_bf16 + LANES] = t


def _project(s, y, modp, mod, ng, w, cos, sin, *, n_rope, n_bf16):
    b, n, d = s.shape
    n_out = w.shape[1]
    tm = ROW_TILE
    has_y = y is not None
    row_spec = pl.BlockSpec((1, tm, d), lambda i, j: (i, j, 0))
    mod_spec = pl.BlockSpec((1, 1, 6, d), lambda i, j: (i, jnp.minimum(j, 1), 0, 0))
    in_specs = [row_spec]
    args = [s]
    if has_y:
        in_specs += [pl.BlockSpec((tm * ROW_SEGS, LANES), lambda i, j: (i * (n // tm) + j, 0)), mod_spec]
        args += [y, modp]
    in_specs += [mod_spec, pl.BlockSpec((1, d), lambda i, j: (0, 0)),
                 pl.BlockSpec((d, n_out), lambda i, j: (0, 0))]
    args += [mod, ng.reshape(1, d), w]
    if n_rope:
        tab_spec = pl.BlockSpec((tm, LANES), lambda i, j: (j, 0))
        in_specs += [tab_spec, tab_spec]
        args += [cos, sin]
    out_shape, out_specs = [], []
    if has_y:
        out_shape.append(jax.ShapeDtypeStruct((b, n, d), F32))
        out_specs.append(row_spec)
    out_shape.append(jax.ShapeDtypeStruct((b, n, n_bf16), BF16))
    out_specs.append(pl.BlockSpec((1, tm, n_bf16), lambda i, j: (i, j, 0)))
    if n_out > n_bf16:
        out_shape.append(jax.ShapeDtypeStruct((b, n, n_out - n_bf16), F32))
        out_specs.append(pl.BlockSpec((1, tm, n_out - n_bf16), lambda i, j: (i, j, 0)))
    return pl.pallas_call(
        functools.partial(_proj_kernel, has_y=has_y, n_rope=n_rope, n_bf16=n_bf16, col_chunk=512),
        out_shape=out_shape,
        grid=(b, n // tm),
        in_specs=in_specs,
        out_specs=out_specs,
        compiler_params=_cparams(("arbitrary", "arbitrary")),
        name="norm_mod_project",
    )(*args)


def _softmax_rows(s):
    e = jnp.exp(s - jnp.max(s, axis=-1, keepdims=True))
    return e * (1.0 / jnp.sum(e, axis=-1, keepdims=True))


def _diff_attn_kernel(*refs, lam_init, n_q):
    q_refs = refs[:n_q]
    k_ref, v_ref, lam_ref, g_ref, o_ref = refs[n_q:]
    lp = lam_ref[...]
    lam = (jnp.exp(jnp.sum(lp[0:1] * lp[1:2], axis=-1, keepdims=True))
           - jnp.exp(jnp.sum(lp[2:3] * lp[3:4], axis=-1, keepdims=True)) + lam_init)
    q = jnp.concatenate([r[0] for r in q_refs], axis=0)
    lane = lax.broadcasted_iota(jnp.int32, (1, LANES), 1)
    first_map = (lane // 32) % 2 == 0
    zero = jnp.zeros_like(q)
    q0 = jnp.where(first_map, q, zero)
    q1 = jnp.where(first_map, zero, q)
    n_keys = k_ref.shape[1]
    blk = min(ATT_KEY_BLOCK, n_keys)
    ones = jnp.ones((blk, A_DV), BF16)
    state = [None, None]
    for kb in range(n_keys // blk):
        k = k_ref[0, kb * blk:(kb + 1) * blk, :]
        v1 = jnp.concatenate([v_ref[0, kb * blk:(kb + 1) * blk, :], ones], axis=1)
        for i, qm in enumerate((q0, q1)):
            s = _dot_nt(qm, k)
            m_new = jnp.max(s, axis=-1, keepdims=True)
            if kb:
                m_old, acc_old = state[i]
                m_new = jnp.maximum(m_old, m_new)
            acc = _dot(jnp.exp2(s - m_new).astype(BF16), v1)
            if kb:
                acc = acc_old * jnp.exp2(m_old - m_new) + acc
            state[i] = (m_new, acc)
    acc0, acc1 = state[0][1], state[1][1]
    o = acc0[:, :A_DV] * (1.0 / acc0[:, A_DV:]) - acc1[:, :A_DV] * (lam / acc1[:, A_DV:])
    o_ref[0] = (_rms(o, g_ref[...]) * (1.0 - lam_init)).astype(BF16)


def _diff_attention(p, lam_p, subln_g, lam_init):
    b, n, _ = p.shape
    half = ROW_TILE
    tq = ATT_Q_HALVES * half
    ctx_blocks = N_CTX // half
    width = A_HEADS * A_DV
    small = [pl.BlockSpec((4, A_DH), lambda i, h, *_: (0, 0)), pl.BlockSpec((1, A_DV), lambda i, h, *_: (0, 0))]
    q_specs = [pl.BlockSpec((1, half, LANES), functools.partial(lambda i, h, j, part: (i, ctx_blocks + ATT_Q_HALVES * j + part, h),
                                                                 part=part)) for part in range(ATT_Q_HALVES)]
    oa_x = pl.pallas_call(
        functools.partial(_diff_attn_kernel, lam_init=lam_init, n_q=ATT_Q_HALVES),
        out_shape=jax.ShapeDtypeStruct((b, n - N_CTX, width), BF16),
        grid=(b, A_HEADS, (n - N_CTX) // tq),
        in_specs=q_specs + [
            pl.BlockSpec((1, n, LANES), lambda i, h, j: (i, 0, A_HEADS + h)),
            pl.BlockSpec((1, n, LANES), lambda i, h, j: (i, 0, 2 * A_HEADS + h)),
        ] + small,
        out_specs=pl.BlockSpec((1, tq, LANES), lambda i, h, j: (i, j, h)),
        compiler_params=_cparams(("arbitrary", "arbitrary", "arbitrary")),
        name="diff_attention",
    )(*([p] * ATT_Q_HALVES), p, p, lam_p, subln_g.reshape(1, A_DV))
    oa_c = pl.pallas_call(
        functools.partial(_diff_attn_kernel, lam_init=lam_init, n_q=1),
        out_shape=jax.ShapeDtypeStruct((b, N_CTX, width), BF16),
        grid=(b, A_HEADS),
        in_specs=[
            pl.BlockSpec((1, N_CTX, LANES), lambda i, h: (i, 0, h)),
            pl.BlockSpec((1, N_CTX, LANES), lambda i, h: (i, 0, A_HEADS + h)),
            pl.BlockSpec((1, N_CTX, LANES), lambda i, h: (i, 0, 2 * A_HEADS + h)),
        ] + small,
        out_specs=pl.BlockSpec((1, N_CTX, LANES), lambda i, h: (i, 0, h)),
        compiler_params=_cparams(("arbitrary", "arbitrary")),
        name="context_diff_attention",
    )(p, p, p, lam_p, subln_g.reshape(1, A_DV))
    return oa_x, oa_c


def _split_heads(q):
    lane = lax.broadcasted_iota(jnp.int32, (1, LANES), 1)
    first_head = lane < B_DH
    zero = jnp.zeros_like(q)
    return first_head, (jnp.where(first_head, q, zero), jnp.where(first_head, zero, q))


def _na_ctx_kernel(q_ref, k_ref, v_ref, o_ref):
    first_head, qh = _split_heads(q_ref[0])
    kc = k_ref[0]
    vc = v_ref[0]
    outs = [_dot(_softmax_rows(_dot_nt(qh[h], kc)).astype(BF16), vc) for h in range(2)]
    o_ref[0] = jnp.where(first_head, outs[0], outs[1]).astype(BF16)


def _na_kernel(*refs):
    q_refs = refs[:-4]
    k_ref, v_ref, bm_ref, o_ref = refs[-4:]
    blk = pl.program_id(2)
    rows = (k_ref.shape[1] - N_CTX) // GRID_W
    sub_q = NA_SUB * GRID_W
    sub_k = NA_KROWS * GRID_W
    kc = k_ref[0, 0:N_CTX, :]
    vc = jnp.concatenate([v_ref[0, 0:N_CTX, :], jnp.ones((N_CTX, LANES), BF16)], axis=1)
    ones = jnp.ones((sub_k, LANES), BF16)
    for sub in range(NA_QROWS // NA_SUB):
        q_ref = q_refs[sub * sub_q // ROW_TILE]
        q0 = (sub * sub_q) % ROW_TILE
        first_head, qh = _split_heads(q_ref[0, q0:q0 + sub_q, :])
        k_row0 = jnp.clip(blk * NA_QROWS + sub * NA_SUB - NA_ROWS // 2, 0, rows - NA_KROWS)
        start = pl.multiple_of(N_CTX + k_row0 * GRID_W, GRID_W)
        kw = k_ref[0, pl.ds(start, sub_k), :]
        vw = jnp.concatenate([v_ref[0, pl.ds(start, sub_k), :], ones], axis=1)
        outs = []
        for h in range(2):
            s_loc = _dot_nt(qh[h], kw) + bm_ref[h, 0, sub]
            s_ctx = _dot_nt(qh[h], kc)
            m = jnp.maximum(jnp.max(s_loc, axis=-1, keepdims=True), jnp.max(s_ctx, axis=-1, keepdims=True))
            acc = _dot(jnp.exp(s_loc - m).astype(BF16), vw) + _dot(jnp.exp(s_ctx - m).astype(BF16), vc)
            outs.append(acc[:, :LANES] * (1.0 / acc[:, LANES:]))
        o_ref[0, sub * sub_q:(sub + 1) * sub_q, :] = jnp.where(first_head, outs[0], outs[1]).astype(BF16)


def _na_bias_table(na_bias, rows):
    h = na_bias.shape[0]
    n_dr, n_dc = 2 * NA_ROWS - 1, 2 * NA_COLS - 1
    width = 2 * GRID_W
    left = GRID_W - NA_COLS
    u = jnp.pad(na_bias, ((0, 0), (0, 0), (left, width - left - n_dc)))
    skew = jnp.tile(u, (1, 1, GRID_W))[:, :, :GRID_W * (width - 1)].reshape(h, n_dr, GRID_W, width - 1)
    toeplitz = skew[:, :, :, GRID_W - 1:]
    margin = NA_KROWS - NA_ROWS
    by_col = jnp.pad(toeplitz.transpose(0, 2, 1, 3), ((0, 0), (0, 0), (margin, margin), (0, 0)))
    by_col = by_col.reshape(h, GRID_W, (n_dr + 2 * margin) * GRID_W)
    col = np.arange(GRID_W)
    col_start = np.clip(col - NA_COLS // 2, 0, GRID_W - NA_COLS)
    col_ok = (col[None, :] >= col_start[:, None]) & (col[None, :] < col_start[:, None] + NA_COLS)
    n_sub = NA_QROWS // NA_SUB
    blocks = []
    for q_row0 in (0, NA_QROWS, rows - NA_QROWS):
        for sub in range(n_sub):
            k_row0 = int(np.clip(q_row0 + sub * NA_SUB - NA_ROWS // 2, 0, rows - NA_KROWS))
            kr = k_row0 + np.arange(NA_KROWS)
            for rq in range(NA_SUB):
                r = q_row0 + sub * NA_SUB + rq
                r0 = int(np.clip(r - NA_ROWS // 2, 0, rows - NA_ROWS))
                row_ok = (kr >= r0) & (kr < r0 + NA_ROWS)
                ok = (col_ok[:, None, :] & row_ok[None, :, None]).reshape(GRID_W, NA_KROWS * GRID_W)
                first = k_row0 - r + NA_ROWS - 1 + margin
                window = by_col[:, :, first * GRID_W:(first + NA_KROWS) * GRID_W]
                blocks.append(jnp.where(ok[None], window, -jnp.inf))
    table = jnp.stack(blocks, axis=1)
    return table.reshape(h, 3, n_sub, NA_SUB * GRID_W, NA_KROWS * GRID_W)


def _neighbourhood_attention(p, bm):
    b, n, _ = p.shape
    tq = NA_QROWS * GRID_W
    parts = tq // ROW_TILE
    n_blocks = (n - N_CTX) // tq
    ctx_blocks = N_CTX // ROW_TILE
    col0 = 3 * A_HEADS
    pairs = B_HEADS // 2
    width = B_HEADS * B_DH

    def pattern(j):
        return jnp.minimum(j, 1) + (j == n_blocks - 1).astype(jnp.int32)

    ob_x = pl.pallas_call(
        _na_kernel,
        out_shape=jax.ShapeDtypeStruct((b, n - N_CTX, width), BF16),
        grid=(b, pairs, n_blocks),
        in_specs=[pl.BlockSpec((1, ROW_TILE, LANES),
                               functools.partial(lambda i, h, j, part: (i, ctx_blocks + parts * j + part, col0 + h),
                                                 part=part)) for part in range(parts)] + [
            pl.BlockSpec((1, n, LANES), lambda i, h, j: (i, 0, col0 + pairs + h)),
            pl.BlockSpec((1, n, LANES), lambda i, h, j: (i, 0, col0 + 2 * pairs + h)),
            pl.BlockSpec((2, 1, NA_QROWS // NA_SUB, NA_SUB * GRID_W, NA_KROWS * GRID_W),
                         lambda i, h, j: (h, pattern(j), 0, 0, 0)),
        ],
        out_specs=pl.BlockSpec((1, tq, LANES), lambda i, h, j: (i, j, h)),
        compiler_params=_cparams(("arbitrary", "arbitrary", "arbitrary")),
        name="neighbourhood_attention",
    )(*([p] * parts), p, p, bm)
    ob_c = pl.pallas_call(
        _na_ctx_kernel,
        out_shape=jax.ShapeDtypeStruct((b, N_CTX, width), BF16),
        grid=(b, pairs),
        in_specs=[
            pl.BlockSpec((1, N_CTX, LANES), lambda i, h: (i, 0, col0 + h)),
            pl.BlockSpec((1, N_CTX, LANES), lambda i, h: (i, 0, col0 + pairs + h)),
            pl.BlockSpec((1, N_CTX, LANES), lambda i, h: (i, 0, col0 + 2 * pairs + h)),
        ],
        out_specs=pl.BlockSpec((1, N_CTX, LANES), lambda i, h: (i, 0, h)),
        compiler_params=_cparams(("arbitrary", "arbitrary")),
        name="context_attention",
    )(p, p, p)
    return ob_x, ob_c


def _gla_kernel(q_ref, k_ref, v_ref, lr_ref, wg_ref, bg_ref, o_ref, ob_ref, g_ref, sf_ref, sb_ref):
    n = q_ref.shape[1]
    blk = GLA_BLOCK
    per_blk = blk // C_CHUNK
    n_blocks = n // blk
    ri = lax.broadcasted_iota(jnp.int32, (blk, blk), 0)
    ci = lax.broadcasted_iota(jnp.int32, (blk, blk), 1)
    same_chunk = (ri // C_CHUNK) == (ci // C_CHUNK)
    keeps = (same_chunk & (ci <= ri), same_chunk & (ci >= ri))
    tris = tuple(jnp.where(kp, 1.0, 0.0).astype(BF16) for kp in keeps)
    w_gate = jnp.concatenate([wg_ref[0], wg_ref[1]], axis=1).astype(BF16)
    b_gate = jnp.concatenate([bg_ref[0:1, :], bg_ref[1:2, :]], axis=1)
    for i in range(n_blocks):
        z = _dot(lr_ref[0, i * blk:(i + 1) * blk, :].astype(BF16), w_gate) + b_gate
        g_ref[i * blk:(i + 1) * blk, :] = (jnp.minimum(z, 0.0) - jnp.log(1.0 + jnp.exp(-jnp.abs(z)))) * (1.0 / C_TAU)

    def block(sb, direction, st_ref):
        keep = keeps[direction]
        end_row = C_CHUNK - 1 if direction == 0 else 0
        mid_row = C_CHUNK // 2 - 1 if direction == 0 else C_CHUNK // 2
        r0 = sb * blk
        g = g_ref[r0:r0 + blk, direction * C_DK:(direction + 1) * C_DK]
        g_hi = g.astype(BF16)
        g_lo = (g - g_hi.astype(F32)).astype(BF16)
        gc2 = _dot(tris[direction], jnp.concatenate([g_hi, g_lo], axis=1))
        gc = (gc2[:, :C_DK] + gc2[:, C_DK:]).reshape(per_blk, C_CHUNK, C_DK)
        g_end = gc[:, end_row:end_row + 1, :]
        g_mid = gc[:, mid_row:mid_row + 1, :]
        q = (q_ref[0, r0:r0 + blk, :].astype(F32) * (C_DK ** -0.5)).reshape(per_blk, C_CHUNK, C_DK)
        k = k_ref[0, r0:r0 + blk, :].astype(F32).reshape(per_blk, C_CHUNK, C_DK)
        v = v_ref[0, r0:r0 + blk, :]
        q_in = (q * jnp.exp(gc)).astype(BF16).reshape(blk, C_DK)
        q_mid = (q * jnp.exp(gc - g_mid)).astype(BF16).reshape(blk, C_DK)
        k_mid = (k * jnp.exp(g_mid - gc)).astype(BF16).reshape(blk, C_DK)
        k_end = (k * jnp.exp(g_end - gc)).astype(BF16).reshape(blk, C_DK)
        a = jnp.where(keep, _dot_nt(q_mid, k_mid), 0.0)
        o_intra = _dot(a.astype(BF16), v)
        ends = jnp.concatenate([g_end.reshape(per_blk, C_DK), jnp.zeros((8 - per_blk, C_DK), F32)], axis=0)
        decay = jnp.transpose(jnp.exp(ends))
        o_inter = [None] * per_blk
        for c in (range(per_blk) if direction == 0 else reversed(range(per_blk))):
            rows = slice(c * C_CHUNK, (c + 1) * C_CHUNK)
            st = st_ref[...]
            o_inter[c] = _dot(q_in[rows], st.astype(BF16))
            st_ref[...] = st * decay[:, c:c + 1] + _dot_tn(k_end[rows], v[rows])
        return o_intra + jnp.concatenate(o_inter, axis=0)

    sf_ref[...] = jnp.zeros_like(sf_ref)
    sb_ref[...] = jnp.zeros_like(sb_ref)
    ctx_blocks = N_CTX // blk
    order_b = list(reversed(range(ctx_blocks))) + list(reversed(range(ctx_blocks, n_blocks)))
    for i in range(n_blocks):
        o_ref[0, i * blk:(i + 1) * blk, :] = block(i, 0, sf_ref)
        sb = order_b[i]
        ob_ref[sb * blk:(sb + 1) * blk, :] = block(sb, 1, sb_ref)
    o_ref[0] = o_ref[0] + ob_ref[...]


def _gla(p, lr, wg, bg):
    b, n, _ = p.shape
    return pl.pallas_call(
        _gla_kernel,
        out_shape=jax.ShapeDtypeStruct((b, n, C_HEADS * C_DV), F32),
        grid=(b, C_HEADS),
        in_specs=[
            pl.BlockSpec((1, n, C_DK), lambda i, h: (i, 0, h)),
            pl.BlockSpec((1, n, C_DK), lambda i, h: (i, 0, C_HEADS + h)),
            pl.BlockSpec((1, n, C_DV), lambda i, h: (i, 0, C_HEADS + h)),
            pl.BlockSpec((1, n, LANES), lambda i, h: (i, 0, 0)),
            pl.BlockSpec((2, LANES, C_DK), lambda i, h: (0, 0, h)),
            pl.BlockSpec((2, C_DK), lambda i, h: (0, h)),
        ],
        out_specs=pl.BlockSpec((1, n, C_DV), lambda i, h: (i, 0, h)),
        scratch_shapes=[pltpu.VMEM((n, C_DV), F32), pltpu.VMEM((n, 2 * C_DK), F32),
                        pltpu.VMEM((C_DK, C_DV), F32), pltpu.VMEM((C_DK, C_DV), F32)],
        compiler_params=_cparams(("arbitrary", "arbitrary")),
        name="gla",
    )(p, p, p, lr, wg, bg)


def _out_kernel(*refs, gla):
    if gla:
        o_ref, gate_ref, gn_ref, w_ref, s_ref, mod_ref, ng_ref, rw_ref, snew_ref, h2_ref, lg_ref = refs
        o = o_ref[0]
        gate = gate_ref[0].astype(F32)
        parts = []
        for hd in range(C_HEADS):
            oh = _rms(o[:, hd * C_DV:(hd + 1) * C_DV], gn_ref[...])
            parts.append((oh * _silu(gate[:, hd * C_DV:(hd + 1) * C_DV])).astype(BF16))
        acc = _dot(jnp.concatenate(parts, axis=-1), w_ref[...])
    else:
        oax_ref, oac_ref, obx_ref, obc_ref, w_ref, s_ref, mod_ref, ng_ref, rw_ref, snew_ref, h2_ref, lg_ref = refs
        half = oax_ref.shape[2]
        is_ctx = pl.program_id(1) == 0
        oa = jnp.where(is_ctx, oac_ref[0], oax_ref[0])
        ob = jnp.where(is_ctx, obc_ref[0], obx_ref[0])
        acc = _dot(oa, w_ref[0:half, :]) + _dot(ob, w_ref[half:, :])
    x = s_ref[0] + mod_ref[0, 0, 2:3, :] * acc
    snew_ref[0] = x
    h2 = _rms(x, ng_ref[...]) * (1.0 + mod_ref[0, 0, 4:5, :]) + mod_ref[0, 0, 3:4, :]
    _store_token_rows(h2_ref, h2)
    t = _dot(h2.astype(BF16), rw_ref[...])
    lg_ref[0] = t + pltpu.roll(t, LANES - N_EXPERTS, 1)


def _out_project(mix, w_out, s, mod, ng, rw, *, gla, gn=None):
    b, n, d = s.shape
    tm = ROW_TILE
    row_spec = pl.BlockSpec((1, tm, d), lambda i, j: (i, j, 0))
    const2 = lambda i, j: (0, 0)
    if gla:
        o, p = mix
        in_specs = [row_spec, pl.BlockSpec((1, tm, d), lambda i, j: (i, j, 2)),
                    pl.BlockSpec((1, C_DV), const2)]
        args = [o, p, gn.reshape(1, C_DV)]
    else:
        oa_x, oa_c, ob_x, ob_c = mix
        half = oa_x.shape[2]
        latent = pl.BlockSpec((1, tm, half), lambda i, j: (i, jnp.maximum(j - 1, 0), 0))
        context = pl.BlockSpec((1, tm, half), lambda i, j: (i, 0, 0))
        in_specs = [latent, context, latent, context]
        args = [oa_x, oa_c, ob_x, ob_c]
    in_specs += [pl.BlockSpec((d, d), const2), row_spec,
                 pl.BlockSpec((1, 1, 6, d), lambda i, j: (i, jnp.minimum(j, 1), 0, 0)),
                 pl.BlockSpec((1, d), const2), pl.BlockSpec((d, LANES), const2)]
    args += [w_out, s, mod, ng.reshape(1, d), rw]
    return pl.pallas_call(
        functools.partial(_out_kernel, gla=gla),
        out_shape=[jax.ShapeDtypeStruct((b, n, d), F32), jax.ShapeDtypeStruct((b * n * ROW_SEGS, LANES), jnp.int32),
                   jax.ShapeDtypeStruct((b, n, LANES), F32)],
        grid=(b, n // tm),
        in_specs=in_specs,
        out_specs=[row_spec, pl.BlockSpec((tm * ROW_SEGS, LANES), lambda i, j: (i * (n // tm) + j, 0)),
                   pl.BlockSpec((1, tm, LANES), lambda i, j: (i, j, 0))],
        compiler_params=_cparams(("arbitrary", "arbitrary")),
        name="out_project",
    )(*args)


def _route_kernel(bias_ref, lg_ref, cls_ref, wlo_ref, whi_ref):
    score = [_sigmoid(lg_ref[e]) for e in range(N_EXPERTS)]
    sel = [score[e] + bias_ref[e] for e in range(N_EXPERTS)]
    grp_score = []
    for g in range(N_GROUPS):
        v = sel[g * PER_GROUP:(g + 1) * PER_GROUP]
        best = v[0] + v[1]
        for a in range(PER_GROUP):
            for c in range(a + 1, PER_GROUP):
                if (a, c) != (0, 1):
                    best = jnp.maximum(best, v[a] + v[c])
        grp_score.append(best)
    grp = jnp.zeros(grp_score[0].shape, jnp.int32)
    best = grp_score[0]
    for g in range(1, N_GROUPS):
        upd = grp_score[g] > best
        best = jnp.where(upd, grp_score[g], best)
        grp = jnp.where(upd, g, grp)

    def pick(vals, j):
        out = vals[j]
        for g in range(1, N_GROUPS):
            out = jnp.where(grp == g, vals[g * PER_GROUP + j], out)
        return out

    v = [pick(sel, j) for j in range(PER_GROUP)]
    sc = [pick(score, j) for j in range(PER_GROUP)]
    one = jnp.ones(grp.shape, jnp.int32)
    zero = jnp.zeros(grp.shape, jnp.int32)
    chosen = []
    for j in range(PER_GROUP):
        rank = zero
        for m in range(PER_GROUP):
            if m == j:
                continue
            ahead = (v[m] >= v[j]) if m < j else (v[m] > v[j])
            rank = rank + jnp.where(ahead, one, zero)
        chosen.append(rank < 2)
    code = zero
    for j in range(PER_GROUP):
        code = code + jnp.where(chosen[j], one * (1 << j), zero)
    pair = zero
    for idx in range(6):
        pair = jnp.where(code == (1 << PAIR_LO[idx]) + (1 << PAIR_HI[idx]), idx, pair)
    s_lo = jnp.where(chosen[0], sc[0], jnp.where(chosen[1], sc[1], sc[2]))
    s_hi = jnp.where(chosen[3], sc[3], jnp.where(chosen[2], sc[2], sc[1]))
    den = s_lo + s_hi
    cls_ref[...] = grp * 6 + pair
    wlo_ref[...] = s_lo / den
    whi_ref[...] = s_hi / den


def _route(logits_t, router_bias):
    _, r, _ = logits_t.shape
    full = pl.BlockSpec((r, LANES), lambda i: (0, 0))
    return pl.pallas_call(
        _route_kernel,
        out_shape=[jax.ShapeDtypeStruct((r, LANES), jnp.int32), jax.ShapeDtypeStruct((r, LANES), F32),
                   jax.ShapeDtypeStruct((r, LANES), F32)],
        grid=(1,),
        in_specs=[pl.BlockSpec(memory_space=pltpu.SMEM),
                  pl.BlockSpec((N_EXPERTS, r, LANES), lambda i: (0, 0, 0))],
        out_specs=[full, full, full],
        compiler_params=_cparams(("arbitrary",)),
        name="route",
    )(router_bias, logits_t)


def _gather_rows(table, idx):
    p = idx.shape[0]
    per_worker = p // SC_WORKERS
    n_chunks = per_worker // SC_GATHER_ROWS
    mesh = plsc.VectorSubcoreMesh(core_axis_name="c", subcore_axis_name="s", num_cores=SC_CORES,
                                  num_subcores=SC_SUBCORES)

    @functools.partial(
        pl.kernel, mesh=mesh,
        out_type=jax.ShapeDtypeStruct((p,) + table.shape[1:], table.dtype),
        scratch_types=[pltpu.VMEM((per_worker,), jnp.int32),
                       pltpu.VMEM((SC_GATHER_ROWS,) + table.shape[1:], table.dtype),
                       pltpu.SemaphoreType.DMA],
        compiler_params=pltpu.CompilerParams(use_tc_tiling_on_sc=True),
        name="gather_rows",
    )
    def gather(table_hbm, idx_hbm, out_hbm, idx_v, rows_v, sem):
        worker = lax.axis_index("s") * SC_CORES + lax.axis_index("c")
        base = worker * per_worker
        pltpu.sync_copy(idx_hbm.at[pl.ds(base, per_worker)], idx_v)

        @pl.loop(0, n_chunks)
        def _(i):
            off = pl.multiple_of(i * SC_GATHER_ROWS, SC_GATHER_ROWS)
            pltpu.async_copy(table_hbm.at[idx_v.at[pl.ds(off, SC_GATHER_ROWS)]], rows_v, sem).wait()
            pltpu.sync_copy(rows_v, out_hbm.at[pl.ds(base + off, SC_GATHER_ROWS)])

    return gather(table, idx)


def _moe_kernel(lo_ref, hi_ref, nt_ref, x_ref, wr_ref, w1l_ref, w1h_ref, w3l_ref, w3h_ref, w2l_ref, w2h_ref, y_ref):
    j = pl.program_id(0)

    @pl.when(j < nt_ref[0])
    def _():
        x = _token_rows(x_ref).astype(BF16)
        wr = wr_ref[...]
        he_lo = (_silu(_dot(x, w1l_ref[0, 0])) * _dot(x, w3l_ref[0, 0]) * wr[:, 0:1]).astype(BF16)
        he_hi = (_silu(_dot(x, w1h_ref[0, 0])) * _dot(x, w3h_ref[0, 0]) * wr[:, 1:2]).astype(BF16)
        y = _dot(he_lo, w2l_ref[0, 0]) + _dot(he_hi, w2h_ref[0, 0])
        _store_token_rows(y_ref, y)

    @pl.when(j >= nt_ref[0])
    def _():
        y_ref[...] = jnp.zeros_like(y_ref)


def _moe_experts(xs, wrow, tile_lo, tile_hi, n_tiles, layer, w1, w3, w2):
    tp = xs.shape[0] // ROW_SEGS
    tm = MOE_TILE
    d, de = w1.shape[2], w1.shape[3]

    def gate_map(j, lo, hi, nt):
        return (jnp.minimum(j, nt[0] - 1), 0)

    def lo_map(j, lo, hi, nt):
        return (layer, lo[jnp.minimum(j, nt[0] - 1)], 0, 0)

    def hi_map(j, lo, hi, nt):
        return (layer, hi[jnp.minimum(j, nt[0] - 1)], 0, 0)

    up = (1, 1, d, de)
    down = (1, 1, de, d)
    return pl.pallas_call(
        _moe_kernel,
        out_shape=jax.ShapeDtypeStruct((tp * ROW_SEGS, LANES), jnp.int32),
        grid_spec=pltpu.PrefetchScalarGridSpec(
            num_scalar_prefetch=3,
            grid=(tp // tm,),
            in_specs=[pl.BlockSpec((tm * ROW_SEGS, LANES), gate_map), pl.BlockSpec((tm, 2), gate_map),
                      pl.BlockSpec(up, lo_map), pl.BlockSpec(up, hi_map),
                      pl.BlockSpec(up, lo_map), pl.BlockSpec(up, hi_map),
                      pl.BlockSpec(down, lo_map), pl.BlockSpec(down, hi_map)],
            out_specs=pl.BlockSpec((tm * ROW_SEGS, LANES), lambda j, lo, hi, nt: (j, 0)),
        ),
        compiler_params=_cparams(("arbitrary",)),
        name="moe_experts",
    )(tile_lo, tile_hi, n_tiles, xs, wrow, w1, w1, w3, w3, w2, w2)


def _moe(h2, logits, router_bias, layer, w1, w3, w2):
    t = h2.shape[0]
    tm = MOE_TILE
    n_tiles_max = t // tm + N_CLASSES
    tp = n_tiles_max * tm
    lg_t = logits[:, :N_EXPERTS].T.reshape(N_EXPERTS, t // LANES, LANES)
    cls, wlo, whi = _route(lg_t, router_bias)
    cls, wlo, whi = cls.reshape(t), wlo.reshape(t), whi.reshape(t)
    classes = jnp.arange(N_CLASSES, dtype=jnp.int32)
    onehot = (cls[:, None] == classes[None, :]).astype(jnp.int32)
    csum = jnp.cumsum(onehot, axis=0)
    rank = jnp.sum(csum * onehot, axis=1) - 1
    counts = csum[-1]
    tiles_per = (counts + tm - 1) // tm
    tile_end = jnp.cumsum(tiles_per)
    tile_start = tile_end - tiles_per
    dest = jnp.sum(onehot * (tile_start * tm)[None, :], axis=1) + rank
    tiles = jnp.arange(n_tiles_max, dtype=jnp.int32)
    tile_cls = jnp.minimum(jnp.sum((tile_end[None, :] <= tiles[:, None]).astype(jnp.int32), axis=1), N_CLASSES - 1)
    pair = tile_cls % 6
    base = (tile_cls // 6) * PER_GROUP
    pair_onehot = (pair[:, None] == jnp.arange(6, dtype=jnp.int32)[None, :]).astype(jnp.int32)
    tile_lo = base + jnp.sum(pair_onehot * jnp.asarray(PAIR_LO, jnp.int32)[None, :], axis=1)
    tile_hi = base + jnp.sum(pair_onehot * jnp.asarray(PAIR_HI, jnp.int32)[None, :], axis=1)
    per_token = jnp.stack([jnp.arange(t, dtype=jnp.int32), lax.bitcast_convert_type(wlo, jnp.int32),
                           lax.bitcast_convert_type(whi, jnp.int32)], axis=1)
    padding = jnp.stack([jnp.arange(tp, dtype=jnp.int32) % t, jnp.zeros((tp,), jnp.int32),
                         jnp.zeros((tp,), jnp.int32)], axis=1)
    per_row = padding.at[dest].set(per_token, unique_indices=True)
    tok = per_row[:, 0]
    wrow = lax.bitcast_convert_type(per_row[:, 1:3], F32)
    xs = _gather_rows(h2, tok).reshape(tp * ROW_SEGS, LANES)
    ys = _moe_experts(xs, wrow, tile_lo, tile_hi, tile_end[-1:], layer, w1, w3, w2)
    return _gather_rows(ys.reshape(tp, ROW_SEGS, LANES), dest)


def _final_kernel(s_ref, y_ref, mod_ref, g_ref, o_ref):
    x = s_ref[0] + mod_ref[0, 0, 5:6, :] * _token_rows(y_ref)
    o_ref[0] = _rms(x, g_ref[...])


def _final(s, y, mod, final_g):
    b, n, d = s.shape
    tm = ROW_TILE
    skip = N_CTX // tm
    row_spec = pl.BlockSpec((1, tm, d), lambda i, j: (i, j + skip, 0))
    return pl.pallas_call(
        _final_kernel,
        out_shape=jax.ShapeDtypeStruct((b, n - N_CTX, d), F32),
        grid=(b, (n - N_CTX) // tm),
        in_specs=[row_spec, pl.BlockSpec((tm * ROW_SEGS, LANES), lambda i, j: (i * (n // tm) + j + skip, 0)),
                  pl.BlockSpec((1, 1, 6, d), lambda i, j: (i, 1, 0, 0)),
                  pl.BlockSpec((1, d), lambda i, j: (0, 0))],
        out_specs=pl.BlockSpec((1, tm, d), lambda i, j: (i, j, 0)),
        compiler_params=_cparams(("arbitrary", "arbitrary")),
        name="final_norm",
    )(s, y, mod, final_g.reshape(1, d))


def _rope_tables(seq):
    pos = jnp.arange(seq)
    row_pos, col_pos = pos // GRID_W, pos % GRID_W
    n = A_DH // 2
    inv = ROPE_BASE ** (-jnp.arange(0, n, 2, dtype=F32) / n)
    ang_r = row_pos.astype(F32)[:, None] * inv[None, :]
    ang_c = col_pos.astype(F32)[:, None] * inv[None, :]
    ang = jnp.concatenate([ang_r, ang_c], axis=-1)
    cos = jnp.tile(jnp.cos(ang), (1, 4))
    sin = jnp.tile(jnp.sin(ang), (1, 4))
    sin = jnp.concatenate([-sin[:, :LANES // 2], sin[:, LANES // 2:]], axis=-1)
    cos = jnp.concatenate([jnp.ones((N_CTX, LANES), F32), cos], axis=0)
    sin = jnp.concatenate([jnp.zeros((N_CTX, LANES), F32), sin], axis=0)
    return cos, sin


def _interleave_maps(w):
    d = w.shape[0]
    w = w.reshape(d, A_HEADS, 2, 2, 2, A_DH // 4)
    return w.transpose(0, 1, 4, 2, 3, 5).reshape(d, A_HEADS * LANES)


def _att_weights(w_in):
    a_qk = A_HEADS * 2 * A_DH
    a_v = A_HEADS * A_DV
    b_w = B_HEADS * B_DH
    qa = _interleave_maps(w_in[:, :a_qk]) * (A_DH ** -0.5 * LOG2_E)
    ka = _interleave_maps(w_in[:, a_qk:2 * a_qk])
    va = w_in[:, 2 * a_qk:2 * a_qk + a_v]
    o = 2 * a_qk + a_v
    qb = w_in[:, o:o + b_w] * (B_DH ** -0.5)
    rest = w_in[:, o + b_w:]
    return jnp.concatenate([qa, ka, va, qb, rest], axis=1).astype(BF16)


def _gla_weights(w_in):
    d = w_in.shape[0]
    n_main = 2 * C_HEADS * C_DK + 2 * C_HEADS * C_DV
    pad = jnp.zeros((d, LANES - 2 * C_RANK), w_in.dtype)
    return jnp.concatenate([w_in, pad], axis=1).astype(BF16), n_main


def kernel(x, c, ctx, c_ctx, w_mod, b_mod, norm_g, final_g, att_w_in, att_w_out, att_lambda, att_subln_g, na_bias,
           gla_w_in, gla_w_gate, gla_b_gate, gla_norm_g, gla_w_out, router_w, router_bias, moe_w1, moe_w3, moe_w2):
    b, seq, d = x.shape
    n = N_CTX + seq
    s = jnp.concatenate([ctx, x], axis=1)

    rows = b + 1
    rows_pad = -(-rows // 8) * 8
    cc = jnp.concatenate([c, c_ctx[None, :], jnp.zeros((rows_pad - rows, d), F32)], axis=0)
    mod_all = _mod_vectors(cc, w_mod, b_mod)
    mod_x = mod_all[:, :b].reshape(DEPTH, b, 1, 6, d)
    mod_c = jnp.broadcast_to(mod_all[:, b].reshape(DEPTH, 1, 1, 6, d), (DEPTH, b, 1, 6, d))
    mods = jnp.concatenate([mod_c, mod_x], axis=2)

    cos, sin = _rope_tables(seq)
    rw_hi = router_w.astype(BF16)
    rw_lo = (router_w - rw_hi.astype(F32)).astype(BF16)
    zpad = jnp.zeros((d, LANES - 2 * N_EXPERTS), BF16)
    rw = jnp.concatenate([rw_hi, rw_lo, zpad], axis=1)

    w1_b, w3_b, w2_b = moe_w1.astype(BF16), moe_w3.astype(BF16), moe_w2.astype(BF16)
    y = None
    for i in range(DEPTH):
        j = i // 2
        modp = mods[i - 1] if i else None
        if i % 2 == 0:
            lam_init = 0.8 - 0.6 * math.exp(-0.3 * i)
            w = _att_weights(att_w_in[j])
            outs = _project(s, y, modp, mods[i], norm_g[i, 0], w, cos, sin,
                            n_rope=2 * A_HEADS * LANES, n_bf16=w.shape[1])
            if i:
                s = outs[0]
            p = outs[-1]
            oa_x, oa_c = _diff_attention(p, att_lambda[j], att_subln_g[j], lam_init)
            ob_x, ob_c = _neighbourhood_attention(p, _na_bias_table(na_bias[j], seq // GRID_W))
            s, h2, logits = _out_project((oa_x, oa_c, ob_x, ob_c), att_w_out[j].astype(BF16), s, mods[i], norm_g[i, 1],
                                         rw, gla=False)
        else:
            w, n_main = _gla_weights(gla_w_in[j])
            outs = _project(s, y, modp, mods[i], norm_g[i, 0], w, None, None, n_rope=0, n_bf16=n_main)
            s, p, lr = outs
            wg = jnp.zeros((2, LANES, C_HEADS * C_DK), F32)
            wg = wg.at[0, :C_RANK].set(gla_w_gate[j, 0]).at[1, C_RANK:2 * C_RANK].set(gla_w_gate[j, 1])
            o = _gla(p, lr, wg, gla_b_gate[j])
            s, h2, logits = _out_project((o, p), gla_w_out[j].astype(BF16), s, mods[i], norm_g[i, 1],
                                         rw, gla=True, gn=gla_norm_g[j])
        y = _moe(h2.reshape(b * n, ROW_SEGS, LANES), logits.reshape(b * n, LANES), router_bias,
                 i, w1_b, w3_b, w2_b)
        y = y.reshape(b * n * ROW_SEGS, LANES)
    return _final(s, y, mods[DEPTH - 1], final_g)
```

```python
import functools
import math

import jax
import jax.numpy as jnp
import numpy as np
from jax import lax
from jax.experimental import pallas as pl
from jax.experimental.pallas import tpu as pltpu
from jax.experimental.pallas import tpu_sc as plsc

F32 = jnp.float32
BF16 = jnp.bfloat16

D_MODEL = 1024
DEPTH = 4
GRID_W = 64
N_CTX = 256
A_HEADS = 4
A_DH = 64
A_DV = 128
B_HEADS = 8
B_DH = 64
NA_ROWS = 8
NA_COLS = 16
C_HEADS = 4
C_DK = 128
C_DV = 256
C_RANK = 16
C_TAU = 16.0
C_CHUNK = 64
GLA_BLOCK = 256
N_EXPERTS = 16
N_GROUPS = 4
PER_GROUP = 4
ROPE_BASE = 10000.0
EPS = 1e-6
LOG2_E = math.log2(math.e)
ATT_KEY_BLOCK = 768
ATT_Q_HALVES = 4

LANES = 128
ROW_TILE = 256
MOE_TILE = 256
SC_CORES = 2
SC_SUBCORES = 16
SC_WORKERS = SC_CORES * SC_SUBCORES
SC_GATHER_ROWS = 64
N_CLASSES = N_GROUPS * 6
NA_QROWS = 16
NA_SUB = 2
NA_KROWS = NA_SUB + NA_ROWS - 1
VMEM_LIMIT = 52 * 1024 * 1024
ROW_SEGS = D_MODEL // LANES // 2

PAIR_LO = (0, 0, 0, 1, 1, 2)
PAIR_HI = (1, 2, 3, 2, 3, 3)


def _cparams(sem):
    return pltpu.CompilerParams(dimension_semantics=sem, vmem_limit_bytes=VMEM_LIMIT)


def _sigmoid(x):
    return 1.0 / (1.0 + jnp.exp(-x))


def _silu(x):
    return x * _sigmoid(x)


def _rms(x, g):
    return x * lax.rsqrt(jnp.mean(x * x, axis=-1, keepdims=True) + EPS) * g


def _dot(a, b):
    return jnp.dot(a, b, preferred_element_type=F32)


def _dot_nt(a, b):
    return lax.dot_general(a, b, (((1,), (1,)), ((), ())), preferred_element_type=F32)


def _dot_tn(a, b):
    return lax.dot_general(a, b, (((0,), (0,)), ((), ())), preferred_element_type=F32)


def _mod_kernel(c_ref, w_ref, b_ref, o_ref):
    a = _silu(c_ref[...]).astype(BF16)
    o_ref[0] = _dot(a, w_ref[0].astype(BF16)) + b_ref[0]


def _mod_vectors(cc, w_mod, b_mod):
    depth, d, n6 = w_mod.shape
    rows = cc.shape[0]
    tn = 1536
    return pl.pallas_call(
        _mod_kernel,
        out_shape=jax.ShapeDtypeStruct((depth, rows, n6), F32),
        grid=(depth, n6 // tn),
        in_specs=[
            pl.BlockSpec((rows, d), lambda i, j: (0, 0)),
            pl.BlockSpec((1, d, tn), lambda i, j: (i, 0, j)),
            pl.BlockSpec((1, 1, tn), lambda i, j: (i, 0, j)),
        ],
        out_specs=pl.BlockSpec((1, rows, tn), lambda i, j: (i, 0, j)),
        compiler_params=_cparams(("arbitrary", "arbitrary")),
        name="mod_vectors",
    )(cc, w_mod, b_mod.reshape(depth, 1, n6))


def _token_rows(ref):
    rows = ref.shape[0] // ROW_SEGS
    packed = jnp.concatenate([ref[pl.ds(sg, rows, stride=ROW_SEGS), :] for sg in range(ROW_SEGS)], axis=-1)
    low = lax.bitcast_convert_type(lax.shift_left(packed, jnp.int32(16)), F32)
    high = lax.bitcast_convert_type(packed & jnp.int32(-65536), F32)
    return jnp.concatenate([low, high], axis=-1)


def _store_token_rows(ref, val):
    rows, d = val.shape
    rounded = val.astype(BF16).astype(F32)
    bits = lax.bitcast_convert_type(rounded, jnp.int32)
    packed = lax.shift_right_logical(bits[:, :d // 2], jnp.int32(16)) | bits[:, d // 2:]
    for sg in range(ROW_SEGS):
        ref[pl.ds(sg, rows, stride=ROW_SEGS), :] = packed[:, sg * LANES:(sg + 1) * LANES]


def _proj_kernel(*refs, has_y, n_rope, n_bf16, col_chunk):
    it = iter(refs)
    s_ref = next(it)
    y_ref = next(it) if has_y else None
    modp_ref = next(it) if has_y else None
    mod_ref = next(it)
    ng_ref = next(it)
    w_ref = next(it)
    cos_ref = next(it) if n_rope else None
    sin_ref = next(it) if n_rope else None
    snew_ref = next(it) if has_y else None
    p_ref = next(it)
    lr_ref = next(it) if w_ref.shape[1] > n_bf16 else None

    x = s_ref[0]
    if has_y:
        x = x + modp_ref[0, 0, 5:6, :] * _token_rows(y_ref)
        snew_ref[0] = x
    h = _rms(x, ng_ref[...])
    h = h * (1.0 + mod_ref[0, 0, 1:2, :]) + mod_ref[0, 0, 0:1, :]
    hb = h.astype(BF16)
    n_out = w_ref.shape[1]
    for c0 in range(0, n_out, col_chunk):
        c1 = min(c0 + col_chunk, n_out)
        acc = _dot(hb, w_ref[:, c0:c1])
        for b0 in range(c0, c1, LANES):
            t = acc[:, b0 - c0:b0 - c0 + LANES]
            if b0 < n_rope:
                t = t * cos_ref[...] + pltpu.roll(t, LANES // 2, 1) * sin_ref[...]
            if b0 < n_bf16:
                p_ref[0, :, b0:b0 + LANES] = t.astype(BF16)
            else:
                lr_ref[0, :, b0 - n_bf16:b0 - n_bf16 + LANES] = t


def _project(s, y, modp, mod, ng, w, cos, sin, *, n_rope, n_bf16):
    b, n, d = s.shape
    n_out = w.shape[1]
    tm = ROW_TILE
    has_y = y is not None
    row_spec = pl.BlockSpec((1, tm, d), lambda i, j: (i, j, 0))
    mod_spec = pl.BlockSpec((1, 1, 6, d), lambda i, j: (i, jnp.minimum(j, 1), 0, 0))
    in_specs = [row_spec]
    args = [s]
    if has_y:
        in_specs += [pl.BlockSpec((tm * ROW_SEGS, LANES), lambda i, j: (i * (n // tm) + j, 0)), mod_spec]
        args += [y, modp]
    in_specs += [mod_spec, pl.BlockSpec((1, d), lambda i, j: (0, 0)),
                 pl.BlockSpec((d, n_out), lambda i, j: (0, 0))]
    args += [mod, ng.reshape(1, d), w]
    if n_rope:
        tab_spec = pl.BlockSpec((tm, LANES), lambda i, j: (j, 0))
        in_specs += [tab_spec, tab_spec]
        args += [cos, sin]
    out_shape, out_specs = [], []
    if has_y:
        out_shape.append(jax.ShapeDtypeStruct((b, n, d), F32))
        out_specs.append(row_spec)
    out_shape.append(jax.ShapeDtypeStruct((b, n, n_bf16), BF16))
    out_specs.append(pl.BlockSpec((1, tm, n_bf16), lambda i, j: (i, j, 0)))
    if n_out > n_bf16:
        out_shape.append(jax.ShapeDtypeStruct((b, n, n_out - n_bf16), F32))
        out_specs.append(pl.BlockSpec((1, tm, n_out - n_bf16), lambda i, j: (i, j, 0)))
    return pl.pallas_call(
        functools.partial(_proj_kernel, has_y=has_y, n_rope=n_rope, n_bf16=n_bf16, col_chunk=512),
        out_shape=out_shape,
        grid=(b, n // tm),
        in_specs=in_specs,
        out_specs=out_specs,
        compiler_params=_cparams(("arbitrary", "arbitrary")),
        name="norm_mod_project",
    )(*args)


def _softmax_rows(s):
    e = jnp.exp(s - jnp.max(s, axis=-1, keepdims=True))
    return e * (1.0 / jnp.sum(e, axis=-1, keepdims=True))


def _diff_attn_kernel(*refs, lam_init, n_q):
    q_refs = refs[:n_q]
    k_ref, v_ref, lam_ref, g_ref, o_ref = refs[n_q:]
    lp = lam_ref[...]
    lam = (jnp.exp(jnp.sum(lp[0:1] * lp[1:2], axis=-1, keepdims=True))
           - jnp.exp(jnp.sum(lp[2:3] * lp[3:4], axis=-1, keepdims=True)) + lam_init)
    q = jnp.concatenate([r[0] for r in q_refs], axis=0)
    lane = lax.broadcasted_iota(jnp.int32, (1, LANES), 1)
    first_map = (lane // 32) % 2 == 0
    zero = jnp.zeros_like(q)
    q0 = jnp.where(first_map, q, zero)
    q1 = jnp.where(first_map, zero, q)
    n_keys = k_ref.shape[1]
    blk = min(ATT_KEY_BLOCK, n_keys)
    ones = jnp.ones((blk, A_DV), BF16)
    state = [None, None]
    for kb in range(n_keys // blk):
        k = k_ref[0, kb * blk:(kb + 1) * blk, :]
        v1 = jnp.concatenate([v_ref[0, kb * blk:(kb + 1) * blk, :], ones], axis=1)
        for i, qm in enumerate((q0, q1)):
            s = _dot_nt(qm, k)
            m_new = jnp.max(s, axis=-1, keepdims=True)
            if kb:
                m_old, acc_old = state[i]
                m_new = jnp.maximum(m_old, m_new)
            acc = _dot(jnp.exp2(s - m_new).astype(BF16), v1)
            if kb:
                acc = acc_old * jnp.exp2(m_old - m_new) + acc
            state[i] = (m_new, acc)
    acc0, acc1 = state[0][1], state[1][1]
    o = acc0[:, :A_DV] * (1.0 / acc0[:, A_DV:]) - acc1[:, :A_DV] * (lam / acc1[:, A_DV:])
    o_ref[0] = (_rms(o, g_ref[...]) * (1.0 - lam_init)).astype(BF16)


def _diff_attention(p, lam_p, subln_g, lam_init):
    b, n, _ = p.shape
    half = ROW_TILE
    tq = ATT_Q_HALVES * half
    ctx_blocks = N_CTX // half
    width = A_HEADS * A_DV
    small = [pl.BlockSpec((4, A_DH), lambda i, h, *_: (0, 0)), pl.BlockSpec((1, A_DV), lambda i, h, *_: (0, 0))]
    q_specs = [pl.BlockSpec((1, half, LANES), functools.partial(lambda i, h, j, part: (i, ctx_blocks + ATT_Q_HALVES * j + part, h),
                                                                 part=part)) for part in range(ATT_Q_HALVES)]
    oa_x = pl.pallas_call(
        functools.partial(_diff_attn_kernel, lam_init=lam_init, n_q=ATT_Q_HALVES),
        out_shape=jax.ShapeDtypeStruct((b, n - N_CTX, width), BF16),
        grid=(b, A_HEADS, (n - N_CTX) // tq),
        in_specs=q_specs + [
            pl.BlockSpec((1, n, LANES), lambda i, h, j: (i, 0, A_HEADS + h)),
            pl.BlockSpec((1, n, LANES), lambda i, h, j: (i, 0, 2 * A_HEADS + h)),
        ] + small,
        out_specs=pl.BlockSpec((1, tq, LANES), lambda i, h, j: (i, j, h)),
        compiler_params=_cparams(("arbitrary", "arbitrary", "arbitrary")),
        name="diff_attention",
    )(*([p] * ATT_Q_HALVES), p, p, lam_p, subln_g.reshape(1, A_DV))
    oa_c = pl.pallas_call(
        functools.partial(_diff_attn_kernel, lam_init=lam_init, n_q=1),
        out_shape=jax.ShapeDtypeStruct((b, N_CTX, width), BF16),
        grid=(b, A_HEADS),
        in_specs=[
            pl.BlockSpec((1, N_CTX, LANES), lambda i, h: (i, 0, h)),
            pl.BlockSpec((1, N_CTX, LANES), lambda i, h: (i, 0, A_HEADS + h)),
            pl.BlockSpec((1, N_CTX, LANES), lambda i, h: (i, 0, 2 * A_HEADS + h)),
        ] + small,
        out_specs=pl.BlockSpec((1, N_CTX, LANES), lambda i, h: (i, 0, h)),
        compiler_params=_cparams(("arbitrary", "arbitrary")),
        name="context_diff_attention",
    )(p, p, p, lam_p, subln_g.reshape(1, A_DV))
    return oa_x, oa_c


def _split_heads(q):
    lane = lax.broadcasted_iota(jnp.int32, (1, LANES), 1)
    first_head = lane < B_DH
    zero = jnp.zeros_like(q)
    return first_head, (jnp.where(first_head, q, zero), jnp.where(first_head, zero, q))


def _na_ctx_kernel(q_ref, k_ref, v_ref, o_ref):
    first_head, qh = _split_heads(q_ref[0])
    kc = k_ref[0]
    vc = v_ref[0]
    outs = [_dot(_softmax_rows(_dot_nt(qh[h], kc)).astype(BF16), vc) for h in range(2)]
    o_ref[0] = jnp.where(first_head, outs[0], outs[1]).astype(BF16)


def _na_kernel(*refs):
    q_refs = refs[:-4]
    k_ref, v_ref, bm_ref, o_ref = refs[-4:]
    blk = pl.program_id(2)
    rows = (k_ref.shape[1] - N_CTX) // GRID_W
    sub_q = NA_SUB * GRID_W
    sub_k = NA_KROWS * GRID_W
    kc = k_ref[0, 0:N_CTX, :]
    vc = jnp.concatenate([v_ref[0, 0:N_CTX, :], jnp.ones((N_CTX, LANES), BF16)], axis=1)
    ones = jnp.ones((sub_k, LANES), BF16)
    for sub in range(NA_QROWS // NA_SUB):
        q_ref = q_refs[sub * sub_q // ROW_TILE]
        q0 = (sub * sub_q) % ROW_TILE
        first_head, qh = _split_heads(q_ref[0, q0:q0 + sub_q, :])
        k_row0 = jnp.clip(blk * NA_QROWS + sub * NA_SUB - NA_ROWS // 2, 0, rows - NA_KROWS)
        start = pl.multiple_of(N_CTX + k_row0 * GRID_W, GRID_W)
        kw = k_ref[0, pl.ds(start, sub_k), :]
        vw = jnp.concatenate([v_ref[0, pl.ds(start, sub_k), :], ones], axis=1)
        outs = []
        for h in range(2):
            s_loc = _dot_nt(qh[h], kw) + bm_ref[h, 0, sub]
            s_ctx = _dot_nt(qh[h], kc)
            m = jnp.maximum(jnp.max(s_loc, axis=-1, keepdims=True), jnp.max(s_ctx, axis=-1, keepdims=True))
            acc = _dot(jnp.exp(s_loc - m).astype(BF16), vw) + _dot(jnp.exp(s_ctx - m).astype(BF16), vc)
            outs.append(acc[:, :LANES] * (1.0 / acc[:, LANES:]))
        o_ref[0, sub * sub_q:(sub + 1) * sub_q, :] = jnp.where(first_head, outs[0], outs[1]).astype(BF16)


def _na_bias_table(na_bias, rows):
    h = na_bias.shape[0]
    n_dr, n_dc = 2 * NA_ROWS - 1, 2 * NA_COLS - 1
    width = 2 * GRID_W
    left = GRID_W - NA_COLS
    u = jnp.pad(na_bias, ((0, 0), (0, 0), (left, width - left - n_dc)))
    skew = jnp.tile(u, (1, 1, GRID_W))[:, :, :GRID_W * (width - 1)].reshape(h, n_dr, GRID_W, width - 1)
    toeplitz = skew[:, :, :, GRID_W - 1:]
    margin = NA_KROWS - NA_ROWS
    by_col = jnp.pad(toeplitz.transpose(0, 2, 1, 3), ((0, 0), (0, 0), (margin, margin), (0, 0)))
    by_col = by_col.reshape(h, GRID_W, (n_dr + 2 * margin) * GRID_W)
    col = np.arange(GRID_W)
    col_start = np.clip(col - NA_COLS // 2, 0, GRID_W - NA_COLS)
    col_ok = (col[None, :] >= col_start[:, None]) & (col[None, :] < col_start[:, None] + NA_COLS)
    n_sub = NA_QROWS // NA_SUB
    blocks = []
    for q_row0 in (0, NA_QROWS, rows - NA_QROWS):
        for sub in range(n_sub):
            k_row0 = int(np.clip(q_row0 + sub * NA_SUB - NA_ROWS // 2, 0, rows - NA_KROWS))
            kr = k_row0 + np.arange(NA_KROWS)
            for rq in range(NA_SUB):
                r = q_row0 + sub * NA_SUB + rq
                r0 = int(np.clip(r - NA_ROWS // 2, 0, rows - NA_ROWS))
                row_ok = (kr >= r0) & (kr < r0 + NA_ROWS)
                ok = (col_ok[:, None, :] & row_ok[None, :, None]).reshape(GRID_W, NA_KROWS * GRID_W)
                first = k_row0 - r + NA_ROWS - 1 + margin
                window = by_col[:, :, first * GRID_W:(first + NA_KROWS) * GRID_W]
                blocks.append(jnp.where(ok[None], window, -jnp.inf))
    table = jnp.stack(blocks, axis=1)
    return table.reshape(h, 3, n_sub, NA_SUB * GRID_W, NA_KROWS * GRID_W)


def _neighbourhood_attention(p, bm):
    b, n, _ = p.shape
    tq = NA_QROWS * GRID_W
    parts = tq // ROW_TILE
    n_blocks = (n - N_CTX) // tq
    ctx_blocks = N_CTX // ROW_TILE
    col0 = 3 * A_HEADS
    pairs = B_HEADS // 2
    width = B_HEADS * B_DH

    def pattern(j):
        return jnp.minimum(j, 1) + (j == n_blocks - 1).astype(jnp.int32)

    ob_x = pl.pallas_call(
        _na_kernel,
        out_shape=jax.ShapeDtypeStruct((b, n - N_CTX, width), BF16),
        grid=(b, pairs, n_blocks),
        in_specs=[pl.BlockSpec((1, ROW_TILE, LANES),
                               functools.partial(lambda i, h, j, part: (i, ctx_blocks + parts * j + part, col0 + h),
                                                 part=part)) for part in range(parts)] + [
            pl.BlockSpec((1, n, LANES), lambda i, h, j: (i, 0, col0 + pairs + h)),
            pl.BlockSpec((1, n, LANES), lambda i, h, j: (i, 0, col0 + 2 * pairs + h)),
            pl.BlockSpec((2, 1, NA_QROWS // NA_SUB, NA_SUB * GRID_W, NA_KROWS * GRID_W),
                         lambda i, h, j: (h, pattern(j), 0, 0, 0)),
        ],
        out_specs=pl.BlockSpec((1, tq, LANES), lambda i, h, j: (i, j, h)),
        compiler_params=_cparams(("arbitrary", "arbitrary", "arbitrary")),
        name="neighbourhood_attention",
    )(*([p] * parts), p, p, bm)
    ob_c = pl.pallas_call(
        _na_ctx_kernel,
        out_shape=jax.ShapeDtypeStruct((b, N_CTX, width), BF16),
        grid=(b, pairs),
        in_specs=[
            pl.BlockSpec((1, N_CTX, LANES), lambda i, h: (i, 0, col0 + h)),
            pl.BlockSpec((1, N_CTX, LANES), lambda i, h: (i, 0, col0 + pairs + h)),
            pl.BlockSpec((1, N_CTX, LANES), lambda i, h: (i, 0, col0 + 2 * pairs + h)),
        ],
        out_specs=pl.BlockSpec((1, N_CTX, LANES), lambda i, h: (i, 0, h)),
        compiler_params=_cparams(("arbitrary", "arbitrary")),
        name="context_attention",
    )(p, p, p)
    return ob_x, ob_c


def _gla_kernel(q_ref, k_ref, v_ref, lr_ref, wg_ref, bg_ref, o_ref, ob_ref, g_ref, sf_ref, sb_ref):
    n = q_ref.shape[1]
    blk = GLA_BLOCK
    per_blk = blk // C_CHUNK
    n_blocks = n // blk
    ri = lax.broadcasted_iota(jnp.int32, (blk, blk), 0)
    ci = lax.broadcasted_iota(jnp.int32, (blk, blk), 1)
    same_chunk = (ri // C_CHUNK) == (ci // C_CHUNK)
    keeps = (same_chunk & (ci <= ri), same_chunk & (ci >= ri))
    tris = tuple(jnp.where(kp, 1.0, 0.0).astype(BF16) for kp in keeps)
    w_gate = jnp.concatenate([wg_ref[0], wg_ref[1]], axis=1).astype(BF16)
    b_gate = jnp.concatenate([bg_ref[0:1, :], bg_ref[1:2, :]], axis=1)
    for i in range(n_blocks):
        z = _dot(lr_ref[0, i * blk:(i + 1) * blk, :].astype(BF16), w_gate) + b_gate
        g_ref[i * blk:(i + 1) * blk, :] = (jnp.minimum(z, 0.0) - jnp.log(1.0 + jnp.exp(-jnp.abs(z)))) * (1.0 / C_TAU)

    def block(sb, direction, st_ref):
        keep = keeps[direction]
        end_row = C_CHUNK - 1 if direction == 0 else 0
        mid_row = C_CHUNK // 2 - 1 if direction == 0 else C_CHUNK // 2
        r0 = sb * blk
        g = g_ref[r0:r0 + blk, direction * C_DK:(direction + 1) * C_DK]
        g_hi = g.astype(BF16)
        g_lo = (g - g_hi.astype(F32)).astype(BF16)
        gc2 = _dot(tris[direction], jnp.concatenate([g_hi, g_lo], axis=1))
        gc = (gc2[:, :C_DK] + gc2[:, C_DK:]).reshape(per_blk, C_CHUNK, C_DK)
        g_end = gc[:, end_row:end_row + 1, :]
        g_mid = gc[:, mid_row:mid_row + 1, :]
        q = (q_ref[0, r0:r0 + blk, :].astype(F32) * (C_DK ** -0.5)).reshape(per_blk, C_CHUNK, C_DK)
        k = k_ref[0, r0:r0 + blk, :].astype(F32).reshape(per_blk, C_CHUNK, C_DK)
        v = v_ref[0, r0:r0 + blk, :]
        q_in = (q * jnp.exp(gc)).astype(BF16).reshape(blk, C_DK)
        q_mid = (q * jnp.exp(gc - g_mid)).astype(BF16).reshape(blk, C_DK)
        k_mid = (k * jnp.exp(g_mid - gc)).astype(BF16).reshape(blk, C_DK)
        k_end = (k * jnp.exp(g_end - gc)).astype(BF16).reshape(blk, C_DK)
        a = jnp.where(keep, _dot_nt(q_mid, k_mid), 0.0)
        o_intra = _dot(a.astype(BF16), v)
        ends = jnp.concatenate([g_end.reshape(per_blk, C_DK), jnp.zeros((8 - per_blk, C_DK), F32)], axis=0)
        decay = jnp.transpose(jnp.exp(ends))
        o_inter = [None] * per_blk
        for c in (range(per_blk) if direction == 0 else reversed(range(per_blk))):
            rows = slice(c * C_CHUNK, (c + 1) * C_CHUNK)
            st = st_ref[...]
            o_inter[c] = _dot(q_in[rows], st.astype(BF16))
            st_ref[...] = st * decay[:, c:c + 1] + _dot_tn(k_end[rows], v[rows])
        return o_intra + jnp.concatenate(o_inter, axis=0)

    sf_ref[...] = jnp.zeros_like(sf_ref)
    sb_ref[...] = jnp.zeros_like(sb_ref)
    ctx_blocks = N_CTX // blk
    order_b = list(reversed(range(ctx_blocks))) + list(reversed(range(ctx_blocks, n_blocks)))
    for i in range(n_blocks):
        o_ref[0, i * blk:(i + 1) * blk, :] = block(i, 0, sf_ref)
        sb = order_b[i]
        ob_ref[sb * blk:(sb + 1) * blk, :] = block(sb, 1, sb_ref)
    o_ref[0] = o_ref[0] + ob_ref[...]


def _gla(p, lr, wg, bg):
    b, n, _ = p.shape
    return pl.pallas_call(
        _gla_kernel,
        out_shape=jax.ShapeDtypeStruct((b, n, C_HEADS * C_DV), F32),
        grid=(b, C_HEADS),
        in_specs=[
            pl.BlockSpec((1, n, C_DK), lambda i, h: (i, 0, h)),
            pl.BlockSpec((1, n, C_DK), lambda i, h: (i, 0, C_HEADS + h)),
            pl.BlockSpec((1, n, C_DV), lambda i, h: (i, 0, C_HEADS + h)),
            pl.BlockSpec((1, n, LANES), lambda i, h: (i, 0, 0)),
            pl.BlockSpec((2, LANES, C_DK), lambda i, h: (0, 0, h)),
            pl.BlockSpec((2, C_DK), lambda i, h: (0, h)),
        ],
        out_specs=pl.BlockSpec((1, n, C_DV), lambda i, h: (i, 0, h)),
        scratch_shapes=[pltpu.VMEM((n, C_DV), F32), pltpu.VMEM((n, 2 * C_DK), F32),
                        pltpu.VMEM((C_DK, C_DV), F32), pltpu.VMEM((C_DK, C_DV), F32)],
        compiler_params=_cparams(("arbitrary", "arbitrary")),
        name="gla",
    )(p, p, p, lr, wg, bg)


def _out_kernel(*refs, gla):
    if gla:
        o_ref, gate_ref, gn_ref, w_ref, s_ref, mod_ref, ng_ref, rw_ref, snew_ref, h2_ref, lg_ref = refs
        o = o_ref[0]
        gate = gate_ref[0].astype(F32)
        parts = []
        for hd in range(C_HEADS):
            oh = _rms(o[:, hd * C_DV:(hd + 1) * C_DV], gn_ref[...])
            parts.append((oh * _silu(gate[:, hd * C_DV:(hd + 1) * C_DV])).astype(BF16))
        acc = _dot(jnp.concatenate(parts, axis=-1), w_ref[...])
    else:
        oax_ref, oac_ref, obx_ref, obc_ref, w_ref, s_ref, mod_ref, ng_ref, rw_ref, snew_ref, h2_ref, lg_ref = refs
        half = oax_ref.shape[2]
        is_ctx = pl.program_id(1) == 0
        oa = jnp.where(is_ctx, oac_ref[0], oax_ref[0])
        ob = jnp.where(is_ctx, obc_ref[0], obx_ref[0])
        acc = _dot(oa, w_ref[0:half, :]) + _dot(ob, w_ref[half:, :])
    x = s_ref[0] + mod_ref[0, 0, 2:3, :] * acc
    snew_ref[0] = x
    h2 = _rms(x, ng_ref[...]) * (1.0 + mod_ref[0, 0, 4:5, :]) + mod_ref[0, 0, 3:4, :]
    _store_token_rows(h2_ref, h2)
    t = _dot(h2.astype(BF16), rw_ref[...])
    lg_ref[0] = t + pltpu.roll(t, LANES - N_EXPERTS, 1)


def _out_project(mix, w_out, s, mod, ng, rw, *, gla, gn=None):
    b, n, d = s.shape
    tm = ROW_TILE
    row_spec = pl.BlockSpec((1, tm, d), lambda i, j: (i, j, 0))
    const2 = lambda i, j: (0, 0)
    if gla:
        o, p = mix
        in_specs = [row_spec, pl.BlockSpec((1, tm, d), lambda i, j: (i, j, 2)),
                    pl.BlockSpec((1, C_DV), const2)]
        args = [o, p, gn.reshape(1, C_DV)]
    else:
        oa_x, oa_c, ob_x, ob_c = mix
        half = oa_x.shape[2]
        latent = pl.BlockSpec((1, tm, half), lambda i, j: (i, jnp.maximum(j - 1, 0), 0))
        context = pl.BlockSpec((1, tm, half), lambda i, j: (i, 0, 0))
        in_specs = [latent, context, latent, context]
        args = [oa_x, oa_c, ob_x, ob_c]
    in_specs += [pl.BlockSpec((d, d), const2), row_spec,
                 pl.BlockSpec((1, 1, 6, d), lambda i, j: (i, jnp.minimum(j, 1), 0, 0)),
                 pl.BlockSpec((1, d), const2), pl.BlockSpec((d, LANES), const2)]
    args += [w_out, s, mod, ng.reshape(1, d), rw]
    return pl.pallas_call(
        functools.partial(_out_kernel, gla=gla),
        out_shape=[jax.ShapeDtypeStruct((b, n, d), F32), jax.ShapeDtypeStruct((b * n * ROW_SEGS, LANES), jnp.int32),
                   jax.ShapeDtypeStruct((b, n, LANES), F32)],
        grid=(b, n // tm),
        in_specs=in_specs,
        out_specs=[row_spec, pl.BlockSpec((tm * ROW_SEGS, LANES), lambda i, j: (i * (n // tm) + j, 0)),
                   pl.BlockSpec((1, tm, LANES), lambda i, j: (i, j, 0))],
        compiler_params=_cparams(("arbitrary", "arbitrary")),
        name="out_project",
    )(*args)


def _route_kernel(bias_ref, lg_ref, cls_ref, wlo_ref, whi_ref):
    score = [_sigmoid(lg_ref[e]) for e in range(N_EXPERTS)]
    sel = [score[e] + bias_ref[e] for e in range(N_EXPERTS)]
    grp_score = []
    for g in range(N_GROUPS):
        v = sel[g * PER_GROUP:(g + 1) * PER_GROUP]
        best = v[0] + v[1]
        for a in range(PER_GROUP):
            for c in range(a + 1, PER_GROUP):
                if (a, c) != (0, 1):
                    best = jnp.maximum(best, v[a] + v[c])
        grp_score.append(best)
    grp = jnp.zeros(grp_score[0].shape, jnp.int32)
    best = grp_score[0]
    for g in range(1, N_GROUPS):
        upd = grp_score[g] > best
        best = jnp.where(upd, grp_score[g], best)
        grp = jnp.where(upd, g, grp)

    def pick(vals, j):
        out = vals[j]
        for g in range(1, N_GROUPS):
            out = jnp.where(grp == g, vals[g * PER_GROUP + j], out)
        return out

    v = [pick(sel, j) for j in range(PER_GROUP)]
    sc = [pick(score, j) for j in range(PER_GROUP)]
    one = jnp.ones(grp.shape, jnp.int32)
    zero = jnp.zeros(grp.shape, jnp.int32)
    chosen = []
    for j in range(PER_GROUP):
        rank = zero
        for m in range(PER_GROUP):
            if m == j:
                continue
            ahead = (v[m] >= v[j]) if m < j else (v[m] > v[j])
            rank = rank + jnp.where(ahead, one, zero)
        chosen.append(rank < 2)
    code = zero
    for j in range(PER_GROUP):
        code = code + jnp.where(chosen[j], one * (1 << j), zero)
    pair = zero
    for idx in range(6):
        pair = jnp.where(code == (1 << PAIR_LO[idx]) + (1 << PAIR_HI[idx]), idx, pair)
    s_lo = jnp.where(chosen[0], sc[0], jnp.where(chosen[1], sc[1], sc[2]))
    s_hi = jnp.where(chosen[3], sc[3], jnp.where(chosen[2], sc[2], sc[1]))
    den = s_lo + s_hi
    cls_ref[...] = grp * 6 + pair
    wlo_ref[...] = s_lo / den
    whi_ref[...] = s_hi / den


def _route(logits_t, router_bias):
    _, r, _ = logits_t.shape
    full = pl.BlockSpec((r, LANES), lambda i: (0, 0))
    return pl.pallas_call(
        _route_kernel,
        out_shape=[jax.ShapeDtypeStruct((r, LANES), jnp.int32), jax.ShapeDtypeStruct((r, LANES), F32),
                   jax.ShapeDtypeStruct((r, LANES), F32)],
        grid=(1,),
        in_specs=[pl.BlockSpec(memory_space=pltpu.SMEM),
                  pl.BlockSpec((N_EXPERTS, r, LANES), lambda i: (0, 0, 0))],
        out_specs=[full, full, full],
        compiler_params=_cparams(("arbitrary",)),
        name="route",
    )(router_bias, logits_t)


def _gather_rows(table, idx):
    p = idx.shape[0]
    per_worker = p // SC_WORKERS
    n_chunks = per_worker // SC_GATHER_ROWS
    mesh = plsc.VectorSubcoreMesh(core_axis_name="c", subcore_axis_name="s", num_cores=SC_CORES,
                                  num_subcores=SC_SUBCORES)

    @functools.partial(
        pl.kernel, mesh=mesh,
        out_type=jax.ShapeDtypeStruct((p,) + table.shape[1:], table.dtype),
        scratch_types=[pltpu.VMEM((per_worker,), jnp.int32),
                       pltpu.VMEM((SC_GATHER_ROWS,) + table.shape[1:], table.dtype),
                       pltpu.SemaphoreType.DMA],
        compiler_params=pltpu.CompilerParams(use_tc_tiling_on_sc=True),
        name="gather_rows",
    )
    def gather(table_hbm, idx_hbm, out_hbm, idx_v, rows_v, sem):
        worker = lax.axis_index("s") * SC_CORES + lax.axis_index("c")
        base = worker * per_worker
        pltpu.sync_copy(idx_hbm.at[pl.ds(base, per_worker)], idx_v)

        @pl.loop(0, n_chunks)
        def _(i):
            off = pl.multiple_of(i * SC_GATHER_ROWS, SC_GATHER_ROWS)
            pltpu.async_copy(table_hbm.at[idx_v.at[pl.ds(off, SC_GATHER_ROWS)]], rows_v, sem).wait()
            pltpu.sync_copy(rows_v, out_hbm.at[pl.ds(base + off, SC_GATHER_ROWS)])

    return gather(table, idx)


def _scatter_rows(src, idx, n_out):
    t = idx.shape[0]
    per_worker = t // SC_WORKERS
    n_chunks = per_worker // SC_GATHER_ROWS
    mesh = plsc.VectorSubcoreMesh(core_axis_name="c", subcore_axis_name="s", num_cores=SC_CORES,
                                  num_subcores=SC_SUBCORES)

    @functools.partial(
        pl.kernel, mesh=mesh,
        out_type=jax.ShapeDtypeStruct((n_out,) + src.shape[1:], src.dtype),
        scratch_types=[pltpu.VMEM((n_chunks, SC_GATHER_ROWS), jnp.int32),
                       pltpu.VMEM((SC_GATHER_ROWS,) + src.shape[1:], src.dtype),
                       pltpu.SemaphoreType.DMA],
        compiler_params=pltpu.CompilerParams(use_tc_tiling_on_sc=True),
        name="scatter_rows",
    )
    def scatter(src_hbm, idx_hbm, out_hbm, idx_v, rows_v, sem):
        worker = lax.axis_index("s") * SC_CORES + lax.axis_index("c")
        base = worker * per_worker
        pltpu.sync_copy(idx_hbm.at[worker], idx_v)

        @pl.loop(0, n_chunks)
        def _(i):
            off = pl.multiple_of(i * SC_GATHER_ROWS, SC_GATHER_ROWS)
            pltpu.sync_copy(src_hbm.at[pl.ds(base + off, SC_GATHER_ROWS)], rows_v)
            pltpu.async_copy(rows_v, out_hbm.at[idx_v.at[i]], sem).wait()

    return scatter(src, idx.reshape(SC_WORKERS, n_chunks, SC_GATHER_ROWS))


def _cast_kernel(*refs):
    n = (len(refs) - 1) // 2
    for i in range(n):
        refs[n + 1 + i][0] = refs[i][0, 0].astype(BF16)


def _cast_expert_weights(layer, w1, w3, w2, after):
    ws = (w1, w3, w2)
    e = w1.shape[1]
    in_specs = [pl.BlockSpec((1, 1) + w.shape[2:], lambda j: (layer, j, 0, 0)) for w in ws]
    in_specs.append(pl.BlockSpec(memory_space=pl.ANY))
    return pl.pallas_call(
        _cast_kernel,
        out_shape=[jax.ShapeDtypeStruct(w.shape[1:], BF16) for w in ws],
        grid=(e,),
        in_specs=in_specs,
        out_specs=[pl.BlockSpec((1,) + w.shape[2:], lambda j: (j, 0, 0)) for w in ws],
        compiler_params=_cparams(("arbitrary",)),
        name="cast_expert_weights",
    )(*ws, after)


def _moe_kernel(lo_ref, hi_ref, nt_ref, x_ref, wr_ref, w1l_ref, w1h_ref, w3l_ref, w3h_ref, w2l_ref, w2h_ref, y_ref):
    j = pl.program_id(0)

    @pl.when(j < nt_ref[0])
    def _():
        x = _token_rows(x_ref).astype(BF16)
        wr = wr_ref[...]
        he_lo = (_silu(_dot(x, w1l_ref[0])) * _dot(x, w3l_ref[0]) * wr[:, 0:1]).astype(BF16)
        he_hi = (_silu(_dot(x, w1h_ref[0])) * _dot(x, w3h_ref[0]) * wr[:, 1:2]).astype(BF16)
        y = _dot(he_lo, w2l_ref[0]) + _dot(he_hi, w2h_ref[0])
        _store_token_rows(y_ref, y)

    @pl.when(j >= nt_ref[0])
    def _():
        y_ref[...] = jnp.zeros_like(y_ref)


def _moe_experts(xs, wrow, tile_lo, tile_hi, n_tiles, w1, w3, w2):
    tp = xs.shape[0] // ROW_SEGS
    tm = MOE_TILE
    d, de = w1.shape[1], w1.shape[2]

    def gate_map(j, lo, hi, nt):
        return (jnp.minimum(j, nt[0] - 1), 0)

    def lo_map(j, lo, hi, nt):
        return (lo[jnp.minimum(j, nt[0] - 1)], 0, 0)

    def hi_map(j, lo, hi, nt):
        return (hi[jnp.minimum(j, nt[0] - 1)], 0, 0)

    up = (1, d, de)
    down = (1, de, d)
    return pl.pallas_call(
        _moe_kernel,
        out_shape=jax.ShapeDtypeStruct((tp * ROW_SEGS, LANES), jnp.int32),
        grid_spec=pltpu.PrefetchScalarGridSpec(
            num_scalar_prefetch=3,
            grid=(tp // tm,),
            in_specs=[pl.BlockSpec((tm * ROW_SEGS, LANES), gate_map), pl.BlockSpec((tm, LANES), gate_map),
                      pl.BlockSpec(up, lo_map), pl.BlockSpec(up, hi_map),
                      pl.BlockSpec(up, lo_map), pl.BlockSpec(up, hi_map),
                      pl.BlockSpec(down, lo_map), pl.BlockSpec(down, hi_map)],
            out_specs=pl.BlockSpec((tm * ROW_SEGS, LANES), lambda j, lo, hi, nt: (j, 0)),
        ),
        compiler_params=_cparams(("arbitrary",)),
        name="moe_experts",
    )(tile_lo, tile_hi, n_tiles, xs, wrow, w1, w1, w3, w3, w2, w2)


def _moe(h2, logits, router_bias, w1, w3, w2):
    t = h2.shape[0]
    tm = MOE_TILE
    n_tiles_max = t // tm + N_CLASSES
    tp = n_tiles_max * tm
    lg_t = logits[:, :N_EXPERTS].T.reshape(N_EXPERTS, t // LANES, LANES)
    cls, wlo, whi = _route(lg_t, router_bias)
    cls, wlo, whi = cls.reshape(t), wlo.reshape(t), whi.reshape(t)
    classes = jnp.arange(N_CLASSES, dtype=jnp.int32)
    onehot = (cls[:, None] == classes[None, :]).astype(jnp.int32)
    csum = jnp.cumsum(onehot, axis=0)
    rank = jnp.sum(csum * onehot, axis=1) - 1
    counts = csum[-1]
    tiles_per = (counts + tm - 1) // tm
    tile_end = jnp.cumsum(tiles_per)
    tile_start = tile_end - tiles_per
    dest = jnp.sum(onehot * (tile_start * tm)[None, :], axis=1) + rank
    tiles = jnp.arange(n_tiles_max, dtype=jnp.int32)
    tile_cls = jnp.minimum(jnp.sum((tile_end[None, :] <= tiles[:, None]).astype(jnp.int32), axis=1), N_CLASSES - 1)
    pair = tile_cls % 6
    base = (tile_cls // 6) * PER_GROUP
    pair_onehot = (pair[:, None] == jnp.arange(6, dtype=jnp.int32)[None, :]).astype(jnp.int32)
    tile_lo = base + jnp.sum(pair_onehot * jnp.asarray(PAIR_LO, jnp.int32)[None, :], axis=1)
    tile_hi = base + jnp.sum(pair_onehot * jnp.asarray(PAIR_HI, jnp.int32)[None, :], axis=1)
    gates = jnp.pad(jnp.stack([wlo, whi], axis=1), ((0, 0), (0, LANES - 2))).reshape(t, 1, LANES)
    wrow = _scatter_rows(gates, dest, tp).reshape(tp, LANES)
    xs = _scatter_rows(h2, dest, tp).reshape(tp * ROW_SEGS, LANES)
    ys = _moe_experts(xs, wrow, tile_lo, tile_hi, tile_end[-1:], w1, w3, w2)
    return _gather_rows(ys.reshape(tp, ROW_SEGS, LANES), dest), dest


def _final_kernel(s_ref, y_ref, mod_ref, g_ref, o_ref):
    x = s_ref[0] + mod_ref[0, 0, 5:6, :] * _token_rows(y_ref)
    o_ref[0] = _rms(x, g_ref[...])


def _final(s, y, mod, final_g):
    b, n, d = s.shape
    tm = ROW_TILE
    skip = N_CTX // tm
    row_spec = pl.BlockSpec((1, tm, d), lambda i, j: (i, j + skip, 0))
    return pl.pallas_call(
        _final_kernel,
        out_shape=jax.ShapeDtypeStruct((b, n - N_CTX, d), F32),
        grid=(b, (n - N_CTX) // tm),
        in_specs=[row_spec, pl.BlockSpec((tm * ROW_SEGS, LANES), lambda i, j: (i * (n // tm) + j + skip, 0)),
                  pl.BlockSpec((1, 1, 6, d), lambda i, j: (i, 1, 0, 0)),
                  pl.BlockSpec((1, d), lambda i, j: (0, 0))],
        out_specs=pl.BlockSpec((1, tm, d), lambda i, j: (i, j, 0)),
        compiler_params=_cparams(("arbitrary", "arbitrary")),
        name="final_norm",
    )(s, y, mod, final_g.reshape(1, d))


def _rope_tables(seq):
    pos = jnp.arange(seq)
    row_pos, col_pos = pos // GRID_W, pos % GRID_W
    n = A_DH // 2
    inv = ROPE_BASE ** (-jnp.arange(0, n, 2, dtype=F32) / n)
    ang_r = row_pos.astype(F32)[:, None] * inv[None, :]
    ang_c = col_pos.astype(F32)[:, None] * inv[None, :]
    ang = jnp.concatenate([ang_r, ang_c], axis=-1)
    cos = jnp.tile(jnp.cos(ang), (1, 4))
    sin = jnp.tile(jnp.sin(ang), (1, 4))
    sin = jnp.concatenate([-sin[:, :LANES // 2], sin[:, LANES // 2:]], axis=-1)
    cos = jnp.concatenate([jnp.ones((N_CTX, LANES), F32), cos], axis=0)
    sin = jnp.concatenate([jnp.zeros((N_CTX, LANES), F32), sin], axis=0)
    return cos, sin


def _interleave_maps(w):
    d = w.shape[0]
    w = w.reshape(d, A_HEADS, 2, 2, 2, A_DH // 4)
    return w.transpose(0, 1, 4, 2, 3, 5).reshape(d, A_HEADS * LANES)


def _att_weights(w_in):
    a_qk = A_HEADS * 2 * A_DH
    a_v = A_HEADS * A_DV
    b_w = B_HEADS * B_DH
    qa = _interleave_maps(w_in[:, :a_qk]) * (A_DH ** -0.5 * LOG2_E)
    ka = _interleave_maps(w_in[:, a_qk:2 * a_qk])
    va = w_in[:, 2 * a_qk:2 * a_qk + a_v]
    o = 2 * a_qk + a_v
    qb = w_in[:, o:o + b_w] * (B_DH ** -0.5)
    rest = w_in[:, o + b_w:]
    return jnp.concatenate([qa, ka, va, qb, rest], axis=1).astype(BF16)


def _gla_weights(w_in):
    d = w_in.shape[0]
    n_main = 2 * C_HEADS * C_DK + 2 * C_HEADS * C_DV
    pad = jnp.zeros((d, LANES - 2 * C_RANK), w_in.dtype)
    return jnp.concatenate([w_in, pad], axis=1).astype(BF16), n_main


def kernel(x, c, ctx, c_ctx, w_mod, b_mod, norm_g, final_g, att_w_in, att_w_out, att_lambda, att_subln_g, na_bias,
           gla_w_in, gla_w_gate, gla_b_gate, gla_norm_g, gla_w_out, router_w, router_bias, moe_w1, moe_w3, moe_w2):
    b, seq, d = x.shape
    n = N_CTX + seq
    s = jnp.concatenate([ctx, x], axis=1)

    rows = b + 1
    rows_pad = -(-rows // 8) * 8
    cc = jnp.concatenate([c, c_ctx[None, :], jnp.zeros((rows_pad - rows, d), F32)], axis=0)
    mod_all = _mod_vectors(cc, w_mod, b_mod)
    mod_x = mod_all[:, :b].reshape(DEPTH, b, 1, 6, d)
    mod_c = jnp.broadcast_to(mod_all[:, b].reshape(DEPTH, 1, 1, 6, d), (DEPTH, b, 1, 6, d))
    mods = jnp.concatenate([mod_c, mod_x], axis=2)

    cos, sin = _rope_tables(seq)
    rw_hi = router_w.astype(BF16)
    rw_lo = (router_w - rw_hi.astype(F32)).astype(BF16)
    zpad = jnp.zeros((d, LANES - 2 * N_EXPERTS), BF16)
    rw = jnp.concatenate([rw_hi, rw_lo, zpad], axis=1)

    y = None
    order_after = router_bias
    for i in range(DEPTH):
        j = i // 2
        modp = mods[i - 1] if i else None
        if i % 2 == 0:
            lam_init = 0.8 - 0.6 * math.exp(-0.3 * i)
            w = _att_weights(att_w_in[j])
            outs = _project(s, y, modp, mods[i], norm_g[i, 0], w, cos, sin,
                            n_rope=2 * A_HEADS * LANES, n_bf16=w.shape[1])
            if i:
                s = outs[0]
            p = outs[-1]
            oa_x, oa_c = _diff_attention(p, att_lambda[j], att_subln_g[j], lam_init)
            ob_x, ob_c = _neighbourhood_attention(p, _na_bias_table(na_bias[j], seq // GRID_W))
            s, h2, logits = _out_project((oa_x, oa_c, ob_x, ob_c), att_w_out[j].astype(BF16), s, mods[i], norm_g[i, 1],
                                         rw, gla=False)
        else:
            w, n_main = _gla_weights(gla_w_in[j])
            outs = _project(s, y, modp, mods[i], norm_g[i, 0], w, None, None, n_rope=0, n_bf16=n_main)
            s, p, lr = outs
            wg = jnp.zeros((2, LANES, C_HEADS * C_DK), F32)
            wg = wg.at[0, :C_RANK].set(gla_w_gate[j, 0]).at[1, C_RANK:2 * C_RANK].set(gla_w_gate[j, 1])
            o = _gla(p, lr, wg, gla_b_gate[j])
            s, h2, logits = _out_project((o, p), gla_w_out[j].astype(BF16), s, mods[i], norm_g[i, 1],
                                         rw, gla=True, gn=gla_norm_g[j])
        if i == 0:
            expert_w = _cast_expert_weights(0, moe_w1, moe_w3, moe_w2, order_after)
        y, order_after = _moe(h2.reshape(b * n, ROW_SEGS, LANES), logits.reshape(b * n, LANES), router_bias, *expert_w)
        y = y.reshape(b * n * ROW_SEGS, LANES)
        if i + 1 < DEPTH:
            expert_w = _cast_expert_weights(i + 1, moe_w1, moe_w3, moe_w2, order_after)
    return _final(s, y, mods[DEPTH - 1], final_g)
```

```python
import functools
import math

import jax
import jax.numpy as jnp
import numpy as np
from jax import lax
from jax.experimental import pallas as pl
from jax.experimental.pallas import tpu as pltpu
from jax.experimental.pallas import tpu_sc as plsc

F32 = jnp.float32
BF16 = jnp.bfloat16

D_MODEL = 1024
DEPTH = 4
GRID_W = 64
N_CTX = 256
A_HEADS = 4
A_DH = 64
A_DV = 128
B_HEADS = 8
B_DH = 64
NA_ROWS = 8
NA_COLS = 16
C_HEADS = 4
C_DK = 128
C_DV = 256
C_RANK = 16
C_TAU = 16.0
C_CHUNK = 64
GLA_BLOCK = 256
N_EXPERTS = 16
N_GROUPS = 4
PER_GROUP = 4
ROPE_BASE = 10000.0
EPS = 1e-6
LOG2_E = math.log2(math.e)
ATT_KEY_BLOCK = 768
ATT_Q_HALVES = 4

LANES = 128
ROW_TILE = 256
MOE_TILE = 256
SC_CORES = 2
SC_SUBCORES = 16
SC_WORKERS = SC_CORES * SC_SUBCORES
SC_GATHER_ROWS = 64
N_CLASSES = N_GROUPS * 6
NA_QROWS = 16
NA_SUB = 2
NA_KROWS = NA_SUB + NA_ROWS - 1
VMEM_LIMIT = 52 * 1024 * 1024
ROW_SEGS = D_MODEL // LANES // 2

PAIR_LO = (0, 0, 0, 1, 1, 2)
PAIR_HI = (1, 2, 3, 2, 3, 3)


def _cparams(sem):
    return pltpu.CompilerParams(dimension_semantics=sem, vmem_limit_bytes=VMEM_LIMIT)


def _sigmoid(x):
    return 1.0 / (1.0 + jnp.exp(-x))


def _silu(x):
    return x * _sigmoid(x)


def _rms(x, g):
    return x * lax.rsqrt(jnp.mean(x * x, axis=-1, keepdims=True) + EPS) * g


def _dot(a, b):
    return jnp.dot(a, b, preferred_element_type=F32)


def _dot_nt(a, b):
    return lax.dot_general(a, b, (((1,), (1,)), ((), ())), preferred_element_type=F32)


def _dot_tn(a, b):
    return lax.dot_general(a, b, (((0,), (0,)), ((), ())), preferred_element_type=F32)


def _mod_kernel(c_ref, w_ref, b_ref, o_ref):
    a = _silu(c_ref[...]).astype(BF16)
    o_ref[0] = _dot(a, w_ref[0].astype(BF16)) + b_ref[0]


def _mod_vectors(cc, w_mod, b_mod):
    depth, d, n6 = w_mod.shape
    rows = cc.shape[0]
    tn = 1536
    return pl.pallas_call(
        _mod_kernel,
        out_shape=jax.ShapeDtypeStruct((depth, rows, n6), F32),
        grid=(depth, n6 // tn),
        in_specs=[
            pl.BlockSpec((rows, d), lambda i, j: (0, 0)),
            pl.BlockSpec((1, d, tn), lambda i, j: (i, 0, j)),
            pl.BlockSpec((1, 1, tn), lambda i, j: (i, 0, j)),
        ],
        out_specs=pl.BlockSpec((1, rows, tn), lambda i, j: (i, 0, j)),
        compiler_params=_cparams(("arbitrary", "arbitrary")),
        name="mod_vectors",
    )(cc, w_mod, b_mod.reshape(depth, 1, n6))


def _token_rows(ref):
    rows = ref.shape[0] // ROW_SEGS
    packed = jnp.concatenate([ref[pl.ds(sg, rows, stride=ROW_SEGS), :] for sg in range(ROW_SEGS)], axis=-1)
    low = lax.bitcast_convert_type(lax.shift_left(packed, jnp.int32(16)), F32)
    high = lax.bitcast_convert_type(packed & jnp.int32(-65536), F32)
    return jnp.concatenate([low, high], axis=-1)


def _store_token_rows(ref, val):
    rows, d = val.shape
    rounded = val.astype(BF16).astype(F32)
    bits = lax.bitcast_convert_type(rounded, jnp.int32)
    packed = lax.shift_right_logical(bits[:, :d // 2], jnp.int32(16)) | bits[:, d // 2:]
    for sg in range(ROW_SEGS):
        ref[pl.ds(sg, rows, stride=ROW_SEGS), :] = packed[:, sg * LANES:(sg + 1) * LANES]


def _proj_kernel(*refs, has_y, n_rope, n_bf16, col_chunk):
    it = iter(refs)
    s_ref = next(it)
    y_ref = next(it) if has_y else None
    modp_ref = next(it) if has_y else None
    mod_ref = next(it)
    ng_ref = next(it)
    w_ref = next(it)
    cos_ref = next(it) if n_rope else None
    sin_ref = next(it) if n_rope else None
    snew_ref = next(it) if has_y else None
    p_ref = next(it)
    lr_ref = next(it) if w_ref.shape[1] > n_bf16 else None

    x = s_ref[0]
    if has_y:
        x = x + modp_ref[0, 0, 5:6, :] * _token_rows(y_ref)
        snew_ref[0] = x
    h = _rms(x, ng_ref[...])
    h = h * (1.0 + mod_ref[0, 0, 1:2, :]) + mod_ref[0, 0, 0:1, :]
    hb = h.astype(BF16)
    n_out = w_ref.shape[1]
    for c0 in range(0, n_out, col_chunk):
        c1 = min(c0 + col_chunk, n_out)
        acc = _dot(hb, w_ref[:, c0:c1])
        for b0 in range(c0, c1, LANES):
            t = acc[:, b0 - c0:b0 - c0 + LANES]
            if b0 < n_rope:
                t = t * cos_ref[...] + pltpu.roll(t, LANES // 2, 1) * sin_ref[...]
            if b0 < n_bf16:
                p_ref[0, :, b0:b0 + LANES] = t.astype(BF16)
            else:
                lr_ref[0, :, b0 - n_bf16:b0 - n_bf16 + LANES] = t


def _project(s, y, modp, mod, ng, w, cos, sin, *, n_rope, n_bf16):
    b, n, d = s.shape
    n_out = w.shape[1]
    tm = ROW_TILE
    has_y = y is not None
    row_spec = pl.BlockSpec((1, tm, d), lambda i, j: (i, j, 0))
    mod_spec = pl.BlockSpec((1, 1, 6, d), lambda i, j: (i, jnp.minimum(j, 1), 0, 0))
    in_specs = [row_spec]
    args = [s]
    if has_y:
        in_specs += [pl.BlockSpec((tm * ROW_SEGS, LANES), lambda i, j: (i * (n // tm) + j, 0)), mod_spec]
        args += [y, modp]
    in_specs += [mod_spec, pl.BlockSpec((1, d), lambda i, j: (0, 0)),
                 pl.BlockSpec((d, n_out), lambda i, j: (0, 0))]
    args += [mod, ng.reshape(1, d), w]
    if n_rope:
        tab_spec = pl.BlockSpec((tm, LANES), lambda i, j: (j, 0))
        in_specs += [tab_spec, tab_spec]
        args += [cos, sin]
    out_shape, out_specs = [], []
    if has_y:
        out_shape.append(jax.ShapeDtypeStruct((b, n, d), F32))
        out_specs.append(row_spec)
    out_shape.append(jax.ShapeDtypeStruct((b, n, n_bf16), BF16))
    out_specs.append(pl.BlockSpec((1, tm, n_bf16), lambda i, j: (i, j, 0)))
    if n_out > n_bf16:
        out_shape.append(jax.ShapeDtypeStruct((b, n, n_out - n_bf16), F32))
        out_specs.append(pl.BlockSpec((1, tm, n_out - n_bf16), lambda i, j: (i, j, 0)))
    return pl.pallas_call(
        functools.partial(_proj_kernel, has_y=has_y, n_rope=n_rope, n_bf16=n_bf16, col_chunk=512),
        out_shape=out_shape,
        grid=(b, n // tm),
        in_specs=in_specs,
        out_specs=out_specs,
        compiler_params=_cparams(("arbitrary", "arbitrary")),
        name="norm_mod_project",
    )(*args)


def _softmax_rows(s):
    e = jnp.exp(s - jnp.max(s, axis=-1, keepdims=True))
    return e * (1.0 / jnp.sum(e, axis=-1, keepdims=True))


def _diff_attn_kernel(*refs, lam_init, n_q):
    q_refs = refs[:n_q]
    k_ref, v_ref, lam_ref, g_ref, o_ref = refs[n_q:]
    lp = lam_ref[...]
    lam = (jnp.exp(jnp.sum(lp[0:1] * lp[1:2], axis=-1, keepdims=True))
           - jnp.exp(jnp.sum(lp[2:3] * lp[3:4], axis=-1, keepdims=True)) + lam_init)
    q = jnp.concatenate([r[0] for r in q_refs], axis=0)
    lane = lax.broadcasted_iota(jnp.int32, (1, LANES), 1)
    first_map = (lane // 32) % 2 == 0
    zero = jnp.zeros_like(q)
    q0 = jnp.where(first_map, q, zero)
    q1 = jnp.where(first_map, zero, q)
    n_keys = k_ref.shape[1]
    blk = min(ATT_KEY_BLOCK, n_keys)
    ones = jnp.ones((blk, A_DV), BF16)
    state = [None, None]
    for kb in range(n_keys // blk):
        k = k_ref[0, kb * blk:(kb + 1) * blk, :]
        v1 = jnp.concatenate([v_ref[0, kb * blk:(kb + 1) * blk, :], ones], axis=1)
        for i, qm in enumerate((q0, q1)):
            s = _dot_nt(qm, k)
            m_new = jnp.max(s, axis=-1, keepdims=True)
            if kb:
                m_old, acc_old = state[i]
                m_new = jnp.maximum(m_old, m_new)
            acc = _dot(jnp.exp2(s - m_new).astype(BF16), v1)
            if kb:
                acc = acc_old * jnp.exp2(m_old - m_new) + acc
            state[i] = (m_new, acc)
    acc0, acc1 = state[0][1], state[1][1]
    o = acc0[:, :A_DV] * (1.0 / acc0[:, A_DV:]) - acc1[:, :A_DV] * (lam / acc1[:, A_DV:])
    o_ref[0] = (_rms(o, g_ref[...]) * (1.0 - lam_init)).astype(BF16)


def _diff_attention(p, lam_p, subln_g, lam_init):
    b, n, _ = p.shape
    half = ROW_TILE
    tq = ATT_Q_HALVES * half
    ctx_blocks = N_CTX // half
    width = A_HEADS * A_DV
    small = [pl.BlockSpec((4, A_DH), lambda i, h, *_: (0, 0)), pl.BlockSpec((1, A_DV), lambda i, h, *_: (0, 0))]
    q_specs = [pl.BlockSpec((1, half, LANES), functools.partial(lambda i, h, j, part: (i, ctx_blocks + ATT_Q_HALVES * j + part, h),
                                                                 part=part)) for part in range(ATT_Q_HALVES)]
    oa_x = pl.pallas_call(
        functools.partial(_diff_attn_kernel, lam_init=lam_init, n_q=ATT_Q_HALVES),
        out_shape=jax.ShapeDtypeStruct((b, n - N_CTX, width), BF16),
        grid=(b, A_HEADS, (n - N_CTX) // tq),
        in_specs=q_specs + [
            pl.BlockSpec((1, n, LANES), lambda i, h, j: (i, 0, A_HEADS + h)),
            pl.BlockSpec((1, n, LANES), lambda i, h, j: (i, 0, 2 * A_HEADS + h)),
        ] + small,
        out_specs=pl.BlockSpec((1, tq, LANES), lambda i, h, j: (i, j, h)),
        compiler_params=_cparams(("arbitrary", "arbitrary", "arbitrary")),
        name="diff_attention",
    )(*([p] * ATT_Q_HALVES), p, p, lam_p, subln_g.reshape(1, A_DV))
    oa_c = pl.pallas_call(
        functools.partial(_diff_attn_kernel, lam_init=lam_init, n_q=1),
        out_shape=jax.ShapeDtypeStruct((b, N_CTX, width), BF16),
        grid=(b, A_HEADS),
        in_specs=[
            pl.BlockSpec((1, N_CTX, LANES), lambda i, h: (i, 0, h)),
            pl.BlockSpec((1, N_CTX, LANES), lambda i, h: (i, 0, A_HEADS + h)),
            pl.BlockSpec((1, N_CTX, LANES), lambda i, h: (i, 0, 2 * A_HEADS + h)),
        ] + small,
        out_specs=pl.BlockSpec((1, N_CTX, LANES), lambda i, h: (i, 0, h)),
        compiler_params=_cparams(("arbitrary", "arbitrary")),
        name="context_diff_attention",
    )(p, p, p, lam_p, subln_g.reshape(1, A_DV))
    return oa_x, oa_c


def _split_heads(q):
    lane = lax.broadcasted_iota(jnp.int32, (1, LANES), 1)
    first_head = lane < B_DH
    zero = jnp.zeros_like(q)
    return first_head, (jnp.where(first_head, q, zero), jnp.where(first_head, zero, q))


def _na_ctx_kernel(q_ref, k_ref, v_ref, o_ref):
    first_head, qh = _split_heads(q_ref[0])
    kc = k_ref[0]
    vc = v_ref[0]
    outs = [_dot(_softmax_rows(_dot_nt(qh[h], kc)).astype(BF16), vc) for h in range(2)]
    o_ref[0] = jnp.where(first_head, outs[0], outs[1]).astype(BF16)


def _na_kernel(*refs):
    q_refs = refs[:-4]
    k_ref, v_ref, bm_ref, o_ref = refs[-4:]
    blk = pl.program_id(2)
    rows = (k_ref.shape[1] - N_CTX) // GRID_W
    sub_q = NA_SUB * GRID_W
    sub_k = NA_KROWS * GRID_W
    kc = k_ref[0, 0:N_CTX, :]
    vc = jnp.concatenate([v_ref[0, 0:N_CTX, :], jnp.ones((N_CTX, LANES), BF16)], axis=1)
    ones = jnp.ones((sub_k, LANES), BF16)
    for sub in range(NA_QROWS // NA_SUB):
        q_ref = q_refs[sub * sub_q // ROW_TILE]
        q0 = (sub * sub_q) % ROW_TILE
        first_head, qh = _split_heads(q_ref[0, q0:q0 + sub_q, :])
        k_row0 = jnp.clip(blk * NA_QROWS + sub * NA_SUB - NA_ROWS // 2, 0, rows - NA_KROWS)
        start = pl.multiple_of(N_CTX + k_row0 * GRID_W, GRID_W)
        kw = k_ref[0, pl.ds(start, sub_k), :]
        vw = jnp.concatenate([v_ref[0, pl.ds(start, sub_k), :], ones], axis=1)
        outs = []
        for h in range(2):
            s_loc = _dot_nt(qh[h], kw) + bm_ref[h, 0, sub]
            s_ctx = _dot_nt(qh[h], kc)
            m = jnp.maximum(jnp.max(s_loc, axis=-1, keepdims=True), jnp.max(s_ctx, axis=-1, keepdims=True))
            acc = _dot(jnp.exp(s_loc - m).astype(BF16), vw) + _dot(jnp.exp(s_ctx - m).astype(BF16), vc)
            outs.append(acc[:, :LANES] * (1.0 / acc[:, LANES:]))
        o_ref[0, sub * sub_q:(sub + 1) * sub_q, :] = jnp.where(first_head, outs[0], outs[1]).astype(BF16)


def _na_bias_table(na_bias, rows):
    h = na_bias.shape[0]
    n_dr, n_dc = 2 * NA_ROWS - 1, 2 * NA_COLS - 1
    width = 2 * GRID_W
    left = GRID_W - NA_COLS
    u = jnp.pad(na_bias, ((0, 0), (0, 0), (left, width - left - n_dc)))
    skew = jnp.tile(u, (1, 1, GRID_W))[:, :, :GRID_W * (width - 1)].reshape(h, n_dr, GRID_W, width - 1)
    toeplitz = skew[:, :, :, GRID_W - 1:]
    margin = NA_KROWS - NA_ROWS
    by_col = jnp.pad(toeplitz.transpose(0, 2, 1, 3), ((0, 0), (0, 0), (margin, margin), (0, 0)))
    by_col = by_col.reshape(h, GRID_W, (n_dr + 2 * margin) * GRID_W)
    col = np.arange(GRID_W)
    col_start = np.clip(col - NA_COLS // 2, 0, GRID_W - NA_COLS)
    col_ok = (col[None, :] >= col_start[:, None]) & (col[None, :] < col_start[:, None] + NA_COLS)
    n_sub = NA_QROWS // NA_SUB
    blocks = []
    for q_row0 in (0, NA_QROWS, rows - NA_QROWS):
        for sub in range(n_sub):
            k_row0 = int(np.clip(q_row0 + sub * NA_SUB - NA_ROWS // 2, 0, rows - NA_KROWS))
            kr = k_row0 + np.arange(NA_KROWS)
            for rq in range(NA_SUB):
                r = q_row0 + sub * NA_SUB + rq
                r0 = int(np.clip(r - NA_ROWS // 2, 0, rows - NA_ROWS))
                row_ok = (kr >= r0) & (kr < r0 + NA_ROWS)
                ok = (col_ok[:, None, :] & row_ok[None, :, None]).reshape(GRID_W, NA_KROWS * GRID_W)
                first = k_row0 - r + NA_ROWS - 1 + margin
                window = by_col[:, :, first * GRID_W:(first + NA_KROWS) * GRID_W]
                blocks.append(jnp.where(ok[None], window, -jnp.inf))
    table = jnp.stack(blocks, axis=1)
    return table.reshape(h, 3, n_sub, NA_SUB * GRID_W, NA_KROWS * GRID_W)


def _neighbourhood_attention(p, bm):
    b, n, _ = p.shape
    tq = NA_QROWS * GRID_W
    parts = tq // ROW_TILE
    n_blocks = (n - N_CTX) // tq
    ctx_blocks = N_CTX // ROW_TILE
    col0 = 3 * A_HEADS
    pairs = B_HEADS // 2
    width = B_HEADS * B_DH

    def pattern(j):
        return jnp.minimum(j, 1) + (j == n_blocks - 1).astype(jnp.int32)

    ob_x = pl.pallas_call(
        _na_kernel,
        out_shape=jax.ShapeDtypeStruct((b, n - N_CTX, width), BF16),
        grid=(b, pairs, n_blocks),
        in_specs=[pl.BlockSpec((1, ROW_TILE, LANES),
                               functools.partial(lambda i, h, j, part: (i, ctx_blocks + parts * j + part, col0 + h),
                                                 part=part)) for part in range(parts)] + [
            pl.BlockSpec((1, n, LANES), lambda i, h, j: (i, 0, col0 + pairs + h)),
            pl.BlockSpec((1, n, LANES), lambda i, h, j: (i, 0, col0 + 2 * pairs + h)),
            pl.BlockSpec((2, 1, NA_QROWS // NA_SUB, NA_SUB * GRID_W, NA_KROWS * GRID_W),
                         lambda i, h, j: (h, pattern(j), 0, 0, 0)),
        ],
        out_specs=pl.BlockSpec((1, tq, LANES), lambda i, h, j: (i, j, h)),
        compiler_params=_cparams(("arbitrary", "arbitrary", "arbitrary")),
        name="neighbourhood_attention",
    )(*([p] * parts), p, p, bm)
    ob_c = pl.pallas_call(
        _na_ctx_kernel,
        out_shape=jax.ShapeDtypeStruct((b, N_CTX, width), BF16),
        grid=(b, pairs),
        in_specs=[
            pl.BlockSpec((1, N_CTX, LANES), lambda i, h: (i, 0, col0 + h)),
            pl.BlockSpec((1, N_CTX, LANES), lambda i, h: (i, 0, col0 + pairs + h)),
            pl.BlockSpec((1, N_CTX, LANES), lambda i, h: (i, 0, col0 + 2 * pairs + h)),
        ],
        out_specs=pl.BlockSpec((1, N_CTX, LANES), lambda i, h: (i, 0, h)),
        compiler_params=_cparams(("arbitrary", "arbitrary")),
        name="context_attention",
    )(p, p, p)
    return ob_x, ob_c


def _gla_kernel(q_ref, k_ref, v_ref, lr_ref, wg_ref, bg_ref, o_ref, ob_ref, g_ref, sf_ref, sb_ref):
    n = q_ref.shape[1]
    blk = GLA_BLOCK
    per_blk = blk // C_CHUNK
    n_blocks = n // blk
    ri = lax.broadcasted_iota(jnp.int32, (blk, blk), 0)
    ci = lax.broadcasted_iota(jnp.int32, (blk, blk), 1)
    same_chunk = (ri // C_CHUNK) == (ci // C_CHUNK)
    keeps = (same_chunk & (ci <= ri), same_chunk & (ci >= ri))
    tris = tuple(jnp.where(kp, 1.0, 0.0).astype(BF16) for kp in keeps)
    w_gate = jnp.concatenate([wg_ref[0], wg_ref[1]], axis=1).astype(BF16)
    b_gate = jnp.concatenate([bg_ref[0:1, :], bg_ref[1:2, :]], axis=1)
    for i in range(n_blocks):
        z = _dot(lr_ref[0, i * blk:(i + 1) * blk, :].astype(BF16), w_gate) + b_gate
        g_ref[i * blk:(i + 1) * blk, :] = (jnp.minimum(z, 0.0) - jnp.log(1.0 + jnp.exp(-jnp.abs(z)))) * (1.0 / C_TAU)

    def block(sb, direction, st_ref):
        keep = keeps[direction]
        end_row = C_CHUNK - 1 if direction == 0 else 0
        mid_row = C_CHUNK // 2 - 1 if direction == 0 else C_CHUNK // 2
        r0 = sb * blk
        g = g_ref[r0:r0 + blk, direction * C_DK:(direction + 1) * C_DK]
        g_hi = g.astype(BF16)
        g_lo = (g - g_hi.astype(F32)).astype(BF16)
        gc2 = _dot(tris[direction], jnp.concatenate([g_hi, g_lo], axis=1))
        gc = (gc2[:, :C_DK] + gc2[:, C_DK:]).reshape(per_blk, C_CHUNK, C_DK)
        g_end = gc[:, end_row:end_row + 1, :]
        g_mid = gc[:, mid_row:mid_row + 1, :]
        q = (q_ref[0, r0:r0 + blk, :].astype(F32) * (C_DK ** -0.5)).reshape(per_blk, C_CHUNK, C_DK)
        k = k_ref[0, r0:r0 + blk, :].astype(F32).reshape(per_blk, C_CHUNK, C_DK)
        v = v_ref[0, r0:r0 + blk, :]
        q_in = (q * jnp.exp(gc)).astype(BF16).reshape(blk, C_DK)
        q_mid = (q * jnp.exp(gc - g_mid)).astype(BF16).reshape(blk, C_DK)
        k_mid = (k * jnp.exp(g_mid - gc)).astype(BF16).reshape(blk, C_DK)
        k_end = (k * jnp.exp(g_end - gc)).astype(BF16).reshape(blk, C_DK)
        a = jnp.where(keep, _dot_nt(q_mid, k_mid), 0.0)
        o_intra = _dot(a.astype(BF16), v)
        ends = jnp.concatenate([g_end.reshape(per_blk, C_DK), jnp.zeros((8 - per_blk, C_DK), F32)], axis=0)
        decay = jnp.transpose(jnp.exp(ends))
        o_inter = [None] * per_blk
        for c in (range(per_blk) if direction == 0 else reversed(range(per_blk))):
            rows = slice(c * C_CHUNK, (c + 1) * C_CHUNK)
            st = st_ref[...]
            o_inter[c] = _dot(q_in[rows], st.astype(BF16))
            st_ref[...] = st * decay[:, c:c + 1] + _dot_tn(k_end[rows], v[rows])
        return o_intra + jnp.concatenate(o_inter, axis=0)

    sf_ref[...] = jnp.zeros_like(sf_ref)
    sb_ref[...] = jnp.zeros_like(sb_ref)
    ctx_blocks = N_CTX // blk
    order_b = list(reversed(range(ctx_blocks))) + list(reversed(range(ctx_blocks, n_blocks)))
    for i in range(n_blocks):
        o_ref[0, i * blk:(i + 1) * blk, :] = block(i, 0, sf_ref)
        sb = order_b[i]
        ob_ref[sb * blk:(sb + 1) * blk, :] = block(sb, 1, sb_ref)
    o_ref[0] = o_ref[0] + ob_ref[...]


def _gla(p, lr, wg, bg):
    b, n, _ = p.shape
    return pl.pallas_call(
        _gla_kernel,
        out_shape=jax.ShapeDtypeStruct((b, n, C_HEADS * C_DV), F32),
        grid=(b, C_HEADS),
        in_specs=[
            pl.BlockSpec((1, n, C_DK), lambda i, h: (i, 0, h)),
            pl.BlockSpec((1, n, C_DK), lambda i, h: (i, 0, C_HEADS + h)),
            pl.BlockSpec((1, n, C_DV), lambda i, h: (i, 0, C_HEADS + h)),
            pl.BlockSpec((1, n, LANES), lambda i, h: (i, 0, 0)),
            pl.BlockSpec((2, LANES, C_DK), lambda i, h: (0, 0, h)),
            pl.BlockSpec((2, C_DK), lambda i, h: (0, h)),
        ],
        out_specs=pl.BlockSpec((1, n, C_DV), lambda i, h: (i, 0, h)),
        scratch_shapes=[pltpu.VMEM((n, C_DV), F32), pltpu.VMEM((n, 2 * C_DK), F32),
                        pltpu.VMEM((C_DK, C_DV), F32), pltpu.VMEM((C_DK, C_DV), F32)],
        compiler_params=_cparams(("arbitrary", "arbitrary")),
        name="gla",
    )(p, p, p, lr, wg, bg)


def _out_kernel(*refs, gla):
    if gla:
        o_ref, gate_ref, gn_ref, w_ref, s_ref, mod_ref, ng_ref, rw_ref, snew_ref, h2_ref, lg_ref = refs
        o = o_ref[0]
        gate = gate_ref[0].astype(F32)
        parts = []
        for hd in range(C_HEADS):
            oh = _rms(o[:, hd * C_DV:(hd + 1) * C_DV], gn_ref[...])
            parts.append((oh * _silu(gate[:, hd * C_DV:(hd + 1) * C_DV])).astype(BF16))
        acc = _dot(jnp.concatenate(parts, axis=-1), w_ref[...])
    else:
        oax_ref, oac_ref, obx_ref, obc_ref, w_ref, s_ref, mod_ref, ng_ref, rw_ref, snew_ref, h2_ref, lg_ref = refs
        half = oax_ref.shape[2]
        is_ctx = pl.program_id(1) == 0
        oa = jnp.where(is_ctx, oac_ref[0], oax_ref[0])
        ob = jnp.where(is_ctx, obc_ref[0], obx_ref[0])
        acc = _dot(oa, w_ref[0:half, :]) + _dot(ob, w_ref[half:, :])
    x = s_ref[0] + mod_ref[0, 0, 2:3, :] * acc
    snew_ref[0] = x
    h2 = _rms(x, ng_ref[...]) * (1.0 + mod_ref[0, 0, 4:5, :]) + mod_ref[0, 0, 3:4, :]
    _store_token_rows(h2_ref, h2)
    t = _dot(h2.astype(BF16), rw_ref[...])
    lg_ref[0] = t + pltpu.roll(t, LANES - N_EXPERTS, 1)


def _out_project(mix, w_out, s, mod, ng, rw, *, gla, gn=None):
    b, n, d = s.shape
    tm = ROW_TILE
    row_spec = pl.BlockSpec((1, tm, d), lambda i, j: (i, j, 0))
    const2 = lambda i, j: (0, 0)
    if gla:
        o, p = mix
        in_specs = [row_spec, pl.BlockSpec((1, tm, d), lambda i, j: (i, j, 2)),
                    pl.BlockSpec((1, C_DV), const2)]
        args = [o, p, gn.reshape(1, C_DV)]
    else:
        oa_x, oa_c, ob_x, ob_c = mix
        half = oa_x.shape[2]
        latent = pl.BlockSpec((1, tm, half), lambda i, j: (i, jnp.maximum(j - 1, 0), 0))
        context = pl.BlockSpec((1, tm, half), lambda i, j: (i, 0, 0))
        in_specs = [latent, context, latent, context]
        args = [oa_x, oa_c, ob_x, ob_c]
    in_specs += [pl.BlockSpec((d, d), const2), row_spec,
                 pl.BlockSpec((1, 1, 6, d), lambda i, j: (i, jnp.minimum(j, 1), 0, 0)),
                 pl.BlockSpec((1, d), const2), pl.BlockSpec((d, LANES), const2)]
    args += [w_out, s, mod, ng.reshape(1, d), rw]
    return pl.pallas_call(
        functools.partial(_out_kernel, gla=gla),
        out_shape=[jax.ShapeDtypeStruct((b, n, d), F32), jax.ShapeDtypeStruct((b * n * ROW_SEGS, LANES), jnp.int32),
                   jax.ShapeDtypeStruct((b, n, LANES), F32)],
        grid=(b, n // tm),
        in_specs=in_specs,
        out_specs=[row_spec, pl.BlockSpec((tm * ROW_SEGS, LANES), lambda i, j: (i * (n // tm) + j, 0)),
                   pl.BlockSpec((1, tm, LANES), lambda i, j: (i, j, 0))],
        compiler_params=_cparams(("arbitrary", "arbitrary")),
        name="out_project",
    )(*args)


def _route_kernel(bias_ref, lg_ref, cls_ref, wlo_ref, whi_ref, rank_ref, cnt_ref):
    score = [_sigmoid(lg_ref[e]) for e in range(N_EXPERTS)]
    sel = [score[e] + bias_ref[e] for e in range(N_EXPERTS)]
    grp_score = []
    for g in range(N_GROUPS):
        v = sel[g * PER_GROUP:(g + 1) * PER_GROUP]
        best = v[0] + v[1]
        for a in range(PER_GROUP):
            for c in range(a + 1, PER_GROUP):
                if (a, c) != (0, 1):
                    best = jnp.maximum(best, v[a] + v[c])
        grp_score.append(best)
    grp = jnp.zeros(grp_score[0].shape, jnp.int32)
    best = grp_score[0]
    for g in range(1, N_GROUPS):
        upd = grp_score[g] > best
        best = jnp.where(upd, grp_score[g], best)
        grp = jnp.where(upd, g, grp)

    def pick(vals, j):
        out = vals[j]
        for g in range(1, N_GROUPS):
            out = jnp.where(grp == g, vals[g * PER_GROUP + j], out)
        return out

    v = [pick(sel, j) for j in range(PER_GROUP)]
    sc = [pick(score, j) for j in range(PER_GROUP)]
    one = jnp.ones(grp.shape, jnp.int32)
    zero = jnp.zeros(grp.shape, jnp.int32)
    chosen = []
    for j in range(PER_GROUP):
        rank = zero
        for m in range(PER_GROUP):
            if m == j:
                continue
            ahead = (v[m] >= v[j]) if m < j else (v[m] > v[j])
            rank = rank + jnp.where(ahead, one, zero)
        chosen.append(rank < 2)
    code = zero
    for j in range(PER_GROUP):
        code = code + jnp.where(chosen[j], one * (1 << j), zero)
    pair = zero
    for idx in range(6):
        pair = jnp.where(code == (1 << PAIR_LO[idx]) + (1 << PAIR_HI[idx]), idx, pair)
    s_lo = jnp.where(chosen[0], sc[0], jnp.where(chosen[1], sc[1], sc[2]))
    s_hi = jnp.where(chosen[3], sc[3], jnp.where(chosen[2], sc[2], sc[1]))
    den = s_lo + s_hi
    cls = grp * 6 + pair
    cls_ref[...] = cls
    wlo_ref[...] = s_lo / den
    whi_ref[...] = s_hi / den
    r = cls.shape[0]
    lane = lax.broadcasted_iota(jnp.int32, (1, LANES), 1)
    li = lax.broadcasted_iota(jnp.int32, (LANES, LANES), 0)
    lj = lax.broadcasted_iota(jnp.int32, (LANES, LANES), 1)
    upto_lane = jnp.where(li <= lj, 1.0, 0.0).astype(BF16)
    ri = lax.broadcasted_iota(jnp.int32, (r, r), 0)
    rj = lax.broadcasted_iota(jnp.int32, (r, r), 1)
    earlier_rows = jnp.where(rj < ri, 1.0, 0.0).astype(BF16)
    member = [jnp.where(cls == c, 1.0, 0.0) for c in range(N_CLASSES)]
    row_totals = jnp.zeros((r, LANES), F32)
    for c in range(N_CLASSES):
        row_totals = row_totals + jnp.where(lane == c, jnp.sum(member[c], axis=1, keepdims=True), 0.0)
    before_row = _dot(earlier_rows, row_totals.astype(BF16))
    rank = jnp.zeros((r, LANES), F32)
    for c in range(N_CLASSES):
        in_row = _dot(member[c].astype(BF16), upto_lane)
        offset = jnp.sum(jnp.where(lane == c, before_row, 0.0), axis=1, keepdims=True)
        rank = rank + member[c] * (in_row + offset - 1.0)
    rank_ref[...] = rank.astype(jnp.int32)
    counts = before_row[r - 1:r, :] + row_totals[r - 1:r, :]
    cnt_ref[...] = jnp.broadcast_to(counts, cnt_ref.shape).astype(jnp.int32)


def _route(logits_t, router_bias):
    _, r, _ = logits_t.shape
    full = pl.BlockSpec((r, LANES), lambda i: (0, 0))
    return pl.pallas_call(
        _route_kernel,
        out_shape=[jax.ShapeDtypeStruct((r, LANES), jnp.int32), jax.ShapeDtypeStruct((r, LANES), F32),
                   jax.ShapeDtypeStruct((r, LANES), F32), jax.ShapeDtypeStruct((r, LANES), jnp.int32),
                   jax.ShapeDtypeStruct((8, LANES), jnp.int32)],
        grid=(1,),
        in_specs=[pl.BlockSpec(memory_space=pltpu.SMEM),
                  pl.BlockSpec((N_EXPERTS, r, LANES), lambda i: (0, 0, 0))],
        out_specs=[full, full, full, full, pl.BlockSpec((8, LANES), lambda i: (0, 0))],
        compiler_params=_cparams(("arbitrary",)),
        name="route",
    )(router_bias, logits_t)


def _gather_rows(table, idx):
    p = idx.shape[0]
    per_worker = p // SC_WORKERS
    n_chunks = per_worker // SC_GATHER_ROWS
    mesh = plsc.VectorSubcoreMesh(core_axis_name="c", subcore_axis_name="s", num_cores=SC_CORES,
                                  num_subcores=SC_SUBCORES)

    @functools.partial(
        pl.kernel, mesh=mesh,
        out_type=jax.ShapeDtypeStruct((p,) + table.shape[1:], table.dtype),
        scratch_types=[pltpu.VMEM((per_worker,), jnp.int32),
                       pltpu.VMEM((SC_GATHER_ROWS,) + table.shape[1:], table.dtype),
                       pltpu.SemaphoreType.DMA],
        compiler_params=pltpu.CompilerParams(use_tc_tiling_on_sc=True),
        name="gather_rows",
    )
    def gather(table_hbm, idx_hbm, out_hbm, idx_v, rows_v, sem):
        worker = lax.axis_index("s") * SC_CORES + lax.axis_index("c")
        base = worker * per_worker
        pltpu.sync_copy(idx_hbm.at[pl.ds(base, per_worker)], idx_v)

        @pl.loop(0, n_chunks)
        def _(i):
            off = pl.multiple_of(i * SC_GATHER_ROWS, SC_GATHER_ROWS)
            pltpu.async_copy(table_hbm.at[idx_v.at[pl.ds(off, SC_GATHER_ROWS)]], rows_v, sem).wait()
            pltpu.sync_copy(rows_v, out_hbm.at[pl.ds(base + off, SC_GATHER_ROWS)])

    return gather(table, idx)


def _scatter_rows(src, idx, n_out):
    t = idx.shape[0]
    per_worker = t // SC_WORKERS
    n_chunks = per_worker // SC_GATHER_ROWS
    mesh = plsc.VectorSubcoreMesh(core_axis_name="c", subcore_axis_name="s", num_cores=SC_CORES,
                                  num_subcores=SC_SUBCORES)

    @functools.partial(
        pl.kernel, mesh=mesh,
        out_type=jax.ShapeDtypeStruct((n_out,) + src.shape[1:], src.dtype),
        scratch_types=[pltpu.VMEM((n_chunks, SC_GATHER_ROWS), jnp.int32),
                       pltpu.VMEM((SC_GATHER_ROWS,) + src.shape[1:], src.dtype),
                       pltpu.SemaphoreType.DMA],
        compiler_params=pltpu.CompilerParams(use_tc_tiling_on_sc=True),
        name="scatter_rows",
    )
    def scatter(src_hbm, idx_hbm, out_hbm, idx_v, rows_v, sem):
        worker = lax.axis_index("s") * SC_CORES + lax.axis_index("c")
        base = worker * per_worker
        pltpu.sync_copy(idx_hbm.at[worker], idx_v)

        @pl.loop(0, n_chunks)
        def _(i):
            off = pl.multiple_of(i * SC_GATHER_ROWS, SC_GATHER_ROWS)
            pltpu.sync_copy(src_hbm.at[pl.ds(base + off, SC_GATHER_ROWS)], rows_v)
            pltpu.async_copy(rows_v, out_hbm.at[idx_v.at[i]], sem).wait()

    return scatter(src, idx.reshape(SC_WORKERS, n_chunks, SC_GATHER_ROWS))


def _cast_kernel(*refs):
    n = (len(refs) - 1) // 2
    for i in range(n):
        refs[n + 1 + i][0] = refs[i][0, 0].astype(BF16)


def _cast_expert_weights(layer, w1, w3, w2, after):
    ws = (w1, w3, w2)
    e = w1.shape[1]
    in_specs = [pl.BlockSpec((1, 1) + w.shape[2:], lambda j: (layer, j, 0, 0)) for w in ws]
    in_specs.append(pl.BlockSpec(memory_space=pl.ANY))
    return pl.pallas_call(
        _cast_kernel,
        out_shape=[jax.ShapeDtypeStruct(w.shape[1:], BF16) for w in ws],
        grid=(e,),
        in_specs=in_specs,
        out_specs=[pl.BlockSpec((1,) + w.shape[2:], lambda j: (j, 0, 0)) for w in ws],
        compiler_params=_cparams(("arbitrary",)),
        name="cast_expert_weights",
    )(*ws, after)


def _moe_kernel(lo_ref, hi_ref, nt_ref, x_ref, wr_ref, w1l_ref, w1h_ref, w3l_ref, w3h_ref, w2l_ref, w2h_ref, y_ref):
    j = pl.program_id(0)

    @pl.when(j < nt_ref[0])
    def _():
        x = _token_rows(x_ref).astype(BF16)
        wr = wr_ref[...]
        he_lo = (_silu(_dot(x, w1l_ref[0])) * _dot(x, w3l_ref[0]) * wr[:, 0:1]).astype(BF16)
        he_hi = (_silu(_dot(x, w1h_ref[0])) * _dot(x, w3h_ref[0]) * wr[:, 1:2]).astype(BF16)
        y = _dot(he_lo, w2l_ref[0]) + _dot(he_hi, w2h_ref[0])
        _store_token_rows(y_ref, y)

    @pl.when(j >= nt_ref[0])
    def _():
        y_ref[...] = jnp.zeros_like(y_ref)


def _moe_experts(xs, wrow, tile_lo, tile_hi, n_tiles, w1, w3, w2):
    tp = xs.shape[0] // ROW_SEGS
    tm = MOE_TILE
    d, de = w1.shape[1], w1.shape[2]

    def gate_map(j, lo, hi, nt):
        return (jnp.minimum(j, nt[0] - 1), 0)

    def lo_map(j, lo, hi, nt):
        return (lo[jnp.minimum(j, nt[0] - 1)], 0, 0)

    def hi_map(j, lo, hi, nt):
        return (hi[jnp.minimum(j, nt[0] - 1)], 0, 0)

    up = (1, d, de)
    down = (1, de, d)
    return pl.pallas_call(
        _moe_kernel,
        out_shape=jax.ShapeDtypeStruct((tp * ROW_SEGS, LANES), jnp.int32),
        grid_spec=pltpu.PrefetchScalarGridSpec(
            num_scalar_prefetch=3,
            grid=(tp // tm,),
            in_specs=[pl.BlockSpec((tm * ROW_SEGS, LANES), gate_map), pl.BlockSpec((tm, LANES), gate_map),
                      pl.BlockSpec(up, lo_map), pl.BlockSpec(up, hi_map),
                      pl.BlockSpec(up, lo_map), pl.BlockSpec(up, hi_map),
                      pl.BlockSpec(down, lo_map), pl.BlockSpec(down, hi_map)],
            out_specs=pl.BlockSpec((tm * ROW_SEGS, LANES), lambda j, lo, hi, nt: (j, 0)),
        ),
        compiler_params=_cparams(("arbitrary",)),
        name="moe_experts",
    )(tile_lo, tile_hi, n_tiles, xs, wrow, w1, w1, w3, w3, w2, w2)


def _moe(h2, logits, router_bias, w1, w3, w2):
    t = h2.shape[0]
    tm = MOE_TILE
    n_tiles_max = t // tm + N_CLASSES
    tp = n_tiles_max * tm
    lg_t = logits[:, :N_EXPERTS].T.reshape(N_EXPERTS, t // LANES, LANES)
    cls, wlo, whi, rank, counts = _route(lg_t, router_bias)
    cls, wlo, whi, rank = cls.reshape(t), wlo.reshape(t), whi.reshape(t), rank.reshape(t)
    counts = counts[0, :N_CLASSES]
    classes = jnp.arange(N_CLASSES, dtype=jnp.int32)
    onehot = (cls[:, None] == classes[None, :]).astype(jnp.int32)
    tiles_per = (counts + tm - 1) // tm
    tile_end = jnp.cumsum(tiles_per)
    tile_start = tile_end - tiles_per
    dest = jnp.sum(onehot * (tile_start * tm)[None, :], axis=1) + rank
    tiles = jnp.arange(n_tiles_max, dtype=jnp.int32)
    tile_cls = jnp.minimum(jnp.sum((tile_end[None, :] <= tiles[:, None]).astype(jnp.int32), axis=1), N_CLASSES - 1)
    pair = tile_cls % 6
    base = (tile_cls // 6) * PER_GROUP
    pair_onehot = (pair[:, None] == jnp.arange(6, dtype=jnp.int32)[None, :]).astype(jnp.int32)
    tile_lo = base + jnp.sum(pair_onehot * jnp.asarray(PAIR_LO, jnp.int32)[None, :], axis=1)
    tile_hi = base + jnp.sum(pair_onehot * jnp.asarray(PAIR_HI, jnp.int32)[None, :], axis=1)
    gates = jnp.pad(jnp.stack([wlo, whi], axis=1), ((0, 0), (0, LANES - 2))).reshape(t, 1, LANES)
    wrow = _scatter_rows(gates, dest, tp).reshape(tp, LANES)
    xs = _scatter_rows(h2, dest, tp).reshape(tp * ROW_SEGS, LANES)
    ys = _moe_experts(xs, wrow, tile_lo, tile_hi, tile_end[-1:], w1, w3, w2)
    return _gather_rows(ys.reshape(tp, ROW_SEGS, LANES), dest), dest


def _final_kernel(s_ref, y_ref, mod_ref, g_ref, o_ref):
    x = s_ref[0] + mod_ref[0, 0, 5:6, :] * _token_rows(y_ref)
    o_ref[0] = _rms(x, g_ref[...])


def _final(s, y, mod, final_g):
    b, n, d = s.shape
    tm = ROW_TILE
    skip = N_CTX // tm
    row_spec = pl.BlockSpec((1, tm, d), lambda i, j: (i, j + skip, 0))
    return pl.pallas_call(
        _final_kernel,
        out_shape=jax.ShapeDtypeStruct((b, n - N_CTX, d), F32),
        grid=(b, (n - N_CTX) // tm),
        in_specs=[row_spec, pl.BlockSpec((tm * ROW_SEGS, LANES), lambda i, j: (i * (n // tm) + j + skip, 0)),
                  pl.BlockSpec((1, 1, 6, d), lambda i, j: (i, 1, 0, 0)),
                  pl.BlockSpec((1, d), lambda i, j: (0, 0))],
        out_specs=pl.BlockSpec((1, tm, d), lambda i, j: (i, j, 0)),
        compiler_params=_cparams(("arbitrary", "arbitrary")),
        name="final_norm",
    )(s, y, mod, final_g.reshape(1, d))


def _rope_tables(seq):
    pos = jnp.arange(seq)
    row_pos, col_pos = pos // GRID_W, pos % GRID_W
    n = A_DH // 2
    inv = ROPE_BASE ** (-jnp.arange(0, n, 2, dtype=F32) / n)
    ang_r = row_pos.astype(F32)[:, None] * inv[None, :]
    ang_c = col_pos.astype(F32)[:, None] * inv[None, :]
    ang = jnp.concatenate([ang_r, ang_c], axis=-1)
    cos = jnp.tile(jnp.cos(ang), (1, 4))
    sin = jnp.tile(jnp.sin(ang), (1, 4))
    sin = jnp.concatenate([-sin[:, :LANES // 2], sin[:, LANES // 2:]], axis=-1)
    cos = jnp.concatenate([jnp.ones((N_CTX, LANES), F32), cos], axis=0)
    sin = jnp.concatenate([jnp.zeros((N_CTX, LANES), F32), sin], axis=0)
    return cos, sin


def _interleave_maps(w):
    d = w.shape[0]
    w = w.reshape(d, A_HEADS, 2, 2, 2, A_DH // 4)
    return w.transpose(0, 1, 4, 2, 3, 5).reshape(d, A_HEADS * LANES)


def _att_weights(w_in):
    a_qk = A_HEADS * 2 * A_DH
    a_v = A_HEADS * A_DV
    b_w = B_HEADS * B_DH
    qa = _interleave_maps(w_in[:, :a_qk]) * (A_DH ** -0.5 * LOG2_E)
    ka = _interleave_maps(w_in[:, a_qk:2 * a_qk])
    va = w_in[:, 2 * a_qk:2 * a_qk + a_v]
    o = 2 * a_qk + a_v
    qb = w_in[:, o:o + b_w] * (B_DH ** -0.5)
    rest = w_in[:, o + b_w:]
    return jnp.concatenate([qa, ka, va, qb, rest], axis=1).astype(BF16)


def _gla_weights(w_in):
    d = w_in.shape[0]
    n_main = 2 * C_HEADS * C_DK + 2 * C_HEADS * C_DV
    pad = jnp.zeros((d, LANES - 2 * C_RANK), w_in.dtype)
    return jnp.concatenate([w_in, pad], axis=1).astype(BF16), n_main


def kernel(x, c, ctx, c_ctx, w_mod, b_mod, norm_g, final_g, att_w_in, att_w_out, att_lambda, att_subln_g, na_bias,
           gla_w_in, gla_w_gate, gla_b_gate, gla_norm_g, gla_w_out, router_w, router_bias, moe_w1, moe_w3, moe_w2):
    b, seq, d = x.shape
    n = N_CTX + seq
    s = jnp.concatenate([ctx, x], axis=1)

    rows = b + 1
    rows_pad = -(-rows // 8) * 8
    cc = jnp.concatenate([c, c_ctx[None, :], jnp.zeros((rows_pad - rows, d), F32)], axis=0)
    mod_all = _mod_vectors(cc, w_mod, b_mod)
    mod_x = mod_all[:, :b].reshape(DEPTH, b, 1, 6, d)
    mod_c = jnp.broadcast_to(mod_all[:, b].reshape(DEPTH, 1, 1, 6, d), (DEPTH, b, 1, 6, d))
    mods = jnp.concatenate([mod_c, mod_x], axis=2)

    cos, sin = _rope_tables(seq)
    rw_hi = router_w.astype(BF16)
    rw_lo = (router_w - rw_hi.astype(F32)).astype(BF16)
    zpad = jnp.zeros((d, LANES - 2 * N_EXPERTS), BF16)
    rw = jnp.concatenate([rw_hi, rw_lo, zpad], axis=1)

    y = None
    order_after = router_bias
    for i in range(DEPTH):
        j = i // 2
        modp = mods[i - 1] if i else None
        if i % 2 == 0:
            lam_init = 0.8 - 0.6 * math.exp(-0.3 * i)
            w = _att_weights(att_w_in[j])
            outs = _project(s, y, modp, mods[i], norm_g[i, 0], w, cos, sin,
                            n_rope=2 * A_HEADS * LANES, n_bf16=w.shape[1])
            if i:
                s = outs[0]
            p = outs[-1]
            oa_x, oa_c = _diff_attention(p, att_lambda[j], att_subln_g[j], lam_init)
            ob_x, ob_c = _neighbourhood_attention(p, _na_bias_table(na_bias[j], seq // GRID_W))
            s, h2, logits = _out_project((oa_x, oa_c, ob_x, ob_c), att_w_out[j].astype(BF16), s, mods[i], norm_g[i, 1],
                                         rw, gla=False)
        else:
            w, n_main = _gla_weights(gla_w_in[j])
            outs = _project(s, y, modp, mods[i], norm_g[i, 0], w, None, None, n_rope=0, n_bf16=n_main)
            s, p, lr = outs
            wg = jnp.zeros((2, LANES, C_HEADS * C_DK), F32)
            wg = wg.at[0, :C_RANK].set(gla_w_gate[j, 0]).at[1, C_RANK:2 * C_RANK].set(gla_w_gate[j, 1])
            o = _gla(p, lr, wg, gla_b_gate[j])
            s, h2, logits = _out_project((o, p), gla_w_out[j].astype(BF16), s, mods[i], norm_g[i, 1],
                                         rw, gla=True, gn=gla_norm_g[j])
        if i == 0:
            expert_w = _cast_expert_weights(0, moe_w1, moe_w3, moe_w2, order_after)
        y, order_after = _moe(h2.reshape(b * n, ROW_SEGS, LANES), logits.reshape(b * n, LANES), router_bias, *expert_w)
        y = y.reshape(b * n * ROW_SEGS, LANES)
        if i + 1 < DEPTH:
            expert_w = _cast_expert_weights(i + 1, moe_w1, moe_w3, moe_w2, order_after)
    return _final(s, y, mods[DEPTH - 1], final_g)
```

```python
import functools
import math

import jax
import jax.numpy as jnp
import numpy as np
from jax import lax
from jax.experimental import pallas as pl
from jax.experimental.pallas import tpu as pltpu
from jax.experimental.pallas import tpu_sc as plsc

F32 = jnp.float32
BF16 = jnp.bfloat16

D_MODEL = 1024
DEPTH = 4
GRID_W = 64
N_CTX = 256
A_HEADS = 4
A_DH = 64
A_DV = 128
B_HEADS = 8
B_DH = 64
NA_ROWS = 8
NA_COLS = 16
C_HEADS = 4
C_DK = 128
C_DV = 256
C_RANK = 16
C_TAU = 16.0
C_CHUNK = 64
GLA_BLOCK = 256
N_EXPERTS = 16
N_GROUPS = 4
PER_GROUP = 4
ROPE_BASE = 10000.0
EPS = 1e-6
LOG2_E = math.log2(math.e)
ATT_KEY_BLOCK = 768
ATT_Q_HALVES = 4

LANES = 128
ROW_TILE = 256
MOE_TILE = 256
SC_CORES = 2
SC_SUBCORES = 16
SC_WORKERS = SC_CORES * SC_SUBCORES
SC_GATHER_ROWS = 64
N_CLASSES = N_GROUPS * 6
NA_QROWS = 16
NA_SUB = 2
NA_KROWS = NA_SUB + NA_ROWS - 1
VMEM_LIMIT = 52 * 1024 * 1024
ROW_SEGS = D_MODEL // LANES // 2

PAIR_LO = (0, 0, 0, 1, 1, 2)
PAIR_HI = (1, 2, 3, 2, 3, 3)


def _cparams(sem):
    return pltpu.CompilerParams(dimension_semantics=sem, vmem_limit_bytes=VMEM_LIMIT)


def _sigmoid(x):
    return 1.0 / (1.0 + jnp.exp(-x))


def _silu(x):
    return x * _sigmoid(x)


def _rms(x, g):
    return x * lax.rsqrt(jnp.mean(x * x, axis=-1, keepdims=True) + EPS) * g


def _dot(a, b):
    return jnp.dot(a, b, preferred_element_type=F32)


def _dot_nt(a, b):
    return lax.dot_general(a, b, (((1,), (1,)), ((), ())), preferred_element_type=F32)


def _dot_tn(a, b):
    return lax.dot_general(a, b, (((0,), (0,)), ((), ())), preferred_element_type=F32)


def _mod_kernel(c_ref, w_ref, b_ref, o_ref):
    a = _silu(c_ref[...]).astype(BF16)
    o_ref[0] = _dot(a, w_ref[0].astype(BF16)) + b_ref[0]


def _mod_vectors(cc, w_mod, b_mod):
    depth, d, n6 = w_mod.shape
    rows = cc.shape[0]
    tn = 1536
    return pl.pallas_call(
        _mod_kernel,
        out_shape=jax.ShapeDtypeStruct((depth, rows, n6), F32),
        grid=(depth, n6 // tn),
        in_specs=[
            pl.BlockSpec((rows, d), lambda i, j: (0, 0)),
            pl.BlockSpec((1, d, tn), lambda i, j: (i, 0, j)),
            pl.BlockSpec((1, 1, tn), lambda i, j: (i, 0, j)),
        ],
        out_specs=pl.BlockSpec((1, rows, tn), lambda i, j: (i, 0, j)),
        compiler_params=_cparams(("arbitrary", "arbitrary")),
        name="mod_vectors",
    )(cc, w_mod, b_mod.reshape(depth, 1, n6))


def _token_rows(ref):
    rows = ref.shape[0] // ROW_SEGS
    packed = jnp.concatenate([ref[pl.ds(sg, rows, stride=ROW_SEGS), :] for sg in range(ROW_SEGS)], axis=-1)
    low = lax.bitcast_convert_type(lax.shift_left(packed, jnp.int32(16)), F32)
    high = lax.bitcast_convert_type(packed & jnp.int32(-65536), F32)
    return jnp.concatenate([low, high], axis=-1)


def _store_token_rows(ref, val):
    rows, d = val.shape
    rounded = val.astype(BF16).astype(F32)
    bits = lax.bitcast_convert_type(rounded, jnp.int32)
    packed = lax.shift_right_logical(bits[:, :d // 2], jnp.int32(16)) | bits[:, d // 2:]
    for sg in range(ROW_SEGS):
        ref[pl.ds(sg, rows, stride=ROW_SEGS), :] = packed[:, sg * LANES:(sg + 1) * LANES]


def _proj_kernel(*refs, has_y, first, n_rope, n_bf16, col_chunk):
    it = iter(refs)
    s_ref = next(it)
    x_ref = next(it) if first else None
    y_ref = next(it) if has_y else None
    modp_ref = next(it) if has_y else None
    mod_ref = next(it)
    ng_ref = next(it)
    w_ref = next(it)
    cos_ref = next(it) if n_rope else None
    sin_ref = next(it) if n_rope else None
    snew_ref = next(it) if has_y or first else None
    p_ref = next(it)
    lr_ref = next(it) if w_ref.shape[1] > n_bf16 else None

    x = s_ref[0]
    if first:
        x = jnp.where(pl.program_id(1) == 0, x, x_ref[0])
        snew_ref[0] = x
    if has_y:
        x = x + modp_ref[0, 0, 5:6, :] * _token_rows(y_ref)
        snew_ref[0] = x
    h = _rms(x, ng_ref[...])
    h = h * (1.0 + mod_ref[0, 0, 1:2, :]) + mod_ref[0, 0, 0:1, :]
    hb = h.astype(BF16)
    n_out = w_ref.shape[1]
    for c0 in range(0, n_out, col_chunk):
        c1 = min(c0 + col_chunk, n_out)
        acc = _dot(hb, w_ref[:, c0:c1])
        for b0 in range(c0, c1, LANES):
            t = acc[:, b0 - c0:b0 - c0 + LANES]
            if b0 < n_rope:
                t = t * cos_ref[...] + pltpu.roll(t, LANES // 2, 1) * sin_ref[...]
            if b0 < n_bf16:
                p_ref[0, :, b0:b0 + LANES] = t.astype(BF16)
            else:
                lr_ref[0, :, b0 - n_bf16:b0 - n_bf16 + LANES] = t


def _project(s, y, modp, mod, ng, w, cos, sin, *, n_rope, n_bf16):
    first = isinstance(s, tuple)
    tm = ROW_TILE
    row_spec = pl.BlockSpec((1, tm, s[0].shape[2] if first else s.shape[2]), lambda i, j: (i, j, 0))
    if first:
        ctx, x = s
        b, n, d = x.shape[0], ctx.shape[1] + x.shape[1], x.shape[2]
        in_specs = [pl.BlockSpec((1, tm, d), lambda i, j: (i, 0, 0)),
                    pl.BlockSpec((1, tm, d), lambda i, j: (i, jnp.maximum(j - 1, 0), 0))]
        args = [ctx, x]
    else:
        b, n, d = s.shape
        in_specs = [row_spec]
        args = [s]
    n_out = w.shape[1]
    has_y = y is not None
    mod_spec = pl.BlockSpec((1, 1, 6, d), lambda i, j: (i, jnp.minimum(j, 1), 0, 0))
    if has_y:
        in_specs += [pl.BlockSpec((tm * ROW_SEGS, LANES), lambda i, j: (i * (n // tm) + j, 0)), mod_spec]
        args += [y, modp]
    in_specs += [mod_spec, pl.BlockSpec((1, d), lambda i, j: (0, 0)),
                 pl.BlockSpec((d, n_out), lambda i, j: (0, 0))]
    args += [mod, ng.reshape(1, d), w]
    if n_rope:
        tab_spec = pl.BlockSpec((tm, LANES), lambda i, j: (j, 0))
        in_specs += [tab_spec, tab_spec]
        args += [cos, sin]
    out_shape, out_specs = [], []
    if has_y or first:
        out_shape.append(jax.ShapeDtypeStruct((b, n, d), F32))
        out_specs.append(row_spec)
    out_shape.append(jax.ShapeDtypeStruct((b, n, n_bf16), BF16))
    out_specs.append(pl.BlockSpec((1, tm, n_bf16), lambda i, j: (i, j, 0)))
    if n_out > n_bf16:
        out_shape.append(jax.ShapeDtypeStruct((b, n, n_out - n_bf16), F32))
        out_specs.append(pl.BlockSpec((1, tm, n_out - n_bf16), lambda i, j: (i, j, 0)))
    return pl.pallas_call(
        functools.partial(_proj_kernel, has_y=has_y, first=first, n_rope=n_rope, n_bf16=n_bf16, col_chunk=512),
        out_shape=out_shape,
        grid=(b, n // tm),
        in_specs=in_specs,
        out_specs=out_specs,
        compiler_params=_cparams(("arbitrary", "arbitrary")),
        name="norm_mod_project",
    )(*args)


def _softmax_rows(s):
    e = jnp.exp(s - jnp.max(s, axis=-1, keepdims=True))
    return e * (1.0 / jnp.sum(e, axis=-1, keepdims=True))


def _diff_attn_kernel(*refs, lam_init, n_q):
    q_refs = refs[:n_q]
    k_ref, v_ref, lam_ref, g_ref, o_ref = refs[n_q:]
    lp = lam_ref[...]
    lam = (jnp.exp(jnp.sum(lp[0:1] * lp[1:2], axis=-1, keepdims=True))
           - jnp.exp(jnp.sum(lp[2:3] * lp[3:4], axis=-1, keepdims=True)) + lam_init)
    q = jnp.concatenate([r[0] for r in q_refs], axis=0)
    lane = lax.broadcasted_iota(jnp.int32, (1, LANES), 1)
    first_map = (lane // 32) % 2 == 0
    zero = jnp.zeros_like(q)
    q0 = jnp.where(first_map, q, zero)
    q1 = jnp.where(first_map, zero, q)
    n_keys = k_ref.shape[1]
    blk = min(ATT_KEY_BLOCK, n_keys)
    ones = jnp.ones((blk, A_DV), BF16)
    state = [None, None]
    for kb in range(n_keys // blk):
        k = k_ref[0, kb * blk:(kb + 1) * blk, :]
        v1 = jnp.concatenate([v_ref[0, kb * blk:(kb + 1) * blk, :], ones], axis=1)
        for i, qm in enumerate((q0, q1)):
            s = _dot_nt(qm, k)
            m_new = jnp.max(s, axis=-1, keepdims=True)
            if kb:
                m_old, acc_old = state[i]
                m_new = jnp.maximum(m_old, m_new)
            acc = _dot(jnp.exp2(s - m_new).astype(BF16), v1)
            if kb:
                acc = acc_old * jnp.exp2(m_old - m_new) + acc
            state[i] = (m_new, acc)
    acc0, acc1 = state[0][1], state[1][1]
    o = acc0[:, :A_DV] * (1.0 / acc0[:, A_DV:]) - acc1[:, :A_DV] * (lam / acc1[:, A_DV:])
    o_ref[0] = (_rms(o, g_ref[...]) * (1.0 - lam_init)).astype(BF16)


def _diff_attention(p, lam_p, subln_g, lam_init):
    b, n, _ = p.shape
    half = ROW_TILE
    tq = ATT_Q_HALVES * half
    ctx_blocks = N_CTX // half
    width = A_HEADS * A_DV
    small = [pl.BlockSpec((4, A_DH), lambda i, h, *_: (0, 0)), pl.BlockSpec((1, A_DV), lambda i, h, *_: (0, 0))]
    q_specs = [pl.BlockSpec((1, half, LANES), functools.partial(lambda i, h, j, part: (i, ctx_blocks + ATT_Q_HALVES * j + part, h),
                                                                 part=part)) for part in range(ATT_Q_HALVES)]
    oa_x = pl.pallas_call(
        functools.partial(_diff_attn_kernel, lam_init=lam_init, n_q=ATT_Q_HALVES),
        out_shape=jax.ShapeDtypeStruct((b, n - N_CTX, width), BF16),
        grid=(b, A_HEADS, (n - N_CTX) // tq),
        in_specs=q_specs + [
            pl.BlockSpec((1, n, LANES), lambda i, h, j: (i, 0, A_HEADS + h)),
            pl.BlockSpec((1, n, LANES), lambda i, h, j: (i, 0, 2 * A_HEADS + h)),
        ] + small,
        out_specs=pl.BlockSpec((1, tq, LANES), lambda i, h, j: (i, j, h)),
        compiler_params=_cparams(("arbitrary", "arbitrary", "arbitrary")),
        name="diff_attention",
    )(*([p] * ATT_Q_HALVES), p, p, lam_p, subln_g.reshape(1, A_DV))
    oa_c = pl.pallas_call(
        functools.partial(_diff_attn_kernel, lam_init=lam_init, n_q=1),
        out_shape=jax.ShapeDtypeStruct((b, N_CTX, width), BF16),
        grid=(b, A_HEADS),
        in_specs=[
            pl.BlockSpec((1, N_CTX, LANES), lambda i, h: (i, 0, h)),
            pl.BlockSpec((1, N_CTX, LANES), lambda i, h: (i, 0, A_HEADS + h)),
            pl.BlockSpec((1, N_CTX, LANES), lambda i, h: (i, 0, 2 * A_HEADS + h)),
        ] + small,
        out_specs=pl.BlockSpec((1, N_CTX, LANES), lambda i, h: (i, 0, h)),
        compiler_params=_cparams(("arbitrary", "arbitrary")),
        name="context_diff_attention",
    )(p, p, p, lam_p, subln_g.reshape(1, A_DV))
    return oa_x, oa_c


def _split_heads(q):
    lane = lax.broadcasted_iota(jnp.int32, (1, LANES), 1)
    first_head = lane < B_DH
    zero = jnp.zeros_like(q)
    return first_head, (jnp.where(first_head, q, zero), jnp.where(first_head, zero, q))


def _na_ctx_kernel(q_ref, k_ref, v_ref, o_ref):
    first_head, qh = _split_heads(q_ref[0])
    kc = k_ref[0]
    vc = v_ref[0]
    outs = [_dot(_softmax_rows(_dot_nt(qh[h], kc)).astype(BF16), vc) for h in range(2)]
    o_ref[0] = jnp.where(first_head, outs[0], outs[1]).astype(BF16)


def _na_kernel(*refs):
    q_refs = refs[:-4]
    k_ref, v_ref, bm_ref, o_ref = refs[-4:]
    blk = pl.program_id(2)
    rows = (k_ref.shape[1] - N_CTX) // GRID_W
    sub_q = NA_SUB * GRID_W
    sub_k = NA_KROWS * GRID_W
    kc = k_ref[0, 0:N_CTX, :]
    vc = jnp.concatenate([v_ref[0, 0:N_CTX, :], jnp.ones((N_CTX, LANES), BF16)], axis=1)
    ones = jnp.ones((sub_k, LANES), BF16)
    for sub in range(NA_QROWS // NA_SUB):
        q_ref = q_refs[sub * sub_q // ROW_TILE]
        q0 = (sub * sub_q) % ROW_TILE
        first_head, qh = _split_heads(q_ref[0, q0:q0 + sub_q, :])
        k_row0 = jnp.clip(blk * NA_QROWS + sub * NA_SUB - NA_ROWS // 2, 0, rows - NA_KROWS)
        start = pl.multiple_of(N_CTX + k_row0 * GRID_W, GRID_W)
        kw = k_ref[0, pl.ds(start, sub_k), :]
        vw = jnp.concatenate([v_ref[0, pl.ds(start, sub_k), :], ones], axis=1)
        outs = []
        for h in range(2):
            s_loc = _dot_nt(qh[h], kw) + bm_ref[h, 0, sub]
            s_ctx = _dot_nt(qh[h], kc)
            m = jnp.maximum(jnp.max(s_loc, axis=-1, keepdims=True), jnp.max(s_ctx, axis=-1, keepdims=True))
            acc = _dot(jnp.exp(s_loc - m).astype(BF16), vw) + _dot(jnp.exp(s_ctx - m).astype(BF16), vc)
            outs.append(acc[:, :LANES] * (1.0 / acc[:, LANES:]))
        o_ref[0, sub * sub_q:(sub + 1) * sub_q, :] = jnp.where(first_head, outs[0], outs[1]).astype(BF16)


def _na_bias_table(na_bias, rows):
    h = na_bias.shape[0]
    n_dr, n_dc = 2 * NA_ROWS - 1, 2 * NA_COLS - 1
    width = 2 * GRID_W
    left = GRID_W - NA_COLS
    u = jnp.pad(na_bias, ((0, 0), (0, 0), (left, width - left - n_dc)))
    skew = jnp.tile(u, (1, 1, GRID_W))[:, :, :GRID_W * (width - 1)].reshape(h, n_dr, GRID_W, width - 1)
    toeplitz = skew[:, :, :, GRID_W - 1:]
    margin = NA_KROWS - NA_ROWS
    by_col = jnp.pad(toeplitz.transpose(0, 2, 1, 3), ((0, 0), (0, 0), (margin, margin), (0, 0)))
    by_col = by_col.reshape(h, GRID_W, (n_dr + 2 * margin) * GRID_W)
    col = np.arange(GRID_W)
    col_start = np.clip(col - NA_COLS // 2, 0, GRID_W - NA_COLS)
    col_ok = (col[None, :] >= col_start[:, None]) & (col[None, :] < col_start[:, None] + NA_COLS)
    n_sub = NA_QROWS // NA_SUB
    blocks = []
    for q_row0 in (0, NA_QROWS, rows - NA_QROWS):
        for sub in range(n_sub):
            k_row0 = int(np.clip(q_row0 + sub * NA_SUB - NA_ROWS // 2, 0, rows - NA_KROWS))
            kr = k_row0 + np.arange(NA_KROWS)
            for rq in range(NA_SUB):
                r = q_row0 + sub * NA_SUB + rq
                r0 = int(np.clip(r - NA_ROWS // 2, 0, rows - NA_ROWS))
                row_ok = (kr >= r0) & (kr < r0 + NA_ROWS)
                ok = (col_ok[:, None, :] & row_ok[None, :, None]).reshape(GRID_W, NA_KROWS * GRID_W)
                first = k_row0 - r + NA_ROWS - 1 + margin
                window = by_col[:, :, first * GRID_W:(first + NA_KROWS) * GRID_W]
                blocks.append(jnp.where(ok[None], window, -jnp.inf))
    table = jnp.stack(blocks, axis=1)
    return table.reshape(h, 3, n_sub, NA_SUB * GRID_W, NA_KROWS * GRID_W)


def _neighbourhood_attention(p, bm):
    b, n, _ = p.shape
    tq = NA_QROWS * GRID_W
    parts = tq // ROW_TILE
    n_blocks = (n - N_CTX) // tq
    ctx_blocks = N_CTX // ROW_TILE
    col0 = 3 * A_HEADS
    pairs = B_HEADS // 2
    width = B_HEADS * B_DH

    def pattern(j):
        return jnp.minimum(j, 1) + (j == n_blocks - 1).astype(jnp.int32)

    ob_x = pl.pallas_call(
        _na_kernel,
        out_shape=jax.ShapeDtypeStruct((b, n - N_CTX, width), BF16),
        grid=(b, pairs, n_blocks),
        in_specs=[pl.BlockSpec((1, ROW_TILE, LANES),
                               functools.partial(lambda i, h, j, part: (i, ctx_blocks + parts * j + part, col0 + h),
                                                 part=part)) for part in range(parts)] + [
            pl.BlockSpec((1, n, LANES), lambda i, h, j: (i, 0, col0 + pairs + h)),
            pl.BlockSpec((1, n, LANES), lambda i, h, j: (i, 0, col0 + 2 * pairs + h)),
            pl.BlockSpec((2, 1, NA_QROWS // NA_SUB, NA_SUB * GRID_W, NA_KROWS * GRID_W),
                         lambda i, h, j: (h, pattern(j), 0, 0, 0)),
        ],
        out_specs=pl.BlockSpec((1, tq, LANES), lambda i, h, j: (i, j, h)),
        compiler_params=_cparams(("arbitrary", "arbitrary", "arbitrary")),
        name="neighbourhood_attention",
    )(*([p] * parts), p, p, bm)
    ob_c = pl.pallas_call(
        _na_ctx_kernel,
        out_shape=jax.ShapeDtypeStruct((b, N_CTX, width), BF16),
        grid=(b, pairs),
        in_specs=[
            pl.BlockSpec((1, N_CTX, LANES), lambda i, h: (i, 0, col0 + h)),
            pl.BlockSpec((1, N_CTX, LANES), lambda i, h: (i, 0, col0 + pairs + h)),
            pl.BlockSpec((1, N_CTX, LANES), lambda i, h: (i, 0, col0 + 2 * pairs + h)),
        ],
        out_specs=pl.BlockSpec((1, N_CTX, LANES), lambda i, h: (i, 0, h)),
        compiler_params=_cparams(("arbitrary", "arbitrary")),
        name="context_attention",
    )(p, p, p)
    return ob_x, ob_c


def _gla_kernel(q_ref, k_ref, v_ref, lr_ref, wg_ref, bg_ref, o_ref, ob_ref, g_ref, sf_ref, sb_ref):
    n = q_ref.shape[1]
    blk = GLA_BLOCK
    per_blk = blk // C_CHUNK
    n_blocks = n // blk
    ri = lax.broadcasted_iota(jnp.int32, (blk, blk), 0)
    ci = lax.broadcasted_iota(jnp.int32, (blk, blk), 1)
    same_chunk = (ri // C_CHUNK) == (ci // C_CHUNK)
    keeps = (same_chunk & (ci <= ri), same_chunk & (ci >= ri))
    tris = tuple(jnp.where(kp, 1.0, 0.0).astype(BF16) for kp in keeps)
    w_gate = jnp.concatenate([wg_ref[0], wg_ref[1]], axis=1).astype(BF16)
    b_gate = jnp.concatenate([bg_ref[0:1, :], bg_ref[1:2, :]], axis=1)
    for i in range(n_blocks):
        z = _dot(lr_ref[0, i * blk:(i + 1) * blk, :].astype(BF16), w_gate) + b_gate
        g_ref[i * blk:(i + 1) * blk, :] = (jnp.minimum(z, 0.0) - jnp.log(1.0 + jnp.exp(-jnp.abs(z)))) * (1.0 / C_TAU)

    def block(sb, direction, st_ref):
        keep = keeps[direction]
        end_row = C_CHUNK - 1 if direction == 0 else 0
        mid_row = C_CHUNK // 2 - 1 if direction == 0 else C_CHUNK // 2
        r0 = sb * blk
        g = g_ref[r0:r0 + blk, direction * C_DK:(direction + 1) * C_DK]
        g_hi = g.astype(BF16)
        g_lo = (g - g_hi.astype(F32)).astype(BF16)
        gc2 = _dot(tris[direction], jnp.concatenate([g_hi, g_lo], axis=1))
        gc = (gc2[:, :C_DK] + gc2[:, C_DK:]).reshape(per_blk, C_CHUNK, C_DK)
        g_end = gc[:, end_row:end_row + 1, :]
        g_mid = gc[:, mid_row:mid_row + 1, :]
        q = (q_ref[0, r0:r0 + blk, :].astype(F32) * (C_DK ** -0.5)).reshape(per_blk, C_CHUNK, C_DK)
        k = k_ref[0, r0:r0 + blk, :].astype(F32).reshape(per_blk, C_CHUNK, C_DK)
        v = v_ref[0, r0:r0 + blk, :]
        q_in = (q * jnp.exp(gc)).astype(BF16).reshape(blk, C_DK)
        q_mid = (q * jnp.exp(gc - g_mid)).astype(BF16).reshape(blk, C_DK)
        k_mid = (k * jnp.exp(g_mid - gc)).astype(BF16).reshape(blk, C_DK)
        k_end = (k * jnp.exp(g_end - gc)).astype(BF16).reshape(blk, C_DK)
        a = jnp.where(keep, _dot_nt(q_mid, k_mid), 0.0)
        o_intra = _dot(a.astype(BF16), v)
        ends = jnp.concatenate([g_end.reshape(per_blk, C_DK), jnp.zeros((8 - per_blk, C_DK), F32)], axis=0)
        decay = jnp.transpose(jnp.exp(ends))
        o_inter = [None] * per_blk
        for c in (range(per_blk) if direction == 0 else reversed(range(per_blk))):
            rows = slice(c * C_CHUNK, (c + 1) * C_CHUNK)
            st = st_ref[...]
            o_inter[c] = _dot(q_in[rows], st.astype(BF16))
            st_ref[...] = st * decay[:, c:c + 1] + _dot_tn(k_end[rows], v[rows])
        return o_intra + jnp.concatenate(o_inter, axis=0)

    sf_ref[...] = jnp.zeros_like(sf_ref)
    sb_ref[...] = jnp.zeros_like(sb_ref)
    ctx_blocks = N_CTX // blk
    order_b = list(reversed(range(ctx_blocks))) + list(reversed(range(ctx_blocks, n_blocks)))
    for i in range(n_blocks):
        o_ref[0, i * blk:(i + 1) * blk, :] = block(i, 0, sf_ref)
        sb = order_b[i]
        ob_ref[sb * blk:(sb + 1) * blk, :] = block(sb, 1, sb_ref)
    o_ref[0] = o_ref[0] + ob_ref[...]


def _gla(p, lr, wg, bg):
    b, n, _ = p.shape
    return pl.pallas_call(
        _gla_kernel,
        out_shape=jax.ShapeDtypeStruct((b, n, C_HEADS * C_DV), F32),
        grid=(b, C_HEADS),
        in_specs=[
            pl.BlockSpec((1, n, C_DK), lambda i, h: (i, 0, h)),
            pl.BlockSpec((1, n, C_DK), lambda i, h: (i, 0, C_HEADS + h)),
            pl.BlockSpec((1, n, C_DV), lambda i, h: (i, 0, C_HEADS + h)),
            pl.BlockSpec((1, n, LANES), lambda i, h: (i, 0, 0)),
            pl.BlockSpec((2, LANES, C_DK), lambda i, h: (0, 0, h)),
            pl.BlockSpec((2, C_DK), lambda i, h: (0, h)),
        ],
        out_specs=pl.BlockSpec((1, n, C_DV), lambda i, h: (i, 0, h)),
        scratch_shapes=[pltpu.VMEM((n, C_DV), F32), pltpu.VMEM((n, 2 * C_DK), F32),
                        pltpu.VMEM((C_DK, C_DV), F32), pltpu.VMEM((C_DK, C_DV), F32)],
        compiler_params=_cparams(("arbitrary", "arbitrary")),
        name="gla",
    )(p, p, p, lr, wg, bg)


def _out_kernel(*refs, gla):
    if gla:
        o_ref, gate_ref, gn_ref, w_ref, s_ref, mod_ref, ng_ref, rw_ref, snew_ref, h2_ref, lg_ref = refs
        o = o_ref[0]
        gate = gate_ref[0].astype(F32)
        parts = []
        for hd in range(C_HEADS):
            oh = _rms(o[:, hd * C_DV:(hd + 1) * C_DV], gn_ref[...])
            parts.append((oh * _silu(gate[:, hd * C_DV:(hd + 1) * C_DV])).astype(BF16))
        acc = _dot(jnp.concatenate(parts, axis=-1), w_ref[...])
    else:
        oax_ref, oac_ref, obx_ref, obc_ref, w_ref, s_ref, mod_ref, ng_ref, rw_ref, snew_ref, h2_ref, lg_ref = refs
        half = oax_ref.shape[2]
        is_ctx = pl.program_id(1) == 0
        oa = jnp.where(is_ctx, oac_ref[0], oax_ref[0])
        ob = jnp.where(is_ctx, obc_ref[0], obx_ref[0])
        acc = _dot(oa, w_ref[0:half, :]) + _dot(ob, w_ref[half:, :])
    x = s_ref[0] + mod_ref[0, 0, 2:3, :] * acc
    snew_ref[0] = x
    h2 = _rms(x, ng_ref[...]) * (1.0 + mod_ref[0, 0, 4:5, :]) + mod_ref[0, 0, 3:4, :]
    _store_token_rows(h2_ref, h2)
    t = _dot(h2.astype(BF16), rw_ref[...])
    lg_ref[0] = t + pltpu.roll(t, LANES - N_EXPERTS, 1)


def _out_project(mix, w_out, s, mod, ng, rw, *, gla, gn=None):
    b, n, d = s.shape
    tm = ROW_TILE
    row_spec = pl.BlockSpec((1, tm, d), lambda i, j: (i, j, 0))
    const2 = lambda i, j: (0, 0)
    if gla:
        o, p = mix
        in_specs = [row_spec, pl.BlockSpec((1, tm, d), lambda i, j: (i, j, 2)),
                    pl.BlockSpec((1, C_DV), const2)]
        args = [o, p, gn.reshape(1, C_DV)]
    else:
        oa_x, oa_c, ob_x, ob_c = mix
        half = oa_x.shape[2]
        latent = pl.BlockSpec((1, tm, half), lambda i, j: (i, jnp.maximum(j - 1, 0), 0))
        context = pl.BlockSpec((1, tm, half), lambda i, j: (i, 0, 0))
        in_specs = [latent, context, latent, context]
        args = [oa_x, oa_c, ob_x, ob_c]
    in_specs += [pl.BlockSpec((d, d), const2), row_spec,
                 pl.BlockSpec((1, 1, 6, d), lambda i, j: (i, jnp.minimum(j, 1), 0, 0)),
                 pl.BlockSpec((1, d), const2), pl.BlockSpec((d, LANES), const2)]
    args += [w_out, s, mod, ng.reshape(1, d), rw]
    return pl.pallas_call(
        functools.partial(_out_kernel, gla=gla),
        out_shape=[jax.ShapeDtypeStruct((b, n, d), F32), jax.ShapeDtypeStruct((b * n * ROW_SEGS, LANES), jnp.int32),
                   jax.ShapeDtypeStruct((b, n, LANES), F32)],
        grid=(b, n // tm),
        in_specs=in_specs,
        out_specs=[row_spec, pl.BlockSpec((tm * ROW_SEGS, LANES), lambda i, j: (i * (n // tm) + j, 0)),
                   pl.BlockSpec((1, tm, LANES), lambda i, j: (i, j, 0))],
        compiler_params=_cparams(("arbitrary", "arbitrary")),
        name="out_project",
    )(*args)


def _route_kernel(bias_ref, lg_ref, cls_ref, wlo_ref, whi_ref, rank_ref, cnt_ref):
    score = [_sigmoid(lg_ref[e]) for e in range(N_EXPERTS)]
    sel = [score[e] + bias_ref[e] for e in range(N_EXPERTS)]
    grp_score = []
    for g in range(N_GROUPS):
        v = sel[g * PER_GROUP:(g + 1) * PER_GROUP]
        best = v[0] + v[1]
        for a in range(PER_GROUP):
            for c in range(a + 1, PER_GROUP):
                if (a, c) != (0, 1):
                    best = jnp.maximum(best, v[a] + v[c])
        grp_score.append(best)
    grp = jnp.zeros(grp_score[0].shape, jnp.int32)
    best = grp_score[0]
    for g in range(1, N_GROUPS):
        upd = grp_score[g] > best
        best = jnp.where(upd, grp_score[g], best)
        grp = jnp.where(upd, g, grp)

    def pick(vals, j):
        out = vals[j]
        for g in range(1, N_GROUPS):
            out = jnp.where(grp == g, vals[g * PER_GROUP + j], out)
        return out

    v = [pick(sel, j) for j in range(PER_GROUP)]
    sc = [pick(score, j) for j in range(PER_GROUP)]
    one = jnp.ones(grp.shape, jnp.int32)
    zero = jnp.zeros(grp.shape, jnp.int32)
    chosen = []
    for j in range(PER_GROUP):
        rank = zero
        for m in range(PER_GROUP):
            if m == j:
                continue
            ahead = (v[m] >= v[j]) if m < j else (v[m] > v[j])
            rank = rank + jnp.where(ahead, one, zero)
        chosen.append(rank < 2)
    code = zero
    for j in range(PER_GROUP):
        code = code + jnp.where(chosen[j], one * (1 << j), zero)
    pair = zero
    for idx in range(6):
        pair = jnp.where(code == (1 << PAIR_LO[idx]) + (1 << PAIR_HI[idx]), idx, pair)
    s_lo = jnp.where(chosen[0], sc[0], jnp.where(chosen[1], sc[1], sc[2]))
    s_hi = jnp.where(chosen[3], sc[3], jnp.where(chosen[2], sc[2], sc[1]))
    den = s_lo + s_hi
    cls = grp * 6 + pair
    cls_ref[...] = cls
    wlo_ref[...] = s_lo / den
    whi_ref[...] = s_hi / den
    r = cls.shape[0]
    lane = lax.broadcasted_iota(jnp.int32, (1, LANES), 1)
    li = lax.broadcasted_iota(jnp.int32, (LANES, LANES), 0)
    lj = lax.broadcasted_iota(jnp.int32, (LANES, LANES), 1)
    upto_lane = jnp.where(li <= lj, 1.0, 0.0).astype(BF16)
    ri = lax.broadcasted_iota(jnp.int32, (r, r), 0)
    rj = lax.broadcasted_iota(jnp.int32, (r, r), 1)
    earlier_rows = jnp.where(rj < ri, 1.0, 0.0).astype(BF16)
    member = [jnp.where(cls == c, 1.0, 0.0) for c in range(N_CLASSES)]
    row_totals = jnp.zeros((r, LANES), F32)
    for c in range(N_CLASSES):
        row_totals = row_totals + jnp.where(lane == c, jnp.sum(member[c], axis=1, keepdims=True), 0.0)
    before_row = _dot(earlier_rows, row_totals.astype(BF16))
    rank = jnp.zeros((r, LANES), F32)
    for c in range(N_CLASSES):
        in_row = _dot(member[c].astype(BF16), upto_lane)
        offset = jnp.sum(jnp.where(lane == c, before_row, 0.0), axis=1, keepdims=True)
        rank = rank + member[c] * (in_row + offset - 1.0)
    rank_ref[...] = rank.astype(jnp.int32)
    counts = before_row[r - 1:r, :] + row_totals[r - 1:r, :]
    cnt_ref[...] = jnp.broadcast_to(counts, cnt_ref.shape).astype(jnp.int32)


def _route(logits_t, router_bias):
    _, r, _ = logits_t.shape
    full = pl.BlockSpec((r, LANES), lambda i: (0, 0))
    return pl.pallas_call(
        _route_kernel,
        out_shape=[jax.ShapeDtypeStruct((r, LANES), jnp.int32), jax.ShapeDtypeStruct((r, LANES), F32),
                   jax.ShapeDtypeStruct((r, LANES), F32), jax.ShapeDtypeStruct((r, LANES), jnp.int32),
                   jax.ShapeDtypeStruct((8, LANES), jnp.int32)],
        grid=(1,),
        in_specs=[pl.BlockSpec(memory_space=pltpu.SMEM),
                  pl.BlockSpec((N_EXPERTS, r, LANES), lambda i: (0, 0, 0))],
        out_specs=[full, full, full, full, pl.BlockSpec((8, LANES), lambda i: (0, 0))],
        compiler_params=_cparams(("arbitrary",)),
        name="route",
    )(router_bias, logits_t)


def _gather_rows(table, idx):
    p = idx.shape[0]
    per_worker = p // SC_WORKERS
    n_chunks = per_worker // SC_GATHER_ROWS
    mesh = plsc.VectorSubcoreMesh(core_axis_name="c", subcore_axis_name="s", num_cores=SC_CORES,
                                  num_subcores=SC_SUBCORES)

    @functools.partial(
        pl.kernel, mesh=mesh,
        out_type=jax.ShapeDtypeStruct((p,) + table.shape[1:], table.dtype),
        scratch_types=[pltpu.VMEM((per_worker,), jnp.int32),
                       pltpu.VMEM((SC_GATHER_ROWS,) + table.shape[1:], table.dtype),
                       pltpu.SemaphoreType.DMA],
        compiler_params=pltpu.CompilerParams(use_tc_tiling_on_sc=True),
        name="gather_rows",
    )
    def gather(table_hbm, idx_hbm, out_hbm, idx_v, rows_v, sem):
        worker = lax.axis_index("s") * SC_CORES + lax.axis_index("c")
        base = worker * per_worker
        pltpu.sync_copy(idx_hbm.at[pl.ds(base, per_worker)], idx_v)

        @pl.loop(0, n_chunks)
        def _(i):
            off = pl.multiple_of(i * SC_GATHER_ROWS, SC_GATHER_ROWS)
            pltpu.async_copy(table_hbm.at[idx_v.at[pl.ds(off, SC_GATHER_ROWS)]], rows_v, sem).wait()
            pltpu.sync_copy(rows_v, out_hbm.at[pl.ds(base + off, SC_GATHER_ROWS)])

    return gather(table, idx)


def _scatter_rows(src, idx, n_out):
    t = idx.shape[0]
    per_worker = t // SC_WORKERS
    n_chunks = per_worker // SC_GATHER_ROWS
    mesh = plsc.VectorSubcoreMesh(core_axis_name="c", subcore_axis_name="s", num_cores=SC_CORES,
                                  num_subcores=SC_SUBCORES)

    @functools.partial(
        pl.kernel, mesh=mesh,
        out_type=jax.ShapeDtypeStruct((n_out,) + src.shape[1:], src.dtype),
        scratch_types=[pltpu.VMEM((n_chunks, SC_GATHER_ROWS), jnp.int32),
                       pltpu.VMEM((SC_GATHER_ROWS,) + src.shape[1:], src.dtype),
                       pltpu.SemaphoreType.DMA],
        compiler_params=pltpu.CompilerParams(use_tc_tiling_on_sc=True),
        name="scatter_rows",
    )
    def scatter(src_hbm, idx_hbm, out_hbm, idx_v, rows_v, sem):
        worker = lax.axis_index("s") * SC_CORES + lax.axis_index("c")
        base = worker * per_worker
        pltpu.sync_copy(idx_hbm.at[worker], idx_v)

        @pl.loop(0, n_chunks)
        def _(i):
            off = pl.multiple_of(i * SC_GATHER_ROWS, SC_GATHER_ROWS)
            pltpu.sync_copy(src_hbm.at[pl.ds(base + off, SC_GATHER_ROWS)], rows_v)
            pltpu.async_copy(rows_v, out_hbm.at[idx_v.at[i]], sem).wait()

    return scatter(src, idx.reshape(SC_WORKERS, n_chunks, SC_GATHER_ROWS))


def _cast_kernel(*refs):
    n = (len(refs) - 1) // 2
    for i in range(n):
        refs[n + 1 + i][0] = refs[i][0, 0].astype(BF16)


def _cast_expert_weights(layer, w1, w3, w2, after):
    ws = (w1, w3, w2)
    e = w1.shape[1]
    in_specs = [pl.BlockSpec((1, 1) + w.shape[2:], lambda j: (layer, j, 0, 0)) for w in ws]
    in_specs.append(pl.BlockSpec(memory_space=pl.ANY))
    return pl.pallas_call(
        _cast_kernel,
        out_shape=[jax.ShapeDtypeStruct(w.shape[1:], BF16) for w in ws],
        grid=(e,),
        in_specs=in_specs,
        out_specs=[pl.BlockSpec((1,) + w.shape[2:], lambda j: (j, 0, 0)) for w in ws],
        compiler_params=_cparams(("arbitrary",)),
        name="cast_expert_weights",
    )(*ws, after)


def _moe_kernel(lo_ref, hi_ref, nt_ref, x_ref, wr_ref, w1l_ref, w1h_ref, w3l_ref, w3h_ref, w2l_ref, w2h_ref, y_ref):
    j = pl.program_id(0)

    @pl.when(j < nt_ref[0])
    def _():
        x = _token_rows(x_ref).astype(BF16)
        wr = wr_ref[...]
        he_lo = (_silu(_dot(x, w1l_ref[0])) * _dot(x, w3l_ref[0]) * wr[:, 0:1]).astype(BF16)
        he_hi = (_silu(_dot(x, w1h_ref[0])) * _dot(x, w3h_ref[0]) * wr[:, 1:2]).astype(BF16)
        y = _dot(he_lo, w2l_ref[0]) + _dot(he_hi, w2h_ref[0])
        _store_token_rows(y_ref, y)

    @pl.when(j >= nt_ref[0])
    def _():
        y_ref[...] = jnp.zeros_like(y_ref)


def _moe_experts(xs, wrow, tile_lo, tile_hi, n_tiles, w1, w3, w2):
    tp = xs.shape[0] // ROW_SEGS
    tm = MOE_TILE
    d, de = w1.shape[1], w1.shape[2]

    def gate_map(j, lo, hi, nt):
        return (jnp.minimum(j, nt[0] - 1), 0)

    def lo_map(j, lo, hi, nt):
        return (lo[jnp.minimum(j, nt[0] - 1)], 0, 0)

    def hi_map(j, lo, hi, nt):
        return (hi[jnp.minimum(j, nt[0] - 1)], 0, 0)

    up = (1, d, de)
    down = (1, de, d)
    return pl.pallas_call(
        _moe_kernel,
        out_shape=jax.ShapeDtypeStruct((tp * ROW_SEGS, LANES), jnp.int32),
        grid_spec=pltpu.PrefetchScalarGridSpec(
            num_scalar_prefetch=3,
            grid=(tp // tm,),
            in_specs=[pl.BlockSpec((tm * ROW_SEGS, LANES), gate_map), pl.BlockSpec((tm, LANES), gate_map),
                      pl.BlockSpec(up, lo_map), pl.BlockSpec(up, hi_map),
                      pl.BlockSpec(up, lo_map), pl.BlockSpec(up, hi_map),
                      pl.BlockSpec(down, lo_map), pl.BlockSpec(down, hi_map)],
            out_specs=pl.BlockSpec((tm * ROW_SEGS, LANES), lambda j, lo, hi, nt: (j, 0)),
        ),
        compiler_params=_cparams(("arbitrary",)),
        name="moe_experts",
    )(tile_lo, tile_hi, n_tiles, xs, wrow, w1, w1, w3, w3, w2, w2)


def _moe(h2, logits, router_bias, w1, w3, w2):
    t = h2.shape[0]
    tm = MOE_TILE
    n_tiles_max = t // tm + N_CLASSES
    tp = n_tiles_max * tm
    lg_t = logits[:, :N_EXPERTS].T.reshape(N_EXPERTS, t // LANES, LANES)
    cls, wlo, whi, rank, counts = _route(lg_t, router_bias)
    cls, wlo, whi, rank = cls.reshape(t), wlo.reshape(t), whi.reshape(t), rank.reshape(t)
    counts = counts[0, :N_CLASSES]
    classes = jnp.arange(N_CLASSES, dtype=jnp.int32)
    onehot = (cls[:, None] == classes[None, :]).astype(jnp.int32)
    tiles_per = (counts + tm - 1) // tm
    tile_end = jnp.cumsum(tiles_per)
    tile_start = tile_end - tiles_per
    dest = jnp.sum(onehot * (tile_start * tm)[None, :], axis=1) + rank
    tiles = jnp.arange(n_tiles_max, dtype=jnp.int32)
    tile_cls = jnp.minimum(jnp.sum((tile_end[None, :] <= tiles[:, None]).astype(jnp.int32), axis=1), N_CLASSES - 1)
    pair = tile_cls % 6
    base = (tile_cls // 6) * PER_GROUP
    pair_onehot = (pair[:, None] == jnp.arange(6, dtype=jnp.int32)[None, :]).astype(jnp.int32)
    tile_lo = base + jnp.sum(pair_onehot * jnp.asarray(PAIR_LO, jnp.int32)[None, :], axis=1)
    tile_hi = base + jnp.sum(pair_onehot * jnp.asarray(PAIR_HI, jnp.int32)[None, :], axis=1)
    gates = jnp.pad(jnp.stack([wlo, whi], axis=1), ((0, 0), (0, LANES - 2))).reshape(t, 1, LANES)
    wrow = _scatter_rows(gates, dest, tp).reshape(tp, LANES)
    xs = _scatter_rows(h2, dest, tp).reshape(tp * ROW_SEGS, LANES)
    ys = _moe_experts(xs, wrow, tile_lo, tile_hi, tile_end[-1:], w1, w3, w2)
    return _gather_rows(ys.reshape(tp, ROW_SEGS, LANES), dest), dest


def _final_kernel(s_ref, y_ref, mod_ref, g_ref, o_ref):
    x = s_ref[0] + mod_ref[0, 0, 5:6, :] * _token_rows(y_ref)
    o_ref[0] = _rms(x, g_ref[...])


def _final(s, y, mod, final_g):
    b, n, d = s.shape
    tm = ROW_TILE
    skip = N_CTX // tm
    row_spec = pl.BlockSpec((1, tm, d), lambda i, j: (i, j + skip, 0))
    return pl.pallas_call(
        _final_kernel,
        out_shape=jax.ShapeDtypeStruct((b, n - N_CTX, d), F32),
        grid=(b, (n - N_CTX) // tm),
        in_specs=[row_spec, pl.BlockSpec((tm * ROW_SEGS, LANES), lambda i, j: (i * (n // tm) + j + skip, 0)),
                  pl.BlockSpec((1, 1, 6, d), lambda i, j: (i, 1, 0, 0)),
                  pl.BlockSpec((1, d), lambda i, j: (0, 0))],
        out_specs=pl.BlockSpec((1, tm, d), lambda i, j: (i, j, 0)),
        compiler_params=_cparams(("arbitrary", "arbitrary")),
        name="final_norm",
    )(s, y, mod, final_g.reshape(1, d))


def _rope_tables(seq):
    pos = jnp.arange(seq)
    row_pos, col_pos = pos // GRID_W, pos % GRID_W
    n = A_DH // 2
    inv = ROPE_BASE ** (-jnp.arange(0, n, 2, dtype=F32) / n)
    ang_r = row_pos.astype(F32)[:, None] * inv[None, :]
    ang_c = col_pos.astype(F32)[:, None] * inv[None, :]
    ang = jnp.concatenate([ang_r, ang_c], axis=-1)
    cos = jnp.tile(jnp.cos(ang), (1, 4))
    sin = jnp.tile(jnp.sin(ang), (1, 4))
    sin = jnp.concatenate([-sin[:, :LANES // 2], sin[:, LANES // 2:]], axis=-1)
    cos = jnp.concatenate([jnp.ones((N_CTX, LANES), F32), cos], axis=0)
    sin = jnp.concatenate([jnp.zeros((N_CTX, LANES), F32), sin], axis=0)
    return cos, sin


def _interleave_maps(w):
    d = w.shape[0]
    w = w.reshape(d, A_HEADS, 2, 2, 2, A_DH // 4)
    return w.transpose(0, 1, 4, 2, 3, 5).reshape(d, A_HEADS * LANES)


def _att_weights(w_in):
    a_qk = A_HEADS * 2 * A_DH
    a_v = A_HEADS * A_DV
    b_w = B_HEADS * B_DH
    qa = _interleave_maps(w_in[:, :a_qk]) * (A_DH ** -0.5 * LOG2_E)
    ka = _interleave_maps(w_in[:, a_qk:2 * a_qk])
    va = w_in[:, 2 * a_qk:2 * a_qk + a_v]
    o = 2 * a_qk + a_v
    qb = w_in[:, o:o + b_w] * (B_DH ** -0.5)
    rest = w_in[:, o + b_w:]
    return jnp.concatenate([qa, ka, va, qb, rest], axis=1).astype(BF16)


def _gla_weights(w_in):
    d = w_in.shape[0]
    n_main = 2 * C_HEADS * C_DK + 2 * C_HEADS * C_DV
    pad = jnp.zeros((d, LANES - 2 * C_RANK), w_in.dtype)
    return jnp.concatenate([w_in, pad], axis=1).astype(BF16), n_main


def kernel(x, c, ctx, c_ctx, w_mod, b_mod, norm_g, final_g, att_w_in, att_w_out, att_lambda, att_subln_g, na_bias,
           gla_w_in, gla_w_gate, gla_b_gate, gla_norm_g, gla_w_out, router_w, router_bias, moe_w1, moe_w3, moe_w2):
    b, seq, d = x.shape
    n = N_CTX + seq
    s = (ctx, x)

    rows = b + 1
    rows_pad = -(-rows // 8) * 8
    cc = jnp.concatenate([c, c_ctx[None, :], jnp.zeros((rows_pad - rows, d), F32)], axis=0)
    mod_all = _mod_vectors(cc, w_mod, b_mod)
    mod_x = mod_all[:, :b].reshape(DEPTH, b, 1, 6, d)
    mod_c = jnp.broadcast_to(mod_all[:, b].reshape(DEPTH, 1, 1, 6, d), (DEPTH, b, 1, 6, d))
    mods = jnp.concatenate([mod_c, mod_x], axis=2)

    cos, sin = _rope_tables(seq)
    rw_hi = router_w.astype(BF16)
    rw_lo = (router_w - rw_hi.astype(F32)).astype(BF16)
    zpad = jnp.zeros((d, LANES - 2 * N_EXPERTS), BF16)
    rw = jnp.concatenate([rw_hi, rw_lo, zpad], axis=1)

    y = None
    order_after = router_bias
    for i in range(DEPTH):
        j = i // 2
        modp = mods[i - 1] if i else None
        if i % 2 == 0:
            lam_init = 0.8 - 0.6 * math.exp(-0.3 * i)
            w = _att_weights(att_w_in[j])
            outs = _project(s, y, modp, mods[i], norm_g[i, 0], w, cos, sin,
                            n_rope=2 * A_HEADS * LANES, n_bf16=w.shape[1])
            s = outs[0]
            p = outs[-1]
            oa_x, oa_c = _diff_attention(p, att_lambda[j], att_subln_g[j], lam_init)
            ob_x, ob_c = _neighbourhood_attention(p, _na_bias_table(na_bias[j], seq // GRID_W))
            s, h2, logits = _out_project((oa_x, oa_c, ob_x, ob_c), att_w_out[j].astype(BF16), s, mods[i], norm_g[i, 1],
                                         rw, gla=False)
        else:
            w, n_main = _gla_weights(gla_w_in[j])
            outs = _project(s, y, modp, mods[i], norm_g[i, 0], w, None, None, n_rope=0, n_bf16=n_main)
            s, p, lr = outs
            wg = jnp.zeros((2, LANES, C_HEADS * C_DK), F32)
            wg = wg.at[0, :C_RANK].set(gla_w_gate[j, 0]).at[1, C_RANK:2 * C_RANK].set(gla_w_gate[j, 1])
            o = _gla(p, lr, wg, gla_b_gate[j])
            s, h2, logits = _out_project((o, p), gla_w_out[j].astype(BF16), s, mods[i], norm_g[i, 1],
                                         rw, gla=True, gn=gla_norm_g[j])
        if i == 0:
            expert_w = _cast_expert_weights(0, moe_w1, moe_w3, moe_w2, order_after)
        y, order_after = _moe(h2.reshape(b * n, ROW_SEGS, LANES), logits.reshape(b * n, LANES), router_bias, *expert_w)
        y = y.reshape(b * n * ROW_SEGS, LANES)
        if i + 1 < DEPTH:
            expert_w = _cast_expert_weights(i + 1, moe_w1, moe_w3, moe_w2, order_after)
    return _final(s, y, mods[DEPTH - 1], final_g)
```

```python
import functools
import math

import jax
import jax.numpy as jnp
import numpy as np
from jax import lax
from jax.experimental import pallas as pl
from jax.experimental.pallas import tpu as pltpu
from jax.experimental.pallas import tpu_sc as plsc

F32 = jnp.float32
BF16 = jnp.bfloat16

D_MODEL = 1024
DEPTH = 4
GRID_W = 64
N_CTX = 256
A_HEADS = 4
A_DH = 64
A_DV = 128
B_HEADS = 8
B_DH = 64
NA_ROWS = 8
NA_COLS = 16
C_HEADS = 4
C_DK = 128
C_DV = 256
C_RANK = 16
C_TAU = 16.0
C_CHUNK = 64
GLA_BLOCK = 256
N_EXPERTS = 16
N_GROUPS = 4
PER_GROUP = 4
ROPE_BASE = 10000.0
EPS = 1e-6
LOG2_E = math.log2(math.e)
ATT_KEY_BLOCK = 768
ATT_Q_HALVES = 4

LANES = 128
ROW_TILE = 256
MOE_TILE = 256
SC_CORES = 2
SC_SUBCORES = 16
SC_WORKERS = SC_CORES * SC_SUBCORES
SC_GATHER_ROWS = 64
N_CLASSES = N_GROUPS * 6
NA_QROWS = 16
NA_SUB = 2
NA_KROWS = NA_SUB + NA_ROWS - 1
VMEM_LIMIT = 52 * 1024 * 1024
ROW_SEGS = D_MODEL // LANES // 2

PAIR_LO = (0, 0, 0, 1, 1, 2)
PAIR_HI = (1, 2, 3, 2, 3, 3)


def _cparams(sem):
    return pltpu.CompilerParams(dimension_semantics=sem, vmem_limit_bytes=VMEM_LIMIT)


def _sigmoid(x):
    return 1.0 / (1.0 + jnp.exp(-x))


def _silu(x):
    return x * _sigmoid(x)


def _rms(x, g):
    return x * lax.rsqrt(jnp.mean(x * x, axis=-1, keepdims=True) + EPS) * g


def _dot(a, b):
    return jnp.dot(a, b, preferred_element_type=F32)


def _dot_nt(a, b):
    return lax.dot_general(a, b, (((1,), (1,)), ((), ())), preferred_element_type=F32)


def _dot_tn(a, b):
    return lax.dot_general(a, b, (((0,), (0,)), ((), ())), preferred_element_type=F32)


def _mod_kernel(c_ref, w_ref, b_ref, o_ref):
    a = _silu(c_ref[...]).astype(BF16)
    o_ref[0] = _dot(a, w_ref[0].astype(BF16)) + b_ref[0]


def _mod_vectors(cc, w_mod, b_mod):
    depth, d, n6 = w_mod.shape
    rows = cc.shape[0]
    tn = 1536
    return pl.pallas_call(
        _mod_kernel,
        out_shape=jax.ShapeDtypeStruct((depth, rows, n6), F32),
        grid=(depth, n6 // tn),
        in_specs=[
            pl.BlockSpec((rows, d), lambda i, j: (0, 0)),
            pl.BlockSpec((1, d, tn), lambda i, j: (i, 0, j)),
            pl.BlockSpec((1, 1, tn), lambda i, j: (i, 0, j)),
        ],
        out_specs=pl.BlockSpec((1, rows, tn), lambda i, j: (i, 0, j)),
        compiler_params=_cparams(("arbitrary", "arbitrary")),
        name="mod_vectors",
    )(cc, w_mod, b_mod.reshape(depth, 1, n6))


def _token_rows(ref):
    rows = ref.shape[0] // ROW_SEGS
    packed = jnp.concatenate([ref[pl.ds(sg, rows, stride=ROW_SEGS), :] for sg in range(ROW_SEGS)], axis=-1)
    low = lax.bitcast_convert_type(lax.shift_left(packed, jnp.int32(16)), F32)
    high = lax.bitcast_convert_type(packed & jnp.int32(-65536), F32)
    return jnp.concatenate([low, high], axis=-1)


def _store_token_rows(ref, val):
    rows, d = val.shape
    rounded = val.astype(BF16).astype(F32)
    bits = lax.bitcast_convert_type(rounded, jnp.int32)
    packed = lax.shift_right_logical(bits[:, :d // 2], jnp.int32(16)) | bits[:, d // 2:]
    for sg in range(ROW_SEGS):
        ref[pl.ds(sg, rows, stride=ROW_SEGS), :] = packed[:, sg * LANES:(sg + 1) * LANES]


def _proj_kernel(*refs, has_y, first, n_rope, n_bf16, col_chunk):
    it = iter(refs)
    s_ref = next(it)
    x_ref = next(it) if first else None
    y_ref = next(it) if has_y else None
    modp_ref = next(it) if has_y else None
    mod_ref = next(it)
    ng_ref = next(it)
    w_ref = next(it)
    cos_ref = next(it) if n_rope else None
    sin_ref = next(it) if n_rope else None
    snew_ref = next(it) if has_y or first else None
    p_ref = next(it)
    lr_ref = next(it) if w_ref.shape[1] > n_bf16 else None

    x = s_ref[0]
    if first:
        x = jnp.where(pl.program_id(1) == 0, x, x_ref[0])
        snew_ref[0] = x
    if has_y:
        x = x + modp_ref[0, 0, 5:6, :] * _token_rows(y_ref)
        snew_ref[0] = x
    h = _rms(x, ng_ref[...])
    h = h * (1.0 + mod_ref[0, 0, 1:2, :]) + mod_ref[0, 0, 0:1, :]
    hb = h.astype(BF16)
    n_out = w_ref.shape[1]
    for c0 in range(0, n_out, col_chunk):
        c1 = min(c0 + col_chunk, n_out)
        acc = _dot(hb, w_ref[:, c0:c1])
        for b0 in range(c0, c1, LANES):
            t = acc[:, b0 - c0:b0 - c0 + LANES]
            if b0 < n_rope:
                t = t * cos_ref[...] + pltpu.roll(t, LANES // 2, 1) * sin_ref[...]
            if b0 < n_bf16:
                p_ref[0, :, b0:b0 + LANES] = t.astype(BF16)
            else:
                lr_ref[0, :, b0 - n_bf16:b0 - n_bf16 + LANES] = t


def _project(s, y, modp, mod, ng, w, cos, sin, *, n_rope, n_bf16):
    first = isinstance(s, tuple)
    tm = ROW_TILE
    row_spec = pl.BlockSpec((1, tm, s[0].shape[2] if first else s.shape[2]), lambda i, j: (i, j, 0))
    if first:
        ctx, x = s
        b, n, d = x.shape[0], ctx.shape[1] + x.shape[1], x.shape[2]
        in_specs = [pl.BlockSpec((1, tm, d), lambda i, j: (i, 0, 0)),
                    pl.BlockSpec((1, tm, d), lambda i, j: (i, jnp.maximum(j - 1, 0), 0))]
        args = [ctx, x]
    else:
        b, n, d = s.shape
        in_specs = [row_spec]
        args = [s]
    n_out = w.shape[1]
    has_y = y is not None
    mod_spec = pl.BlockSpec((1, 1, 6, d), lambda i, j: (i, jnp.minimum(j, 1), 0, 0))
    if has_y:
        in_specs += [pl.BlockSpec((tm * ROW_SEGS, LANES), lambda i, j: (i * (n // tm) + j, 0)), mod_spec]
        args += [y, modp]
    in_specs += [mod_spec, pl.BlockSpec((1, d), lambda i, j: (0, 0)),
                 pl.BlockSpec((d, n_out), lambda i, j: (0, 0))]
    args += [mod, ng.reshape(1, d), w]
    if n_rope:
        tab_spec = pl.BlockSpec((tm, LANES), lambda i, j: (j, 0))
        in_specs += [tab_spec, tab_spec]
        args += [cos, sin]
    out_shape, out_specs = [], []
    if has_y or first:
        out_shape.append(jax.ShapeDtypeStruct((b, n, d), F32))
        out_specs.append(row_spec)
    out_shape.append(jax.ShapeDtypeStruct((b, n, n_bf16), BF16))
    out_specs.append(pl.BlockSpec((1, tm, n_bf16), lambda i, j: (i, j, 0)))
    if n_out > n_bf16:
        out_shape.append(jax.ShapeDtypeStruct((b, n, n_out - n_bf16), F32))
        out_specs.append(pl.BlockSpec((1, tm, n_out - n_bf16), lambda i, j: (i, j, 0)))
    return pl.pallas_call(
        functools.partial(_proj_kernel, has_y=has_y, first=first, n_rope=n_rope, n_bf16=n_bf16, col_chunk=512),
        out_shape=out_shape,
        grid=(b, n // tm),
        in_specs=in_specs,
        out_specs=out_specs,
        compiler_params=_cparams(("arbitrary", "arbitrary")),
        name="norm_mod_project",
    )(*args)


def _softmax_rows(s):
    e = jnp.exp(s - jnp.max(s, axis=-1, keepdims=True))
    return e * (1.0 / jnp.sum(e, axis=-1, keepdims=True))


def _diff_attn_kernel(*refs, lam_init, n_q):
    q_refs = refs[:n_q]
    k_ref, v_ref, lam_ref, g_ref, o_ref = refs[n_q:]
    lp = lam_ref[...]
    lam = (jnp.exp(jnp.sum(lp[0:1] * lp[1:2], axis=-1, keepdims=True))
           - jnp.exp(jnp.sum(lp[2:3] * lp[3:4], axis=-1, keepdims=True)) + lam_init)
    q = jnp.concatenate([r[0] for r in q_refs], axis=0)
    lane = lax.broadcasted_iota(jnp.int32, (1, LANES), 1)
    first_map = (lane // 32) % 2 == 0
    zero = jnp.zeros_like(q)
    q0 = jnp.where(first_map, q, zero)
    q1 = jnp.where(first_map, zero, q)
    n_keys = k_ref.shape[1]
    blk = min(ATT_KEY_BLOCK, n_keys)
    ones = jnp.ones((blk, A_DV), BF16)
    state = [None, None]
    for kb in range(n_keys // blk):
        k = k_ref[0, kb * blk:(kb + 1) * blk, :]
        v1 = jnp.concatenate([v_ref[0, kb * blk:(kb + 1) * blk, :], ones], axis=1)
        for i, qm in enumerate((q0, q1)):
            s = _dot_nt(qm, k)
            m_new = jnp.max(s, axis=-1, keepdims=True)
            if kb:
                m_old, acc_old = state[i]
                m_new = jnp.maximum(m_old, m_new)
            acc = _dot(jnp.exp2(s - m_new).astype(BF16), v1)
            if kb:
                acc = acc_old * jnp.exp2(m_old - m_new) + acc
            state[i] = (m_new, acc)
    acc0, acc1 = state[0][1], state[1][1]
    o = acc0[:, :A_DV] * (1.0 / acc0[:, A_DV:]) - acc1[:, :A_DV] * (lam / acc1[:, A_DV:])
    o_ref[0] = (_rms(o, g_ref[...]) * (1.0 - lam_init)).astype(BF16)


def _diff_attention(p, lam_p, subln_g, lam_init):
    b, n, _ = p.shape
    half = ROW_TILE
    tq = ATT_Q_HALVES * half
    ctx_blocks = N_CTX // half
    width = A_HEADS * A_DV
    small = [pl.BlockSpec((4, A_DH), lambda i, h, *_: (0, 0)), pl.BlockSpec((1, A_DV), lambda i, h, *_: (0, 0))]
    q_specs = [pl.BlockSpec((1, half, LANES), functools.partial(lambda i, h, j, part: (i, ctx_blocks + ATT_Q_HALVES * j + part, h),
                                                                 part=part)) for part in range(ATT_Q_HALVES)]
    oa_x = pl.pallas_call(
        functools.partial(_diff_attn_kernel, lam_init=lam_init, n_q=ATT_Q_HALVES),
        out_shape=jax.ShapeDtypeStruct((b, n - N_CTX, width), BF16),
        grid=(b, A_HEADS, (n - N_CTX) // tq),
        in_specs=q_specs + [
            pl.BlockSpec((1, n, LANES), lambda i, h, j: (i, 0, A_HEADS + h)),
            pl.BlockSpec((1, n, LANES), lambda i, h, j: (i, 0, 2 * A_HEADS + h)),
        ] + small,
        out_specs=pl.BlockSpec((1, tq, LANES), lambda i, h, j: (i, j, h)),
        compiler_params=_cparams(("arbitrary", "arbitrary", "arbitrary")),
        name="diff_attention",
    )(*([p] * ATT_Q_HALVES), p, p, lam_p, subln_g.reshape(1, A_DV))
    oa_c = pl.pallas_call(
        functools.partial(_diff_attn_kernel, lam_init=lam_init, n_q=1),
        out_shape=jax.ShapeDtypeStruct((b, N_CTX, width), BF16),
        grid=(b, A_HEADS),
        in_specs=[
            pl.BlockSpec((1, N_CTX, LANES), lambda i, h: (i, 0, h)),
            pl.BlockSpec((1, N_CTX, LANES), lambda i, h: (i, 0, A_HEADS + h)),
            pl.BlockSpec((1, N_CTX, LANES), lambda i, h: (i, 0, 2 * A_HEADS + h)),
        ] + small,
        out_specs=pl.BlockSpec((1, N_CTX, LANES), lambda i, h: (i, 0, h)),
        compiler_params=_cparams(("arbitrary", "arbitrary")),
        name="context_diff_attention",
    )(p, p, p, lam_p, subln_g.reshape(1, A_DV))
    return oa_x, oa_c


def _split_heads(q):
    lane = lax.broadcasted_iota(jnp.int32, (1, LANES), 1)
    first_head = lane < B_DH
    zero = jnp.zeros_like(q)
    return first_head, (jnp.where(first_head, q, zero), jnp.where(first_head, zero, q))


def _na_ctx_kernel(q_ref, k_ref, v_ref, o_ref):
    first_head, qh = _split_heads(q_ref[0])
    kc = k_ref[0]
    vc = v_ref[0]
    outs = [_dot(_softmax_rows(_dot_nt(qh[h], kc)).astype(BF16), vc) for h in range(2)]
    o_ref[0] = jnp.where(first_head, outs[0], outs[1]).astype(BF16)


def _na_kernel(*refs):
    q_refs = refs[:-4]
    k_ref, v_ref, bm_ref, o_ref = refs[-4:]
    blk = pl.program_id(2)
    rows = (k_ref.shape[1] - N_CTX) // GRID_W
    sub_q = NA_SUB * GRID_W
    sub_k = NA_KROWS * GRID_W
    kc = k_ref[0, 0:N_CTX, :]
    vc = jnp.concatenate([v_ref[0, 0:N_CTX, :], jnp.ones((N_CTX, LANES), BF16)], axis=1)
    ones = jnp.ones((sub_k, LANES), BF16)
    for sub in range(NA_QROWS // NA_SUB):
        q_ref = q_refs[sub * sub_q // ROW_TILE]
        q0 = (sub * sub_q) % ROW_TILE
        first_head, qh = _split_heads(q_ref[0, q0:q0 + sub_q, :])
        k_row0 = jnp.clip(blk * NA_QROWS + sub * NA_SUB - NA_ROWS // 2, 0, rows - NA_KROWS)
        start = pl.multiple_of(N_CTX + k_row0 * GRID_W, GRID_W)
        kw = k_ref[0, pl.ds(start, sub_k), :]
        vw = jnp.concatenate([v_ref[0, pl.ds(start, sub_k), :], ones], axis=1)
        outs = []
        for h in range(2):
            s_loc = _dot_nt(qh[h], kw) + bm_ref[h, 0, sub]
            s_ctx = _dot_nt(qh[h], kc)
            m = jnp.maximum(jnp.max(s_loc, axis=-1, keepdims=True), jnp.max(s_ctx, axis=-1, keepdims=True))
            acc = _dot(jnp.exp(s_loc - m).astype(BF16), vw) + _dot(jnp.exp(s_ctx - m).astype(BF16), vc)
            outs.append(acc[:, :LANES] * (1.0 / acc[:, LANES:]))
        o_ref[0, sub * sub_q:(sub + 1) * sub_q, :] = jnp.where(first_head, outs[0], outs[1]).astype(BF16)


def _na_bias_table(na_bias, rows):
    h = na_bias.shape[0]
    n_dr, n_dc = 2 * NA_ROWS - 1, 2 * NA_COLS - 1
    width = 2 * GRID_W
    left = GRID_W - NA_COLS
    u = jnp.pad(na_bias, ((0, 0), (0, 0), (left, width - left - n_dc)))
    skew = jnp.tile(u, (1, 1, GRID_W))[:, :, :GRID_W * (width - 1)].reshape(h, n_dr, GRID_W, width - 1)
    toeplitz = skew[:, :, :, GRID_W - 1:]
    margin = NA_KROWS - NA_ROWS
    by_col = jnp.pad(toeplitz.transpose(0, 2, 1, 3), ((0, 0), (0, 0), (margin, margin), (0, 0)))
    by_col = by_col.reshape(h, GRID_W, (n_dr + 2 * margin) * GRID_W)
    col = np.arange(GRID_W)
    col_start = np.clip(col - NA_COLS // 2, 0, GRID_W - NA_COLS)
    col_ok = (col[None, :] >= col_start[:, None]) & (col[None, :] < col_start[:, None] + NA_COLS)
    n_sub = NA_QROWS // NA_SUB
    runs = []
    for q_row0 in (0, NA_QROWS, rows - NA_QROWS):
        for sub in range(n_sub):
            k_row0 = int(np.clip(q_row0 + sub * NA_SUB - NA_ROWS // 2, 0, rows - NA_KROWS))
            kr = k_row0 + np.arange(NA_KROWS)
            firsts, oks = [], []
            for rq in range(NA_SUB):
                r = q_row0 + sub * NA_SUB + rq
                r0 = int(np.clip(r - NA_ROWS // 2, 0, rows - NA_ROWS))
                row_ok = (kr >= r0) & (kr < r0 + NA_ROWS)
                oks.append((col_ok[:, None, :] & row_ok[None, :, None]).reshape(GRID_W, NA_KROWS * GRID_W))
                firsts.append(k_row0 - r + NA_ROWS - 1 + margin)
            key = (tuple(firsts), np.stack(oks).tobytes())
            if runs and runs[-1][0] == key:
                runs[-1][2] += 1
                continue
            windows = [by_col[:, :, f * GRID_W:(f + NA_KROWS) * GRID_W] for f in firsts]
            block = jnp.where(np.concatenate(oks, axis=0)[None], jnp.concatenate(windows, axis=1), -jnp.inf)
            runs.append([key, block, 1])
    table = jnp.concatenate([jnp.broadcast_to(blk[:, None], (h, rep) + blk.shape[1:]) for _, blk, rep in runs], axis=1)
    return table.reshape(h, 3, n_sub, NA_SUB * GRID_W, NA_KROWS * GRID_W)


def _neighbourhood_attention(p, bm):
    b, n, _ = p.shape
    tq = NA_QROWS * GRID_W
    parts = tq // ROW_TILE
    n_blocks = (n - N_CTX) // tq
    ctx_blocks = N_CTX // ROW_TILE
    col0 = 3 * A_HEADS
    pairs = B_HEADS // 2
    width = B_HEADS * B_DH

    def pattern(j):
        return jnp.minimum(j, 1) + (j == n_blocks - 1).astype(jnp.int32)

    ob_x = pl.pallas_call(
        _na_kernel,
        out_shape=jax.ShapeDtypeStruct((b, n - N_CTX, width), BF16),
        grid=(b, pairs, n_blocks),
        in_specs=[pl.BlockSpec((1, ROW_TILE, LANES),
                               functools.partial(lambda i, h, j, part: (i, ctx_blocks + parts * j + part, col0 + h),
                                                 part=part)) for part in range(parts)] + [
            pl.BlockSpec((1, n, LANES), lambda i, h, j: (i, 0, col0 + pairs + h)),
            pl.BlockSpec((1, n, LANES), lambda i, h, j: (i, 0, col0 + 2 * pairs + h)),
            pl.BlockSpec((2, 1, NA_QROWS // NA_SUB, NA_SUB * GRID_W, NA_KROWS * GRID_W),
                         lambda i, h, j: (h, pattern(j), 0, 0, 0)),
        ],
        out_specs=pl.BlockSpec((1, tq, LANES), lambda i, h, j: (i, j, h)),
        compiler_params=_cparams(("arbitrary", "arbitrary", "arbitrary")),
        name="neighbourhood_attention",
    )(*([p] * parts), p, p, bm)
    ob_c = pl.pallas_call(
        _na_ctx_kernel,
        out_shape=jax.ShapeDtypeStruct((b, N_CTX, width), BF16),
        grid=(b, pairs),
        in_specs=[
            pl.BlockSpec((1, N_CTX, LANES), lambda i, h: (i, 0, col0 + h)),
            pl.BlockSpec((1, N_CTX, LANES), lambda i, h: (i, 0, col0 + pairs + h)),
            pl.BlockSpec((1, N_CTX, LANES), lambda i, h: (i, 0, col0 + 2 * pairs + h)),
        ],
        out_specs=pl.BlockSpec((1, N_CTX, LANES), lambda i, h: (i, 0, h)),
        compiler_params=_cparams(("arbitrary", "arbitrary")),
        name="context_attention",
    )(p, p, p)
    return ob_x, ob_c


def _gla_kernel(q_ref, k_ref, v_ref, lr_ref, wg_ref, bg_ref, o_ref, ob_ref, g_ref, sf_ref, sb_ref):
    n = q_ref.shape[1]
    blk = GLA_BLOCK
    per_blk = blk // C_CHUNK
    n_blocks = n // blk
    ri = lax.broadcasted_iota(jnp.int32, (blk, blk), 0)
    ci = lax.broadcasted_iota(jnp.int32, (blk, blk), 1)
    same_chunk = (ri // C_CHUNK) == (ci // C_CHUNK)
    keeps = (same_chunk & (ci <= ri), same_chunk & (ci >= ri))
    tris = tuple(jnp.where(kp, 1.0, 0.0).astype(BF16) for kp in keeps)
    w_gate = jnp.concatenate([wg_ref[0], wg_ref[1]], axis=1).astype(BF16)
    b_gate = jnp.concatenate([bg_ref[0:1, :], bg_ref[1:2, :]], axis=1)
    for i in range(n_blocks):
        z = _dot(lr_ref[0, i * blk:(i + 1) * blk, :].astype(BF16), w_gate) + b_gate
        g_ref[i * blk:(i + 1) * blk, :] = (jnp.minimum(z, 0.0) - jnp.log(1.0 + jnp.exp(-jnp.abs(z)))) * (1.0 / C_TAU)

    def block(sb, direction, st_ref):
        keep = keeps[direction]
        end_row = C_CHUNK - 1 if direction == 0 else 0
        mid_row = C_CHUNK // 2 - 1 if direction == 0 else C_CHUNK // 2
        r0 = sb * blk
        g = g_ref[r0:r0 + blk, direction * C_DK:(direction + 1) * C_DK]
        g_hi = g.astype(BF16)
        g_lo = (g - g_hi.astype(F32)).astype(BF16)
        gc2 = _dot(tris[direction], jnp.concatenate([g_hi, g_lo], axis=1))
        gc = (gc2[:, :C_DK] + gc2[:, C_DK:]).reshape(per_blk, C_CHUNK, C_DK)
        g_end = gc[:, end_row:end_row + 1, :]
        g_mid = gc[:, mid_row:mid_row + 1, :]
        q = (q_ref[0, r0:r0 + blk, :].astype(F32) * (C_DK ** -0.5)).reshape(per_blk, C_CHUNK, C_DK)
        k = k_ref[0, r0:r0 + blk, :].astype(F32).reshape(per_blk, C_CHUNK, C_DK)
        v = v_ref[0, r0:r0 + blk, :]
        q_in = (q * jnp.exp(gc)).astype(BF16).reshape(blk, C_DK)
        q_mid = (q * jnp.exp(gc - g_mid)).astype(BF16).reshape(blk, C_DK)
        k_mid = (k * jnp.exp(g_mid - gc)).astype(BF16).reshape(blk, C_DK)
        k_end = (k * jnp.exp(g_end - gc)).astype(BF16).reshape(blk, C_DK)
        a = jnp.where(keep, _dot_nt(q_mid, k_mid), 0.0)
        o_intra = _dot(a.astype(BF16), v)
        ends = jnp.concatenate([g_end.reshape(per_blk, C_DK), jnp.zeros((8 - per_blk, C_DK), F32)], axis=0)
        decay = jnp.transpose(jnp.exp(ends))
        o_inter = [None] * per_blk
        for c in (range(per_blk) if direction == 0 else reversed(range(per_blk))):
            rows = slice(c * C_CHUNK, (c + 1) * C_CHUNK)
            st = st_ref[...]
            o_inter[c] = _dot(q_in[rows], st.astype(BF16))
            st_ref[...] = st * decay[:, c:c + 1] + _dot_tn(k_end[rows], v[rows])
        return o_intra + jnp.concatenate(o_inter, axis=0)

    sf_ref[...] = jnp.zeros_like(sf_ref)
    sb_ref[...] = jnp.zeros_like(sb_ref)
    ctx_blocks = N_CTX // blk
    order_b = list(reversed(range(ctx_blocks))) + list(reversed(range(ctx_blocks, n_blocks)))
    for i in range(n_blocks):
        o_ref[0, i * blk:(i + 1) * blk, :] = block(i, 0, sf_ref)
        sb = order_b[i]
        ob_ref[sb * blk:(sb + 1) * blk, :] = block(sb, 1, sb_ref)
    o_ref[0] = o_ref[0] + ob_ref[...]


def _gla(p, lr, wg, bg):
    b, n, _ = p.shape
    return pl.pallas_call(
        _gla_kernel,
        out_shape=jax.ShapeDtypeStruct((b, n, C_HEADS * C_DV), F32),
        grid=(b, C_HEADS),
        in_specs=[
            pl.BlockSpec((1, n, C_DK), lambda i, h: (i, 0, h)),
            pl.BlockSpec((1, n, C_DK), lambda i, h: (i, 0, C_HEADS + h)),
            pl.BlockSpec((1, n, C_DV), lambda i, h: (i, 0, C_HEADS + h)),
            pl.BlockSpec((1, n, LANES), lambda i, h: (i, 0, 0)),
            pl.BlockSpec((2, LANES, C_DK), lambda i, h: (0, 0, h)),
            pl.BlockSpec((2, C_DK), lambda i, h: (0, h)),
        ],
        out_specs=pl.BlockSpec((1, n, C_DV), lambda i, h: (i, 0, h)),
        scratch_shapes=[pltpu.VMEM((n, C_DV), F32), pltpu.VMEM((n, 2 * C_DK), F32),
                        pltpu.VMEM((C_DK, C_DV), F32), pltpu.VMEM((C_DK, C_DV), F32)],
        compiler_params=_cparams(("arbitrary", "arbitrary")),
        name="gla",
    )(p, p, p, lr, wg, bg)


def _out_kernel(*refs, gla):
    if gla:
        o_ref, gate_ref, gn_ref, w_ref, s_ref, mod_ref, ng_ref, rw_ref, snew_ref, h2_ref, lg_ref = refs
        o = o_ref[0]
        gate = gate_ref[0].astype(F32)
        parts = []
        for hd in range(C_HEADS):
            oh = _rms(o[:, hd * C_DV:(hd + 1) * C_DV], gn_ref[...])
            parts.append((oh * _silu(gate[:, hd * C_DV:(hd + 1) * C_DV])).astype(BF16))
        acc = _dot(jnp.concatenate(parts, axis=-1), w_ref[...])
    else:
        oax_ref, oac_ref, obx_ref, obc_ref, w_ref, s_ref, mod_ref, ng_ref, rw_ref, snew_ref, h2_ref, lg_ref = refs
        half = oax_ref.shape[2]
        is_ctx = pl.program_id(1) == 0
        oa = jnp.where(is_ctx, oac_ref[0], oax_ref[0])
        ob = jnp.where(is_ctx, obc_ref[0], obx_ref[0])
        acc = _dot(oa, w_ref[0:half, :]) + _dot(ob, w_ref[half:, :])
    x = s_ref[0] + mod_ref[0, 0, 2:3, :] * acc
    snew_ref[0] = x
    h2 = _rms(x, ng_ref[...]) * (1.0 + mod_ref[0, 0, 4:5, :]) + mod_ref[0, 0, 3:4, :]
    _store_token_rows(h2_ref, h2)
    t = _dot(h2.astype(BF16), rw_ref[...])
    lg_ref[0] = t + pltpu.roll(t, LANES - N_EXPERTS, 1)


def _out_project(mix, w_out, s, mod, ng, rw, *, gla, gn=None):
    b, n, d = s.shape
    tm = ROW_TILE
    row_spec = pl.BlockSpec((1, tm, d), lambda i, j: (i, j, 0))
    const2 = lambda i, j: (0, 0)
    if gla:
        o, p = mix
        in_specs = [row_spec, pl.BlockSpec((1, tm, d), lambda i, j: (i, j, 2)),
                    pl.BlockSpec((1, C_DV), const2)]
        args = [o, p, gn.reshape(1, C_DV)]
    else:
        oa_x, oa_c, ob_x, ob_c = mix
        half = oa_x.shape[2]
        latent = pl.BlockSpec((1, tm, half), lambda i, j: (i, jnp.maximum(j - 1, 0), 0))
        context = pl.BlockSpec((1, tm, half), lambda i, j: (i, 0, 0))
        in_specs = [latent, context, latent, context]
        args = [oa_x, oa_c, ob_x, ob_c]
    in_specs += [pl.BlockSpec((d, d), const2), row_spec,
                 pl.BlockSpec((1, 1, 6, d), lambda i, j: (i, jnp.minimum(j, 1), 0, 0)),
                 pl.BlockSpec((1, d), const2), pl.BlockSpec((d, LANES), const2)]
    args += [w_out, s, mod, ng.reshape(1, d), rw]
    return pl.pallas_call(
        functools.partial(_out_kernel, gla=gla),
        out_shape=[jax.ShapeDtypeStruct((b, n, d), F32), jax.ShapeDtypeStruct((b * n * ROW_SEGS, LANES), jnp.int32),
                   jax.ShapeDtypeStruct((b, n, LANES), F32)],
        grid=(b, n // tm),
        in_specs=in_specs,
        out_specs=[row_spec, pl.BlockSpec((tm * ROW_SEGS, LANES), lambda i, j: (i * (n // tm) + j, 0)),
                   pl.BlockSpec((1, tm, LANES), lambda i, j: (i, j, 0))],
        compiler_params=_cparams(("arbitrary", "arbitrary")),
        name="out_project",
    )(*args)


def _route_kernel(bias_ref, lg_ref, cls_ref, wlo_ref, whi_ref, rank_ref, cnt_ref):
    score = [_sigmoid(lg_ref[e]) for e in range(N_EXPERTS)]
    sel = [score[e] + bias_ref[e] for e in range(N_EXPERTS)]
    grp_score = []
    for g in range(N_GROUPS):
        v = sel[g * PER_GROUP:(g + 1) * PER_GROUP]
        best = v[0] + v[1]
        for a in range(PER_GROUP):
            for c in range(a + 1, PER_GROUP):
                if (a, c) != (0, 1):
                    best = jnp.maximum(best, v[a] + v[c])
        grp_score.append(best)
    grp = jnp.zeros(grp_score[0].shape, jnp.int32)
    best = grp_score[0]
    for g in range(1, N_GROUPS):
        upd = grp_score[g] > best
        best = jnp.where(upd, grp_score[g], best)
        grp = jnp.where(upd, g, grp)

    def pick(vals, j):
        out = vals[j]
        for g in range(1, N_GROUPS):
            out = jnp.where(grp == g, vals[g * PER_GROUP + j], out)
        return out

    v = [pick(sel, j) for j in range(PER_GROUP)]
    sc = [pick(score, j) for j in range(PER_GROUP)]
    one = jnp.ones(grp.shape, jnp.int32)
    zero = jnp.zeros(grp.shape, jnp.int32)
    chosen = []
    for j in range(PER_GROUP):
        rank = zero
        for m in range(PER_GROUP):
            if m == j:
                continue
            ahead = (v[m] >= v[j]) if m < j else (v[m] > v[j])
            rank = rank + jnp.where(ahead, one, zero)
        chosen.append(rank < 2)
    code = zero
    for j in range(PER_GROUP):
        code = code + jnp.where(chosen[j], one * (1 << j), zero)
    pair = zero
    for idx in range(6):
        pair = jnp.where(code == (1 << PAIR_LO[idx]) + (1 << PAIR_HI[idx]), idx, pair)
    s_lo = jnp.where(chosen[0], sc[0], jnp.where(chosen[1], sc[1], sc[2]))
    s_hi = jnp.where(chosen[3], sc[3], jnp.where(chosen[2], sc[2], sc[1]))
    den = s_lo + s_hi
    cls = grp * 6 + pair
    cls_ref[...] = cls
    wlo_ref[...] = s_lo / den
    whi_ref[...] = s_hi / den
    r = cls.shape[0]
    lane = lax.broadcasted_iota(jnp.int32, (1, LANES), 1)
    li = lax.broadcasted_iota(jnp.int32, (LANES, LANES), 0)
    lj = lax.broadcasted_iota(jnp.int32, (LANES, LANES), 1)
    upto_lane = jnp.where(li <= lj, 1.0, 0.0).astype(BF16)
    ri = lax.broadcasted_iota(jnp.int32, (r, r), 0)
    rj = lax.broadcasted_iota(jnp.int32, (r, r), 1)
    earlier_rows = jnp.where(rj < ri, 1.0, 0.0).astype(BF16)
    member = [jnp.where(cls == c, 1.0, 0.0) for c in range(N_CLASSES)]
    row_totals = jnp.zeros((r, LANES), F32)
    for c in range(N_CLASSES):
        row_totals = row_totals + jnp.where(lane == c, jnp.sum(member[c], axis=1, keepdims=True), 0.0)
    before_row = _dot(earlier_rows, row_totals.astype(BF16))
    rank = jnp.zeros((r, LANES), F32)
    for c in range(N_CLASSES):
        in_row = _dot(member[c].astype(BF16), upto_lane)
        offset = jnp.sum(jnp.where(lane == c, before_row, 0.0), axis=1, keepdims=True)
        rank = rank + member[c] * (in_row + offset - 1.0)
    rank_ref[...] = rank.astype(jnp.int32)
    counts = before_row[r - 1:r, :] + row_totals[r - 1:r, :]
    cnt_ref[...] = jnp.broadcast_to(counts, cnt_ref.shape).astype(jnp.int32)


def _route(logits_t, router_bias):
    _, r, _ = logits_t.shape
    full = pl.BlockSpec((r, LANES), lambda i: (0, 0))
    return pl.pallas_call(
        _route_kernel,
        out_shape=[jax.ShapeDtypeStruct((r, LANES), jnp.int32), jax.ShapeDtypeStruct((r, LANES), F32),
                   jax.ShapeDtypeStruct((r, LANES), F32), jax.ShapeDtypeStruct((r, LANES), jnp.int32),
                   jax.ShapeDtypeStruct((8, LANES), jnp.int32)],
        grid=(1,),
        in_specs=[pl.BlockSpec(memory_space=pltpu.SMEM),
                  pl.BlockSpec((N_EXPERTS, r, LANES), lambda i: (0, 0, 0))],
        out_specs=[full, full, full, full, pl.BlockSpec((8, LANES), lambda i: (0, 0))],
        compiler_params=_cparams(("arbitrary",)),
        name="route",
    )(router_bias, logits_t)


def _gather_rows(table, idx):
    p = idx.shape[0]
    per_worker = p // SC_WORKERS
    n_chunks = per_worker // SC_GATHER_ROWS
    mesh = plsc.VectorSubcoreMesh(core_axis_name="c", subcore_axis_name="s", num_cores=SC_CORES,
                                  num_subcores=SC_SUBCORES)

    @functools.partial(
        pl.kernel, mesh=mesh,
        out_type=jax.ShapeDtypeStruct((p,) + table.shape[1:], table.dtype),
        scratch_types=[pltpu.VMEM((per_worker,), jnp.int32),
                       pltpu.VMEM((SC_GATHER_ROWS,) + table.shape[1:], table.dtype),
                       pltpu.SemaphoreType.DMA],
        compiler_params=pltpu.CompilerParams(use_tc_tiling_on_sc=True),
        name="gather_rows",
    )
    def gather(table_hbm, idx_hbm, out_hbm, idx_v, rows_v, sem):
        worker = lax.axis_index("s") * SC_CORES + lax.axis_index("c")
        base = worker * per_worker
        pltpu.sync_copy(idx_hbm.at[pl.ds(base, per_worker)], idx_v)

        @pl.loop(0, n_chunks)
        def _(i):
            off = pl.multiple_of(i * SC_GATHER_ROWS, SC_GATHER_ROWS)
            pltpu.async_copy(table_hbm.at[idx_v.at[pl.ds(off, SC_GATHER_ROWS)]], rows_v, sem).wait()
            pltpu.sync_copy(rows_v, out_hbm.at[pl.ds(base + off, SC_GATHER_ROWS)])

    return gather(table, idx)


def _scatter_rows(src, idx, n_out):
    t = idx.shape[0]
    per_worker = t // SC_WORKERS
    n_chunks = per_worker // SC_GATHER_ROWS
    mesh = plsc.VectorSubcoreMesh(core_axis_name="c", subcore_axis_name="s", num_cores=SC_CORES,
                                  num_subcores=SC_SUBCORES)

    @functools.partial(
        pl.kernel, mesh=mesh,
        out_type=jax.ShapeDtypeStruct((n_out,) + src.shape[1:], src.dtype),
        scratch_types=[pltpu.VMEM((n_chunks, SC_GATHER_ROWS), jnp.int32),
                       pltpu.VMEM((SC_GATHER_ROWS,) + src.shape[1:], src.dtype),
                       pltpu.SemaphoreType.DMA],
        compiler_params=pltpu.CompilerParams(use_tc_tiling_on_sc=True),
        name="scatter_rows",
    )
    def scatter(src_hbm, idx_hbm, out_hbm, idx_v, rows_v, sem):
        worker = lax.axis_index("s") * SC_CORES + lax.axis_index("c")
        base = worker * per_worker
        pltpu.sync_copy(idx_hbm.at[worker], idx_v)

        @pl.loop(0, n_chunks)
        def _(i):
            off = pl.multiple_of(i * SC_GATHER_ROWS, SC_GATHER_ROWS)
            pltpu.sync_copy(src_hbm.at[pl.ds(base + off, SC_GATHER_ROWS)], rows_v)
            pltpu.async_copy(rows_v, out_hbm.at[idx_v.at[i]], sem).wait()

    return scatter(src, idx.reshape(SC_WORKERS, n_chunks, SC_GATHER_ROWS))


def _cast_kernel(*refs):
    n = (len(refs) - 1) // 2
    for i in range(n):
        refs[n + 1 + i][0] = refs[i][0, 0].astype(BF16)


def _cast_expert_weights(layer, w1, w3, w2, after):
    ws = (w1, w3, w2)
    e = w1.shape[1]
    in_specs = [pl.BlockSpec((1, 1) + w.shape[2:], lambda j: (layer, j, 0, 0)) for w in ws]
    in_specs.append(pl.BlockSpec(memory_space=pl.ANY))
    return pl.pallas_call(
        _cast_kernel,
        out_shape=[jax.ShapeDtypeStruct(w.shape[1:], BF16) for w in ws],
        grid=(e,),
        in_specs=in_specs,
        out_specs=[pl.BlockSpec((1,) + w.shape[2:], lambda j: (j, 0, 0)) for w in ws],
        compiler_params=_cparams(("arbitrary",)),
        name="cast_expert_weights",
    )(*ws, after)


def _moe_kernel(lo_ref, hi_ref, nt_ref, x_ref, wr_ref, w1l_ref, w1h_ref, w3l_ref, w3h_ref, w2l_ref, w2h_ref, y_ref):
    j = pl.program_id(0)

    @pl.when(j < nt_ref[0])
    def _():
        x = _token_rows(x_ref).astype(BF16)
        wr = wr_ref[...]
        he_lo = (_silu(_dot(x, w1l_ref[0])) * _dot(x, w3l_ref[0]) * wr[:, 0:1]).astype(BF16)
        he_hi = (_silu(_dot(x, w1h_ref[0])) * _dot(x, w3h_ref[0]) * wr[:, 1:2]).astype(BF16)
        y = _dot(he_lo, w2l_ref[0]) + _dot(he_hi, w2h_ref[0])
        _store_token_rows(y_ref, y)

    @pl.when(j >= nt_ref[0])
    def _():
        y_ref[...] = jnp.zeros_like(y_ref)


def _moe_experts(xs, wrow, tile_lo, tile_hi, n_tiles, w1, w3, w2):
    tp = xs.shape[0] // ROW_SEGS
    tm = MOE_TILE
    d, de = w1.shape[1], w1.shape[2]

    def gate_map(j, lo, hi, nt):
        return (jnp.minimum(j, nt[0] - 1), 0)

    def lo_map(j, lo, hi, nt):
        return (lo[jnp.minimum(j, nt[0] - 1)], 0, 0)

    def hi_map(j, lo, hi, nt):
        return (hi[jnp.minimum(j, nt[0] - 1)], 0, 0)

    up = (1, d, de)
    down = (1, de, d)
    return pl.pallas_call(
        _moe_kernel,
        out_shape=jax.ShapeDtypeStruct((tp * ROW_SEGS, LANES), jnp.int32),
        grid_spec=pltpu.PrefetchScalarGridSpec(
            num_scalar_prefetch=3,
            grid=(tp // tm,),
            in_specs=[pl.BlockSpec((tm * ROW_SEGS, LANES), gate_map), pl.BlockSpec((tm, LANES), gate_map),
                      pl.BlockSpec(up, lo_map), pl.BlockSpec(up, hi_map),
                      pl.BlockSpec(up, lo_map), pl.BlockSpec(up, hi_map),
                      pl.BlockSpec(down, lo_map), pl.BlockSpec(down, hi_map)],
            out_specs=pl.BlockSpec((tm * ROW_SEGS, LANES), lambda j, lo, hi, nt: (j, 0)),
        ),
        compiler_params=_cparams(("arbitrary",)),
        name="moe_experts",
    )(tile_lo, tile_hi, n_tiles, xs, wrow, w1, w1, w3, w3, w2, w2)


def _moe(h2, logits, router_bias, w1, w3, w2):
    t = h2.shape[0]
    tm = MOE_TILE
    n_tiles_max = t // tm + N_CLASSES
    tp = n_tiles_max * tm
    lg_t = logits[:, :N_EXPERTS].T.reshape(N_EXPERTS, t // LANES, LANES)
    cls, wlo, whi, rank, counts = _route(lg_t, router_bias)
    cls, wlo, whi, rank = cls.reshape(t), wlo.reshape(t), whi.reshape(t), rank.reshape(t)
    counts = counts[0, :N_CLASSES]
    classes = jnp.arange(N_CLASSES, dtype=jnp.int32)
    onehot = (cls[:, None] == classes[None, :]).astype(jnp.int32)
    tiles_per = (counts + tm - 1) // tm
    tile_end = jnp.cumsum(tiles_per)
    tile_start = tile_end - tiles_per
    dest = jnp.sum(onehot * (tile_start * tm)[None, :], axis=1) + rank
    tiles = jnp.arange(n_tiles_max, dtype=jnp.int32)
    tile_cls = jnp.minimum(jnp.sum((tile_end[None, :] <= tiles[:, None]).astype(jnp.int32), axis=1), N_CLASSES - 1)
    pair = tile_cls % 6
    base = (tile_cls // 6) * PER_GROUP
    pair_onehot = (pair[:, None] == jnp.arange(6, dtype=jnp.int32)[None, :]).astype(jnp.int32)
    tile_lo = base + jnp.sum(pair_onehot * jnp.asarray(PAIR_LO, jnp.int32)[None, :], axis=1)
    tile_hi = base + jnp.sum(pair_onehot * jnp.asarray(PAIR_HI, jnp.int32)[None, :], axis=1)
    gates = jnp.pad(jnp.stack([wlo, whi], axis=1), ((0, 0), (0, LANES - 2))).reshape(t, 1, LANES)
    wrow = _scatter_rows(gates, dest, tp).reshape(tp, LANES)
    xs = _scatter_rows(h2, dest, tp).reshape(tp * ROW_SEGS, LANES)
    ys = _moe_experts(xs, wrow, tile_lo, tile_hi, tile_end[-1:], w1, w3, w2)
    return _gather_rows(ys.reshape(tp, ROW_SEGS, LANES), dest), dest


def _final_kernel(s_ref, y_ref, mod_ref, g_ref, o_ref):
    x = s_ref[0] + mod_ref[0, 0, 5:6, :] * _token_rows(y_ref)
    o_ref[0] = _rms(x, g_ref[...])


def _final(s, y, mod, final_g):
    b, n, d = s.shape
    tm = ROW_TILE
    skip = N_CTX // tm
    row_spec = pl.BlockSpec((1, tm, d), lambda i, j: (i, j + skip, 0))
    return pl.pallas_call(
        _final_kernel,
        out_shape=jax.ShapeDtypeStruct((b, n - N_CTX, d), F32),
        grid=(b, (n - N_CTX) // tm),
        in_specs=[row_spec, pl.BlockSpec((tm * ROW_SEGS, LANES), lambda i, j: (i * (n // tm) + j + skip, 0)),
                  pl.BlockSpec((1, 1, 6, d), lambda i, j: (i, 1, 0, 0)),
                  pl.BlockSpec((1, d), lambda i, j: (0, 0))],
        out_specs=pl.BlockSpec((1, tm, d), lambda i, j: (i, j, 0)),
        compiler_params=_cparams(("arbitrary", "arbitrary")),
        name="final_norm",
    )(s, y, mod, final_g.reshape(1, d))


def _rope_tables(seq):
    pos = jnp.arange(seq)
    row_pos, col_pos = pos // GRID_W, pos % GRID_W
    n = A_DH // 2
    inv = ROPE_BASE ** (-jnp.arange(0, n, 2, dtype=F32) / n)
    ang_r = row_pos.astype(F32)[:, None] * inv[None, :]
    ang_c = col_pos.astype(F32)[:, None] * inv[None, :]
    ang = jnp.concatenate([ang_r, ang_c], axis=-1)
    cos = jnp.tile(jnp.cos(ang), (1, 4))
    sin = jnp.tile(jnp.sin(ang), (1, 4))
    sin = jnp.concatenate([-sin[:, :LANES // 2], sin[:, LANES // 2:]], axis=-1)
    cos = jnp.concatenate([jnp.ones((N_CTX, LANES), F32), cos], axis=0)
    sin = jnp.concatenate([jnp.zeros((N_CTX, LANES), F32), sin], axis=0)
    return cos, sin


def _interleave_maps(w):
    d = w.shape[0]
    w = w.reshape(d, A_HEADS, 2, 2, 2, A_DH // 4)
    return w.transpose(0, 1, 4, 2, 3, 5).reshape(d, A_HEADS * LANES)


def _att_weights(w_in):
    a_qk = A_HEADS * 2 * A_DH
    a_v = A_HEADS * A_DV
    b_w = B_HEADS * B_DH
    qa = _interleave_maps(w_in[:, :a_qk]) * (A_DH ** -0.5 * LOG2_E)
    ka = _interleave_maps(w_in[:, a_qk:2 * a_qk])
    va = w_in[:, 2 * a_qk:2 * a_qk + a_v]
    o = 2 * a_qk + a_v
    qb = w_in[:, o:o + b_w] * (B_DH ** -0.5)
    rest = w_in[:, o + b_w:]
    return jnp.concatenate([qa, ka, va, qb, rest], axis=1).astype(BF16)


def _gla_weights(w_in):
    d = w_in.shape[0]
    n_main = 2 * C_HEADS * C_DK + 2 * C_HEADS * C_DV
    pad = jnp.zeros((d, LANES - 2 * C_RANK), w_in.dtype)
    return jnp.concatenate([w_in, pad], axis=1).astype(BF16), n_main


def kernel(x, c, ctx, c_ctx, w_mod, b_mod, norm_g, final_g, att_w_in, att_w_out, att_lambda, att_subln_g, na_bias,
           gla_w_in, gla_w_gate, gla_b_gate, gla_norm_g, gla_w_out, router_w, router_bias, moe_w1, moe_w3, moe_w2):
    b, seq, d = x.shape
    n = N_CTX + seq
    s = (ctx, x)

    rows = b + 1
    rows_pad = -(-rows // 8) * 8
    cc = jnp.concatenate([c, c_ctx[None, :], jnp.zeros((rows_pad - rows, d), F32)], axis=0)
    mod_all = _mod_vectors(cc, w_mod, b_mod)
    mod_x = mod_all[:, :b].reshape(DEPTH, b, 1, 6, d)
    mod_c = jnp.broadcast_to(mod_all[:, b].reshape(DEPTH, 1, 1, 6, d), (DEPTH, b, 1, 6, d))
    mods = jnp.concatenate([mod_c, mod_x], axis=2)

    cos, sin = _rope_tables(seq)
    rw_hi = router_w.astype(BF16)
    rw_lo = (router_w - rw_hi.astype(F32)).astype(BF16)
    zpad = jnp.zeros((d, LANES - 2 * N_EXPERTS), BF16)
    rw = jnp.concatenate([rw_hi, rw_lo, zpad], axis=1)

    y = None
    order_after = router_bias
    for i in range(DEPTH):
        j = i // 2
        modp = mods[i - 1] if i else None
        if i % 2 == 0:
            lam_init = 0.8 - 0.6 * math.exp(-0.3 * i)
            w = _att_weights(att_w_in[j])
            outs = _project(s, y, modp, mods[i], norm_g[i, 0], w, cos, sin,
                            n_rope=2 * A_HEADS * LANES, n_bf16=w.shape[1])
            s = outs[0]
            p = outs[-1]
            oa_x, oa_c = _diff_attention(p, att_lambda[j], att_subln_g[j], lam_init)
            ob_x, ob_c = _neighbourhood_attention(p, _na_bias_table(na_bias[j], seq // GRID_W))
            s, h2, logits = _out_project((oa_x, oa_c, ob_x, ob_c), att_w_out[j].astype(BF16), s, mods[i], norm_g[i, 1],
                                         rw, gla=False)
        else:
            w, n_main = _gla_weights(gla_w_in[j])
            outs = _project(s, y, modp, mods[i], norm_g[i, 0], w, None, None, n_rope=0, n_bf16=n_main)
            s, p, lr = outs
            wg = jnp.zeros((2, LANES, C_HEADS * C_DK), F32)
            wg = wg.at[0, :C_RANK].set(gla_w_gate[j, 0]).at[1, C_RANK:2 * C_RANK].set(gla_w_gate[j, 1])
            o = _gla(p, lr, wg, gla_b_gate[j])
            s, h2, logits = _out_project((o, p), gla_w_out[j].astype(BF16), s, mods[i], norm_g[i, 1],
                                         rw, gla=True, gn=gla_norm_g[j])
        if i == 0:
            expert_w = _cast_expert_weights(0, moe_w1, moe_w3, moe_w2, order_after)
        y, order_after = _moe(h2.reshape(b * n, ROW_SEGS, LANES), logits.reshape(b * n, LANES), router_bias, *expert_w)
        y = y.reshape(b * n * ROW_SEGS, LANES)
        if i + 1 < DEPTH:
            expert_w = _cast_expert_weights(i + 1, moe_w1, moe_w3, moe_w2, order_after)
    return _final(s, y, mods[DEPTH - 1], final_g)
```

```python
import functools
import math

import jax
import jax.numpy as jnp
import numpy as np
from jax import lax
from jax.experimental import pallas as pl
from jax.experimental.pallas import tpu as pltpu
from jax.experimental.pallas import tpu_sc as plsc

F32 = jnp.float32
BF16 = jnp.bfloat16

D_MODEL = 1024
DEPTH = 4
GRID_W = 64
N_CTX = 256
A_HEADS = 4
A_DH = 64
A_DV = 128
B_HEADS = 8
B_DH = 64
NA_ROWS = 8
NA_COLS = 16
C_HEADS = 4
C_DK = 128
C_DV = 256
C_RANK = 16
C_TAU = 16.0
C_CHUNK = 64
GLA_BLOCK = 256
N_EXPERTS = 16
N_GROUPS = 4
PER_GROUP = 4
ROPE_BASE = 10000.0
EPS = 1e-6
LOG2_E = math.log2(math.e)
ATT_KEY_BLOCK = 768
ATT_Q_HALVES = 4

LANES = 128
ROW_TILE = 256
MOE_TILE = 256
SC_CORES = 2
SC_SUBCORES = 16
SC_WORKERS = SC_CORES * SC_SUBCORES
SC_GATHER_ROWS = 64
N_CLASSES = N_GROUPS * 6
NA_QROWS = 16
NA_SUB = 2
NA_KROWS = NA_SUB + NA_ROWS - 1
VMEM_LIMIT = 52 * 1024 * 1024
ROW_SEGS = D_MODEL // LANES // 2

PAIR_LO = (0, 0, 0, 1, 1, 2)
PAIR_HI = (1, 2, 3, 2, 3, 3)


def _cparams(sem):
    return pltpu.CompilerParams(dimension_semantics=sem, vmem_limit_bytes=VMEM_LIMIT)


def _sigmoid(x):
    return 1.0 / (1.0 + jnp.exp(-x))


def _silu(x):
    return x * _sigmoid(x)


def _rms(x, g):
    return x * lax.rsqrt(jnp.mean(x * x, axis=-1, keepdims=True) + EPS) * g


def _dot(a, b):
    return jnp.dot(a, b, preferred_element_type=F32)


def _dot_nt(a, b):
    return lax.dot_general(a, b, (((1,), (1,)), ((), ())), preferred_element_type=F32)


def _dot_tn(a, b):
    return lax.dot_general(a, b, (((0,), (0,)), ((), ())), preferred_element_type=F32)


def _mod_kernel(c_ref, w_ref, b_ref, o_ref):
    a = _silu(c_ref[...]).astype(BF16)
    o_ref[0] = _dot(a, w_ref[0].astype(BF16)) + b_ref[0]


def _mod_vectors(cc, w_mod, b_mod):
    depth, d, n6 = w_mod.shape
    rows = cc.shape[0]
    tn = 1536
    return pl.pallas_call(
        _mod_kernel,
        out_shape=jax.ShapeDtypeStruct((depth, rows, n6), F32),
        grid=(depth, n6 // tn),
        in_specs=[
            pl.BlockSpec((rows, d), lambda i, j: (0, 0)),
            pl.BlockSpec((1, d, tn), lambda i, j: (i, 0, j)),
            pl.BlockSpec((1, 1, tn), lambda i, j: (i, 0, j)),
        ],
        out_specs=pl.BlockSpec((1, rows, tn), lambda i, j: (i, 0, j)),
        compiler_params=_cparams(("arbitrary", "arbitrary")),
        name="mod_vectors",
    )(cc, w_mod, b_mod.reshape(depth, 1, n6))


def _token_rows(ref):
    rows = ref.shape[0] // ROW_SEGS
    packed = jnp.concatenate([ref[pl.ds(sg, rows, stride=ROW_SEGS), :] for sg in range(ROW_SEGS)], axis=-1)
    low = lax.bitcast_convert_type(lax.shift_left(packed, jnp.int32(16)), F32)
    high = lax.bitcast_convert_type(packed & jnp.int32(-65536), F32)
    return jnp.concatenate([low, high], axis=-1)


def _store_token_rows(ref, val):
    rows, d = val.shape
    rounded = val.astype(BF16).astype(F32)
    bits = lax.bitcast_convert_type(rounded, jnp.int32)
    packed = lax.shift_right_logical(bits[:, :d // 2], jnp.int32(16)) | bits[:, d // 2:]
    for sg in range(ROW_SEGS):
        ref[pl.ds(sg, rows, stride=ROW_SEGS), :] = packed[:, sg * LANES:(sg + 1) * LANES]


def _proj_kernel(*refs, has_y, first, n_rope, n_bf16, col_chunk):
    it = iter(refs)
    s_ref = next(it)
    x_ref = next(it) if first else None
    y_ref = next(it) if has_y else None
    modp_ref = next(it) if has_y else None
    mod_ref = next(it)
    ng_ref = next(it)
    w_ref = next(it)
    cos_ref = next(it) if n_rope else None
    sin_ref = next(it) if n_rope else None
    snew_ref = next(it) if has_y or first else None
    p_ref = next(it)
    lr_ref = next(it) if w_ref.shape[1] > n_bf16 else None

    x = s_ref[0]
    if first:
        x = jnp.where(pl.program_id(1) == 0, x, x_ref[0])
        snew_ref[0] = x
    if has_y:
        x = x + modp_ref[0, 0, 5:6, :] * _token_rows(y_ref)
        snew_ref[0] = x
    h = _rms(x, ng_ref[...])
    h = h * (1.0 + mod_ref[0, 0, 1:2, :]) + mod_ref[0, 0, 0:1, :]
    hb = h.astype(BF16)
    n_out = w_ref.shape[1]
    for c0 in range(0, n_out, col_chunk):
        c1 = min(c0 + col_chunk, n_out)
        acc = _dot(hb, w_ref[:, c0:c1])
        for b0 in range(c0, c1, LANES):
            t = acc[:, b0 - c0:b0 - c0 + LANES]
            if b0 < n_rope:
                t = t * cos_ref[...] + pltpu.roll(t, LANES // 2, 1) * sin_ref[...]
            if b0 < n_bf16:
                p_ref[0, :, b0:b0 + LANES] = t.astype(BF16)
            else:
                lr_ref[0, :, b0 - n_bf16:b0 - n_bf16 + LANES] = t


def _project(s, y, modp, mod, ng, w, cos, sin, *, n_rope, n_bf16):
    first = isinstance(s, tuple)
    tm = ROW_TILE
    row_spec = pl.BlockSpec((1, tm, s[0].shape[2] if first else s.shape[2]), lambda i, j: (i, j, 0))
    if first:
        ctx, x = s
        b, n, d = x.shape[0], ctx.shape[1] + x.shape[1], x.shape[2]
        in_specs = [pl.BlockSpec((1, tm, d), lambda i, j: (i, 0, 0)),
                    pl.BlockSpec((1, tm, d), lambda i, j: (i, jnp.maximum(j - 1, 0), 0))]
        args = [ctx, x]
    else:
        b, n, d = s.shape
        in_specs = [row_spec]
        args = [s]
    n_out = w.shape[1]
    has_y = y is not None
    mod_spec = pl.BlockSpec((1, 1, 6, d), lambda i, j: (i, jnp.minimum(j, 1), 0, 0))
    if has_y:
        in_specs += [pl.BlockSpec((tm * ROW_SEGS, LANES), lambda i, j: (i * (n // tm) + j, 0)), mod_spec]
        args += [y, modp]
    in_specs += [mod_spec, pl.BlockSpec((1, d), lambda i, j: (0, 0)),
                 pl.BlockSpec((d, n_out), lambda i, j: (0, 0))]
    args += [mod, ng.reshape(1, d), w]
    if n_rope:
        tab_spec = pl.BlockSpec((tm, LANES), lambda i, j: (j, 0))
        in_specs += [tab_spec, tab_spec]
        args += [cos, sin]
    out_shape, out_specs = [], []
    if has_y or first:
        out_shape.append(jax.ShapeDtypeStruct((b, n, d), F32))
        out_specs.append(row_spec)
    out_shape.append(jax.ShapeDtypeStruct((b, n, n_bf16), BF16))
    out_specs.append(pl.BlockSpec((1, tm, n_bf16), lambda i, j: (i, j, 0)))
    if n_out > n_bf16:
        out_shape.append(jax.ShapeDtypeStruct((b, n, n_out - n_bf16), F32))
        out_specs.append(pl.BlockSpec((1, tm, n_out - n_bf16), lambda i, j: (i, j, 0)))
    return pl.pallas_call(
        functools.partial(_proj_kernel, has_y=has_y, first=first, n_rope=n_rope, n_bf16=n_bf16, col_chunk=512),
        out_shape=out_shape,
        grid=(b, n // tm),
        in_specs=in_specs,
        out_specs=out_specs,
        compiler_params=_cparams(("arbitrary", "arbitrary")),
        name="norm_mod_project",
    )(*args)


def _softmax_rows(s):
    e = jnp.exp(s - jnp.max(s, axis=-1, keepdims=True))
    return e * (1.0 / jnp.sum(e, axis=-1, keepdims=True))


def _diff_attn_kernel(*refs, lam_init, n_q):
    q_refs = refs[:n_q]
    k_ref, v_ref, lam_ref, g_ref, o_ref = refs[n_q:]
    lp = lam_ref[...]
    lam = (jnp.exp(jnp.sum(lp[0:1] * lp[1:2], axis=-1, keepdims=True))
           - jnp.exp(jnp.sum(lp[2:3] * lp[3:4], axis=-1, keepdims=True)) + lam_init)
    q = jnp.concatenate([r[0] for r in q_refs], axis=0)
    lane = lax.broadcasted_iota(jnp.int32, (1, LANES), 1)
    first_map = (lane // 32) % 2 == 0
    zero = jnp.zeros_like(q)
    q0 = jnp.where(first_map, q, zero)
    q1 = jnp.where(first_map, zero, q)
    n_keys = k_ref.shape[1]
    blk = min(ATT_KEY_BLOCK, n_keys)
    ones = jnp.ones((blk, A_DV), BF16)
    state = [None, None]
    for kb in range(n_keys // blk):
        k = k_ref[0, kb * blk:(kb + 1) * blk, :]
        v1 = jnp.concatenate([v_ref[0, kb * blk:(kb + 1) * blk, :], ones], axis=1)
        for i, qm in enumerate((q0, q1)):
            s = _dot_nt(qm, k)
            m_new = jnp.max(s, axis=-1, keepdims=True)
            if kb:
                m_old, acc_old = state[i]
                m_new = jnp.maximum(m_old, m_new)
            acc = _dot(jnp.exp2(s - m_new).astype(BF16), v1)
            if kb:
                acc = acc_old * jnp.exp2(m_old - m_new) + acc
            state[i] = (m_new, acc)
    acc0, acc1 = state[0][1], state[1][1]
    o = acc0[:, :A_DV] * (1.0 / acc0[:, A_DV:]) - acc1[:, :A_DV] * (lam / acc1[:, A_DV:])
    o_ref[0] = (_rms(o, g_ref[...]) * (1.0 - lam_init)).astype(BF16)


def _diff_attention(p, lam_p, subln_g, lam_init):
    b, n, _ = p.shape
    half = ROW_TILE
    tq = ATT_Q_HALVES * half
    ctx_blocks = N_CTX // half
    width = A_HEADS * A_DV
    small = [pl.BlockSpec((4, A_DH), lambda i, h, *_: (0, 0)), pl.BlockSpec((1, A_DV), lambda i, h, *_: (0, 0))]
    q_specs = [pl.BlockSpec((1, half, LANES), functools.partial(lambda i, h, j, part: (i, ctx_blocks + ATT_Q_HALVES * j + part, h),
                                                                 part=part)) for part in range(ATT_Q_HALVES)]
    oa_x = pl.pallas_call(
        functools.partial(_diff_attn_kernel, lam_init=lam_init, n_q=ATT_Q_HALVES),
        out_shape=jax.ShapeDtypeStruct((b, n - N_CTX, width), BF16),
        grid=(b, A_HEADS, (n - N_CTX) // tq),
        in_specs=q_specs + [
            pl.BlockSpec((1, n, LANES), lambda i, h, j: (i, 0, A_HEADS + h)),
            pl.BlockSpec((1, n, LANES), lambda i, h, j: (i, 0, 2 * A_HEADS + h)),
        ] + small,
        out_specs=pl.BlockSpec((1, tq, LANES), lambda i, h, j: (i, j, h)),
        compiler_params=_cparams(("arbitrary", "arbitrary", "arbitrary")),
        name="diff_attention",
    )(*([p] * ATT_Q_HALVES), p, p, lam_p, subln_g.reshape(1, A_DV))
    def all_heads(*refs):
        q_ref, k_ref, v_ref, lam_ref, g_ref, o_ref = refs
        for h in range(A_HEADS):
            cols = pl.ds(h * LANES, LANES)
            _diff_attn_kernel(q_ref.at[:, :, cols], k_ref.at[:, :, cols], v_ref.at[:, :, cols], lam_ref, g_ref,
                              o_ref.at[:, :, cols], lam_init=lam_init, n_q=1)

    oa_c = pl.pallas_call(
        all_heads,
        out_shape=jax.ShapeDtypeStruct((b, N_CTX, width), BF16),
        grid=(b,),
        in_specs=[pl.BlockSpec((1, N_CTX, width), functools.partial(lambda i, part: (i, 0, part), part=part))
                  for part in range(3)] + [pl.BlockSpec((4, A_DH), lambda i: (0, 0)),
                                           pl.BlockSpec((1, A_DV), lambda i: (0, 0))],
        out_specs=pl.BlockSpec((1, N_CTX, width), lambda i: (i, 0, 0)),
        compiler_params=_cparams(("arbitrary",)),
        name="context_diff_attention",
    )(p, p, p, lam_p, subln_g.reshape(1, A_DV))
    return oa_x, oa_c


def _split_heads(q):
    lane = lax.broadcasted_iota(jnp.int32, (1, LANES), 1)
    first_head = lane < B_DH
    zero = jnp.zeros_like(q)
    return first_head, (jnp.where(first_head, q, zero), jnp.where(first_head, zero, q))


def _na_ctx_kernel(q_ref, k_ref, v_ref, o_ref):
    for pair in range(q_ref.shape[2] // LANES):
        cols = slice(pair * LANES, (pair + 1) * LANES)
        first_head, qh = _split_heads(q_ref[0, :, cols])
        kc = k_ref[0, :, cols]
        vc = v_ref[0, :, cols]
        outs = [_dot(_softmax_rows(_dot_nt(qh[h], kc)).astype(BF16), vc) for h in range(2)]
        o_ref[0, :, cols] = jnp.where(first_head, outs[0], outs[1]).astype(BF16)


def _na_kernel(*refs):
    q_refs = refs[:-4]
    k_ref, v_ref, bm_ref, o_ref = refs[-4:]
    blk = pl.program_id(2)
    rows = (k_ref.shape[1] - N_CTX) // GRID_W
    sub_q = NA_SUB * GRID_W
    sub_k = NA_KROWS * GRID_W
    kc = k_ref[0, 0:N_CTX, :]
    vc = jnp.concatenate([v_ref[0, 0:N_CTX, :], jnp.ones((N_CTX, LANES), BF16)], axis=1)
    ones = jnp.ones((sub_k, LANES), BF16)
    for sub in range(NA_QROWS // NA_SUB):
        q_ref = q_refs[sub * sub_q // ROW_TILE]
        q0 = (sub * sub_q) % ROW_TILE
        first_head, qh = _split_heads(q_ref[0, q0:q0 + sub_q, :])
        k_row0 = jnp.clip(blk * NA_QROWS + sub * NA_SUB - NA_ROWS // 2, 0, rows - NA_KROWS)
        start = pl.multiple_of(N_CTX + k_row0 * GRID_W, GRID_W)
        kw = k_ref[0, pl.ds(start, sub_k), :]
        vw = jnp.concatenate([v_ref[0, pl.ds(start, sub_k), :], ones], axis=1)
        outs = []
        for h in range(2):
            s_loc = _dot_nt(qh[h], kw) + bm_ref[h, 0, sub]
            s_ctx = _dot_nt(qh[h], kc)
            m = jnp.maximum(jnp.max(s_loc, axis=-1, keepdims=True), jnp.max(s_ctx, axis=-1, keepdims=True))
            acc = _dot(jnp.exp(s_loc - m).astype(BF16), vw) + _dot(jnp.exp(s_ctx - m).astype(BF16), vc)
            outs.append(acc[:, :LANES] * (1.0 / acc[:, LANES:]))
        o_ref[0, sub * sub_q:(sub + 1) * sub_q, :] = jnp.where(first_head, outs[0], outs[1]).astype(BF16)


def _na_bias_table(na_bias, rows):
    h = na_bias.shape[0]
    n_dr, n_dc = 2 * NA_ROWS - 1, 2 * NA_COLS - 1
    width = 2 * GRID_W
    left = GRID_W - NA_COLS
    u = jnp.pad(na_bias, ((0, 0), (0, 0), (left, width - left - n_dc)))
    skew = jnp.tile(u, (1, 1, GRID_W))[:, :, :GRID_W * (width - 1)].reshape(h, n_dr, GRID_W, width - 1)
    toeplitz = skew[:, :, :, GRID_W - 1:]
    margin = NA_KROWS - NA_ROWS
    by_col = jnp.pad(toeplitz.transpose(0, 2, 1, 3), ((0, 0), (0, 0), (margin, margin), (0, 0)))
    by_col = by_col.reshape(h, GRID_W, (n_dr + 2 * margin) * GRID_W)
    col = np.arange(GRID_W)
    col_start = np.clip(col - NA_COLS // 2, 0, GRID_W - NA_COLS)
    col_ok = (col[None, :] >= col_start[:, None]) & (col[None, :] < col_start[:, None] + NA_COLS)
    n_sub = NA_QROWS // NA_SUB
    runs = []
    for q_row0 in (0, NA_QROWS, rows - NA_QROWS):
        for sub in range(n_sub):
            k_row0 = int(np.clip(q_row0 + sub * NA_SUB - NA_ROWS // 2, 0, rows - NA_KROWS))
            kr = k_row0 + np.arange(NA_KROWS)
            firsts, oks = [], []
            for rq in range(NA_SUB):
                r = q_row0 + sub * NA_SUB + rq
                r0 = int(np.clip(r - NA_ROWS // 2, 0, rows - NA_ROWS))
                row_ok = (kr >= r0) & (kr < r0 + NA_ROWS)
                oks.append((col_ok[:, None, :] & row_ok[None, :, None]).reshape(GRID_W, NA_KROWS * GRID_W))
                firsts.append(k_row0 - r + NA_ROWS - 1 + margin)
            key = (tuple(firsts), np.stack(oks).tobytes())
            if runs and runs[-1][0] == key:
                runs[-1][2] += 1
                continue
            windows = [by_col[:, :, f * GRID_W:(f + NA_KROWS) * GRID_W] for f in firsts]
            block = jnp.where(np.concatenate(oks, axis=0)[None], jnp.concatenate(windows, axis=1), -jnp.inf)
            runs.append([key, block, 1])
    table = jnp.concatenate([jnp.broadcast_to(blk[:, None], (h, rep) + blk.shape[1:]) for _, blk, rep in runs], axis=1)
    return table.reshape(h, 3, n_sub, NA_SUB * GRID_W, NA_KROWS * GRID_W)


def _neighbourhood_attention(p, bm):
    b, n, _ = p.shape
    tq = NA_QROWS * GRID_W
    parts = tq // ROW_TILE
    n_blocks = (n - N_CTX) // tq
    ctx_blocks = N_CTX // ROW_TILE
    col0 = 3 * A_HEADS
    pairs = B_HEADS // 2
    width = B_HEADS * B_DH

    def pattern(j):
        return jnp.minimum(j, 1) + (j == n_blocks - 1).astype(jnp.int32)

    ob_x = pl.pallas_call(
        _na_kernel,
        out_shape=jax.ShapeDtypeStruct((b, n - N_CTX, width), BF16),
        grid=(b, pairs, n_blocks),
        in_specs=[pl.BlockSpec((1, ROW_TILE, LANES),
                               functools.partial(lambda i, h, j, part: (i, ctx_blocks + parts * j + part, col0 + h),
                                                 part=part)) for part in range(parts)] + [
            pl.BlockSpec((1, n, LANES), lambda i, h, j: (i, 0, col0 + pairs + h)),
            pl.BlockSpec((1, n, LANES), lambda i, h, j: (i, 0, col0 + 2 * pairs + h)),
            pl.BlockSpec((2, 1, NA_QROWS // NA_SUB, NA_SUB * GRID_W, NA_KROWS * GRID_W),
                         lambda i, h, j: (h, pattern(j), 0, 0, 0)),
        ],
        out_specs=pl.BlockSpec((1, tq, LANES), lambda i, h, j: (i, j, h)),
        compiler_params=_cparams(("arbitrary", "arbitrary", "arbitrary")),
        name="neighbourhood_attention",
    )(*([p] * parts), p, p, bm)
    ob_c = pl.pallas_call(
        _na_ctx_kernel,
        out_shape=jax.ShapeDtypeStruct((b, N_CTX, width), BF16),
        grid=(b,),
        in_specs=[pl.BlockSpec((1, N_CTX, width), functools.partial(lambda i, part: (i, 0, col0 // pairs + part), part=part))
                  for part in range(3)],
        out_specs=pl.BlockSpec((1, N_CTX, width), lambda i: (i, 0, 0)),
        compiler_params=_cparams(("arbitrary",)),
        name="context_attention",
    )(p, p, p)
    return ob_x, ob_c


def _gla_kernel(q_ref, k_ref, v_ref, lr_ref, wg_ref, bg_ref, o_ref, ob_ref, g_ref, sf_ref, sb_ref):
    n = q_ref.shape[1]
    blk = GLA_BLOCK
    per_blk = blk // C_CHUNK
    n_blocks = n // blk
    ri = lax.broadcasted_iota(jnp.int32, (blk, blk), 0)
    ci = lax.broadcasted_iota(jnp.int32, (blk, blk), 1)
    same_chunk = (ri // C_CHUNK) == (ci // C_CHUNK)
    keeps = (same_chunk & (ci <= ri), same_chunk & (ci >= ri))
    tris = tuple(jnp.where(kp, 1.0, 0.0).astype(BF16) for kp in keeps)
    w_gate = jnp.concatenate([wg_ref[0], wg_ref[1]], axis=1).astype(BF16)
    b_gate = jnp.concatenate([bg_ref[0:1, :], bg_ref[1:2, :]], axis=1)
    for i in range(n_blocks):
        z = _dot(lr_ref[0, i * blk:(i + 1) * blk, :].astype(BF16), w_gate) + b_gate
        g_ref[i * blk:(i + 1) * blk, :] = (jnp.minimum(z, 0.0) - jnp.log(1.0 + jnp.exp(-jnp.abs(z)))) * (1.0 / C_TAU)

    def block(sb, direction, st_ref):
        keep = keeps[direction]
        end_row = C_CHUNK - 1 if direction == 0 else 0
        mid_row = C_CHUNK // 2 - 1 if direction == 0 else C_CHUNK // 2
        r0 = sb * blk
        g = g_ref[r0:r0 + blk, direction * C_DK:(direction + 1) * C_DK]
        g_hi = g.astype(BF16)
        g_lo = (g - g_hi.astype(F32)).astype(BF16)
        gc2 = _dot(tris[direction], jnp.concatenate([g_hi, g_lo], axis=1))
        gc = (gc2[:, :C_DK] + gc2[:, C_DK:]).reshape(per_blk, C_CHUNK, C_DK)
        g_end = gc[:, end_row:end_row + 1, :]
        g_mid = gc[:, mid_row:mid_row + 1, :]
        q = (q_ref[0, r0:r0 + blk, :].astype(F32) * (C_DK ** -0.5)).reshape(per_blk, C_CHUNK, C_DK)
        k = k_ref[0, r0:r0 + blk, :].astype(F32).reshape(per_blk, C_CHUNK, C_DK)
        v = v_ref[0, r0:r0 + blk, :]
        q_in = (q * jnp.exp(gc)).astype(BF16).reshape(blk, C_DK)
        q_mid = (q * jnp.exp(gc - g_mid)).astype(BF16).reshape(blk, C_DK)
        k_mid = (k * jnp.exp(g_mid - gc)).astype(BF16).reshape(blk, C_DK)
        k_end = (k * jnp.exp(g_end - gc)).astype(BF16).reshape(blk, C_DK)
        a = jnp.where(keep, _dot_nt(q_mid, k_mid), 0.0)
        o_intra = _dot(a.astype(BF16), v)
        ends = jnp.concatenate([g_end.reshape(per_blk, C_DK), jnp.zeros((8 - per_blk, C_DK), F32)], axis=0)
        decay = jnp.transpose(jnp.exp(ends))
        o_inter = [None] * per_blk
        for c in (range(per_blk) if direction == 0 else reversed(range(per_blk))):
            rows = slice(c * C_CHUNK, (c + 1) * C_CHUNK)
            st = st_ref[...]
            o_inter[c] = _dot(q_in[rows], st.astype(BF16))
            st_ref[...] = st * decay[:, c:c + 1] + _dot_tn(k_end[rows], v[rows])
        return o_intra + jnp.concatenate(o_inter, axis=0)

    sf_ref[...] = jnp.zeros_like(sf_ref)
    sb_ref[...] = jnp.zeros_like(sb_ref)
    ctx_blocks = N_CTX // blk
    order_b = list(reversed(range(ctx_blocks))) + list(reversed(range(ctx_blocks, n_blocks)))
    for i in range(n_blocks):
        o_ref[0, i * blk:(i + 1) * blk, :] = block(i, 0, sf_ref)
        sb = order_b[i]
        ob_ref[sb * blk:(sb + 1) * blk, :] = block(sb, 1, sb_ref)
    o_ref[0] = o_ref[0] + ob_ref[...]


def _gla(p, lr, wg, bg):
    b, n, _ = p.shape
    return pl.pallas_call(
        _gla_kernel,
        out_shape=jax.ShapeDtypeStruct((b, n, C_HEADS * C_DV), F32),
        grid=(b, C_HEADS),
        in_specs=[
            pl.BlockSpec((1, n, C_DK), lambda i, h: (i, 0, h)),
            pl.BlockSpec((1, n, C_DK), lambda i, h: (i, 0, C_HEADS + h)),
            pl.BlockSpec((1, n, C_DV), lambda i, h: (i, 0, C_HEADS + h)),
            pl.BlockSpec((1, n, LANES), lambda i, h: (i, 0, 0)),
            pl.BlockSpec((2, LANES, C_DK), lambda i, h: (0, 0, h)),
            pl.BlockSpec((2, C_DK), lambda i, h: (0, h)),
        ],
        out_specs=pl.BlockSpec((1, n, C_DV), lambda i, h: (i, 0, h)),
        scratch_shapes=[pltpu.VMEM((n, C_DV), F32), pltpu.VMEM((n, 2 * C_DK), F32),
                        pltpu.VMEM((C_DK, C_DV), F32), pltpu.VMEM((C_DK, C_DV), F32)],
        compiler_params=_cparams(("arbitrary", "arbitrary")),
        name="gla",
    )(p, p, p, lr, wg, bg)


def _out_kernel(*refs, gla):
    if gla:
        o_ref, gate_ref, gn_ref, w_ref, s_ref, mod_ref, ng_ref, rw_ref, snew_ref, h2_ref, lg_ref = refs
        o = o_ref[0]
        gate = gate_ref[0].astype(F32)
        parts = []
        for hd in range(C_HEADS):
            oh = _rms(o[:, hd * C_DV:(hd + 1) * C_DV], gn_ref[...])
            parts.append((oh * _silu(gate[:, hd * C_DV:(hd + 1) * C_DV])).astype(BF16))
        acc = _dot(jnp.concatenate(parts, axis=-1), w_ref[...])
    else:
        oax_ref, oac_ref, obx_ref, obc_ref, w_ref, s_ref, mod_ref, ng_ref, rw_ref, snew_ref, h2_ref, lg_ref = refs
        half = oax_ref.shape[2]
        is_ctx = pl.program_id(1) == 0
        oa = jnp.where(is_ctx, oac_ref[0], oax_ref[0])
        ob = jnp.where(is_ctx, obc_ref[0], obx_ref[0])
        acc = _dot(oa, w_ref[0:half, :]) + _dot(ob, w_ref[half:, :])
    x = s_ref[0] + mod_ref[0, 0, 2:3, :] * acc
    snew_ref[0] = x
    h2 = _rms(x, ng_ref[...]) * (1.0 + mod_ref[0, 0, 4:5, :]) + mod_ref[0, 0, 3:4, :]
    _store_token_rows(h2_ref, h2)
    t = _dot(h2.astype(BF16), rw_ref[...])
    lg_ref[0] = t + pltpu.roll(t, LANES - N_EXPERTS, 1)


def _out_project(mix, w_out, s, mod, ng, rw, *, gla, gn=None):
    b, n, d = s.shape
    tm = ROW_TILE
    row_spec = pl.BlockSpec((1, tm, d), lambda i, j: (i, j, 0))
    const2 = lambda i, j: (0, 0)
    if gla:
        o, p = mix
        in_specs = [row_spec, pl.BlockSpec((1, tm, d), lambda i, j: (i, j, 2)),
                    pl.BlockSpec((1, C_DV), const2)]
        args = [o, p, gn.reshape(1, C_DV)]
    else:
        oa_x, oa_c, ob_x, ob_c = mix
        half = oa_x.shape[2]
        latent = pl.BlockSpec((1, tm, half), lambda i, j: (i, jnp.maximum(j - 1, 0), 0))
        context = pl.BlockSpec((1, tm, half), lambda i, j: (i, 0, 0))
        in_specs = [latent, context, latent, context]
        args = [oa_x, oa_c, ob_x, ob_c]
    in_specs += [pl.BlockSpec((d, d), const2), row_spec,
                 pl.BlockSpec((1, 1, 6, d), lambda i, j: (i, jnp.minimum(j, 1), 0, 0)),
                 pl.BlockSpec((1, d), const2), pl.BlockSpec((d, LANES), const2)]
    args += [w_out, s, mod, ng.reshape(1, d), rw]
    return pl.pallas_call(
        functools.partial(_out_kernel, gla=gla),
        out_shape=[jax.ShapeDtypeStruct((b, n, d), F32), jax.ShapeDtypeStruct((b * n * ROW_SEGS, LANES), jnp.int32),
                   jax.ShapeDtypeStruct((b, n, LANES), F32)],
        grid=(b, n // tm),
        in_specs=in_specs,
        out_specs=[row_spec, pl.BlockSpec((tm * ROW_SEGS, LANES), lambda i, j: (i * (n // tm) + j, 0)),
                   pl.BlockSpec((1, tm, LANES), lambda i, j: (i, j, 0))],
        compiler_params=_cparams(("arbitrary", "arbitrary")),
        name="out_project",
    )(*args)


def _route_kernel(bias_ref, lg_ref, cls_ref, wlo_ref, whi_ref, rank_ref, cnt_ref):
    score = [_sigmoid(lg_ref[e]) for e in range(N_EXPERTS)]
    sel = [score[e] + bias_ref[e] for e in range(N_EXPERTS)]
    grp_score = []
    for g in range(N_GROUPS):
        v = sel[g * PER_GROUP:(g + 1) * PER_GROUP]
        best = v[0] + v[1]
        for a in range(PER_GROUP):
            for c in range(a + 1, PER_GROUP):
                if (a, c) != (0, 1):
                    best = jnp.maximum(best, v[a] + v[c])
        grp_score.append(best)
    grp = jnp.zeros(grp_score[0].shape, jnp.int32)
    best = grp_score[0]
    for g in range(1, N_GROUPS):
        upd = grp_score[g] > best
        best = jnp.where(upd, grp_score[g], best)
        grp = jnp.where(upd, g, grp)

    def pick(vals, j):
        out = vals[j]
        for g in range(1, N_GROUPS):
            out = jnp.where(grp == g, vals[g * PER_GROUP + j], out)
        return out

    v = [pick(sel, j) for j in range(PER_GROUP)]
    sc = [pick(score, j) for j in range(PER_GROUP)]
    one = jnp.ones(grp.shape, jnp.int32)
    zero = jnp.zeros(grp.shape, jnp.int32)
    chosen = []
    for j in range(PER_GROUP):
        rank = zero
        for m in range(PER_GROUP):
            if m == j:
                continue
            ahead = (v[m] >= v[j]) if m < j else (v[m] > v[j])
            rank = rank + jnp.where(ahead, one, zero)
        chosen.append(rank < 2)
    code = zero
    for j in range(PER_GROUP):
        code = code + jnp.where(chosen[j], one * (1 << j), zero)
    pair = zero
    for idx in range(6):
        pair = jnp.where(code == (1 << PAIR_LO[idx]) + (1 << PAIR_HI[idx]), idx, pair)
    s_lo = jnp.where(chosen[0], sc[0], jnp.where(chosen[1], sc[1], sc[2]))
    s_hi = jnp.where(chosen[3], sc[3], jnp.where(chosen[2], sc[2], sc[1]))
    den = s_lo + s_hi
    cls = grp * 6 + pair
    cls_ref[...] = cls
    wlo_ref[...] = s_lo / den
    whi_ref[...] = s_hi / den
    r = cls.shape[0]
    lane = lax.broadcasted_iota(jnp.int32, (1, LANES), 1)
    li = lax.broadcasted_iota(jnp.int32, (LANES, LANES), 0)
    lj = lax.broadcasted_iota(jnp.int32, (LANES, LANES), 1)
    upto_lane = jnp.where(li <= lj, 1.0, 0.0).astype(BF16)
    ri = lax.broadcasted_iota(jnp.int32, (r, r), 0)
    rj = lax.broadcasted_iota(jnp.int32, (r, r), 1)
    earlier_rows = jnp.where(rj < ri, 1.0, 0.0).astype(BF16)
    member = [jnp.where(cls == c, 1.0, 0.0) for c in range(N_CLASSES)]
    row_totals = jnp.zeros((r, LANES), F32)
    for c in range(N_CLASSES):
        row_totals = row_totals + jnp.where(lane == c, jnp.sum(member[c], axis=1, keepdims=True), 0.0)
    before_row = _dot(earlier_rows, row_totals.astype(BF16))
    rank = jnp.zeros((r, LANES), F32)
    for c in range(N_CLASSES):
        in_row = _dot(member[c].astype(BF16), upto_lane)
        offset = jnp.sum(jnp.where(lane == c, before_row, 0.0), axis=1, keepdims=True)
        rank = rank + member[c] * (in_row + offset - 1.0)
    rank_ref[...] = rank.astype(jnp.int32)
    counts = before_row[r - 1:r, :] + row_totals[r - 1:r, :]
    cnt_ref[...] = jnp.broadcast_to(counts, cnt_ref.shape).astype(jnp.int32)


def _route(logits_t, router_bias):
    _, r, _ = logits_t.shape
    full = pl.BlockSpec((r, LANES), lambda i: (0, 0))
    return pl.pallas_call(
        _route_kernel,
        out_shape=[jax.ShapeDtypeStruct((r, LANES), jnp.int32), jax.ShapeDtypeStruct((r, LANES), F32),
                   jax.ShapeDtypeStruct((r, LANES), F32), jax.ShapeDtypeStruct((r, LANES), jnp.int32),
                   jax.ShapeDtypeStruct((8, LANES), jnp.int32)],
        grid=(1,),
        in_specs=[pl.BlockSpec(memory_space=pltpu.SMEM),
                  pl.BlockSpec((N_EXPERTS, r, LANES), lambda i: (0, 0, 0))],
        out_specs=[full, full, full, full, pl.BlockSpec((8, LANES), lambda i: (0, 0))],
        compiler_params=_cparams(("arbitrary",)),
        name="route",
    )(router_bias, logits_t)


def _gather_rows(table, idx):
    p = idx.shape[0]
    per_worker = p // SC_WORKERS
    n_chunks = per_worker // SC_GATHER_ROWS
    mesh = plsc.VectorSubcoreMesh(core_axis_name="c", subcore_axis_name="s", num_cores=SC_CORES,
                                  num_subcores=SC_SUBCORES)

    @functools.partial(
        pl.kernel, mesh=mesh,
        out_type=jax.ShapeDtypeStruct((p,) + table.shape[1:], table.dtype),
        scratch_types=[pltpu.VMEM((per_worker,), jnp.int32),
                       pltpu.VMEM((SC_GATHER_ROWS,) + table.shape[1:], table.dtype),
                       pltpu.SemaphoreType.DMA],
        compiler_params=pltpu.CompilerParams(use_tc_tiling_on_sc=True),
        name="gather_rows",
    )
    def gather(table_hbm, idx_hbm, out_hbm, idx_v, rows_v, sem):
        worker = lax.axis_index("s") * SC_CORES + lax.axis_index("c")
        base = worker * per_worker
        pltpu.sync_copy(idx_hbm.at[pl.ds(base, per_worker)], idx_v)

        @pl.loop(0, n_chunks)
        def _(i):
            off = pl.multiple_of(i * SC_GATHER_ROWS, SC_GATHER_ROWS)
            pltpu.async_copy(table_hbm.at[idx_v.at[pl.ds(off, SC_GATHER_ROWS)]], rows_v, sem).wait()
            pltpu.sync_copy(rows_v, out_hbm.at[pl.ds(base + off, SC_GATHER_ROWS)])

    return gather(table, idx)


def _scatter_rows(src, idx, n_out):
    t = idx.shape[0]
    per_worker = t // SC_WORKERS
    n_chunks = per_worker // SC_GATHER_ROWS
    mesh = plsc.VectorSubcoreMesh(core_axis_name="c", subcore_axis_name="s", num_cores=SC_CORES,
                                  num_subcores=SC_SUBCORES)

    @functools.partial(
        pl.kernel, mesh=mesh,
        out_type=jax.ShapeDtypeStruct((n_out,) + src.shape[1:], src.dtype),
        scratch_types=[pltpu.VMEM((n_chunks, SC_GATHER_ROWS), jnp.int32),
                       pltpu.VMEM((SC_GATHER_ROWS,) + src.shape[1:], src.dtype),
                       pltpu.SemaphoreType.DMA],
        compiler_params=pltpu.CompilerParams(use_tc_tiling_on_sc=True),
        name="scatter_rows",
    )
    def scatter(src_hbm, idx_hbm, out_hbm, idx_v, rows_v, sem):
        worker = lax.axis_index("s") * SC_CORES + lax.axis_index("c")
        base = worker * per_worker
        pltpu.sync_copy(idx_hbm.at[worker], idx_v)

        @pl.loop(0, n_chunks)
        def _(i):
            off = pl.multiple_of(i * SC_GATHER_ROWS, SC_GATHER_ROWS)
            pltpu.sync_copy(src_hbm.at[pl.ds(base + off, SC_GATHER_ROWS)], rows_v)
            pltpu.async_copy(rows_v, out_hbm.at[idx_v.at[i]], sem).wait()

    return scatter(src, idx.reshape(SC_WORKERS, n_chunks, SC_GATHER_ROWS))


def _cast_kernel(*refs):
    n = (len(refs) - 1) // 2
    for i in range(n):
        refs[n + 1 + i][0] = refs[i][0, 0].astype(BF16)


def _cast_expert_weights(layer, w1, w3, w2, after):
    ws = (w1, w3, w2)
    e = w1.shape[1]
    in_specs = [pl.BlockSpec((1, 1) + w.shape[2:], lambda j: (layer, j, 0, 0)) for w in ws]
    in_specs.append(pl.BlockSpec(memory_space=pl.ANY))
    return pl.pallas_call(
        _cast_kernel,
        out_shape=[jax.ShapeDtypeStruct(w.shape[1:], BF16) for w in ws],
        grid=(e,),
        in_specs=in_specs,
        out_specs=[pl.BlockSpec((1,) + w.shape[2:], lambda j: (j, 0, 0)) for w in ws],
        compiler_params=_cparams(("arbitrary",)),
        name="cast_expert_weights",
    )(*ws, after)


def _moe_kernel(lo_ref, hi_ref, nt_ref, x_ref, wr_ref, w1l_ref, w1h_ref, w3l_ref, w3h_ref, w2l_ref, w2h_ref, y_ref):
    j = pl.program_id(0)

    @pl.when(j < nt_ref[0])
    def _():
        x = _token_rows(x_ref).astype(BF16)
        wr = wr_ref[...]
        he_lo = (_silu(_dot(x, w1l_ref[0])) * _dot(x, w3l_ref[0]) * wr[:, 0:1]).astype(BF16)
        he_hi = (_silu(_dot(x, w1h_ref[0])) * _dot(x, w3h_ref[0]) * wr[:, 1:2]).astype(BF16)
        y = _dot(he_lo, w2l_ref[0]) + _dot(he_hi, w2h_ref[0])
        _store_token_rows(y_ref, y)

    @pl.when(j >= nt_ref[0])
    def _():
        y_ref[...] = jnp.zeros_like(y_ref)


def _moe_experts(xs, wrow, tile_lo, tile_hi, n_tiles, w1, w3, w2):
    tp = xs.shape[0] // ROW_SEGS
    tm = MOE_TILE
    d, de = w1.shape[1], w1.shape[2]

    def gate_map(j, lo, hi, nt):
        return (jnp.minimum(j, nt[0] - 1), 0)

    def lo_map(j, lo, hi, nt):
        return (lo[jnp.minimum(j, nt[0] - 1)], 0, 0)

    def hi_map(j, lo, hi, nt):
        return (hi[jnp.minimum(j, nt[0] - 1)], 0, 0)

    up = (1, d, de)
    down = (1, de, d)
    return pl.pallas_call(
        _moe_kernel,
        out_shape=jax.ShapeDtypeStruct((tp * ROW_SEGS, LANES), jnp.int32),
        grid_spec=pltpu.PrefetchScalarGridSpec(
            num_scalar_prefetch=3,
            grid=(tp // tm,),
            in_specs=[pl.BlockSpec((tm * ROW_SEGS, LANES), gate_map), pl.BlockSpec((tm, LANES), gate_map),
                      pl.BlockSpec(up, lo_map), pl.BlockSpec(up, hi_map),
                      pl.BlockSpec(up, lo_map), pl.BlockSpec(up, hi_map),
                      pl.BlockSpec(down, lo_map), pl.BlockSpec(down, hi_map)],
            out_specs=pl.BlockSpec((tm * ROW_SEGS, LANES), lambda j, lo, hi, nt: (j, 0)),
        ),
        compiler_params=_cparams(("arbitrary",)),
        name="moe_experts",
    )(tile_lo, tile_hi, n_tiles, xs, wrow, w1, w1, w3, w3, w2, w2)


def _moe(h2, logits, router_bias, w1, w3, w2):
    t = h2.shape[0]
    tm = MOE_TILE
    n_tiles_max = t // tm + N_CLASSES
    tp = n_tiles_max * tm
    lg_t = logits[:, :N_EXPERTS].T.reshape(N_EXPERTS, t // LANES, LANES)
    cls, wlo, whi, rank, counts = _route(lg_t, router_bias)
    cls, wlo, whi, rank = cls.reshape(t), wlo.reshape(t), whi.reshape(t), rank.reshape(t)
    counts = counts[0, :N_CLASSES]
    classes = jnp.arange(N_CLASSES, dtype=jnp.int32)
    onehot = (cls[:, None] == classes[None, :]).astype(jnp.int32)
    tiles_per = (counts + tm - 1) // tm
    tile_end = jnp.cumsum(tiles_per)
    tile_start = tile_end - tiles_per
    dest = jnp.sum(onehot * (tile_start * tm)[None, :], axis=1) + rank
    tiles = jnp.arange(n_tiles_max, dtype=jnp.int32)
    tile_cls = jnp.minimum(jnp.sum((tile_end[None, :] <= tiles[:, None]).astype(jnp.int32), axis=1), N_CLASSES - 1)
    pair = tile_cls % 6
    base = (tile_cls // 6) * PER_GROUP
    pair_onehot = (pair[:, None] == jnp.arange(6, dtype=jnp.int32)[None, :]).astype(jnp.int32)
    tile_lo = base + jnp.sum(pair_onehot * jnp.asarray(PAIR_LO, jnp.int32)[None, :], axis=1)
    tile_hi = base + jnp.sum(pair_onehot * jnp.asarray(PAIR_HI, jnp.int32)[None, :], axis=1)
    gates = jnp.pad(jnp.stack([wlo, whi], axis=1), ((0, 0), (0, LANES - 2))).reshape(t, 1, LANES)
    wrow = _scatter_rows(gates, dest, tp).reshape(tp, LANES)
    xs = _scatter_rows(h2, dest, tp).reshape(tp * ROW_SEGS, LANES)
    ys = _moe_experts(xs, wrow, tile_lo, tile_hi, tile_end[-1:], w1, w3, w2)
    return _gather_rows(ys.reshape(tp, ROW_SEGS, LANES), dest), dest


def _final_kernel(s_ref, y_ref, mod_ref, g_ref, o_ref):
    x = s_ref[0] + mod_ref[0, 0, 5:6, :] * _token_rows(y_ref)
    o_ref[0] = _rms(x, g_ref[...])


def _final(s, y, mod, final_g):
    b, n, d = s.shape
    tm = ROW_TILE
    skip = N_CTX // tm
    row_spec = pl.BlockSpec((1, tm, d), lambda i, j: (i, j + skip, 0))
    return pl.pallas_call(
        _final_kernel,
        out_shape=jax.ShapeDtypeStruct((b, n - N_CTX, d), F32),
        grid=(b, (n - N_CTX) // tm),
        in_specs=[row_spec, pl.BlockSpec((tm * ROW_SEGS, LANES), lambda i, j: (i * (n // tm) + j + skip, 0)),
                  pl.BlockSpec((1, 1, 6, d), lambda i, j: (i, 1, 0, 0)),
                  pl.BlockSpec((1, d), lambda i, j: (0, 0))],
        out_specs=pl.BlockSpec((1, tm, d), lambda i, j: (i, j, 0)),
        compiler_params=_cparams(("arbitrary", "arbitrary")),
        name="final_norm",
    )(s, y, mod, final_g.reshape(1, d))


def _rope_tables(seq):
    pos = jnp.arange(seq)
    row_pos, col_pos = pos // GRID_W, pos % GRID_W
    n = A_DH // 2
    inv = ROPE_BASE ** (-jnp.arange(0, n, 2, dtype=F32) / n)
    ang_r = row_pos.astype(F32)[:, None] * inv[None, :]
    ang_c = col_pos.astype(F32)[:, None] * inv[None, :]
    ang = jnp.concatenate([ang_r, ang_c], axis=-1)
    cos = jnp.tile(jnp.cos(ang), (1, 4))
    sin = jnp.tile(jnp.sin(ang), (1, 4))
    sin = jnp.concatenate([-sin[:, :LANES // 2], sin[:, LANES // 2:]], axis=-1)
    cos = jnp.concatenate([jnp.ones((N_CTX, LANES), F32), cos], axis=0)
    sin = jnp.concatenate([jnp.zeros((N_CTX, LANES), F32), sin], axis=0)
    return cos, sin


def _interleave_maps(w):
    d = w.shape[0]
    w = w.reshape(d, A_HEADS, 2, 2, 2, A_DH // 4)
    return w.transpose(0, 1, 4, 2, 3, 5).reshape(d, A_HEADS * LANES)


def _att_weights(w_in):
    a_qk = A_HEADS * 2 * A_DH
    a_v = A_HEADS * A_DV
    b_w = B_HEADS * B_DH
    qa = _interleave_maps(w_in[:, :a_qk]) * (A_DH ** -0.5 * LOG2_E)
    ka = _interleave_maps(w_in[:, a_qk:2 * a_qk])
    va = w_in[:, 2 * a_qk:2 * a_qk + a_v]
    o = 2 * a_qk + a_v
    qb = w_in[:, o:o + b_w] * (B_DH ** -0.5)
    rest = w_in[:, o + b_w:]
    return jnp.concatenate([qa, ka, va, qb, rest], axis=1).astype(BF16)


def _gla_weights(w_in):
    d = w_in.shape[0]
    n_main = 2 * C_HEADS * C_DK + 2 * C_HEADS * C_DV
    pad = jnp.zeros((d, LANES - 2 * C_RANK), w_in.dtype)
    return jnp.concatenate([w_in, pad], axis=1).astype(BF16), n_main


def kernel(x, c, ctx, c_ctx, w_mod, b_mod, norm_g, final_g, att_w_in, att_w_out, att_lambda, att_subln_g, na_bias,
           gla_w_in, gla_w_gate, gla_b_gate, gla_norm_g, gla_w_out, router_w, router_bias, moe_w1, moe_w3, moe_w2):
    b, seq, d = x.shape
    n = N_CTX + seq
    s = (ctx, x)

    rows = b + 1
    rows_pad = -(-rows // 8) * 8
    cc = jnp.concatenate([c, c_ctx[None, :], jnp.zeros((rows_pad - rows, d), F32)], axis=0)
    mod_all = _mod_vectors(cc, w_mod, b_mod)
    mod_x = mod_all[:, :b].reshape(DEPTH, b, 1, 6, d)
    mod_c = jnp.broadcast_to(mod_all[:, b].reshape(DEPTH, 1, 1, 6, d), (DEPTH, b, 1, 6, d))
    mods = jnp.concatenate([mod_c, mod_x], axis=2)

    cos, sin = _rope_tables(seq)
    rw_hi = router_w.astype(BF16)
    rw_lo = (router_w - rw_hi.astype(F32)).astype(BF16)
    zpad = jnp.zeros((d, LANES - 2 * N_EXPERTS), BF16)
    rw = jnp.concatenate([rw_hi, rw_lo, zpad], axis=1)

    y = None
    order_after = router_bias
    for i in range(DEPTH):
        j = i // 2
        modp = mods[i - 1] if i else None
        if i % 2 == 0:
            lam_init = 0.8 - 0.6 * math.exp(-0.3 * i)
            w = _att_weights(att_w_in[j])
            outs = _project(s, y, modp, mods[i], norm_g[i, 0], w, cos, sin,
                            n_rope=2 * A_HEADS * LANES, n_bf16=w.shape[1])
            s = outs[0]
            p = outs[-1]
            oa_x, oa_c = _diff_attention(p, att_lambda[j], att_subln_g[j], lam_init)
            ob_x, ob_c = _neighbourhood_attention(p, _na_bias_table(na_bias[j], seq // GRID_W))
            s, h2, logits = _out_project((oa_x, oa_c, ob_x, ob_c), att_w_out[j].astype(BF16), s, mods[i], norm_g[i, 1],
                                         rw, gla=False)
        else:
            w, n_main = _gla_weights(gla_w_in[j])
            outs = _project(s, y, modp, mods[i], norm_g[i, 0], w, None, None, n_rope=0, n_bf16=n_main)
            s, p, lr = outs
            wg = jnp.zeros((2, LANES, C_HEADS * C_DK), F32)
            wg = wg.at[0, :C_RANK].set(gla_w_gate[j, 0]).at[1, C_RANK:2 * C_RANK].set(gla_w_gate[j, 1])
            o = _gla(p, lr, wg, gla_b_gate[j])
            s, h2, logits = _out_project((o, p), gla_w_out[j].astype(BF16), s, mods[i], norm_g[i, 1],
                                         rw, gla=True, gn=gla_norm_g[j])
        if i == 0:
            expert_w = _cast_expert_weights(0, moe_w1, moe_w3, moe_w2, order_after)
        y, order_after = _moe(h2.reshape(b * n, ROW_SEGS, LANES), logits.reshape(b * n, LANES), router_bias, *expert_w)
        y = y.reshape(b * n * ROW_SEGS, LANES)
        if i + 1 < DEPTH:
            expert_w = _cast_expert_weights(i + 1, moe_w1, moe_w3, moe_w2, order_after)
    return _final(s, y, mods[DEPTH - 1], final_g)
```

```python
import functools
import math

import jax
import jax.numpy as jnp
import numpy as np
from jax import lax
from jax.experimental import pallas as pl
from jax.experimental.pallas import tpu as pltpu
from jax.experimental.pallas import tpu_sc as plsc

F32 = jnp.float32
BF16 = jnp.bfloat16

D_MODEL = 1024
DEPTH = 4
GRID_W = 64
N_CTX = 256
A_HEADS = 4
A_DH = 64
A_DV = 128
B_HEADS = 8
B_DH = 64
NA_ROWS = 8
NA_COLS = 16
C_HEADS = 4
C_DK = 128
C_DV = 256
C_RANK = 16
C_TAU = 16.0
C_CHUNK = 64
GLA_BLOCK = 256
N_EXPERTS = 16
N_GROUPS = 4
PER_GROUP = 4
ROPE_BASE = 10000.0
EPS = 1e-6
LOG2_E = math.log2(math.e)
ATT_KEY_BLOCK = 768
ATT_Q_HALVES = 4

LANES = 128
ROW_TILE = 256
MOE_TILE = 256
SC_CORES = 2
SC_SUBCORES = 16
SC_WORKERS = SC_CORES * SC_SUBCORES
SC_GATHER_ROWS = 64
N_CLASSES = N_GROUPS * 6
NA_QROWS = 16
NA_SUB = 2
NA_KROWS = NA_SUB + NA_ROWS - 1
VMEM_LIMIT = 52 * 1024 * 1024
ROW_SEGS = D_MODEL // LANES // 2

PAIR_LO = (0, 0, 0, 1, 1, 2)
PAIR_HI = (1, 2, 3, 2, 3, 3)


def _cparams(sem):
    return pltpu.CompilerParams(dimension_semantics=sem, vmem_limit_bytes=VMEM_LIMIT)


def _sigmoid(x):
    return 1.0 / (1.0 + jnp.exp(-x))


def _silu(x):
    return x * _sigmoid(x)


def _rms(x, g):
    return x * lax.rsqrt(jnp.mean(x * x, axis=-1, keepdims=True) + EPS) * g


def _dot(a, b):
    return jnp.dot(a, b, preferred_element_type=F32)


def _dot_nt(a, b):
    return lax.dot_general(a, b, (((1,), (1,)), ((), ())), preferred_element_type=F32)


def _dot_tn(a, b):
    return lax.dot_general(a, b, (((0,), (0,)), ((), ())), preferred_element_type=F32)


def _mod_kernel(c_ref, w_ref, b_ref, o_ref):
    a = _silu(c_ref[...]).astype(BF16)
    o_ref[0] = _dot(a, w_ref[0].astype(BF16)) + b_ref[0]


def _mod_vectors(cc, w_mod, b_mod):
    depth, d, n6 = w_mod.shape
    rows = cc.shape[0]
    tn = 1536
    return pl.pallas_call(
        _mod_kernel,
        out_shape=jax.ShapeDtypeStruct((depth, rows, n6), F32),
        grid=(depth, n6 // tn),
        in_specs=[
            pl.BlockSpec((rows, d), lambda i, j: (0, 0)),
            pl.BlockSpec((1, d, tn), lambda i, j: (i, 0, j)),
            pl.BlockSpec((1, 1, tn), lambda i, j: (i, 0, j)),
        ],
        out_specs=pl.BlockSpec((1, rows, tn), lambda i, j: (i, 0, j)),
        compiler_params=_cparams(("arbitrary", "arbitrary")),
        name="mod_vectors",
    )(cc, w_mod, b_mod.reshape(depth, 1, n6))


def _token_rows(ref):
    rows = ref.shape[0] // ROW_SEGS
    packed = jnp.concatenate([ref[pl.ds(sg, rows, stride=ROW_SEGS), :] for sg in range(ROW_SEGS)], axis=-1)
    low = lax.bitcast_convert_type(lax.shift_left(packed, jnp.int32(16)), F32)
    high = lax.bitcast_convert_type(packed & jnp.int32(-65536), F32)
    return jnp.concatenate([low, high], axis=-1)


def _store_token_rows(ref, val):
    rows, d = val.shape
    rounded = val.astype(BF16).astype(F32)
    bits = lax.bitcast_convert_type(rounded, jnp.int32)
    packed = lax.shift_right_logical(bits[:, :d // 2], jnp.int32(16)) | bits[:, d // 2:]
    for sg in range(ROW_SEGS):
        ref[pl.ds(sg, rows, stride=ROW_SEGS), :] = packed[:, sg * LANES:(sg + 1) * LANES]


def _proj_kernel(*refs, has_y, first, n_rope, n_bf16, col_chunk):
    it = iter(refs)
    s_ref = next(it)
    x_ref = next(it) if first else None
    y_ref = next(it) if has_y else None
    modp_ref = next(it) if has_y else None
    mod_ref = next(it)
    ng_ref = next(it)
    w_ref = next(it)
    cos_ref = next(it) if n_rope else None
    sin_ref = next(it) if n_rope else None
    snew_ref = next(it) if has_y or first else None
    p_ref = next(it)
    lr_ref = next(it) if w_ref.shape[1] > n_bf16 else None

    x = s_ref[0]
    if first:
        x = jnp.where(pl.program_id(1) == 0, x, x_ref[0])
        snew_ref[0] = x
    if has_y:
        x = x + modp_ref[0, 0, 5:6, :] * _token_rows(y_ref)
        snew_ref[0] = x
    h = _rms(x, ng_ref[...])
    h = h * (1.0 + mod_ref[0, 0, 1:2, :]) + mod_ref[0, 0, 0:1, :]
    hb = h.astype(BF16)
    n_out = w_ref.shape[1]
    for c0 in range(0, n_out, col_chunk):
        c1 = min(c0 + col_chunk, n_out)
        acc = _dot(hb, w_ref[:, c0:c1])
        for b0 in range(c0, c1, LANES):
            t = acc[:, b0 - c0:b0 - c0 + LANES]
            if b0 < n_rope:
                t = t * cos_ref[...] + pltpu.roll(t, LANES // 2, 1) * sin_ref[...]
            if b0 < n_bf16:
                p_ref[0, :, b0:b0 + LANES] = t.astype(BF16)
            else:
                lr_ref[0, :, b0 - n_bf16:b0 - n_bf16 + LANES] = t


def _project(s, y, modp, mod, ng, w, cos, sin, *, n_rope, n_bf16):
    first = isinstance(s, tuple)
    tm = ROW_TILE
    row_spec = pl.BlockSpec((1, tm, s[0].shape[2] if first else s.shape[2]), lambda i, j: (i, j, 0))
    if first:
        ctx, x = s
        b, n, d = x.shape[0], ctx.shape[1] + x.shape[1], x.shape[2]
        in_specs = [pl.BlockSpec((1, tm, d), lambda i, j: (i, 0, 0)),
                    pl.BlockSpec((1, tm, d), lambda i, j: (i, jnp.maximum(j - 1, 0), 0))]
        args = [ctx, x]
    else:
        b, n, d = s.shape
        in_specs = [row_spec]
        args = [s]
    n_out = w.shape[1]
    has_y = y is not None
    mod_spec = pl.BlockSpec((1, 1, 6, d), lambda i, j: (i, jnp.minimum(j, 1), 0, 0))
    if has_y:
        in_specs += [pl.BlockSpec((tm * ROW_SEGS, LANES), lambda i, j: (i * (n // tm) + j, 0)), mod_spec]
        args += [y, modp]
    in_specs += [mod_spec, pl.BlockSpec((1, d), lambda i, j: (0, 0)),
                 pl.BlockSpec((d, n_out), lambda i, j: (0, 0))]
    args += [mod, ng.reshape(1, d), w]
    if n_rope:
        tab_spec = pl.BlockSpec((tm, LANES), lambda i, j: (j, 0))
        in_specs += [tab_spec, tab_spec]
        args += [cos, sin]
    out_shape, out_specs = [], []
    if has_y or first:
        out_shape.append(jax.ShapeDtypeStruct((b, n, d), F32))
        out_specs.append(row_spec)
    out_shape.append(jax.ShapeDtypeStruct((b, n, n_bf16), BF16))
    out_specs.append(pl.BlockSpec((1, tm, n_bf16), lambda i, j: (i, j, 0)))
    if n_out > n_bf16:
        out_shape.append(jax.ShapeDtypeStruct((b, n, n_out - n_bf16), F32))
        out_specs.append(pl.BlockSpec((1, tm, n_out - n_bf16), lambda i, j: (i, j, 0)))
    return pl.pallas_call(
        functools.partial(_proj_kernel, has_y=has_y, first=first, n_rope=n_rope, n_bf16=n_bf16, col_chunk=512),
        out_shape=out_shape,
        grid=(b, n // tm),
        in_specs=in_specs,
        out_specs=out_specs,
        compiler_params=_cparams(("arbitrary", "arbitrary")),
        name="norm_mod_project",
    )(*args)


def _softmax_rows(s):
    e = jnp.exp(s - jnp.max(s, axis=-1, keepdims=True))
    return e * (1.0 / jnp.sum(e, axis=-1, keepdims=True))


def _diff_attn_kernel(*refs, lam_init, n_q):
    q_refs = refs[:n_q]
    k_ref, v_ref, lam_ref, g_ref, o_ref = refs[n_q:]
    lp = lam_ref[...]
    lam = (jnp.exp(jnp.sum(lp[0:1] * lp[1:2], axis=-1, keepdims=True))
           - jnp.exp(jnp.sum(lp[2:3] * lp[3:4], axis=-1, keepdims=True)) + lam_init)
    q = jnp.concatenate([r[0] for r in q_refs], axis=0)
    lane = lax.broadcasted_iota(jnp.int32, (1, LANES), 1)
    first_map = (lane // 32) % 2 == 0
    zero = jnp.zeros_like(q)
    q0 = jnp.where(first_map, q, zero)
    q1 = jnp.where(first_map, zero, q)
    n_keys = k_ref.shape[1]
    blk = min(ATT_KEY_BLOCK, n_keys)
    ones = jnp.ones((blk, A_DV), BF16)
    state = [None, None]
    for kb in range(n_keys // blk):
        k = k_ref[0, kb * blk:(kb + 1) * blk, :]
        v1 = jnp.concatenate([v_ref[0, kb * blk:(kb + 1) * blk, :], ones], axis=1)
        for i, qm in enumerate((q0, q1)):
            s = _dot_nt(qm, k)
            m_new = jnp.max(s, axis=-1, keepdims=True)
            if kb:
                m_old, acc_old = state[i]
                m_new = jnp.maximum(m_old, m_new)
            acc = _dot(jnp.exp2(s - m_new).astype(BF16), v1)
            if kb:
                acc = acc_old * jnp.exp2(m_old - m_new) + acc
            state[i] = (m_new, acc)
    acc0, acc1 = state[0][1], state[1][1]
    o = acc0[:, :A_DV] * (1.0 / acc0[:, A_DV:]) - acc1[:, :A_DV] * (lam / acc1[:, A_DV:])
    o_ref[0] = (_rms(o, g_ref[...]) * (1.0 - lam_init)).astype(BF16)


def _diff_attention(p, lam_p, subln_g, lam_init):
    b, n, _ = p.shape
    half = ROW_TILE
    tq = ATT_Q_HALVES * half
    ctx_blocks = N_CTX // half
    width = A_HEADS * A_DV
    small = [pl.BlockSpec((4, A_DH), lambda i, h, *_: (0, 0)), pl.BlockSpec((1, A_DV), lambda i, h, *_: (0, 0))]
    q_specs = [pl.BlockSpec((1, half, LANES), functools.partial(lambda i, h, j, part: (i, ctx_blocks + ATT_Q_HALVES * j + part, h),
                                                                 part=part)) for part in range(ATT_Q_HALVES)]
    oa_x = pl.pallas_call(
        functools.partial(_diff_attn_kernel, lam_init=lam_init, n_q=ATT_Q_HALVES),
        out_shape=jax.ShapeDtypeStruct((b, n - N_CTX, width), BF16),
        grid=(b, A_HEADS, (n - N_CTX) // tq),
        in_specs=q_specs + [
            pl.BlockSpec((1, n, LANES), lambda i, h, j: (i, 0, A_HEADS + h)),
            pl.BlockSpec((1, n, LANES), lambda i, h, j: (i, 0, 2 * A_HEADS + h)),
        ] + small,
        out_specs=pl.BlockSpec((1, tq, LANES), lambda i, h, j: (i, j, h)),
        compiler_params=_cparams(("arbitrary", "arbitrary", "arbitrary")),
        name="diff_attention",
    )(*([p] * ATT_Q_HALVES), p, p, lam_p, subln_g.reshape(1, A_DV))
    def all_heads(*refs):
        q_ref, k_ref, v_ref, lam_ref, g_ref, o_ref = refs
        for h in range(A_HEADS):
            cols = pl.ds(h * LANES, LANES)
            _diff_attn_kernel(q_ref.at[:, :, cols], k_ref.at[:, :, cols], v_ref.at[:, :, cols], lam_ref, g_ref,
                              o_ref.at[:, :, cols], lam_init=lam_init, n_q=1)

    oa_c = pl.pallas_call(
        all_heads,
        out_shape=jax.ShapeDtypeStruct((b, N_CTX, width), BF16),
        grid=(b,),
        in_specs=[pl.BlockSpec((1, N_CTX, width), functools.partial(lambda i, part: (i, 0, part), part=part))
                  for part in range(3)] + [pl.BlockSpec((4, A_DH), lambda i: (0, 0)),
                                           pl.BlockSpec((1, A_DV), lambda i: (0, 0))],
        out_specs=pl.BlockSpec((1, N_CTX, width), lambda i: (i, 0, 0)),
        compiler_params=_cparams(("arbitrary",)),
        name="context_diff_attention",
    )(p, p, p, lam_p, subln_g.reshape(1, A_DV))
    return oa_x, oa_c


def _split_heads(q):
    lane = lax.broadcasted_iota(jnp.int32, (1, LANES), 1)
    first_head = lane < B_DH
    zero = jnp.zeros_like(q)
    return first_head, (jnp.where(first_head, q, zero), jnp.where(first_head, zero, q))


def _na_ctx_kernel(q_ref, k_ref, v_ref, o_ref):
    for pair in range(q_ref.shape[2] // LANES):
        cols = slice(pair * LANES, (pair + 1) * LANES)
        first_head, qh = _split_heads(q_ref[0, :, cols])
        kc = k_ref[0, :, cols]
        vc = v_ref[0, :, cols]
        outs = [_dot(_softmax_rows(_dot_nt(qh[h], kc)).astype(BF16), vc) for h in range(2)]
        o_ref[0, :, cols] = jnp.where(first_head, outs[0], outs[1]).astype(BF16)


def _na_kernel(*refs):
    q_refs = refs[:-4]
    k_ref, v_ref, bm_ref, o_ref = refs[-4:]
    blk = pl.program_id(2)
    rows = (k_ref.shape[1] - N_CTX) // GRID_W
    sub_q = NA_SUB * GRID_W
    sub_k = NA_KROWS * GRID_W
    kc = k_ref[0, 0:N_CTX, :]
    vc = jnp.concatenate([v_ref[0, 0:N_CTX, :], jnp.ones((N_CTX, LANES), BF16)], axis=1)
    ones = jnp.ones((sub_k, LANES), BF16)
    for sub in range(NA_QROWS // NA_SUB):
        q_ref = q_refs[sub * sub_q // ROW_TILE]
        q0 = (sub * sub_q) % ROW_TILE
        first_head, qh = _split_heads(q_ref[0, q0:q0 + sub_q, :])
        k_row0 = jnp.clip(blk * NA_QROWS + sub * NA_SUB - NA_ROWS // 2, 0, rows - NA_KROWS)
        start = pl.multiple_of(N_CTX + k_row0 * GRID_W, GRID_W)
        kw = k_ref[0, pl.ds(start, sub_k), :]
        vw = jnp.concatenate([v_ref[0, pl.ds(start, sub_k), :], ones], axis=1)
        outs = []
        for h in range(2):
            s_loc = _dot_nt(qh[h], kw) + bm_ref[h, 0, sub]
            s_ctx = _dot_nt(qh[h], kc)
            m = jnp.maximum(jnp.max(s_loc, axis=-1, keepdims=True), jnp.max(s_ctx, axis=-1, keepdims=True))
            acc = _dot(jnp.exp(s_loc - m).astype(BF16), vw) + _dot(jnp.exp(s_ctx - m).astype(BF16), vc)
            outs.append(acc[:, :LANES] * (1.0 / acc[:, LANES:]))
        o_ref[0, sub * sub_q:(sub + 1) * sub_q, :] = jnp.where(first_head, outs[0], outs[1]).astype(BF16)


def _na_bias_table(na_bias, rows):
    h = na_bias.shape[0]
    n_dr, n_dc = 2 * NA_ROWS - 1, 2 * NA_COLS - 1
    width = 2 * GRID_W
    left = GRID_W - NA_COLS
    u = jnp.pad(na_bias, ((0, 0), (0, 0), (left, width - left - n_dc)))
    skew = jnp.tile(u, (1, 1, GRID_W))[:, :, :GRID_W * (width - 1)].reshape(h, n_dr, GRID_W, width - 1)
    toeplitz = skew[:, :, :, GRID_W - 1:]
    margin = NA_KROWS - NA_ROWS
    by_col = jnp.pad(toeplitz.transpose(0, 2, 1, 3), ((0, 0), (0, 0), (margin, margin), (0, 0)))
    by_col = by_col.reshape(h, GRID_W, (n_dr + 2 * margin) * GRID_W)
    col = np.arange(GRID_W)
    col_start = np.clip(col - NA_COLS // 2, 0, GRID_W - NA_COLS)
    col_ok = (col[None, :] >= col_start[:, None]) & (col[None, :] < col_start[:, None] + NA_COLS)
    n_sub = NA_QROWS // NA_SUB
    runs = []
    for q_row0 in (0, NA_QROWS, rows - NA_QROWS):
        for sub in range(n_sub):
            k_row0 = int(np.clip(q_row0 + sub * NA_SUB - NA_ROWS // 2, 0, rows - NA_KROWS))
            kr = k_row0 + np.arange(NA_KROWS)
            firsts, oks = [], []
            for rq in range(NA_SUB):
                r = q_row0 + sub * NA_SUB + rq
                r0 = int(np.clip(r - NA_ROWS // 2, 0, rows - NA_ROWS))
                row_ok = (kr >= r0) & (kr < r0 + NA_ROWS)
                oks.append((col_ok[:, None, :] & row_ok[None, :, None]).reshape(GRID_W, NA_KROWS * GRID_W))
                firsts.append(k_row0 - r + NA_ROWS - 1 + margin)
            key = (tuple(firsts), np.stack(oks).tobytes())
            if runs and runs[-1][0] == key:
                runs[-1][2] += 1
                continue
            windows = [by_col[:, :, f * GRID_W:(f + NA_KROWS) * GRID_W] for f in firsts]
            block = jnp.where(np.concatenate(oks, axis=0)[None], jnp.concatenate(windows, axis=1), -jnp.inf)
            runs.append([key, block, 1])
    table = jnp.concatenate([jnp.broadcast_to(blk[:, None], (h, rep) + blk.shape[1:]) for _, blk, rep in runs], axis=1)
    return table.reshape(h, 3, n_sub, NA_SUB * GRID_W, NA_KROWS * GRID_W)


def _neighbourhood_attention(p, bm):
    b, n, _ = p.shape
    tq = NA_QROWS * GRID_W
    parts = tq // ROW_TILE
    n_blocks = (n - N_CTX) // tq
    ctx_blocks = N_CTX // ROW_TILE
    col0 = 3 * A_HEADS
    pairs = B_HEADS // 2
    width = B_HEADS * B_DH

    def pattern(j):
        return jnp.minimum(j, 1) + (j == n_blocks - 1).astype(jnp.int32)

    ob_x = pl.pallas_call(
        _na_kernel,
        out_shape=jax.ShapeDtypeStruct((b, n - N_CTX, width), BF16),
        grid=(b, pairs, n_blocks),
        in_specs=[pl.BlockSpec((1, ROW_TILE, LANES),
                               functools.partial(lambda i, h, j, part: (i, ctx_blocks + parts * j + part, col0 + h),
                                                 part=part)) for part in range(parts)] + [
            pl.BlockSpec((1, n, LANES), lambda i, h, j: (i, 0, col0 + pairs + h)),
            pl.BlockSpec((1, n, LANES), lambda i, h, j: (i, 0, col0 + 2 * pairs + h)),
            pl.BlockSpec((2, 1, NA_QROWS // NA_SUB, NA_SUB * GRID_W, NA_KROWS * GRID_W),
                         lambda i, h, j: (h, pattern(j), 0, 0, 0)),
        ],
        out_specs=pl.BlockSpec((1, tq, LANES), lambda i, h, j: (i, j, h)),
        compiler_params=_cparams(("arbitrary", "arbitrary", "arbitrary")),
        name="neighbourhood_attention",
    )(*([p] * parts), p, p, bm)
    ob_c = pl.pallas_call(
        _na_ctx_kernel,
        out_shape=jax.ShapeDtypeStruct((b, N_CTX, width), BF16),
        grid=(b,),
        in_specs=[pl.BlockSpec((1, N_CTX, width), functools.partial(lambda i, part: (i, 0, col0 // pairs + part), part=part))
                  for part in range(3)],
        out_specs=pl.BlockSpec((1, N_CTX, width), lambda i: (i, 0, 0)),
        compiler_params=_cparams(("arbitrary",)),
        name="context_attention",
    )(p, p, p)
    return ob_x, ob_c


def _gla_kernel(q_ref, k_ref, v_ref, lr_ref, wg_ref, bg_ref, o_ref, ob_ref, g_ref, sf_ref, sb_ref):
    n = q_ref.shape[1]
    blk = GLA_BLOCK
    per_blk = blk // C_CHUNK
    n_blocks = n // blk
    ri = lax.broadcasted_iota(jnp.int32, (blk, blk), 0)
    ci = lax.broadcasted_iota(jnp.int32, (blk, blk), 1)
    same_chunk = (ri // C_CHUNK) == (ci // C_CHUNK)
    keeps = (same_chunk & (ci <= ri), same_chunk & (ci >= ri))
    tris = tuple(jnp.where(kp, 1.0, 0.0).astype(BF16) for kp in keeps)
    w_gate = jnp.concatenate([wg_ref[0], wg_ref[1]], axis=1).astype(BF16)
    b_gate = jnp.concatenate([bg_ref[0:1, :], bg_ref[1:2, :]], axis=1)
    for i in range(n_blocks):
        z = _dot(lr_ref[0, i * blk:(i + 1) * blk, :].astype(BF16), w_gate) + b_gate
        g_ref[i * blk:(i + 1) * blk, :] = (jnp.minimum(z, 0.0) - jnp.log(1.0 + jnp.exp(-jnp.abs(z)))) * (1.0 / C_TAU)

    def block(sb, direction, st_ref):
        keep = keeps[direction]
        end_row = C_CHUNK - 1 if direction == 0 else 0
        mid_row = C_CHUNK // 2 - 1 if direction == 0 else C_CHUNK // 2
        r0 = sb * blk
        g = g_ref[r0:r0 + blk, direction * C_DK:(direction + 1) * C_DK]
        g_hi = g.astype(BF16)
        g_lo = (g - g_hi.astype(F32)).astype(BF16)
        gc2 = _dot(tris[direction], jnp.concatenate([g_hi, g_lo], axis=1))
        gc = (gc2[:, :C_DK] + gc2[:, C_DK:]).reshape(per_blk, C_CHUNK, C_DK)
        g_end = gc[:, end_row:end_row + 1, :]
        g_mid = gc[:, mid_row:mid_row + 1, :]
        q = (q_ref[0, r0:r0 + blk, :].astype(F32) * (C_DK ** -0.5)).reshape(per_blk, C_CHUNK, C_DK)
        k = k_ref[0, r0:r0 + blk, :].astype(F32).reshape(per_blk, C_CHUNK, C_DK)
        v = v_ref[0, r0:r0 + blk, :]
        q_in = (q * jnp.exp(gc)).astype(BF16).reshape(blk, C_DK)
        q_mid = (q * jnp.exp(gc - g_mid)).astype(BF16).reshape(blk, C_DK)
        k_mid = (k * jnp.exp(g_mid - gc)).astype(BF16).reshape(blk, C_DK)
        k_end = (k * jnp.exp(g_end - gc)).astype(BF16).reshape(blk, C_DK)
        a = jnp.where(keep, _dot_nt(q_mid, k_mid), 0.0)
        o_intra = _dot(a.astype(BF16), v)
        ends = jnp.concatenate([g_end.reshape(per_blk, C_DK), jnp.zeros((8 - per_blk, C_DK), F32)], axis=0)
        decay = jnp.transpose(jnp.exp(ends))
        o_inter = [None] * per_blk
        for c in (range(per_blk) if direction == 0 else reversed(range(per_blk))):
            rows = slice(c * C_CHUNK, (c + 1) * C_CHUNK)
            st = st_ref[...]
            o_inter[c] = _dot(q_in[rows], st.astype(BF16))
            st_ref[...] = st * decay[:, c:c + 1] + _dot_tn(k_end[rows], v[rows])
        return o_intra + jnp.concatenate(o_inter, axis=0)

    sf_ref[...] = jnp.zeros_like(sf_ref)
    sb_ref[...] = jnp.zeros_like(sb_ref)
    ctx_blocks = N_CTX // blk
    order_b = list(reversed(range(ctx_blocks))) + list(reversed(range(ctx_blocks, n_blocks)))
    for i in range(n_blocks):
        o_ref[0, i * blk:(i + 1) * blk, :] = block(i, 0, sf_ref)
        sb = order_b[i]
        ob_ref[sb * blk:(sb + 1) * blk, :] = block(sb, 1, sb_ref)
    o_ref[0] = o_ref[0] + ob_ref[...]


def _gla(p, lr, wg, bg):
    b, n, _ = p.shape
    return pl.pallas_call(
        _gla_kernel,
        out_shape=jax.ShapeDtypeStruct((b, n, C_HEADS * C_DV), F32),
        grid=(b, C_HEADS),
        in_specs=[
            pl.BlockSpec((1, n, C_DK), lambda i, h: (i, 0, h)),
            pl.BlockSpec((1, n, C_DK), lambda i, h: (i, 0, C_HEADS + h)),
            pl.BlockSpec((1, n, C_DV), lambda i, h: (i, 0, C_HEADS + h)),
            pl.BlockSpec((1, n, LANES), lambda i, h: (i, 0, 0)),
            pl.BlockSpec((2, LANES, C_DK), lambda i, h: (0, 0, h)),
            pl.BlockSpec((2, C_DK), lambda i, h: (0, h)),
        ],
        out_specs=pl.BlockSpec((1, n, C_DV), lambda i, h: (i, 0, h)),
        scratch_shapes=[pltpu.VMEM((n, C_DV), F32), pltpu.VMEM((n, 2 * C_DK), F32),
                        pltpu.VMEM((C_DK, C_DV), F32), pltpu.VMEM((C_DK, C_DV), F32)],
        compiler_params=_cparams(("arbitrary", "arbitrary")),
        name="gla",
    )(p, p, p, lr, wg, bg)


def _out_kernel(*refs, gla):
    if gla:
        o_ref, gate_ref, gn_ref, w_ref, s_ref, mod_ref, ng_ref, rw_ref, snew_ref, h2_ref, lg_ref = refs
        o = o_ref[0]
        gate = gate_ref[0].astype(F32)
        parts = []
        for hd in range(C_HEADS):
            oh = _rms(o[:, hd * C_DV:(hd + 1) * C_DV], gn_ref[...])
            parts.append((oh * _silu(gate[:, hd * C_DV:(hd + 1) * C_DV])).astype(BF16))
        acc = _dot(jnp.concatenate(parts, axis=-1), w_ref[...])
    else:
        oax_ref, oac_ref, obx_ref, obc_ref, w_ref, s_ref, mod_ref, ng_ref, rw_ref, snew_ref, h2_ref, lg_ref = refs
        half = oax_ref.shape[2]
        is_ctx = pl.program_id(1) == 0
        oa = jnp.where(is_ctx, oac_ref[0], oax_ref[0])
        ob = jnp.where(is_ctx, obc_ref[0], obx_ref[0])
        acc = _dot(oa, w_ref[0:half, :]) + _dot(ob, w_ref[half:, :])
    x = s_ref[0] + mod_ref[0, 0, 2:3, :] * acc
    snew_ref[0] = x
    h2 = _rms(x, ng_ref[...]) * (1.0 + mod_ref[0, 0, 4:5, :]) + mod_ref[0, 0, 3:4, :]
    _store_token_rows(h2_ref, h2)
    t = _dot(h2.astype(BF16), rw_ref[...])
    lg_ref[0] = t + pltpu.roll(t, LANES - N_EXPERTS, 1)


def _out_project(mix, w_out, s, mod, ng, rw, *, gla, gn=None):
    b, n, d = s.shape
    tm = ROW_TILE
    row_spec = pl.BlockSpec((1, tm, d), lambda i, j: (i, j, 0))
    const2 = lambda i, j: (0, 0)
    if gla:
        o, p = mix
        in_specs = [row_spec, pl.BlockSpec((1, tm, d), lambda i, j: (i, j, 2)),
                    pl.BlockSpec((1, C_DV), const2)]
        args = [o, p, gn.reshape(1, C_DV)]
    else:
        oa_x, oa_c, ob_x, ob_c = mix
        half = oa_x.shape[2]
        latent = pl.BlockSpec((1, tm, half), lambda i, j: (i, jnp.maximum(j - 1, 0), 0))
        context = pl.BlockSpec((1, tm, half), lambda i, j: (i, 0, 0))
        in_specs = [latent, context, latent, context]
        args = [oa_x, oa_c, ob_x, ob_c]
    in_specs += [pl.BlockSpec((d, d), const2), row_spec,
                 pl.BlockSpec((1, 1, 6, d), lambda i, j: (i, jnp.minimum(j, 1), 0, 0)),
                 pl.BlockSpec((1, d), const2), pl.BlockSpec((d, LANES), const2)]
    args += [w_out, s, mod, ng.reshape(1, d), rw]
    return pl.pallas_call(
        functools.partial(_out_kernel, gla=gla),
        out_shape=[jax.ShapeDtypeStruct((b, n, d), F32), jax.ShapeDtypeStruct((b * n * ROW_SEGS, LANES), jnp.int32),
                   jax.ShapeDtypeStruct((b, n, LANES), F32)],
        grid=(b, n // tm),
        in_specs=in_specs,
        out_specs=[row_spec, pl.BlockSpec((tm * ROW_SEGS, LANES), lambda i, j: (i * (n // tm) + j, 0)),
                   pl.BlockSpec((1, tm, LANES), lambda i, j: (i, j, 0))],
        compiler_params=_cparams(("arbitrary", "arbitrary")),
        name="out_project",
    )(*args)


def _route_kernel(bias_ref, lg_ref, cls_ref, gate_ref, rank_ref, cnt_ref):
    score = [_sigmoid(lg_ref[e]) for e in range(N_EXPERTS)]
    sel = [score[e] + bias_ref[e] for e in range(N_EXPERTS)]
    grp_score = []
    for g in range(N_GROUPS):
        v = sel[g * PER_GROUP:(g + 1) * PER_GROUP]
        best = v[0] + v[1]
        for a in range(PER_GROUP):
            for c in range(a + 1, PER_GROUP):
                if (a, c) != (0, 1):
                    best = jnp.maximum(best, v[a] + v[c])
        grp_score.append(best)
    grp = jnp.zeros(grp_score[0].shape, jnp.int32)
    best = grp_score[0]
    for g in range(1, N_GROUPS):
        upd = grp_score[g] > best
        best = jnp.where(upd, grp_score[g], best)
        grp = jnp.where(upd, g, grp)

    def pick(vals, j):
        out = vals[j]
        for g in range(1, N_GROUPS):
            out = jnp.where(grp == g, vals[g * PER_GROUP + j], out)
        return out

    v = [pick(sel, j) for j in range(PER_GROUP)]
    sc = [pick(score, j) for j in range(PER_GROUP)]
    one = jnp.ones(grp.shape, jnp.int32)
    zero = jnp.zeros(grp.shape, jnp.int32)
    chosen = []
    for j in range(PER_GROUP):
        rank = zero
        for m in range(PER_GROUP):
            if m == j:
                continue
            ahead = (v[m] >= v[j]) if m < j else (v[m] > v[j])
            rank = rank + jnp.where(ahead, one, zero)
        chosen.append(rank < 2)
    code = zero
    for j in range(PER_GROUP):
        code = code + jnp.where(chosen[j], one * (1 << j), zero)
    pair = zero
    for idx in range(6):
        pair = jnp.where(code == (1 << PAIR_LO[idx]) + (1 << PAIR_HI[idx]), idx, pair)
    s_lo = jnp.where(chosen[0], sc[0], jnp.where(chosen[1], sc[1], sc[2]))
    s_hi = jnp.where(chosen[3], sc[3], jnp.where(chosen[2], sc[2], sc[1]))
    den = s_lo + s_hi
    cls = grp * 6 + pair
    cls_ref[...] = cls
    w_lo = s_lo / den
    w_hi = s_hi / den
    gate_ref[...] = jnp.zeros_like(gate_ref)
    filler = jnp.zeros((6, LANES), F32)
    for row in range(cls.shape[0]):
        pair_rows = jnp.concatenate([w_lo[row:row + 1], w_hi[row:row + 1], filler], axis=0)
        gate_ref[row * LANES:(row + 1) * LANES, 0:8] = jnp.transpose(pair_rows)
    r = cls.shape[0]
    lane = lax.broadcasted_iota(jnp.int32, (1, LANES), 1)
    li = lax.broadcasted_iota(jnp.int32, (LANES, LANES), 0)
    lj = lax.broadcasted_iota(jnp.int32, (LANES, LANES), 1)
    upto_lane = jnp.where(li <= lj, 1.0, 0.0).astype(BF16)
    ri = lax.broadcasted_iota(jnp.int32, (r, r), 0)
    rj = lax.broadcasted_iota(jnp.int32, (r, r), 1)
    earlier_rows = jnp.where(rj < ri, 1.0, 0.0).astype(BF16)
    member = [jnp.where(cls == c, 1.0, 0.0) for c in range(N_CLASSES)]
    row_totals = jnp.zeros((r, LANES), F32)
    for c in range(N_CLASSES):
        row_totals = row_totals + jnp.where(lane == c, jnp.sum(member[c], axis=1, keepdims=True), 0.0)
    before_row = _dot(earlier_rows, row_totals.astype(BF16))
    rank = jnp.zeros((r, LANES), F32)
    for c in range(N_CLASSES):
        in_row = _dot(member[c].astype(BF16), upto_lane)
        offset = jnp.sum(jnp.where(lane == c, before_row, 0.0), axis=1, keepdims=True)
        rank = rank + member[c] * (in_row + offset - 1.0)
    rank_ref[...] = rank.astype(jnp.int32)
    counts = before_row[r - 1:r, :] + row_totals[r - 1:r, :]
    cnt_ref[...] = jnp.broadcast_to(counts, cnt_ref.shape).astype(jnp.int32)


def _route(logits_t, router_bias):
    _, r, _ = logits_t.shape
    full = pl.BlockSpec((r, LANES), lambda i: (0, 0))
    return pl.pallas_call(
        _route_kernel,
        out_shape=[jax.ShapeDtypeStruct((r, LANES), jnp.int32), jax.ShapeDtypeStruct((r * LANES, LANES), F32),
                   jax.ShapeDtypeStruct((r, LANES), jnp.int32), jax.ShapeDtypeStruct((8, LANES), jnp.int32)],
        grid=(1,),
        in_specs=[pl.BlockSpec(memory_space=pltpu.SMEM),
                  pl.BlockSpec((N_EXPERTS, r, LANES), lambda i: (0, 0, 0))],
        out_specs=[full, pl.BlockSpec((r * LANES, LANES), lambda i: (0, 0)), full,
                   pl.BlockSpec((8, LANES), lambda i: (0, 0))],
        compiler_params=_cparams(("arbitrary",)),
        name="route",
    )(router_bias, logits_t)


def _gather_rows(table, idx):
    p = idx.shape[0]
    per_worker = p // SC_WORKERS
    n_chunks = per_worker // SC_GATHER_ROWS
    mesh = plsc.VectorSubcoreMesh(core_axis_name="c", subcore_axis_name="s", num_cores=SC_CORES,
                                  num_subcores=SC_SUBCORES)

    @functools.partial(
        pl.kernel, mesh=mesh,
        out_type=jax.ShapeDtypeStruct((p,) + table.shape[1:], table.dtype),
        scratch_types=[pltpu.VMEM((per_worker,), jnp.int32),
                       pltpu.VMEM((SC_GATHER_ROWS,) + table.shape[1:], table.dtype),
                       pltpu.SemaphoreType.DMA],
        compiler_params=pltpu.CompilerParams(use_tc_tiling_on_sc=True),
        name="gather_rows",
    )
    def gather(table_hbm, idx_hbm, out_hbm, idx_v, rows_v, sem):
        worker = lax.axis_index("s") * SC_CORES + lax.axis_index("c")
        base = worker * per_worker
        pltpu.sync_copy(idx_hbm.at[pl.ds(base, per_worker)], idx_v)

        @pl.loop(0, n_chunks)
        def _(i):
            off = pl.multiple_of(i * SC_GATHER_ROWS, SC_GATHER_ROWS)
            pltpu.async_copy(table_hbm.at[idx_v.at[pl.ds(off, SC_GATHER_ROWS)]], rows_v, sem).wait()
            pltpu.sync_copy(rows_v, out_hbm.at[pl.ds(base + off, SC_GATHER_ROWS)])

    return gather(table, idx)


def _scatter_rows(src, idx, n_out):
    t = idx.shape[0]
    per_worker = t // SC_WORKERS
    n_chunks = per_worker // SC_GATHER_ROWS
    mesh = plsc.VectorSubcoreMesh(core_axis_name="c", subcore_axis_name="s", num_cores=SC_CORES,
                                  num_subcores=SC_SUBCORES)

    @functools.partial(
        pl.kernel, mesh=mesh,
        out_type=jax.ShapeDtypeStruct((n_out,) + src.shape[1:], src.dtype),
        scratch_types=[pltpu.VMEM((n_chunks, SC_GATHER_ROWS), jnp.int32),
                       pltpu.VMEM((SC_GATHER_ROWS,) + src.shape[1:], src.dtype),
                       pltpu.SemaphoreType.DMA],
        compiler_params=pltpu.CompilerParams(use_tc_tiling_on_sc=True),
        name="scatter_rows",
    )
    def scatter(src_hbm, idx_hbm, out_hbm, idx_v, rows_v, sem):
        worker = lax.axis_index("s") * SC_CORES + lax.axis_index("c")
        base = worker * per_worker
        pltpu.sync_copy(idx_hbm.at[worker], idx_v)

        @pl.loop(0, n_chunks)
        def _(i):
            off = pl.multiple_of(i * SC_GATHER_ROWS, SC_GATHER_ROWS)
            pltpu.sync_copy(src_hbm.at[pl.ds(base + off, SC_GATHER_ROWS)], rows_v)
            pltpu.async_copy(rows_v, out_hbm.at[idx_v.at[i]], sem).wait()

    return scatter(src, idx.reshape(SC_WORKERS, n_chunks, SC_GATHER_ROWS))


def _cast_kernel(*refs):
    n = (len(refs) - 1) // 2
    for i in range(n):
        refs[n + 1 + i][0] = refs[i][0, 0].astype(BF16)


def _cast_expert_weights(layer, w1, w3, w2, after):
    ws = (w1, w3, w2)
    e = w1.shape[1]
    in_specs = [pl.BlockSpec((1, 1) + w.shape[2:], lambda j: (layer, j, 0, 0)) for w in ws]
    in_specs.append(pl.BlockSpec(memory_space=pl.ANY))
    return pl.pallas_call(
        _cast_kernel,
        out_shape=[jax.ShapeDtypeStruct(w.shape[1:], BF16) for w in ws],
        grid=(e,),
        in_specs=in_specs,
        out_specs=[pl.BlockSpec((1,) + w.shape[2:], lambda j: (j, 0, 0)) for w in ws],
        compiler_params=_cparams(("arbitrary",)),
        name="cast_expert_weights",
    )(*ws, after)


def _moe_kernel(lo_ref, hi_ref, nt_ref, x_ref, wr_ref, w1l_ref, w1h_ref, w3l_ref, w3h_ref, w2l_ref, w2h_ref, y_ref):
    j = pl.program_id(0)

    @pl.when(j < nt_ref[0])
    def _():
        x = _token_rows(x_ref).astype(BF16)
        wr = wr_ref[...]
        he_lo = (_silu(_dot(x, w1l_ref[0])) * _dot(x, w3l_ref[0]) * wr[:, 0:1]).astype(BF16)
        he_hi = (_silu(_dot(x, w1h_ref[0])) * _dot(x, w3h_ref[0]) * wr[:, 1:2]).astype(BF16)
        y = _dot(he_lo, w2l_ref[0]) + _dot(he_hi, w2h_ref[0])
        _store_token_rows(y_ref, y)

    @pl.when(j >= nt_ref[0])
    def _():
        y_ref[...] = jnp.zeros_like(y_ref)


def _moe_experts(xs, wrow, tile_lo, tile_hi, n_tiles, w1, w3, w2):
    tp = xs.shape[0] // ROW_SEGS
    tm = MOE_TILE
    d, de = w1.shape[1], w1.shape[2]

    def gate_map(j, lo, hi, nt):
        return (jnp.minimum(j, nt[0] - 1), 0)

    def lo_map(j, lo, hi, nt):
        return (lo[jnp.minimum(j, nt[0] - 1)], 0, 0)

    def hi_map(j, lo, hi, nt):
        return (hi[jnp.minimum(j, nt[0] - 1)], 0, 0)

    up = (1, d, de)
    down = (1, de, d)
    return pl.pallas_call(
        _moe_kernel,
        out_shape=jax.ShapeDtypeStruct((tp * ROW_SEGS, LANES), jnp.int32),
        grid_spec=pltpu.PrefetchScalarGridSpec(
            num_scalar_prefetch=3,
            grid=(tp // tm,),
            in_specs=[pl.BlockSpec((tm * ROW_SEGS, LANES), gate_map), pl.BlockSpec((tm, LANES), gate_map),
                      pl.BlockSpec(up, lo_map), pl.BlockSpec(up, hi_map),
                      pl.BlockSpec(up, lo_map), pl.BlockSpec(up, hi_map),
                      pl.BlockSpec(down, lo_map), pl.BlockSpec(down, hi_map)],
            out_specs=pl.BlockSpec((tm * ROW_SEGS, LANES), lambda j, lo, hi, nt: (j, 0)),
        ),
        compiler_params=_cparams(("arbitrary",)),
        name="moe_experts",
    )(tile_lo, tile_hi, n_tiles, xs, wrow, w1, w1, w3, w3, w2, w2)


def _moe(h2, logits, router_bias, w1, w3, w2):
    t = h2.shape[0]
    tm = MOE_TILE
    n_tiles_max = t // tm + N_CLASSES
    tp = n_tiles_max * tm
    lg_t = logits[:, :N_EXPERTS].T.reshape(N_EXPERTS, t // LANES, LANES)
    cls, gates, rank, counts = _route(lg_t, router_bias)
    cls, rank = cls.reshape(t), rank.reshape(t)
    counts = counts[0, :N_CLASSES]
    classes = jnp.arange(N_CLASSES, dtype=jnp.int32)
    onehot = (cls[:, None] == classes[None, :]).astype(jnp.int32)
    tiles_per = (counts + tm - 1) // tm
    tile_end = jnp.cumsum(tiles_per)
    tile_start = tile_end - tiles_per
    dest = jnp.sum(onehot * (tile_start * tm)[None, :], axis=1) + rank
    tiles = jnp.arange(n_tiles_max, dtype=jnp.int32)
    tile_cls = jnp.minimum(jnp.sum((tile_end[None, :] <= tiles[:, None]).astype(jnp.int32), axis=1), N_CLASSES - 1)
    pair = tile_cls % 6
    base = (tile_cls // 6) * PER_GROUP
    pair_onehot = (pair[:, None] == jnp.arange(6, dtype=jnp.int32)[None, :]).astype(jnp.int32)
    tile_lo = base + jnp.sum(pair_onehot * jnp.asarray(PAIR_LO, jnp.int32)[None, :], axis=1)
    tile_hi = base + jnp.sum(pair_onehot * jnp.asarray(PAIR_HI, jnp.int32)[None, :], axis=1)
    wrow = _scatter_rows(gates.reshape(t, 1, LANES), dest, tp).reshape(tp, LANES)
    xs = _scatter_rows(h2, dest, tp).reshape(tp * ROW_SEGS, LANES)
    ys = _moe_experts(xs, wrow, tile_lo, tile_hi, tile_end[-1:], w1, w3, w2)
    return _gather_rows(ys.reshape(tp, ROW_SEGS, LANES), dest), dest


def _final_kernel(s_ref, y_ref, mod_ref, g_ref, o_ref):
    x = s_ref[0] + mod_ref[0, 0, 5:6, :] * _token_rows(y_ref)
    o_ref[0] = _rms(x, g_ref[...])


def _final(s, y, mod, final_g):
    b, n, d = s.shape
    tm = ROW_TILE
    skip = N_CTX // tm
    row_spec = pl.BlockSpec((1, tm, d), lambda i, j: (i, j + skip, 0))
    return pl.pallas_call(
        _final_kernel,
        out_shape=jax.ShapeDtypeStruct((b, n - N_CTX, d), F32),
        grid=(b, (n - N_CTX) // tm),
        in_specs=[row_spec, pl.BlockSpec((tm * ROW_SEGS, LANES), lambda i, j: (i * (n // tm) + j + skip, 0)),
                  pl.BlockSpec((1, 1, 6, d), lambda i, j: (i, 1, 0, 0)),
                  pl.BlockSpec((1, d), lambda i, j: (0, 0))],
        out_specs=pl.BlockSpec((1, tm, d), lambda i, j: (i, j, 0)),
        compiler_params=_cparams(("arbitrary", "arbitrary")),
        name="final_norm",
    )(s, y, mod, final_g.reshape(1, d))


def _rope_tables(seq):
    pos = jnp.arange(seq)
    row_pos, col_pos = pos // GRID_W, pos % GRID_W
    n = A_DH // 2
    inv = ROPE_BASE ** (-jnp.arange(0, n, 2, dtype=F32) / n)
    ang_r = row_pos.astype(F32)[:, None] * inv[None, :]
    ang_c = col_pos.astype(F32)[:, None] * inv[None, :]
    ang = jnp.concatenate([ang_r, ang_c], axis=-1)
    cos = jnp.tile(jnp.cos(ang), (1, 4))
    sin = jnp.tile(jnp.sin(ang), (1, 4))
    sin = jnp.concatenate([-sin[:, :LANES // 2], sin[:, LANES // 2:]], axis=-1)
    cos = jnp.concatenate([jnp.ones((N_CTX, LANES), F32), cos], axis=0)
    sin = jnp.concatenate([jnp.zeros((N_CTX, LANES), F32), sin], axis=0)
    return cos, sin


def _interleave_maps(w):
    d = w.shape[0]
    w = w.reshape(d, A_HEADS, 2, 2, 2, A_DH // 4)
    return w.transpose(0, 1, 4, 2, 3, 5).reshape(d, A_HEADS * LANES)


def _att_weights(w_in):
    a_qk = A_HEADS * 2 * A_DH
    a_v = A_HEADS * A_DV
    b_w = B_HEADS * B_DH
    qa = _interleave_maps(w_in[:, :a_qk]) * (A_DH ** -0.5 * LOG2_E)
    ka = _interleave_maps(w_in[:, a_qk:2 * a_qk])
    va = w_in[:, 2 * a_qk:2 * a_qk + a_v]
    o = 2 * a_qk + a_v
    qb = w_in[:, o:o + b_w] * (B_DH ** -0.5)
    rest = w_in[:, o + b_w:]
    return jnp.concatenate([qa, ka, va, qb, rest], axis=1).astype(BF16)


def _gla_weights(w_in):
    d = w_in.shape[0]
    n_main = 2 * C_HEADS * C_DK + 2 * C_HEADS * C_DV
    pad = jnp.zeros((d, LANES - 2 * C_RANK), w_in.dtype)
    return jnp.concatenate([w_in, pad], axis=1).astype(BF16), n_main


def kernel(x, c, ctx, c_ctx, w_mod, b_mod, norm_g, final_g, att_w_in, att_w_out, att_lambda, att_subln_g, na_bias,
           gla_w_in, gla_w_gate, gla_b_gate, gla_norm_g, gla_w_out, router_w, router_bias, moe_w1, moe_w3, moe_w2):
    b, seq, d = x.shape
    n = N_CTX + seq
    s = (ctx, x)

    rows = b + 1
    rows_pad = -(-rows // 8) * 8
    cc = jnp.concatenate([c, c_ctx[None, :], jnp.zeros((rows_pad - rows, d), F32)], axis=0)
    mod_all = _mod_vectors(cc, w_mod, b_mod)
    mod_x = mod_all[:, :b].reshape(DEPTH, b, 1, 6, d)
    mod_c = jnp.broadcast_to(mod_all[:, b].reshape(DEPTH, 1, 1, 6, d), (DEPTH, b, 1, 6, d))
    mods = jnp.concatenate([mod_c, mod_x], axis=2)

    cos, sin = _rope_tables(seq)
    rw_hi = router_w.astype(BF16)
    rw_lo = (router_w - rw_hi.astype(F32)).astype(BF16)
    zpad = jnp.zeros((d, LANES - 2 * N_EXPERTS), BF16)
    rw = jnp.concatenate([rw_hi, rw_lo, zpad], axis=1)

    y = None
    order_after = router_bias
    for i in range(DEPTH):
        j = i // 2
        modp = mods[i - 1] if i else None
        if i % 2 == 0:
            lam_init = 0.8 - 0.6 * math.exp(-0.3 * i)
            w = _att_weights(att_w_in[j])
            outs = _project(s, y, modp, mods[i], norm_g[i, 0], w, cos, sin,
                            n_rope=2 * A_HEADS * LANES, n_bf16=w.shape[1])
            s = outs[0]
            p = outs[-1]
            oa_x, oa_c = _diff_attention(p, att_lambda[j], att_subln_g[j], lam_init)
            ob_x, ob_c = _neighbourhood_attention(p, _na_bias_table(na_bias[j], seq // GRID_W))
            s, h2, logits = _out_project((oa_x, oa_c, ob_x, ob_c), att_w_out[j].astype(BF16), s, mods[i], norm_g[i, 1],
                                         rw, gla=False)
        else:
            w, n_main = _gla_weights(gla_w_in[j])
            outs = _project(s, y, modp, mods[i], norm_g[i, 0], w, None, None, n_rope=0, n_bf16=n_main)
            s, p, lr = outs
            wg = jnp.zeros((2, LANES, C_HEADS * C_DK), F32)
            wg = wg.at[0, :C_RANK].set(gla_w_gate[j, 0]).at[1, C_RANK:2 * C_RANK].set(gla_w_gate[j, 1])
            o = _gla(p, lr, wg, gla_b_gate[j])
            s, h2, logits = _out_project((o, p), gla_w_out[j].astype(BF16), s, mods[i], norm_g[i, 1],
                                         rw, gla=True, gn=gla_norm_g[j])
        if i == 0:
            expert_w = _cast_expert_weights(0, moe_w1, moe_w3, moe_w2, order_after)
        y, order_after = _moe(h2.reshape(b * n, ROW_SEGS, LANES), logits.reshape(b * n, LANES), router_bias, *expert_w)
        y = y.reshape(b * n * ROW_SEGS, LANES)
        if i + 1 < DEPTH:
            expert_w = _cast_expert_weights(i + 1, moe_w1, moe_w3, moe_w2, order_after)
    return _final(s, y, mods[DEPTH - 1], final_g)
```

```python
import functools
import math

import jax
import jax.numpy as jnp
import numpy as np
from jax import lax
from jax.experimental import pallas as pl
from jax.experimental.pallas import tpu as pltpu
from jax.experimental.pallas import tpu_sc as plsc

F32 = jnp.float32
BF16 = jnp.bfloat16

D_MODEL = 1024
DEPTH = 4
GRID_W = 64
N_CTX = 256
A_HEADS = 4
A_DH = 64
A_DV = 128
B_HEADS = 8
B_DH = 64
NA_ROWS = 8
NA_COLS = 16
C_HEADS = 4
C_DK = 128
C_DV = 256
C_RANK = 16
C_TAU = 16.0
C_CHUNK = 64
GLA_BLOCK = 256
N_EXPERTS = 16
N_GROUPS = 4
PER_GROUP = 4
ROPE_BASE = 10000.0
EPS = 1e-6
LOG2_E = math.log2(math.e)
ATT_KEY_BLOCK = 1152
ATT_Q_HALVES = 4

LANES = 128
ROW_TILE = 256
MOE_TILE = 256
SC_CORES = 2
SC_SUBCORES = 16
SC_WORKERS = SC_CORES * SC_SUBCORES
SC_GATHER_ROWS = 64
N_CLASSES = N_GROUPS * 6
NA_QROWS = 16
NA_SUB = 2
NA_KROWS = NA_SUB + NA_ROWS - 1
VMEM_LIMIT = 52 * 1024 * 1024
ROW_SEGS = D_MODEL // LANES // 2

PAIR_LO = (0, 0, 0, 1, 1, 2)
PAIR_HI = (1, 2, 3, 2, 3, 3)


def _cparams(sem):
    return pltpu.CompilerParams(dimension_semantics=sem, vmem_limit_bytes=VMEM_LIMIT)


def _sigmoid(x):
    return 1.0 / (1.0 + jnp.exp(-x))


def _silu(x):
    return x * _sigmoid(x)


def _rms(x, g):
    return x * lax.rsqrt(jnp.mean(x * x, axis=-1, keepdims=True) + EPS) * g


def _dot(a, b):
    return jnp.dot(a, b, preferred_element_type=F32)


def _dot_nt(a, b):
    return lax.dot_general(a, b, (((1,), (1,)), ((), ())), preferred_element_type=F32)


def _dot_tn(a, b):
    return lax.dot_general(a, b, (((0,), (0,)), ((), ())), preferred_element_type=F32)


def _mod_kernel(c_ref, w_ref, b_ref, o_ref):
    a = _silu(c_ref[...]).astype(BF16)
    o_ref[0] = _dot(a, w_ref[0].astype(BF16)) + b_ref[0]


def _mod_vectors(cc, w_mod, b_mod):
    depth, d, n6 = w_mod.shape
    rows = cc.shape[0]
    tn = 1536
    return pl.pallas_call(
        _mod_kernel,
        out_shape=jax.ShapeDtypeStruct((depth, rows, n6), F32),
        grid=(depth, n6 // tn),
        in_specs=[
            pl.BlockSpec((rows, d), lambda i, j: (0, 0)),
            pl.BlockSpec((1, d, tn), lambda i, j: (i, 0, j)),
            pl.BlockSpec((1, 1, tn), lambda i, j: (i, 0, j)),
        ],
        out_specs=pl.BlockSpec((1, rows, tn), lambda i, j: (i, 0, j)),
        compiler_params=_cparams(("arbitrary", "arbitrary")),
        name="mod_vectors",
    )(cc, w_mod, b_mod.reshape(depth, 1, n6))


def _token_rows(ref):
    rows = ref.shape[0] // ROW_SEGS
    packed = jnp.concatenate([ref[pl.ds(sg, rows, stride=ROW_SEGS), :] for sg in range(ROW_SEGS)], axis=-1)
    low = lax.bitcast_convert_type(lax.shift_left(packed, jnp.int32(16)), F32)
    high = lax.bitcast_convert_type(packed & jnp.int32(-65536), F32)
    return jnp.concatenate([low, high], axis=-1)


def _store_token_rows(ref, val):
    rows, d = val.shape
    rounded = val.astype(BF16).astype(F32)
    bits = lax.bitcast_convert_type(rounded, jnp.int32)
    packed = lax.shift_right_logical(bits[:, :d // 2], jnp.int32(16)) | bits[:, d // 2:]
    for sg in range(ROW_SEGS):
        ref[pl.ds(sg, rows, stride=ROW_SEGS), :] = packed[:, sg * LANES:(sg + 1) * LANES]


def _proj_kernel(*refs, has_y, first, n_rope, n_bf16, col_chunk):
    it = iter(refs)
    s_ref = next(it)
    x_ref = next(it) if first else None
    y_ref = next(it) if has_y else None
    modp_ref = next(it) if has_y else None
    mod_ref = next(it)
    ng_ref = next(it)
    w_ref = next(it)
    cos_ref = next(it) if n_rope else None
    sin_ref = next(it) if n_rope else None
    snew_ref = next(it) if has_y or first else None
    p_ref = next(it)
    lr_ref = next(it) if w_ref.shape[1] > n_bf16 else None

    x = s_ref[0]
    if first:
        x = jnp.where(pl.program_id(1) == 0, x, x_ref[0])
        snew_ref[0] = x
    if has_y:
        x = x + modp_ref[0, 0, 5:6, :] * _token_rows(y_ref)
        snew_ref[0] = x
    h = _rms(x, ng_ref[...])
    h = h * (1.0 + mod_ref[0, 0, 1:2, :]) + mod_ref[0, 0, 0:1, :]
    hb = h.astype(BF16)
    n_out = w_ref.shape[1]
    for c0 in range(0, n_out, col_chunk):
        c1 = min(c0 + col_chunk, n_out)
        acc = _dot(hb, w_ref[:, c0:c1])
        for b0 in range(c0, c1, LANES):
            t = acc[:, b0 - c0:b0 - c0 + LANES]
            if b0 < n_rope:
                t = t * cos_ref[...] + pltpu.roll(t, LANES // 2, 1) * sin_ref[...]
            if b0 < n_bf16:
                p_ref[0, :, b0:b0 + LANES] = t.astype(BF16)
            else:
                lr_ref[0, :, b0 - n_bf16:b0 - n_bf16 + LANES] = t


def _project(s, y, modp, mod, ng, w, cos, sin, *, n_rope, n_bf16):
    first = isinstance(s, tuple)
    tm = ROW_TILE
    row_spec = pl.BlockSpec((1, tm, s[0].shape[2] if first else s.shape[2]), lambda i, j: (i, j, 0))
    if first:
        ctx, x = s
        b, n, d = x.shape[0], ctx.shape[1] + x.shape[1], x.shape[2]
        in_specs = [pl.BlockSpec((1, tm, d), lambda i, j: (i, 0, 0)),
                    pl.BlockSpec((1, tm, d), lambda i, j: (i, jnp.maximum(j - 1, 0), 0))]
        args = [ctx, x]
    else:
        b, n, d = s.shape
        in_specs = [row_spec]
        args = [s]
    n_out = w.shape[1]
    has_y = y is not None
    mod_spec = pl.BlockSpec((1, 1, 6, d), lambda i, j: (i, jnp.minimum(j, 1), 0, 0))
    if has_y:
        in_specs += [pl.BlockSpec((tm * ROW_SEGS, LANES), lambda i, j: (i * (n // tm) + j, 0)), mod_spec]
        args += [y, modp]
    in_specs += [mod_spec, pl.BlockSpec((1, d), lambda i, j: (0, 0)),
                 pl.BlockSpec((d, n_out), lambda i, j: (0, 0))]
    args += [mod, ng.reshape(1, d), w]
    if n_rope:
        tab_spec = pl.BlockSpec((tm, LANES), lambda i, j: (j, 0))
        in_specs += [tab_spec, tab_spec]
        args += [cos, sin]
    out_shape, out_specs = [], []
    if has_y or first:
        out_shape.append(jax.ShapeDtypeStruct((b, n, d), F32))
        out_specs.append(row_spec)
    out_shape.append(jax.ShapeDtypeStruct((b, n, n_bf16), BF16))
    out_specs.append(pl.BlockSpec((1, tm, n_bf16), lambda i, j: (i, j, 0)))
    if n_out > n_bf16:
        out_shape.append(jax.ShapeDtypeStruct((b, n, n_out - n_bf16), F32))
        out_specs.append(pl.BlockSpec((1, tm, n_out - n_bf16), lambda i, j: (i, j, 0)))
    return pl.pallas_call(
        functools.partial(_proj_kernel, has_y=has_y, first=first, n_rope=n_rope, n_bf16=n_bf16, col_chunk=512),
        out_shape=out_shape,
        grid=(b, n // tm),
        in_specs=in_specs,
        out_specs=out_specs,
        compiler_params=_cparams(("arbitrary", "arbitrary")),
        name="norm_mod_project",
    )(*args)


def _softmax_rows(s):
    e = jnp.exp(s - jnp.max(s, axis=-1, keepdims=True))
    return e * (1.0 / jnp.sum(e, axis=-1, keepdims=True))


def _diff_attn_kernel(*refs, lam_init, n_q):
    q_refs = refs[:n_q]
    k_ref, v_ref, lam_ref, g_ref, o_ref = refs[n_q:]
    lp = lam_ref[...]
    lam = (jnp.exp(jnp.sum(lp[0:1] * lp[1:2], axis=-1, keepdims=True))
           - jnp.exp(jnp.sum(lp[2:3] * lp[3:4], axis=-1, keepdims=True)) + lam_init)
    q = jnp.concatenate([r[0] for r in q_refs], axis=0)
    lane = lax.broadcasted_iota(jnp.int32, (1, LANES), 1)
    first_map = (lane // 32) % 2 == 0
    zero = jnp.zeros_like(q)
    q0 = jnp.where(first_map, q, zero)
    q1 = jnp.where(first_map, zero, q)
    n_keys = k_ref.shape[1]
    blk = min(ATT_KEY_BLOCK, n_keys)
    ones = jnp.ones((blk, A_DV), BF16)
    state = [None, None]
    for kb in range(n_keys // blk):
        k = k_ref[0, kb * blk:(kb + 1) * blk, :]
        v1 = jnp.concatenate([v_ref[0, kb * blk:(kb + 1) * blk, :], ones], axis=1)
        for i, qm in enumerate((q0, q1)):
            s = _dot_nt(qm, k)
            m_new = jnp.max(s, axis=-1, keepdims=True)
            if kb:
                m_old, acc_old = state[i]
                m_new = jnp.maximum(m_old, m_new)
            acc = _dot(jnp.exp2(s - m_new).astype(BF16), v1)
            if kb:
                acc = acc_old * jnp.exp2(m_old - m_new) + acc
            state[i] = (m_new, acc)
    acc0, acc1 = state[0][1], state[1][1]
    o = acc0[:, :A_DV] * (1.0 / acc0[:, A_DV:]) - acc1[:, :A_DV] * (lam / acc1[:, A_DV:])
    o_ref[0] = (_rms(o, g_ref[...]) * (1.0 - lam_init)).astype(BF16)


def _diff_attention(p, lam_p, subln_g, lam_init):
    b, n, _ = p.shape
    half = ROW_TILE
    tq = ATT_Q_HALVES * half
    ctx_blocks = N_CTX // half
    width = A_HEADS * A_DV
    small = [pl.BlockSpec((4, A_DH), lambda i, h, *_: (0, 0)), pl.BlockSpec((1, A_DV), lambda i, h, *_: (0, 0))]
    q_specs = [pl.BlockSpec((1, half, LANES), functools.partial(lambda i, h, j, part: (i, ctx_blocks + ATT_Q_HALVES * j + part, h),
                                                                 part=part)) for part in range(ATT_Q_HALVES)]
    oa_x = pl.pallas_call(
        functools.partial(_diff_attn_kernel, lam_init=lam_init, n_q=ATT_Q_HALVES),
        out_shape=jax.ShapeDtypeStruct((b, n - N_CTX, width), BF16),
        grid=(b, A_HEADS, (n - N_CTX) // tq),
        in_specs=q_specs + [
            pl.BlockSpec((1, n, LANES), lambda i, h, j: (i, 0, A_HEADS + h)),
            pl.BlockSpec((1, n, LANES), lambda i, h, j: (i, 0, 2 * A_HEADS + h)),
        ] + small,
        out_specs=pl.BlockSpec((1, tq, LANES), lambda i, h, j: (i, j, h)),
        compiler_params=_cparams(("arbitrary", "arbitrary", "arbitrary")),
        name="diff_attention",
    )(*([p] * ATT_Q_HALVES), p, p, lam_p, subln_g.reshape(1, A_DV))
    def all_heads(*refs):
        q_ref, k_ref, v_ref, lam_ref, g_ref, o_ref = refs
        for h in range(A_HEADS):
            cols = pl.ds(h * LANES, LANES)
            _diff_attn_kernel(q_ref.at[:, :, cols], k_ref.at[:, :, cols], v_ref.at[:, :, cols], lam_ref, g_ref,
                              o_ref.at[:, :, cols], lam_init=lam_init, n_q=1)

    oa_c = pl.pallas_call(
        all_heads,
        out_shape=jax.ShapeDtypeStruct((b, N_CTX, width), BF16),
        grid=(b,),
        in_specs=[pl.BlockSpec((1, N_CTX, width), functools.partial(lambda i, part: (i, 0, part), part=part))
                  for part in range(3)] + [pl.BlockSpec((4, A_DH), lambda i: (0, 0)),
                                           pl.BlockSpec((1, A_DV), lambda i: (0, 0))],
        out_specs=pl.BlockSpec((1, N_CTX, width), lambda i: (i, 0, 0)),
        compiler_params=_cparams(("arbitrary",)),
        name="context_diff_attention",
    )(p, p, p, lam_p, subln_g.reshape(1, A_DV))
    return oa_x, oa_c


def _split_heads(q):
    lane = lax.broadcasted_iota(jnp.int32, (1, LANES), 1)
    first_head = lane < B_DH
    zero = jnp.zeros_like(q)
    return first_head, (jnp.where(first_head, q, zero), jnp.where(first_head, zero, q))


def _na_ctx_kernel(q_ref, k_ref, v_ref, o_ref):
    for pair in range(q_ref.shape[2] // LANES):
        cols = slice(pair * LANES, (pair + 1) * LANES)
        first_head, qh = _split_heads(q_ref[0, :, cols])
        kc = k_ref[0, :, cols]
        vc = v_ref[0, :, cols]
        outs = [_dot(_softmax_rows(_dot_nt(qh[h], kc)).astype(BF16), vc) for h in range(2)]
        o_ref[0, :, cols] = jnp.where(first_head, outs[0], outs[1]).astype(BF16)


def _na_kernel(*refs):
    q_refs = refs[:-4]
    k_ref, v_ref, bm_ref, o_ref = refs[-4:]
    blk = pl.program_id(2)
    rows = (k_ref.shape[1] - N_CTX) // GRID_W
    sub_q = NA_SUB * GRID_W
    sub_k = NA_KROWS * GRID_W
    kc = k_ref[0, 0:N_CTX, :]
    vc = jnp.concatenate([v_ref[0, 0:N_CTX, :], jnp.ones((N_CTX, LANES), BF16)], axis=1)
    ones = jnp.ones((sub_k, LANES), BF16)
    for sub in range(NA_QROWS // NA_SUB):
        q_ref = q_refs[sub * sub_q // ROW_TILE]
        q0 = (sub * sub_q) % ROW_TILE
        first_head, qh = _split_heads(q_ref[0, q0:q0 + sub_q, :])
        k_row0 = jnp.clip(blk * NA_QROWS + sub * NA_SUB - NA_ROWS // 2, 0, rows - NA_KROWS)
        start = pl.multiple_of(N_CTX + k_row0 * GRID_W, GRID_W)
        kw = k_ref[0, pl.ds(start, sub_k), :]
        vw = jnp.concatenate([v_ref[0, pl.ds(start, sub_k), :], ones], axis=1)
        outs = []
        for h in range(2):
            s_loc = _dot_nt(qh[h], kw) + bm_ref[h, 0, sub]
            s_ctx = _dot_nt(qh[h], kc)
            m = jnp.maximum(jnp.max(s_loc, axis=-1, keepdims=True), jnp.max(s_ctx, axis=-1, keepdims=True))
            acc = _dot(jnp.exp(s_loc - m).astype(BF16), vw) + _dot(jnp.exp(s_ctx - m).astype(BF16), vc)
            outs.append(acc[:, :LANES] * (1.0 / acc[:, LANES:]))
        o_ref[0, sub * sub_q:(sub + 1) * sub_q, :] = jnp.where(first_head, outs[0], outs[1]).astype(BF16)


def _na_bias_table(na_bias, rows):
    h = na_bias.shape[0]
    n_dr, n_dc = 2 * NA_ROWS - 1, 2 * NA_COLS - 1
    width = 2 * GRID_W
    left = GRID_W - NA_COLS
    u = jnp.pad(na_bias, ((0, 0), (0, 0), (left, width - left - n_dc)))
    skew = jnp.tile(u, (1, 1, GRID_W))[:, :, :GRID_W * (width - 1)].reshape(h, n_dr, GRID_W, width - 1)
    toeplitz = skew[:, :, :, GRID_W - 1:]
    margin = NA_KROWS - NA_ROWS
    by_col = jnp.pad(toeplitz.transpose(0, 2, 1, 3), ((0, 0), (0, 0), (margin, margin), (0, 0)))
    by_col = by_col.reshape(h, GRID_W, (n_dr + 2 * margin) * GRID_W)
    col = np.arange(GRID_W)
    col_start = np.clip(col - NA_COLS // 2, 0, GRID_W - NA_COLS)
    col_ok = (col[None, :] >= col_start[:, None]) & (col[None, :] < col_start[:, None] + NA_COLS)
    n_sub = NA_QROWS // NA_SUB
    runs = []
    for q_row0 in (0, NA_QROWS, rows - NA_QROWS):
        for sub in range(n_sub):
            k_row0 = int(np.clip(q_row0 + sub * NA_SUB - NA_ROWS // 2, 0, rows - NA_KROWS))
            kr = k_row0 + np.arange(NA_KROWS)
            firsts, oks = [], []
            for rq in range(NA_SUB):
                r = q_row0 + sub * NA_SUB + rq
                r0 = int(np.clip(r - NA_ROWS // 2, 0, rows - NA_ROWS))
                row_ok = (kr >= r0) & (kr < r0 + NA_ROWS)
                oks.append((col_ok[:, None, :] & row_ok[None, :, None]).reshape(GRID_W, NA_KROWS * GRID_W))
                firsts.append(k_row0 - r + NA_ROWS - 1 + margin)
            key = (tuple(firsts), np.stack(oks).tobytes())
            if runs and runs[-1][0] == key:
                runs[-1][2] += 1
                continue
            windows = [by_col[:, :, f * GRID_W:(f + NA_KROWS) * GRID_W] for f in firsts]
            block = jnp.where(np.concatenate(oks, axis=0)[None], jnp.concatenate(windows, axis=1), -jnp.inf)
            runs.append([key, block, 1])
    table = jnp.concatenate([jnp.broadcast_to(blk[:, None], (h, rep) + blk.shape[1:]) for _, blk, rep in runs], axis=1)
    return table.reshape(h, 3, n_sub, NA_SUB * GRID_W, NA_KROWS * GRID_W)


def _neighbourhood_attention(p, bm):
    b, n, _ = p.shape
    tq = NA_QROWS * GRID_W
    parts = tq // ROW_TILE
    n_blocks = (n - N_CTX) // tq
    ctx_blocks = N_CTX // ROW_TILE
    col0 = 3 * A_HEADS
    pairs = B_HEADS // 2
    width = B_HEADS * B_DH

    def pattern(j):
        return jnp.minimum(j, 1) + (j == n_blocks - 1).astype(jnp.int32)

    ob_x = pl.pallas_call(
        _na_kernel,
        out_shape=jax.ShapeDtypeStruct((b, n - N_CTX, width), BF16),
        grid=(b, pairs, n_blocks),
        in_specs=[pl.BlockSpec((1, ROW_TILE, LANES),
                               functools.partial(lambda i, h, j, part: (i, ctx_blocks + parts * j + part, col0 + h),
                                                 part=part)) for part in range(parts)] + [
            pl.BlockSpec((1, n, LANES), lambda i, h, j: (i, 0, col0 + pairs + h)),
            pl.BlockSpec((1, n, LANES), lambda i, h, j: (i, 0, col0 + 2 * pairs + h)),
            pl.BlockSpec((2, 1, NA_QROWS // NA_SUB, NA_SUB * GRID_W, NA_KROWS * GRID_W),
                         lambda i, h, j: (h, pattern(j), 0, 0, 0)),
        ],
        out_specs=pl.BlockSpec((1, tq, LANES), lambda i, h, j: (i, j, h)),
        compiler_params=_cparams(("arbitrary", "arbitrary", "arbitrary")),
        name="neighbourhood_attention",
    )(*([p] * parts), p, p, bm)
    ob_c = pl.pallas_call(
        _na_ctx_kernel,
        out_shape=jax.ShapeDtypeStruct((b, N_CTX, width), BF16),
        grid=(b,),
        in_specs=[pl.BlockSpec((1, N_CTX, width), functools.partial(lambda i, part: (i, 0, col0 // pairs + part), part=part))
                  for part in range(3)],
        out_specs=pl.BlockSpec((1, N_CTX, width), lambda i: (i, 0, 0)),
        compiler_params=_cparams(("arbitrary",)),
        name="context_attention",
    )(p, p, p)
    return ob_x, ob_c


def _gla_kernel(q_ref, k_ref, v_ref, lr_ref, wg_ref, bg_ref, o_ref, ob_ref, g_ref, sf_ref, sb_ref):
    n = q_ref.shape[1]
    blk = GLA_BLOCK
    per_blk = blk // C_CHUNK
    n_blocks = n // blk
    ri = lax.broadcasted_iota(jnp.int32, (blk, blk), 0)
    ci = lax.broadcasted_iota(jnp.int32, (blk, blk), 1)
    same_chunk = (ri // C_CHUNK) == (ci // C_CHUNK)
    keeps = (same_chunk & (ci <= ri), same_chunk & (ci >= ri))
    tris = tuple(jnp.where(kp, 1.0, 0.0).astype(BF16) for kp in keeps)
    w_gate = jnp.concatenate([wg_ref[0], wg_ref[1]], axis=1).astype(BF16)
    b_gate = jnp.concatenate([bg_ref[0:1, :], bg_ref[1:2, :]], axis=1)
    for i in range(n_blocks):
        z = _dot(lr_ref[0, i * blk:(i + 1) * blk, :].astype(BF16), w_gate) + b_gate
        g_ref[i * blk:(i + 1) * blk, :] = (jnp.minimum(z, 0.0) - jnp.log(1.0 + jnp.exp(-jnp.abs(z)))) * (1.0 / C_TAU)

    def block(sb, direction, st_ref):
        keep = keeps[direction]
        end_row = C_CHUNK - 1 if direction == 0 else 0
        mid_row = C_CHUNK // 2 - 1 if direction == 0 else C_CHUNK // 2
        r0 = sb * blk
        g = g_ref[r0:r0 + blk, direction * C_DK:(direction + 1) * C_DK]
        g_hi = g.astype(BF16)
        g_lo = (g - g_hi.astype(F32)).astype(BF16)
        gc2 = _dot(tris[direction], jnp.concatenate([g_hi, g_lo], axis=1))
        gc = (gc2[:, :C_DK] + gc2[:, C_DK:]).reshape(per_blk, C_CHUNK, C_DK)
        g_end = gc[:, end_row:end_row + 1, :]
        g_mid = gc[:, mid_row:mid_row + 1, :]
        q = (q_ref[0, r0:r0 + blk, :].astype(F32) * (C_DK ** -0.5)).reshape(per_blk, C_CHUNK, C_DK)
        k = k_ref[0, r0:r0 + blk, :].astype(F32).reshape(per_blk, C_CHUNK, C_DK)
        v = v_ref[0, r0:r0 + blk, :]
        q_in = (q * jnp.exp(gc)).astype(BF16).reshape(blk, C_DK)
        q_mid = (q * jnp.exp(gc - g_mid)).astype(BF16).reshape(blk, C_DK)
        k_mid = (k * jnp.exp(g_mid - gc)).astype(BF16).reshape(blk, C_DK)
        k_end = (k * jnp.exp(g_end - gc)).astype(BF16).reshape(blk, C_DK)
        a = jnp.where(keep, _dot_nt(q_mid, k_mid), 0.0)
        o_intra = _dot(a.astype(BF16), v)
        ends = jnp.concatenate([g_end.reshape(per_blk, C_DK), jnp.zeros((8 - per_blk, C_DK), F32)], axis=0)
        decay = jnp.transpose(jnp.exp(ends))
        o_inter = [None] * per_blk
        for c in (range(per_blk) if direction == 0 else reversed(range(per_blk))):
            rows = slice(c * C_CHUNK, (c + 1) * C_CHUNK)
            st = st_ref[...]
            o_inter[c] = _dot(q_in[rows], st.astype(BF16))
            st_ref[...] = st * decay[:, c:c + 1] + _dot_tn(k_end[rows], v[rows])
        return o_intra + jnp.concatenate(o_inter, axis=0)

    sf_ref[...] = jnp.zeros_like(sf_ref)
    sb_ref[...] = jnp.zeros_like(sb_ref)
    ctx_blocks = N_CTX // blk
    order_b = list(reversed(range(ctx_blocks))) + list(reversed(range(ctx_blocks, n_blocks)))
    for i in range(n_blocks):
        o_ref[0, i * blk:(i + 1) * blk, :] = block(i, 0, sf_ref)
        sb = order_b[i]
        ob_ref[sb * blk:(sb + 1) * blk, :] = block(sb, 1, sb_ref)
    o_ref[0] = o_ref[0] + ob_ref[...]


def _gla(p, lr, wg, bg):
    b, n, _ = p.shape
    return pl.pallas_call(
        _gla_kernel,
        out_shape=jax.ShapeDtypeStruct((b, n, C_HEADS * C_DV), F32),
        grid=(b, C_HEADS),
        in_specs=[
            pl.BlockSpec((1, n, C_DK), lambda i, h: (i, 0, h)),
            pl.BlockSpec((1, n, C_DK), lambda i, h: (i, 0, C_HEADS + h)),
            pl.BlockSpec((1, n, C_DV), lambda i, h: (i, 0, C_HEADS + h)),
            pl.BlockSpec((1, n, LANES), lambda i, h: (i, 0, 0)),
            pl.BlockSpec((2, LANES, C_DK), lambda i, h: (0, 0, h)),
            pl.BlockSpec((2, C_DK), lambda i, h: (0, h)),
        ],
        out_specs=pl.BlockSpec((1, n, C_DV), lambda i, h: (i, 0, h)),
        scratch_shapes=[pltpu.VMEM((n, C_DV), F32), pltpu.VMEM((n, 2 * C_DK), F32),
                        pltpu.VMEM((C_DK, C_DV), F32), pltpu.VMEM((C_DK, C_DV), F32)],
        compiler_params=_cparams(("arbitrary", "arbitrary")),
        name="gla",
    )(p, p, p, lr, wg, bg)


def _out_kernel(*refs, gla):
    if gla:
        o_ref, gate_ref, gn_ref, w_ref, s_ref, mod_ref, ng_ref, rw_ref, snew_ref, h2_ref, lg_ref = refs
        o = o_ref[0]
        gate = gate_ref[0].astype(F32)
        parts = []
        for hd in range(C_HEADS):
            oh = _rms(o[:, hd * C_DV:(hd + 1) * C_DV], gn_ref[...])
            parts.append((oh * _silu(gate[:, hd * C_DV:(hd + 1) * C_DV])).astype(BF16))
        acc = _dot(jnp.concatenate(parts, axis=-1), w_ref[...])
    else:
        oax_ref, oac_ref, obx_ref, obc_ref, w_ref, s_ref, mod_ref, ng_ref, rw_ref, snew_ref, h2_ref, lg_ref = refs
        half = oax_ref.shape[2]
        is_ctx = pl.program_id(1) == 0
        oa = jnp.where(is_ctx, oac_ref[0], oax_ref[0])
        ob = jnp.where(is_ctx, obc_ref[0], obx_ref[0])
        acc = _dot(oa, w_ref[0:half, :]) + _dot(ob, w_ref[half:, :])
    x = s_ref[0] + mod_ref[0, 0, 2:3, :] * acc
    snew_ref[0] = x
    h2 = _rms(x, ng_ref[...]) * (1.0 + mod_ref[0, 0, 4:5, :]) + mod_ref[0, 0, 3:4, :]
    _store_token_rows(h2_ref, h2)
    t = _dot(h2.astype(BF16), rw_ref[...])
    lg_ref[0] = t + pltpu.roll(t, LANES - N_EXPERTS, 1)


def _out_project(mix, w_out, s, mod, ng, rw, *, gla, gn=None):
    b, n, d = s.shape
    tm = ROW_TILE
    row_spec = pl.BlockSpec((1, tm, d), lambda i, j: (i, j, 0))
    const2 = lambda i, j: (0, 0)
    if gla:
        o, p = mix
        in_specs = [row_spec, pl.BlockSpec((1, tm, d), lambda i, j: (i, j, 2)),
                    pl.BlockSpec((1, C_DV), const2)]
        args = [o, p, gn.reshape(1, C_DV)]
    else:
        oa_x, oa_c, ob_x, ob_c = mix
        half = oa_x.shape[2]
        latent = pl.BlockSpec((1, tm, half), lambda i, j: (i, jnp.maximum(j - 1, 0), 0))
        context = pl.BlockSpec((1, tm, half), lambda i, j: (i, 0, 0))
        in_specs = [latent, context, latent, context]
        args = [oa_x, oa_c, ob_x, ob_c]
    in_specs += [pl.BlockSpec((d, d), const2), row_spec,
                 pl.BlockSpec((1, 1, 6, d), lambda i, j: (i, jnp.minimum(j, 1), 0, 0)),
                 pl.BlockSpec((1, d), const2), pl.BlockSpec((d, LANES), const2)]
    args += [w_out, s, mod, ng.reshape(1, d), rw]
    return pl.pallas_call(
        functools.partial(_out_kernel, gla=gla),
        out_shape=[jax.ShapeDtypeStruct((b, n, d), F32), jax.ShapeDtypeStruct((b * n * ROW_SEGS, LANES), jnp.int32),
                   jax.ShapeDtypeStruct((b, n, LANES), F32)],
        grid=(b, n // tm),
        in_specs=in_specs,
        out_specs=[row_spec, pl.BlockSpec((tm * ROW_SEGS, LANES), lambda i, j: (i * (n // tm) + j, 0)),
                   pl.BlockSpec((1, tm, LANES), lambda i, j: (i, j, 0))],
        compiler_params=_cparams(("arbitrary", "arbitrary")),
        name="out_project",
    )(*args)


def _route_kernel(bias_ref, lg_ref, cls_ref, gate_ref, rank_ref, cnt_ref):
    score = [_sigmoid(lg_ref[e]) for e in range(N_EXPERTS)]
    sel = [score[e] + bias_ref[e] for e in range(N_EXPERTS)]
    grp_score = []
    for g in range(N_GROUPS):
        v = sel[g * PER_GROUP:(g + 1) * PER_GROUP]
        best = v[0] + v[1]
        for a in range(PER_GROUP):
            for c in range(a + 1, PER_GROUP):
                if (a, c) != (0, 1):
                    best = jnp.maximum(best, v[a] + v[c])
        grp_score.append(best)
    grp = jnp.zeros(grp_score[0].shape, jnp.int32)
    best = grp_score[0]
    for g in range(1, N_GROUPS):
        upd = grp_score[g] > best
        best = jnp.where(upd, grp_score[g], best)
        grp = jnp.where(upd, g, grp)

    def pick(vals, j):
        out = vals[j]
        for g in range(1, N_GROUPS):
            out = jnp.where(grp == g, vals[g * PER_GROUP + j], out)
        return out

    v = [pick(sel, j) for j in range(PER_GROUP)]
    sc = [pick(score, j) for j in range(PER_GROUP)]
    one = jnp.ones(grp.shape, jnp.int32)
    zero = jnp.zeros(grp.shape, jnp.int32)
    chosen = []
    for j in range(PER_GROUP):
        rank = zero
        for m in range(PER_GROUP):
            if m == j:
                continue
            ahead = (v[m] >= v[j]) if m < j else (v[m] > v[j])
            rank = rank + jnp.where(ahead, one, zero)
        chosen.append(rank < 2)
    code = zero
    for j in range(PER_GROUP):
        code = code + jnp.where(chosen[j], one * (1 << j), zero)
    pair = zero
    for idx in range(6):
        pair = jnp.where(code == (1 << PAIR_LO[idx]) + (1 << PAIR_HI[idx]), idx, pair)
    s_lo = jnp.where(chosen[0], sc[0], jnp.where(chosen[1], sc[1], sc[2]))
    s_hi = jnp.where(chosen[3], sc[3], jnp.where(chosen[2], sc[2], sc[1]))
    den = s_lo + s_hi
    cls = grp * 6 + pair
    cls_ref[...] = cls
    w_lo = s_lo / den
    w_hi = s_hi / den
    gate_ref[...] = jnp.zeros_like(gate_ref)
    filler = jnp.zeros((6, LANES), F32)
    for row in range(cls.shape[0]):
        pair_rows = jnp.concatenate([w_lo[row:row + 1], w_hi[row:row + 1], filler], axis=0)
        gate_ref[row * LANES:(row + 1) * LANES, 0:8] = jnp.transpose(pair_rows)
    r = cls.shape[0]
    lane = lax.broadcasted_iota(jnp.int32, (1, LANES), 1)
    li = lax.broadcasted_iota(jnp.int32, (LANES, LANES), 0)
    lj = lax.broadcasted_iota(jnp.int32, (LANES, LANES), 1)
    upto_lane = jnp.where(li <= lj, 1.0, 0.0).astype(BF16)
    ri = lax.broadcasted_iota(jnp.int32, (r, r), 0)
    rj = lax.broadcasted_iota(jnp.int32, (r, r), 1)
    earlier_rows = jnp.where(rj < ri, 1.0, 0.0).astype(BF16)
    member = [jnp.where(cls == c, 1.0, 0.0) for c in range(N_CLASSES)]
    row_totals = jnp.zeros((r, LANES), F32)
    for c in range(N_CLASSES):
        row_totals = row_totals + jnp.where(lane == c, jnp.sum(member[c], axis=1, keepdims=True), 0.0)
    before_row = _dot(earlier_rows, row_totals.astype(BF16))
    rank = jnp.zeros((r, LANES), F32)
    for c in range(N_CLASSES):
        in_row = _dot(member[c].astype(BF16), upto_lane)
        offset = jnp.sum(jnp.where(lane == c, before_row, 0.0), axis=1, keepdims=True)
        rank = rank + member[c] * (in_row + offset - 1.0)
    rank_ref[...] = rank.astype(jnp.int32)
    counts = before_row[r - 1:r, :] + row_totals[r - 1:r, :]
    cnt_ref[...] = jnp.broadcast_to(counts, cnt_ref.shape).astype(jnp.int32)


def _route(logits_t, router_bias):
    _, r, _ = logits_t.shape
    full = pl.BlockSpec((r, LANES), lambda i: (0, 0))
    return pl.pallas_call(
        _route_kernel,
        out_shape=[jax.ShapeDtypeStruct((r, LANES), jnp.int32), jax.ShapeDtypeStruct((r * LANES, LANES), F32),
                   jax.ShapeDtypeStruct((r, LANES), jnp.int32), jax.ShapeDtypeStruct((8, LANES), jnp.int32)],
        grid=(1,),
        in_specs=[pl.BlockSpec(memory_space=pltpu.SMEM),
                  pl.BlockSpec((N_EXPERTS, r, LANES), lambda i: (0, 0, 0))],
        out_specs=[full, pl.BlockSpec((r * LANES, LANES), lambda i: (0, 0)), full,
                   pl.BlockSpec((8, LANES), lambda i: (0, 0))],
        compiler_params=_cparams(("arbitrary",)),
        name="route",
    )(router_bias, logits_t)


def _gather_rows(table, idx):
    p = idx.shape[0]
    per_worker = p // SC_WORKERS
    n_chunks = per_worker // SC_GATHER_ROWS
    mesh = plsc.VectorSubcoreMesh(core_axis_name="c", subcore_axis_name="s", num_cores=SC_CORES,
                                  num_subcores=SC_SUBCORES)

    @functools.partial(
        pl.kernel, mesh=mesh,
        out_type=jax.ShapeDtypeStruct((p,) + table.shape[1:], table.dtype),
        scratch_types=[pltpu.VMEM((per_worker,), jnp.int32),
                       pltpu.VMEM((SC_GATHER_ROWS,) + table.shape[1:], table.dtype),
                       pltpu.SemaphoreType.DMA],
        compiler_params=pltpu.CompilerParams(use_tc_tiling_on_sc=True),
        name="gather_rows",
    )
    def gather(table_hbm, idx_hbm, out_hbm, idx_v, rows_v, sem):
        worker = lax.axis_index("s") * SC_CORES + lax.axis_index("c")
        base = worker * per_worker
        pltpu.sync_copy(idx_hbm.at[pl.ds(base, per_worker)], idx_v)

        @pl.loop(0, n_chunks)
        def _(i):
            off = pl.multiple_of(i * SC_GATHER_ROWS, SC_GATHER_ROWS)
            pltpu.async_copy(table_hbm.at[idx_v.at[pl.ds(off, SC_GATHER_ROWS)]], rows_v, sem).wait()
            pltpu.sync_copy(rows_v, out_hbm.at[pl.ds(base + off, SC_GATHER_ROWS)])

    return gather(table, idx)


def _scatter_rows(src, idx, n_out):
    t = idx.shape[0]
    per_worker = t // SC_WORKERS
    n_chunks = per_worker // SC_GATHER_ROWS
    mesh = plsc.VectorSubcoreMesh(core_axis_name="c", subcore_axis_name="s", num_cores=SC_CORES,
                                  num_subcores=SC_SUBCORES)

    @functools.partial(
        pl.kernel, mesh=mesh,
        out_type=jax.ShapeDtypeStruct((n_out,) + src.shape[1:], src.dtype),
        scratch_types=[pltpu.VMEM((n_chunks, SC_GATHER_ROWS), jnp.int32),
                       pltpu.VMEM((SC_GATHER_ROWS,) + src.shape[1:], src.dtype),
                       pltpu.SemaphoreType.DMA],
        compiler_params=pltpu.CompilerParams(use_tc_tiling_on_sc=True),
        name="scatter_rows",
    )
    def scatter(src_hbm, idx_hbm, out_hbm, idx_v, rows_v, sem):
        worker = lax.axis_index("s") * SC_CORES + lax.axis_index("c")
        base = worker * per_worker
        pltpu.sync_copy(idx_hbm.at[worker], idx_v)

        @pl.loop(0, n_chunks)
        def _(i):
            off = pl.multiple_of(i * SC_GATHER_ROWS, SC_GATHER_ROWS)
            pltpu.sync_copy(src_hbm.at[pl.ds(base + off, SC_GATHER_ROWS)], rows_v)
            pltpu.async_copy(rows_v, out_hbm.at[idx_v.at[i]], sem).wait()

    return scatter(src, idx.reshape(SC_WORKERS, n_chunks, SC_GATHER_ROWS))


def _cast_kernel(*refs):
    n = (len(refs) - 1) // 2
    for i in range(n):
        refs[n + 1 + i][0] = refs[i][0, 0].astype(BF16)


def _cast_expert_weights(layer, w1, w3, w2, after):
    ws = (w1, w3, w2)
    e = w1.shape[1]
    in_specs = [pl.BlockSpec((1, 1) + w.shape[2:], lambda j: (layer, j, 0, 0)) for w in ws]
    in_specs.append(pl.BlockSpec(memory_space=pl.ANY))
    return pl.pallas_call(
        _cast_kernel,
        out_shape=[jax.ShapeDtypeStruct(w.shape[1:], BF16) for w in ws],
        grid=(e,),
        in_specs=in_specs,
        out_specs=[pl.BlockSpec((1,) + w.shape[2:], lambda j: (j, 0, 0)) for w in ws],
        compiler_params=_cparams(("arbitrary",)),
        name="cast_expert_weights",
    )(*ws, after)


def _moe_kernel(lo_ref, hi_ref, nt_ref, x_ref, wr_ref, w1l_ref, w1h_ref, w3l_ref, w3h_ref, w2l_ref, w2h_ref, y_ref):
    j = pl.program_id(0)

    @pl.when(j < nt_ref[0])
    def _():
        x = _token_rows(x_ref).astype(BF16)
        wr = wr_ref[...]
        he_lo = (_silu(_dot(x, w1l_ref[0])) * _dot(x, w3l_ref[0]) * wr[:, 0:1]).astype(BF16)
        he_hi = (_silu(_dot(x, w1h_ref[0])) * _dot(x, w3h_ref[0]) * wr[:, 1:2]).astype(BF16)
        y = _dot(he_lo, w2l_ref[0]) + _dot(he_hi, w2h_ref[0])
        _store_token_rows(y_ref, y)

    @pl.when(j >= nt_ref[0])
    def _():
        y_ref[...] = jnp.zeros_like(y_ref)


def _moe_experts(xs, wrow, tile_lo, tile_hi, n_tiles, w1, w3, w2):
    tp = xs.shape[0] // ROW_SEGS
    tm = MOE_TILE
    d, de = w1.shape[1], w1.shape[2]

    def gate_map(j, lo, hi, nt):
        return (jnp.minimum(j, nt[0] - 1), 0)

    def lo_map(j, lo, hi, nt):
        return (lo[jnp.minimum(j, nt[0] - 1)], 0, 0)

    def hi_map(j, lo, hi, nt):
        return (hi[jnp.minimum(j, nt[0] - 1)], 0, 0)

    up = (1, d, de)
    down = (1, de, d)
    return pl.pallas_call(
        _moe_kernel,
        out_shape=jax.ShapeDtypeStruct((tp * ROW_SEGS, LANES), jnp.int32),
        grid_spec=pltpu.PrefetchScalarGridSpec(
            num_scalar_prefetch=3,
            grid=(tp // tm,),
            in_specs=[pl.BlockSpec((tm * ROW_SEGS, LANES), gate_map), pl.BlockSpec((tm, LANES), gate_map),
                      pl.BlockSpec(up, lo_map), pl.BlockSpec(up, hi_map),
                      pl.BlockSpec(up, lo_map), pl.BlockSpec(up, hi_map),
                      pl.BlockSpec(down, lo_map), pl.BlockSpec(down, hi_map)],
            out_specs=pl.BlockSpec((tm * ROW_SEGS, LANES), lambda j, lo, hi, nt: (j, 0)),
        ),
        compiler_params=_cparams(("arbitrary",)),
        name="moe_experts",
    )(tile_lo, tile_hi, n_tiles, xs, wrow, w1, w1, w3, w3, w2, w2)


def _moe(h2, logits, router_bias, w1, w3, w2):
    t = h2.shape[0]
    tm = MOE_TILE
    n_tiles_max = t // tm + N_CLASSES
    tp = n_tiles_max * tm
    lg_t = logits[:, :N_EXPERTS].T.reshape(N_EXPERTS, t // LANES, LANES)
    cls, gates, rank, counts = _route(lg_t, router_bias)
    cls, rank = cls.reshape(t), rank.reshape(t)
    counts = counts[0, :N_CLASSES]
    classes = jnp.arange(N_CLASSES, dtype=jnp.int32)
    onehot = (cls[:, None] == classes[None, :]).astype(jnp.int32)
    tiles_per = (counts + tm - 1) // tm
    tile_end = jnp.cumsum(tiles_per)
    tile_start = tile_end - tiles_per
    dest = jnp.sum(onehot * (tile_start * tm)[None, :], axis=1) + rank
    tiles = jnp.arange(n_tiles_max, dtype=jnp.int32)
    tile_cls = jnp.minimum(jnp.sum((tile_end[None, :] <= tiles[:, None]).astype(jnp.int32), axis=1), N_CLASSES - 1)
    pair = tile_cls % 6
    base = (tile_cls // 6) * PER_GROUP
    pair_onehot = (pair[:, None] == jnp.arange(6, dtype=jnp.int32)[None, :]).astype(jnp.int32)
    tile_lo = base + jnp.sum(pair_onehot * jnp.asarray(PAIR_LO, jnp.int32)[None, :], axis=1)
    tile_hi = base + jnp.sum(pair_onehot * jnp.asarray(PAIR_HI, jnp.int32)[None, :], axis=1)
    wrow = _scatter_rows(gates.reshape(t, 1, LANES), dest, tp).reshape(tp, LANES)
    xs = _scatter_rows(h2, dest, tp).reshape(tp * ROW_SEGS, LANES)
    ys = _moe_experts(xs, wrow, tile_lo, tile_hi, tile_end[-1:], w1, w3, w2)
    return _gather_rows(ys.reshape(tp, ROW_SEGS, LANES), dest), dest


def _final_kernel(s_ref, y_ref, mod_ref, g_ref, o_ref):
    x = s_ref[0] + mod_ref[0, 0, 5:6, :] * _token_rows(y_ref)
    o_ref[0] = _rms(x, g_ref[...])


def _final(s, y, mod, final_g):
    b, n, d = s.shape
    tm = ROW_TILE
    skip = N_CTX // tm
    row_spec = pl.BlockSpec((1, tm, d), lambda i, j: (i, j + skip, 0))
    return pl.pallas_call(
        _final_kernel,
        out_shape=jax.ShapeDtypeStruct((b, n - N_CTX, d), F32),
        grid=(b, (n - N_CTX) // tm),
        in_specs=[row_spec, pl.BlockSpec((tm * ROW_SEGS, LANES), lambda i, j: (i * (n // tm) + j + skip, 0)),
                  pl.BlockSpec((1, 1, 6, d), lambda i, j: (i, 1, 0, 0)),
                  pl.BlockSpec((1, d), lambda i, j: (0, 0))],
        out_specs=pl.BlockSpec((1, tm, d), lambda i, j: (i, j, 0)),
        compiler_params=_cparams(("arbitrary", "arbitrary")),
        name="final_norm",
    )(s, y, mod, final_g.reshape(1, d))


def _rope_tables(seq):
    pos = jnp.arange(seq)
    row_pos, col_pos = pos // GRID_W, pos % GRID_W
    n = A_DH // 2
    inv = ROPE_BASE ** (-jnp.arange(0, n, 2, dtype=F32) / n)
    ang_r = row_pos.astype(F32)[:, None] * inv[None, :]
    ang_c = col_pos.astype(F32)[:, None] * inv[None, :]
    ang = jnp.concatenate([ang_r, ang_c], axis=-1)
    cos = jnp.tile(jnp.cos(ang), (1, 4))
    sin = jnp.tile(jnp.sin(ang), (1, 4))
    sin = jnp.concatenate([-sin[:, :LANES // 2], sin[:, LANES // 2:]], axis=-1)
    cos = jnp.concatenate([jnp.ones((N_CTX, LANES), F32), cos], axis=0)
    sin = jnp.concatenate([jnp.zeros((N_CTX, LANES), F32), sin], axis=0)
    return cos, sin


def _interleave_maps(w):
    d = w.shape[0]
    w = w.reshape(d, A_HEADS, 2, 2, 2, A_DH // 4)
    return w.transpose(0, 1, 4, 2, 3, 5).reshape(d, A_HEADS * LANES)


def _att_weights(w_in):
    a_qk = A_HEADS * 2 * A_DH
    a_v = A_HEADS * A_DV
    b_w = B_HEADS * B_DH
    qa = _interleave_maps(w_in[:, :a_qk]) * (A_DH ** -0.5 * LOG2_E)
    ka = _interleave_maps(w_in[:, a_qk:2 * a_qk])
    va = w_in[:, 2 * a_qk:2 * a_qk + a_v]
    o = 2 * a_qk + a_v
    qb = w_in[:, o:o + b_w] * (B_DH ** -0.5)
    rest = w_in[:, o + b_w:]
    return jnp.concatenate([qa, ka, va, qb, rest], axis=1).astype(BF16)


def _gla_weights(w_in):
    d = w_in.shape[0]
    n_main = 2 * C_HEADS * C_DK + 2 * C_HEADS * C_DV
    pad = jnp.zeros((d, LANES - 2 * C_RANK), w_in.dtype)
    return jnp.concatenate([w_in, pad], axis=1).astype(BF16), n_main


def kernel(x, c, ctx, c_ctx, w_mod, b_mod, norm_g, final_g, att_w_in, att_w_out, att_lambda, att_subln_g, na_bias,
           gla_w_in, gla_w_gate, gla_b_gate, gla_norm_g, gla_w_out, router_w, router_bias, moe_w1, moe_w3, moe_w2):
    b, seq, d = x.shape
    n = N_CTX + seq
    s = (ctx, x)

    rows = b + 1
    rows_pad = -(-rows // 8) * 8
    cc = jnp.concatenate([c, c_ctx[None, :], jnp.zeros((rows_pad - rows, d), F32)], axis=0)
    mod_all = _mod_vectors(cc, w_mod, b_mod)
    mod_x = mod_all[:, :b].reshape(DEPTH, b, 1, 6, d)
    mod_c = jnp.broadcast_to(mod_all[:, b].reshape(DEPTH, 1, 1, 6, d), (DEPTH, b, 1, 6, d))
    mods = jnp.concatenate([mod_c, mod_x], axis=2)

    cos, sin = _rope_tables(seq)
    rw_hi = router_w.astype(BF16)
    rw_lo = (router_w - rw_hi.astype(F32)).astype(BF16)
    zpad = jnp.zeros((d, LANES - 2 * N_EXPERTS), BF16)
    rw = jnp.concatenate([rw_hi, rw_lo, zpad], axis=1)

    y = None
    order_after = router_bias
    for i in range(DEPTH):
        j = i // 2
        modp = mods[i - 1] if i else None
        if i % 2 == 0:
            lam_init = 0.8 - 0.6 * math.exp(-0.3 * i)
            w = _att_weights(att_w_in[j])
            outs = _project(s, y, modp, mods[i], norm_g[i, 0], w, cos, sin,
                            n_rope=2 * A_HEADS * LANES, n_bf16=w.shape[1])
            s = outs[0]
            p = outs[-1]
            oa_x, oa_c = _diff_attention(p, att_lambda[j], att_subln_g[j], lam_init)
            ob_x, ob_c = _neighbourhood_attention(p, _na_bias_table(na_bias[j], seq // GRID_W))
            s, h2, logits = _out_project((oa_x, oa_c, ob_x, ob_c), att_w_out[j].astype(BF16), s, mods[i], norm_g[i, 1],
                                         rw, gla=False)
        else:
            w, n_main = _gla_weights(gla_w_in[j])
            outs = _project(s, y, modp, mods[i], norm_g[i, 0], w, None, None, n_rope=0, n_bf16=n_main)
            s, p, lr = outs
            wg = jnp.zeros((2, LANES, C_HEADS * C_DK), F32)
            wg = wg.at[0, :C_RANK].set(gla_w_gate[j, 0]).at[1, C_RANK:2 * C_RANK].set(gla_w_gate[j, 1])
            o = _gla(p, lr, wg, gla_b_gate[j])
            s, h2, logits = _out_project((o, p), gla_w_out[j].astype(BF16), s, mods[i], norm_g[i, 1],
                                         rw, gla=True, gn=gla_norm_g[j])
        if i == 0:
            expert_w = _cast_expert_weights(0, moe_w1, moe_w3, moe_w2, order_after)
        y, order_after = _moe(h2.reshape(b * n, ROW_SEGS, LANES), logits.reshape(b * n, LANES), router_bias, *expert_w)
        y = y.reshape(b * n * ROW_SEGS, LANES)
        if i + 1 < DEPTH:
            expert_w = _cast_expert_weights(i + 1, moe_w1, moe_w3, moe_w2, order_after)
    return _final(s, y, mods[DEPTH - 1], final_g)
```
